```python
import jax, jax.numpy as jnp
from jax import lax
import numpy as np

D_MODEL = 1024
BATCH = 8
SEQ = 2048
DEPTH = 1
DEC_BATCH = 32
DEC_SEQ = 8
PAST_LEN = 8192
PAGE_SIZE = 128

N_HEADS = 8
HEAD_DIM = 64
ATT_WIDTH = N_HEADS * HEAD_DIM
IDX_HEADS = 4
IDX_DIM = 64
TOP_K_MAX = 256
Q_BLOCK = 128
ROPE_THETA = 10000.0
POOL_WINDOWS = (2, 4, 8, 16)
POOL_GROUPS = 4
POOL_WIDTH = 512
POOL_GW = POOL_WIDTH // POOL_GROUPS
POOL_STATE = 15
N_EXPERTS = 64
TOP_K_EXPERTS = 8
N_GROUPS = 8
TOPK_GROUPS = 4
EXPERT_DIM = 256
SHARED_DIM = 256
ROUTED_SCALE = 2.5
PLE_DIM = 256
LN_EPS = 1e-5
DN_ALPHA = (2 * DEPTH) ** 0.25
DN_BETA = (8 * DEPTH) ** -0.25
IN_SIZES = (ATT_WIDTH, HEAD_DIM, HEAD_DIM, IDX_HEADS * IDX_DIM, IDX_DIM, IDX_HEADS, POOL_WIDTH, D_MODEL, D_MODEL)
IN_WIDTH = sum(IN_SIZES)

kernel_name = 'dsa_pool_hybrid_moe_step'


def layer_norm(x, g, b):
    xf = x.astype(jnp.float32)
    mu = jnp.mean(xf, axis=-1, keepdims=True)
    var = jnp.mean(jnp.square(xf - mu), axis=-1, keepdims=True)
    return ((xf - mu) * lax.rsqrt(var + LN_EPS) * g + b).astype(x.dtype)


def rope(x, pos):
    d = x.shape[-1]
    inv = ROPE_THETA ** (-jnp.arange(0, d, 2, dtype=jnp.float32) / d)
    ang = pos.astype(jnp.float32)[:, None] * inv[None, :]
    cos = jnp.cos(ang)[:, None, :]
    sin = jnp.sin(ang)[:, None, :]
    xf = x.astype(jnp.float32)
    x1, x2 = xf[..., : d // 2], xf[..., d // 2:]
    return jnp.concatenate([x1 * cos - x2 * sin, x2 * cos + x1 * sin], axis=-1).astype(x.dtype)


def project_in(x, pos, w_in):
    B, T, _ = x.shape
    offs = np.cumsum(IN_SIZES)[:-1].tolist()
    q, k, v, qi, ki, wi, u, ga, gb = jnp.split(x @ w_in, offs, axis=-1)
    q = rope(q.reshape(B, T, N_HEADS, HEAD_DIM), pos)
    k = rope(k[:, :, None, :], pos)[:, :, 0]
    qi = rope(qi.reshape(B, T, IDX_HEADS, IDX_DIM), pos)
    ki = rope(ki[:, :, None, :], pos)[:, :, 0]
    wi = wi * IDX_HEADS ** -0.5
    return q, k, v, qi, ki, wi, u, ga, gb


def sparse_attention(q, qi, wi, q_pos, k, v, ki, top_k):
    B, T = q.shape[:2]
    L = k.shape[1]
    k_pos = jnp.arange(L, dtype=jnp.int32)
    dots = jnp.einsum('bthd,bsd->bths', qi.astype(jnp.float32), ki.astype(jnp.float32)) * IDX_DIM ** -0.5
    score = jnp.einsum('bth,bths->bts', wi.astype(jnp.float32), jax.nn.relu(dots))
    admissible = k_pos[None, :] <= q_pos[:, None]
    score = jnp.where(admissible[None], score, -jnp.inf)
    _, idx = lax.top_k(score, top_k)
    valid = idx <= q_pos[None, :, None]
    gather = jax.vmap(lambda rows, ids: rows[ids])
    k_sel = gather(k, idx).astype(jnp.float32)
    v_sel = gather(v, idx).astype(jnp.float32)
    logits = jnp.einsum('bthd,btkd->bthk', q.astype(jnp.float32), k_sel) * HEAD_DIM ** -0.5
    logits = jnp.where(valid[:, :, None, :], logits, -jnp.inf)
    p = jax.nn.softmax(logits, axis=-1)
    o = jnp.einsum('bthk,btkd->bthd', p, v_sel)
    return o.reshape(B, T, ATT_WIDTH).astype(q.dtype)


def prompt_attention(q, qi, wi, k, v, ki):
    B, T = q.shape[:2]
    nb = T // Q_BLOCK
    top_k = min(TOP_K_MAX, T // 4)

    def to_blocks(a):
        return jnp.moveaxis(a.reshape((B, nb, Q_BLOCK) + a.shape[2:]), 1, 0)

    def block(args):
        qb, qib, wib, pb = args
        return sparse_attention(qb, qib, wib, pb, k, v, ki, top_k)

    pos = jnp.arange(T, dtype=jnp.int32).reshape(nb, Q_BLOCK)
    out = lax.map(block, (to_blocks(q), to_blocks(qi), to_blocks(wi), pos))
    return jnp.moveaxis(out, 0, 1).reshape(B, T, ATT_WIDTH)


def pool_mix(u_ext, pos, w_grp, scale):
    B = u_ext.shape[0]
    P = POOL_STATE
    T = u_ext.shape[1] - P
    cs = jnp.cumsum(u_ext.astype(jnp.float32), axis=1)
    cs = jnp.concatenate([jnp.zeros_like(cs[:, :1]), cs], axis=1)
    u_new = u_ext[:, P:].astype(jnp.float32)
    outs = []
    for g, w in enumerate(POOL_WINDOWS):
        sl = slice(g * POOL_GW, (g + 1) * POOL_GW)
        win_sum = cs[:, P + 1:P + T + 1, sl] - cs[:, P + 1 - w:P + T + 1 - w, sl]
        count = jnp.minimum(pos + 1, w).astype(jnp.float32)[None, :, None]
        outs.append(win_sum / count - u_new[:, :, sl])
    r = jnp.stack(outs, axis=2)
    r = jnp.einsum('btgc,gcd->btgd', r, w_grp.astype(jnp.float32)).reshape(B, T, POOL_WIDTH)
    return (r * scale).astype(u_ext.dtype)


def merge_out(attn, pool, ga, gb, w_att_out, w_pool_out, w_out):
    return (jax.nn.sigmoid(ga) * (attn @ w_att_out) + jax.nn.sigmoid(gb) * (pool @ w_pool_out)) @ w_out


def swiglu(x, w13, w2):
    a, b = jnp.split(x @ w13, 2, axis=-1)
    return (jax.nn.silu(a) * b) @ w2


def moe(h, w_router, router_bias, w_exp13, w_exp2, w_sh13, w_sh2):
    N = h.shape[0]
    s = jax.nn.sigmoid((h @ w_router).astype(jnp.float32))
    sb = s + router_bias.astype(jnp.float32)
    grp = sb.reshape(N, N_GROUPS, N_EXPERTS // N_GROUPS)
    grp_score = jnp.sum(lax.top_k(grp, 2)[0], axis=-1)
    _, gidx = lax.top_k(grp_score, TOPK_GROUPS)
    gmask = jnp.sum(jax.nn.one_hot(gidx, N_GROUPS, dtype=jnp.float32), axis=1) > 0
    emask = jnp.repeat(gmask, N_EXPERTS // N_GROUPS, axis=1)
    _, eidx = lax.top_k(jnp.where(emask, sb, -jnp.inf), TOP_K_EXPERTS)
    gate = jnp.take_along_axis(s, eidx, axis=1)
    gate = gate / jnp.sum(gate, axis=-1, keepdims=True) * ROUTED_SCALE
    combine = jnp.einsum('nk,nke->ne', gate, jax.nn.one_hot(eidx, N_EXPERTS, dtype=jnp.float32)).astype(h.dtype)
    y = swiglu(h, w_sh13, w_sh2)
    for e in range(N_EXPERTS):
        y = y + combine[:, e:e + 1] * swiglu(h, w_exp13[e], w_exp2[e])
    return y


def setup_inputs(seed: int = 0) -> dict:
    key = jax.random.key(seed)
    keys = jax.random.split(key, 32)

    def nrm(i, shape, scale=1.0):
        return jax.random.normal(keys[i], shape, jnp.float32) * scale

    n_pages = PAST_LEN // PAGE_SIZE
    n_used = DEC_BATCH * n_pages
    n_phys = n_used + max(1, n_used // 4)
    page_table = jax.random.permutation(keys[0], n_phys)[:n_used].reshape(DEC_BATCH, n_pages).astype(jnp.int32)
    return {
        'x_prompt': nrm(1, (BATCH, SEQ, D_MODEL)),
        'x_sample': nrm(2, (DEC_BATCH, DEC_SEQ, D_MODEL)),
        'cache_k': nrm(3, (DEPTH, n_phys, PAGE_SIZE, HEAD_DIM)),
        'cache_v': nrm(4, (DEPTH, n_phys, PAGE_SIZE, HEAD_DIM)),
        'cache_kidx': nrm(5, (DEPTH, n_phys, PAGE_SIZE, IDX_DIM)),
        'state_pool': nrm(6, (DEPTH, DEC_BATCH, POOL_STATE, POOL_WIDTH)),
        'page_table': page_table,
        'p_prompt': nrm(7, (DEPTH, BATCH, SEQ, PLE_DIM)),
        'p_sample': nrm(8, (DEPTH, DEC_BATCH, DEC_SEQ, PLE_DIM)),
        'w_in': nrm(9, (DEPTH, D_MODEL, IN_WIDTH), D_MODEL ** -0.5),
        'w_att_out': nrm(10, (DEPTH, ATT_WIDTH, D_MODEL), ATT_WIDTH ** -0.5),
        'w_pool_grp': nrm(11, (DEPTH, POOL_GROUPS, POOL_GW, POOL_GW), POOL_GW ** -0.5),
        'pool_scale': 1.0 + nrm(12, (DEPTH, POOL_WIDTH), 0.05),
        'w_pool_out': nrm(13, (DEPTH, POOL_WIDTH, D_MODEL), POOL_WIDTH ** -0.5),
        'w_out': nrm(14, (DEPTH, D_MODEL, D_MODEL), D_MODEL ** -0.5 * DN_BETA),
        'ln1_g': 1.0 + nrm(15, (DEPTH, D_MODEL), 0.05),
        'ln1_b': nrm(16, (DEPTH, D_MODEL), 0.02),
        'w_router': nrm(17, (DEPTH, D_MODEL, N_EXPERTS), D_MODEL ** -0.5),
        'router_bias': nrm(18, (DEPTH, N_EXPERTS), 0.01),
        'w_exp13': nrm(19, (DEPTH, N_EXPERTS, D_MODEL, 2 * EXPERT_DIM), D_MODEL ** -0.5),
        'w_exp2': nrm(20, (DEPTH, N_EXPERTS, EXPERT_DIM, D_MODEL), EXPERT_DIM ** -0.5 * DN_BETA),
        'w_sh13': nrm(21, (DEPTH, D_MODEL, 2 * SHARED_DIM), D_MODEL ** -0.5),
        'w_sh2': nrm(22, (DEPTH, SHARED_DIM, D_MODEL), SHARED_DIM ** -0.5 * DN_BETA),
        'ln2_g': 1.0 + nrm(23, (DEPTH, D_MODEL), 0.05),
        'ln2_b': nrm(24, (DEPTH, D_MODEL), 0.02),
        'w_ple_in': nrm(25, (DEPTH, PLE_DIM, D_MODEL), PLE_DIM ** -0.5),
        'w_ple_gate': nrm(26, (DEPTH, D_MODEL, D_MODEL), D_MODEL ** -0.5),
    }


def reference(x_prompt, x_sample, cache_k, cache_v, cache_kidx, state_pool, page_table, p_prompt, p_sample,
              w_in, w_att_out, w_pool_grp, pool_scale, w_pool_out, w_out, ln1_g, ln1_b,
              w_router, router_bias, w_exp13, w_exp2, w_sh13, w_sh2, ln2_g, ln2_b, w_ple_in, w_ple_gate):
    B, S, D = x_prompt.shape
    DB, T, _ = x_sample.shape
    past = page_table.shape[1] * cache_k.shape[2]
    pos_p = jnp.arange(S, dtype=jnp.int32)
    pos_s = past + jnp.arange(T, dtype=jnp.int32)
    top_k_s = min(TOP_K_MAX, (past + T) // 4)
    xp, xs = x_prompt, x_sample
    kp_l, vp_l, kip_l, pp_l, ks_l, vs_l, kis_l, ps_l = [], [], [], [], [], [], [], []
    for i in range(DEPTH):
        q, k, v, qi, ki, wi, u, ga, gb = project_in(xp, pos_p, w_in[i])
        attn_p = prompt_attention(q, qi, wi, k, v, ki)
        u_ext = jnp.concatenate([jnp.zeros((B, POOL_STATE, POOL_WIDTH), u.dtype), u], axis=1)
        pool_p = pool_mix(u_ext, pos_p, w_pool_grp[i], pool_scale[i])
        mix_p = merge_out(attn_p, pool_p, ga, gb, w_att_out[i], w_pool_out[i], w_out[i])
        kp_l.append(k); vp_l.append(v); kip_l.append(ki); pp_l.append(u_ext[:, -POOL_STATE:])
        qs, ks, vs, qis, kis, wis, us, gas, gbs = project_in(xs, pos_s, w_in[i])
        k_all = jnp.concatenate([cache_k[i][page_table].reshape(DB, past, HEAD_DIM), ks], axis=1)
        v_all = jnp.concatenate([cache_v[i][page_table].reshape(DB, past, HEAD_DIM), vs], axis=1)
        ki_all = jnp.concatenate([cache_kidx[i][page_table].reshape(DB, past, IDX_DIM), kis], axis=1)
        attn_s = sparse_attention(qs, qis, wis, pos_s, k_all, v_all, ki_all, top_k_s)
        us_ext = jnp.concatenate([state_pool[i].astype(us.dtype), us], axis=1)
        pool_s = pool_mix(us_ext, pos_s, w_pool_grp[i], pool_scale[i])
        mix_s = merge_out(attn_s, pool_s, gas, gbs, w_att_out[i], w_pool_out[i], w_out[i])
        ks_l.append(ks); vs_l.append(vs); kis_l.append(kis); ps_l.append(us_ext[:, -POOL_STATE:])
        h = jnp.concatenate([(DN_ALPHA * xp + mix_p).reshape(B * S, D), (DN_ALPHA * xs + mix_s).reshape(DB * T, D)], axis=0)
        h = layer_norm(h, ln1_g[i], ln1_b[i])
        h = layer_norm(DN_ALPHA * h + moe(h, w_router[i], router_bias[i], w_exp13[i], w_exp2[i], w_sh13[i], w_sh2[i]), ln2_g[i], ln2_b[i])
        pe = jnp.concatenate([p_prompt[i].reshape(B * S, PLE_DIM), p_sample[i].reshape(DB * T, PLE_DIM)], axis=0).astype(h.dtype)
        h = h + jax.nn.sigmoid(h @ w_ple_gate[i]) * (pe @ w_ple_in[i])
        xp = h[:B * S].reshape(B, S, D)
        xs = h[B * S:].reshape(DB, T, D)
    return (xp, xs, jnp.stack(kp_l), jnp.stack(vp_l), jnp.stack(kip_l), jnp.stack(pp_l),
            jnp.stack(ks_l), jnp.stack(vs_l), jnp.stack(kis_l), jnp.stack(ps_l))
```

```python
import functools

import numpy as np
import jax
import jax.numpy as jnp
from jax import lax
from jax.experimental import pallas as pl
from jax.experimental.pallas import tpu as pltpu

F32 = jnp.float32
BF16 = jnp.bfloat16
I32 = jnp.int32

D_MODEL = 1024
N_HEADS = 8
HEAD_DIM = 64
ATT_WIDTH = N_HEADS * HEAD_DIM
IDX_HEADS = 4
IDX_DIM = 64
TOP_K_MAX = 256
Q_BLOCK = 128
ROPE_THETA = 10000.0
POOL_WINDOWS = (2, 4, 8, 16)
POOL_GROUPS = 4
POOL_WIDTH = 512
POOL_GW = POOL_WIDTH // POOL_GROUPS
POOL_STATE = 15
N_EXPERTS = 64
TOP_K_EXPERTS = 8
N_GROUPS = 8
GROUP_SIZE = N_EXPERTS // N_GROUPS
TOPK_GROUPS = 4
EXPERT_DIM = 256
SHARED_DIM = 256
ROUTED_SCALE = 2.5
PLE_DIM = 256
LN_EPS = 1e-5
IN_SIZES = (ATT_WIDTH, HEAD_DIM, HEAD_DIM, IDX_HEADS * IDX_DIM, IDX_DIM, IDX_HEADS, POOL_WIDTH, D_MODEL, D_MODEL)

LANES = 128
INT_MIN = -2147483648
NEG_BIG = -1e30
VMEM_LIMIT = 56 * 1024 * 1024

C_Q, C_QR = 0, 512
C_QI, C_QIR = 1024, 1280
C_KK, C_KKR = 1536, 1664
C_VW = 1792
C_U = 1920
C_END = 2432

NT_DIMS = (((1,), (1,)), ((), ()))


def _params(sem):
    return pltpu.CompilerParams(dimension_semantics=sem, vmem_limit_bytes=VMEM_LIMIT)


def _layer_norm(x, g, b):
    mu = jnp.mean(x, axis=-1, keepdims=True)
    xc = x - mu
    var = jnp.mean(xc * xc, axis=-1, keepdims=True)
    return xc * lax.rsqrt(var + LN_EPS) * g + b


def _proj_kernel(x_ref, w_ref, cs_ref, q_ref, qi_ref, k_ref, v_ref, ki_ref, kb_ref, vb_ref, kib_ref,
                 wi_ref, u_ref):
    xb = x_ref[...].astype(BF16)
    cos = cs_ref[:, 0:LANES]
    sin = cs_ref[:, LANES:2 * LANES]

    def mm(c0, n):
        return jnp.dot(xb, w_ref[:, c0:c0 + n], preferred_element_type=F32)

    def rope(c0, c0r, n):
        reps = n // LANES
        cosn = jnp.concatenate([cos] * reps, axis=1) if reps > 1 else cos
        sinn = jnp.concatenate([sin] * reps, axis=1) if reps > 1 else sin
        return mm(c0, n) * cosn + mm(c0r, n) * sinn

    q_ref[...] = (rope(C_Q, C_QR, ATT_WIDTH) * 0.125).astype(BF16)
    qi_ref[...] = (rope(C_QI, C_QIR, IDX_HEADS * IDX_DIM) * 0.125).astype(BF16)
    kk = rope(C_KK, C_KKR, LANES)
    k = kk[:, 0:HEAD_DIM]
    ki = kk[:, HEAD_DIM:2 * HEAD_DIM]
    k_ref[...] = k
    ki_ref[...] = ki
    kb_ref[...] = k.astype(BF16)
    kib_ref[...] = ki.astype(BF16)
    vw = mm(C_VW, LANES)
    v = vw[:, 0:HEAD_DIM]
    v_ref[...] = v
    vb_ref[...] = v.astype(BF16)
    wi_ref[...] = vw[:, HEAD_DIM:HEAD_DIM + IDX_HEADS] * (IDX_HEADS ** -0.5)
    u_ref[...] = mm(C_U, POOL_WIDTH)


def _proj(x, w_big, cs, tm):
    n = x.shape[0]
    n_cs_blocks = cs.shape[0] // tm
    row = lambda w: pl.BlockSpec((tm, w), lambda i: (i, 0))
    out_shape = (
        jax.ShapeDtypeStruct((n, ATT_WIDTH), BF16),
        jax.ShapeDtypeStruct((n, IDX_HEADS * IDX_DIM), BF16),
        jax.ShapeDtypeStruct((n, HEAD_DIM), F32),
        jax.ShapeDtypeStruct((n, HEAD_DIM), F32),
        jax.ShapeDtypeStruct((n, IDX_DIM), F32),
        jax.ShapeDtypeStruct((n, HEAD_DIM), BF16),
        jax.ShapeDtypeStruct((n, HEAD_DIM), BF16),
        jax.ShapeDtypeStruct((n, IDX_DIM), BF16),
        jax.ShapeDtypeStruct((n, IDX_HEADS), F32),
        jax.ShapeDtypeStruct((n, POOL_WIDTH), F32),
    )
    return pl.pallas_call(
        _proj_kernel,
        grid=(n // tm,),
        in_specs=[
            row(D_MODEL),
            pl.BlockSpec((D_MODEL, C_END), lambda i: (0, 0)),
            pl.BlockSpec((tm, 2 * LANES), lambda i: (i % n_cs_blocks, 0)),
        ],
        out_specs=(row(ATT_WIDTH), row(IDX_HEADS * IDX_DIM), row(HEAD_DIM), row(HEAD_DIM), row(IDX_DIM),
                   row(HEAD_DIM), row(HEAD_DIM), row(IDX_DIM), row(IDX_HEADS), row(POOL_WIDTH)),
        out_shape=out_shape,
        compiler_params=_params(("parallel",)),
        name="proj",
    )(x, w_big, cs)


def _float_keys(score):
    bits = pltpu.bitcast(score, I32)
    return jnp.where(bits < 0, INT_MIN - bits, bits)


def _count(mask):
    return jnp.sum(mask.astype(F32), axis=1, keepdims=True)


def _topk_bias(key_ref, j_ref, adm, lc, k):
    rows = key_ref.shape[0]
    kf = float(k)

    def value_step(i, t_u):
        cand_u = t_u | jnp.left_shift(jnp.int32(1), 31 - i)
        cand_s = cand_u ^ INT_MIN
        cnt = _count(key_ref[:, 0:lc] >= cand_s)
        return jnp.where(cnt >= kf, cand_u, t_u)

    t_u = lax.fori_loop(0, 32, value_step, jnp.zeros((rows, 1), I32))
    thr = t_u ^ INT_MIN
    keys = key_ref[:, 0:lc]
    cnt_gt = _count(keys > thr)
    cnt_eq = _count(keys == thr)
    need = kf - cnt_gt
    cut_needed = jnp.logical_and(cnt_gt + cnt_eq > kf, thr != INT_MIN)
    any_cut = jnp.max(cut_needed.astype(F32)) > 0.0
    idx = lax.broadcasted_iota(I32, (rows, lc), 1)
    nbits = int(np.ceil(np.log2(lc)))

    j_ref[...] = jnp.full((rows, 1), lc, I32)

    @pl.when(any_cut)
    def _():
        def index_step(i, j):
            cand = j | jnp.left_shift(jnp.int32(1), nbits - 1 - i)
            c = _count(jnp.logical_and(key_ref[:, 0:lc] == thr, idx < cand))
            return jnp.where(c < need, cand, j)

        j_ref[...] = lax.fori_loop(0, nbits, index_step, jnp.zeros((rows, 1), I32))

    sel = jnp.logical_or(keys > thr, jnp.logical_and(keys == thr, idx <= j_ref[...]))
    return jnp.where(jnp.logical_and(sel, adm), 0.0, NEG_BIG)


def _attn_prompt_block(q_ref, qi_ref, wi_ref, kb_ref, vb_ref, kib_ref, o_ref, key_ref, bias_ref, j_ref, q0, lc,
                       top_k):
    tq = q_ref.shape[0]
    kib = kib_ref[0:lc, :]
    score = jnp.zeros((tq, lc), F32)
    for h in range(IDX_HEADS):
        d = lax.dot_general(qi_ref[:, h * IDX_DIM:(h + 1) * IDX_DIM], kib, NT_DIMS, preferred_element_type=F32)
        score = score + wi_ref[:, h:h + 1] * jnp.maximum(d, 0.0)
    idx = lax.broadcasted_iota(I32, (tq, lc), 1)
    qpos = q0 + lax.broadcasted_iota(I32, (tq, lc), 0)
    adm = idx <= qpos
    key_ref[:, 0:lc] = jnp.where(adm, _float_keys(score), INT_MIN)
    bias_ref[:, 0:lc] = _topk_bias(key_ref, j_ref, adm, lc, top_k)

    kb = kb_ref[0:lc, :]
    vb = vb_ref[0:lc, :]
    for h in range(N_HEADS):
        sl = slice(h * HEAD_DIM, (h + 1) * HEAD_DIM)
        lg = lax.dot_general(q_ref[:, sl], kb, NT_DIMS, preferred_element_type=F32) + bias_ref[:, 0:lc]
        m = jnp.max(lg, axis=1, keepdims=True)
        p = jnp.exp(lg - m)
        l = jnp.sum(p, axis=1, keepdims=True)
        o = jnp.dot(p.astype(BF16), vb, preferred_element_type=F32) / l
        o_ref[:, sl] = o.astype(BF16)


def _attn_prompt_kernel(q_ref, qi_ref, wi_ref, kb_ref, vb_ref, kib_ref, o_ref, key_ref, bias_ref, j_ref, *,
                        seq, n_classes, top_k):
    j = pl.program_id(1)
    blocks_per_class = seq // Q_BLOCK // n_classes
    cls = j // blocks_per_class
    for c in range(n_classes):
        @pl.when(cls == c)
        def _(c=c):
            _attn_prompt_block(q_ref, qi_ref, wi_ref, kb_ref, vb_ref, kib_ref, o_ref, key_ref, bias_ref, j_ref,
                               j * Q_BLOCK, (c + 1) * (seq // n_classes), top_k)


def _attn_prompt(q, qi, wi, kb, vb, kib, batch, seq):
    nb = seq // Q_BLOCK
    top_k = min(TOP_K_MAX, seq // 4)
    qrow = lambda w: pl.BlockSpec((Q_BLOCK, w), lambda b, j: (b * nb + j, 0))
    kv = pl.BlockSpec((None, seq, HEAD_DIM), lambda b, j: (b, 0, 0))
    kern = functools.partial(_attn_prompt_kernel, seq=seq, n_classes=4, top_k=top_k)
    return pl.pallas_call(
        kern,
        grid=(batch, nb),
        in_specs=[qrow(ATT_WIDTH), qrow(IDX_HEADS * IDX_DIM), qrow(IDX_HEADS), kv, kv, kv],
        out_specs=qrow(ATT_WIDTH),
        out_shape=jax.ShapeDtypeStruct((batch * seq, ATT_WIDTH), BF16),
        scratch_shapes=[pltpu.VMEM((Q_BLOCK, seq), I32), pltpu.VMEM((Q_BLOCK, seq), F32),
                        pltpu.VMEM((Q_BLOCK, 1), I32)],
        compiler_params=_params(("parallel", "arbitrary")),
        name="attn_prompt",
    )(q, qi, wi, kb.reshape(batch, seq, HEAD_DIM), vb.reshape(batch, seq, HEAD_DIM),
      kib.reshape(batch, seq, IDX_DIM))


def _attn_sample_kernel(pt_ref, q_ref, qi_ref, wi_ref, kn_ref, vn_ref, kin_ref, ck_ref, cv_ref, cki_ref, o_ref,
                        k_scr, v_scr, sc_scr, key_scr, j_scr, *, n_pages, page, t_new, top_k):
    p = pl.program_id(1)
    past = n_pages * page
    lc = past + page

    def head_scores(ki_page):
        d = lax.dot_general(qi_ref[...], ki_page.astype(BF16), NT_DIMS, preferred_element_type=F32)
        r = wi_ref[...] * jnp.maximum(d, 0.0)
        s = r[0:t_new]
        for h in range(1, IDX_HEADS):
            s = s + r[h * t_new:(h + 1) * t_new]
        return s

    off = pl.multiple_of(p * page, page)
    k_scr[pl.ds(off, page), :] = ck_ref[...].astype(BF16)
    v_scr[pl.ds(off, page), :] = cv_ref[...].astype(BF16)
    sc_scr[:, pl.ds(off, page)] = head_scores(cki_ref[...])

    @pl.when(p == n_pages - 1)
    def _():
        pad = jnp.zeros((page - t_new, HEAD_DIM), F32)
        k_scr[past:lc, :] = jnp.concatenate([kn_ref[...], pad], axis=0).astype(BF16)
        v_scr[past:lc, :] = jnp.concatenate([vn_ref[...], pad], axis=0).astype(BF16)
        sc_scr[:, past:lc] = head_scores(jnp.concatenate([kin_ref[...], pad], axis=0))

        idx = lax.broadcasted_iota(I32, (t_new, lc), 1)
        trow = lax.broadcasted_iota(I32, (t_new, lc), 0)
        adm = idx - past <= trow
        key_scr[...] = jnp.where(adm, _float_keys(sc_scr[...]), INT_MIN)
        bias = _topk_bias(key_scr, j_scr, adm, lc, top_k)
        bias_all = jnp.concatenate([bias] * N_HEADS, axis=0)
        lg = lax.dot_general(q_ref[...], k_scr[...], NT_DIMS, preferred_element_type=F32) + bias_all
        m = jnp.max(lg, axis=1, keepdims=True)
        pr = jnp.exp(lg - m)
        l = jnp.sum(pr, axis=1, keepdims=True)
        o_ref[...] = jnp.dot(pr.astype(BF16), v_scr[...], preferred_element_type=F32) / l


def _attn_sample(page_table, q_hq, qi_hq, wi_hq, k_new, v_new, ki_new, cache_k, cache_v, cache_ki):
    db, n_pages = page_table.shape
    page = cache_k.shape[1]
    t_new = k_new.shape[1]
    past = n_pages * page
    lc = past + page
    top_k = min(TOP_K_MAX, (past + t_new) // 4)
    per_b = lambda r, w: pl.BlockSpec((None, r, w), lambda b, p, pt: (b, 0, 0))
    paged = pl.BlockSpec((None, page, HEAD_DIM), lambda b, p, pt: (pt[b * n_pages + p], 0, 0))
    kern = functools.partial(_attn_sample_kernel, n_pages=n_pages, page=page, t_new=t_new, top_k=top_k)
    grid_spec = pltpu.PrefetchScalarGridSpec(
        num_scalar_prefetch=1,
        grid=(db, n_pages),
        in_specs=[per_b(N_HEADS * t_new, HEAD_DIM), per_b(IDX_HEADS * t_new, IDX_DIM), per_b(IDX_HEADS * t_new, 1),
                  per_b(t_new, HEAD_DIM), per_b(t_new, HEAD_DIM), per_b(t_new, IDX_DIM),
                  paged, paged, paged],
        out_specs=per_b(N_HEADS * t_new, HEAD_DIM),
        scratch_shapes=[pltpu.VMEM((lc, HEAD_DIM), BF16), pltpu.VMEM((lc, HEAD_DIM), BF16),
                        pltpu.VMEM((t_new, lc), F32), pltpu.VMEM((t_new, lc), I32), pltpu.VMEM((t_new, 1), I32)],
    )
    return pl.pallas_call(
        kern,
        grid_spec=grid_spec,
        out_shape=jax.ShapeDtypeStruct((db, N_HEADS * t_new, HEAD_DIM), F32),
        compiler_params=_params(("parallel", "arbitrary")),
        name="attn_sample",
    )(page_table.reshape(-1), q_hq, qi_hq, wi_hq, k_new, v_new, ki_new, cache_k, cache_v, cache_ki)


PREV_ROWS = 16


def _pool_kernel(prev_ref, u_ref, wg_ref, sc_ref, o_ref, ext_ref, *, pos0):
    t_len = u_ref.shape[0]
    ext_ref[0:PREV_ROWS, :] = prev_ref[...]
    ext_ref[PREV_ROWS:PREV_ROWS + t_len, :] = u_ref[...]
    pos = pos0 + lax.broadcasted_iota(I32, (t_len, 1), 0)
    for g, w in enumerate(POOL_WINDOWS):
        sl = slice(g * POOL_GW, (g + 1) * POOL_GW)
        u_new = ext_ref[PREV_ROWS:PREV_ROWS + t_len, sl]
        win = u_new
        for back in range(1, w):
            win = win + ext_ref[PREV_ROWS - back:PREV_ROWS - back + t_len, sl]
        count = jnp.minimum(pos + 1, w).astype(F32)
        r = win / count - u_new
        mixed = jnp.dot(r.astype(BF16), wg_ref[g], preferred_element_type=F32) * sc_ref[:, sl]
        o_ref[:, sl] = mixed.astype(BF16)


def _pool(prev, u, w_grp, scale, pos0):
    nb, t_len, _ = u.shape
    return pl.pallas_call(
        functools.partial(_pool_kernel, pos0=pos0),
        grid=(nb,),
        in_specs=[pl.BlockSpec((None, PREV_ROWS, POOL_WIDTH), lambda b: (b, 0, 0)),
                  pl.BlockSpec((None, t_len, POOL_WIDTH), lambda b: (b, 0, 0)),
                  pl.BlockSpec((POOL_GROUPS, POOL_GW, POOL_GW), lambda b: (0, 0, 0)),
                  pl.BlockSpec((1, POOL_WIDTH), lambda b: (0, 0))],
        out_specs=pl.BlockSpec((None, t_len, POOL_WIDTH), lambda b: (b, 0, 0)),
        out_shape=jax.ShapeDtypeStruct((nb, t_len, POOL_WIDTH), BF16),
        scratch_shapes=[pltpu.VMEM((PREV_ROWS + t_len, POOL_WIDTH), F32)],
        compiler_params=_params(("parallel",)),
        name="pool",
    )(prev, u, w_grp, scale)


def _merge_kernel(x_ref, a_ref, p_ref, wga_ref, wgb_ref, wao_ref, wpo_ref, wo_ref, g_ref, b_ref, h_ref, *, alpha):
    x = x_ref[...]
    xb = x.astype(BF16)
    ga = jnp.dot(xb, wga_ref[...], preferred_element_type=F32)
    gb = jnp.dot(xb, wgb_ref[...], preferred_element_type=F32)
    ya = jnp.dot(a_ref[...], wao_ref[...], preferred_element_type=F32)
    yp = jnp.dot(p_ref[...], wpo_ref[...], preferred_element_type=F32)
    mix = jax.nn.sigmoid(ga) * ya + jax.nn.sigmoid(gb) * yp
    out = jnp.dot(mix.astype(BF16), wo_ref[...], preferred_element_type=F32)
    h_ref[...] = _layer_norm(alpha * x + out, g_ref[...], b_ref[...])


def _merge(x, attn, pool, wga, wgb, wao, wpo, wo, g, b, tm, alpha):
    n = x.shape[0]
    row = lambda w: pl.BlockSpec((tm, w), lambda i: (i, 0))
    full = lambda r, c: pl.BlockSpec((r, c), lambda i: (0, 0))
    return pl.pallas_call(
        functools.partial(_merge_kernel, alpha=alpha),
        grid=(n // tm,),
        in_specs=[row(D_MODEL), row(ATT_WIDTH), row(POOL_WIDTH), full(D_MODEL, D_MODEL), full(D_MODEL, D_MODEL),
                  full(ATT_WIDTH, D_MODEL), full(POOL_WIDTH, D_MODEL), full(D_MODEL, D_MODEL),
                  full(1, D_MODEL), full(1, D_MODEL)],
        out_specs=row(D_MODEL),
        out_shape=jax.ShapeDtypeStruct((n, D_MODEL), F32),
        compiler_params=_params(("parallel",)),
        name="merge",
    )(x, attn, pool, wga, wgb, wao, wpo, wo, g, b)


def _router_kernel(h_ref, wr_ref, bias_ref, c_ref):
    tm = h_ref.shape[0]
    logits = lax.dot_general(wr_ref[...], h_ref[...].astype(BF16), NT_DIMS, preferred_element_type=F32)
    s = jax.nn.sigmoid(logits)
    sb = s + bias_ref[...]
    neg_inf = -jnp.inf

    rows = []
    for g in range(N_GROUPS):
        blk = sb[g * GROUP_SIZE:(g + 1) * GROUP_SIZE, :]
        m1 = jnp.max(blk, axis=0, keepdims=True)
        is_m1 = blk == m1
        n_m1 = jnp.sum(is_m1.astype(F32), axis=0, keepdims=True)
        m2 = jnp.max(jnp.where(is_m1, neg_inf, blk), axis=0, keepdims=True)
        rows.append(m1 + jnp.where(n_m1 >= 2.0, m1, m2))
    gs = jnp.concatenate(rows, axis=0)

    gi = lax.broadcasted_iota(I32, (N_GROUPS, tm), 0)
    rank = jnp.zeros((N_GROUPS, tm), F32)
    for g in range(N_GROUPS):
        row = gs[g:g + 1, :]
        beats = jnp.logical_or(row > gs, jnp.logical_and(row == gs, g < gi))
        rank = rank + beats.astype(F32)
    gkeep = rank < float(TOPK_GROUPS)
    emask = jnp.concatenate(
        [jnp.broadcast_to(gkeep[g:g + 1, :], (GROUP_SIZE, tm)) for g in range(N_GROUPS)], axis=0)

    ei = lax.broadcasted_iota(I32, (N_EXPERTS, tm), 0)
    x = jnp.where(emask, sb, neg_inf)
    sel = jnp.zeros((N_EXPERTS, tm), jnp.bool_)
    for _ in range(TOP_K_EXPERTS):
        m = jnp.max(x, axis=0, keepdims=True)
        first = jnp.min(jnp.where(x == m, ei, N_EXPERTS), axis=0, keepdims=True)
        pick = ei == first
        sel = jnp.logical_or(sel, pick)
        x = jnp.where(pick, neg_inf, x)

    gate = jnp.where(sel, s, 0.0)
    comb = gate / jnp.sum(gate, axis=0, keepdims=True) * ROUTED_SCALE
    comb = jnp.concatenate([comb, jnp.zeros((LANES - N_EXPERTS, tm), F32)], axis=0)
    c_ref[...] = comb.T


def _router(h, wr_t, bias_col, tm):
    n = h.shape[0]
    return pl.pallas_call(
        _router_kernel,
        grid=(n // tm,),
        in_specs=[pl.BlockSpec((tm, D_MODEL), lambda i: (i, 0)),
                  pl.BlockSpec((N_EXPERTS, D_MODEL), lambda i: (0, 0)),
                  pl.BlockSpec((N_EXPERTS, 1), lambda i: (0, 0))],
        out_specs=pl.BlockSpec((tm, LANES), lambda i: (i, 0)),
        out_shape=jax.ShapeDtypeStruct((n, LANES), F32),
        compiler_params=_params(("parallel",)),
        name="router",
    )(h, wr_t, bias_col)


def _swiglu(xb, w13, w2, hidden):
    ab = jnp.dot(xb, w13, preferred_element_type=F32)
    act = jax.nn.silu(ab[:, 0:hidden]) * ab[:, hidden:2 * hidden]
    return jnp.dot(act.astype(BF16), w2, preferred_element_type=F32)


def _moe_kernel(h_ref, c_ref, ws13_ref, ws2_ref, w13_ref, w2_ref, y_ref, hb_ref):
    e = pl.program_id(1)

    @pl.when(e == 0)
    def _():
        hb_ref[...] = h_ref[...].astype(BF16)
        y_ref[...] = _swiglu(hb_ref[...], ws13_ref[...], ws2_ref[...], SHARED_DIM)

    ye = _swiglu(hb_ref[...], w13_ref[...], w2_ref[...], EXPERT_DIM)
    lane = lax.broadcasted_iota(I32, c_ref.shape, 1)
    ce = jnp.sum(jnp.where(lane == e, c_ref[...], 0.0), axis=1, keepdims=True)
    y_ref[...] += ce * ye


def _moe(h, comb, ws13, ws2, w13, w2, tm):
    n = h.shape[0]
    return pl.pallas_call(
        _moe_kernel,
        grid=(n // tm, N_EXPERTS),
        in_specs=[pl.BlockSpec((tm, D_MODEL), lambda i, e: (i, 0)),
                  pl.BlockSpec((tm, LANES), lambda i, e: (i, 0)),
                  pl.BlockSpec((D_MODEL, 2 * SHARED_DIM), lambda i, e: (0, 0)),
                  pl.BlockSpec((SHARED_DIM, D_MODEL), lambda i, e: (0, 0)),
                  pl.BlockSpec((None, D_MODEL, 2 * EXPERT_DIM), lambda i, e: (e, 0, 0)),
                  pl.BlockSpec((None, EXPERT_DIM, D_MODEL), lambda i, e: (e, 0, 0))],
        out_specs=pl.BlockSpec((tm, D_MODEL), lambda i, e: (i, 0)),
        out_shape=jax.ShapeDtypeStruct((n, D_MODEL), F32),
        scratch_shapes=[pltpu.VMEM((tm, D_MODEL), BF16)],
        compiler_params=_params(("parallel", "arbitrary")),
        name="moe",
    )(h, comb, ws13, ws2, w13, w2)


def _final_kernel(h_ref, y_ref, pe_ref, g_ref, b_ref, wpg_ref, wpi_ref, o_ref, *, alpha):
    z = _layer_norm(alpha * h_ref[...] + y_ref[...], g_ref[...], b_ref[...])
    gate = jax.nn.sigmoid(jnp.dot(z.astype(BF16), wpg_ref[...], preferred_element_type=F32))
    emb = jnp.dot(pe_ref[...].astype(BF16), wpi_ref[...], preferred_element_type=F32)
    o_ref[...] = z + gate * emb


def _final(h, y, pe, g, b, wpg, wpi, tm, alpha):
    n = h.shape[0]
    row = lambda w: pl.BlockSpec((tm, w), lambda i: (i, 0))
    full = lambda r, c: pl.BlockSpec((r, c), lambda i: (0, 0))
    return pl.pallas_call(
        functools.partial(_final_kernel, alpha=alpha),
        grid=(n // tm,),
        in_specs=[row(D_MODEL), row(D_MODEL), row(PLE_DIM), full(1, D_MODEL), full(1, D_MODEL),
                  full(D_MODEL, D_MODEL), full(PLE_DIM, D_MODEL)],
        out_specs=row(D_MODEL),
        out_shape=jax.ShapeDtypeStruct((n, D_MODEL), F32),
        compiler_params=_params(("parallel",)),
        name="final",
    )(h, y, pe, g, b, wpg, wpi)


def _rope_table(pos):
    inv = ROPE_THETA ** (-jnp.arange(0, HEAD_DIM, 2, dtype=F32) / HEAD_DIM)
    ang = pos.astype(F32)[:, None] * inv[None, :]
    return jnp.concatenate([jnp.tile(jnp.cos(ang), (1, 4)), jnp.tile(jnp.sin(ang), (1, 4))], axis=1)


def _rotate_half_cols(w, n_heads):
    w3 = w.reshape(w.shape[0], n_heads, HEAD_DIM)
    half = HEAD_DIM // 2
    return jnp.concatenate([-w3[..., half:], w3[..., :half]], axis=-1).reshape(w.shape)


def _fused_in_weight(w_in):
    offs = np.cumsum(IN_SIZES)[:-1].tolist()
    wq, wk, wv, wqi, wki, wwi, wu, wga, wgb = jnp.split(w_in, offs, axis=1)
    pad = jnp.zeros((D_MODEL, LANES - HEAD_DIM - IDX_HEADS), w_in.dtype)
    w_big = jnp.concatenate(
        [wq, _rotate_half_cols(wq, N_HEADS), wqi, _rotate_half_cols(wqi, IDX_HEADS),
         wk, wki, _rotate_half_cols(wk, 1), _rotate_half_cols(wki, 1), wv, wwi, pad, wu], axis=1)
    return w_big.astype(BF16), wga.astype(BF16), wgb.astype(BF16)


def _heads_major(a, n_heads):
    b, t, w = a.shape
    d = w // n_heads
    return a.reshape(b, t, n_heads, d).transpose(0, 2, 1, 3).reshape(b, n_heads * t, d)


def kernel(x_prompt, x_sample, cache_k, cache_v, cache_kidx, state_pool, page_table, p_prompt, p_sample, w_in, w_att_out, w_pool_grp, pool_scale, w_pool_out, w_out, ln1_g, ln1_b, w_router, router_bias, w_exp13, w_exp2, w_sh13, w_sh2, ln2_g, ln2_b, w_ple_in, w_ple_gate):
    B, S, D = x_prompt.shape
    DB, T, _ = x_sample.shape
    depth = w_in.shape[0]
    assert depth == 1, "single layer step"
    page = cache_k.shape[2]
    past = page_table.shape[1] * page
    alpha = (2 * depth) ** 0.25
    n_p, n_s = B * S, DB * T

    w_big, wga, wgb = _fused_in_weight(w_in[0])
    wao, wpo, wo = w_att_out[0].astype(BF16), w_pool_out[0].astype(BF16), w_out[0].astype(BF16)
    wgrp = w_pool_grp[0].astype(BF16)
    pscale = pool_scale[0].reshape(1, POOL_WIDTH)
    g1, b1 = ln1_g[0].reshape(1, D), ln1_b[0].reshape(1, D)
    g2, b2 = ln2_g[0].reshape(1, D), ln2_b[0].reshape(1, D)
    wr_t = w_router[0].T.astype(BF16)
    rbias = router_bias[0].reshape(N_EXPERTS, 1)
    w13, w2 = w_exp13[0].astype(BF16), w_exp2[0].astype(BF16)
    ws13, ws2 = w_sh13[0].astype(BF16), w_sh2[0].astype(BF16)
    wpg, wpi = w_ple_gate[0].astype(BF16), w_ple_in[0].astype(BF16)

    cs_p = _rope_table(jnp.arange(S, dtype=I32))
    cs_s = jnp.tile(_rope_table(past + jnp.arange(T, dtype=I32)), (DB, 1))

    xp = x_prompt.reshape(n_p, D)
    q, qi, k, v, ki, kb, vb, kib, wi, u = _proj(xp, w_big, cs_p, 512)
    attn_p = _attn_prompt(q, qi, wi, kb, vb, kib, B, S)
    u3 = u.reshape(B, S, POOL_WIDTH)
    pool_p = _pool(jnp.zeros((B, PREV_ROWS, POOL_WIDTH), F32), u3, wgrp, pscale, 0).reshape(n_p, POOL_WIDTH)
    h_p = _merge(xp, attn_p, pool_p, wga, wgb, wao, wpo, wo, g1, b1, 512, alpha)

    xs = x_sample.reshape(n_s, D)
    qs, qis, ks, vs, kis, _, _, _, wis, us = _proj(xs, w_big, cs_s, n_s)
    q_hq = _heads_major(qs.reshape(DB, T, ATT_WIDTH), N_HEADS)
    qi_hq = _heads_major(qis.reshape(DB, T, IDX_HEADS * IDX_DIM), IDX_HEADS)
    wi_hq = wis.reshape(DB, T, IDX_HEADS).transpose(0, 2, 1).reshape(DB, IDX_HEADS * T, 1)
    o_hq = _attn_sample(page_table, q_hq, qi_hq, wi_hq, ks.reshape(DB, T, HEAD_DIM), vs.reshape(DB, T, HEAD_DIM),
                        kis.reshape(DB, T, IDX_DIM), cache_k[0], cache_v[0], cache_kidx[0])
    attn_s = o_hq.reshape(DB, N_HEADS, T, HEAD_DIM).transpose(0, 2, 1, 3).reshape(n_s, ATT_WIDTH).astype(BF16)
    us3 = us.reshape(DB, T, POOL_WIDTH)
    prev_s = jnp.concatenate([jnp.zeros((DB, PREV_ROWS - POOL_STATE, POOL_WIDTH), F32), state_pool[0]], axis=1)
    pool_s = _pool(prev_s, us3, wgrp, pscale, past).reshape(n_s, POOL_WIDTH)
    h_s = _merge(xs, attn_s, pool_s, wga, wgb, wao, wpo, wo, g1, b1, n_s, alpha)

    def tail(h, pe, tm_r, tm_m, tm_f):
        comb = _router(h, wr_t, rbias, tm_r)
        y = _moe(h, comb, ws13, ws2, w13, w2, tm_m)
        return _final(h, y, pe, g2, b2, wpg, wpi, tm_f, alpha)

    y_p = tail(h_p, p_prompt[0].reshape(n_p, PLE_DIM), 1024, 1024, 512)
    y_s = tail(h_s, p_sample[0].reshape(n_s, PLE_DIM), n_s, n_s, n_s)

    ext_s = jnp.concatenate([state_pool[0], us3], axis=1)
    return (y_p.reshape(B, S, D), y_s.reshape(DB, T, D),
            k.reshape(1, B, S, HEAD_DIM), v.reshape(1, B, S, HEAD_DIM), ki.reshape(1, B, S, IDX_DIM),
            u3[:, S - POOL_STATE:][None],
            ks.reshape(1, DB, T, HEAD_DIM), vs.reshape(1, DB, T, HEAD_DIM), kis.reshape(1, DB, T, IDX_DIM),
            ext_s[:, T:][None])
```

```python
import functools

import numpy as np
import jax
import jax.numpy as jnp
from jax import lax
from jax.experimental import pallas as pl
from jax.experimental.pallas import tpu as pltpu

F32 = jnp.float32
BF16 = jnp.bfloat16
I32 = jnp.int32

D_MODEL = 1024
N_HEADS = 8
HEAD_DIM = 64
ATT_WIDTH = N_HEADS * HEAD_DIM
IDX_HEADS = 4
IDX_DIM = 64
TOP_K_MAX = 256
Q_BLOCK = 128
ROPE_THETA = 10000.0
POOL_WINDOWS = (2, 4, 8, 16)
POOL_GROUPS = 4
POOL_WIDTH = 512
POOL_GW = POOL_WIDTH // POOL_GROUPS
POOL_STATE = 15
N_EXPERTS = 64
TOP_K_EXPERTS = 8
N_GROUPS = 8
GROUP_SIZE = N_EXPERTS // N_GROUPS
TOPK_GROUPS = 4
EXPERT_DIM = 256
SHARED_DIM = 256
ROUTED_SCALE = 2.5
PLE_DIM = 256
LN_EPS = 1e-5
IN_SIZES = (ATT_WIDTH, HEAD_DIM, HEAD_DIM, IDX_HEADS * IDX_DIM, IDX_DIM, IDX_HEADS, POOL_WIDTH, D_MODEL, D_MODEL)

LANES = 128
INT_MIN = -2147483648
NEG_BIG = -1e30
VMEM_LIMIT = 56 * 1024 * 1024

C_Q, C_QR = 0, 512
C_QI, C_QIR = 1024, 1280
C_KK, C_KKR = 1536, 1664
C_VW = 1792
C_U = 1920
C_END = 2432

NT_DIMS = (((1,), (1,)), ((), ()))


def _params(sem):
    return pltpu.CompilerParams(dimension_semantics=sem, vmem_limit_bytes=VMEM_LIMIT)


def _layer_norm(x, g, b):
    mu = jnp.mean(x, axis=-1, keepdims=True)
    xc = x - mu
    var = jnp.mean(xc * xc, axis=-1, keepdims=True)
    return xc * lax.rsqrt(var + LN_EPS) * g + b


def _proj_kernel(x_ref, w_ref, cs_ref, q_ref, qi_ref, k_ref, v_ref, ki_ref, kb_ref, vb_ref, kib_ref,
                 wi_ref, u_ref):
    xb = x_ref[...].astype(BF16)
    cos = cs_ref[:, 0:LANES]
    sin = cs_ref[:, LANES:2 * LANES]

    def mm(c0, n):
        return jnp.dot(xb, w_ref[:, c0:c0 + n], preferred_element_type=F32)

    def rope(c0, c0r, n):
        reps = n // LANES
        cosn = jnp.concatenate([cos] * reps, axis=1) if reps > 1 else cos
        sinn = jnp.concatenate([sin] * reps, axis=1) if reps > 1 else sin
        return mm(c0, n) * cosn + mm(c0r, n) * sinn

    q_ref[...] = (rope(C_Q, C_QR, ATT_WIDTH) * 0.125).astype(BF16)
    qi_ref[...] = (rope(C_QI, C_QIR, IDX_HEADS * IDX_DIM) * 0.125).astype(BF16)
    kk = rope(C_KK, C_KKR, LANES)
    k = kk[:, 0:HEAD_DIM]
    ki = kk[:, HEAD_DIM:2 * HEAD_DIM]
    k_ref[...] = k
    ki_ref[...] = ki
    kb_ref[...] = k.astype(BF16)
    kib_ref[...] = ki.astype(BF16)
    vw = mm(C_VW, LANES)
    v = vw[:, 0:HEAD_DIM]
    v_ref[...] = v
    vb_ref[...] = v.astype(BF16)
    wi_ref[...] = vw[:, HEAD_DIM:HEAD_DIM + IDX_HEADS] * (IDX_HEADS ** -0.5)
    u_ref[...] = mm(C_U, POOL_WIDTH)


def _proj(x, w_big, cs, tm):
    n = x.shape[0]
    n_cs_blocks = cs.shape[0] // tm
    row = lambda w: pl.BlockSpec((tm, w), lambda i: (i, 0))
    out_shape = (
        jax.ShapeDtypeStruct((n, ATT_WIDTH), BF16),
        jax.ShapeDtypeStruct((n, IDX_HEADS * IDX_DIM), BF16),
        jax.ShapeDtypeStruct((n, HEAD_DIM), F32),
        jax.ShapeDtypeStruct((n, HEAD_DIM), F32),
        jax.ShapeDtypeStruct((n, IDX_DIM), F32),
        jax.ShapeDtypeStruct((n, HEAD_DIM), BF16),
        jax.ShapeDtypeStruct((n, HEAD_DIM), BF16),
        jax.ShapeDtypeStruct((n, IDX_DIM), BF16),
        jax.ShapeDtypeStruct((n, IDX_HEADS), F32),
        jax.ShapeDtypeStruct((n, POOL_WIDTH), F32),
    )
    return pl.pallas_call(
        _proj_kernel,
        grid=(n // tm,),
        in_specs=[
            row(D_MODEL),
            pl.BlockSpec((D_MODEL, C_END), lambda i: (0, 0)),
            pl.BlockSpec((tm, 2 * LANES), lambda i: (i % n_cs_blocks, 0)),
        ],
        out_specs=(row(ATT_WIDTH), row(IDX_HEADS * IDX_DIM), row(HEAD_DIM), row(HEAD_DIM), row(IDX_DIM),
                   row(HEAD_DIM), row(HEAD_DIM), row(IDX_DIM), row(IDX_HEADS), row(POOL_WIDTH)),
        out_shape=out_shape,
        compiler_params=_params(("parallel",)),
        name="proj",
    )(x, w_big, cs)


def _float_keys(score):
    bits = pltpu.bitcast(score, I32)
    return jnp.where(bits < 0, INT_MIN - bits, bits)


def _count(mask):
    return jnp.sum(mask.astype(F32), axis=1, keepdims=True)


def _topk_bias(key_ref, j_ref, adm, lc, k):
    rows = key_ref.shape[0]
    kf = float(k)

    def value_step(i, t_u):
        cand_u = t_u | jnp.left_shift(jnp.int32(1), 31 - i)
        cand_s = cand_u ^ INT_MIN
        cnt = _count(key_ref[:, 0:lc] >= cand_s)
        return jnp.where(cnt >= kf, cand_u, t_u)

    t_u = lax.fori_loop(0, 32, value_step, jnp.zeros((rows, 1), I32))
    thr = t_u ^ INT_MIN
    keys = key_ref[:, 0:lc]
    cnt_gt = _count(keys > thr)
    cnt_eq = _count(keys == thr)
    need = kf - cnt_gt
    cut_needed = jnp.logical_and(cnt_gt + cnt_eq > kf, thr != INT_MIN)
    any_cut = jnp.max(cut_needed.astype(F32)) > 0.0
    idx = lax.broadcasted_iota(I32, (rows, lc), 1)
    nbits = int(np.ceil(np.log2(lc)))

    j_ref[...] = jnp.full((rows, 1), lc, I32)

    @pl.when(any_cut)
    def _():
        def index_step(i, j):
            cand = j | jnp.left_shift(jnp.int32(1), nbits - 1 - i)
            c = _count(jnp.logical_and(key_ref[:, 0:lc] == thr, idx < cand))
            return jnp.where(c < need, cand, j)

        j_ref[...] = lax.fori_loop(0, nbits, index_step, jnp.zeros((rows, 1), I32))

    sel = jnp.logical_or(keys > thr, jnp.logical_and(keys == thr, idx <= j_ref[...]))
    return jnp.where(jnp.logical_and(sel, adm), 0.0, NEG_BIG)


def _attn_prompt_block(q_ref, qi_ref, wi_ref, kb_ref, vb_ref, kib_ref, o_ref, key_ref, bias_ref, j_ref, q0, lc,
                       top_k):
    tq = q_ref.shape[0]
    kib = kib_ref[0:lc, :]
    score = jnp.zeros((tq, lc), F32)
    for h in range(IDX_HEADS):
        d = lax.dot_general(qi_ref[:, h * IDX_DIM:(h + 1) * IDX_DIM], kib, NT_DIMS, preferred_element_type=F32)
        score = score + wi_ref[:, h:h + 1] * jnp.maximum(d, 0.0)
    idx = lax.broadcasted_iota(I32, (tq, lc), 1)
    qpos = q0 + lax.broadcasted_iota(I32, (tq, lc), 0)
    adm = idx <= qpos
    key_ref[:, 0:lc] = jnp.where(adm, _float_keys(score), INT_MIN)
    bias_ref[:, 0:lc] = _topk_bias(key_ref, j_ref, adm, lc, top_k)

    kb = kb_ref[0:lc, :]
    vb = vb_ref[0:lc, :]
    for h in range(N_HEADS):
        sl = slice(h * HEAD_DIM, (h + 1) * HEAD_DIM)
        lg = lax.dot_general(q_ref[:, sl], kb, NT_DIMS, preferred_element_type=F32) + bias_ref[:, 0:lc]
        m = jnp.max(lg, axis=1, keepdims=True)
        p = jnp.exp(lg - m)
        l = jnp.sum(p, axis=1, keepdims=True)
        o = jnp.dot(p.astype(BF16), vb, preferred_element_type=F32) / l
        o_ref[:, sl] = o.astype(BF16)


def _attn_prompt_kernel(q_ref, qi_ref, wi_ref, kb_ref, vb_ref, kib_ref, o_ref, key_ref, bias_ref, j_ref, *,
                        seq, n_classes, top_k):
    j = pl.program_id(1)
    blocks_per_class = seq // Q_BLOCK // n_classes
    cls = j // blocks_per_class
    for c in range(n_classes):
        @pl.when(cls == c)
        def _(c=c):
            _attn_prompt_block(q_ref, qi_ref, wi_ref, kb_ref, vb_ref, kib_ref, o_ref, key_ref, bias_ref, j_ref,
                               j * Q_BLOCK, (c + 1) * (seq // n_classes), top_k)


def _attn_prompt(q, qi, wi, kb, vb, kib, batch, seq):
    nb = seq // Q_BLOCK
    top_k = min(TOP_K_MAX, seq // 4)
    qrow = lambda w: pl.BlockSpec((Q_BLOCK, w), lambda b, j: (b * nb + j, 0))
    kv = pl.BlockSpec((None, seq, HEAD_DIM), lambda b, j: (b, 0, 0))
    kern = functools.partial(_attn_prompt_kernel, seq=seq, n_classes=4, top_k=top_k)
    return pl.pallas_call(
        kern,
        grid=(batch, nb),
        in_specs=[qrow(ATT_WIDTH), qrow(IDX_HEADS * IDX_DIM), qrow(IDX_HEADS), kv, kv, kv],
        out_specs=qrow(ATT_WIDTH),
        out_shape=jax.ShapeDtypeStruct((batch * seq, ATT_WIDTH), BF16),
        scratch_shapes=[pltpu.VMEM((Q_BLOCK, seq), I32), pltpu.VMEM((Q_BLOCK, seq), F32),
                        pltpu.VMEM((Q_BLOCK, 1), I32)],
        compiler_params=_params(("parallel", "arbitrary")),
        name="attn_prompt",
    )(q, qi, wi, kb.reshape(batch, seq, HEAD_DIM), vb.reshape(batch, seq, HEAD_DIM),
      kib.reshape(batch, seq, IDX_DIM))


SAMPLE_CHUNK = 1024


def _attn_sample_kernel(pt_ref, q_ref, qi_ref, wi_ref, kn_ref, vn_ref, kin_ref, ck_hbm, cv_hbm, cki_hbm, o_ref,
                        kbuf, vbuf, kibuf, sem, key_scr, bias_scr, lg_scr, j_scr, *, n_pages, page, t_new, top_k):
    b = pl.program_id(0)
    n_b = pl.num_programs(0)
    slot = b % 2
    past = n_pages * page
    lc = past + page
    n_chunks = past // SAMPLE_CHUNK

    def page_copies(bb, sl, p):
        phys = pt_ref[bb * n_pages + p]
        dst = pl.ds(pl.multiple_of(p * page, page), page)
        return [pltpu.make_async_copy(src.at[phys], buf.at[sl, :, dst], sem.at[i, sl])
                for i, (src, buf) in enumerate(((ck_hbm, kbuf), (cv_hbm, vbuf), (cki_hbm, kibuf)))]

    def start_batch(bb, sl):
        def body(p, carry):
            for cp in page_copies(bb, sl, p):
                cp.start()
            return carry
        lax.fori_loop(0, n_pages, body, 0)

    def wait_batch(bb, sl):
        def body(p, carry):
            for cp in page_copies(bb, sl, p):
                cp.wait()
            return carry
        lax.fori_loop(0, n_pages, body, 0)

    @pl.when(b == 0)
    def _():
        start_batch(0, 0)

    @pl.when(b + 1 < n_b)
    def _():
        start_batch(b + 1, 1 - slot)

    wait_batch(b, slot)

    def head_sum(d):
        r = wi_ref[...] * jnp.maximum(d, 0.0)
        s = r[0:t_new]
        for h in range(1, IDX_HEADS):
            s = s + r[h * t_new:(h + 1) * t_new]
        return s

    def new_rows(ref):
        pad = jnp.zeros((page - t_new, ref.shape[1]), F32)
        return jnp.concatenate([ref[...], pad], axis=0).astype(BF16)

    qi = qi_ref[...]
    for c in range(n_chunks):
        sl = slice(c * SAMPLE_CHUNK, (c + 1) * SAMPLE_CHUNK)
        d = jnp.dot(qi, kibuf[slot, :, sl].astype(BF16), preferred_element_type=F32)
        key_scr[:, sl] = _float_keys(head_sum(d))
    d_new = lax.dot_general(qi, new_rows(kin_ref), NT_DIMS, preferred_element_type=F32)
    adm_new = lax.broadcasted_iota(I32, (t_new, page), 1) <= lax.broadcasted_iota(I32, (t_new, page), 0)
    key_scr[:, past:lc] = jnp.where(adm_new, _float_keys(head_sum(d_new)), INT_MIN)

    idx = lax.broadcasted_iota(I32, (t_new, lc), 1)
    trow = lax.broadcasted_iota(I32, (t_new, lc), 0)
    bias_scr[...] = _topk_bias(key_scr, j_scr, idx - past <= trow, lc, top_k)

    q = q_ref[...]

    def bias_rows(sl):
        return jnp.concatenate([bias_scr[:, sl]] * N_HEADS, axis=0)

    m = jnp.full((N_HEADS * t_new, 1), -jnp.inf, F32)
    for c in range(n_chunks):
        sl = slice(c * SAMPLE_CHUNK, (c + 1) * SAMPLE_CHUNK)
        lg = jnp.dot(q, kbuf[slot, :, sl].astype(BF16), preferred_element_type=F32) + bias_rows(sl)
        lg_scr[:, sl] = lg
        m = jnp.maximum(m, jnp.max(lg, axis=1, keepdims=True))
    lg_new = lax.dot_general(q, new_rows(kn_ref), NT_DIMS, preferred_element_type=F32) + bias_rows(slice(past, lc))
    m = jnp.maximum(m, jnp.max(lg_new, axis=1, keepdims=True))

    p_new = jnp.exp(lg_new - m)
    l = jnp.sum(p_new, axis=1, keepdims=True)
    o = jnp.dot(p_new.astype(BF16), new_rows(vn_ref), preferred_element_type=F32)
    for c in range(n_chunks):
        sl = slice(c * SAMPLE_CHUNK, (c + 1) * SAMPLE_CHUNK)
        pr = jnp.exp(lg_scr[:, sl] - m)
        l = l + jnp.sum(pr, axis=1, keepdims=True)
        o = o + lax.dot_general(pr.astype(BF16), vbuf[slot, :, sl].astype(BF16), NT_DIMS,
                                preferred_element_type=F32)
    o_ref[...] = o / l


def _attn_sample(page_table, q_hq, qi_hq, wi_hq, k_new, v_new, ki_new, cache_kt, cache_vt, cache_kit):
    db, n_pages = page_table.shape
    page = cache_kt.shape[2]
    t_new = k_new.shape[1]
    past = n_pages * page
    lc = past + page
    top_k = min(TOP_K_MAX, (past + t_new) // 4)
    per_b = lambda r, w: pl.BlockSpec((None, r, w), lambda b, pt: (b, 0, 0))
    hbm = pl.BlockSpec(memory_space=pl.ANY)
    kern = functools.partial(_attn_sample_kernel, n_pages=n_pages, page=page, t_new=t_new, top_k=top_k)
    slab = pltpu.VMEM((2, HEAD_DIM, past), F32)
    grid_spec = pltpu.PrefetchScalarGridSpec(
        num_scalar_prefetch=1,
        grid=(db,),
        in_specs=[per_b(N_HEADS * t_new, HEAD_DIM), per_b(IDX_HEADS * t_new, IDX_DIM), per_b(IDX_HEADS * t_new, 1),
                  per_b(t_new, HEAD_DIM), per_b(t_new, HEAD_DIM), per_b(t_new, IDX_DIM),
                  hbm, hbm, hbm],
        out_specs=per_b(N_HEADS * t_new, HEAD_DIM),
        scratch_shapes=[slab, slab, slab, pltpu.SemaphoreType.DMA((3, 2)),
                        pltpu.VMEM((t_new, lc), I32), pltpu.VMEM((t_new, lc), F32),
                        pltpu.VMEM((N_HEADS * t_new, past), F32), pltpu.VMEM((t_new, 1), I32)],
    )
    return pl.pallas_call(
        kern,
        grid_spec=grid_spec,
        out_shape=jax.ShapeDtypeStruct((db, N_HEADS * t_new, HEAD_DIM), F32),
        compiler_params=_params(("arbitrary",)),
        name="attn_sample",
    )(page_table.reshape(-1), q_hq, qi_hq, wi_hq, k_new, v_new, ki_new, cache_kt, cache_vt, cache_kit)


PREV_ROWS = 16


def _pool_kernel(prev_ref, u_ref, wg_ref, sc_ref, o_ref, ext_ref, *, pos0):
    t_len = u_ref.shape[0]
    ext_ref[0:PREV_ROWS, :] = prev_ref[...]
    ext_ref[PREV_ROWS:PREV_ROWS + t_len, :] = u_ref[...]
    pos = pos0 + lax.broadcasted_iota(I32, (t_len, 1), 0)
    for g, w in enumerate(POOL_WINDOWS):
        sl = slice(g * POOL_GW, (g + 1) * POOL_GW)
        u_new = ext_ref[PREV_ROWS:PREV_ROWS + t_len, sl]
        win = u_new
        for back in range(1, w):
            win = win + ext_ref[PREV_ROWS - back:PREV_ROWS - back + t_len, sl]
        count = jnp.minimum(pos + 1, w).astype(F32)
        r = win / count - u_new
        mixed = jnp.dot(r.astype(BF16), wg_ref[g], preferred_element_type=F32) * sc_ref[:, sl]
        o_ref[:, sl] = mixed.astype(BF16)


def _pool(prev, u, w_grp, scale, pos0):
    nb, t_len, _ = u.shape
    return pl.pallas_call(
        functools.partial(_pool_kernel, pos0=pos0),
        grid=(nb,),
        in_specs=[pl.BlockSpec((None, PREV_ROWS, POOL_WIDTH), lambda b: (b, 0, 0)),
                  pl.BlockSpec((None, t_len, POOL_WIDTH), lambda b: (b, 0, 0)),
                  pl.BlockSpec((POOL_GROUPS, POOL_GW, POOL_GW), lambda b: (0, 0, 0)),
                  pl.BlockSpec((1, POOL_WIDTH), lambda b: (0, 0))],
        out_specs=pl.BlockSpec((None, t_len, POOL_WIDTH), lambda b: (b, 0, 0)),
        out_shape=jax.ShapeDtypeStruct((nb, t_len, POOL_WIDTH), BF16),
        scratch_shapes=[pltpu.VMEM((PREV_ROWS + t_len, POOL_WIDTH), F32)],
        compiler_params=_params(("parallel",)),
        name="pool",
    )(prev, u, w_grp, scale)


def _merge_kernel(x_ref, a_ref, p_ref, wga_ref, wgb_ref, wao_ref, wpo_ref, wo_ref, g_ref, b_ref, h_ref, *, alpha):
    x = x_ref[...]
    xb = x.astype(BF16)
    ga = jnp.dot(xb, wga_ref[...], preferred_element_type=F32)
    gb = jnp.dot(xb, wgb_ref[...], preferred_element_type=F32)
    ya = jnp.dot(a_ref[...], wao_ref[...], preferred_element_type=F32)
    yp = jnp.dot(p_ref[...], wpo_ref[...], preferred_element_type=F32)
    mix = jax.nn.sigmoid(ga) * ya + jax.nn.sigmoid(gb) * yp
    out = jnp.dot(mix.astype(BF16), wo_ref[...], preferred_element_type=F32)
    h_ref[...] = _layer_norm(alpha * x + out, g_ref[...], b_ref[...])


def _merge(x, attn, pool, wga, wgb, wao, wpo, wo, g, b, tm, alpha):
    n = x.shape[0]
    row = lambda w: pl.BlockSpec((tm, w), lambda i: (i, 0))
    full = lambda r, c: pl.BlockSpec((r, c), lambda i: (0, 0))
    return pl.pallas_call(
        functools.partial(_merge_kernel, alpha=alpha),
        grid=(n // tm,),
        in_specs=[row(D_MODEL), row(ATT_WIDTH), row(POOL_WIDTH), full(D_MODEL, D_MODEL), full(D_MODEL, D_MODEL),
                  full(ATT_WIDTH, D_MODEL), full(POOL_WIDTH, D_MODEL), full(D_MODEL, D_MODEL),
                  full(1, D_MODEL), full(1, D_MODEL)],
        out_specs=row(D_MODEL),
        out_shape=jax.ShapeDtypeStruct((n, D_MODEL), F32),
        compiler_params=_params(("parallel",)),
        name="merge",
    )(x, attn, pool, wga, wgb, wao, wpo, wo, g, b)


def _router_kernel(h_ref, wr_ref, bias_ref, c_ref):
    tm = h_ref.shape[0]
    logits = lax.dot_general(wr_ref[...], h_ref[...].astype(BF16), NT_DIMS, preferred_element_type=F32)
    s = jax.nn.sigmoid(logits)
    sb = s + bias_ref[...]
    neg_inf = -jnp.inf

    rows = []
    for g in range(N_GROUPS):
        blk = sb[g * GROUP_SIZE:(g + 1) * GROUP_SIZE, :]
        m1 = jnp.max(blk, axis=0, keepdims=True)
        is_m1 = blk == m1
        n_m1 = jnp.sum(is_m1.astype(F32), axis=0, keepdims=True)
        m2 = jnp.max(jnp.where(is_m1, neg_inf, blk), axis=0, keepdims=True)
        rows.append(m1 + jnp.where(n_m1 >= 2.0, m1, m2))
    gs = jnp.concatenate(rows, axis=0)

    gi = lax.broadcasted_iota(I32, (N_GROUPS, tm), 0)
    rank = jnp.zeros((N_GROUPS, tm), F32)
    for g in range(N_GROUPS):
        row = gs[g:g + 1, :]
        beats = jnp.logical_or(row > gs, jnp.logical_and(row == gs, g < gi))
        rank = rank + beats.astype(F32)
    gkeep = rank < float(TOPK_GROUPS)
    emask = jnp.concatenate(
        [jnp.broadcast_to(gkeep[g:g + 1, :], (GROUP_SIZE, tm)) for g in range(N_GROUPS)], axis=0)

    ei = lax.broadcasted_iota(I32, (N_EXPERTS, tm), 0)
    x = jnp.where(emask, sb, neg_inf)
    sel = jnp.zeros((N_EXPERTS, tm), jnp.bool_)
    for _ in range(TOP_K_EXPERTS):
        m = jnp.max(x, axis=0, keepdims=True)
        first = jnp.min(jnp.where(x == m, ei, N_EXPERTS), axis=0, keepdims=True)
        pick = ei == first
        sel = jnp.logical_or(sel, pick)
        x = jnp.where(pick, neg_inf, x)

    gate = jnp.where(sel, s, 0.0)
    comb = gate / jnp.sum(gate, axis=0, keepdims=True) * ROUTED_SCALE
    comb = jnp.concatenate([comb, jnp.zeros((LANES - N_EXPERTS, tm), F32)], axis=0)
    c_ref[...] = comb.T


def _router(h, wr_t, bias_col, tm):
    n = h.shape[0]
    return pl.pallas_call(
        _router_kernel,
        grid=(n // tm,),
        in_specs=[pl.BlockSpec((tm, D_MODEL), lambda i: (i, 0)),
                  pl.BlockSpec((N_EXPERTS, D_MODEL), lambda i: (0, 0)),
                  pl.BlockSpec((N_EXPERTS, 1), lambda i: (0, 0))],
        out_specs=pl.BlockSpec((tm, LANES), lambda i: (i, 0)),
        out_shape=jax.ShapeDtypeStruct((n, LANES), F32),
        compiler_params=_params(("parallel",)),
        name="router",
    )(h, wr_t, bias_col)


def _swiglu(xb, w13, w2, hidden):
    ab = jnp.dot(xb, w13, preferred_element_type=F32)
    act = jax.nn.silu(ab[:, 0:hidden]) * ab[:, hidden:2 * hidden]
    return jnp.dot(act.astype(BF16), w2, preferred_element_type=F32)


def _moe_kernel(h_ref, c_ref, ws13_ref, ws2_ref, w13_ref, w2_ref, y_ref, hb_ref):
    e = pl.program_id(1)

    @pl.when(e == 0)
    def _():
        hb_ref[...] = h_ref[...].astype(BF16)
        y_ref[...] = _swiglu(hb_ref[...], ws13_ref[...], ws2_ref[...], SHARED_DIM)

    ye = _swiglu(hb_ref[...], w13_ref[...], w2_ref[...], EXPERT_DIM)
    lane = lax.broadcasted_iota(I32, c_ref.shape, 1)
    ce = jnp.sum(jnp.where(lane == e, c_ref[...], 0.0), axis=1, keepdims=True)
    y_ref[...] += ce * ye


def _moe(h, comb, ws13, ws2, w13, w2, tm):
    n = h.shape[0]
    return pl.pallas_call(
        _moe_kernel,
        grid=(n // tm, N_EXPERTS),
        in_specs=[pl.BlockSpec((tm, D_MODEL), lambda i, e: (i, 0)),
                  pl.BlockSpec((tm, LANES), lambda i, e: (i, 0)),
                  pl.BlockSpec((D_MODEL, 2 * SHARED_DIM), lambda i, e: (0, 0)),
                  pl.BlockSpec((SHARED_DIM, D_MODEL), lambda i, e: (0, 0)),
                  pl.BlockSpec((None, D_MODEL, 2 * EXPERT_DIM), lambda i, e: (e, 0, 0)),
                  pl.BlockSpec((None, EXPERT_DIM, D_MODEL), lambda i, e: (e, 0, 0))],
        out_specs=pl.BlockSpec((tm, D_MODEL), lambda i, e: (i, 0)),
        out_shape=jax.ShapeDtypeStruct((n, D_MODEL), F32),
        scratch_shapes=[pltpu.VMEM((tm, D_MODEL), BF16)],
        compiler_params=_params(("parallel", "arbitrary")),
        name="moe",
    )(h, comb, ws13, ws2, w13, w2)


def _final_kernel(h_ref, y_ref, pe_ref, g_ref, b_ref, wpg_ref, wpi_ref, o_ref, *, alpha):
    z = _layer_norm(alpha * h_ref[...] + y_ref[...], g_ref[...], b_ref[...])
    gate = jax.nn.sigmoid(jnp.dot(z.astype(BF16), wpg_ref[...], preferred_element_type=F32))
    emb = jnp.dot(pe_ref[...].astype(BF16), wpi_ref[...], preferred_element_type=F32)
    o_ref[...] = z + gate * emb


def _final(h, y, pe, g, b, wpg, wpi, tm, alpha):
    n = h.shape[0]
    row = lambda w: pl.BlockSpec((tm, w), lambda i: (i, 0))
    full = lambda r, c: pl.BlockSpec((r, c), lambda i: (0, 0))
    return pl.pallas_call(
        functools.partial(_final_kernel, alpha=alpha),
        grid=(n // tm,),
        in_specs=[row(D_MODEL), row(D_MODEL), row(PLE_DIM), full(1, D_MODEL), full(1, D_MODEL),
                  full(D_MODEL, D_MODEL), full(PLE_DIM, D_MODEL)],
        out_specs=row(D_MODEL),
        out_shape=jax.ShapeDtypeStruct((n, D_MODEL), F32),
        compiler_params=_params(("parallel",)),
        name="final",
    )(h, y, pe, g, b, wpg, wpi)


def _rope_table(pos):
    inv = ROPE_THETA ** (-jnp.arange(0, HEAD_DIM, 2, dtype=F32) / HEAD_DIM)
    ang = pos.astype(F32)[:, None] * inv[None, :]
    return jnp.concatenate([jnp.tile(jnp.cos(ang), (1, 4)), jnp.tile(jnp.sin(ang), (1, 4))], axis=1)


def _rotate_half_cols(w, n_heads):
    w3 = w.reshape(w.shape[0], n_heads, HEAD_DIM)
    half = HEAD_DIM // 2
    return jnp.concatenate([-w3[..., half:], w3[..., :half]], axis=-1).reshape(w.shape)


def _fused_in_weight(w_in):
    offs = np.cumsum(IN_SIZES)[:-1].tolist()
    wq, wk, wv, wqi, wki, wwi, wu, wga, wgb = jnp.split(w_in, offs, axis=1)
    pad = jnp.zeros((D_MODEL, LANES - HEAD_DIM - IDX_HEADS), w_in.dtype)
    w_big = jnp.concatenate(
        [wq, _rotate_half_cols(wq, N_HEADS), wqi, _rotate_half_cols(wqi, IDX_HEADS),
         wk, wki, _rotate_half_cols(wk, 1), _rotate_half_cols(wki, 1), wv, wwi, pad, wu], axis=1)
    return w_big.astype(BF16), wga.astype(BF16), wgb.astype(BF16)


def _pages_transposed(cache):
    return jnp.transpose(cache[0], (0, 2, 1))


def _heads_major(a, n_heads):
    b, t, w = a.shape
    d = w // n_heads
    return a.reshape(b, t, n_heads, d).transpose(0, 2, 1, 3).reshape(b, n_heads * t, d)


def kernel(x_prompt, x_sample, cache_k, cache_v, cache_kidx, state_pool, page_table, p_prompt, p_sample, w_in, w_att_out, w_pool_grp, pool_scale, w_pool_out, w_out, ln1_g, ln1_b, w_router, router_bias, w_exp13, w_exp2, w_sh13, w_sh2, ln2_g, ln2_b, w_ple_in, w_ple_gate):
    B, S, D = x_prompt.shape
    DB, T, _ = x_sample.shape
    depth = w_in.shape[0]
    assert depth == 1, "single layer step"
    page = cache_k.shape[2]
    past = page_table.shape[1] * page
    alpha = (2 * depth) ** 0.25
    n_p, n_s = B * S, DB * T

    w_big, wga, wgb = _fused_in_weight(w_in[0])
    wao, wpo, wo = w_att_out[0].astype(BF16), w_pool_out[0].astype(BF16), w_out[0].astype(BF16)
    wgrp = w_pool_grp[0].astype(BF16)
    pscale = pool_scale[0].reshape(1, POOL_WIDTH)
    g1, b1 = ln1_g[0].reshape(1, D), ln1_b[0].reshape(1, D)
    g2, b2 = ln2_g[0].reshape(1, D), ln2_b[0].reshape(1, D)
    wr_t = w_router[0].T.astype(BF16)
    rbias = router_bias[0].reshape(N_EXPERTS, 1)
    w13, w2 = w_exp13[0].astype(BF16), w_exp2[0].astype(BF16)
    ws13, ws2 = w_sh13[0].astype(BF16), w_sh2[0].astype(BF16)
    wpg, wpi = w_ple_gate[0].astype(BF16), w_ple_in[0].astype(BF16)

    cs_p = _rope_table(jnp.arange(S, dtype=I32))
    cs_s = jnp.tile(_rope_table(past + jnp.arange(T, dtype=I32)), (DB, 1))

    xp = x_prompt.reshape(n_p, D)
    q, qi, k, v, ki, kb, vb, kib, wi, u = _proj(xp, w_big, cs_p, 512)
    attn_p = _attn_prompt(q, qi, wi, kb, vb, kib, B, S)
    u3 = u.reshape(B, S, POOL_WIDTH)
    pool_p = _pool(jnp.zeros((B, PREV_ROWS, POOL_WIDTH), F32), u3, wgrp, pscale, 0).reshape(n_p, POOL_WIDTH)
    h_p = _merge(xp, attn_p, pool_p, wga, wgb, wao, wpo, wo, g1, b1, 512, alpha)

    xs = x_sample.reshape(n_s, D)
    qs, qis, ks, vs, kis, _, _, _, wis, us = _proj(xs, w_big, cs_s, n_s)
    q_hq = _heads_major(qs.reshape(DB, T, ATT_WIDTH), N_HEADS)
    qi_hq = _heads_major(qis.reshape(DB, T, IDX_HEADS * IDX_DIM), IDX_HEADS)
    wi_hq = wis.reshape(DB, T, IDX_HEADS).transpose(0, 2, 1).reshape(DB, IDX_HEADS * T, 1)
    o_hq = _attn_sample(page_table, q_hq, qi_hq, wi_hq, ks.reshape(DB, T, HEAD_DIM), vs.reshape(DB, T, HEAD_DIM),
                        kis.reshape(DB, T, IDX_DIM), _pages_transposed(cache_k), _pages_transposed(cache_v),
                        _pages_transposed(cache_kidx))
    attn_s = o_hq.reshape(DB, N_HEADS, T, HEAD_DIM).transpose(0, 2, 1, 3).reshape(n_s, ATT_WIDTH).astype(BF16)
    us3 = us.reshape(DB, T, POOL_WIDTH)
    prev_s = jnp.concatenate([jnp.zeros((DB, PREV_ROWS - POOL_STATE, POOL_WIDTH), F32), state_pool[0]], axis=1)
    pool_s = _pool(prev_s, us3, wgrp, pscale, past).reshape(n_s, POOL_WIDTH)
    h_s = _merge(xs, attn_s, pool_s, wga, wgb, wao, wpo, wo, g1, b1, n_s, alpha)

    def tail(h, pe, tm_r, tm_m, tm_f):
        comb = _router(h, wr_t, rbias, tm_r)
        y = _moe(h, comb, ws13, ws2, w13, w2, tm_m)
        return _final(h, y, pe, g2, b2, wpg, wpi, tm_f, alpha)

    y_p = tail(h_p, p_prompt[0].reshape(n_p, PLE_DIM), 1024, 1024, 512)
    y_s = tail(h_s, p_sample[0].reshape(n_s, PLE_DIM), n_s, n_s, n_s)

    ext_s = jnp.concatenate([state_pool[0], us3], axis=1)
    return (y_p.reshape(B, S, D), y_s.reshape(DB, T, D),
            k.reshape(1, B, S, HEAD_DIM), v.reshape(1, B, S, HEAD_DIM), ki.reshape(1, B, S, IDX_DIM),
            u3[:, S - POOL_STATE:][None],
            ks.reshape(1, DB, T, HEAD_DIM), vs.reshape(1, DB, T, HEAD_DIM), kis.reshape(1, DB, T, IDX_DIM),
            ext_s[:, T:][None])
```

```python
import functools

import numpy as np
import jax
import jax.numpy as jnp
from jax import lax
from jax.experimental import pallas as pl
from jax.experimental.pallas import tpu as pltpu

F32 = jnp.float32
BF16 = jnp.bfloat16
I32 = jnp.int32

D_MODEL = 1024
N_HEADS = 8
HEAD_DIM = 64
ATT_WIDTH = N_HEADS * HEAD_DIM
IDX_HEADS = 4
IDX_DIM = 64
TOP_K_MAX = 256
Q_BLOCK = 128
ROPE_THETA = 10000.0
POOL_WINDOWS = (2, 4, 8, 16)
POOL_GROUPS = 4
POOL_WIDTH = 512
POOL_GW = POOL_WIDTH // POOL_GROUPS
POOL_STATE = 15
N_EXPERTS = 64
TOP_K_EXPERTS = 8
N_GROUPS = 8
GROUP_SIZE = N_EXPERTS // N_GROUPS
TOPK_GROUPS = 4
EXPERT_DIM = 256
SHARED_DIM = 256
ROUTED_SCALE = 2.5
PLE_DIM = 256
LN_EPS = 1e-5
IN_SIZES = (ATT_WIDTH, HEAD_DIM, HEAD_DIM, IDX_HEADS * IDX_DIM, IDX_DIM, IDX_HEADS, POOL_WIDTH, D_MODEL, D_MODEL)

LANES = 128
SUBLANES = 8
INT_MIN = -2147483648
NEG_BIG = -1e30
VMEM_LIMIT = 56 * 1024 * 1024

C_Q, C_QR = 0, 512
C_QI, C_QIR = 1024, 1280
C_KK, C_KKR = 1536, 1664
C_VW = 1792
C_U = 1920
C_END = 2432

NT_DIMS = (((1,), (1,)), ((), ()))


def _params(sem):
    return pltpu.CompilerParams(dimension_semantics=sem, vmem_limit_bytes=VMEM_LIMIT)


def _layer_norm(x, g, b):
    mu = jnp.mean(x, axis=-1, keepdims=True)
    xc = x - mu
    var = jnp.mean(xc * xc, axis=-1, keepdims=True)
    return xc * lax.rsqrt(var + LN_EPS) * g + b


def _proj_sample_kernel(x_ref, w_ref, cs_ref, q_ref, qi_ref, k_ref, v_ref, ki_ref, wi_ref, u_ref):
    xb = x_ref[...].astype(BF16)
    cos = cs_ref[:, 0:LANES]
    sin = cs_ref[:, LANES:2 * LANES]

    def mm(c0, n):
        return jnp.dot(xb, w_ref[:, c0:c0 + n], preferred_element_type=F32)

    def rope(c0, c0r, n):
        reps = n // LANES
        cosn = jnp.concatenate([cos] * reps, axis=1) if reps > 1 else cos
        sinn = jnp.concatenate([sin] * reps, axis=1) if reps > 1 else sin
        return mm(c0, n) * cosn + mm(c0r, n) * sinn

    q_ref[...] = (rope(C_Q, C_QR, ATT_WIDTH) * 0.125).astype(BF16)
    qi_ref[...] = (rope(C_QI, C_QIR, IDX_HEADS * IDX_DIM) * 0.125).astype(BF16)
    kk = rope(C_KK, C_KKR, LANES)
    k_ref[...] = kk[:, 0:HEAD_DIM]
    ki_ref[...] = kk[:, HEAD_DIM:2 * HEAD_DIM]
    vw = mm(C_VW, LANES)
    v_ref[...] = vw[:, 0:HEAD_DIM]
    wi_ref[...] = vw[:, HEAD_DIM:HEAD_DIM + IDX_HEADS] * (IDX_HEADS ** -0.5)
    u_ref[...] = mm(C_U, POOL_WIDTH)


def _proj_sample(x, w_big, cs):
    n = x.shape[0]
    full = lambda r, c: pl.BlockSpec((r, c), lambda i: (0, 0))
    widths = (ATT_WIDTH, IDX_HEADS * IDX_DIM, HEAD_DIM, HEAD_DIM, IDX_DIM, IDX_HEADS, POOL_WIDTH)
    dtypes = (BF16, BF16, F32, F32, F32, F32, F32)
    return pl.pallas_call(
        _proj_sample_kernel,
        grid=(1,),
        in_specs=[full(n, D_MODEL), full(D_MODEL, C_END), full(n, 2 * LANES)],
        out_specs=tuple(full(n, w) for w in widths),
        out_shape=tuple(jax.ShapeDtypeStruct((n, w), dt) for w, dt in zip(widths, dtypes)),
        compiler_params=_params(("arbitrary",)),
        name="proj_sample",
    )(x, w_big, cs)


def _proj_prompt_kernel(x_ref, w_ref, wt_ref, cs_ref, cst_ref, qt_ref, qit_ref, wit_ref, kb_ref, kib_ref, vbt_ref,
                        kt_ref, vt_ref, kit_ref, u_ref):
    xb = x_ref[...].astype(BF16)
    tm = xb.shape[0]
    cos = cs_ref[:, 0:LANES]
    sin = cs_ref[:, LANES:2 * LANES]
    cos_t = cst_ref[0:HEAD_DIM, :]
    sin_t = cst_ref[LANES:LANES + HEAD_DIM, :]

    def mm(c0, n):
        return jnp.dot(xb, w_ref[:, c0:c0 + n], preferred_element_type=F32)

    def mm_t(c0, n):
        return lax.dot_general(wt_ref[c0:c0 + n, :], xb, NT_DIMS, preferred_element_type=F32)

    def rope_t(c0, c0r, heads):
        cosn = jnp.concatenate([cos_t] * heads, axis=0) if heads > 1 else cos_t
        sinn = jnp.concatenate([sin_t] * heads, axis=0) if heads > 1 else sin_t
        return mm_t(c0, heads * HEAD_DIM) * cosn + mm_t(c0r, heads * HEAD_DIM) * sinn

    kk = mm(C_KK, LANES) * cos + mm(C_KKR, LANES) * sin
    kb_ref[...] = kk[:, 0:HEAD_DIM].astype(BF16)
    kib_ref[...] = kk[:, HEAD_DIM:2 * HEAD_DIM].astype(BF16)
    u_ref[...] = mm(C_U, POOL_WIDTH)

    qt = (rope_t(C_Q, C_QR, N_HEADS) * 0.125).astype(BF16)
    qit = (rope_t(C_QI, C_QIR, IDX_HEADS) * 0.125).astype(BF16)
    for blk in range(tm // Q_BLOCK):
        cols = slice(blk * Q_BLOCK, (blk + 1) * Q_BLOCK)
        for h in range(N_HEADS):
            qt_ref[blk, :, h * Q_BLOCK:(h + 1) * Q_BLOCK] = qt[h * HEAD_DIM:(h + 1) * HEAD_DIM, cols]
        for h in range(IDX_HEADS):
            qit_ref[blk, :, h * Q_BLOCK:(h + 1) * Q_BLOCK] = qit[h * IDX_DIM:(h + 1) * IDX_DIM, cols]

    kkt = rope_t(C_KK, C_KKR, 2)
    kt_ref[...] = kkt[0:HEAD_DIM, :]
    kit_ref[...] = kkt[HEAD_DIM:2 * HEAD_DIM, :]
    vwt = mm_t(C_VW, LANES)
    vt_ref[...] = vwt[0:HEAD_DIM, :]
    vbt_ref[...] = vwt[0:HEAD_DIM, :].astype(BF16)
    wit_ref[...] = vwt[HEAD_DIM:HEAD_DIM + SUBLANES, :] * (IDX_HEADS ** -0.5)


def _proj_prompt(x, w_big, w_t, cs, seq, tm):
    n = x.shape[0]
    nb = seq // tm
    qb = tm // Q_BLOCK
    row = lambda w: pl.BlockSpec((tm, w), lambda i: (i, 0))
    col = lambda r: pl.BlockSpec((None, r, tm), lambda i: (i // nb, 0, i % nb))
    slab = lambda heads: pl.BlockSpec((qb, HEAD_DIM, heads * Q_BLOCK), lambda i: (i, 0, 0))
    pm = lambda r, dt: jax.ShapeDtypeStruct((n // seq, r, seq), dt)
    out_shape = (
        jax.ShapeDtypeStruct((n // Q_BLOCK, HEAD_DIM, N_HEADS * Q_BLOCK), BF16),
        jax.ShapeDtypeStruct((n // Q_BLOCK, IDX_DIM, IDX_HEADS * Q_BLOCK), BF16),
        pm(SUBLANES, F32),
        jax.ShapeDtypeStruct((n, HEAD_DIM), BF16), jax.ShapeDtypeStruct((n, IDX_DIM), BF16),
        pm(HEAD_DIM, BF16),
        pm(HEAD_DIM, F32), pm(HEAD_DIM, F32), pm(IDX_DIM, F32),
        jax.ShapeDtypeStruct((n, POOL_WIDTH), F32),
    )
    return pl.pallas_call(
        _proj_prompt_kernel,
        grid=(n // tm,),
        in_specs=[
            row(D_MODEL),
            pl.BlockSpec((D_MODEL, C_END), lambda i: (0, 0)),
            pl.BlockSpec((C_U, D_MODEL), lambda i: (0, 0)),
            pl.BlockSpec((tm, 2 * LANES), lambda i: (i % nb, 0)),
            pl.BlockSpec((2 * LANES, tm), lambda i: (0, i % nb)),
        ],
        out_specs=(slab(N_HEADS), slab(IDX_HEADS), col(SUBLANES), row(HEAD_DIM), row(IDX_DIM), col(HEAD_DIM),
                   col(HEAD_DIM), col(HEAD_DIM), col(IDX_DIM), row(POOL_WIDTH)),
        out_shape=out_shape,
        compiler_params=_params(("parallel",)),
        name="proj_prompt",
    )(x, w_big, w_t, cs, cs.T)


def _float_of_rank(u):
    key = u ^ INT_MIN
    bits = jnp.where(key < 0, INT_MIN - key, key)
    return pltpu.bitcast(bits, F32)


def _count(mask):
    return jnp.sum(mask.astype(F32), axis=1, keepdims=True)


def _topk_bias(sc_ref, j_ref, adm, n_adm, lc, k):
    rows = sc_ref.shape[0]
    kf = float(k)

    def value_step(i, t_u):
        cand_u = t_u | jnp.left_shift(jnp.int32(1), 31 - i)
        cnt = _count(sc_ref[:, 0:lc] >= _float_of_rank(cand_u))
        return jnp.where(cnt >= kf, cand_u, t_u)

    t_u = lax.fori_loop(0, 32, value_step, jnp.zeros((rows, 1), I32))
    few = n_adm < k
    thr = jnp.where(few, -jnp.inf, _float_of_rank(t_u))
    sc = sc_ref[:, 0:lc]
    cnt_gt = _count(sc > thr)
    cnt_eq = _count(sc == thr)
    need = kf - cnt_gt
    cut_needed = jnp.logical_and(cnt_gt + cnt_eq > kf, jnp.logical_not(few))
    any_cut = jnp.max(cut_needed.astype(F32)) > 0.0
    idx = lax.broadcasted_iota(I32, (rows, lc), 1)
    nbits = int(np.ceil(np.log2(lc)))

    j_ref[...] = jnp.full((rows, 1), lc, I32)

    @pl.when(any_cut)
    def _():
        def index_step(i, j):
            cand = j | jnp.left_shift(jnp.int32(1), nbits - 1 - i)
            c = _count(jnp.logical_and(sc_ref[:, 0:lc] == thr, idx < cand))
            return jnp.where(c < need, cand, j)

        j_ref[...] = lax.fori_loop(0, nbits, index_step, jnp.zeros((rows, 1), I32))

    sel = jnp.logical_or(sc > thr, jnp.logical_and(sc == thr, idx <= j_ref[...]))
    return jnp.where(jnp.logical_and(sel, adm), 0.0, NEG_BIG)


ATTN_CHUNK = 256


def _attn_prompt_block(n_chunks, q0, top_k, qt_ref, qit_ref, wit_ref, kb_ref, kib_ref, vbt_ref, o_ref,
                       key_ref, bias_ref, lg_ref, j_ref):
    tq, ch = Q_BLOCK, ATTN_CHUNK
    seq = key_ref.shape[0]
    kf = float(top_k)
    kpos = lax.broadcasted_iota(I32, (ch, tq), 0)
    qpos = q0 + lax.broadcasted_iota(I32, (ch, tq), 1)

    def rows(c):
        return slice(c * ch, (c + 1) * ch)

    def fold(x, op):
        return op(x.reshape(ch // SUBLANES, SUBLANES, tq), axis=0)

    def head(x, h):
        return x[:, h * tq:(h + 1) * tq]

    qit = qit_ref[...]
    wit = wit_ref[...]
    for c in range(n_chunks):
        d = jnp.dot(kib_ref[rows(c), :], qit, preferred_element_type=F32)
        s = wit[0:1, :] * jnp.maximum(head(d, 0), 0.0)
        for h in range(1, IDX_HEADS):
            s = s + wit[h:h + 1, :] * jnp.maximum(head(d, h), 0.0)
        key_ref[rows(c), :] = jnp.where(c * ch + kpos <= qpos, s, -jnp.inf)

    def count(pred):
        acc = jnp.zeros((SUBLANES, tq), F32)
        for c in range(n_chunks):
            acc = acc + fold(pred(key_ref[rows(c), :], c).astype(F32), jnp.sum)
        return jnp.sum(acc, axis=0, keepdims=True)

    def value_step(i, t_u):
        cand_u = t_u | jnp.left_shift(jnp.int32(1), 31 - i)
        cand = _float_of_rank(cand_u)
        return jnp.where(count(lambda k, c: k >= cand) >= kf, cand_u, t_u)

    few = qpos[0:1, :] + 1 < top_k
    thr = jnp.where(few, -jnp.inf, _float_of_rank(lax.fori_loop(0, 32, value_step, jnp.zeros((1, tq), I32))))
    cnt_gt = count(lambda k, c: k > thr)
    cnt_eq = count(lambda k, c: k == thr)
    need = kf - cnt_gt
    cut_needed = jnp.logical_and(cnt_gt + cnt_eq > kf, jnp.logical_not(few))
    any_cut = jnp.max(cut_needed.astype(F32)) > 0.0

    nbits = int(np.ceil(np.log2(seq)))
    j_ref[...] = jnp.full(j_ref.shape, seq, I32)

    @pl.when(any_cut)
    def _():
        def index_step(i, j):
            cand = j | jnp.left_shift(jnp.int32(1), nbits - 1 - i)
            n_before = count(lambda k, c: jnp.logical_and(k == thr, c * ch + kpos < cand))
            return jnp.where(n_before < need, cand, j)

        j = lax.fori_loop(0, nbits, index_step, jnp.zeros((1, tq), I32))
        j_ref[...] = jnp.broadcast_to(j, j_ref.shape)

    j_cut = j_ref[0:1, :]
    for c in range(n_chunks):
        k = key_ref[rows(c), :]
        pos = c * ch + kpos
        sel = jnp.logical_or(k > thr, jnp.logical_and(k == thr, pos <= j_cut))
        bias_ref[rows(c), :] = jnp.where(jnp.logical_and(sel, pos <= qpos), 0.0, NEG_BIG)

    qt = qt_ref[...]
    mx = [jnp.full((SUBLANES, tq), -jnp.inf, F32) for _ in range(N_HEADS)]
    for c in range(n_chunks):
        lg = jnp.dot(kb_ref[rows(c), :], qt, preferred_element_type=F32)
        bias = bias_ref[rows(c), :]
        for h in range(N_HEADS):
            lgh = head(lg, h) + bias
            lg_ref[h, rows(c), :] = lgh
            mx[h] = jnp.maximum(mx[h], fold(lgh, jnp.max))

    outs = []
    for h in range(N_HEADS):
        m = jnp.max(mx[h], axis=0, keepdims=True)
        lsum = jnp.zeros((SUBLANES, tq), F32)
        ot = jnp.zeros((HEAD_DIM, tq), F32)
        for c in range(n_chunks):
            p = jnp.exp(lg_ref[h, rows(c), :] - m)
            lsum = lsum + fold(p, jnp.sum)
            ot = ot + jnp.dot(vbt_ref[:, rows(c)], p.astype(BF16), preferred_element_type=F32)
        outs.append(ot / jnp.sum(lsum, axis=0, keepdims=True))
    o_ref[...] = jnp.concatenate(outs, axis=0).T.astype(BF16)


def _attn_prompt_kernel(qt_ref, qit_ref, wit_ref, kb_ref, kib_ref, vbt_ref, o_ref, key_ref, bias_ref, lg_ref, j_ref,
                        *, top_k):
    jq = pl.program_id(1)
    blocks_per_chunk = ATTN_CHUNK // Q_BLOCK
    n_classes = key_ref.shape[0] // ATTN_CHUNK
    for cls in range(n_classes):
        @pl.when(jq // blocks_per_chunk == cls)
        def _(cls=cls):
            _attn_prompt_block(cls + 1, jq * Q_BLOCK, top_k, qt_ref, qit_ref, wit_ref, kb_ref, kib_ref, vbt_ref,
                               o_ref, key_ref, bias_ref, lg_ref, j_ref)


def _attn_prompt(qt, qit, wit, kb, kib, vbt):
    batch, _, seq = vbt.shape
    nb = seq // Q_BLOCK
    top_k = min(TOP_K_MAX, seq // 4)
    slab = lambda heads: pl.BlockSpec((None, HEAD_DIM, heads * Q_BLOCK), lambda b, j: (b * nb + j, 0, 0))
    keys = pl.BlockSpec((seq, HEAD_DIM), lambda b, j: (b, 0))
    return pl.pallas_call(
        functools.partial(_attn_prompt_kernel, top_k=top_k),
        grid=(batch, nb),
        in_specs=[slab(N_HEADS), slab(IDX_HEADS), pl.BlockSpec((None, SUBLANES, Q_BLOCK), lambda b, j: (b, 0, j)),
                  keys, keys, pl.BlockSpec((None, HEAD_DIM, seq), lambda b, j: (b, 0, 0))],
        out_specs=pl.BlockSpec((Q_BLOCK, ATT_WIDTH), lambda b, j: (b * nb + j, 0)),
        out_shape=jax.ShapeDtypeStruct((batch * seq, ATT_WIDTH), BF16),
        scratch_shapes=[pltpu.VMEM((seq, Q_BLOCK), F32), pltpu.VMEM((seq, Q_BLOCK), F32),
                        pltpu.VMEM((N_HEADS, seq, Q_BLOCK), F32), pltpu.VMEM((SUBLANES, Q_BLOCK), I32)],
        compiler_params=_params(("parallel", "arbitrary")),
        name="attn_prompt",
    )(qt, qit, wit, kb, kib, vbt)


SAMPLE_CHUNK = 1024


def _attn_sample_kernel(pt_ref, q_ref, qi_ref, wi_ref, kn_ref, vn_ref, kin_ref, ck_hbm, cv_hbm, cki_hbm, o_ref,
                        kbuf, vbuf, kibuf, sem, key_scr, bias_scr, lg_scr, j_scr, *, n_pages, page, t_new, top_k):
    b = pl.program_id(0)
    n_b = pl.num_programs(0)
    slot = b % 2
    past = n_pages * page
    lc = past + page
    n_chunks = past // SAMPLE_CHUNK

    def page_copies(bb, sl, p):
        phys = pt_ref[bb * n_pages + p]
        dst = pl.ds(pl.multiple_of(p * page, page), page)
        return [pltpu.make_async_copy(src.at[phys], buf.at[sl, :, dst], sem.at[i, sl])
                for i, (src, buf) in enumerate(((ck_hbm, kbuf), (cv_hbm, vbuf), (cki_hbm, kibuf)))]

    def start_batch(bb, sl):
        def body(p, carry):
            for cp in page_copies(bb, sl, p):
                cp.start()
            return carry
        lax.fori_loop(0, n_pages, body, 0)

    def wait_batch(bb, sl):
        def body(p, carry):
            for cp in page_copies(bb, sl, p):
                cp.wait()
            return carry
        lax.fori_loop(0, n_pages, body, 0)

    @pl.when(b == 0)
    def _():
        start_batch(0, 0)

    @pl.when(b + 1 < n_b)
    def _():
        start_batch(b + 1, 1 - slot)

    wait_batch(b, slot)

    def head_sum(d):
        r = wi_ref[...] * jnp.maximum(d, 0.0)
        s = r[0:t_new]
        for h in range(1, IDX_HEADS):
            s = s + r[h * t_new:(h + 1) * t_new]
        return s

    def new_rows(ref):
        pad = jnp.zeros((page - t_new, ref.shape[1]), F32)
        return jnp.concatenate([ref[...], pad], axis=0).astype(BF16)

    qi = qi_ref[...]
    for c in range(n_chunks):
        sl = slice(c * SAMPLE_CHUNK, (c + 1) * SAMPLE_CHUNK)
        d = jnp.dot(qi, kibuf[slot, :, sl].astype(BF16), preferred_element_type=F32)
        key_scr[:, sl] = head_sum(d)
    d_new = lax.dot_general(qi, new_rows(kin_ref), NT_DIMS, preferred_element_type=F32)
    adm_new = lax.broadcasted_iota(I32, (t_new, page), 1) <= lax.broadcasted_iota(I32, (t_new, page), 0)
    key_scr[:, past:lc] = jnp.where(adm_new, head_sum(d_new), -jnp.inf)

    idx = lax.broadcasted_iota(I32, (t_new, lc), 1)
    trow = lax.broadcasted_iota(I32, (t_new, lc), 0)
    n_adm = past + 1 + lax.broadcasted_iota(I32, (t_new, 1), 0)
    bias_scr[...] = _topk_bias(key_scr, j_scr, idx - past <= trow, n_adm, lc, top_k)

    q = q_ref[...]

    def bias_rows(sl):
        return jnp.concatenate([bias_scr[:, sl]] * N_HEADS, axis=0)

    m = jnp.full((N_HEADS * t_new, 1), -jnp.inf, F32)
    for c in range(n_chunks):
        sl = slice(c * SAMPLE_CHUNK, (c + 1) * SAMPLE_CHUNK)
        lg = jnp.dot(q, kbuf[slot, :, sl].astype(BF16), preferred_element_type=F32) + bias_rows(sl)
        lg_scr[:, sl] = lg
        m = jnp.maximum(m, jnp.max(lg, axis=1, keepdims=True))
    lg_new = lax.dot_general(q, new_rows(kn_ref), NT_DIMS, preferred_element_type=F32) + bias_rows(slice(past, lc))
    m = jnp.maximum(m, jnp.max(lg_new, axis=1, keepdims=True))

    p_new = jnp.exp(lg_new - m)
    l = jnp.sum(p_new, axis=1, keepdims=True)
    o = jnp.dot(p_new.astype(BF16), new_rows(vn_ref), preferred_element_type=F32)
    for c in range(n_chunks):
        sl = slice(c * SAMPLE_CHUNK, (c + 1) * SAMPLE_CHUNK)
        pr = jnp.exp(lg_scr[:, sl] - m)
        l = l + jnp.sum(pr, axis=1, keepdims=True)
        o = o + lax.dot_general(pr.astype(BF16), vbuf[slot, :, sl].astype(BF16), NT_DIMS,
                                preferred_element_type=F32)
    o_ref[...] = o / l


def _attn_sample(page_table, q_hq, qi_hq, wi_hq, k_new, v_new, ki_new, cache_kt, cache_vt, cache_kit):
    db, n_pages = page_table.shape
    page = cache_kt.shape[2]
    t_new = k_new.shape[1]
    past = n_pages * page
    lc = past + page
    top_k = min(TOP_K_MAX, (past + t_new) // 4)
    per_b = lambda r, w: pl.BlockSpec((None, r, w), lambda b, pt: (b, 0, 0))
    hbm = pl.BlockSpec(memory_space=pl.ANY)
    kern = functools.partial(_attn_sample_kernel, n_pages=n_pages, page=page, t_new=t_new, top_k=top_k)
    slab = pltpu.VMEM((2, HEAD_DIM, past), F32)
    grid_spec = pltpu.PrefetchScalarGridSpec(
        num_scalar_prefetch=1,
        grid=(db,),
        in_specs=[per_b(N_HEADS * t_new, HEAD_DIM), per_b(IDX_HEADS * t_new, IDX_DIM), per_b(IDX_HEADS * t_new, 1),
                  per_b(t_new, HEAD_DIM), per_b(t_new, HEAD_DIM), per_b(t_new, IDX_DIM),
                  hbm, hbm, hbm],
        out_specs=per_b(N_HEADS * t_new, HEAD_DIM),
        scratch_shapes=[slab, slab, slab, pltpu.SemaphoreType.DMA((3, 2)),
                        pltpu.VMEM((t_new, lc), F32), pltpu.VMEM((t_new, lc), F32),
                        pltpu.VMEM((N_HEADS * t_new, past), F32), pltpu.VMEM((t_new, 1), I32)],
    )
    return pl.pallas_call(
        kern,
        grid_spec=grid_spec,
        out_shape=jax.ShapeDtypeStruct((db, N_HEADS * t_new, HEAD_DIM), F32),
        compiler_params=_params(("arbitrary",)),
        name="attn_sample",
    )(page_table.reshape(-1), q_hq, qi_hq, wi_hq, k_new, v_new, ki_new, cache_kt, cache_vt, cache_kit)


PREV_ROWS = 16


def _pool_kernel(prev_ref, u_ref, wg_ref, sc_ref, o_ref, ext_ref, *, pos0):
    t_len = u_ref.shape[0]
    ext_ref[0:PREV_ROWS, :] = prev_ref[...]
    ext_ref[PREV_ROWS:PREV_ROWS + t_len, :] = u_ref[...]
    pos = pos0 + lax.broadcasted_iota(I32, (t_len, 1), 0)
    for g, w in enumerate(POOL_WINDOWS):
        sl = slice(g * POOL_GW, (g + 1) * POOL_GW)
        u_new = ext_ref[PREV_ROWS:PREV_ROWS + t_len, sl]
        win = u_new
        for back in range(1, w):
            win = win + ext_ref[PREV_ROWS - back:PREV_ROWS - back + t_len, sl]
        count = jnp.minimum(pos + 1, w).astype(F32)
        r = win / count - u_new
        mixed = jnp.dot(r.astype(BF16), wg_ref[g], preferred_element_type=F32) * sc_ref[:, sl]
        o_ref[:, sl] = mixed.astype(BF16)


def _pool(prev, u, w_grp, scale, pos0):
    nb, t_len, _ = u.shape
    return pl.pallas_call(
        functools.partial(_pool_kernel, pos0=pos0),
        grid=(nb,),
        in_specs=[pl.BlockSpec((None, PREV_ROWS, POOL_WIDTH), lambda b: (b, 0, 0)),
                  pl.BlockSpec((None, t_len, POOL_WIDTH), lambda b: (b, 0, 0)),
                  pl.BlockSpec((POOL_GROUPS, POOL_GW, POOL_GW), lambda b: (0, 0, 0)),
                  pl.BlockSpec((1, POOL_WIDTH), lambda b: (0, 0))],
        out_specs=pl.BlockSpec((None, t_len, POOL_WIDTH), lambda b: (b, 0, 0)),
        out_shape=jax.ShapeDtypeStruct((nb, t_len, POOL_WIDTH), BF16),
        scratch_shapes=[pltpu.VMEM((PREV_ROWS + t_len, POOL_WIDTH), F32)],
        compiler_params=_params(("parallel",)),
        name="pool",
    )(prev, u, w_grp, scale)


def _merge_kernel(x_ref, a_ref, p_ref, wga_ref, wgb_ref, wao_ref, wpo_ref, wo_ref, g_ref, b_ref, h_ref, *, alpha):
    x = x_ref[...]
    xb = x.astype(BF16)
    ga = jnp.dot(xb, wga_ref[...], preferred_element_type=F32)
    gb = jnp.dot(xb, wgb_ref[...], preferred_element_type=F32)
    ya = jnp.dot(a_ref[...], wao_ref[...], preferred_element_type=F32)
    yp = jnp.dot(p_ref[...], wpo_ref[...], preferred_element_type=F32)
    mix = jax.nn.sigmoid(ga) * ya + jax.nn.sigmoid(gb) * yp
    out = jnp.dot(mix.astype(BF16), wo_ref[...], preferred_element_type=F32)
    h_ref[...] = _layer_norm(alpha * x + out, g_ref[...], b_ref[...])


def _merge(x, attn, pool, wga, wgb, wao, wpo, wo, g, b, tm, alpha):
    n = x.shape[0]
    row = lambda w: pl.BlockSpec((tm, w), lambda i: (i, 0))
    full = lambda r, c: pl.BlockSpec((r, c), lambda i: (0, 0))
    return pl.pallas_call(
        functools.partial(_merge_kernel, alpha=alpha),
        grid=(n // tm,),
        in_specs=[row(D_MODEL), row(ATT_WIDTH), row(POOL_WIDTH), full(D_MODEL, D_MODEL), full(D_MODEL, D_MODEL),
                  full(ATT_WIDTH, D_MODEL), full(POOL_WIDTH, D_MODEL), full(D_MODEL, D_MODEL),
                  full(1, D_MODEL), full(1, D_MODEL)],
        out_specs=row(D_MODEL),
        out_shape=jax.ShapeDtypeStruct((n, D_MODEL), F32),
        compiler_params=_params(("parallel",)),
        name="merge",
    )(x, attn, pool, wga, wgb, wao, wpo, wo, g, b)


def _router_kernel(h_ref, wr_ref, bias_ref, c_ref):
    tm = h_ref.shape[0]
    logits = lax.dot_general(wr_ref[...], h_ref[...].astype(BF16), NT_DIMS, preferred_element_type=F32)
    s = jax.nn.sigmoid(logits)
    sb = s + bias_ref[...]
    neg_inf = -jnp.inf

    rows = []
    for g in range(N_GROUPS):
        blk = sb[g * GROUP_SIZE:(g + 1) * GROUP_SIZE, :]
        m1 = jnp.max(blk, axis=0, keepdims=True)
        is_m1 = blk == m1
        n_m1 = jnp.sum(is_m1.astype(F32), axis=0, keepdims=True)
        m2 = jnp.max(jnp.where(is_m1, neg_inf, blk), axis=0, keepdims=True)
        rows.append(m1 + jnp.where(n_m1 >= 2.0, m1, m2))
    gs = jnp.concatenate(rows, axis=0)

    gi = lax.broadcasted_iota(I32, (N_GROUPS, tm), 0)
    rank = jnp.zeros((N_GROUPS, tm), F32)
    for g in range(N_GROUPS):
        row = gs[g:g + 1, :]
        beats = jnp.logical_or(row > gs, jnp.logical_and(row == gs, g < gi))
        rank = rank + beats.astype(F32)
    gkeep = rank < float(TOPK_GROUPS)
    emask = jnp.concatenate(
        [jnp.broadcast_to(gkeep[g:g + 1, :], (GROUP_SIZE, tm)) for g in range(N_GROUPS)], axis=0)

    ei = lax.broadcasted_iota(I32, (N_EXPERTS, tm), 0)
    x = jnp.where(emask, sb, neg_inf)
    sel = jnp.zeros((N_EXPERTS, tm), jnp.bool_)
    for _ in range(TOP_K_EXPERTS):
        m = jnp.max(x, axis=0, keepdims=True)
        first = jnp.min(jnp.where(x == m, ei, N_EXPERTS), axis=0, keepdims=True)
        pick = ei == first
        sel = jnp.logical_or(sel, pick)
        x = jnp.where(pick, neg_inf, x)

    gate = jnp.where(sel, s, 0.0)
    comb = gate / jnp.sum(gate, axis=0, keepdims=True) * ROUTED_SCALE
    comb = jnp.concatenate([comb, jnp.zeros((LANES - N_EXPERTS, tm), F32)], axis=0)
    c_ref[...] = comb.T


def _router(h, wr_t, bias_col, tm):
    n = h.shape[0]
    return pl.pallas_call(
        _router_kernel,
        grid=(n // tm,),
        in_specs=[pl.BlockSpec((tm, D_MODEL), lambda i: (i, 0)),
                  pl.BlockSpec((N_EXPERTS, D_MODEL), lambda i: (0, 0)),
                  pl.BlockSpec((N_EXPERTS, 1), lambda i: (0, 0))],
        out_specs=pl.BlockSpec((tm, LANES), lambda i: (i, 0)),
        out_shape=jax.ShapeDtypeStruct((n, LANES), F32),
        compiler_params=_params(("parallel",)),
        name="router",
    )(h, wr_t, bias_col)


def _swiglu(xb, w13, w2, hidden):
    ab = jnp.dot(xb, w13, preferred_element_type=F32)
    act = jax.nn.silu(ab[:, 0:hidden]) * ab[:, hidden:2 * hidden]
    return jnp.dot(act.astype(BF16), w2, preferred_element_type=F32)


def _moe_kernel(h_ref, c_ref, ws13_ref, ws2_ref, w13_ref, w2_ref, y_ref, hb_ref):
    e = pl.program_id(1)

    @pl.when(e == 0)
    def _():
        hb_ref[...] = h_ref[...].astype(BF16)
        y_ref[...] = _swiglu(hb_ref[...], ws13_ref[...], ws2_ref[...], SHARED_DIM)

    ye = _swiglu(hb_ref[...], w13_ref[...], w2_ref[...], EXPERT_DIM)
    lane = lax.broadcasted_iota(I32, c_ref.shape, 1)
    ce = jnp.sum(jnp.where(lane == e, c_ref[...], 0.0), axis=1, keepdims=True)
    y_ref[...] += ce * ye


def _moe(h, comb, ws13, ws2, w13, w2, tm):
    n = h.shape[0]
    return pl.pallas_call(
        _moe_kernel,
        grid=(n // tm, N_EXPERTS),
        in_specs=[pl.BlockSpec((tm, D_MODEL), lambda i, e: (i, 0)),
                  pl.BlockSpec((tm, LANES), lambda i, e: (i, 0)),
                  pl.BlockSpec((D_MODEL, 2 * SHARED_DIM), lambda i, e: (0, 0)),
                  pl.BlockSpec((SHARED_DIM, D_MODEL), lambda i, e: (0, 0)),
                  pl.BlockSpec((None, D_MODEL, 2 * EXPERT_DIM), lambda i, e: (e, 0, 0)),
                  pl.BlockSpec((None, EXPERT_DIM, D_MODEL), lambda i, e: (e, 0, 0))],
        out_specs=pl.BlockSpec((tm, D_MODEL), lambda i, e: (i, 0)),
        out_shape=jax.ShapeDtypeStruct((n, D_MODEL), F32),
        scratch_shapes=[pltpu.VMEM((tm, D_MODEL), BF16)],
        compiler_params=_params(("parallel", "arbitrary")),
        name="moe",
    )(h, comb, ws13, ws2, w13, w2)


def _final_kernel(h_ref, y_ref, pe_ref, g_ref, b_ref, wpg_ref, wpi_ref, o_ref, *, alpha):
    z = _layer_norm(alpha * h_ref[...] + y_ref[...], g_ref[...], b_ref[...])
    gate = jax.nn.sigmoid(jnp.dot(z.astype(BF16), wpg_ref[...], preferred_element_type=F32))
    emb = jnp.dot(pe_ref[...].astype(BF16), wpi_ref[...], preferred_element_type=F32)
    o_ref[...] = z + gate * emb


def _final(h, y, pe, g, b, wpg, wpi, tm, alpha):
    n = h.shape[0]
    row = lambda w: pl.BlockSpec((tm, w), lambda i: (i, 0))
    full = lambda r, c: pl.BlockSpec((r, c), lambda i: (0, 0))
    return pl.pallas_call(
        functools.partial(_final_kernel, alpha=alpha),
        grid=(n // tm,),
        in_specs=[row(D_MODEL), row(D_MODEL), row(PLE_DIM), full(1, D_MODEL), full(1, D_MODEL),
                  full(D_MODEL, D_MODEL), full(PLE_DIM, D_MODEL)],
        out_specs=row(D_MODEL),
        out_shape=jax.ShapeDtypeStruct((n, D_MODEL), F32),
        compiler_params=_params(("parallel",)),
        name="final",
    )(h, y, pe, g, b, wpg, wpi)


def _rope_table(pos):
    inv = ROPE_THETA ** (-jnp.arange(0, HEAD_DIM, 2, dtype=F32) / HEAD_DIM)
    ang = pos.astype(F32)[:, None] * inv[None, :]
    return jnp.concatenate([jnp.tile(jnp.cos(ang), (1, 4)), jnp.tile(jnp.sin(ang), (1, 4))], axis=1)


def _rotate_half_cols(w, n_heads):
    w3 = w.reshape(w.shape[0], n_heads, HEAD_DIM)
    half = HEAD_DIM // 2
    return jnp.concatenate([-w3[..., half:], w3[..., :half]], axis=-1).reshape(w.shape)


def _fused_in_weight(w_in):
    offs = np.cumsum(IN_SIZES)[:-1].tolist()
    wq, wk, wv, wqi, wki, wwi, wu, wga, wgb = jnp.split(w_in, offs, axis=1)
    pad = jnp.zeros((D_MODEL, LANES - HEAD_DIM - IDX_HEADS), w_in.dtype)
    w_big = jnp.concatenate(
        [wq, _rotate_half_cols(wq, N_HEADS), wqi, _rotate_half_cols(wqi, IDX_HEADS),
         wk, wki, _rotate_half_cols(wk, 1), _rotate_half_cols(wki, 1), wv, wwi, pad, wu], axis=1).astype(BF16)
    return w_big, w_big[:, 0:C_U].T, wga.astype(BF16), wgb.astype(BF16)


def _pages_transposed(cache):
    return jnp.transpose(cache[0], (0, 2, 1))


def _heads_major(a, n_heads):
    b, t, w = a.shape
    d = w // n_heads
    return a.reshape(b, t, n_heads, d).transpose(0, 2, 1, 3).reshape(b, n_heads * t, d)


def kernel(x_prompt, x_sample, cache_k, cache_v, cache_kidx, state_pool, page_table, p_prompt, p_sample, w_in, w_att_out, w_pool_grp, pool_scale, w_pool_out, w_out, ln1_g, ln1_b, w_router, router_bias, w_exp13, w_exp2, w_sh13, w_sh2, ln2_g, ln2_b, w_ple_in, w_ple_gate):
    B, S, D = x_prompt.shape
    DB, T, _ = x_sample.shape
    depth = w_in.shape[0]
    assert depth == 1, "single layer step"
    page = cache_k.shape[2]
    past = page_table.shape[1] * page
    alpha = (2 * depth) ** 0.25
    n_p, n_s = B * S, DB * T

    w_big, w_t, wga, wgb = _fused_in_weight(w_in[0])
    wao, wpo, wo = w_att_out[0].astype(BF16), w_pool_out[0].astype(BF16), w_out[0].astype(BF16)
    wgrp = w_pool_grp[0].astype(BF16)
    pscale = pool_scale[0].reshape(1, POOL_WIDTH)
    g1, b1 = ln1_g[0].reshape(1, D), ln1_b[0].reshape(1, D)
    g2, b2 = ln2_g[0].reshape(1, D), ln2_b[0].reshape(1, D)
    wr_t = w_router[0].T.astype(BF16)
    rbias = router_bias[0].reshape(N_EXPERTS, 1)
    w13, w2 = w_exp13[0].astype(BF16), w_exp2[0].astype(BF16)
    ws13, ws2 = w_sh13[0].astype(BF16), w_sh2[0].astype(BF16)
    wpg, wpi = w_ple_gate[0].astype(BF16), w_ple_in[0].astype(BF16)

    cs_p = _rope_table(jnp.arange(S, dtype=I32))
    cs_s = jnp.tile(_rope_table(past + jnp.arange(T, dtype=I32)), (DB, 1))

    xp = x_prompt.reshape(n_p, D)
    qt, qit, wit, kb, kib, vbt, kt, vt, kit, u = _proj_prompt(xp, w_big, w_t, cs_p, S, 512)
    attn_p = _attn_prompt(qt, qit, wit, kb, kib, vbt)
    u3 = u.reshape(B, S, POOL_WIDTH)
    pool_p = _pool(jnp.zeros((B, PREV_ROWS, POOL_WIDTH), F32), u3, wgrp, pscale, 0).reshape(n_p, POOL_WIDTH)
    h_p = _merge(xp, attn_p, pool_p, wga, wgb, wao, wpo, wo, g1, b1, 512, alpha)

    xs = x_sample.reshape(n_s, D)
    qs, qis, ks, vs, kis, wis, us = _proj_sample(xs, w_big, cs_s)
    q_hq = _heads_major(qs.reshape(DB, T, ATT_WIDTH), N_HEADS)
    qi_hq = _heads_major(qis.reshape(DB, T, IDX_HEADS * IDX_DIM), IDX_HEADS)
    wi_hq = wis.reshape(DB, T, IDX_HEADS).transpose(0, 2, 1).reshape(DB, IDX_HEADS * T, 1)
    o_hq = _attn_sample(page_table, q_hq, qi_hq, wi_hq, ks.reshape(DB, T, HEAD_DIM), vs.reshape(DB, T, HEAD_DIM),
                        kis.reshape(DB, T, IDX_DIM), _pages_transposed(cache_k), _pages_transposed(cache_v),
                        _pages_transposed(cache_kidx))
    attn_s = o_hq.reshape(DB, N_HEADS, T, HEAD_DIM).transpose(0, 2, 1, 3).reshape(n_s, ATT_WIDTH).astype(BF16)
    us3 = us.reshape(DB, T, POOL_WIDTH)
    prev_s = jnp.concatenate([jnp.zeros((DB, PREV_ROWS - POOL_STATE, POOL_WIDTH), F32), state_pool[0]], axis=1)
    pool_s = _pool(prev_s, us3, wgrp, pscale, past).reshape(n_s, POOL_WIDTH)
    h_s = _merge(xs, attn_s, pool_s, wga, wgb, wao, wpo, wo, g1, b1, n_s, alpha)

    def tail(h, pe, tm_r, tm_m, tm_f):
        comb = _router(h, wr_t, rbias, tm_r)
        y = _moe(h, comb, ws13, ws2, w13, w2, tm_m)
        return _final(h, y, pe, g2, b2, wpg, wpi, tm_f, alpha)

    y_p = tail(h_p, p_prompt[0].reshape(n_p, PLE_DIM), 1024, 1024, 512)
    y_s = tail(h_s, p_sample[0].reshape(n_s, PLE_DIM), n_s, n_s, n_s)

    ext_s = jnp.concatenate([state_pool[0], us3], axis=1)
    return (y_p.reshape(B, S, D), y_s.reshape(DB, T, D),
            jnp.transpose(kt, (0, 2, 1))[None], jnp.transpose(vt, (0, 2, 1))[None],
            jnp.transpose(kit, (0, 2, 1))[None],
            u3[:, S - POOL_STATE:][None],
            ks.reshape(1, DB, T, HEAD_DIM), vs.reshape(1, DB, T, HEAD_DIM), kis.reshape(1, DB, T, IDX_DIM),
            ext_s[:, T:][None])
```

```python
import functools

import numpy as np
import jax
import jax.numpy as jnp
from jax import lax
from jax.experimental import pallas as pl
from jax.experimental.pallas import tpu as pltpu

F32 = jnp.float32
BF16 = jnp.bfloat16
I32 = jnp.int32

D_MODEL = 1024
N_HEADS = 8
HEAD_DIM = 64
ATT_WIDTH = N_HEADS * HEAD_DIM
IDX_HEADS = 4
IDX_DIM = 64
TOP_K_MAX = 256
Q_BLOCK = 128
ROPE_THETA = 10000.0
POOL_WINDOWS = (2, 4, 8, 16)
POOL_GROUPS = 4
POOL_WIDTH = 512
POOL_GW = POOL_WIDTH // POOL_GROUPS
POOL_STATE = 15
N_EXPERTS = 64
TOP_K_EXPERTS = 8
N_GROUPS = 8
GROUP_SIZE = N_EXPERTS // N_GROUPS
TOPK_GROUPS = 4
EXPERT_DIM = 256
SHARED_DIM = 256
ROUTED_SCALE = 2.5
PLE_DIM = 256
LN_EPS = 1e-5
IN_SIZES = (ATT_WIDTH, HEAD_DIM, HEAD_DIM, IDX_HEADS * IDX_DIM, IDX_DIM, IDX_HEADS, POOL_WIDTH, D_MODEL, D_MODEL)

LANES = 128
SUBLANES = 8
INT_MIN = -2147483648
NEG_BIG = -1e30
VMEM_LIMIT = 56 * 1024 * 1024

C_Q, C_QR = 0, 512
C_QI, C_QIR = 1024, 1280
C_KK, C_KKR = 1536, 1664
C_VW = 1792
C_U = 1920
C_END = 2432

NT_DIMS = (((1,), (1,)), ((), ()))


def _params(sem):
    return pltpu.CompilerParams(dimension_semantics=sem, vmem_limit_bytes=VMEM_LIMIT)


def _layer_norm(x, g, b):
    mu = jnp.mean(x, axis=-1, keepdims=True)
    xc = x - mu
    var = jnp.mean(xc * xc, axis=-1, keepdims=True)
    return xc * lax.rsqrt(var + LN_EPS) * g + b


def _proj_sample_kernel(x_ref, w_ref, cs_ref, q_ref, qi_ref, k_ref, v_ref, ki_ref, wi_ref, u_ref):
    xb = x_ref[...].astype(BF16)
    cos = cs_ref[:, 0:LANES]
    sin = cs_ref[:, LANES:2 * LANES]

    def mm(c0, n):
        return jnp.dot(xb, w_ref[:, c0:c0 + n], preferred_element_type=F32)

    def rope(c0, c0r, n):
        reps = n // LANES
        cosn = jnp.concatenate([cos] * reps, axis=1) if reps > 1 else cos
        sinn = jnp.concatenate([sin] * reps, axis=1) if reps > 1 else sin
        return mm(c0, n) * cosn + mm(c0r, n) * sinn

    q_ref[...] = (rope(C_Q, C_QR, ATT_WIDTH) * 0.125).astype(BF16)
    qi_ref[...] = (rope(C_QI, C_QIR, IDX_HEADS * IDX_DIM) * 0.125).astype(BF16)
    kk = rope(C_KK, C_KKR, LANES)
    k_ref[...] = kk[:, 0:HEAD_DIM]
    ki_ref[...] = kk[:, HEAD_DIM:2 * HEAD_DIM]
    vw = mm(C_VW, LANES)
    v_ref[...] = vw[:, 0:HEAD_DIM]
    wi_ref[...] = vw[:, HEAD_DIM:HEAD_DIM + IDX_HEADS] * (IDX_HEADS ** -0.5)
    u_ref[...] = mm(C_U, POOL_WIDTH)


def _proj_sample(x, w_big, cs):
    n = x.shape[0]
    full = lambda r, c: pl.BlockSpec((r, c), lambda i: (0, 0))
    widths = (ATT_WIDTH, IDX_HEADS * IDX_DIM, HEAD_DIM, HEAD_DIM, IDX_DIM, IDX_HEADS, POOL_WIDTH)
    dtypes = (BF16, BF16, F32, F32, F32, F32, F32)
    return pl.pallas_call(
        _proj_sample_kernel,
        grid=(1,),
        in_specs=[full(n, D_MODEL), full(D_MODEL, C_END), full(n, 2 * LANES)],
        out_specs=tuple(full(n, w) for w in widths),
        out_shape=tuple(jax.ShapeDtypeStruct((n, w), dt) for w, dt in zip(widths, dtypes)),
        compiler_params=_params(("arbitrary",)),
        name="proj_sample",
    )(x, w_big, cs)


def _proj_prompt_kernel(x_ref, w_ref, wt_ref, cs_ref, cst_ref, qt_ref, qit_ref, wit_ref, kb_ref, kib_ref, vbt_ref,
                        kt_ref, vt_ref, kit_ref, u_ref):
    xb = x_ref[...].astype(BF16)
    tm = xb.shape[0]
    cos = cs_ref[:, 0:LANES]
    sin = cs_ref[:, LANES:2 * LANES]
    cos_t = cst_ref[0:HEAD_DIM, :]
    sin_t = cst_ref[LANES:LANES + HEAD_DIM, :]

    def mm(c0, n):
        return jnp.dot(xb, w_ref[:, c0:c0 + n], preferred_element_type=F32)

    def mm_t(c0, n):
        return lax.dot_general(wt_ref[c0:c0 + n, :], xb, NT_DIMS, preferred_element_type=F32)

    def rope_t(c0, c0r, heads):
        cosn = jnp.concatenate([cos_t] * heads, axis=0) if heads > 1 else cos_t
        sinn = jnp.concatenate([sin_t] * heads, axis=0) if heads > 1 else sin_t
        return mm_t(c0, heads * HEAD_DIM) * cosn + mm_t(c0r, heads * HEAD_DIM) * sinn

    kk = mm(C_KK, LANES) * cos + mm(C_KKR, LANES) * sin
    kb_ref[...] = kk[:, 0:HEAD_DIM].astype(BF16)
    kib_ref[...] = kk[:, HEAD_DIM:2 * HEAD_DIM].astype(BF16)
    u_ref[...] = mm(C_U, POOL_WIDTH)

    qt = (rope_t(C_Q, C_QR, N_HEADS) * 0.125).astype(BF16)
    qit = (rope_t(C_QI, C_QIR, IDX_HEADS) * 0.125).astype(BF16)
    for blk in range(tm // Q_BLOCK):
        cols = slice(blk * Q_BLOCK, (blk + 1) * Q_BLOCK)
        for h in range(N_HEADS):
            qt_ref[blk, :, h * Q_BLOCK:(h + 1) * Q_BLOCK] = qt[h * HEAD_DIM:(h + 1) * HEAD_DIM, cols]
        for h in range(IDX_HEADS):
            qit_ref[blk, :, h * Q_BLOCK:(h + 1) * Q_BLOCK] = qit[h * IDX_DIM:(h + 1) * IDX_DIM, cols]

    kkt = rope_t(C_KK, C_KKR, 2)
    kt_ref[...] = kkt[0:HEAD_DIM, :]
    kit_ref[...] = kkt[HEAD_DIM:2 * HEAD_DIM, :]
    vwt = mm_t(C_VW, LANES)
    vt_ref[...] = vwt[0:HEAD_DIM, :]
    vbt_ref[...] = vwt[0:HEAD_DIM, :].astype(BF16)
    wit_ref[...] = vwt[HEAD_DIM:HEAD_DIM + SUBLANES, :] * (IDX_HEADS ** -0.5)


def _proj_prompt(x, w_big, w_t, cs, seq, tm):
    n = x.shape[0]
    nb = seq // tm
    qb = tm // Q_BLOCK
    row = lambda w: pl.BlockSpec((tm, w), lambda i: (i, 0))
    col = lambda r: pl.BlockSpec((None, r, tm), lambda i: (i // nb, 0, i % nb))
    slab = lambda heads: pl.BlockSpec((qb, HEAD_DIM, heads * Q_BLOCK), lambda i: (i, 0, 0))
    pm = lambda r, dt: jax.ShapeDtypeStruct((n // seq, r, seq), dt)
    out_shape = (
        jax.ShapeDtypeStruct((n // Q_BLOCK, HEAD_DIM, N_HEADS * Q_BLOCK), BF16),
        jax.ShapeDtypeStruct((n // Q_BLOCK, IDX_DIM, IDX_HEADS * Q_BLOCK), BF16),
        pm(SUBLANES, F32),
        jax.ShapeDtypeStruct((n, HEAD_DIM), BF16), jax.ShapeDtypeStruct((n, IDX_DIM), BF16),
        pm(HEAD_DIM, BF16),
        pm(HEAD_DIM, F32), pm(HEAD_DIM, F32), pm(IDX_DIM, F32),
        jax.ShapeDtypeStruct((n, POOL_WIDTH), F32),
    )
    return pl.pallas_call(
        _proj_prompt_kernel,
        grid=(n // tm,),
        in_specs=[
            row(D_MODEL),
            pl.BlockSpec((D_MODEL, C_END), lambda i: (0, 0)),
            pl.BlockSpec((C_U, D_MODEL), lambda i: (0, 0)),
            pl.BlockSpec((tm, 2 * LANES), lambda i: (i % nb, 0)),
            pl.BlockSpec((2 * LANES, tm), lambda i: (0, i % nb)),
        ],
        out_specs=(slab(N_HEADS), slab(IDX_HEADS), col(SUBLANES), row(HEAD_DIM), row(IDX_DIM), col(HEAD_DIM),
                   col(HEAD_DIM), col(HEAD_DIM), col(IDX_DIM), row(POOL_WIDTH)),
        out_shape=out_shape,
        compiler_params=_params(("parallel",)),
        name="proj_prompt",
    )(x, w_big, w_t, cs, cs.T)


def _float_of_rank(u):
    key = u ^ INT_MIN
    bits = jnp.where(key < 0, INT_MIN - key, key)
    return pltpu.bitcast(bits, F32)


def _count(mask):
    return jnp.sum(mask.astype(F32), axis=1, keepdims=True)


def _topk_bias(sc_ref, j_ref, adm, n_adm, lc, k):
    rows = sc_ref.shape[0]
    kf = float(k)

    def value_step(i, t_u):
        cand_u = t_u | jnp.left_shift(jnp.int32(1), 31 - i)
        cnt = _count(sc_ref[:, 0:lc] >= _float_of_rank(cand_u))
        return jnp.where(cnt >= kf, cand_u, t_u)

    t_u = lax.fori_loop(0, 32, value_step, jnp.zeros((rows, 1), I32))
    few = n_adm < k
    thr = jnp.where(few, -jnp.inf, _float_of_rank(t_u))
    sc = sc_ref[:, 0:lc]
    cnt_gt = _count(sc > thr)
    cnt_eq = _count(sc == thr)
    need = kf - cnt_gt
    cut_needed = jnp.logical_and(cnt_gt + cnt_eq > kf, jnp.logical_not(few))
    any_cut = jnp.max(cut_needed.astype(F32)) > 0.0
    idx = lax.broadcasted_iota(I32, (rows, lc), 1)
    nbits = int(np.ceil(np.log2(lc)))

    j_ref[...] = jnp.full((rows, 1), lc, I32)

    @pl.when(any_cut)
    def _():
        def index_step(i, j):
            cand = j | jnp.left_shift(jnp.int32(1), nbits - 1 - i)
            c = _count(jnp.logical_and(sc_ref[:, 0:lc] == thr, idx < cand))
            return jnp.where(c < need, cand, j)

        j_ref[...] = lax.fori_loop(0, nbits, index_step, jnp.zeros((rows, 1), I32))

    sel = jnp.logical_or(sc > thr, jnp.logical_and(sc == thr, idx <= j_ref[...]))
    return jnp.where(jnp.logical_and(sel, adm), 0.0, NEG_BIG)


ATTN_CHUNK = 256


def _attn_prompt_block(n_chunks, q0, top_k, qt_ref, qit_ref, wit_ref, kb_ref, kib_ref, vbt_ref, o_ref,
                       key_ref, bias_ref, lg_ref, j_ref):
    tq, ch = Q_BLOCK, ATTN_CHUNK
    seq = key_ref.shape[0]
    kf = float(top_k)
    kpos = lax.broadcasted_iota(I32, (ch, tq), 0)
    qpos = q0 + lax.broadcasted_iota(I32, (ch, tq), 1)

    def rows(c):
        return slice(c * ch, (c + 1) * ch)

    def fold(x, op):
        return op(x.reshape(ch // SUBLANES, SUBLANES, tq), axis=0)

    def head(x, h):
        return x[:, h * tq:(h + 1) * tq]

    qit = qit_ref[...]
    wit = wit_ref[...]
    for c in range(n_chunks):
        d = jnp.dot(kib_ref[rows(c), :], qit, preferred_element_type=F32)
        s = wit[0:1, :] * jnp.maximum(head(d, 0), 0.0)
        for h in range(1, IDX_HEADS):
            s = s + wit[h:h + 1, :] * jnp.maximum(head(d, h), 0.0)
        key_ref[rows(c), :] = jnp.where(c * ch + kpos <= qpos, s, -jnp.inf)

    def count(pred):
        acc = jnp.zeros((SUBLANES, tq), F32)
        for c in range(n_chunks):
            acc = acc + fold(pred(key_ref[rows(c), :], c).astype(F32), jnp.sum)
        return jnp.sum(acc, axis=0, keepdims=True)

    def value_step(i, t_u):
        cand_u = t_u | jnp.left_shift(jnp.int32(1), 31 - i)
        cand = _float_of_rank(cand_u)
        return jnp.where(count(lambda k, c: k >= cand) >= kf, cand_u, t_u)

    few = qpos[0:1, :] + 1 < top_k
    thr = jnp.where(few, -jnp.inf, _float_of_rank(lax.fori_loop(0, 32, value_step, jnp.zeros((1, tq), I32))))
    cnt_gt = count(lambda k, c: k > thr)
    cnt_eq = count(lambda k, c: k == thr)
    need = kf - cnt_gt
    cut_needed = jnp.logical_and(cnt_gt + cnt_eq > kf, jnp.logical_not(few))
    any_cut = jnp.max(cut_needed.astype(F32)) > 0.0

    nbits = int(np.ceil(np.log2(seq)))
    j_ref[...] = jnp.full(j_ref.shape, seq, I32)

    @pl.when(any_cut)
    def _():
        def index_step(i, j):
            cand = j | jnp.left_shift(jnp.int32(1), nbits - 1 - i)
            n_before = count(lambda k, c: jnp.logical_and(k == thr, c * ch + kpos < cand))
            return jnp.where(n_before < need, cand, j)

        j = lax.fori_loop(0, nbits, index_step, jnp.zeros((1, tq), I32))
        j_ref[...] = jnp.broadcast_to(j, j_ref.shape)

    j_cut = j_ref[0:1, :]
    for c in range(n_chunks):
        k = key_ref[rows(c), :]
        pos = c * ch + kpos
        sel = jnp.logical_or(k > thr, jnp.logical_and(k == thr, pos <= j_cut))
        bias_ref[rows(c), :] = jnp.where(jnp.logical_and(sel, pos <= qpos), 0.0, NEG_BIG)

    qt = qt_ref[...]
    mx = [jnp.full((SUBLANES, tq), -jnp.inf, F32) for _ in range(N_HEADS)]
    for c in range(n_chunks):
        lg = jnp.dot(kb_ref[rows(c), :], qt, preferred_element_type=F32)
        bias = bias_ref[rows(c), :]
        for h in range(N_HEADS):
            lgh = head(lg, h) + bias
            lg_ref[h, rows(c), :] = lgh
            mx[h] = jnp.maximum(mx[h], fold(lgh, jnp.max))

    outs = []
    for h in range(N_HEADS):
        m = jnp.max(mx[h], axis=0, keepdims=True)
        lsum = jnp.zeros((SUBLANES, tq), F32)
        ot = jnp.zeros((HEAD_DIM, tq), F32)
        for c in range(n_chunks):
            p = jnp.exp(lg_ref[h, rows(c), :] - m)
            lsum = lsum + fold(p, jnp.sum)
            ot = ot + jnp.dot(vbt_ref[:, rows(c)], p.astype(BF16), preferred_element_type=F32)
        outs.append(ot / jnp.sum(lsum, axis=0, keepdims=True))
    o_ref[...] = jnp.concatenate(outs, axis=0).T.astype(BF16)


def _attn_prompt_kernel(qt_ref, qit_ref, wit_ref, kb_ref, kib_ref, vbt_ref, o_ref, key_ref, bias_ref, lg_ref, j_ref,
                        *, top_k):
    jq = pl.program_id(1)
    blocks_per_chunk = ATTN_CHUNK // Q_BLOCK
    n_classes = key_ref.shape[0] // ATTN_CHUNK
    for cls in range(n_classes):
        @pl.when(jq // blocks_per_chunk == cls)
        def _(cls=cls):
            _attn_prompt_block(cls + 1, jq * Q_BLOCK, top_k, qt_ref, qit_ref, wit_ref, kb_ref, kib_ref, vbt_ref,
                               o_ref, key_ref, bias_ref, lg_ref, j_ref)


def _attn_prompt(qt, qit, wit, kb, kib, vbt):
    batch, _, seq = vbt.shape
    nb = seq // Q_BLOCK
    top_k = min(TOP_K_MAX, seq // 4)
    slab = lambda heads: pl.BlockSpec((None, HEAD_DIM, heads * Q_BLOCK), lambda b, j: (b * nb + j, 0, 0))
    keys = pl.BlockSpec((seq, HEAD_DIM), lambda b, j: (b, 0))
    return pl.pallas_call(
        functools.partial(_attn_prompt_kernel, top_k=top_k),
        grid=(batch, nb),
        in_specs=[slab(N_HEADS), slab(IDX_HEADS), pl.BlockSpec((None, SUBLANES, Q_BLOCK), lambda b, j: (b, 0, j)),
                  keys, keys, pl.BlockSpec((None, HEAD_DIM, seq), lambda b, j: (b, 0, 0))],
        out_specs=pl.BlockSpec((Q_BLOCK, ATT_WIDTH), lambda b, j: (b * nb + j, 0)),
        out_shape=jax.ShapeDtypeStruct((batch * seq, ATT_WIDTH), BF16),
        scratch_shapes=[pltpu.VMEM((seq, Q_BLOCK), F32), pltpu.VMEM((seq, Q_BLOCK), F32),
                        pltpu.VMEM((N_HEADS, seq, Q_BLOCK), F32), pltpu.VMEM((SUBLANES, Q_BLOCK), I32)],
        compiler_params=_params(("parallel", "arbitrary")),
        name="attn_prompt",
    )(qt, qit, wit, kb, kib, vbt)


SAMPLE_CHUNK = 1024


def _attn_sample_kernel(pt_ref, q_ref, qi_ref, wi_ref, kn_ref, vn_ref, kin_ref, ck_hbm, cv_hbm, cki_hbm, o_ref,
                        kbuf, vbuf, kibuf, sem, key_scr, bias_scr, lg_scr, j_scr, *, n_pages, page, t_new, top_k):
    b = pl.program_id(0)
    n_b = pl.num_programs(0)
    slot = b % 2
    past = n_pages * page
    lc = past + page
    n_chunks = past // SAMPLE_CHUNK

    def page_copies(bb, sl, p):
        phys = pt_ref[bb * n_pages + p]
        dst = pl.ds(pl.multiple_of(p * page, page), page)
        return [pltpu.make_async_copy(src.at[phys], buf.at[sl, :, dst], sem.at[i, sl])
                for i, (src, buf) in enumerate(((ck_hbm, kbuf), (cv_hbm, vbuf), (cki_hbm, kibuf)))]

    def start_batch(bb, sl):
        def body(p, carry):
            for cp in page_copies(bb, sl, p):
                cp.start()
            return carry
        lax.fori_loop(0, n_pages, body, 0)

    def wait_batch(bb, sl):
        def body(p, carry):
            for cp in page_copies(bb, sl, p):
                cp.wait()
            return carry
        lax.fori_loop(0, n_pages, body, 0)

    @pl.when(b == 0)
    def _():
        start_batch(0, 0)

    @pl.when(b + 1 < n_b)
    def _():
        start_batch(b + 1, 1 - slot)

    wait_batch(b, slot)

    def head_sum(d):
        r = wi_ref[...] * jnp.maximum(d, 0.0)
        s = r[0:t_new]
        for h in range(1, IDX_HEADS):
            s = s + r[h * t_new:(h + 1) * t_new]
        return s

    def new_rows(ref):
        pad = jnp.zeros((page - t_new, ref.shape[1]), F32)
        return jnp.concatenate([ref[...], pad], axis=0).astype(BF16)

    qi = qi_ref[...]
    for c in range(n_chunks):
        sl = slice(c * SAMPLE_CHUNK, (c + 1) * SAMPLE_CHUNK)
        d = jnp.dot(qi, kibuf[slot, :, sl].astype(BF16), preferred_element_type=F32)
        key_scr[:, sl] = head_sum(d)
    d_new = lax.dot_general(qi, new_rows(kin_ref), NT_DIMS, preferred_element_type=F32)
    adm_new = lax.broadcasted_iota(I32, (t_new, page), 1) <= lax.broadcasted_iota(I32, (t_new, page), 0)
    key_scr[:, past:lc] = jnp.where(adm_new, head_sum(d_new), -jnp.inf)

    idx = lax.broadcasted_iota(I32, (t_new, lc), 1)
    trow = lax.broadcasted_iota(I32, (t_new, lc), 0)
    n_adm = past + 1 + lax.broadcasted_iota(I32, (t_new, 1), 0)
    bias_scr[...] = _topk_bias(key_scr, j_scr, idx - past <= trow, n_adm, lc, top_k)

    q = q_ref[...]

    def bias_rows(sl):
        return jnp.concatenate([bias_scr[:, sl]] * N_HEADS, axis=0)

    m = jnp.full((N_HEADS * t_new, 1), -jnp.inf, F32)
    for c in range(n_chunks):
        sl = slice(c * SAMPLE_CHUNK, (c + 1) * SAMPLE_CHUNK)
        lg = jnp.dot(q, kbuf[slot, :, sl].astype(BF16), preferred_element_type=F32) + bias_rows(sl)
        lg_scr[:, sl] = lg
        m = jnp.maximum(m, jnp.max(lg, axis=1, keepdims=True))
    lg_new = lax.dot_general(q, new_rows(kn_ref), NT_DIMS, preferred_element_type=F32) + bias_rows(slice(past, lc))
    m = jnp.maximum(m, jnp.max(lg_new, axis=1, keepdims=True))

    p_new = jnp.exp(lg_new - m)
    l = jnp.sum(p_new, axis=1, keepdims=True)
    o = jnp.dot(p_new.astype(BF16), new_rows(vn_ref), preferred_element_type=F32)
    for c in range(n_chunks):
        sl = slice(c * SAMPLE_CHUNK, (c + 1) * SAMPLE_CHUNK)
        pr = jnp.exp(lg_scr[:, sl] - m)
        l = l + jnp.sum(pr, axis=1, keepdims=True)
        o = o + lax.dot_general(pr.astype(BF16), vbuf[slot, :, sl].astype(BF16), NT_DIMS,
                                preferred_element_type=F32)
    o_ref[...] = o / l


def _attn_sample(page_table, q_hq, qi_hq, wi_hq, k_new, v_new, ki_new, cache_kt, cache_vt, cache_kit):
    db, n_pages = page_table.shape
    page = cache_kt.shape[2]
    t_new = k_new.shape[1]
    past = n_pages * page
    lc = past + page
    top_k = min(TOP_K_MAX, (past + t_new) // 4)
    per_b = lambda r, w: pl.BlockSpec((None, r, w), lambda b, pt: (b, 0, 0))
    hbm = pl.BlockSpec(memory_space=pl.ANY)
    kern = functools.partial(_attn_sample_kernel, n_pages=n_pages, page=page, t_new=t_new, top_k=top_k)
    slab = pltpu.VMEM((2, HEAD_DIM, past), F32)
    grid_spec = pltpu.PrefetchScalarGridSpec(
        num_scalar_prefetch=1,
        grid=(db,),
        in_specs=[per_b(N_HEADS * t_new, HEAD_DIM), per_b(IDX_HEADS * t_new, IDX_DIM), per_b(IDX_HEADS * t_new, 1),
                  per_b(t_new, HEAD_DIM), per_b(t_new, HEAD_DIM), per_b(t_new, IDX_DIM),
                  hbm, hbm, hbm],
        out_specs=per_b(N_HEADS * t_new, HEAD_DIM),
        scratch_shapes=[slab, slab, slab, pltpu.SemaphoreType.DMA((3, 2)),
                        pltpu.VMEM((t_new, lc), F32), pltpu.VMEM((t_new, lc), F32),
                        pltpu.VMEM((N_HEADS * t_new, past), F32), pltpu.VMEM((t_new, 1), I32)],
    )
    return pl.pallas_call(
        kern,
        grid_spec=grid_spec,
        out_shape=jax.ShapeDtypeStruct((db, N_HEADS * t_new, HEAD_DIM), F32),
        compiler_params=_params(("arbitrary",)),
        name="attn_sample",
    )(page_table.reshape(-1), q_hq, qi_hq, wi_hq, k_new, v_new, ki_new, cache_kt, cache_vt, cache_kit)


PREV_ROWS = 16


def _pool_kernel(prev_ref, u_ref, wg_ref, sc_ref, o_ref, ext_ref, *, pos0):
    t_len = u_ref.shape[0]
    ext_ref[0:PREV_ROWS, :] = prev_ref[...]
    ext_ref[PREV_ROWS:PREV_ROWS + t_len, :] = u_ref[...]
    pos = pos0 + lax.broadcasted_iota(I32, (t_len, 1), 0)
    for g, w in enumerate(POOL_WINDOWS):
        sl = slice(g * POOL_GW, (g + 1) * POOL_GW)
        u_new = ext_ref[PREV_ROWS:PREV_ROWS + t_len, sl]
        win = u_new
        for back in range(1, w):
            win = win + ext_ref[PREV_ROWS - back:PREV_ROWS - back + t_len, sl]
        count = jnp.minimum(pos + 1, w).astype(F32)
        r = win / count - u_new
        mixed = jnp.dot(r.astype(BF16), wg_ref[g], preferred_element_type=F32) * sc_ref[:, sl]
        o_ref[:, sl] = mixed.astype(BF16)


def _pool(prev, u, w_grp, scale, pos0):
    nb, t_len, _ = u.shape
    return pl.pallas_call(
        functools.partial(_pool_kernel, pos0=pos0),
        grid=(nb,),
        in_specs=[pl.BlockSpec((None, PREV_ROWS, POOL_WIDTH), lambda b: (b, 0, 0)),
                  pl.BlockSpec((None, t_len, POOL_WIDTH), lambda b: (b, 0, 0)),
                  pl.BlockSpec((POOL_GROUPS, POOL_GW, POOL_GW), lambda b: (0, 0, 0)),
                  pl.BlockSpec((1, POOL_WIDTH), lambda b: (0, 0))],
        out_specs=pl.BlockSpec((None, t_len, POOL_WIDTH), lambda b: (b, 0, 0)),
        out_shape=jax.ShapeDtypeStruct((nb, t_len, POOL_WIDTH), BF16),
        scratch_shapes=[pltpu.VMEM((PREV_ROWS + t_len, POOL_WIDTH), F32)],
        compiler_params=_params(("parallel",)),
        name="pool",
    )(prev, u, w_grp, scale)


def _merge_kernel(x_ref, a_ref, p_ref, wga_ref, wgb_ref, wao_ref, wpo_ref, wo_ref, g_ref, b_ref, h_ref, *, alpha):
    x = x_ref[...]
    xb = x.astype(BF16)
    ga = jnp.dot(xb, wga_ref[...], preferred_element_type=F32)
    gb = jnp.dot(xb, wgb_ref[...], preferred_element_type=F32)
    ya = jnp.dot(a_ref[...], wao_ref[...], preferred_element_type=F32)
    yp = jnp.dot(p_ref[...], wpo_ref[...], preferred_element_type=F32)
    mix = jax.nn.sigmoid(ga) * ya + jax.nn.sigmoid(gb) * yp
    out = jnp.dot(mix.astype(BF16), wo_ref[...], preferred_element_type=F32)
    h_ref[...] = _layer_norm(alpha * x + out, g_ref[...], b_ref[...])


def _merge(x, attn, pool, wga, wgb, wao, wpo, wo, g, b, tm, alpha):
    n = x.shape[0]
    row = lambda w: pl.BlockSpec((tm, w), lambda i: (i, 0))
    full = lambda r, c: pl.BlockSpec((r, c), lambda i: (0, 0))
    return pl.pallas_call(
        functools.partial(_merge_kernel, alpha=alpha),
        grid=(n // tm,),
        in_specs=[row(D_MODEL), row(ATT_WIDTH), row(POOL_WIDTH), full(D_MODEL, D_MODEL), full(D_MODEL, D_MODEL),
                  full(ATT_WIDTH, D_MODEL), full(POOL_WIDTH, D_MODEL), full(D_MODEL, D_MODEL),
                  full(1, D_MODEL), full(1, D_MODEL)],
        out_specs=row(D_MODEL),
        out_shape=jax.ShapeDtypeStruct((n, D_MODEL), F32),
        compiler_params=_params(("parallel",)),
        name="merge",
    )(x, attn, pool, wga, wgb, wao, wpo, wo, g, b)


def _route(h, wr_t, bias_col):
    tm = h.shape[0]
    logits = lax.dot_general(wr_t, h.astype(BF16), NT_DIMS, preferred_element_type=F32)
    s = jax.nn.sigmoid(logits)
    sb = s + bias_col
    neg_inf = -jnp.inf

    rows = []
    for g in range(N_GROUPS):
        blk = sb[g * GROUP_SIZE:(g + 1) * GROUP_SIZE, :]
        m1 = jnp.max(blk, axis=0, keepdims=True)
        is_m1 = blk == m1
        n_m1 = jnp.sum(is_m1.astype(F32), axis=0, keepdims=True)
        m2 = jnp.max(jnp.where(is_m1, neg_inf, blk), axis=0, keepdims=True)
        rows.append(m1 + jnp.where(n_m1 >= 2.0, m1, m2))
    gs = jnp.concatenate(rows, axis=0)

    gi = lax.broadcasted_iota(I32, (N_GROUPS, tm), 0)
    rank = jnp.zeros((N_GROUPS, tm), F32)
    for g in range(N_GROUPS):
        row = gs[g:g + 1, :]
        beats = jnp.logical_or(row > gs, jnp.logical_and(row == gs, g < gi))
        rank = rank + beats.astype(F32)
    gkeep = rank < float(TOPK_GROUPS)
    emask = jnp.concatenate(
        [jnp.broadcast_to(gkeep[g:g + 1, :], (GROUP_SIZE, tm)) for g in range(N_GROUPS)], axis=0)

    ei = lax.broadcasted_iota(I32, (N_EXPERTS, tm), 0)
    x = jnp.where(emask, sb, neg_inf)
    sel = jnp.zeros((N_EXPERTS, tm), jnp.bool_)
    picks = []
    for _ in range(TOP_K_EXPERTS):
        m = jnp.max(x, axis=0, keepdims=True)
        first = jnp.min(jnp.where(x == m, ei, N_EXPERTS), axis=0, keepdims=True)
        pick = ei == first
        sel = jnp.logical_or(sel, pick)
        x = jnp.where(pick, neg_inf, x)
        picks.append(first)

    gate = jnp.where(sel, s, 0.0)
    comb = gate / jnp.sum(gate, axis=0, keepdims=True) * ROUTED_SCALE
    return comb, sel, picks


def _router_kernel(h_ref, wr_ref, bias_ref, c_ref):
    comb, _, _ = _route(h_ref[...], wr_ref[...], bias_ref[...])
    comb = jnp.concatenate([comb, jnp.zeros((LANES - N_EXPERTS, comb.shape[1]), F32)], axis=0)
    c_ref[...] = comb.T


def _router(h, wr_t, bias_col, tm):
    n = h.shape[0]
    return pl.pallas_call(
        _router_kernel,
        grid=(n // tm,),
        in_specs=[pl.BlockSpec((tm, D_MODEL), lambda i: (i, 0)),
                  pl.BlockSpec((N_EXPERTS, D_MODEL), lambda i: (0, 0)),
                  pl.BlockSpec((N_EXPERTS, 1), lambda i: (0, 0))],
        out_specs=pl.BlockSpec((tm, LANES), lambda i: (i, 0)),
        out_shape=jax.ShapeDtypeStruct((n, LANES), F32),
        compiler_params=_params(("parallel",)),
        name="router",
    )(h, wr_t, bias_col)


def _swiglu(xb, w13, w2, hidden):
    ab = jnp.dot(xb, w13, preferred_element_type=F32)
    act = jax.nn.silu(ab[:, 0:hidden]) * ab[:, hidden:2 * hidden]
    return jnp.dot(act.astype(BF16), w2, preferred_element_type=F32)


def _moe_kernel(h_ref, c_ref, ws13_ref, ws2_ref, w13_ref, w2_ref, y_ref, hb_ref):
    e = pl.program_id(1)

    @pl.when(e == 0)
    def _():
        hb_ref[...] = h_ref[...].astype(BF16)
        y_ref[...] = _swiglu(hb_ref[...], ws13_ref[...], ws2_ref[...], SHARED_DIM)

    ye = _swiglu(hb_ref[...], w13_ref[...], w2_ref[...], EXPERT_DIM)
    lane = lax.broadcasted_iota(I32, c_ref.shape, 1)
    ce = jnp.sum(jnp.where(lane == e, c_ref[...], 0.0), axis=1, keepdims=True)
    y_ref[...] += ce * ye


def _moe(h, comb, ws13, ws2, w13, w2, tm):
    n = h.shape[0]
    return pl.pallas_call(
        _moe_kernel,
        grid=(n // tm, N_EXPERTS),
        in_specs=[pl.BlockSpec((tm, D_MODEL), lambda i, e: (i, 0)),
                  pl.BlockSpec((tm, LANES), lambda i, e: (i, 0)),
                  pl.BlockSpec((D_MODEL, 2 * SHARED_DIM), lambda i, e: (0, 0)),
                  pl.BlockSpec((SHARED_DIM, D_MODEL), lambda i, e: (0, 0)),
                  pl.BlockSpec((None, D_MODEL, 2 * EXPERT_DIM), lambda i, e: (e, 0, 0)),
                  pl.BlockSpec((None, EXPERT_DIM, D_MODEL), lambda i, e: (e, 0, 0))],
        out_specs=pl.BlockSpec((tm, D_MODEL), lambda i, e: (i, 0)),
        out_shape=jax.ShapeDtypeStruct((n, D_MODEL), F32),
        scratch_shapes=[pltpu.VMEM((tm, D_MODEL), BF16)],
        compiler_params=_params(("parallel", "arbitrary")),
        name="moe",
    )(h, comb, ws13, ws2, w13, w2)


def _final_kernel(h_ref, y_ref, pe_ref, g_ref, b_ref, wpg_ref, wpi_ref, o_ref, *, alpha):
    z = _layer_norm(alpha * h_ref[...] + y_ref[...], g_ref[...], b_ref[...])
    gate = jax.nn.sigmoid(jnp.dot(z.astype(BF16), wpg_ref[...], preferred_element_type=F32))
    emb = jnp.dot(pe_ref[...].astype(BF16), wpi_ref[...], preferred_element_type=F32)
    o_ref[...] = z + gate * emb


def _final(h, y, pe, g, b, wpg, wpi, tm, alpha):
    n = h.shape[0]
    row = lambda w: pl.BlockSpec((tm, w), lambda i: (i, 0))
    full = lambda r, c: pl.BlockSpec((r, c), lambda i: (0, 0))
    return pl.pallas_call(
        functools.partial(_final_kernel, alpha=alpha),
        grid=(n // tm,),
        in_specs=[row(D_MODEL), row(D_MODEL), row(PLE_DIM), full(1, D_MODEL), full(1, D_MODEL),
                  full(D_MODEL, D_MODEL), full(PLE_DIM, D_MODEL)],
        out_specs=row(D_MODEL),
        out_shape=jax.ShapeDtypeStruct((n, D_MODEL), F32),
        compiler_params=_params(("parallel",)),
        name="final",
    )(h, y, pe, g, b, wpg, wpi)


MOE_BLOCK = 512


def _sorted_rows(n_tokens):
    worst = n_tokens * TOP_K_EXPERTS + N_EXPERTS * (MOE_BLOCK - 1)
    return -(-worst // MOE_BLOCK) * MOE_BLOCK


def _dispatch_kernel(h_ref, wr_ref, bias_ref, tri_ref, pos_ref, gate_ref, blk_ref, used_ref,
                     eidx_s, rank_s, gate_s, cnt_s):
    p = pl.program_id(0)
    i = pl.program_id(1)
    tm = h_ref.shape[0]
    ei = lax.broadcasted_iota(I32, (N_EXPERTS, tm), 0)

    @pl.when(p == 0)
    def _():
        comb, sel, picks = _route(h_ref[...], wr_ref[...], bias_ref[...])
        before = jnp.dot(sel.astype(BF16), tri_ref[...], preferred_element_type=F32)
        ranks, gates = [], []
        for first in picks:
            pick = ei == first
            ranks.append(jnp.sum(jnp.where(pick, before, 0.0), axis=0, keepdims=True))
            gates.append(jnp.sum(jnp.where(pick, comb, 0.0), axis=0, keepdims=True))
        eidx_s[i] = jnp.concatenate(picks, axis=0)
        rank_s[i] = jnp.concatenate(ranks, axis=0)
        gate_s[i] = jnp.concatenate(gates, axis=0)
        cnt_s[i] = jnp.broadcast_to(jnp.sum(sel.astype(F32), axis=1, keepdims=True), (N_EXPERTS, LANES))

    @pl.when(p == 1)
    def _():
        cnt = cnt_s[...]
        tile_id = lax.broadcasted_iota(I32, cnt.shape, 0)
        total = jnp.sum(cnt, axis=0)
        prior = jnp.sum(jnp.where(tile_id < i, cnt, 0.0), axis=0)
        seg = jnp.ceil(total * (1.0 / MOE_BLOCK)) * MOE_BLOCK
        lower = (lax.broadcasted_iota(I32, (N_EXPERTS, N_EXPERTS), 1)
                 < lax.broadcasted_iota(I32, (N_EXPERTS, N_EXPERTS), 0)).astype(F32)
        seg_off = jnp.dot(lower, seg, precision=lax.Precision.HIGHEST, preferred_element_type=F32)
        base = (seg_off + prior)[:, 0:1]
        eidx = eidx_s[i]
        rank = rank_s[i]
        rows = []
        for k in range(TOP_K_EXPERTS):
            pick = ei == eidx[k:k + 1, :]
            rows.append(rank[k:k + 1, :] + jnp.sum(jnp.where(pick, base, 0.0), axis=0, keepdims=True))
        pos_ref[...] = jnp.concatenate(rows, axis=0).astype(I32)
        gate_ref[...] = jnp.concatenate([gate_s[i], jnp.zeros((LANES - TOP_K_EXPERTS, tm), F32)], axis=0).T

        seg_end = (seg_off + seg)[:, 0:1]
        n_blk = blk_ref.shape[1]
        blk_start = (lax.broadcasted_iota(I32, (N_EXPERTS, n_blk), 1) * MOE_BLOCK).astype(F32)
        owner = jnp.sum((seg_end <= blk_start).astype(F32), axis=0, keepdims=True)
        blk_ref[...] = jnp.minimum(owner, N_EXPERTS - 1.0).astype(I32)
        used = seg_end[N_EXPERTS - 1:N_EXPERTS, :] * (1.0 / MOE_BLOCK)
        used_ref[...] = jnp.broadcast_to(used, used_ref.shape).astype(I32)


def _dispatch(h, wr_t, bias_col, tm):
    n = h.shape[0]
    n_tiles = n // tm
    n_blk = _sorted_rows(n) // MOE_BLOCK
    n_blk_pad = -(-n_blk // LANES) * LANES
    tri = jnp.triu(jnp.ones((tm, tm), BF16), k=1)
    const = lambda r, c: pl.BlockSpec((r, c), lambda p, i: (0, 0))
    per_tile = lambda dt: pltpu.VMEM((n_tiles, TOP_K_EXPERTS, tm), dt)
    return pl.pallas_call(
        _dispatch_kernel,
        grid=(2, n_tiles),
        in_specs=[pl.BlockSpec((tm, D_MODEL), lambda p, i: (i * (1 - p), 0)),
                  const(N_EXPERTS, D_MODEL), const(N_EXPERTS, 1), const(tm, tm)],
        out_specs=(pl.BlockSpec((TOP_K_EXPERTS, tm), lambda p, i: (0, i * p)),
                   pl.BlockSpec((tm, LANES), lambda p, i: (i * p, 0)),
                   const(1, n_blk_pad), const(1, LANES)),
        out_shape=(jax.ShapeDtypeStruct((TOP_K_EXPERTS, n), I32), jax.ShapeDtypeStruct((n, LANES), F32),
                   jax.ShapeDtypeStruct((1, n_blk_pad), I32), jax.ShapeDtypeStruct((1, LANES), I32)),
        scratch_shapes=[per_tile(I32), per_tile(F32), per_tile(F32), pltpu.VMEM((n_tiles, N_EXPERTS, LANES), F32)],
        compiler_params=_params(("arbitrary", "arbitrary")),
        name="dispatch",
    )(h, wr_t, bias_col, tri)


def _grouped_kernel(blk_ref, used_ref, xs_ref, w13_ref, w2_ref, ys_ref):
    b = pl.program_id(0)

    @pl.when(b < used_ref[0])
    def _():
        ys = _swiglu(xs_ref[...], w13_ref[...].astype(BF16), w2_ref[...].astype(BF16), EXPERT_DIM)
        ys_ref[...] = ys.astype(BF16)

    @pl.when(b >= used_ref[0])
    def _():
        ys_ref[...] = jnp.zeros(ys_ref.shape, BF16)


def _grouped(blk, used, xs, w13, w2):
    ns = xs.shape[0]
    grid_spec = pltpu.PrefetchScalarGridSpec(
        num_scalar_prefetch=2,
        grid=(ns // MOE_BLOCK,),
        in_specs=[pl.BlockSpec((MOE_BLOCK, D_MODEL), lambda b, blk, used: (b, 0)),
                  pl.BlockSpec((None, D_MODEL, 2 * EXPERT_DIM), lambda b, blk, used: (blk[b], 0, 0)),
                  pl.BlockSpec((None, EXPERT_DIM, D_MODEL), lambda b, blk, used: (blk[b], 0, 0))],
        out_specs=pl.BlockSpec((MOE_BLOCK, D_MODEL), lambda b, blk, used: (b, 0)),
    )
    return pl.pallas_call(
        _grouped_kernel,
        grid_spec=grid_spec,
        out_shape=jax.ShapeDtypeStruct((ns, D_MODEL), BF16),
        compiler_params=_params(("arbitrary",)),
        name="grouped",
    )(blk, used, xs, w13, w2)


def _combine_kernel(h_ref, g_ref, gate_ref, pe_ref, ws13_ref, ws2_ref, ln_g_ref, ln_b_ref, wpg_ref, wpi_ref, o_ref, *,
                    alpha):
    h = h_ref[...]
    y = _swiglu(h.astype(BF16), ws13_ref[...], ws2_ref[...], SHARED_DIM)
    gate = gate_ref[...]
    for k in range(TOP_K_EXPERTS):
        y = y + gate[:, k:k + 1] * g_ref[k].astype(F32)
    z = _layer_norm(alpha * h + y, ln_g_ref[...], ln_b_ref[...])
    ple_gate = jax.nn.sigmoid(jnp.dot(z.astype(BF16), wpg_ref[...], preferred_element_type=F32))
    emb = jnp.dot(pe_ref[...].astype(BF16), wpi_ref[...], preferred_element_type=F32)
    o_ref[...] = z + ple_gate * emb


def _combine(h, gathered, gate, pe, ws13, ws2, g, b, wpg, wpi, tm, alpha):
    n = h.shape[0]
    row = lambda w: pl.BlockSpec((tm, w), lambda i: (i, 0))
    full = lambda r, c: pl.BlockSpec((r, c), lambda i: (0, 0))
    return pl.pallas_call(
        functools.partial(_combine_kernel, alpha=alpha),
        grid=(n // tm,),
        in_specs=[row(D_MODEL), pl.BlockSpec((TOP_K_EXPERTS, tm, D_MODEL), lambda i: (0, i, 0)), row(LANES),
                  row(PLE_DIM), full(D_MODEL, 2 * SHARED_DIM), full(SHARED_DIM, D_MODEL),
                  full(1, D_MODEL), full(1, D_MODEL), full(D_MODEL, D_MODEL), full(PLE_DIM, D_MODEL)],
        out_specs=row(D_MODEL),
        out_shape=jax.ShapeDtypeStruct((n, D_MODEL), F32),
        compiler_params=_params(("parallel",)),
        name="combine",
    )(h, gathered, gate, pe, ws13, ws2, g, b, wpg, wpi)


def _rope_table(pos):
    inv = ROPE_THETA ** (-jnp.arange(0, HEAD_DIM, 2, dtype=F32) / HEAD_DIM)
    ang = pos.astype(F32)[:, None] * inv[None, :]
    return jnp.concatenate([jnp.tile(jnp.cos(ang), (1, 4)), jnp.tile(jnp.sin(ang), (1, 4))], axis=1)


def _rotate_half_cols(w, n_heads):
    w3 = w.reshape(w.shape[0], n_heads, HEAD_DIM)
    half = HEAD_DIM // 2
    return jnp.concatenate([-w3[..., half:], w3[..., :half]], axis=-1).reshape(w.shape)


def _fused_in_weight(w_in):
    offs = np.cumsum(IN_SIZES)[:-1].tolist()
    wq, wk, wv, wqi, wki, wwi, wu, wga, wgb = jnp.split(w_in, offs, axis=1)
    pad = jnp.zeros((D_MODEL, LANES - HEAD_DIM - IDX_HEADS), w_in.dtype)
    w_big = jnp.concatenate(
        [wq, _rotate_half_cols(wq, N_HEADS), wqi, _rotate_half_cols(wqi, IDX_HEADS),
         wk, wki, _rotate_half_cols(wk, 1), _rotate_half_cols(wki, 1), wv, wwi, pad, wu], axis=1).astype(BF16)
    return w_big, w_big[:, 0:C_U].T, wga.astype(BF16), wgb.astype(BF16)


def _pages_transposed(cache):
    return jnp.transpose(cache[0], (0, 2, 1))


def _heads_major(a, n_heads):
    b, t, w = a.shape
    d = w // n_heads
    return a.reshape(b, t, n_heads, d).transpose(0, 2, 1, 3).reshape(b, n_heads * t, d)


def kernel(x_prompt, x_sample, cache_k, cache_v, cache_kidx, state_pool, page_table, p_prompt, p_sample, w_in, w_att_out, w_pool_grp, pool_scale, w_pool_out, w_out, ln1_g, ln1_b, w_router, router_bias, w_exp13, w_exp2, w_sh13, w_sh2, ln2_g, ln2_b, w_ple_in, w_ple_gate):
    B, S, D = x_prompt.shape
    DB, T, _ = x_sample.shape
    depth = w_in.shape[0]
    assert depth == 1, "single layer step"
    page = cache_k.shape[2]
    past = page_table.shape[1] * page
    alpha = (2 * depth) ** 0.25
    n_p, n_s = B * S, DB * T

    w_big, w_t, wga, wgb = _fused_in_weight(w_in[0])
    wao, wpo, wo = w_att_out[0].astype(BF16), w_pool_out[0].astype(BF16), w_out[0].astype(BF16)
    wgrp = w_pool_grp[0].astype(BF16)
    pscale = pool_scale[0].reshape(1, POOL_WIDTH)
    g1, b1 = ln1_g[0].reshape(1, D), ln1_b[0].reshape(1, D)
    g2, b2 = ln2_g[0].reshape(1, D), ln2_b[0].reshape(1, D)
    wr_t = w_router[0].T.astype(BF16)
    rbias = router_bias[0].reshape(N_EXPERTS, 1)
    w13, w2 = w_exp13[0].astype(BF16), w_exp2[0].astype(BF16)
    ws13, ws2 = w_sh13[0].astype(BF16), w_sh2[0].astype(BF16)
    wpg, wpi = w_ple_gate[0].astype(BF16), w_ple_in[0].astype(BF16)

    cs_p = _rope_table(jnp.arange(S, dtype=I32))
    cs_s = jnp.tile(_rope_table(past + jnp.arange(T, dtype=I32)), (DB, 1))

    xp = x_prompt.reshape(n_p, D)
    qt, qit, wit, kb, kib, vbt, kt, vt, kit, u = _proj_prompt(xp, w_big, w_t, cs_p, S, 512)
    attn_p = _attn_prompt(qt, qit, wit, kb, kib, vbt)
    u3 = u.reshape(B, S, POOL_WIDTH)
    pool_p = _pool(jnp.zeros((B, PREV_ROWS, POOL_WIDTH), F32), u3, wgrp, pscale, 0).reshape(n_p, POOL_WIDTH)
    h_p = _merge(xp, attn_p, pool_p, wga, wgb, wao, wpo, wo, g1, b1, 512, alpha)

    xs = x_sample.reshape(n_s, D)
    qs, qis, ks, vs, kis, wis, us = _proj_sample(xs, w_big, cs_s)
    q_hq = _heads_major(qs.reshape(DB, T, ATT_WIDTH), N_HEADS)
    qi_hq = _heads_major(qis.reshape(DB, T, IDX_HEADS * IDX_DIM), IDX_HEADS)
    wi_hq = wis.reshape(DB, T, IDX_HEADS).transpose(0, 2, 1).reshape(DB, IDX_HEADS * T, 1)
    o_hq = _attn_sample(page_table, q_hq, qi_hq, wi_hq, ks.reshape(DB, T, HEAD_DIM), vs.reshape(DB, T, HEAD_DIM),
                        kis.reshape(DB, T, IDX_DIM), _pages_transposed(cache_k), _pages_transposed(cache_v),
                        _pages_transposed(cache_kidx))
    attn_s = o_hq.reshape(DB, N_HEADS, T, HEAD_DIM).transpose(0, 2, 1, 3).reshape(n_s, ATT_WIDTH).astype(BF16)
    us3 = us.reshape(DB, T, POOL_WIDTH)
    prev_s = jnp.concatenate([jnp.zeros((DB, PREV_ROWS - POOL_STATE, POOL_WIDTH), F32), state_pool[0]], axis=1)
    pool_s = _pool(prev_s, us3, wgrp, pscale, past).reshape(n_s, POOL_WIDTH)
    h_s = _merge(xs, attn_s, pool_s, wga, wgb, wao, wpo, wo, g1, b1, n_s, alpha)

    def tail(h, pe, tm_r, tm_m, tm_f):
        comb = _router(h, wr_t, rbias, tm_r)
        y = _moe(h, comb, ws13, ws2, w13, w2, tm_m)
        return _final(h, y, pe, g2, b2, wpg, wpi, tm_f, alpha)

    y_s = tail(h_s, p_sample[0].reshape(n_s, PLE_DIM), n_s, n_s, n_s)

    pos, gate, blk, used = _dispatch(h_p, wr_t, rbias, 1024)
    xs = jnp.zeros((_sorted_rows(n_p), D), BF16).at[pos.reshape(-1)].set(jnp.tile(h_p.astype(BF16), (TOP_K_EXPERTS, 1)))
    ys = _grouped(blk.reshape(-1), used.reshape(-1), xs, w_exp13[0], w_exp2[0])
    gathered = ys[pos]
    y_p = _combine(h_p, gathered, gate, p_prompt[0].reshape(n_p, PLE_DIM), ws13, ws2, g2, b2, wpg, wpi, 512, alpha)

    ext_s = jnp.concatenate([state_pool[0], us3], axis=1)
    return (y_p.reshape(B, S, D), y_s.reshape(DB, T, D),
            jnp.transpose(kt, (0, 2, 1))[None], jnp.transpose(vt, (0, 2, 1))[None],
            jnp.transpose(kit, (0, 2, 1))[None],
            u3[:, S - POOL_STATE:][None],
            ks.reshape(1, DB, T, HEAD_DIM), vs.reshape(1, DB, T, HEAD_DIM), kis.reshape(1, DB, T, IDX_DIM),
            ext_s[:, T:][None])
```

```python
import functools

import numpy as np
import jax
import jax.numpy as jnp
from jax import lax
from jax.experimental import pallas as pl
from jax.experimental.pallas import tpu as pltpu
from jax.experimental.pallas import tpu_sc as plsc

F32 = jnp.float32
BF16 = jnp.bfloat16
I32 = jnp.int32

D_MODEL = 1024
N_HEADS = 8
HEAD_DIM = 64
ATT_WIDTH = N_HEADS * HEAD_DIM
IDX_HEADS = 4
IDX_DIM = 64
TOP_K_MAX = 256
Q_BLOCK = 128
ROPE_THETA = 10000.0
POOL_WINDOWS = (2, 4, 8, 16)
POOL_GROUPS = 4
POOL_WIDTH = 512
POOL_GW = POOL_WIDTH // POOL_GROUPS
POOL_STATE = 15
N_EXPERTS = 64
TOP_K_EXPERTS = 8
N_GROUPS = 8
GROUP_SIZE = N_EXPERTS // N_GROUPS
TOPK_GROUPS = 4
EXPERT_DIM = 256
SHARED_DIM = 256
ROUTED_SCALE = 2.5
PLE_DIM = 256
LN_EPS = 1e-5
IN_SIZES = (ATT_WIDTH, HEAD_DIM, HEAD_DIM, IDX_HEADS * IDX_DIM, IDX_DIM, IDX_HEADS, POOL_WIDTH, D_MODEL, D_MODEL)

LANES = 128
SUBLANES = 8
INT_MIN = -2147483648
NEG_BIG = -1e30
VMEM_LIMIT = 56 * 1024 * 1024

C_Q, C_QR = 0, 512
C_QI, C_QIR = 1024, 1280
C_KK, C_KKR = 1536, 1664
C_VW = 1792
C_U = 1920
C_END = 2432

NT_DIMS = (((1,), (1,)), ((), ()))


def _params(sem):
    return pltpu.CompilerParams(dimension_semantics=sem, vmem_limit_bytes=VMEM_LIMIT)


def _layer_norm(x, g, b):
    mu = jnp.mean(x, axis=-1, keepdims=True)
    xc = x - mu
    var = jnp.mean(xc * xc, axis=-1, keepdims=True)
    return xc * lax.rsqrt(var + LN_EPS) * g + b


def _proj_sample_kernel(x_ref, w_ref, cs_ref, q_ref, qi_ref, k_ref, v_ref, ki_ref, wi_ref, u_ref):
    xb = x_ref[...].astype(BF16)
    cos = cs_ref[:, 0:LANES]
    sin = cs_ref[:, LANES:2 * LANES]

    def mm(c0, n):
        return jnp.dot(xb, w_ref[:, c0:c0 + n], preferred_element_type=F32)

    def rope(c0, c0r, n):
        reps = n // LANES
        cosn = jnp.concatenate([cos] * reps, axis=1) if reps > 1 else cos
        sinn = jnp.concatenate([sin] * reps, axis=1) if reps > 1 else sin
        return mm(c0, n) * cosn + mm(c0r, n) * sinn

    q_ref[...] = (rope(C_Q, C_QR, ATT_WIDTH) * 0.125).astype(BF16)
    qi_ref[...] = (rope(C_QI, C_QIR, IDX_HEADS * IDX_DIM) * 0.125).astype(BF16)
    kk = rope(C_KK, C_KKR, LANES)
    k_ref[...] = kk[:, 0:HEAD_DIM]
    ki_ref[...] = kk[:, HEAD_DIM:2 * HEAD_DIM]
    vw = mm(C_VW, LANES)
    v_ref[...] = vw[:, 0:HEAD_DIM]
    wi_ref[...] = vw[:, HEAD_DIM:HEAD_DIM + IDX_HEADS] * (IDX_HEADS ** -0.5)
    u_ref[...] = mm(C_U, POOL_WIDTH)


def _proj_sample(x, w_big, cs):
    n = x.shape[0]
    full = lambda r, c: pl.BlockSpec((r, c), lambda i: (0, 0))
    widths = (ATT_WIDTH, IDX_HEADS * IDX_DIM, HEAD_DIM, HEAD_DIM, IDX_DIM, IDX_HEADS, POOL_WIDTH)
    dtypes = (BF16, BF16, F32, F32, F32, F32, F32)
    return pl.pallas_call(
        _proj_sample_kernel,
        grid=(1,),
        in_specs=[full(n, D_MODEL), full(D_MODEL, C_END), full(n, 2 * LANES)],
        out_specs=tuple(full(n, w) for w in widths),
        out_shape=tuple(jax.ShapeDtypeStruct((n, w), dt) for w, dt in zip(widths, dtypes)),
        compiler_params=_params(("arbitrary",)),
        name="proj_sample",
    )(x, w_big, cs)


def _proj_prompt_kernel(x_ref, w_ref, wt_ref, cs_ref, cst_ref, qt_ref, qit_ref, wit_ref, kb_ref, kib_ref, vbt_ref,
                        kt_ref, vt_ref, kit_ref, u_ref):
    xb = x_ref[...].astype(BF16)
    tm = xb.shape[0]
    cos = cs_ref[:, 0:LANES]
    sin = cs_ref[:, LANES:2 * LANES]
    cos_t = cst_ref[0:HEAD_DIM, :]
    sin_t = cst_ref[LANES:LANES + HEAD_DIM, :]

    def mm(c0, n):
        return jnp.dot(xb, w_ref[:, c0:c0 + n], preferred_element_type=F32)

    def mm_t(c0, n):
        return lax.dot_general(wt_ref[c0:c0 + n, :], xb, NT_DIMS, preferred_element_type=F32)

    def rope_t(c0, c0r, heads):
        cosn = jnp.concatenate([cos_t] * heads, axis=0) if heads > 1 else cos_t
        sinn = jnp.concatenate([sin_t] * heads, axis=0) if heads > 1 else sin_t
        return mm_t(c0, heads * HEAD_DIM) * cosn + mm_t(c0r, heads * HEAD_DIM) * sinn

    kk = mm(C_KK, LANES) * cos + mm(C_KKR, LANES) * sin
    kb_ref[...] = kk[:, 0:HEAD_DIM].astype(BF16)
    kib_ref[...] = kk[:, HEAD_DIM:2 * HEAD_DIM].astype(BF16)
    u_ref[...] = mm(C_U, POOL_WIDTH)

    qt = (rope_t(C_Q, C_QR, N_HEADS) * 0.125).astype(BF16)
    qit = (rope_t(C_QI, C_QIR, IDX_HEADS) * 0.125).astype(BF16)
    for blk in range(tm // Q_BLOCK):
        cols = slice(blk * Q_BLOCK, (blk + 1) * Q_BLOCK)
        for h in range(N_HEADS):
            qt_ref[blk, :, h * Q_BLOCK:(h + 1) * Q_BLOCK] = qt[h * HEAD_DIM:(h + 1) * HEAD_DIM, cols]
        for h in range(IDX_HEADS):
            qit_ref[blk, :, h * Q_BLOCK:(h + 1) * Q_BLOCK] = qit[h * IDX_DIM:(h + 1) * IDX_DIM, cols]

    kkt = rope_t(C_KK, C_KKR, 2)
    kt_ref[...] = kkt[0:HEAD_DIM, :]
    kit_ref[...] = kkt[HEAD_DIM:2 * HEAD_DIM, :]
    vwt = mm_t(C_VW, LANES)
    vt_ref[...] = vwt[0:HEAD_DIM, :]
    vbt_ref[...] = vwt[0:HEAD_DIM, :].astype(BF16)
    wit_ref[...] = vwt[HEAD_DIM:HEAD_DIM + SUBLANES, :] * (IDX_HEADS ** -0.5)


def _proj_prompt(x, w_big, w_t, cs, seq, tm):
    n = x.shape[0]
    nb = seq // tm
    qb = tm // Q_BLOCK
    row = lambda w: pl.BlockSpec((tm, w), lambda i: (i, 0))
    col = lambda r: pl.BlockSpec((None, r, tm), lambda i: (i // nb, 0, i % nb))
    slab = lambda heads: pl.BlockSpec((qb, HEAD_DIM, heads * Q_BLOCK), lambda i: (i, 0, 0))
    pm = lambda r, dt: jax.ShapeDtypeStruct((n // seq, r, seq), dt)
    out_shape = (
        jax.ShapeDtypeStruct((n // Q_BLOCK, HEAD_DIM, N_HEADS * Q_BLOCK), BF16),
        jax.ShapeDtypeStruct((n // Q_BLOCK, IDX_DIM, IDX_HEADS * Q_BLOCK), BF16),
        pm(SUBLANES, F32),
        jax.ShapeDtypeStruct((n, HEAD_DIM), BF16), jax.ShapeDtypeStruct((n, IDX_DIM), BF16),
        pm(HEAD_DIM, BF16),
        pm(HEAD_DIM, F32), pm(HEAD_DIM, F32), pm(IDX_DIM, F32),
        jax.ShapeDtypeStruct((n, POOL_WIDTH), F32),
    )
    return pl.pallas_call(
        _proj_prompt_kernel,
        grid=(n // tm,),
        in_specs=[
            row(D_MODEL),
            pl.BlockSpec((D_MODEL, C_END), lambda i: (0, 0)),
            pl.BlockSpec((C_U, D_MODEL), lambda i: (0, 0)),
            pl.BlockSpec((tm, 2 * LANES), lambda i: (i % nb, 0)),
            pl.BlockSpec((2 * LANES, tm), lambda i: (0, i % nb)),
        ],
        out_specs=(slab(N_HEADS), slab(IDX_HEADS), col(SUBLANES), row(HEAD_DIM), row(IDX_DIM), col(HEAD_DIM),
                   col(HEAD_DIM), col(HEAD_DIM), col(IDX_DIM), row(POOL_WIDTH)),
        out_shape=out_shape,
        compiler_params=_params(("parallel",)),
        name="proj_prompt",
    )(x, w_big, w_t, cs, cs.T)


def _float_of_rank(u):
    key = u ^ INT_MIN
    bits = jnp.where(key < 0, INT_MIN - key, key)
    return pltpu.bitcast(bits, F32)


def _count(mask):
    return jnp.sum(mask.astype(F32), axis=1, keepdims=True)


def _topk_bias(sc_ref, j_ref, adm, n_adm, lc, k):
    rows = sc_ref.shape[0]
    kf = float(k)

    def value_step(i, t_u):
        cand_u = t_u | jnp.left_shift(jnp.int32(1), 31 - i)
        cnt = _count(sc_ref[:, 0:lc] >= _float_of_rank(cand_u))
        return jnp.where(cnt >= kf, cand_u, t_u)

    t_u = lax.fori_loop(0, 32, value_step, jnp.zeros((rows, 1), I32))
    few = n_adm < k
    thr = jnp.where(few, -jnp.inf, _float_of_rank(t_u))
    sc = sc_ref[:, 0:lc]
    cnt_gt = _count(sc > thr)
    cnt_eq = _count(sc == thr)
    need = kf - cnt_gt
    cut_needed = jnp.logical_and(cnt_gt + cnt_eq > kf, jnp.logical_not(few))
    any_cut = jnp.max(cut_needed.astype(F32)) > 0.0
    idx = lax.broadcasted_iota(I32, (rows, lc), 1)
    nbits = int(np.ceil(np.log2(lc)))

    j_ref[...] = jnp.full((rows, 1), lc, I32)

    @pl.when(any_cut)
    def _():
        def index_step(i, j):
            cand = j | jnp.left_shift(jnp.int32(1), nbits - 1 - i)
            c = _count(jnp.logical_and(sc_ref[:, 0:lc] == thr, idx < cand))
            return jnp.where(c < need, cand, j)

        j_ref[...] = lax.fori_loop(0, nbits, index_step, jnp.zeros((rows, 1), I32))

    sel = jnp.logical_or(sc > thr, jnp.logical_and(sc == thr, idx <= j_ref[...]))
    return jnp.where(jnp.logical_and(sel, adm), 0.0, NEG_BIG)


ATTN_CHUNK = 256


def _attn_prompt_block(n_chunks, q0, top_k, qt_ref, qit_ref, wit_ref, kb_ref, kib_ref, vbt_ref, o_ref,
                       key_ref, bias_ref, lg_ref, j_ref):
    tq, ch = Q_BLOCK, ATTN_CHUNK
    seq = key_ref.shape[0]
    kf = float(top_k)
    kpos = lax.broadcasted_iota(I32, (ch, tq), 0)
    qpos = q0 + lax.broadcasted_iota(I32, (ch, tq), 1)

    def rows(c):
        return slice(c * ch, (c + 1) * ch)

    def fold(x, op):
        return op(x.reshape(ch // SUBLANES, SUBLANES, tq), axis=0)

    def head(x, h):
        return x[:, h * tq:(h + 1) * tq]

    qit = qit_ref[...]
    wit = wit_ref[...]
    for c in range(n_chunks):
        d = jnp.dot(kib_ref[rows(c), :], qit, preferred_element_type=F32)
        s = wit[0:1, :] * jnp.maximum(head(d, 0), 0.0)
        for h in range(1, IDX_HEADS):
            s = s + wit[h:h + 1, :] * jnp.maximum(head(d, h), 0.0)
        key_ref[rows(c), :] = jnp.where(c * ch + kpos <= qpos, s, -jnp.inf)

    def count(pred):
        acc = jnp.zeros((SUBLANES, tq), F32)
        for c in range(n_chunks):
            acc = acc + fold(pred(key_ref[rows(c), :], c).astype(F32), jnp.sum)
        return jnp.sum(acc, axis=0, keepdims=True)

    def value_step(i, t_u):
        cand_u = t_u | jnp.left_shift(jnp.int32(1), 31 - i)
        cand = _float_of_rank(cand_u)
        return jnp.where(count(lambda k, c: k >= cand) >= kf, cand_u, t_u)

    few = qpos[0:1, :] + 1 < top_k
    thr = jnp.where(few, -jnp.inf, _float_of_rank(lax.fori_loop(0, 32, value_step, jnp.zeros((1, tq), I32))))
    cnt_gt = count(lambda k, c: k > thr)
    cnt_eq = count(lambda k, c: k == thr)
    need = kf - cnt_gt
    cut_needed = jnp.logical_and(cnt_gt + cnt_eq > kf, jnp.logical_not(few))
    any_cut = jnp.max(cut_needed.astype(F32)) > 0.0

    nbits = int(np.ceil(np.log2(seq)))
    j_ref[...] = jnp.full(j_ref.shape, seq, I32)

    @pl.when(any_cut)
    def _():
        def index_step(i, j):
            cand = j | jnp.left_shift(jnp.int32(1), nbits - 1 - i)
            n_before = count(lambda k, c: jnp.logical_and(k == thr, c * ch + kpos < cand))
            return jnp.where(n_before < need, cand, j)

        j = lax.fori_loop(0, nbits, index_step, jnp.zeros((1, tq), I32))
        j_ref[...] = jnp.broadcast_to(j, j_ref.shape)

    j_cut = j_ref[0:1, :]
    for c in range(n_chunks):
        k = key_ref[rows(c), :]
        pos = c * ch + kpos
        sel = jnp.logical_or(k > thr, jnp.logical_and(k == thr, pos <= j_cut))
        bias_ref[rows(c), :] = jnp.where(jnp.logical_and(sel, pos <= qpos), 0.0, NEG_BIG)

    qt = qt_ref[...]
    mx = [jnp.full((SUBLANES, tq), -jnp.inf, F32) for _ in range(N_HEADS)]
    for c in range(n_chunks):
        lg = jnp.dot(kb_ref[rows(c), :], qt, preferred_element_type=F32)
        bias = bias_ref[rows(c), :]
        for h in range(N_HEADS):
            lgh = head(lg, h) + bias
            lg_ref[h, rows(c), :] = lgh
            mx[h] = jnp.maximum(mx[h], fold(lgh, jnp.max))

    outs = []
    for h in range(N_HEADS):
        m = jnp.max(mx[h], axis=0, keepdims=True)
        lsum = jnp.zeros((SUBLANES, tq), F32)
        ot = jnp.zeros((HEAD_DIM, tq), F32)
        for c in range(n_chunks):
            p = jnp.exp(lg_ref[h, rows(c), :] - m)
            lsum = lsum + fold(p, jnp.sum)
            ot = ot + jnp.dot(vbt_ref[:, rows(c)], p.astype(BF16), preferred_element_type=F32)
        outs.append(ot / jnp.sum(lsum, axis=0, keepdims=True))
    o_ref[...] = jnp.concatenate(outs, axis=0).T.astype(BF16)


def _attn_prompt_kernel(qt_ref, qit_ref, wit_ref, kb_ref, kib_ref, vbt_ref, o_ref, key_ref, bias_ref, lg_ref, j_ref,
                        *, top_k):
    jq = pl.program_id(1)
    blocks_per_chunk = ATTN_CHUNK // Q_BLOCK
    n_classes = key_ref.shape[0] // ATTN_CHUNK
    for cls in range(n_classes):
        @pl.when(jq // blocks_per_chunk == cls)
        def _(cls=cls):
            _attn_prompt_block(cls + 1, jq * Q_BLOCK, top_k, qt_ref, qit_ref, wit_ref, kb_ref, kib_ref, vbt_ref,
                               o_ref, key_ref, bias_ref, lg_ref, j_ref)


def _attn_prompt(qt, qit, wit, kb, kib, vbt):
    batch, _, seq = vbt.shape
    nb = seq // Q_BLOCK
    top_k = min(TOP_K_MAX, seq // 4)
    slab = lambda heads: pl.BlockSpec((None, HEAD_DIM, heads * Q_BLOCK), lambda b, j: (b * nb + j, 0, 0))
    keys = pl.BlockSpec((seq, HEAD_DIM), lambda b, j: (b, 0))
    return pl.pallas_call(
        functools.partial(_attn_prompt_kernel, top_k=top_k),
        grid=(batch, nb),
        in_specs=[slab(N_HEADS), slab(IDX_HEADS), pl.BlockSpec((None, SUBLANES, Q_BLOCK), lambda b, j: (b, 0, j)),
                  keys, keys, pl.BlockSpec((None, HEAD_DIM, seq), lambda b, j: (b, 0, 0))],
        out_specs=pl.BlockSpec((Q_BLOCK, ATT_WIDTH), lambda b, j: (b * nb + j, 0)),
        out_shape=jax.ShapeDtypeStruct((batch * seq, ATT_WIDTH), BF16),
        scratch_shapes=[pltpu.VMEM((seq, Q_BLOCK), F32), pltpu.VMEM((seq, Q_BLOCK), F32),
                        pltpu.VMEM((N_HEADS, seq, Q_BLOCK), F32), pltpu.VMEM((SUBLANES, Q_BLOCK), I32)],
        compiler_params=_params(("parallel", "arbitrary")),
        name="attn_prompt",
    )(qt, qit, wit, kb, kib, vbt)


SAMPLE_CHUNK = 1024


def _attn_sample_kernel(pt_ref, q_ref, qi_ref, wi_ref, kn_ref, vn_ref, kin_ref, ck_hbm, cv_hbm, cki_hbm, o_ref,
                        kbuf, vbuf, kibuf, sem, key_scr, bias_scr, lg_scr, j_scr, *, n_pages, page, t_new, top_k):
    b = pl.program_id(0)
    n_b = pl.num_programs(0)
    slot = b % 2
    past = n_pages * page
    lc = past + page
    n_chunks = past // SAMPLE_CHUNK

    def page_copies(bb, sl, p):
        phys = pt_ref[bb * n_pages + p]
        dst = pl.ds(pl.multiple_of(p * page, page), page)
        return [pltpu.make_async_copy(src.at[phys], buf.at[sl, :, dst], sem.at[i, sl])
                for i, (src, buf) in enumerate(((ck_hbm, kbuf), (cv_hbm, vbuf), (cki_hbm, kibuf)))]

    def start_batch(bb, sl):
        def body(p, carry):
            for cp in page_copies(bb, sl, p):
                cp.start()
            return carry
        lax.fori_loop(0, n_pages, body, 0)

    def wait_batch(bb, sl):
        def body(p, carry):
            for cp in page_copies(bb, sl, p):
                cp.wait()
            return carry
        lax.fori_loop(0, n_pages, body, 0)

    @pl.when(b == 0)
    def _():
        start_batch(0, 0)

    @pl.when(b + 1 < n_b)
    def _():
        start_batch(b + 1, 1 - slot)

    wait_batch(b, slot)

    def head_sum(d):
        r = wi_ref[...] * jnp.maximum(d, 0.0)
        s = r[0:t_new]
        for h in range(1, IDX_HEADS):
            s = s + r[h * t_new:(h + 1) * t_new]
        return s

    def new_rows(ref):
        pad = jnp.zeros((page - t_new, ref.shape[1]), F32)
        return jnp.concatenate([ref[...], pad], axis=0).astype(BF16)

    qi = qi_ref[...]
    for c in range(n_chunks):
        sl = slice(c * SAMPLE_CHUNK, (c + 1) * SAMPLE_CHUNK)
        d = jnp.dot(qi, kibuf[slot, :, sl].astype(BF16), preferred_element_type=F32)
        key_scr[:, sl] = head_sum(d)
    d_new = lax.dot_general(qi, new_rows(kin_ref), NT_DIMS, preferred_element_type=F32)
    adm_new = lax.broadcasted_iota(I32, (t_new, page), 1) <= lax.broadcasted_iota(I32, (t_new, page), 0)
    key_scr[:, past:lc] = jnp.where(adm_new, head_sum(d_new), -jnp.inf)

    idx = lax.broadcasted_iota(I32, (t_new, lc), 1)
    trow = lax.broadcasted_iota(I32, (t_new, lc), 0)
    n_adm = past + 1 + lax.broadcasted_iota(I32, (t_new, 1), 0)
    bias_scr[...] = _topk_bias(key_scr, j_scr, idx - past <= trow, n_adm, lc, top_k)

    q = q_ref[...]

    def bias_rows(sl):
        return jnp.concatenate([bias_scr[:, sl]] * N_HEADS, axis=0)

    m = jnp.full((N_HEADS * t_new, 1), -jnp.inf, F32)
    for c in range(n_chunks):
        sl = slice(c * SAMPLE_CHUNK, (c + 1) * SAMPLE_CHUNK)
        lg = jnp.dot(q, kbuf[slot, :, sl].astype(BF16), preferred_element_type=F32) + bias_rows(sl)
        lg_scr[:, sl] = lg
        m = jnp.maximum(m, jnp.max(lg, axis=1, keepdims=True))
    lg_new = lax.dot_general(q, new_rows(kn_ref), NT_DIMS, preferred_element_type=F32) + bias_rows(slice(past, lc))
    m = jnp.maximum(m, jnp.max(lg_new, axis=1, keepdims=True))

    p_new = jnp.exp(lg_new - m)
    l = jnp.sum(p_new, axis=1, keepdims=True)
    o = jnp.dot(p_new.astype(BF16), new_rows(vn_ref), preferred_element_type=F32)
    for c in range(n_chunks):
        sl = slice(c * SAMPLE_CHUNK, (c + 1) * SAMPLE_CHUNK)
        pr = jnp.exp(lg_scr[:, sl] - m)
        l = l + jnp.sum(pr, axis=1, keepdims=True)
        o = o + lax.dot_general(pr.astype(BF16), vbuf[slot, :, sl].astype(BF16), NT_DIMS,
                                preferred_element_type=F32)
    o_ref[...] = o / l


def _attn_sample(page_table, q_hq, qi_hq, wi_hq, k_new, v_new, ki_new, cache_kt, cache_vt, cache_kit):
    db, n_pages = page_table.shape
    page = cache_kt.shape[2]
    t_new = k_new.shape[1]
    past = n_pages * page
    lc = past + page
    top_k = min(TOP_K_MAX, (past + t_new) // 4)
    per_b = lambda r, w: pl.BlockSpec((None, r, w), lambda b, pt: (b, 0, 0))
    hbm = pl.BlockSpec(memory_space=pl.ANY)
    kern = functools.partial(_attn_sample_kernel, n_pages=n_pages, page=page, t_new=t_new, top_k=top_k)
    slab = pltpu.VMEM((2, HEAD_DIM, past), F32)
    grid_spec = pltpu.PrefetchScalarGridSpec(
        num_scalar_prefetch=1,
        grid=(db,),
        in_specs=[per_b(N_HEADS * t_new, HEAD_DIM), per_b(IDX_HEADS * t_new, IDX_DIM), per_b(IDX_HEADS * t_new, 1),
                  per_b(t_new, HEAD_DIM), per_b(t_new, HEAD_DIM), per_b(t_new, IDX_DIM),
                  hbm, hbm, hbm],
        out_specs=per_b(N_HEADS * t_new, HEAD_DIM),
        scratch_shapes=[slab, slab, slab, pltpu.SemaphoreType.DMA((3, 2)),
                        pltpu.VMEM((t_new, lc), F32), pltpu.VMEM((t_new, lc), F32),
                        pltpu.VMEM((N_HEADS * t_new, past), F32), pltpu.VMEM((t_new, 1), I32)],
    )
    return pl.pallas_call(
        kern,
        grid_spec=grid_spec,
        out_shape=jax.ShapeDtypeStruct((db, N_HEADS * t_new, HEAD_DIM), F32),
        compiler_params=_params(("arbitrary",)),
        name="attn_sample",
    )(page_table.reshape(-1), q_hq, qi_hq, wi_hq, k_new, v_new, ki_new, cache_kt, cache_vt, cache_kit)


PREV_ROWS = 16


def _pool_kernel(prev_ref, u_ref, wg_ref, sc_ref, o_ref, ext_ref, *, pos0):
    t_len = u_ref.shape[0]
    ext_ref[0:PREV_ROWS, :] = prev_ref[...]
    ext_ref[PREV_ROWS:PREV_ROWS + t_len, :] = u_ref[...]
    pos = pos0 + lax.broadcasted_iota(I32, (t_len, 1), 0)
    for g, w in enumerate(POOL_WINDOWS):
        sl = slice(g * POOL_GW, (g + 1) * POOL_GW)
        u_new = ext_ref[PREV_ROWS:PREV_ROWS + t_len, sl]
        win = u_new
        for back in range(1, w):
            win = win + ext_ref[PREV_ROWS - back:PREV_ROWS - back + t_len, sl]
        count = jnp.minimum(pos + 1, w).astype(F32)
        r = win / count - u_new
        mixed = jnp.dot(r.astype(BF16), wg_ref[g], preferred_element_type=F32) * sc_ref[:, sl]
        o_ref[:, sl] = mixed.astype(BF16)


def _pool(prev, u, w_grp, scale, pos0):
    nb, t_len, _ = u.shape
    return pl.pallas_call(
        functools.partial(_pool_kernel, pos0=pos0),
        grid=(nb,),
        in_specs=[pl.BlockSpec((None, PREV_ROWS, POOL_WIDTH), lambda b: (b, 0, 0)),
                  pl.BlockSpec((None, t_len, POOL_WIDTH), lambda b: (b, 0, 0)),
                  pl.BlockSpec((POOL_GROUPS, POOL_GW, POOL_GW), lambda b: (0, 0, 0)),
                  pl.BlockSpec((1, POOL_WIDTH), lambda b: (0, 0))],
        out_specs=pl.BlockSpec((None, t_len, POOL_WIDTH), lambda b: (b, 0, 0)),
        out_shape=jax.ShapeDtypeStruct((nb, t_len, POOL_WIDTH), BF16),
        scratch_shapes=[pltpu.VMEM((PREV_ROWS + t_len, POOL_WIDTH), F32)],
        compiler_params=_params(("parallel",)),
        name="pool",
    )(prev, u, w_grp, scale)


def _merge_kernel(x_ref, a_ref, p_ref, wga_ref, wgb_ref, wao_ref, wpo_ref, wo_ref, g_ref, b_ref, h_ref, hp_ref, *,
                  alpha):
    x = x_ref[...]
    xb = x.astype(BF16)
    ga = jnp.dot(xb, wga_ref[...], preferred_element_type=F32)
    gb = jnp.dot(xb, wgb_ref[...], preferred_element_type=F32)
    ya = jnp.dot(a_ref[...], wao_ref[...], preferred_element_type=F32)
    yp = jnp.dot(p_ref[...], wpo_ref[...], preferred_element_type=F32)
    mix = jax.nn.sigmoid(ga) * ya + jax.nn.sigmoid(gb) * yp
    out = jnp.dot(mix.astype(BF16), wo_ref[...], preferred_element_type=F32)
    h = _layer_norm(alpha * x + out, g_ref[...], b_ref[...])
    h_ref[...] = h
    hp_ref[...] = _pack_rows(h)


def _merge(x, attn, pool, wga, wgb, wao, wpo, wo, g, b, tm, alpha):
    n = x.shape[0]
    row = lambda w: pl.BlockSpec((tm, w), lambda i: (i, 0))
    full = lambda r, c: pl.BlockSpec((r, c), lambda i: (0, 0))
    return pl.pallas_call(
        functools.partial(_merge_kernel, alpha=alpha),
        grid=(n // tm,),
        in_specs=[row(D_MODEL), row(ATT_WIDTH), row(POOL_WIDTH), full(D_MODEL, D_MODEL), full(D_MODEL, D_MODEL),
                  full(ATT_WIDTH, D_MODEL), full(POOL_WIDTH, D_MODEL), full(D_MODEL, D_MODEL),
                  full(1, D_MODEL), full(1, D_MODEL)],
        out_specs=(row(D_MODEL), row(PACKED)),
        out_shape=(jax.ShapeDtypeStruct((n, D_MODEL), F32), jax.ShapeDtypeStruct((n, PACKED), I32)),
        compiler_params=_params(("parallel",)),
        name="merge",
    )(x, attn, pool, wga, wgb, wao, wpo, wo, g, b)


def _route(h, wr_t, bias_col):
    tm = h.shape[0]
    logits = lax.dot_general(wr_t, h.astype(BF16), NT_DIMS, preferred_element_type=F32)
    s = jax.nn.sigmoid(logits)
    sb = s + bias_col
    neg_inf = -jnp.inf

    rows = []
    for g in range(N_GROUPS):
        blk = sb[g * GROUP_SIZE:(g + 1) * GROUP_SIZE, :]
        m1 = jnp.max(blk, axis=0, keepdims=True)
        is_m1 = blk == m1
        n_m1 = jnp.sum(is_m1.astype(F32), axis=0, keepdims=True)
        m2 = jnp.max(jnp.where(is_m1, neg_inf, blk), axis=0, keepdims=True)
        rows.append(m1 + jnp.where(n_m1 >= 2.0, m1, m2))
    gs = jnp.concatenate(rows, axis=0)

    gi = lax.broadcasted_iota(I32, (N_GROUPS, tm), 0)
    rank = jnp.zeros((N_GROUPS, tm), F32)
    for g in range(N_GROUPS):
        row = gs[g:g + 1, :]
        beats = jnp.logical_or(row > gs, jnp.logical_and(row == gs, g < gi))
        rank = rank + beats.astype(F32)
    gkeep = rank < float(TOPK_GROUPS)
    emask = jnp.concatenate(
        [jnp.broadcast_to(gkeep[g:g + 1, :], (GROUP_SIZE, tm)) for g in range(N_GROUPS)], axis=0)

    ei = lax.broadcasted_iota(I32, (N_EXPERTS, tm), 0)
    x = jnp.where(emask, sb, neg_inf)
    sel = jnp.zeros((N_EXPERTS, tm), jnp.bool_)
    picks = []
    for _ in range(TOP_K_EXPERTS):
        m = jnp.max(x, axis=0, keepdims=True)
        first = jnp.min(jnp.where(x == m, ei, N_EXPERTS), axis=0, keepdims=True)
        pick = ei == first
        sel = jnp.logical_or(sel, pick)
        x = jnp.where(pick, neg_inf, x)
        picks.append(first)

    gate = jnp.where(sel, s, 0.0)
    comb = gate / jnp.sum(gate, axis=0, keepdims=True) * ROUTED_SCALE
    return comb, sel, picks


def _router_kernel(h_ref, wr_ref, bias_ref, c_ref):
    comb, _, _ = _route(h_ref[...], wr_ref[...], bias_ref[...])
    comb = jnp.concatenate([comb, jnp.zeros((LANES - N_EXPERTS, comb.shape[1]), F32)], axis=0)
    c_ref[...] = comb.T


def _router(h, wr_t, bias_col, tm):
    n = h.shape[0]
    return pl.pallas_call(
        _router_kernel,
        grid=(n // tm,),
        in_specs=[pl.BlockSpec((tm, D_MODEL), lambda i: (i, 0)),
                  pl.BlockSpec((N_EXPERTS, D_MODEL), lambda i: (0, 0)),
                  pl.BlockSpec((N_EXPERTS, 1), lambda i: (0, 0))],
        out_specs=pl.BlockSpec((tm, LANES), lambda i: (i, 0)),
        out_shape=jax.ShapeDtypeStruct((n, LANES), F32),
        compiler_params=_params(("parallel",)),
        name="router",
    )(h, wr_t, bias_col)


def _swiglu(xb, w13, w2, hidden):
    ab = jnp.dot(xb, w13, preferred_element_type=F32)
    act = jax.nn.silu(ab[:, 0:hidden]) * ab[:, hidden:2 * hidden]
    return jnp.dot(act.astype(BF16), w2, preferred_element_type=F32)


def _moe_kernel(h_ref, c_ref, ws13_ref, ws2_ref, w13_ref, w2_ref, y_ref, hb_ref):
    e = pl.program_id(1)

    @pl.when(e == 0)
    def _():
        hb_ref[...] = h_ref[...].astype(BF16)
        y_ref[...] = _swiglu(hb_ref[...], ws13_ref[...], ws2_ref[...], SHARED_DIM)

    ye = _swiglu(hb_ref[...], w13_ref[...], w2_ref[...], EXPERT_DIM)
    lane = lax.broadcasted_iota(I32, c_ref.shape, 1)
    ce = jnp.sum(jnp.where(lane == e, c_ref[...], 0.0), axis=1, keepdims=True)
    y_ref[...] += ce * ye


def _moe(h, comb, ws13, ws2, w13, w2, tm):
    n = h.shape[0]
    return pl.pallas_call(
        _moe_kernel,
        grid=(n // tm, N_EXPERTS),
        in_specs=[pl.BlockSpec((tm, D_MODEL), lambda i, e: (i, 0)),
                  pl.BlockSpec((tm, LANES), lambda i, e: (i, 0)),
                  pl.BlockSpec((D_MODEL, 2 * SHARED_DIM), lambda i, e: (0, 0)),
                  pl.BlockSpec((SHARED_DIM, D_MODEL), lambda i, e: (0, 0)),
                  pl.BlockSpec((None, D_MODEL, 2 * EXPERT_DIM), lambda i, e: (e, 0, 0)),
                  pl.BlockSpec((None, EXPERT_DIM, D_MODEL), lambda i, e: (e, 0, 0))],
        out_specs=pl.BlockSpec((tm, D_MODEL), lambda i, e: (i, 0)),
        out_shape=jax.ShapeDtypeStruct((n, D_MODEL), F32),
        scratch_shapes=[pltpu.VMEM((tm, D_MODEL), BF16)],
        compiler_params=_params(("parallel", "arbitrary")),
        name="moe",
    )(h, comb, ws13, ws2, w13, w2)


def _final_kernel(h_ref, y_ref, pe_ref, g_ref, b_ref, wpg_ref, wpi_ref, o_ref, *, alpha):
    z = _layer_norm(alpha * h_ref[...] + y_ref[...], g_ref[...], b_ref[...])
    gate = jax.nn.sigmoid(jnp.dot(z.astype(BF16), wpg_ref[...], preferred_element_type=F32))
    emb = jnp.dot(pe_ref[...].astype(BF16), wpi_ref[...], preferred_element_type=F32)
    o_ref[...] = z + gate * emb


def _final(h, y, pe, g, b, wpg, wpi, tm, alpha):
    n = h.shape[0]
    row = lambda w: pl.BlockSpec((tm, w), lambda i: (i, 0))
    full = lambda r, c: pl.BlockSpec((r, c), lambda i: (0, 0))
    return pl.pallas_call(
        functools.partial(_final_kernel, alpha=alpha),
        grid=(n // tm,),
        in_specs=[row(D_MODEL), row(D_MODEL), row(PLE_DIM), full(1, D_MODEL), full(1, D_MODEL),
                  full(D_MODEL, D_MODEL), full(PLE_DIM, D_MODEL)],
        out_specs=row(D_MODEL),
        out_shape=jax.ShapeDtypeStruct((n, D_MODEL), F32),
        compiler_params=_params(("parallel",)),
        name="final",
    )(h, y, pe, g, b, wpg, wpi)


MOE_BLOCK = 512


def _sorted_rows(n_tokens):
    worst = n_tokens * TOP_K_EXPERTS + N_EXPERTS * (MOE_BLOCK - 1)
    return -(-worst // MOE_BLOCK) * MOE_BLOCK


def _dispatch_kernel(h_ref, wr_ref, bias_ref, tri_ref, pos_ref, gate_ref, blk_ref, used_ref,
                     eidx_s, rank_s, gate_s, cnt_s):
    p = pl.program_id(0)
    i = pl.program_id(1)
    tm = h_ref.shape[0]
    ei = lax.broadcasted_iota(I32, (N_EXPERTS, tm), 0)

    @pl.when(p == 0)
    def _():
        comb, sel, picks = _route(h_ref[...], wr_ref[...], bias_ref[...])
        before = jnp.dot(sel.astype(BF16), tri_ref[...], preferred_element_type=F32)
        ranks, gates = [], []
        for first in picks:
            pick = ei == first
            ranks.append(jnp.sum(jnp.where(pick, before, 0.0), axis=0, keepdims=True))
            gates.append(jnp.sum(jnp.where(pick, comb, 0.0), axis=0, keepdims=True))
        eidx_s[i] = jnp.concatenate(picks, axis=0)
        rank_s[i] = jnp.concatenate(ranks, axis=0)
        gate_s[i] = jnp.concatenate(gates, axis=0)
        cnt_s[i] = jnp.broadcast_to(jnp.sum(sel.astype(F32), axis=1, keepdims=True), (N_EXPERTS, LANES))

    @pl.when(p == 1)
    def _():
        cnt = cnt_s[...]
        tile_id = lax.broadcasted_iota(I32, cnt.shape, 0)
        total = jnp.sum(cnt, axis=0)
        prior = jnp.sum(jnp.where(tile_id < i, cnt, 0.0), axis=0)
        seg = jnp.ceil(total * (1.0 / MOE_BLOCK)) * MOE_BLOCK
        lower = (lax.broadcasted_iota(I32, (N_EXPERTS, N_EXPERTS), 1)
                 < lax.broadcasted_iota(I32, (N_EXPERTS, N_EXPERTS), 0)).astype(F32)
        seg_off = jnp.dot(lower, seg, precision=lax.Precision.HIGHEST, preferred_element_type=F32)
        base = (seg_off + prior)[:, 0:1]
        eidx = eidx_s[i]
        rank = rank_s[i]
        rows = []
        for k in range(TOP_K_EXPERTS):
            pick = ei == eidx[k:k + 1, :]
            rows.append(rank[k:k + 1, :] + jnp.sum(jnp.where(pick, base, 0.0), axis=0, keepdims=True))
        pos_ref[...] = jnp.concatenate(rows, axis=0).astype(I32)
        gate_ref[...] = jnp.concatenate([gate_s[i], jnp.zeros((LANES - TOP_K_EXPERTS, tm), F32)], axis=0).T

        seg_end = (seg_off + seg)[:, 0:1]
        n_blk = blk_ref.shape[1]
        blk_start = (lax.broadcasted_iota(I32, (N_EXPERTS, n_blk), 1) * MOE_BLOCK).astype(F32)
        owner = jnp.sum((seg_end <= blk_start).astype(F32), axis=0, keepdims=True)
        blk_ref[...] = jnp.minimum(owner, N_EXPERTS - 1.0).astype(I32)
        used = seg_end[N_EXPERTS - 1:N_EXPERTS, :] * (1.0 / MOE_BLOCK)
        used_ref[...] = jnp.broadcast_to(used, used_ref.shape).astype(I32)


def _dispatch(h, wr_t, bias_col, tm):
    n = h.shape[0]
    n_tiles = n // tm
    n_blk = _sorted_rows(n) // MOE_BLOCK
    n_blk_pad = -(-n_blk // LANES) * LANES
    tri = jnp.triu(jnp.ones((tm, tm), BF16), k=1)
    const = lambda r, c: pl.BlockSpec((r, c), lambda p, i: (0, 0))
    per_tile = lambda dt: pltpu.VMEM((n_tiles, TOP_K_EXPERTS, tm), dt)
    return pl.pallas_call(
        _dispatch_kernel,
        grid=(2, n_tiles),
        in_specs=[pl.BlockSpec((tm, D_MODEL), lambda p, i: (i * (1 - p), 0)),
                  const(N_EXPERTS, D_MODEL), const(N_EXPERTS, 1), const(tm, tm)],
        out_specs=(pl.BlockSpec((TOP_K_EXPERTS, tm), lambda p, i: (0, i * p)),
                   pl.BlockSpec((tm, LANES), lambda p, i: (i * p, 0)),
                   const(1, n_blk_pad), const(1, LANES)),
        out_shape=(jax.ShapeDtypeStruct((TOP_K_EXPERTS, n), I32), jax.ShapeDtypeStruct((n, LANES), F32),
                   jax.ShapeDtypeStruct((1, n_blk_pad), I32), jax.ShapeDtypeStruct((1, LANES), I32)),
        scratch_shapes=[per_tile(I32), per_tile(F32), per_tile(F32), pltpu.VMEM((n_tiles, N_EXPERTS, LANES), F32)],
        compiler_params=_params(("arbitrary", "arbitrary")),
        name="dispatch",
    )(h, wr_t, bias_col, tri)


PACKED = D_MODEL // 2


def _pack_rows(x):
    lo = pltpu.bitcast(x[:, 0:PACKED].astype(BF16).astype(F32), I32)
    hi = pltpu.bitcast(x[:, PACKED:D_MODEL].astype(BF16).astype(F32), I32)
    return jnp.bitwise_or(hi, lax.shift_right_logical(lo, 16))


def _unpack_rows(w):
    lo = pltpu.bitcast(lax.shift_left(w, 16), F32)
    hi = pltpu.bitcast(jnp.bitwise_and(w, -65536), F32)
    return jnp.concatenate([lo, hi], axis=1).astype(BF16)


def _grouped_kernel(blk_ref, used_ref, xs_ref, w13_ref, w2_ref, ys_ref):
    b = pl.program_id(0)

    @pl.when(b < used_ref[0])
    def _():
        ys = _swiglu(_unpack_rows(xs_ref[...]), w13_ref[...].astype(BF16), w2_ref[...].astype(BF16), EXPERT_DIM)
        ys_ref[...] = _pack_rows(ys)

    @pl.when(b >= used_ref[0])
    def _():
        ys_ref[...] = jnp.zeros(ys_ref.shape, I32)


def _grouped(blk, used, xs, w13, w2):
    ns = xs.shape[0]
    grid_spec = pltpu.PrefetchScalarGridSpec(
        num_scalar_prefetch=2,
        grid=(ns // MOE_BLOCK,),
        in_specs=[pl.BlockSpec((MOE_BLOCK, PACKED), lambda b, blk, used: (b, 0)),
                  pl.BlockSpec((None, D_MODEL, 2 * EXPERT_DIM), lambda b, blk, used: (blk[b], 0, 0)),
                  pl.BlockSpec((None, EXPERT_DIM, D_MODEL), lambda b, blk, used: (blk[b], 0, 0))],
        out_specs=pl.BlockSpec((MOE_BLOCK, PACKED), lambda b, blk, used: (b, 0)),
    )
    return pl.pallas_call(
        _grouped_kernel,
        grid_spec=grid_spec,
        out_shape=jax.ShapeDtypeStruct((ns, PACKED), I32),
        compiler_params=_params(("arbitrary",)),
        name="grouped",
    )(blk, used, xs, w13, w2)


SC_WINDOW = 128


def _sc_mesh():
    return plsc.VectorSubcoreMesh(core_axis_name="core", subcore_axis_name="subcore")


def _sc_worker(n_items):
    info = plsc.get_sparse_core_info()
    n_workers = info.num_cores * info.num_subcores
    wid = lax.axis_index("subcore") * info.num_cores + lax.axis_index("core")
    return wid, n_items // (SC_WINDOW * n_workers)


def _scatter_rows(x, pos, n_out):
    n, width = x.shape
    picks = pos.shape[0]

    @functools.partial(
        pl.kernel, mesh=_sc_mesh(), out_type=jax.ShapeDtypeStruct((n_out, width), I32),
        scratch_types=[pltpu.VMEM((picks, SC_WINDOW), I32), pltpu.VMEM((SC_WINDOW, width), I32)],
        name="scatter_rows")
    def scatter(x_hbm, pos_hbm, out_hbm, idx_v, rows_v):
        wid, n_win = _sc_worker(n)

        @pl.loop(0, n_win)
        def _(j):
            base = (wid * n_win + j) * SC_WINDOW
            pltpu.sync_copy(pos_hbm.at[:, pl.ds(base, SC_WINDOW)], idx_v)
            pltpu.sync_copy(x_hbm.at[pl.ds(base, SC_WINDOW)], rows_v)
            for k in range(picks):
                pltpu.sync_copy(rows_v, out_hbm.at[idx_v.at[k]])

    return scatter(x, pos)


def _gather_rows(src, pos):
    width = src.shape[1]
    picks, n = pos.shape

    @functools.partial(
        pl.kernel, mesh=_sc_mesh(), out_type=jax.ShapeDtypeStruct((picks * n, width), I32),
        scratch_types=[pltpu.VMEM((SC_WINDOW,), I32), pltpu.VMEM((SC_WINDOW, width), I32)],
        name="gather_rows")
    def gather(src_hbm, pos_hbm, out_hbm, idx_v, rows_v):
        wid, n_win = _sc_worker(picks * n)

        @pl.loop(0, n_win)
        def _(j):
            base = (wid * n_win + j) * SC_WINDOW
            pltpu.sync_copy(pos_hbm.at[pl.ds(base, SC_WINDOW)], idx_v)
            pltpu.sync_copy(src_hbm.at[idx_v], rows_v)
            pltpu.sync_copy(rows_v, out_hbm.at[pl.ds(base, SC_WINDOW)])

    return gather(src, pos.reshape(-1)).reshape(picks, n, width)


def _combine_kernel(h_ref, g_ref, gate_ref, pe_ref, ws13_ref, ws2_ref, ln_g_ref, ln_b_ref, wpg_ref, wpi_ref, o_ref, *,
                    alpha):
    h = h_ref[...]
    y = _swiglu(h.astype(BF16), ws13_ref[...], ws2_ref[...], SHARED_DIM)
    gate = gate_ref[...]
    for k in range(TOP_K_EXPERTS):
        y = y + gate[:, k:k + 1] * _unpack_rows(g_ref[k]).astype(F32)
    z = _layer_norm(alpha * h + y, ln_g_ref[...], ln_b_ref[...])
    ple_gate = jax.nn.sigmoid(jnp.dot(z.astype(BF16), wpg_ref[...], preferred_element_type=F32))
    emb = jnp.dot(pe_ref[...].astype(BF16), wpi_ref[...], preferred_element_type=F32)
    o_ref[...] = z + ple_gate * emb


def _combine(h, gathered, gate, pe, ws13, ws2, g, b, wpg, wpi, tm, alpha):
    n = h.shape[0]
    row = lambda w: pl.BlockSpec((tm, w), lambda i: (i, 0))
    full = lambda r, c: pl.BlockSpec((r, c), lambda i: (0, 0))
    return pl.pallas_call(
        functools.partial(_combine_kernel, alpha=alpha),
        grid=(n // tm,),
        in_specs=[row(D_MODEL), pl.BlockSpec((TOP_K_EXPERTS, tm, PACKED), lambda i: (0, i, 0)), row(LANES),
                  row(PLE_DIM), full(D_MODEL, 2 * SHARED_DIM), full(SHARED_DIM, D_MODEL),
                  full(1, D_MODEL), full(1, D_MODEL), full(D_MODEL, D_MODEL), full(PLE_DIM, D_MODEL)],
        out_specs=row(D_MODEL),
        out_shape=jax.ShapeDtypeStruct((n, D_MODEL), F32),
        compiler_params=_params(("parallel",)),
        name="combine",
    )(h, gathered, gate, pe, ws13, ws2, g, b, wpg, wpi)


def _rope_table(pos):
    inv = ROPE_THETA ** (-jnp.arange(0, HEAD_DIM, 2, dtype=F32) / HEAD_DIM)
    ang = pos.astype(F32)[:, None] * inv[None, :]
    return jnp.concatenate([jnp.tile(jnp.cos(ang), (1, 4)), jnp.tile(jnp.sin(ang), (1, 4))], axis=1)


def _rotate_half_cols(w, n_heads):
    w3 = w.reshape(w.shape[0], n_heads, HEAD_DIM)
    half = HEAD_DIM // 2
    return jnp.concatenate([-w3[..., half:], w3[..., :half]], axis=-1).reshape(w.shape)


def _fused_in_weight(w_in):
    offs = np.cumsum(IN_SIZES)[:-1].tolist()
    wq, wk, wv, wqi, wki, wwi, wu, wga, wgb = jnp.split(w_in, offs, axis=1)
    pad = jnp.zeros((D_MODEL, LANES - HEAD_DIM - IDX_HEADS), w_in.dtype)
    w_big = jnp.concatenate(
        [wq, _rotate_half_cols(wq, N_HEADS), wqi, _rotate_half_cols(wqi, IDX_HEADS),
         wk, wki, _rotate_half_cols(wk, 1), _rotate_half_cols(wki, 1), wv, wwi, pad, wu], axis=1).astype(BF16)
    return w_big, w_big[:, 0:C_U].T, wga.astype(BF16), wgb.astype(BF16)


def _pages_transposed(cache):
    return jnp.transpose(cache[0], (0, 2, 1))


def _heads_major(a, n_heads):
    b, t, w = a.shape
    d = w // n_heads
    return a.reshape(b, t, n_heads, d).transpose(0, 2, 1, 3).reshape(b, n_heads * t, d)


def kernel(x_prompt, x_sample, cache_k, cache_v, cache_kidx, state_pool, page_table, p_prompt, p_sample, w_in, w_att_out, w_pool_grp, pool_scale, w_pool_out, w_out, ln1_g, ln1_b, w_router, router_bias, w_exp13, w_exp2, w_sh13, w_sh2, ln2_g, ln2_b, w_ple_in, w_ple_gate):
    B, S, D = x_prompt.shape
    DB, T, _ = x_sample.shape
    depth = w_in.shape[0]
    assert depth == 1, "single layer step"
    page = cache_k.shape[2]
    past = page_table.shape[1] * page
    alpha = (2 * depth) ** 0.25
    n_p, n_s = B * S, DB * T

    w_big, w_t, wga, wgb = _fused_in_weight(w_in[0])
    wao, wpo, wo = w_att_out[0].astype(BF16), w_pool_out[0].astype(BF16), w_out[0].astype(BF16)
    wgrp = w_pool_grp[0].astype(BF16)
    pscale = pool_scale[0].reshape(1, POOL_WIDTH)
    g1, b1 = ln1_g[0].reshape(1, D), ln1_b[0].reshape(1, D)
    g2, b2 = ln2_g[0].reshape(1, D), ln2_b[0].reshape(1, D)
    wr_t = w_router[0].T.astype(BF16)
    rbias = router_bias[0].reshape(N_EXPERTS, 1)
    w13, w2 = w_exp13[0].astype(BF16), w_exp2[0].astype(BF16)
    ws13, ws2 = w_sh13[0].astype(BF16), w_sh2[0].astype(BF16)
    wpg, wpi = w_ple_gate[0].astype(BF16), w_ple_in[0].astype(BF16)

    cs_p = _rope_table(jnp.arange(S, dtype=I32))
    cs_s = jnp.tile(_rope_table(past + jnp.arange(T, dtype=I32)), (DB, 1))

    xp = x_prompt.reshape(n_p, D)
    qt, qit, wit, kb, kib, vbt, kt, vt, kit, u = _proj_prompt(xp, w_big, w_t, cs_p, S, 512)
    attn_p = _attn_prompt(qt, qit, wit, kb, kib, vbt)
    u3 = u.reshape(B, S, POOL_WIDTH)
    pool_p = _pool(jnp.zeros((B, PREV_ROWS, POOL_WIDTH), F32), u3, wgrp, pscale, 0).reshape(n_p, POOL_WIDTH)
    h_p, hp_p = _merge(xp, attn_p, pool_p, wga, wgb, wao, wpo, wo, g1, b1, 512, alpha)

    xs = x_sample.reshape(n_s, D)
    qs, qis, ks, vs, kis, wis, us = _proj_sample(xs, w_big, cs_s)
    q_hq = _heads_major(qs.reshape(DB, T, ATT_WIDTH), N_HEADS)
    qi_hq = _heads_major(qis.reshape(DB, T, IDX_HEADS * IDX_DIM), IDX_HEADS)
    wi_hq = wis.reshape(DB, T, IDX_HEADS).transpose(0, 2, 1).reshape(DB, IDX_HEADS * T, 1)
    o_hq = _attn_sample(page_table, q_hq, qi_hq, wi_hq, ks.reshape(DB, T, HEAD_DIM), vs.reshape(DB, T, HEAD_DIM),
                        kis.reshape(DB, T, IDX_DIM), _pages_transposed(cache_k), _pages_transposed(cache_v),
                        _pages_transposed(cache_kidx))
    attn_s = o_hq.reshape(DB, N_HEADS, T, HEAD_DIM).transpose(0, 2, 1, 3).reshape(n_s, ATT_WIDTH).astype(BF16)
    us3 = us.reshape(DB, T, POOL_WIDTH)
    prev_s = jnp.concatenate([jnp.zeros((DB, PREV_ROWS - POOL_STATE, POOL_WIDTH), F32), state_pool[0]], axis=1)
    pool_s = _pool(prev_s, us3, wgrp, pscale, past).reshape(n_s, POOL_WIDTH)
    h_s, _ = _merge(xs, attn_s, pool_s, wga, wgb, wao, wpo, wo, g1, b1, n_s, alpha)

    def tail(h, pe, tm_r, tm_m, tm_f):
        comb = _router(h, wr_t, rbias, tm_r)
        y = _moe(h, comb, ws13, ws2, w13, w2, tm_m)
        return _final(h, y, pe, g2, b2, wpg, wpi, tm_f, alpha)

    y_s = tail(h_s, p_sample[0].reshape(n_s, PLE_DIM), n_s, n_s, n_s)

    pos, gate, blk, used = _dispatch(h_p, wr_t, rbias, 1024)
    sorted_in = _scatter_rows(hp_p, pos, _sorted_rows(n_p))
    sorted_out = _grouped(blk.reshape(-1), used.reshape(-1), sorted_in, w_exp13[0], w_exp2[0])
    gathered = _gather_rows(sorted_out, pos)
    y_p = _combine(h_p, gathered, gate, p_prompt[0].reshape(n_p, PLE_DIM), ws13, ws2, g2, b2, wpg, wpi, 512, alpha)

    ext_s = jnp.concatenate([state_pool[0], us3], axis=1)
    return (y_p.reshape(B, S, D), y_s.reshape(DB, T, D),
            jnp.transpose(kt, (0, 2, 1))[None], jnp.transpose(vt, (0, 2, 1))[None],
            jnp.transpose(kit, (0, 2, 1))[None],
            u3[:, S - POOL_STATE:][None],
            ks.reshape(1, DB, T, HEAD_DIM), vs.reshape(1, DB, T, HEAD_DIM), kis.reshape(1, DB, T, IDX_DIM),
            ext_s[:, T:][None])
```

```python
import functools

import numpy as np
import jax
import jax.numpy as jnp
from jax import lax
from jax.experimental import pallas as pl
from jax.experimental.pallas import tpu as pltpu
from jax.experimental.pallas import tpu_sc as plsc

F32 = jnp.float32
BF16 = jnp.bfloat16
I32 = jnp.int32

D_MODEL = 1024
N_HEADS = 8
HEAD_DIM = 64
ATT_WIDTH = N_HEADS * HEAD_DIM
IDX_HEADS = 4
IDX_DIM = 64
TOP_K_MAX = 256
Q_BLOCK = 128
ROPE_THETA = 10000.0
POOL_WINDOWS = (2, 4, 8, 16)
POOL_GROUPS = 4
POOL_WIDTH = 512
POOL_GW = POOL_WIDTH // POOL_GROUPS
POOL_STATE = 15
N_EXPERTS = 64
TOP_K_EXPERTS = 8
N_GROUPS = 8
GROUP_SIZE = N_EXPERTS // N_GROUPS
TOPK_GROUPS = 4
EXPERT_DIM = 256
SHARED_DIM = 256
ROUTED_SCALE = 2.5
PLE_DIM = 256
LN_EPS = 1e-5
IN_SIZES = (ATT_WIDTH, HEAD_DIM, HEAD_DIM, IDX_HEADS * IDX_DIM, IDX_DIM, IDX_HEADS, POOL_WIDTH, D_MODEL, D_MODEL)

LANES = 128
SUBLANES = 8
INT_MIN = -2147483648
NEG_BIG = -1e30
VMEM_LIMIT = 56 * 1024 * 1024

C_Q, C_QR = 0, 512
C_QI, C_QIR = 1024, 1280
C_KK, C_KKR = 1536, 1664
C_VW = 1792
C_U = 1920
C_END = 2432

NT_DIMS = (((1,), (1,)), ((), ()))

Q_SCALE = HEAD_DIM ** -0.5 * float(np.log2(np.e))
QI_SCALE = IDX_DIM ** -0.5


def _params(sem):
    return pltpu.CompilerParams(dimension_semantics=sem, vmem_limit_bytes=VMEM_LIMIT)


def _layer_norm(x, g, b):
    mu = jnp.mean(x, axis=-1, keepdims=True)
    xc = x - mu
    var = jnp.mean(xc * xc, axis=-1, keepdims=True)
    return xc * lax.rsqrt(var + LN_EPS) * g + b


def _proj_sample_kernel(x_ref, w_ref, cs_ref, q_ref, qi_ref, k_ref, v_ref, ki_ref, wi_ref, u_ref):
    xb = x_ref[...].astype(BF16)
    cos = cs_ref[:, 0:LANES]
    sin = cs_ref[:, LANES:2 * LANES]

    def mm(c0, n):
        return jnp.dot(xb, w_ref[:, c0:c0 + n], preferred_element_type=F32)

    def rope(c0, c0r, n):
        reps = n // LANES
        cosn = jnp.concatenate([cos] * reps, axis=1) if reps > 1 else cos
        sinn = jnp.concatenate([sin] * reps, axis=1) if reps > 1 else sin
        return mm(c0, n) * cosn + mm(c0r, n) * sinn

    q_ref[...] = (rope(C_Q, C_QR, ATT_WIDTH) * Q_SCALE).astype(BF16)
    qi_ref[...] = (rope(C_QI, C_QIR, IDX_HEADS * IDX_DIM) * QI_SCALE).astype(BF16)
    kk = rope(C_KK, C_KKR, LANES)
    k_ref[...] = kk[:, 0:HEAD_DIM]
    ki_ref[...] = kk[:, HEAD_DIM:2 * HEAD_DIM]
    vw = mm(C_VW, LANES)
    v_ref[...] = vw[:, 0:HEAD_DIM]
    wi_ref[...] = vw[:, HEAD_DIM:HEAD_DIM + IDX_HEADS] * (IDX_HEADS ** -0.5)
    u_ref[...] = mm(C_U, POOL_WIDTH)


def _proj_sample(x, w_big, cs):
    n = x.shape[0]
    full = lambda r, c: pl.BlockSpec((r, c), lambda i: (0, 0))
    widths = (ATT_WIDTH, IDX_HEADS * IDX_DIM, HEAD_DIM, HEAD_DIM, IDX_DIM, IDX_HEADS, POOL_WIDTH)
    dtypes = (BF16, BF16, F32, F32, F32, F32, F32)
    return pl.pallas_call(
        _proj_sample_kernel,
        grid=(1,),
        in_specs=[full(n, D_MODEL), full(D_MODEL, C_END), full(n, 2 * LANES)],
        out_specs=tuple(full(n, w) for w in widths),
        out_shape=tuple(jax.ShapeDtypeStruct((n, w), dt) for w, dt in zip(widths, dtypes)),
        compiler_params=_params(("arbitrary",)),
        name="proj_sample",
    )(x, w_big, cs)


def _proj_prompt_kernel(x_ref, w_ref, wt_ref, cs_ref, cst_ref, qt_ref, qit_ref, wit_ref, kb_ref, kib_ref, vbt_ref,
                        kt_ref, vt_ref, kit_ref, u_ref):
    xb = x_ref[...].astype(BF16)
    tm = xb.shape[0]
    cos = cs_ref[:, 0:LANES]
    sin = cs_ref[:, LANES:2 * LANES]
    cos_t = cst_ref[0:HEAD_DIM, :]
    sin_t = cst_ref[LANES:LANES + HEAD_DIM, :]

    def mm(c0, n):
        return jnp.dot(xb, w_ref[:, c0:c0 + n], preferred_element_type=F32)

    def mm_t(c0, n):
        return lax.dot_general(wt_ref[c0:c0 + n, :], xb, NT_DIMS, preferred_element_type=F32)

    def rope_t(c0, c0r, heads):
        cosn = jnp.concatenate([cos_t] * heads, axis=0) if heads > 1 else cos_t
        sinn = jnp.concatenate([sin_t] * heads, axis=0) if heads > 1 else sin_t
        return mm_t(c0, heads * HEAD_DIM) * cosn + mm_t(c0r, heads * HEAD_DIM) * sinn

    kk = mm(C_KK, LANES) * cos + mm(C_KKR, LANES) * sin
    kb_ref[...] = kk[:, 0:HEAD_DIM].astype(BF16)
    kib_ref[...] = kk[:, HEAD_DIM:2 * HEAD_DIM].astype(BF16)
    u_ref[...] = mm(C_U, POOL_WIDTH)

    qt = (rope_t(C_Q, C_QR, N_HEADS) * Q_SCALE).astype(BF16)
    qit = (rope_t(C_QI, C_QIR, IDX_HEADS) * QI_SCALE).astype(BF16)
    for blk in range(tm // Q_BLOCK):
        cols = slice(blk * Q_BLOCK, (blk + 1) * Q_BLOCK)
        for h in range(N_HEADS):
            qt_ref[blk, :, h * Q_BLOCK:(h + 1) * Q_BLOCK] = qt[h * HEAD_DIM:(h + 1) * HEAD_DIM, cols]
        for h in range(IDX_HEADS):
            qit_ref[blk, :, h * Q_BLOCK:(h + 1) * Q_BLOCK] = qit[h * IDX_DIM:(h + 1) * IDX_DIM, cols]

    kkt = rope_t(C_KK, C_KKR, 2)
    kt_ref[...] = kkt[0:HEAD_DIM, :]
    kit_ref[...] = kkt[HEAD_DIM:2 * HEAD_DIM, :]
    vwt = mm_t(C_VW, LANES)
    vt_ref[...] = vwt[0:HEAD_DIM, :]
    vbt_ref[...] = vwt[0:HEAD_DIM, :].astype(BF16)
    wit_ref[...] = vwt[HEAD_DIM:HEAD_DIM + SUBLANES, :] * (IDX_HEADS ** -0.5)


def _proj_prompt(x, w_big, w_t, cs, seq, tm):
    n = x.shape[0]
    nb = seq // tm
    qb = tm // Q_BLOCK
    row = lambda w: pl.BlockSpec((tm, w), lambda i: (i, 0))
    col = lambda r: pl.BlockSpec((None, r, tm), lambda i: (i // nb, 0, i % nb))
    slab = lambda heads: pl.BlockSpec((qb, HEAD_DIM, heads * Q_BLOCK), lambda i: (i, 0, 0))
    pm = lambda r, dt: jax.ShapeDtypeStruct((n // seq, r, seq), dt)
    out_shape = (
        jax.ShapeDtypeStruct((n // Q_BLOCK, HEAD_DIM, N_HEADS * Q_BLOCK), BF16),
        jax.ShapeDtypeStruct((n // Q_BLOCK, IDX_DIM, IDX_HEADS * Q_BLOCK), BF16),
        pm(SUBLANES, F32),
        jax.ShapeDtypeStruct((n, HEAD_DIM), BF16), jax.ShapeDtypeStruct((n, IDX_DIM), BF16),
        pm(HEAD_DIM, BF16),
        pm(HEAD_DIM, F32), pm(HEAD_DIM, F32), pm(IDX_DIM, F32),
        jax.ShapeDtypeStruct((n, POOL_WIDTH), F32),
    )
    return pl.pallas_call(
        _proj_prompt_kernel,
        grid=(n // tm,),
        in_specs=[
            row(D_MODEL),
            pl.BlockSpec((D_MODEL, C_END), lambda i: (0, 0)),
            pl.BlockSpec((C_U, D_MODEL), lambda i: (0, 0)),
            pl.BlockSpec((tm, 2 * LANES), lambda i: (i % nb, 0)),
            pl.BlockSpec((2 * LANES, tm), lambda i: (0, i % nb)),
        ],
        out_specs=(slab(N_HEADS), slab(IDX_HEADS), col(SUBLANES), row(HEAD_DIM), row(IDX_DIM), col(HEAD_DIM),
                   col(HEAD_DIM), col(HEAD_DIM), col(IDX_DIM), row(POOL_WIDTH)),
        out_shape=out_shape,
        compiler_params=_params(("parallel",)),
        name="proj_prompt",
    )(x, w_big, w_t, cs, cs.T)


def _float_of_rank(u):
    key = u ^ INT_MIN
    bits = jnp.where(key < 0, INT_MIN - key, key)
    return pltpu.bitcast(bits, F32)


def _count(mask):
    return jnp.sum(mask.astype(F32), axis=1, keepdims=True)


def _topk_bias(sc_ref, j_ref, adm, n_adm, lc, k):
    rows = sc_ref.shape[0]
    kf = float(k)

    def value_step(i, t_u):
        hi = jnp.left_shift(jnp.int32(1), 31 - 2 * i)
        lo = jnp.left_shift(jnp.int32(1), 30 - 2 * i)
        for cand_u in (t_u | lo, t_u | hi, t_u | hi | lo):
            cnt = _count(sc_ref[:, 0:lc] >= _float_of_rank(cand_u))
            t_u = jnp.where(cnt >= kf, cand_u, t_u)
        return t_u

    t_u = lax.fori_loop(0, 16, value_step, jnp.zeros((rows, 1), I32))
    few = n_adm < k
    thr = jnp.where(few, -jnp.inf, _float_of_rank(t_u))
    sc = sc_ref[:, 0:lc]
    cnt_gt = _count(sc > thr)
    cnt_eq = _count(sc == thr)
    need = kf - cnt_gt
    cut_needed = jnp.logical_and(cnt_gt + cnt_eq > kf, jnp.logical_not(few))
    any_cut = jnp.max(cut_needed.astype(F32)) > 0.0
    idx = lax.broadcasted_iota(I32, (rows, lc), 1)
    nbits = int(np.ceil(np.log2(lc)))

    j_ref[...] = jnp.full((rows, 1), lc, I32)

    @pl.when(any_cut)
    def _():
        def index_step(i, j):
            cand = j | jnp.left_shift(jnp.int32(1), nbits - 1 - i)
            c = _count(jnp.logical_and(sc_ref[:, 0:lc] == thr, idx < cand))
            return jnp.where(c < need, cand, j)

        j_ref[...] = lax.fori_loop(0, nbits, index_step, jnp.zeros((rows, 1), I32))

    sel = jnp.logical_or(sc > thr, jnp.logical_and(sc == thr, idx <= j_ref[...]))
    return jnp.where(jnp.logical_and(sel, adm), 0.0, NEG_BIG)


ATTN_CHUNK = 256


def _attn_prompt_block(n_chunks, q0, top_k, qt_ref, qit_ref, wit_ref, kb_ref, kib_ref, vbt_ref, o_ref,
                       key_ref, bias_ref, lg_ref, j_ref):
    tq, ch = Q_BLOCK, ATTN_CHUNK
    seq = key_ref.shape[0]
    kf = float(top_k)
    kpos = lax.broadcasted_iota(I32, (ch, tq), 0)
    qpos = q0 + lax.broadcasted_iota(I32, (ch, tq), 1)

    def rows(c):
        return slice(c * ch, (c + 1) * ch)

    def fold(x, op):
        return op(x.reshape(ch // SUBLANES, SUBLANES, tq), axis=0)

    def head(x, h):
        return x[:, h * tq:(h + 1) * tq]

    qit = qit_ref[...]
    wit = wit_ref[...]
    for c in range(n_chunks):
        d = jnp.dot(kib_ref[rows(c), :], qit, preferred_element_type=F32)
        s = wit[0:1, :] * jnp.maximum(head(d, 0), 0.0)
        for h in range(1, IDX_HEADS):
            s = s + wit[h:h + 1, :] * jnp.maximum(head(d, h), 0.0)
        key_ref[rows(c), :] = jnp.where(c * ch + kpos <= qpos, s, -jnp.inf)

    def count(pred):
        acc = jnp.zeros((SUBLANES, tq), F32)
        for c in range(n_chunks):
            acc = acc + fold(pred(key_ref[rows(c), :], c).astype(F32), jnp.sum)
        return jnp.sum(acc, axis=0, keepdims=True)

    def value_step(i, t_u):
        cand_u = t_u | jnp.left_shift(jnp.int32(1), 31 - i)
        cand = _float_of_rank(cand_u)
        return jnp.where(count(lambda k, c: k >= cand) >= kf, cand_u, t_u)

    few = qpos[0:1, :] + 1 < top_k
    thr = jnp.where(few, -jnp.inf, _float_of_rank(lax.fori_loop(0, 32, value_step, jnp.zeros((1, tq), I32))))
    cnt_gt = count(lambda k, c: k > thr)
    cnt_eq = count(lambda k, c: k == thr)
    need = kf - cnt_gt
    cut_needed = jnp.logical_and(cnt_gt + cnt_eq > kf, jnp.logical_not(few))
    any_cut = jnp.max(cut_needed.astype(F32)) > 0.0

    nbits = int(np.ceil(np.log2(seq)))
    j_ref[...] = jnp.full(j_ref.shape, seq, I32)

    @pl.when(any_cut)
    def _():
        def index_step(i, j):
            cand = j | jnp.left_shift(jnp.int32(1), nbits - 1 - i)
            n_before = count(lambda k, c: jnp.logical_and(k == thr, c * ch + kpos < cand))
            return jnp.where(n_before < need, cand, j)

        j = lax.fori_loop(0, nbits, index_step, jnp.zeros((1, tq), I32))
        j_ref[...] = jnp.broadcast_to(j, j_ref.shape)

    j_cut = j_ref[0:1, :]
    for c in range(n_chunks):
        k = key_ref[rows(c), :]
        pos = c * ch + kpos
        sel = jnp.logical_or(k > thr, jnp.logical_and(k == thr, pos <= j_cut))
        bias_ref[rows(c), :] = jnp.where(jnp.logical_and(sel, pos <= qpos), 0.0, NEG_BIG)

    qt = qt_ref[...]
    mx = [jnp.full((SUBLANES, tq), -jnp.inf, F32) for _ in range(N_HEADS)]
    for c in range(n_chunks):
        lg = jnp.dot(kb_ref[rows(c), :], qt, preferred_element_type=F32)
        bias = bias_ref[rows(c), :]
        for h in range(N_HEADS):
            lgh = head(lg, h) + bias
            lg_ref[h, rows(c), :] = lgh
            mx[h] = jnp.maximum(mx[h], fold(lgh, jnp.max))

    outs = []
    for h in range(N_HEADS):
        m = jnp.max(mx[h], axis=0, keepdims=True)
        lsum = jnp.zeros((SUBLANES, tq), F32)
        ot = jnp.zeros((HEAD_DIM, tq), F32)
        for c in range(n_chunks):
            p = jnp.exp2(lg_ref[h, rows(c), :] - m)
            lsum = lsum + fold(p, jnp.sum)
            ot = ot + jnp.dot(vbt_ref[:, rows(c)], p.astype(BF16), preferred_element_type=F32)
        outs.append(ot / jnp.sum(lsum, axis=0, keepdims=True))
    o_ref[...] = jnp.concatenate(outs, axis=0).T.astype(BF16)


def _attn_prompt_kernel(qt_ref, qit_ref, wit_ref, kb_ref, kib_ref, vbt_ref, o_ref, key_ref, bias_ref, lg_ref, j_ref,
                        *, top_k):
    jq = pl.program_id(1)
    blocks_per_chunk = ATTN_CHUNK // Q_BLOCK
    n_classes = key_ref.shape[0] // ATTN_CHUNK
    for cls in range(n_classes):
        @pl.when(jq // blocks_per_chunk == cls)
        def _(cls=cls):
            _attn_prompt_block(cls + 1, jq * Q_BLOCK, top_k, qt_ref, qit_ref, wit_ref, kb_ref, kib_ref, vbt_ref,
                               o_ref, key_ref, bias_ref, lg_ref, j_ref)


def _attn_prompt(qt, qit, wit, kb, kib, vbt):
    batch, _, seq = vbt.shape
    nb = seq // Q_BLOCK
    top_k = min(TOP_K_MAX, seq // 4)
    slab = lambda heads: pl.BlockSpec((None, HEAD_DIM, heads * Q_BLOCK), lambda b, j: (b * nb + j, 0, 0))
    keys = pl.BlockSpec((seq, HEAD_DIM), lambda b, j: (b, 0))
    return pl.pallas_call(
        functools.partial(_attn_prompt_kernel, top_k=top_k),
        grid=(batch, nb),
        in_specs=[slab(N_HEADS), slab(IDX_HEADS), pl.BlockSpec((None, SUBLANES, Q_BLOCK), lambda b, j: (b, 0, j)),
                  keys, keys, pl.BlockSpec((None, HEAD_DIM, seq), lambda b, j: (b, 0, 0))],
        out_specs=pl.BlockSpec((Q_BLOCK, ATT_WIDTH), lambda b, j: (b * nb + j, 0)),
        out_shape=jax.ShapeDtypeStruct((batch * seq, ATT_WIDTH), BF16),
        scratch_shapes=[pltpu.VMEM((seq, Q_BLOCK), F32), pltpu.VMEM((seq, Q_BLOCK), F32),
                        pltpu.VMEM((N_HEADS, seq, Q_BLOCK), F32), pltpu.VMEM((SUBLANES, Q_BLOCK), I32)],
        compiler_params=_params(("parallel", "arbitrary")),
        name="attn_prompt",
    )(qt, qit, wit, kb, kib, vbt)


SAMPLE_CHUNK = 1024


def _attn_sample_kernel(pt_ref, q_ref, qi_ref, wi_ref, kn_ref, vn_ref, kin_ref, ck_hbm, cv_hbm, cki_hbm, o_ref,
                        kbuf, vbuf, kibuf, sem, key_scr, bias_scr, lg_scr, j_scr, *, n_pages, page, t_new, top_k):
    b = pl.program_id(0)
    n_b = pl.num_programs(0)
    slot = b % 2
    past = n_pages * page
    lc = past + page
    n_chunks = past // SAMPLE_CHUNK

    def page_copies(bb, sl, p):
        phys = pt_ref[bb * n_pages + p]
        dst = pl.ds(pl.multiple_of(p * page, page), page)
        return [pltpu.make_async_copy(src.at[phys], buf.at[sl, :, dst], sem.at[i, sl])
                for i, (src, buf) in enumerate(((ck_hbm, kbuf), (cv_hbm, vbuf), (cki_hbm, kibuf)))]

    def start_batch(bb, sl):
        def body(p, carry):
            for cp in page_copies(bb, sl, p):
                cp.start()
            return carry
        lax.fori_loop(0, n_pages, body, 0)

    def wait_batch(bb, sl):
        def body(p, carry):
            for cp in page_copies(bb, sl, p):
                cp.wait()
            return carry
        lax.fori_loop(0, n_pages, body, 0)

    @pl.when(b == 0)
    def _():
        start_batch(0, 0)

    @pl.when(b + 1 < n_b)
    def _():
        start_batch(b + 1, 1 - slot)

    wait_batch(b, slot)

    def head_sum(d):
        r = wi_ref[...] * jnp.maximum(d, 0.0)
        s = r[0:t_new]
        for h in range(1, IDX_HEADS):
            s = s + r[h * t_new:(h + 1) * t_new]
        return s

    def new_rows(ref):
        pad = jnp.zeros((page - t_new, ref.shape[1]), F32)
        return jnp.concatenate([ref[...], pad], axis=0).astype(BF16)

    qi = qi_ref[...]
    for c in range(n_chunks):
        sl = slice(c * SAMPLE_CHUNK, (c + 1) * SAMPLE_CHUNK)
        d = jnp.dot(qi, kibuf[slot, :, sl].astype(BF16), preferred_element_type=F32)
        key_scr[:, sl] = head_sum(d)
    d_new = lax.dot_general(qi, new_rows(kin_ref), NT_DIMS, preferred_element_type=F32)
    adm_new = lax.broadcasted_iota(I32, (t_new, page), 1) <= lax.broadcasted_iota(I32, (t_new, page), 0)
    key_scr[:, past:lc] = jnp.where(adm_new, head_sum(d_new), -jnp.inf)

    idx = lax.broadcasted_iota(I32, (t_new, lc), 1)
    trow = lax.broadcasted_iota(I32, (t_new, lc), 0)
    n_adm = past + 1 + lax.broadcasted_iota(I32, (t_new, 1), 0)
    bias_scr[...] = _topk_bias(key_scr, j_scr, idx - past <= trow, n_adm, lc, top_k)

    q = q_ref[...]

    def bias_rows(sl):
        return jnp.concatenate([bias_scr[:, sl]] * N_HEADS, axis=0)

    m = jnp.full((N_HEADS * t_new, 1), -jnp.inf, F32)
    for c in range(n_chunks):
        sl = slice(c * SAMPLE_CHUNK, (c + 1) * SAMPLE_CHUNK)
        lg = jnp.dot(q, kbuf[slot, :, sl].astype(BF16), preferred_element_type=F32) + bias_rows(sl)
        lg_scr[:, sl] = lg
        m = jnp.maximum(m, jnp.max(lg, axis=1, keepdims=True))
    lg_new = lax.dot_general(q, new_rows(kn_ref), NT_DIMS, preferred_element_type=F32) + bias_rows(slice(past, lc))
    m = jnp.maximum(m, jnp.max(lg_new, axis=1, keepdims=True))

    p_new = jnp.exp2(lg_new - m)
    l = jnp.sum(p_new, axis=1, keepdims=True)
    o = jnp.dot(p_new.astype(BF16), new_rows(vn_ref), preferred_element_type=F32)
    for c in range(n_chunks):
        sl = slice(c * SAMPLE_CHUNK, (c + 1) * SAMPLE_CHUNK)
        pr = jnp.exp2(lg_scr[:, sl] - m)
        l = l + jnp.sum(pr, axis=1, keepdims=True)
        o = o + lax.dot_general(pr.astype(BF16), vbuf[slot, :, sl].astype(BF16), NT_DIMS,
                                preferred_element_type=F32)
    o_ref[...] = o / l


def _attn_sample(page_table, q_hq, qi_hq, wi_hq, k_new, v_new, ki_new, cache_kt, cache_vt, cache_kit):
    db, n_pages = page_table.shape
    page = cache_kt.shape[2]
    t_new = k_new.shape[1]
    past = n_pages * page
    lc = past + page
    top_k = min(TOP_K_MAX, (past + t_new) // 4)
    per_b = lambda r, w: pl.BlockSpec((None, r, w), lambda b, pt: (b, 0, 0))
    hbm = pl.BlockSpec(memory_space=pl.ANY)
    kern = functools.partial(_attn_sample_kernel, n_pages=n_pages, page=page, t_new=t_new, top_k=top_k)
    slab = pltpu.VMEM((2, HEAD_DIM, past), F32)
    grid_spec = pltpu.PrefetchScalarGridSpec(
        num_scalar_prefetch=1,
        grid=(db,),
        in_specs=[per_b(N_HEADS * t_new, HEAD_DIM), per_b(IDX_HEADS * t_new, IDX_DIM), per_b(IDX_HEADS * t_new, 1),
                  per_b(t_new, HEAD_DIM), per_b(t_new, HEAD_DIM), per_b(t_new, IDX_DIM),
                  hbm, hbm, hbm],
        out_specs=per_b(N_HEADS * t_new, HEAD_DIM),
        scratch_shapes=[slab, slab, slab, pltpu.SemaphoreType.DMA((3, 2)),
                        pltpu.VMEM((t_new, lc), F32), pltpu.VMEM((t_new, lc), F32),
                        pltpu.VMEM((N_HEADS * t_new, past), F32), pltpu.VMEM((t_new, 1), I32)],
    )
    return pl.pallas_call(
        kern,
        grid_spec=grid_spec,
        out_shape=jax.ShapeDtypeStruct((db, N_HEADS * t_new, HEAD_DIM), F32),
        compiler_params=_params(("arbitrary",)),
        name="attn_sample",
    )(page_table.reshape(-1), q_hq, qi_hq, wi_hq, k_new, v_new, ki_new, cache_kt, cache_vt, cache_kit)


PREV_ROWS = 16


def _pool_kernel(prev_ref, u_ref, wg_ref, sc_ref, o_ref, ext_ref, *, pos0):
    t_len = u_ref.shape[0]
    ext_ref[0:PREV_ROWS, :] = prev_ref[...]
    ext_ref[PREV_ROWS:PREV_ROWS + t_len, :] = u_ref[...]
    pos = pos0 + lax.broadcasted_iota(I32, (t_len, 1), 0)
    for g, w in enumerate(POOL_WINDOWS):
        sl = slice(g * POOL_GW, (g + 1) * POOL_GW)
        u_new = ext_ref[PREV_ROWS:PREV_ROWS + t_len, sl]
        win = u_new
        for back in range(1, w):
            win = win + ext_ref[PREV_ROWS - back:PREV_ROWS - back + t_len, sl]
        count = jnp.minimum(pos + 1, w).astype(F32)
        r = win / count - u_new
        mixed = jnp.dot(r.astype(BF16), wg_ref[g], preferred_element_type=F32) * sc_ref[:, sl]
        o_ref[:, sl] = mixed.astype(BF16)


def _pool(prev, u, w_grp, scale, pos0):
    nb, t_len, _ = u.shape
    return pl.pallas_call(
        functools.partial(_pool_kernel, pos0=pos0),
        grid=(nb,),
        in_specs=[pl.BlockSpec((None, PREV_ROWS, POOL_WIDTH), lambda b: (b, 0, 0)),
                  pl.BlockSpec((None, t_len, POOL_WIDTH), lambda b: (b, 0, 0)),
                  pl.BlockSpec((POOL_GROUPS, POOL_GW, POOL_GW), lambda b: (0, 0, 0)),
                  pl.BlockSpec((1, POOL_WIDTH), lambda b: (0, 0))],
        out_specs=pl.BlockSpec((None, t_len, POOL_WIDTH), lambda b: (b, 0, 0)),
        out_shape=jax.ShapeDtypeStruct((nb, t_len, POOL_WIDTH), BF16),
        scratch_shapes=[pltpu.VMEM((PREV_ROWS + t_len, POOL_WIDTH), F32)],
        compiler_params=_params(("parallel",)),
        name="pool",
    )(prev, u, w_grp, scale)


def _merge_kernel(x_ref, a_ref, p_ref, wga_ref, wgb_ref, wao_ref, wpo_ref, wo_ref, g_ref, b_ref, h_ref, hp_ref, *,
                  alpha):
    x = x_ref[...]
    xb = x.astype(BF16)
    ga = jnp.dot(xb, wga_ref[...], preferred_element_type=F32)
    gb = jnp.dot(xb, wgb_ref[...], preferred_element_type=F32)
    ya = jnp.dot(a_ref[...], wao_ref[...], preferred_element_type=F32)
    yp = jnp.dot(p_ref[...], wpo_ref[...], preferred_element_type=F32)
    mix = jax.nn.sigmoid(ga) * ya + jax.nn.sigmoid(gb) * yp
    out = jnp.dot(mix.astype(BF16), wo_ref[...], preferred_element_type=F32)
    h = _layer_norm(alpha * x + out, g_ref[...], b_ref[...])
    h_ref[...] = h
    hp_ref[...] = _pack_rows(h)


def _merge(x, attn, pool, wga, wgb, wao, wpo, wo, g, b, tm, alpha):
    n = x.shape[0]
    row = lambda w: pl.BlockSpec((tm, w), lambda i: (i, 0))
    full = lambda r, c: pl.BlockSpec((r, c), lambda i: (0, 0))
    return pl.pallas_call(
        functools.partial(_merge_kernel, alpha=alpha),
        grid=(n // tm,),
        in_specs=[row(D_MODEL), row(ATT_WIDTH), row(POOL_WIDTH), full(D_MODEL, D_MODEL), full(D_MODEL, D_MODEL),
                  full(ATT_WIDTH, D_MODEL), full(POOL_WIDTH, D_MODEL), full(D_MODEL, D_MODEL),
                  full(1, D_MODEL), full(1, D_MODEL)],
        out_specs=(row(D_MODEL), row(PACKED)),
        out_shape=(jax.ShapeDtypeStruct((n, D_MODEL), F32), jax.ShapeDtypeStruct((n, PACKED), I32)),
        compiler_params=_params(("parallel",)),
        name="merge",
    )(x, attn, pool, wga, wgb, wao, wpo, wo, g, b)


def _route(h, wr_t, bias_col):
    tm = h.shape[0]
    logits = lax.dot_general(wr_t, h.astype(BF16), NT_DIMS, preferred_element_type=F32)
    s = jax.nn.sigmoid(logits)
    sb = s + bias_col
    neg_inf = -jnp.inf

    rows = []
    for g in range(N_GROUPS):
        blk = sb[g * GROUP_SIZE:(g + 1) * GROUP_SIZE, :]
        m1 = jnp.max(blk, axis=0, keepdims=True)
        is_m1 = blk == m1
        n_m1 = jnp.sum(is_m1.astype(F32), axis=0, keepdims=True)
        m2 = jnp.max(jnp.where(is_m1, neg_inf, blk), axis=0, keepdims=True)
        rows.append(m1 + jnp.where(n_m1 >= 2.0, m1, m2))
    gs = jnp.concatenate(rows, axis=0)

    gi = lax.broadcasted_iota(I32, (N_GROUPS, tm), 0)
    rank = jnp.zeros((N_GROUPS, tm), F32)
    for g in range(N_GROUPS):
        row = gs[g:g + 1, :]
        beats = jnp.logical_or(row > gs, jnp.logical_and(row == gs, g < gi))
        rank = rank + beats.astype(F32)
    gkeep = rank < float(TOPK_GROUPS)
    emask = jnp.concatenate(
        [jnp.broadcast_to(gkeep[g:g + 1, :], (GROUP_SIZE, tm)) for g in range(N_GROUPS)], axis=0)

    ei = lax.broadcasted_iota(I32, (N_EXPERTS, tm), 0)
    x = jnp.where(emask, sb, neg_inf)
    sel = jnp.zeros((N_EXPERTS, tm), jnp.bool_)
    picks = []
    for _ in range(TOP_K_EXPERTS):
        m = jnp.max(x, axis=0, keepdims=True)
        first = jnp.min(jnp.where(x == m, ei, N_EXPERTS), axis=0, keepdims=True)
        pick = ei == first
        sel = jnp.logical_or(sel, pick)
        x = jnp.where(pick, neg_inf, x)
        picks.append(first)

    gate = jnp.where(sel, s, 0.0)
    comb = gate / jnp.sum(gate, axis=0, keepdims=True) * ROUTED_SCALE
    return comb, sel, picks


def _router_kernel(h_ref, wr_ref, bias_ref, c_ref):
    comb, _, _ = _route(h_ref[...], wr_ref[...], bias_ref[...])
    comb = jnp.concatenate([comb, jnp.zeros((LANES - N_EXPERTS, comb.shape[1]), F32)], axis=0)
    c_ref[...] = comb.T


def _router(h, wr_t, bias_col, tm):
    n = h.shape[0]
    return pl.pallas_call(
        _router_kernel,
        grid=(n // tm,),
        in_specs=[pl.BlockSpec((tm, D_MODEL), lambda i: (i, 0)),
                  pl.BlockSpec((N_EXPERTS, D_MODEL), lambda i: (0, 0)),
                  pl.BlockSpec((N_EXPERTS, 1), lambda i: (0, 0))],
        out_specs=pl.BlockSpec((tm, LANES), lambda i: (i, 0)),
        out_shape=jax.ShapeDtypeStruct((n, LANES), F32),
        compiler_params=_params(("parallel",)),
        name="router",
    )(h, wr_t, bias_col)


def _swiglu(xb, w13, w2, hidden):
    ab = jnp.dot(xb, w13, preferred_element_type=F32)
    act = jax.nn.silu(ab[:, 0:hidden]) * ab[:, hidden:2 * hidden]
    return jnp.dot(act.astype(BF16), w2, preferred_element_type=F32)


def _moe_kernel(h_ref, c_ref, ws13_ref, ws2_ref, w13_ref, w2_ref, y_ref, hb_ref):
    e = pl.program_id(1)

    @pl.when(e == 0)
    def _():
        hb_ref[...] = h_ref[...].astype(BF16)
        y_ref[...] = _swiglu(hb_ref[...], ws13_ref[...], ws2_ref[...], SHARED_DIM)

    ye = _swiglu(hb_ref[...], w13_ref[...].astype(BF16), w2_ref[...].astype(BF16), EXPERT_DIM)
    lane = lax.broadcasted_iota(I32, c_ref.shape, 1)
    ce = jnp.sum(jnp.where(lane == e, c_ref[...], 0.0), axis=1, keepdims=True)
    y_ref[...] += ce * ye


def _moe(h, comb, ws13, ws2, w13, w2, tm):
    n = h.shape[0]
    return pl.pallas_call(
        _moe_kernel,
        grid=(n // tm, N_EXPERTS),
        in_specs=[pl.BlockSpec((tm, D_MODEL), lambda i, e: (i, 0)),
                  pl.BlockSpec((tm, LANES), lambda i, e: (i, 0)),
                  pl.BlockSpec((D_MODEL, 2 * SHARED_DIM), lambda i, e: (0, 0)),
                  pl.BlockSpec((SHARED_DIM, D_MODEL), lambda i, e: (0, 0)),
                  pl.BlockSpec((None, D_MODEL, 2 * EXPERT_DIM), lambda i, e: (e, 0, 0)),
                  pl.BlockSpec((None, EXPERT_DIM, D_MODEL), lambda i, e: (e, 0, 0))],
        out_specs=pl.BlockSpec((tm, D_MODEL), lambda i, e: (i, 0)),
        out_shape=jax.ShapeDtypeStruct((n, D_MODEL), F32),
        scratch_shapes=[pltpu.VMEM((tm, D_MODEL), BF16)],
        compiler_params=_params(("parallel", "arbitrary")),
        name="moe",
    )(h, comb, ws13, ws2, w13, w2)


def _final_kernel(h_ref, y_ref, pe_ref, g_ref, b_ref, wpg_ref, wpi_ref, o_ref, *, alpha):
    z = _layer_norm(alpha * h_ref[...] + y_ref[...], g_ref[...], b_ref[...])
    gate = jax.nn.sigmoid(jnp.dot(z.astype(BF16), wpg_ref[...], preferred_element_type=F32))
    emb = jnp.dot(pe_ref[...].astype(BF16), wpi_ref[...], preferred_element_type=F32)
    o_ref[...] = z + gate * emb


def _final(h, y, pe, g, b, wpg, wpi, tm, alpha):
    n = h.shape[0]
    row = lambda w: pl.BlockSpec((tm, w), lambda i: (i, 0))
    full = lambda r, c: pl.BlockSpec((r, c), lambda i: (0, 0))
    return pl.pallas_call(
        functools.partial(_final_kernel, alpha=alpha),
        grid=(n // tm,),
        in_specs=[row(D_MODEL), row(D_MODEL), row(PLE_DIM), full(1, D_MODEL), full(1, D_MODEL),
                  full(D_MODEL, D_MODEL), full(PLE_DIM, D_MODEL)],
        out_specs=row(D_MODEL),
        out_shape=jax.ShapeDtypeStruct((n, D_MODEL), F32),
        compiler_params=_params(("parallel",)),
        name="final",
    )(h, y, pe, g, b, wpg, wpi)


MOE_BLOCK = 512


def _sorted_rows(n_tokens):
    worst = n_tokens * TOP_K_EXPERTS + N_EXPERTS * (MOE_BLOCK - 1)
    return -(-worst // MOE_BLOCK) * MOE_BLOCK


def _dispatch_kernel(h_ref, wr_ref, bias_ref, tri_ref, pos_ref, gate_ref, blk_ref, used_ref,
                     eidx_s, rank_s, gate_s, cnt_s):
    p = pl.program_id(0)
    i = pl.program_id(1)
    tm = h_ref.shape[0]
    ei = lax.broadcasted_iota(I32, (N_EXPERTS, tm), 0)

    @pl.when(p == 0)
    def _():
        comb, sel, picks = _route(h_ref[...], wr_ref[...], bias_ref[...])
        before = jnp.dot(sel.astype(BF16), tri_ref[...], preferred_element_type=F32)
        ranks, gates = [], []
        for first in picks:
            pick = ei == first
            ranks.append(jnp.sum(jnp.where(pick, before, 0.0), axis=0, keepdims=True))
            gates.append(jnp.sum(jnp.where(pick, comb, 0.0), axis=0, keepdims=True))
        eidx_s[i] = jnp.concatenate(picks, axis=0)
        rank_s[i] = jnp.concatenate(ranks, axis=0)
        gate_s[i] = jnp.concatenate(gates, axis=0)
        cnt_s[i] = jnp.broadcast_to(jnp.sum(sel.astype(F32), axis=1, keepdims=True), (N_EXPERTS, LANES))

    @pl.when(p == 1)
    def _():
        cnt = cnt_s[...]
        tile_id = lax.broadcasted_iota(I32, cnt.shape, 0)
        total = jnp.sum(cnt, axis=0)
        prior = jnp.sum(jnp.where(tile_id < i, cnt, 0.0), axis=0)
        seg = jnp.ceil(total * (1.0 / MOE_BLOCK)) * MOE_BLOCK
        lower = (lax.broadcasted_iota(I32, (N_EXPERTS, N_EXPERTS), 1)
                 < lax.broadcasted_iota(I32, (N_EXPERTS, N_EXPERTS), 0)).astype(F32)
        seg_off = jnp.dot(lower, seg, precision=lax.Precision.HIGHEST, preferred_element_type=F32)
        base = (seg_off + prior)[:, 0:1]
        eidx = eidx_s[i]
        rank = rank_s[i]
        rows = []
        for k in range(TOP_K_EXPERTS):
            pick = ei == eidx[k:k + 1, :]
            rows.append(rank[k:k + 1, :] + jnp.sum(jnp.where(pick, base, 0.0), axis=0, keepdims=True))
        pos_ref[...] = jnp.concatenate(rows, axis=0).astype(I32)
        gate_ref[...] = jnp.concatenate([gate_s[i], jnp.zeros((LANES - TOP_K_EXPERTS, tm), F32)], axis=0).T

        seg_end = (seg_off + seg)[:, 0:1]
        n_blk = blk_ref.shape[1]
        blk_start = (lax.broadcasted_iota(I32, (N_EXPERTS, n_blk), 1) * MOE_BLOCK).astype(F32)
        owner = jnp.sum((seg_end <= blk_start).astype(F32), axis=0, keepdims=True)
        blk_ref[...] = jnp.minimum(owner, N_EXPERTS - 1.0).astype(I32)
        used = seg_end[N_EXPERTS - 1:N_EXPERTS, :] * (1.0 / MOE_BLOCK)
        used_ref[...] = jnp.broadcast_to(used, used_ref.shape).astype(I32)


def _dispatch(h, wr_t, bias_col, tm):
    n = h.shape[0]
    n_tiles = n // tm
    n_blk = _sorted_rows(n) // MOE_BLOCK
    n_blk_pad = -(-n_blk // LANES) * LANES
    tri = jnp.triu(jnp.ones((tm, tm), BF16), k=1)
    const = lambda r, c: pl.BlockSpec((r, c), lambda p, i: (0, 0))
    per_tile = lambda dt: pltpu.VMEM((n_tiles, TOP_K_EXPERTS, tm), dt)
    return pl.pallas_call(
        _dispatch_kernel,
        grid=(2, n_tiles),
        in_specs=[pl.BlockSpec((tm, D_MODEL), lambda p, i: (i * (1 - p), 0)),
                  const(N_EXPERTS, D_MODEL), const(N_EXPERTS, 1), const(tm, tm)],
        out_specs=(pl.BlockSpec((TOP_K_EXPERTS, tm), lambda p, i: (0, i * p)),
                   pl.BlockSpec((tm, LANES), lambda p, i: (i * p, 0)),
                   const(1, n_blk_pad), const(1, LANES)),
        out_shape=(jax.ShapeDtypeStruct((TOP_K_EXPERTS, n), I32), jax.ShapeDtypeStruct((n, LANES), F32),
                   jax.ShapeDtypeStruct((1, n_blk_pad), I32), jax.ShapeDtypeStruct((1, LANES), I32)),
        scratch_shapes=[per_tile(I32), per_tile(F32), per_tile(F32), pltpu.VMEM((n_tiles, N_EXPERTS, LANES), F32)],
        compiler_params=_params(("arbitrary", "arbitrary")),
        name="dispatch",
    )(h, wr_t, bias_col, tri)


PACKED = D_MODEL // 2


def _pack_rows(x):
    lo = pltpu.bitcast(x[:, 0:PACKED].astype(BF16).astype(F32), I32)
    hi = pltpu.bitcast(x[:, PACKED:D_MODEL].astype(BF16).astype(F32), I32)
    return jnp.bitwise_or(hi, lax.shift_right_logical(lo, 16))


def _unpack_rows(w):
    lo = pltpu.bitcast(lax.shift_left(w, 16), F32)
    hi = pltpu.bitcast(jnp.bitwise_and(w, -65536), F32)
    return jnp.concatenate([lo, hi], axis=1).astype(BF16)


def _grouped_kernel(blk_ref, used_ref, xs_ref, w13_ref, w2_ref, ys_ref):
    b = pl.program_id(0)

    @pl.when(b < used_ref[0])
    def _():
        ys = _swiglu(_unpack_rows(xs_ref[...]), w13_ref[...].astype(BF16), w2_ref[...].astype(BF16), EXPERT_DIM)
        ys_ref[...] = _pack_rows(ys)

    @pl.when(b >= used_ref[0])
    def _():
        ys_ref[...] = jnp.zeros(ys_ref.shape, I32)


def _grouped(blk, used, xs, w13, w2):
    ns = xs.shape[0]
    grid_spec = pltpu.PrefetchScalarGridSpec(
        num_scalar_prefetch=2,
        grid=(ns // MOE_BLOCK,),
        in_specs=[pl.BlockSpec((MOE_BLOCK, PACKED), lambda b, blk, used: (b, 0)),
                  pl.BlockSpec((None, D_MODEL, 2 * EXPERT_DIM), lambda b, blk, used: (blk[b], 0, 0)),
                  pl.BlockSpec((None, EXPERT_DIM, D_MODEL), lambda b, blk, used: (blk[b], 0, 0))],
        out_specs=pl.BlockSpec((MOE_BLOCK, PACKED), lambda b, blk, used: (b, 0)),
    )
    return pl.pallas_call(
        _grouped_kernel,
        grid_spec=grid_spec,
        out_shape=jax.ShapeDtypeStruct((ns, PACKED), I32),
        compiler_params=_params(("arbitrary",)),
        name="grouped",
    )(blk, used, xs, w13, w2)


SC_WINDOW = 128


def _sc_mesh():
    return plsc.VectorSubcoreMesh(core_axis_name="core", subcore_axis_name="subcore")


def _sc_worker(n_items):
    info = plsc.get_sparse_core_info()
    n_workers = info.num_cores * info.num_subcores
    wid = lax.axis_index("subcore") * info.num_cores + lax.axis_index("core")
    return wid, n_items // (SC_WINDOW * n_workers)


def _scatter_rows(x, pos, n_out):
    n, width = x.shape
    picks = pos.shape[0]

    @functools.partial(
        pl.kernel, mesh=_sc_mesh(), out_type=jax.ShapeDtypeStruct((n_out, width), I32),
        scratch_types=[pltpu.VMEM((picks, SC_WINDOW), I32), pltpu.VMEM((SC_WINDOW, width), I32)],
        name="scatter_rows")
    def scatter(x_hbm, pos_hbm, out_hbm, idx_v, rows_v):
        wid, n_win = _sc_worker(n)

        @pl.loop(0, n_win)
        def _(j):
            base = (wid * n_win + j) * SC_WINDOW
            pltpu.sync_copy(pos_hbm.at[:, pl.ds(base, SC_WINDOW)], idx_v)
            pltpu.sync_copy(x_hbm.at[pl.ds(base, SC_WINDOW)], rows_v)
            for k in range(picks):
                pltpu.sync_copy(rows_v, out_hbm.at[idx_v.at[k]])

    return scatter(x, pos)


def _gather_rows(src, pos):
    width = src.shape[1]
    picks, n = pos.shape

    @functools.partial(
        pl.kernel, mesh=_sc_mesh(), out_type=jax.ShapeDtypeStruct((picks * n, width), I32),
        scratch_types=[pltpu.VMEM((SC_WINDOW,), I32), pltpu.VMEM((SC_WINDOW, width), I32)],
        name="gather_rows")
    def gather(src_hbm, pos_hbm, out_hbm, idx_v, rows_v):
        wid, n_win = _sc_worker(picks * n)

        @pl.loop(0, n_win)
        def _(j):
            base = (wid * n_win + j) * SC_WINDOW
            pltpu.sync_copy(pos_hbm.at[pl.ds(base, SC_WINDOW)], idx_v)
            pltpu.sync_copy(src_hbm.at[idx_v], rows_v)
            pltpu.sync_copy(rows_v, out_hbm.at[pl.ds(base, SC_WINDOW)])

    return gather(src, pos.reshape(-1)).reshape(picks, n, width)


def _combine_kernel(h_ref, g_ref, gate_ref, pe_ref, ws13_ref, ws2_ref, ln_g_ref, ln_b_ref, wpg_ref, wpi_ref, o_ref, *,
                    alpha):
    h = h_ref[...]
    y = _swiglu(h.astype(BF16), ws13_ref[...], ws2_ref[...], SHARED_DIM)
    gate = gate_ref[...]
    for k in range(TOP_K_EXPERTS):
        y = y + gate[:, k:k + 1] * _unpack_rows(g_ref[k]).astype(F32)
    z = _layer_norm(alpha * h + y, ln_g_ref[...], ln_b_ref[...])
    ple_gate = jax.nn.sigmoid(jnp.dot(z.astype(BF16), wpg_ref[...], preferred_element_type=F32))
    emb = jnp.dot(pe_ref[...].astype(BF16), wpi_ref[...], preferred_element_type=F32)
    o_ref[...] = z + ple_gate * emb


def _combine(h, gathered, gate, pe, ws13, ws2, g, b, wpg, wpi, tm, alpha):
    n = h.shape[0]
    row = lambda w: pl.BlockSpec((tm, w), lambda i: (i, 0))
    full = lambda r, c: pl.BlockSpec((r, c), lambda i: (0, 0))
    return pl.pallas_call(
        functools.partial(_combine_kernel, alpha=alpha),
        grid=(n // tm,),
        in_specs=[row(D_MODEL), pl.BlockSpec((TOP_K_EXPERTS, tm, PACKED), lambda i: (0, i, 0)), row(LANES),
                  row(PLE_DIM), full(D_MODEL, 2 * SHARED_DIM), full(SHARED_DIM, D_MODEL),
                  full(1, D_MODEL), full(1, D_MODEL), full(D_MODEL, D_MODEL), full(PLE_DIM, D_MODEL)],
        out_specs=row(D_MODEL),
        out_shape=jax.ShapeDtypeStruct((n, D_MODEL), F32),
        compiler_params=_params(("parallel",)),
        name="combine",
    )(h, gathered, gate, pe, ws13, ws2, g, b, wpg, wpi)


def _rope_table(pos):
    inv = ROPE_THETA ** (-jnp.arange(0, HEAD_DIM, 2, dtype=F32) / HEAD_DIM)
    ang = pos.astype(F32)[:, None] * inv[None, :]
    return jnp.concatenate([jnp.tile(jnp.cos(ang), (1, 4)), jnp.tile(jnp.sin(ang), (1, 4))], axis=1)


def _rotate_half_cols(w, n_heads):
    w3 = w.reshape(w.shape[0], n_heads, HEAD_DIM)
    half = HEAD_DIM // 2
    return jnp.concatenate([-w3[..., half:], w3[..., :half]], axis=-1).reshape(w.shape)


def _fused_in_weight(w_in):
    offs = np.cumsum(IN_SIZES)[:-1].tolist()
    wq, wk, wv, wqi, wki, wwi, wu, wga, wgb = jnp.split(w_in, offs, axis=1)
    pad = jnp.zeros((D_MODEL, LANES - HEAD_DIM - IDX_HEADS), w_in.dtype)
    w_big = jnp.concatenate(
        [wq, _rotate_half_cols(wq, N_HEADS), wqi, _rotate_half_cols(wqi, IDX_HEADS),
         wk, wki, _rotate_half_cols(wk, 1), _rotate_half_cols(wki, 1), wv, wwi, pad, wu], axis=1).astype(BF16)
    return w_big, w_big[:, 0:C_U].T, wga.astype(BF16), wgb.astype(BF16)


def _pages_transposed(cache):
    return jnp.transpose(cache[0], (0, 2, 1))


def _heads_major(a, n_heads):
    b, t, w = a.shape
    d = w // n_heads
    return a.reshape(b, t, n_heads, d).transpose(0, 2, 1, 3).reshape(b, n_heads * t, d)


def kernel(x_prompt, x_sample, cache_k, cache_v, cache_kidx, state_pool, page_table, p_prompt, p_sample, w_in, w_att_out, w_pool_grp, pool_scale, w_pool_out, w_out, ln1_g, ln1_b, w_router, router_bias, w_exp13, w_exp2, w_sh13, w_sh2, ln2_g, ln2_b, w_ple_in, w_ple_gate):
    B, S, D = x_prompt.shape
    DB, T, _ = x_sample.shape
    depth = w_in.shape[0]
    assert depth == 1, "single layer step"
    page = cache_k.shape[2]
    past = page_table.shape[1] * page
    alpha = (2 * depth) ** 0.25
    n_p, n_s = B * S, DB * T

    w_big, w_t, wga, wgb = _fused_in_weight(w_in[0])
    wao, wpo, wo = w_att_out[0].astype(BF16), w_pool_out[0].astype(BF16), w_out[0].astype(BF16)
    wgrp = w_pool_grp[0].astype(BF16)
    pscale = pool_scale[0].reshape(1, POOL_WIDTH)
    g1, b1 = ln1_g[0].reshape(1, D), ln1_b[0].reshape(1, D)
    g2, b2 = ln2_g[0].reshape(1, D), ln2_b[0].reshape(1, D)
    wr_t = w_router[0].T.astype(BF16)
    rbias = router_bias[0].reshape(N_EXPERTS, 1)
    w13, w2 = w_exp13[0], w_exp2[0]
    ws13, ws2 = w_sh13[0].astype(BF16), w_sh2[0].astype(BF16)
    wpg, wpi = w_ple_gate[0].astype(BF16), w_ple_in[0].astype(BF16)

    cs_p = _rope_table(jnp.arange(S, dtype=I32))
    cs_s = jnp.tile(_rope_table(past + jnp.arange(T, dtype=I32)), (DB, 1))

    xp = x_prompt.reshape(n_p, D)
    qt, qit, wit, kb, kib, vbt, kt, vt, kit, u = _proj_prompt(xp, w_big, w_t, cs_p, S, 512)
    attn_p = _attn_prompt(qt, qit, wit, kb, kib, vbt)
    u3 = u.reshape(B, S, POOL_WIDTH)
    pool_p = _pool(jnp.zeros((B, PREV_ROWS, POOL_WIDTH), F32), u3, wgrp, pscale, 0).reshape(n_p, POOL_WIDTH)
    h_p, hp_p = _merge(xp, attn_p, pool_p, wga, wgb, wao, wpo, wo, g1, b1, 512, alpha)

    xs = x_sample.reshape(n_s, D)
    qs, qis, ks, vs, kis, wis, us = _proj_sample(xs, w_big, cs_s)
    q_hq = _heads_major(qs.reshape(DB, T, ATT_WIDTH), N_HEADS)
    qi_hq = _heads_major(qis.reshape(DB, T, IDX_HEADS * IDX_DIM), IDX_HEADS)
    wi_hq = wis.reshape(DB, T, IDX_HEADS).transpose(0, 2, 1).reshape(DB, IDX_HEADS * T, 1)
    o_hq = _attn_sample(page_table, q_hq, qi_hq, wi_hq, ks.reshape(DB, T, HEAD_DIM), vs.reshape(DB, T, HEAD_DIM),
                        kis.reshape(DB, T, IDX_DIM), _pages_transposed(cache_k), _pages_transposed(cache_v),
                        _pages_transposed(cache_kidx))
    attn_s = o_hq.reshape(DB, N_HEADS, T, HEAD_DIM).transpose(0, 2, 1, 3).reshape(n_s, ATT_WIDTH).astype(BF16)
    us3 = us.reshape(DB, T, POOL_WIDTH)
    prev_s = jnp.concatenate([jnp.zeros((DB, PREV_ROWS - POOL_STATE, POOL_WIDTH), F32), state_pool[0]], axis=1)
    pool_s = _pool(prev_s, us3, wgrp, pscale, past).reshape(n_s, POOL_WIDTH)
    h_s, _ = _merge(xs, attn_s, pool_s, wga, wgb, wao, wpo, wo, g1, b1, n_s, alpha)

    def tail(h, pe, tm_r, tm_m, tm_f):
        comb = _router(h, wr_t, rbias, tm_r)
        y = _moe(h, comb, ws13, ws2, w13, w2, tm_m)
        return _final(h, y, pe, g2, b2, wpg, wpi, tm_f, alpha)

    y_s = tail(h_s, p_sample[0].reshape(n_s, PLE_DIM), n_s, n_s, n_s)

    pos, gate, blk, used = _dispatch(h_p, wr_t, rbias, 1024)
    sorted_in = _scatter_rows(hp_p, pos, _sorted_rows(n_p))
    sorted_out = _grouped(blk.reshape(-1), used.reshape(-1), sorted_in, w13, w2)
    gathered = _gather_rows(sorted_out, pos)
    y_p = _combine(h_p, gathered, gate, p_prompt[0].reshape(n_p, PLE_DIM), ws13, ws2, g2, b2, wpg, wpi, 512, alpha)

    ext_s = jnp.concatenate([state_pool[0], us3], axis=1)
    return (y_p.reshape(B, S, D), y_s.reshape(DB, T, D),
            jnp.transpose(kt, (0, 2, 1))[None], jnp.transpose(vt, (0, 2, 1))[None],
            jnp.transpose(kit, (0, 2, 1))[None],
            u3[:, S - POOL_STATE:][None],
            ks.reshape(1, DB, T, HEAD_DIM), vs.reshape(1, DB, T, HEAD_DIM), kis.reshape(1, DB, T, IDX_DIM),
            ext_s[:, T:][None])
```

```python
import functools

import numpy as np
import jax
import jax.numpy as jnp
from jax import lax
from jax.experimental import pallas as pl
from jax.experimental.pallas import tpu as pltpu
from jax.experimental.pallas import tpu_sc as plsc

F32 = jnp.float32
BF16 = jnp.bfloat16
I32 = jnp.int32

D_MODEL = 1024
N_HEADS = 8
HEAD_DIM = 64
ATT_WIDTH = N_HEADS * HEAD_DIM
IDX_HEADS = 4
IDX_DIM = 64
TOP_K_MAX = 256
Q_BLOCK = 128
ROPE_THETA = 10000.0
POOL_WINDOWS = (2, 4, 8, 16)
POOL_GROUPS = 4
POOL_WIDTH = 512
POOL_GW = POOL_WIDTH // POOL_GROUPS
POOL_STATE = 15
N_EXPERTS = 64
TOP_K_EXPERTS = 8
N_GROUPS = 8
GROUP_SIZE = N_EXPERTS // N_GROUPS
TOPK_GROUPS = 4
EXPERT_DIM = 256
SHARED_DIM = 256
ROUTED_SCALE = 2.5
PLE_DIM = 256
LN_EPS = 1e-5
IN_SIZES = (ATT_WIDTH, HEAD_DIM, HEAD_DIM, IDX_HEADS * IDX_DIM, IDX_DIM, IDX_HEADS, POOL_WIDTH, D_MODEL, D_MODEL)

LANES = 128
SUBLANES = 8
INT_MIN = -2147483648
NEG_BIG = -1e30
VMEM_LIMIT = 56 * 1024 * 1024

C_Q, C_QR = 0, 512
C_QI, C_QIR = 1024, 1280
C_KK, C_KKR = 1536, 1664
C_VW = 1792
C_U = 1920
C_END = 2432

NT_DIMS = (((1,), (1,)), ((), ()))

Q_SCALE = HEAD_DIM ** -0.5 * float(np.log2(np.e))
QI_SCALE = IDX_DIM ** -0.5


def _params(sem):
    return pltpu.CompilerParams(dimension_semantics=sem, vmem_limit_bytes=VMEM_LIMIT)


def _layer_norm(x, g, b):
    mu = jnp.mean(x, axis=-1, keepdims=True)
    xc = x - mu
    var = jnp.mean(xc * xc, axis=-1, keepdims=True)
    return xc * lax.rsqrt(var + LN_EPS) * g + b


def _proj_sample_kernel(x_ref, w_ref, cs_ref, q_ref, qi_ref, k_ref, v_ref, ki_ref, wi_ref, u_ref):
    xb = x_ref[...].astype(BF16)
    cos = cs_ref[:, 0:LANES]
    sin = cs_ref[:, LANES:2 * LANES]

    def mm(c0, n):
        return jnp.dot(xb, w_ref[:, c0:c0 + n], preferred_element_type=F32)

    def rope(c0, c0r, n):
        reps = n // LANES
        cosn = jnp.concatenate([cos] * reps, axis=1) if reps > 1 else cos
        sinn = jnp.concatenate([sin] * reps, axis=1) if reps > 1 else sin
        return mm(c0, n) * cosn + mm(c0r, n) * sinn

    q_ref[...] = (rope(C_Q, C_QR, ATT_WIDTH) * Q_SCALE).astype(BF16)
    qi_ref[...] = (rope(C_QI, C_QIR, IDX_HEADS * IDX_DIM) * QI_SCALE).astype(BF16)
    kk = rope(C_KK, C_KKR, LANES)
    k_ref[...] = kk[:, 0:HEAD_DIM]
    ki_ref[...] = kk[:, HEAD_DIM:2 * HEAD_DIM]
    vw = mm(C_VW, LANES)
    v_ref[...] = vw[:, 0:HEAD_DIM]
    wi_ref[...] = vw[:, HEAD_DIM:HEAD_DIM + IDX_HEADS] * (IDX_HEADS ** -0.5)
    u_ref[...] = mm(C_U, POOL_WIDTH)


def _proj_sample(x, w_big, cs):
    n = x.shape[0]
    full = lambda r, c: pl.BlockSpec((r, c), lambda i: (0, 0))
    widths = (ATT_WIDTH, IDX_HEADS * IDX_DIM, HEAD_DIM, HEAD_DIM, IDX_DIM, IDX_HEADS, POOL_WIDTH)
    dtypes = (BF16, BF16, F32, F32, F32, F32, F32)
    return pl.pallas_call(
        _proj_sample_kernel,
        grid=(1,),
        in_specs=[full(n, D_MODEL), full(D_MODEL, C_END), full(n, 2 * LANES)],
        out_specs=tuple(full(n, w) for w in widths),
        out_shape=tuple(jax.ShapeDtypeStruct((n, w), dt) for w, dt in zip(widths, dtypes)),
        compiler_params=_params(("arbitrary",)),
        name="proj_sample",
    )(x, w_big, cs)


def _proj_prompt_kernel(x_ref, w_ref, wt_ref, cs_ref, cst_ref, qt_ref, qit_ref, wit_ref, kb_ref, kib_ref, vbt_ref,
                        kt_ref, vt_ref, kit_ref, u_ref):
    xb = x_ref[...].astype(BF16)
    tm = xb.shape[0]
    cos = cs_ref[:, 0:LANES]
    sin = cs_ref[:, LANES:2 * LANES]
    cos_t = cst_ref[0:HEAD_DIM, :]
    sin_t = cst_ref[LANES:LANES + HEAD_DIM, :]

    def mm(c0, n):
        return jnp.dot(xb, w_ref[:, c0:c0 + n], preferred_element_type=F32)

    def mm_t(c0, n):
        return lax.dot_general(wt_ref[c0:c0 + n, :], xb, NT_DIMS, preferred_element_type=F32)

    def rope_t(c0, c0r, heads):
        cosn = jnp.concatenate([cos_t] * heads, axis=0) if heads > 1 else cos_t
        sinn = jnp.concatenate([sin_t] * heads, axis=0) if heads > 1 else sin_t
        return mm_t(c0, heads * HEAD_DIM) * cosn + mm_t(c0r, heads * HEAD_DIM) * sinn

    kk = mm(C_KK, LANES) * cos + mm(C_KKR, LANES) * sin
    kb_ref[...] = kk[:, 0:HEAD_DIM].astype(BF16)
    kib_ref[...] = kk[:, HEAD_DIM:2 * HEAD_DIM].astype(BF16)
    u_ref[...] = mm(C_U, POOL_WIDTH)

    qt = (rope_t(C_Q, C_QR, N_HEADS) * Q_SCALE).astype(BF16)
    qit = (rope_t(C_QI, C_QIR, IDX_HEADS) * QI_SCALE).astype(BF16)
    for blk in range(tm // Q_BLOCK):
        cols = slice(blk * Q_BLOCK, (blk + 1) * Q_BLOCK)
        for h in range(N_HEADS):
            qt_ref[blk, :, h * Q_BLOCK:(h + 1) * Q_BLOCK] = qt[h * HEAD_DIM:(h + 1) * HEAD_DIM, cols]
        for h in range(IDX_HEADS):
            qit_ref[blk, :, h * Q_BLOCK:(h + 1) * Q_BLOCK] = qit[h * IDX_DIM:(h + 1) * IDX_DIM, cols]

    kkt = rope_t(C_KK, C_KKR, 2)
    kt_ref[...] = kkt[0:HEAD_DIM, :]
    kit_ref[...] = kkt[HEAD_DIM:2 * HEAD_DIM, :]
    vwt = mm_t(C_VW, LANES)
    vt_ref[...] = vwt[0:HEAD_DIM, :]
    vbt_ref[...] = vwt[0:HEAD_DIM, :].astype(BF16)
    wit_ref[...] = vwt[HEAD_DIM:HEAD_DIM + SUBLANES, :] * (IDX_HEADS ** -0.5)


def _proj_prompt(x, w_big, w_t, cs, seq, tm):
    n = x.shape[0]
    nb = seq // tm
    qb = tm // Q_BLOCK
    row = lambda w: pl.BlockSpec((tm, w), lambda i: (i, 0))
    col = lambda r: pl.BlockSpec((None, r, tm), lambda i: (i // nb, 0, i % nb))
    slab = lambda heads: pl.BlockSpec((qb, HEAD_DIM, heads * Q_BLOCK), lambda i: (i, 0, 0))
    pm = lambda r, dt: jax.ShapeDtypeStruct((n // seq, r, seq), dt)
    out_shape = (
        jax.ShapeDtypeStruct((n // Q_BLOCK, HEAD_DIM, N_HEADS * Q_BLOCK), BF16),
        jax.ShapeDtypeStruct((n // Q_BLOCK, IDX_DIM, IDX_HEADS * Q_BLOCK), BF16),
        pm(SUBLANES, F32),
        jax.ShapeDtypeStruct((n, HEAD_DIM), BF16), jax.ShapeDtypeStruct((n, IDX_DIM), BF16),
        pm(HEAD_DIM, BF16),
        pm(HEAD_DIM, F32), pm(HEAD_DIM, F32), pm(IDX_DIM, F32),
        jax.ShapeDtypeStruct((n, POOL_WIDTH), F32),
    )
    return pl.pallas_call(
        _proj_prompt_kernel,
        grid=(n // tm,),
        in_specs=[
            row(D_MODEL),
            pl.BlockSpec((D_MODEL, C_END), lambda i: (0, 0)),
            pl.BlockSpec((C_U, D_MODEL), lambda i: (0, 0)),
            pl.BlockSpec((tm, 2 * LANES), lambda i: (i % nb, 0)),
            pl.BlockSpec((2 * LANES, tm), lambda i: (0, i % nb)),
        ],
        out_specs=(slab(N_HEADS), slab(IDX_HEADS), col(SUBLANES), row(HEAD_DIM), row(IDX_DIM), col(HEAD_DIM),
                   col(HEAD_DIM), col(HEAD_DIM), col(IDX_DIM), row(POOL_WIDTH)),
        out_shape=out_shape,
        compiler_params=_params(("parallel",)),
        name="proj_prompt",
    )(x, w_big, w_t, cs, cs.T)


def _float_of_rank(u):
    key = u ^ INT_MIN
    bits = jnp.where(key < 0, INT_MIN - key, key)
    return pltpu.bitcast(bits, F32)


def _count(mask):
    return jnp.sum(mask.astype(F32), axis=1, keepdims=True)


def _topk_bias(sc_ref, j_ref, adm, n_adm, lc, k):
    rows = sc_ref.shape[0]
    kf = float(k)

    def value_step(i, t_u):
        hi = jnp.left_shift(jnp.int32(1), 31 - 2 * i)
        lo = jnp.left_shift(jnp.int32(1), 30 - 2 * i)
        for cand_u in (t_u | lo, t_u | hi, t_u | hi | lo):
            cnt = _count(sc_ref[:, 0:lc] >= _float_of_rank(cand_u))
            t_u = jnp.where(cnt >= kf, cand_u, t_u)
        return t_u

    t_u = lax.fori_loop(0, 16, value_step, jnp.zeros((rows, 1), I32))
    few = n_adm < k
    thr = jnp.where(few, -jnp.inf, _float_of_rank(t_u))
    sc = sc_ref[:, 0:lc]
    cnt_gt = _count(sc > thr)
    cnt_eq = _count(sc == thr)
    need = kf - cnt_gt
    cut_needed = jnp.logical_and(cnt_gt + cnt_eq > kf, jnp.logical_not(few))
    any_cut = jnp.max(cut_needed.astype(F32)) > 0.0
    idx = lax.broadcasted_iota(I32, (rows, lc), 1)
    nbits = int(np.ceil(np.log2(lc)))

    j_ref[...] = jnp.full((rows, 1), lc, I32)

    @pl.when(any_cut)
    def _():
        def index_step(i, j):
            cand = j | jnp.left_shift(jnp.int32(1), nbits - 1 - i)
            c = _count(jnp.logical_and(sc_ref[:, 0:lc] == thr, idx < cand))
            return jnp.where(c < need, cand, j)

        j_ref[...] = lax.fori_loop(0, nbits, index_step, jnp.zeros((rows, 1), I32))

    sel = jnp.logical_or(sc > thr, jnp.logical_and(sc == thr, idx <= j_ref[...]))
    return jnp.where(jnp.logical_and(sel, adm), 0.0, NEG_BIG)


ATTN_CHUNK = 256


def _attn_prompt_block(n_chunks, q0, top_k, qt_ref, qit_ref, wit_ref, kb_ref, kib_ref, vbt_ref, o_ref,
                       key_ref, bias_ref, lg_ref, j_ref):
    tq, ch = Q_BLOCK, ATTN_CHUNK
    seq = key_ref.shape[0]
    kf = float(top_k)
    kpos = lax.broadcasted_iota(I32, (ch, tq), 0)
    qpos = q0 + lax.broadcasted_iota(I32, (ch, tq), 1)

    def rows(c):
        return slice(c * ch, (c + 1) * ch)

    def fold(x, op):
        return op(x.reshape(ch // SUBLANES, SUBLANES, tq), axis=0)

    def head(x, h):
        return x[:, h * tq:(h + 1) * tq]

    qit = qit_ref[...]
    wit = wit_ref[...]
    for c in range(n_chunks if n_chunks * ch > top_k else 0):
        d = jnp.dot(kib_ref[rows(c), :], qit, preferred_element_type=F32)
        s = wit[0:1, :] * jnp.maximum(head(d, 0), 0.0)
        for h in range(1, IDX_HEADS):
            s = s + wit[h:h + 1, :] * jnp.maximum(head(d, h), 0.0)
        key_ref[rows(c), :] = jnp.where(c * ch + kpos <= qpos, s, -jnp.inf)

    def count(pred):
        acc = jnp.zeros((SUBLANES, tq), F32)
        for c in range(n_chunks):
            acc = acc + fold(pred(key_ref[rows(c), :], c).astype(F32), jnp.sum)
        return jnp.sum(acc, axis=0, keepdims=True)

    if n_chunks * ch <= top_k:
        for c in range(n_chunks):
            bias_ref[rows(c), :] = jnp.where(c * ch + kpos <= qpos, 0.0, NEG_BIG)
    else:
        def value_step(i, carry):
            t_u, n_ge = carry
            cand_u = t_u | jnp.left_shift(jnp.int32(1), 31 - i)
            cand = _float_of_rank(cand_u)
            cnt = count(lambda k, c: k >= cand)
            ok = cnt >= kf
            return jnp.where(ok, cand_u, t_u), jnp.where(ok, cnt, n_ge)

        t_u, n_ge = lax.fori_loop(0, 32, value_step,
                                  (jnp.zeros((1, tq), I32), jnp.full((1, tq), float(n_chunks * ch), F32)))
        few = qpos[0:1, :] + 1 <= top_k
        thr = jnp.where(few, -jnp.inf, _float_of_rank(t_u))
        cut_needed = jnp.logical_and(n_ge > kf, jnp.logical_not(few))
        any_cut = jnp.max(cut_needed.astype(F32)) > 0.0

        nbits = int(np.ceil(np.log2(seq)))
        j_ref[...] = jnp.full(j_ref.shape, seq, I32)

        @pl.when(any_cut)
        def _():
            need = kf - count(lambda k, c: k > thr)

            def index_step(i, j):
                cand = j | jnp.left_shift(jnp.int32(1), nbits - 1 - i)
                n_before = count(lambda k, c: jnp.logical_and(k == thr, c * ch + kpos < cand))
                return jnp.where(n_before < need, cand, j)

            j = lax.fori_loop(0, nbits, index_step, jnp.zeros((1, tq), I32))
            j_ref[...] = jnp.broadcast_to(j, j_ref.shape)

        j_cut = j_ref[0:1, :]
        for c in range(n_chunks):
            k = key_ref[rows(c), :]
            pos = c * ch + kpos
            sel = jnp.logical_or(k > thr, jnp.logical_and(k == thr, pos <= j_cut))
            bias_ref[rows(c), :] = jnp.where(jnp.logical_and(sel, pos <= qpos), 0.0, NEG_BIG)

    qt = qt_ref[...]
    mx = [jnp.full((SUBLANES, tq), -jnp.inf, F32) for _ in range(N_HEADS)]
    for c in range(n_chunks):
        lg = jnp.dot(kb_ref[rows(c), :], qt, preferred_element_type=F32)
        bias = bias_ref[rows(c), :]
        for h in range(N_HEADS):
            lgh = head(lg, h) + bias
            lg_ref[h, rows(c), :] = lgh
            mx[h] = jnp.maximum(mx[h], fold(lgh, jnp.max))

    outs = []
    for h in range(N_HEADS):
        m = jnp.max(mx[h], axis=0, keepdims=True)
        lsum = jnp.zeros((SUBLANES, tq), F32)
        ot = jnp.zeros((HEAD_DIM, tq), F32)
        for c in range(n_chunks):
            p = jnp.exp2(lg_ref[h, rows(c), :] - m)
            lsum = lsum + fold(p, jnp.sum)
            ot = ot + jnp.dot(vbt_ref[:, rows(c)], p.astype(BF16), preferred_element_type=F32)
        outs.append(ot / jnp.sum(lsum, axis=0, keepdims=True))
    o_ref[...] = jnp.concatenate(outs, axis=0).T.astype(BF16)


def _attn_prompt_kernel(qt_ref, qit_ref, wit_ref, kb_ref, kib_ref, vbt_ref, o_ref, key_ref, bias_ref, lg_ref, j_ref,
                        *, top_k):
    jq = pl.program_id(1)
    blocks_per_chunk = ATTN_CHUNK // Q_BLOCK
    n_classes = key_ref.shape[0] // ATTN_CHUNK
    for cls in range(n_classes):
        @pl.when(jq // blocks_per_chunk == cls)
        def _(cls=cls):
            _attn_prompt_block(cls + 1, jq * Q_BLOCK, top_k, qt_ref, qit_ref, wit_ref, kb_ref, kib_ref, vbt_ref,
                               o_ref, key_ref, bias_ref, lg_ref, j_ref)


def _attn_prompt(qt, qit, wit, kb, kib, vbt):
    batch, _, seq = vbt.shape
    nb = seq // Q_BLOCK
    top_k = min(TOP_K_MAX, seq // 4)
    slab = lambda heads: pl.BlockSpec((None, HEAD_DIM, heads * Q_BLOCK), lambda b, j: (b * nb + j, 0, 0))
    keys = pl.BlockSpec((seq, HEAD_DIM), lambda b, j: (b, 0))
    return pl.pallas_call(
        functools.partial(_attn_prompt_kernel, top_k=top_k),
        grid=(batch, nb),
        in_specs=[slab(N_HEADS), slab(IDX_HEADS), pl.BlockSpec((None, SUBLANES, Q_BLOCK), lambda b, j: (b, 0, j)),
                  keys, keys, pl.BlockSpec((None, HEAD_DIM, seq), lambda b, j: (b, 0, 0))],
        out_specs=pl.BlockSpec((Q_BLOCK, ATT_WIDTH), lambda b, j: (b * nb + j, 0)),
        out_shape=jax.ShapeDtypeStruct((batch * seq, ATT_WIDTH), BF16),
        scratch_shapes=[pltpu.VMEM((seq, Q_BLOCK), F32), pltpu.VMEM((seq, Q_BLOCK), F32),
                        pltpu.VMEM((N_HEADS, seq, Q_BLOCK), F32), pltpu.VMEM((SUBLANES, Q_BLOCK), I32)],
        compiler_params=_params(("parallel", "arbitrary")),
        name="attn_prompt",
    )(qt, qit, wit, kb, kib, vbt)


SAMPLE_CHUNK = 1024


def _attn_sample_kernel(pt_ref, q_ref, qi_ref, wi_ref, kn_ref, vn_ref, kin_ref, ck_hbm, cv_hbm, cki_hbm, o_ref,
                        kbuf, vbuf, kibuf, sem, key_scr, bias_scr, lg_scr, j_scr, *, n_pages, page, t_new, top_k):
    b = pl.program_id(0)
    n_b = pl.num_programs(0)
    slot = b % 2
    past = n_pages * page
    lc = past + page
    n_chunks = past // SAMPLE_CHUNK

    def page_copies(bb, sl, p):
        phys = pt_ref[bb * n_pages + p]
        dst = pl.ds(pl.multiple_of(p * page, page), page)
        return [pltpu.make_async_copy(src.at[phys], buf.at[sl, :, dst], sem.at[i, sl])
                for i, (src, buf) in enumerate(((ck_hbm, kbuf), (cv_hbm, vbuf), (cki_hbm, kibuf)))]

    def start_batch(bb, sl):
        def body(p, carry):
            for cp in page_copies(bb, sl, p):
                cp.start()
            return carry
        lax.fori_loop(0, n_pages, body, 0)

    def wait_batch(bb, sl):
        def body(p, carry):
            for cp in page_copies(bb, sl, p):
                cp.wait()
            return carry
        lax.fori_loop(0, n_pages, body, 0)

    @pl.when(b == 0)
    def _():
        start_batch(0, 0)

    @pl.when(b + 1 < n_b)
    def _():
        start_batch(b + 1, 1 - slot)

    wait_batch(b, slot)

    def head_sum(d):
        r = wi_ref[...] * jnp.maximum(d, 0.0)
        s = r[0:t_new]
        for h in range(1, IDX_HEADS):
            s = s + r[h * t_new:(h + 1) * t_new]
        return s

    def new_rows(ref):
        pad = jnp.zeros((page - t_new, ref.shape[1]), F32)
        return jnp.concatenate([ref[...], pad], axis=0).astype(BF16)

    qi = qi_ref[...]
    for c in range(n_chunks):
        sl = slice(c * SAMPLE_CHUNK, (c + 1) * SAMPLE_CHUNK)
        d = jnp.dot(qi, kibuf[slot, :, sl].astype(BF16), preferred_element_type=F32)
        key_scr[:, sl] = head_sum(d)
    d_new = lax.dot_general(qi, new_rows(kin_ref), NT_DIMS, preferred_element_type=F32)
    adm_new = lax.broadcasted_iota(I32, (t_new, page), 1) <= lax.broadcasted_iota(I32, (t_new, page), 0)
    key_scr[:, past:lc] = jnp.where(adm_new, head_sum(d_new), -jnp.inf)

    idx = lax.broadcasted_iota(I32, (t_new, lc), 1)
    trow = lax.broadcasted_iota(I32, (t_new, lc), 0)
    n_adm = past + 1 + lax.broadcasted_iota(I32, (t_new, 1), 0)
    bias_scr[...] = _topk_bias(key_scr, j_scr, idx - past <= trow, n_adm, lc, top_k)

    q = q_ref[...]

    def bias_rows(sl):
        return jnp.concatenate([bias_scr[:, sl]] * N_HEADS, axis=0)

    m = jnp.full((N_HEADS * t_new, 1), -jnp.inf, F32)
    for c in range(n_chunks):
        sl = slice(c * SAMPLE_CHUNK, (c + 1) * SAMPLE_CHUNK)
        lg = jnp.dot(q, kbuf[slot, :, sl].astype(BF16), preferred_element_type=F32) + bias_rows(sl)
        lg_scr[:, sl] = lg
        m = jnp.maximum(m, jnp.max(lg, axis=1, keepdims=True))
    lg_new = lax.dot_general(q, new_rows(kn_ref), NT_DIMS, preferred_element_type=F32) + bias_rows(slice(past, lc))
    m = jnp.maximum(m, jnp.max(lg_new, axis=1, keepdims=True))

    p_new = jnp.exp2(lg_new - m)
    l = jnp.sum(p_new, axis=1, keepdims=True)
    o = jnp.dot(p_new.astype(BF16), new_rows(vn_ref), preferred_element_type=F32)
    for c in range(n_chunks):
        sl = slice(c * SAMPLE_CHUNK, (c + 1) * SAMPLE_CHUNK)
        pr = jnp.exp2(lg_scr[:, sl] - m)
        l = l + jnp.sum(pr, axis=1, keepdims=True)
        o = o + lax.dot_general(pr.astype(BF16), vbuf[slot, :, sl].astype(BF16), NT_DIMS,
                                preferred_element_type=F32)
    o_ref[...] = o / l


def _attn_sample(page_table, q_hq, qi_hq, wi_hq, k_new, v_new, ki_new, cache_kt, cache_vt, cache_kit):
    db, n_pages = page_table.shape
    page = cache_kt.shape[2]
    t_new = k_new.shape[1]
    past = n_pages * page
    lc = past + page
    top_k = min(TOP_K_MAX, (past + t_new) // 4)
    per_b = lambda r, w: pl.BlockSpec((None, r, w), lambda b, pt: (b, 0, 0))
    hbm = pl.BlockSpec(memory_space=pl.ANY)
    kern = functools.partial(_attn_sample_kernel, n_pages=n_pages, page=page, t_new=t_new, top_k=top_k)
    slab = pltpu.VMEM((2, HEAD_DIM, past), F32)
    grid_spec = pltpu.PrefetchScalarGridSpec(
        num_scalar_prefetch=1,
        grid=(db,),
        in_specs=[per_b(N_HEADS * t_new, HEAD_DIM), per_b(IDX_HEADS * t_new, IDX_DIM), per_b(IDX_HEADS * t_new, 1),
                  per_b(t_new, HEAD_DIM), per_b(t_new, HEAD_DIM), per_b(t_new, IDX_DIM),
                  hbm, hbm, hbm],
        out_specs=per_b(N_HEADS * t_new, HEAD_DIM),
        scratch_shapes=[slab, slab, slab, pltpu.SemaphoreType.DMA((3, 2)),
                        pltpu.VMEM((t_new, lc), F32), pltpu.VMEM((t_new, lc), F32),
                        pltpu.VMEM((N_HEADS * t_new, past), F32), pltpu.VMEM((t_new, 1), I32)],
    )
    return pl.pallas_call(
        kern,
        grid_spec=grid_spec,
        out_shape=jax.ShapeDtypeStruct((db, N_HEADS * t_new, HEAD_DIM), F32),
        compiler_params=_params(("arbitrary",)),
        name="attn_sample",
    )(page_table.reshape(-1), q_hq, qi_hq, wi_hq, k_new, v_new, ki_new, cache_kt, cache_vt, cache_kit)


PREV_ROWS = 16


def _pool_kernel(prev_ref, u_ref, wg_ref, sc_ref, o_ref, ext_ref, *, pos0):
    t_len = u_ref.shape[0]
    ext_ref[0:PREV_ROWS, :] = prev_ref[...]
    ext_ref[PREV_ROWS:PREV_ROWS + t_len, :] = u_ref[...]
    pos = pos0 + lax.broadcasted_iota(I32, (t_len, 1), 0)
    for g, w in enumerate(POOL_WINDOWS):
        sl = slice(g * POOL_GW, (g + 1) * POOL_GW)
        u_new = ext_ref[PREV_ROWS:PREV_ROWS + t_len, sl]
        win = u_new
        for back in range(1, w):
            win = win + ext_ref[PREV_ROWS - back:PREV_ROWS - back + t_len, sl]
        count = jnp.minimum(pos + 1, w).astype(F32)
        r = win / count - u_new
        mixed = jnp.dot(r.astype(BF16), wg_ref[g], preferred_element_type=F32) * sc_ref[:, sl]
        o_ref[:, sl] = mixed.astype(BF16)


def _pool(prev, u, w_grp, scale, pos0):
    nb, t_len, _ = u.shape
    return pl.pallas_call(
        functools.partial(_pool_kernel, pos0=pos0),
        grid=(nb,),
        in_specs=[pl.BlockSpec((None, PREV_ROWS, POOL_WIDTH), lambda b: (b, 0, 0)),
                  pl.BlockSpec((None, t_len, POOL_WIDTH), lambda b: (b, 0, 0)),
                  pl.BlockSpec((POOL_GROUPS, POOL_GW, POOL_GW), lambda b: (0, 0, 0)),
                  pl.BlockSpec((1, POOL_WIDTH), lambda b: (0, 0))],
        out_specs=pl.BlockSpec((None, t_len, POOL_WIDTH), lambda b: (b, 0, 0)),
        out_shape=jax.ShapeDtypeStruct((nb, t_len, POOL_WIDTH), BF16),
        scratch_shapes=[pltpu.VMEM((PREV_ROWS + t_len, POOL_WIDTH), F32)],
        compiler_params=_params(("parallel",)),
        name="pool",
    )(prev, u, w_grp, scale)


def _merge_kernel(x_ref, a_ref, p_ref, wga_ref, wgb_ref, wao_ref, wpo_ref, wo_ref, g_ref, b_ref, h_ref, hp_ref, *,
                  alpha):
    x = x_ref[...]
    xb = x.astype(BF16)
    ga = jnp.dot(xb, wga_ref[...], preferred_element_type=F32)
    gb = jnp.dot(xb, wgb_ref[...], preferred_element_type=F32)
    ya = jnp.dot(a_ref[...], wao_ref[...], preferred_element_type=F32)
    yp = jnp.dot(p_ref[...], wpo_ref[...], preferred_element_type=F32)
    mix = jax.nn.sigmoid(ga) * ya + jax.nn.sigmoid(gb) * yp
    out = jnp.dot(mix.astype(BF16), wo_ref[...], preferred_element_type=F32)
    h = _layer_norm(alpha * x + out, g_ref[...], b_ref[...])
    h_ref[...] = h
    hp_ref[...] = _pack_rows(h)


def _merge(x, attn, pool, wga, wgb, wao, wpo, wo, g, b, tm, alpha):
    n = x.shape[0]
    row = lambda w: pl.BlockSpec((tm, w), lambda i: (i, 0))
    full = lambda r, c: pl.BlockSpec((r, c), lambda i: (0, 0))
    return pl.pallas_call(
        functools.partial(_merge_kernel, alpha=alpha),
        grid=(n // tm,),
        in_specs=[row(D_MODEL), row(ATT_WIDTH), row(POOL_WIDTH), full(D_MODEL, D_MODEL), full(D_MODEL, D_MODEL),
                  full(ATT_WIDTH, D_MODEL), full(POOL_WIDTH, D_MODEL), full(D_MODEL, D_MODEL),
                  full(1, D_MODEL), full(1, D_MODEL)],
        out_specs=(row(D_MODEL), row(PACKED)),
        out_shape=(jax.ShapeDtypeStruct((n, D_MODEL), F32), jax.ShapeDtypeStruct((n, PACKED), I32)),
        compiler_params=_params(("parallel",)),
        name="merge",
    )(x, attn, pool, wga, wgb, wao, wpo, wo, g, b)


def _route(h, wr_t, bias_col):
    tm = h.shape[0]
    logits = lax.dot_general(wr_t, h.astype(BF16), NT_DIMS, preferred_element_type=F32)
    s = jax.nn.sigmoid(logits)
    sb = s + bias_col
    neg_inf = -jnp.inf

    rows = []
    for g in range(N_GROUPS):
        blk = sb[g * GROUP_SIZE:(g + 1) * GROUP_SIZE, :]
        m1 = jnp.max(blk, axis=0, keepdims=True)
        is_m1 = blk == m1
        n_m1 = jnp.sum(is_m1.astype(F32), axis=0, keepdims=True)
        m2 = jnp.max(jnp.where(is_m1, neg_inf, blk), axis=0, keepdims=True)
        rows.append(m1 + jnp.where(n_m1 >= 2.0, m1, m2))
    gs = jnp.concatenate(rows, axis=0)

    gi = lax.broadcasted_iota(I32, (N_GROUPS, tm), 0)
    rank = jnp.zeros((N_GROUPS, tm), F32)
    for g in range(N_GROUPS):
        row = gs[g:g + 1, :]
        beats = jnp.logical_or(row > gs, jnp.logical_and(row == gs, g < gi))
        rank = rank + beats.astype(F32)
    gkeep = rank < float(TOPK_GROUPS)
    emask = jnp.concatenate(
        [jnp.broadcast_to(gkeep[g:g + 1, :], (GROUP_SIZE, tm)) for g in range(N_GROUPS)], axis=0)

    ei = lax.broadcasted_iota(I32, (N_EXPERTS, tm), 0)
    x = jnp.where(emask, sb, neg_inf)
    sel = jnp.zeros((N_EXPERTS, tm), jnp.bool_)
    picks = []
    for _ in range(TOP_K_EXPERTS):
        m = jnp.max(x, axis=0, keepdims=True)
        first = jnp.min(jnp.where(x == m, ei, N_EXPERTS), axis=0, keepdims=True)
        pick = ei == first
        sel = jnp.logical_or(sel, pick)
        x = jnp.where(pick, neg_inf, x)
        picks.append(first)

    gate = jnp.where(sel, s, 0.0)
    comb = gate / jnp.sum(gate, axis=0, keepdims=True) * ROUTED_SCALE
    return comb, sel, picks


def _router_kernel(h_ref, wr_ref, bias_ref, c_ref):
    comb, _, _ = _route(h_ref[...], wr_ref[...], bias_ref[...])
    comb = jnp.concatenate([comb, jnp.zeros((LANES - N_EXPERTS, comb.shape[1]), F32)], axis=0)
    c_ref[...] = comb.T


def _router(h, wr_t, bias_col, tm):
    n = h.shape[0]
    return pl.pallas_call(
        _router_kernel,
        grid=(n // tm,),
        in_specs=[pl.BlockSpec((tm, D_MODEL), lambda i: (i, 0)),
                  pl.BlockSpec((N_EXPERTS, D_MODEL), lambda i: (0, 0)),
                  pl.BlockSpec((N_EXPERTS, 1), lambda i: (0, 0))],
        out_specs=pl.BlockSpec((tm, LANES), lambda i: (i, 0)),
        out_shape=jax.ShapeDtypeStruct((n, LANES), F32),
        compiler_params=_params(("parallel",)),
        name="router",
    )(h, wr_t, bias_col)


def _swiglu(xb, w13, w2, hidden):
    ab = jnp.dot(xb, w13, preferred_element_type=F32)
    act = jax.nn.silu(ab[:, 0:hidden]) * ab[:, hidden:2 * hidden]
    return jnp.dot(act.astype(BF16), w2, preferred_element_type=F32)


def _moe_kernel(h_ref, c_ref, ws13_ref, ws2_ref, w13_ref, w2_ref, y_ref, hb_ref):
    e = pl.program_id(1)

    @pl.when(e == 0)
    def _():
        hb_ref[...] = h_ref[...].astype(BF16)
        y_ref[...] = _swiglu(hb_ref[...], ws13_ref[...], ws2_ref[...], SHARED_DIM)

    ye = _swiglu(hb_ref[...], w13_ref[...].astype(BF16), w2_ref[...].astype(BF16), EXPERT_DIM)
    lane = lax.broadcasted_iota(I32, c_ref.shape, 1)
    ce = jnp.sum(jnp.where(lane == e, c_ref[...], 0.0), axis=1, keepdims=True)
    y_ref[...] += ce * ye


def _moe(h, comb, ws13, ws2, w13, w2, tm):
    n = h.shape[0]
    return pl.pallas_call(
        _moe_kernel,
        grid=(n // tm, N_EXPERTS),
        in_specs=[pl.BlockSpec((tm, D_MODEL), lambda i, e: (i, 0)),
                  pl.BlockSpec((tm, LANES), lambda i, e: (i, 0)),
                  pl.BlockSpec((D_MODEL, 2 * SHARED_DIM), lambda i, e: (0, 0)),
                  pl.BlockSpec((SHARED_DIM, D_MODEL), lambda i, e: (0, 0)),
                  pl.BlockSpec((None, D_MODEL, 2 * EXPERT_DIM), lambda i, e: (e, 0, 0)),
                  pl.BlockSpec((None, EXPERT_DIM, D_MODEL), lambda i, e: (e, 0, 0))],
        out_specs=pl.BlockSpec((tm, D_MODEL), lambda i, e: (i, 0)),
        out_shape=jax.ShapeDtypeStruct((n, D_MODEL), F32),
        scratch_shapes=[pltpu.VMEM((tm, D_MODEL), BF16)],
        compiler_params=_params(("parallel", "arbitrary")),
        name="moe",
    )(h, comb, ws13, ws2, w13, w2)


def _final_kernel(h_ref, y_ref, pe_ref, g_ref, b_ref, wpg_ref, wpi_ref, o_ref, *, alpha):
    z = _layer_norm(alpha * h_ref[...] + y_ref[...], g_ref[...], b_ref[...])
    gate = jax.nn.sigmoid(jnp.dot(z.astype(BF16), wpg_ref[...], preferred_element_type=F32))
    emb = jnp.dot(pe_ref[...].astype(BF16), wpi_ref[...], preferred_element_type=F32)
    o_ref[...] = z + gate * emb


def _final(h, y, pe, g, b, wpg, wpi, tm, alpha):
    n = h.shape[0]
    row = lambda w: pl.BlockSpec((tm, w), lambda i: (i, 0))
    full = lambda r, c: pl.BlockSpec((r, c), lambda i: (0, 0))
    return pl.pallas_call(
        functools.partial(_final_kernel, alpha=alpha),
        grid=(n // tm,),
        in_specs=[row(D_MODEL), row(D_MODEL), row(PLE_DIM), full(1, D_MODEL), full(1, D_MODEL),
                  full(D_MODEL, D_MODEL), full(PLE_DIM, D_MODEL)],
        out_specs=row(D_MODEL),
        out_shape=jax.ShapeDtypeStruct((n, D_MODEL), F32),
        compiler_params=_params(("parallel",)),
        name="final",
    )(h, y, pe, g, b, wpg, wpi)


MOE_BLOCK = 1024


def _sorted_rows(n_tokens):
    worst = n_tokens * TOP_K_EXPERTS + N_EXPERTS * (MOE_BLOCK - 1)
    return -(-worst // MOE_BLOCK) * MOE_BLOCK


def _dispatch_kernel(h_ref, wr_ref, bias_ref, tri_ref, pos_ref, gate_ref, blk_ref, used_ref,
                     eidx_s, rank_s, gate_s, cnt_s):
    p = pl.program_id(0)
    i = pl.program_id(1)
    tm = h_ref.shape[0]
    ei = lax.broadcasted_iota(I32, (N_EXPERTS, tm), 0)

    @pl.when(p == 0)
    def _():
        comb, sel, picks = _route(h_ref[...], wr_ref[...], bias_ref[...])
        before = jnp.dot(sel.astype(BF16), tri_ref[...], preferred_element_type=F32)
        ranks, gates = [], []
        for first in picks:
            pick = ei == first
            ranks.append(jnp.sum(jnp.where(pick, before, 0.0), axis=0, keepdims=True))
            gates.append(jnp.sum(jnp.where(pick, comb, 0.0), axis=0, keepdims=True))
        eidx_s[i] = jnp.concatenate(picks, axis=0)
        rank_s[i] = jnp.concatenate(ranks, axis=0)
        gate_s[i] = jnp.concatenate(gates, axis=0)
        cnt_s[i] = jnp.broadcast_to(jnp.sum(sel.astype(F32), axis=1, keepdims=True), (N_EXPERTS, LANES))

    @pl.when(p == 1)
    def _():
        cnt = cnt_s[...]
        tile_id = lax.broadcasted_iota(I32, cnt.shape, 0)
        total = jnp.sum(cnt, axis=0)
        prior = jnp.sum(jnp.where(tile_id < i, cnt, 0.0), axis=0)
        seg = jnp.ceil(total * (1.0 / MOE_BLOCK)) * MOE_BLOCK
        lower = (lax.broadcasted_iota(I32, (N_EXPERTS, N_EXPERTS), 1)
                 < lax.broadcasted_iota(I32, (N_EXPERTS, N_EXPERTS), 0)).astype(F32)
        seg_off = jnp.dot(lower, seg, precision=lax.Precision.HIGHEST, preferred_element_type=F32)
        base = (seg_off + prior)[:, 0:1]
        eidx = eidx_s[i]
        rank = rank_s[i]
        rows = []
        for k in range(TOP_K_EXPERTS):
            pick = ei == eidx[k:k + 1, :]
            rows.append(rank[k:k + 1, :] + jnp.sum(jnp.where(pick, base, 0.0), axis=0, keepdims=True))
        pos_ref[...] = jnp.concatenate(rows, axis=0).astype(I32)
        gate_ref[...] = jnp.concatenate([gate_s[i], jnp.zeros((LANES - TOP_K_EXPERTS, tm), F32)], axis=0).T

        seg_end = (seg_off + seg)[:, 0:1]
        n_blk = blk_ref.shape[1]
        blk_start = (lax.broadcasted_iota(I32, (N_EXPERTS, n_blk), 1) * MOE_BLOCK).astype(F32)
        owner = jnp.sum((seg_end <= blk_start).astype(F32), axis=0, keepdims=True)
        blk_ref[...] = jnp.minimum(owner, N_EXPERTS - 1.0).astype(I32)
        used = seg_end[N_EXPERTS - 1:N_EXPERTS, :] * (1.0 / MOE_BLOCK)
        used_ref[...] = jnp.broadcast_to(used, used_ref.shape).astype(I32)


def _dispatch(h, wr_t, bias_col, tm):
    n = h.shape[0]
    n_tiles = n // tm
    n_blk = _sorted_rows(n) // MOE_BLOCK
    n_blk_pad = -(-n_blk // LANES) * LANES
    tri = jnp.triu(jnp.ones((tm, tm), BF16), k=1)
    const = lambda r, c: pl.BlockSpec((r, c), lambda p, i: (0, 0))
    per_tile = lambda dt: pltpu.VMEM((n_tiles, TOP_K_EXPERTS, tm), dt)
    return pl.pallas_call(
        _dispatch_kernel,
        grid=(2, n_tiles),
        in_specs=[pl.BlockSpec((tm, D_MODEL), lambda p, i: (i * (1 - p), 0)),
                  const(N_EXPERTS, D_MODEL), const(N_EXPERTS, 1), const(tm, tm)],
        out_specs=(pl.BlockSpec((TOP_K_EXPERTS, tm), lambda p, i: (0, i * p)),
                   pl.BlockSpec((tm, LANES), lambda p, i: (i * p, 0)),
                   const(1, n_blk_pad), const(1, LANES)),
        out_shape=(jax.ShapeDtypeStruct((TOP_K_EXPERTS, n), I32), jax.ShapeDtypeStruct((n, LANES), F32),
                   jax.ShapeDtypeStruct((1, n_blk_pad), I32), jax.ShapeDtypeStruct((1, LANES), I32)),
        scratch_shapes=[per_tile(I32), per_tile(F32), per_tile(F32), pltpu.VMEM((n_tiles, N_EXPERTS, LANES), F32)],
        compiler_params=_params(("arbitrary", "arbitrary")),
        name="dispatch",
    )(h, wr_t, bias_col, tri)


PACKED = D_MODEL // 2


def _pack_rows(x):
    lo = pltpu.bitcast(x[:, 0:PACKED].astype(BF16).astype(F32), I32)
    hi = pltpu.bitcast(x[:, PACKED:D_MODEL].astype(BF16).astype(F32), I32)
    return jnp.bitwise_or(hi, lax.shift_right_logical(lo, 16))


def _unpack_rows(w):
    lo = pltpu.bitcast(lax.shift_left(w, 16), F32)
    hi = pltpu.bitcast(jnp.bitwise_and(w, -65536), F32)
    return jnp.concatenate([lo, hi], axis=1).astype(BF16)


def _grouped_kernel(blk_ref, used_ref, xs_ref, w13_ref, w2_ref, ys_ref):
    b = pl.program_id(0)

    @pl.when(b < used_ref[0])
    def _():
        ys = _swiglu(_unpack_rows(xs_ref[...]), w13_ref[...].astype(BF16), w2_ref[...].astype(BF16), EXPERT_DIM)
        ys_ref[...] = _pack_rows(ys)

    @pl.when(b >= used_ref[0])
    def _():
        ys_ref[...] = jnp.zeros(ys_ref.shape, I32)


def _grouped(blk, used, xs, w13, w2):
    ns = xs.shape[0]
    grid_spec = pltpu.PrefetchScalarGridSpec(
        num_scalar_prefetch=2,
        grid=(ns // MOE_BLOCK,),
        in_specs=[pl.BlockSpec((MOE_BLOCK, PACKED), lambda b, blk, used: (b, 0)),
                  pl.BlockSpec((None, D_MODEL, 2 * EXPERT_DIM), lambda b, blk, used: (blk[b], 0, 0)),
                  pl.BlockSpec((None, EXPERT_DIM, D_MODEL), lambda b, blk, used: (blk[b], 0, 0))],
        out_specs=pl.BlockSpec((MOE_BLOCK, PACKED), lambda b, blk, used: (b, 0)),
    )
    return pl.pallas_call(
        _grouped_kernel,
        grid_spec=grid_spec,
        out_shape=jax.ShapeDtypeStruct((ns, PACKED), I32),
        compiler_params=_params(("arbitrary",)),
        name="grouped",
    )(blk, used, xs, w13, w2)


SC_WINDOW = 128


def _sc_mesh():
    return plsc.VectorSubcoreMesh(core_axis_name="core", subcore_axis_name="subcore")


def _sc_worker(n_items):
    info = plsc.get_sparse_core_info()
    n_workers = info.num_cores * info.num_subcores
    wid = lax.axis_index("subcore") * info.num_cores + lax.axis_index("core")
    return wid, n_items // (SC_WINDOW * n_workers)


def _scatter_rows(x, pos, n_out):
    n, width = x.shape
    picks = pos.shape[0]

    @functools.partial(
        pl.kernel, mesh=_sc_mesh(), out_type=jax.ShapeDtypeStruct((n_out, width), I32),
        scratch_types=[pltpu.VMEM((picks, SC_WINDOW), I32), pltpu.VMEM((SC_WINDOW, width), I32)],
        name="scatter_rows")
    def scatter(x_hbm, pos_hbm, out_hbm, idx_v, rows_v):
        wid, n_win = _sc_worker(n)

        @pl.loop(0, n_win)
        def _(j):
            base = (wid * n_win + j) * SC_WINDOW
            pltpu.sync_copy(pos_hbm.at[:, pl.ds(base, SC_WINDOW)], idx_v)
            pltpu.sync_copy(x_hbm.at[pl.ds(base, SC_WINDOW)], rows_v)
            for k in range(picks):
                pltpu.sync_copy(rows_v, out_hbm.at[idx_v.at[k]])

    return scatter(x, pos)


def _gather_rows(src, pos):
    width = src.shape[1]
    picks, n = pos.shape

    @functools.partial(
        pl.kernel, mesh=_sc_mesh(), out_type=jax.ShapeDtypeStruct((picks * n, width), I32),
        scratch_types=[pltpu.VMEM((SC_WINDOW,), I32), pltpu.VMEM((SC_WINDOW, width), I32)],
        name="gather_rows")
    def gather(src_hbm, pos_hbm, out_hbm, idx_v, rows_v):
        wid, n_win = _sc_worker(picks * n)

        @pl.loop(0, n_win)
        def _(j):
            base = (wid * n_win + j) * SC_WINDOW
            pltpu.sync_copy(pos_hbm.at[pl.ds(base, SC_WINDOW)], idx_v)
            pltpu.sync_copy(src_hbm.at[idx_v], rows_v)
            pltpu.sync_copy(rows_v, out_hbm.at[pl.ds(base, SC_WINDOW)])

    return gather(src, pos.reshape(-1)).reshape(picks, n, width)


def _combine_kernel(h_ref, g_ref, gate_ref, pe_ref, ws13_ref, ws2_ref, ln_g_ref, ln_b_ref, wpg_ref, wpi_ref, o_ref, *,
                    alpha):
    h = h_ref[...]
    y = _swiglu(h.astype(BF16), ws13_ref[...], ws2_ref[...], SHARED_DIM)
    gate = gate_ref[...]
    for k in range(TOP_K_EXPERTS):
        y = y + gate[:, k:k + 1] * _unpack_rows(g_ref[k]).astype(F32)
    z = _layer_norm(alpha * h + y, ln_g_ref[...], ln_b_ref[...])
    ple_gate = jax.nn.sigmoid(jnp.dot(z.astype(BF16), wpg_ref[...], preferred_element_type=F32))
    emb = jnp.dot(pe_ref[...].astype(BF16), wpi_ref[...], preferred_element_type=F32)
    o_ref[...] = z + ple_gate * emb


def _combine(h, gathered, gate, pe, ws13, ws2, g, b, wpg, wpi, tm, alpha):
    n = h.shape[0]
    row = lambda w: pl.BlockSpec((tm, w), lambda i: (i, 0))
    full = lambda r, c: pl.BlockSpec((r, c), lambda i: (0, 0))
    return pl.pallas_call(
        functools.partial(_combine_kernel, alpha=alpha),
        grid=(n // tm,),
        in_specs=[row(D_MODEL), pl.BlockSpec((TOP_K_EXPERTS, tm, PACKED), lambda i: (0, i, 0)), row(LANES),
                  row(PLE_DIM), full(D_MODEL, 2 * SHARED_DIM), full(SHARED_DIM, D_MODEL),
                  full(1, D_MODEL), full(1, D_MODEL), full(D_MODEL, D_MODEL), full(PLE_DIM, D_MODEL)],
        out_specs=row(D_MODEL),
        out_shape=jax.ShapeDtypeStruct((n, D_MODEL), F32),
        compiler_params=_params(("parallel",)),
        name="combine",
    )(h, gathered, gate, pe, ws13, ws2, g, b, wpg, wpi)


def _rope_table(pos):
    inv = ROPE_THETA ** (-jnp.arange(0, HEAD_DIM, 2, dtype=F32) / HEAD_DIM)
    ang = pos.astype(F32)[:, None] * inv[None, :]
    return jnp.concatenate([jnp.tile(jnp.cos(ang), (1, 4)), jnp.tile(jnp.sin(ang), (1, 4))], axis=1)


def _rotate_half_cols(w, n_heads):
    w3 = w.reshape(w.shape[0], n_heads, HEAD_DIM)
    half = HEAD_DIM // 2
    return jnp.concatenate([-w3[..., half:], w3[..., :half]], axis=-1).reshape(w.shape)


def _fused_in_weight(w_in):
    offs = np.cumsum(IN_SIZES)[:-1].tolist()
    wq, wk, wv, wqi, wki, wwi, wu, wga, wgb = jnp.split(w_in, offs, axis=1)
    pad = jnp.zeros((D_MODEL, LANES - HEAD_DIM - IDX_HEADS), w_in.dtype)
    w_big = jnp.concatenate(
        [wq, _rotate_half_cols(wq, N_HEADS), wqi, _rotate_half_cols(wqi, IDX_HEADS),
         wk, wki, _rotate_half_cols(wk, 1), _rotate_half_cols(wki, 1), wv, wwi, pad, wu], axis=1).astype(BF16)
    return w_big, w_big[:, 0:C_U].T, wga.astype(BF16), wgb.astype(BF16)


def _pages_transposed(cache):
    return jnp.transpose(cache[0], (0, 2, 1))


def _heads_major(a, n_heads):
    b, t, w = a.shape
    d = w // n_heads
    return a.reshape(b, t, n_heads, d).transpose(0, 2, 1, 3).reshape(b, n_heads * t, d)


def kernel(x_prompt, x_sample, cache_k, cache_v, cache_kidx, state_pool, page_table, p_prompt, p_sample, w_in, w_att_out, w_pool_grp, pool_scale, w_pool_out, w_out, ln1_g, ln1_b, w_router, router_bias, w_exp13, w_exp2, w_sh13, w_sh2, ln2_g, ln2_b, w_ple_in, w_ple_gate):
    B, S, D = x_prompt.shape
    DB, T, _ = x_sample.shape
    depth = w_in.shape[0]
    assert depth == 1, "single layer step"
    page = cache_k.shape[2]
    past = page_table.shape[1] * page
    alpha = (2 * depth) ** 0.25
    n_p, n_s = B * S, DB * T

    w_big, w_t, wga, wgb = _fused_in_weight(w_in[0])
    wao, wpo, wo = w_att_out[0].astype(BF16), w_pool_out[0].astype(BF16), w_out[0].astype(BF16)
    wgrp = w_pool_grp[0].astype(BF16)
    pscale = pool_scale[0].reshape(1, POOL_WIDTH)
    g1, b1 = ln1_g[0].reshape(1, D), ln1_b[0].reshape(1, D)
    g2, b2 = ln2_g[0].reshape(1, D), ln2_b[0].reshape(1, D)
    wr_t = w_router[0].T.astype(BF16)
    rbias = router_bias[0].reshape(N_EXPERTS, 1)
    w13, w2 = w_exp13[0], w_exp2[0]
    ws13, ws2 = w_sh13[0].astype(BF16), w_sh2[0].astype(BF16)
    wpg, wpi = w_ple_gate[0].astype(BF16), w_ple_in[0].astype(BF16)

    cs_p = _rope_table(jnp.arange(S, dtype=I32))
    cs_s = jnp.tile(_rope_table(past + jnp.arange(T, dtype=I32)), (DB, 1))

    xp = x_prompt.reshape(n_p, D)
    qt, qit, wit, kb, kib, vbt, kt, vt, kit, u = _proj_prompt(xp, w_big, w_t, cs_p, S, 512)
    attn_p = _attn_prompt(qt, qit, wit, kb, kib, vbt)
    u3 = u.reshape(B, S, POOL_WIDTH)
    pool_p = _pool(jnp.zeros((B, PREV_ROWS, POOL_WIDTH), F32), u3, wgrp, pscale, 0).reshape(n_p, POOL_WIDTH)
    h_p, hp_p = _merge(xp, attn_p, pool_p, wga, wgb, wao, wpo, wo, g1, b1, 512, alpha)

    xs = x_sample.reshape(n_s, D)
    qs, qis, ks, vs, kis, wis, us = _proj_sample(xs, w_big, cs_s)
    q_hq = _heads_major(qs.reshape(DB, T, ATT_WIDTH), N_HEADS)
    qi_hq = _heads_major(qis.reshape(DB, T, IDX_HEADS * IDX_DIM), IDX_HEADS)
    wi_hq = wis.reshape(DB, T, IDX_HEADS).transpose(0, 2, 1).reshape(DB, IDX_HEADS * T, 1)
    o_hq = _attn_sample(page_table, q_hq, qi_hq, wi_hq, ks.reshape(DB, T, HEAD_DIM), vs.reshape(DB, T, HEAD_DIM),
                        kis.reshape(DB, T, IDX_DIM), _pages_transposed(cache_k), _pages_transposed(cache_v),
                        _pages_transposed(cache_kidx))
    attn_s = o_hq.reshape(DB, N_HEADS, T, HEAD_DIM).transpose(0, 2, 1, 3).reshape(n_s, ATT_WIDTH).astype(BF16)
    us3 = us.reshape(DB, T, POOL_WIDTH)
    prev_s = jnp.concatenate([jnp.zeros((DB, PREV_ROWS - POOL_STATE, POOL_WIDTH), F32), state_pool[0]], axis=1)
    pool_s = _pool(prev_s, us3, wgrp, pscale, past).reshape(n_s, POOL_WIDTH)
    h_s, _ = _merge(xs, attn_s, pool_s, wga, wgb, wao, wpo, wo, g1, b1, n_s, alpha)

    def tail(h, pe, tm_r, tm_m, tm_f):
        comb = _router(h, wr_t, rbias, tm_r)
        y = _moe(h, comb, ws13, ws2, w13, w2, tm_m)
        return _final(h, y, pe, g2, b2, wpg, wpi, tm_f, alpha)

    y_s = tail(h_s, p_sample[0].reshape(n_s, PLE_DIM), n_s, n_s, n_s)

    pos, gate, blk, used = _dispatch(h_p, wr_t, rbias, 1024)
    sorted_in = _scatter_rows(hp_p, pos, _sorted_rows(n_p))
    sorted_out = _grouped(blk.reshape(-1), used.reshape(-1), sorted_in, w13, w2)
    gathered = _gather_rows(sorted_out, pos)
    y_p = _combine(h_p, gathered, gate, p_prompt[0].reshape(n_p, PLE_DIM), ws13, ws2, g2, b2, wpg, wpi, 512, alpha)

    ext_s = jnp.concatenate([state_pool[0], us3], axis=1)
    return (y_p.reshape(B, S, D), y_s.reshape(DB, T, D),
            jnp.transpose(kt, (0, 2, 1))[None], jnp.transpose(vt, (0, 2, 1))[None],
            jnp.transpose(kit, (0, 2, 1))[None],
            u3[:, S - POOL_STATE:][None],
            ks.reshape(1, DB, T, HEAD_DIM), vs.reshape(1, DB, T, HEAD_DIM), kis.reshape(1, DB, T, IDX_DIM),
            ext_s[:, T:][None])
```

```python
import functools

import numpy as np
import jax
import jax.numpy as jnp
from jax import lax
from jax.experimental import pallas as pl
from jax.experimental.pallas import tpu as pltpu
from jax.experimental.pallas import tpu_sc as plsc

F32 = jnp.float32
BF16 = jnp.bfloat16
I32 = jnp.int32

D_MODEL = 1024
N_HEADS = 8
HEAD_DIM = 64
ATT_WIDTH = N_HEADS * HEAD_DIM
IDX_HEADS = 4
IDX_DIM = 64
TOP_K_MAX = 256
Q_BLOCK = 128
ROPE_THETA = 10000.0
POOL_WINDOWS = (2, 4, 8, 16)
POOL_GROUPS = 4
POOL_WIDTH = 512
POOL_GW = POOL_WIDTH // POOL_GROUPS
POOL_STATE = 15
N_EXPERTS = 64
TOP_K_EXPERTS = 8
N_GROUPS = 8
GROUP_SIZE = N_EXPERTS // N_GROUPS
TOPK_GROUPS = 4
EXPERT_DIM = 256
SHARED_DIM = 256
ROUTED_SCALE = 2.5
PLE_DIM = 256
LN_EPS = 1e-5
IN_SIZES = (ATT_WIDTH, HEAD_DIM, HEAD_DIM, IDX_HEADS * IDX_DIM, IDX_DIM, IDX_HEADS, POOL_WIDTH, D_MODEL, D_MODEL)

LANES = 128
SUBLANES = 8
INT_MIN = -2147483648
NEG_BIG = -1e30
VMEM_LIMIT = 56 * 1024 * 1024

C_Q, C_QR = 0, 512
C_QI, C_QIR = 1024, 1280
C_KK, C_KKR = 1536, 1664
C_VW = 1792
C_U = 1920
C_END = 2432

NT_DIMS = (((1,), (1,)), ((), ()))

Q_SCALE = HEAD_DIM ** -0.5 * float(np.log2(np.e))
QI_SCALE = IDX_DIM ** -0.5


def _params(sem):
    return pltpu.CompilerParams(dimension_semantics=sem, vmem_limit_bytes=VMEM_LIMIT)


def _layer_norm(x, g, b):
    mu = jnp.mean(x, axis=-1, keepdims=True)
    xc = x - mu
    var = jnp.mean(xc * xc, axis=-1, keepdims=True)
    return xc * lax.rsqrt(var + LN_EPS) * g + b


def _proj_sample_kernel(x_ref, w_ref, cs_ref, q_ref, qi_ref, k_ref, v_ref, ki_ref, wi_ref, u_ref):
    xb = x_ref[...].astype(BF16)
    cos = cs_ref[:, 0:LANES]
    sin = cs_ref[:, LANES:2 * LANES]

    def mm(c0, n):
        return jnp.dot(xb, w_ref[:, c0:c0 + n], preferred_element_type=F32)

    def rope(c0, c0r, n):
        reps = n // LANES
        cosn = jnp.concatenate([cos] * reps, axis=1) if reps > 1 else cos
        sinn = jnp.concatenate([sin] * reps, axis=1) if reps > 1 else sin
        return mm(c0, n) * cosn + mm(c0r, n) * sinn

    q_ref[...] = (rope(C_Q, C_QR, ATT_WIDTH) * Q_SCALE).astype(BF16)
    qi_ref[...] = (rope(C_QI, C_QIR, IDX_HEADS * IDX_DIM) * QI_SCALE).astype(BF16)
    kk = rope(C_KK, C_KKR, LANES)
    k_ref[...] = kk[:, 0:HEAD_DIM]
    ki_ref[...] = kk[:, HEAD_DIM:2 * HEAD_DIM]
    vw = mm(C_VW, LANES)
    v_ref[...] = vw[:, 0:HEAD_DIM]
    wi_ref[...] = vw[:, HEAD_DIM:HEAD_DIM + IDX_HEADS] * (IDX_HEADS ** -0.5)
    u_ref[...] = mm(C_U, POOL_WIDTH)


def _proj_sample(x, w_big, cs):
    n = x.shape[0]
    full = lambda r, c: pl.BlockSpec((r, c), lambda i: (0, 0))
    widths = (ATT_WIDTH, IDX_HEADS * IDX_DIM, HEAD_DIM, HEAD_DIM, IDX_DIM, IDX_HEADS, POOL_WIDTH)
    dtypes = (BF16, BF16, F32, F32, F32, F32, F32)
    return pl.pallas_call(
        _proj_sample_kernel,
        grid=(1,),
        in_specs=[full(n, D_MODEL), full(D_MODEL, C_END), full(n, 2 * LANES)],
        out_specs=tuple(full(n, w) for w in widths),
        out_shape=tuple(jax.ShapeDtypeStruct((n, w), dt) for w, dt in zip(widths, dtypes)),
        compiler_params=_params(("arbitrary",)),
        name="proj_sample",
    )(x, w_big, cs)


def _proj_prompt_kernel(x_ref, w_ref, wt_ref, cs_ref, cst_ref, qt_ref, qit_ref, wit_ref, kb_ref, kib_ref, vbt_ref,
                        kt_ref, vt_ref, kit_ref, u_ref):
    xb = x_ref[...].astype(BF16)
    tm = xb.shape[0]
    cos = cs_ref[:, 0:LANES]
    sin = cs_ref[:, LANES:2 * LANES]
    cos_t = cst_ref[0:HEAD_DIM, :]
    sin_t = cst_ref[LANES:LANES + HEAD_DIM, :]

    def mm(c0, n):
        return jnp.dot(xb, w_ref[:, c0:c0 + n], preferred_element_type=F32)

    def mm_t(c0, n):
        return lax.dot_general(wt_ref[c0:c0 + n, :], xb, NT_DIMS, preferred_element_type=F32)

    def rope_t(c0, c0r, heads):
        cosn = jnp.concatenate([cos_t] * heads, axis=0) if heads > 1 else cos_t
        sinn = jnp.concatenate([sin_t] * heads, axis=0) if heads > 1 else sin_t
        return mm_t(c0, heads * HEAD_DIM) * cosn + mm_t(c0r, heads * HEAD_DIM) * sinn

    kk = mm(C_KK, LANES) * cos + mm(C_KKR, LANES) * sin
    kb_ref[...] = kk[:, 0:HEAD_DIM].astype(BF16)
    kib_ref[...] = kk[:, HEAD_DIM:2 * HEAD_DIM].astype(BF16)
    u_ref[...] = mm(C_U, POOL_WIDTH)

    qt = (rope_t(C_Q, C_QR, N_HEADS) * Q_SCALE).astype(BF16)
    qit = (rope_t(C_QI, C_QIR, IDX_HEADS) * QI_SCALE).astype(BF16)
    for blk in range(tm // Q_BLOCK):
        cols = slice(blk * Q_BLOCK, (blk + 1) * Q_BLOCK)
        for h in range(N_HEADS):
            qt_ref[blk, :, h * Q_BLOCK:(h + 1) * Q_BLOCK] = qt[h * HEAD_DIM:(h + 1) * HEAD_DIM, cols]
        for h in range(IDX_HEADS):
            qit_ref[blk, :, h * Q_BLOCK:(h + 1) * Q_BLOCK] = qit[h * IDX_DIM:(h + 1) * IDX_DIM, cols]

    kkt = rope_t(C_KK, C_KKR, 2)
    kt_ref[...] = kkt[0:HEAD_DIM, :]
    kit_ref[...] = kkt[HEAD_DIM:2 * HEAD_DIM, :]
    vwt = mm_t(C_VW, LANES)
    vt_ref[...] = vwt[0:HEAD_DIM, :]
    vbt_ref[...] = vwt[0:HEAD_DIM, :].astype(BF16)
    wit_ref[...] = vwt[HEAD_DIM:HEAD_DIM + SUBLANES, :] * (IDX_HEADS ** -0.5)


def _proj_prompt(x, w_big, w_t, cs, seq, tm):
    n = x.shape[0]
    nb = seq // tm
    qb = tm // Q_BLOCK
    row = lambda w: pl.BlockSpec((tm, w), lambda i: (i, 0))
    col = lambda r: pl.BlockSpec((None, r, tm), lambda i: (i // nb, 0, i % nb))
    slab = lambda heads: pl.BlockSpec((qb, HEAD_DIM, heads * Q_BLOCK), lambda i: (i, 0, 0))
    pm = lambda r, dt: jax.ShapeDtypeStruct((n // seq, r, seq), dt)
    out_shape = (
        jax.ShapeDtypeStruct((n // Q_BLOCK, HEAD_DIM, N_HEADS * Q_BLOCK), BF16),
        jax.ShapeDtypeStruct((n // Q_BLOCK, IDX_DIM, IDX_HEADS * Q_BLOCK), BF16),
        pm(SUBLANES, F32),
        jax.ShapeDtypeStruct((n, HEAD_DIM), BF16), jax.ShapeDtypeStruct((n, IDX_DIM), BF16),
        pm(HEAD_DIM, BF16),
        pm(HEAD_DIM, F32), pm(HEAD_DIM, F32), pm(IDX_DIM, F32),
        jax.ShapeDtypeStruct((n, POOL_WIDTH), F32),
    )
    return pl.pallas_call(
        _proj_prompt_kernel,
        grid=(n // tm,),
        in_specs=[
            row(D_MODEL),
            pl.BlockSpec((D_MODEL, C_END), lambda i: (0, 0)),
            pl.BlockSpec((C_U, D_MODEL), lambda i: (0, 0)),
            pl.BlockSpec((tm, 2 * LANES), lambda i: (i % nb, 0)),
            pl.BlockSpec((2 * LANES, tm), lambda i: (0, i % nb)),
        ],
        out_specs=(slab(N_HEADS), slab(IDX_HEADS), col(SUBLANES), row(HEAD_DIM), row(IDX_DIM), col(HEAD_DIM),
                   col(HEAD_DIM), col(HEAD_DIM), col(IDX_DIM), row(POOL_WIDTH)),
        out_shape=out_shape,
        compiler_params=_params(("parallel",)),
        name="proj_prompt",
    )(x, w_big, w_t, cs, cs.T)


def _float_of_rank(u):
    key = u ^ INT_MIN
    bits = jnp.where(key < 0, INT_MIN - key, key)
    return pltpu.bitcast(bits, F32)


def _count(mask):
    return jnp.sum(mask.astype(F32), axis=1, keepdims=True)


def _topk_bias(sc_ref, j_ref, adm, n_adm, lc, k):
    rows = sc_ref.shape[0]
    kf = float(k)

    def value_step(i, t_u):
        hi = jnp.left_shift(jnp.int32(1), 31 - 2 * i)
        lo = jnp.left_shift(jnp.int32(1), 30 - 2 * i)
        for cand_u in (t_u | lo, t_u | hi, t_u | hi | lo):
            cnt = _count(sc_ref[:, 0:lc] >= _float_of_rank(cand_u))
            t_u = jnp.where(cnt >= kf, cand_u, t_u)
        return t_u

    t_u = lax.fori_loop(0, 16, value_step, jnp.zeros((rows, 1), I32))
    few = n_adm < k
    thr = jnp.where(few, -jnp.inf, _float_of_rank(t_u))
    sc = sc_ref[:, 0:lc]
    cnt_gt = _count(sc > thr)
    cnt_eq = _count(sc == thr)
    need = kf - cnt_gt
    cut_needed = jnp.logical_and(cnt_gt + cnt_eq > kf, jnp.logical_not(few))
    any_cut = jnp.max(cut_needed.astype(F32)) > 0.0
    idx = lax.broadcasted_iota(I32, (rows, lc), 1)
    nbits = int(np.ceil(np.log2(lc)))

    j_ref[...] = jnp.full((rows, 1), lc, I32)

    @pl.when(any_cut)
    def _():
        def index_step(i, j):
            cand = j | jnp.left_shift(jnp.int32(1), nbits - 1 - i)
            c = _count(jnp.logical_and(sc_ref[:, 0:lc] == thr, idx < cand))
            return jnp.where(c < need, cand, j)

        j_ref[...] = lax.fori_loop(0, nbits, index_step, jnp.zeros((rows, 1), I32))

    sel = jnp.logical_or(sc > thr, jnp.logical_and(sc == thr, idx <= j_ref[...]))
    return jnp.where(jnp.logical_and(sel, adm), 0.0, NEG_BIG)


ATTN_CHUNK = 256


def _attn_prompt_block(n_chunks, q0, top_k, qt_ref, qit_ref, wit_ref, kb_ref, kib_ref, vbt_ref, o_ref,
                       key_ref, bias_ref, lg_ref, j_ref):
    tq, ch = Q_BLOCK, ATTN_CHUNK
    seq = key_ref.shape[0]
    kf = float(top_k)
    kpos = lax.broadcasted_iota(I32, (ch, tq), 0)
    qpos = q0 + lax.broadcasted_iota(I32, (ch, tq), 1)

    def rows(c):
        return slice(c * ch, (c + 1) * ch)

    def fold(x, op):
        return op(x.reshape(ch // SUBLANES, SUBLANES, tq), axis=0)

    def head(x, h):
        return x[:, h * tq:(h + 1) * tq]

    qit = qit_ref[...]
    wit = wit_ref[...]
    for c in range(n_chunks if n_chunks * ch > top_k else 0):
        d = jnp.dot(kib_ref[rows(c), :], qit, preferred_element_type=F32)
        s = wit[0:1, :] * jnp.maximum(head(d, 0), 0.0)
        for h in range(1, IDX_HEADS):
            s = s + wit[h:h + 1, :] * jnp.maximum(head(d, h), 0.0)
        key_ref[rows(c), :] = jnp.where(c * ch + kpos <= qpos, s, -jnp.inf)

    def count(pred):
        acc = jnp.zeros((SUBLANES, tq), F32)
        for c in range(n_chunks):
            acc = acc + fold(pred(key_ref[rows(c), :], c).astype(F32), jnp.sum)
        return jnp.sum(acc, axis=0, keepdims=True)

    if n_chunks * ch <= top_k:
        for c in range(n_chunks):
            bias_ref[rows(c), :] = jnp.where(c * ch + kpos <= qpos, 0.0, NEG_BIG)
    else:
        def value_step(i, carry):
            t_u, n_ge = carry
            cand_u = t_u | jnp.left_shift(jnp.int32(1), 31 - i)
            cand = _float_of_rank(cand_u)
            cnt = count(lambda k, c: k >= cand)
            ok = cnt >= kf
            return jnp.where(ok, cand_u, t_u), jnp.where(ok, cnt, n_ge)

        t_u, n_ge = lax.fori_loop(0, 32, value_step,
                                  (jnp.zeros((1, tq), I32), jnp.full((1, tq), float(n_chunks * ch), F32)))
        few = qpos[0:1, :] + 1 <= top_k
        thr = jnp.where(few, -jnp.inf, _float_of_rank(t_u))
        cut_needed = jnp.logical_and(n_ge > kf, jnp.logical_not(few))
        any_cut = jnp.max(cut_needed.astype(F32)) > 0.0

        nbits = int(np.ceil(np.log2(seq)))
        j_ref[...] = jnp.full(j_ref.shape, seq, I32)

        @pl.when(any_cut)
        def _():
            need = kf - count(lambda k, c: k > thr)

            def index_step(i, j):
                cand = j | jnp.left_shift(jnp.int32(1), nbits - 1 - i)
                n_before = count(lambda k, c: jnp.logical_and(k == thr, c * ch + kpos < cand))
                return jnp.where(n_before < need, cand, j)

            j = lax.fori_loop(0, nbits, index_step, jnp.zeros((1, tq), I32))
            j_ref[...] = jnp.broadcast_to(j, j_ref.shape)

        j_cut = j_ref[0:1, :]
        for c in range(n_chunks):
            k = key_ref[rows(c), :]
            pos = c * ch + kpos
            sel = jnp.logical_or(k > thr, jnp.logical_and(k == thr, pos <= j_cut))
            bias_ref[rows(c), :] = jnp.where(jnp.logical_and(sel, pos <= qpos), 0.0, NEG_BIG)

    qt = qt_ref[...]
    mx = [jnp.full((SUBLANES, tq), -jnp.inf, F32) for _ in range(N_HEADS)]
    for c in range(n_chunks):
        lg = jnp.dot(kb_ref[rows(c), :], qt, preferred_element_type=F32)
        bias = bias_ref[rows(c), :]
        for h in range(N_HEADS):
            lgh = head(lg, h) + bias
            lg_ref[h, rows(c), :] = lgh
            mx[h] = jnp.maximum(mx[h], fold(lgh, jnp.max))

    outs = []
    for h in range(N_HEADS):
        m = jnp.max(mx[h], axis=0, keepdims=True)
        lsum = jnp.zeros((SUBLANES, tq), F32)
        ot = jnp.zeros((HEAD_DIM, tq), F32)
        for c in range(n_chunks):
            p = jnp.exp2(lg_ref[h, rows(c), :] - m)
            lsum = lsum + fold(p, jnp.sum)
            ot = ot + jnp.dot(vbt_ref[:, rows(c)], p.astype(BF16), preferred_element_type=F32)
        outs.append(ot / jnp.sum(lsum, axis=0, keepdims=True))
    o_ref[...] = jnp.concatenate(outs, axis=0).T.astype(BF16)


def _attn_prompt_kernel(qt_ref, qit_ref, wit_ref, kb_ref, kib_ref, vbt_ref, o_ref, key_ref, bias_ref, lg_ref, j_ref,
                        *, top_k):
    jq = pl.program_id(1)
    blocks_per_chunk = ATTN_CHUNK // Q_BLOCK
    n_classes = key_ref.shape[0] // ATTN_CHUNK
    for cls in range(n_classes):
        @pl.when(jq // blocks_per_chunk == cls)
        def _(cls=cls):
            _attn_prompt_block(cls + 1, jq * Q_BLOCK, top_k, qt_ref, qit_ref, wit_ref, kb_ref, kib_ref, vbt_ref,
                               o_ref, key_ref, bias_ref, lg_ref, j_ref)


def _attn_prompt(qt, qit, wit, kb, kib, vbt):
    batch, _, seq = vbt.shape
    nb = seq // Q_BLOCK
    top_k = min(TOP_K_MAX, seq // 4)
    slab = lambda heads: pl.BlockSpec((None, HEAD_DIM, heads * Q_BLOCK), lambda b, j: (b * nb + j, 0, 0))
    keys = pl.BlockSpec((seq, HEAD_DIM), lambda b, j: (b, 0))
    return pl.pallas_call(
        functools.partial(_attn_prompt_kernel, top_k=top_k),
        grid=(batch, nb),
        in_specs=[slab(N_HEADS), slab(IDX_HEADS), pl.BlockSpec((None, SUBLANES, Q_BLOCK), lambda b, j: (b, 0, j)),
                  keys, keys, pl.BlockSpec((None, HEAD_DIM, seq), lambda b, j: (b, 0, 0))],
        out_specs=pl.BlockSpec((Q_BLOCK, ATT_WIDTH), lambda b, j: (b * nb + j, 0)),
        out_shape=jax.ShapeDtypeStruct((batch * seq, ATT_WIDTH), BF16),
        scratch_shapes=[pltpu.VMEM((seq, Q_BLOCK), F32), pltpu.VMEM((seq, Q_BLOCK), F32),
                        pltpu.VMEM((N_HEADS, seq, Q_BLOCK), F32), pltpu.VMEM((SUBLANES, Q_BLOCK), I32)],
        compiler_params=_params(("parallel", "arbitrary")),
        name="attn_prompt",
    )(qt, qit, wit, kb, kib, vbt)


SAMPLE_CHUNK = 1024


def _attn_sample_kernel(pt_ref, q_ref, qi_ref, wi_ref, kn_ref, vn_ref, kin_ref, ck_hbm, cv_hbm, cki_hbm, o_ref,
                        kbuf, vbuf, kibuf, sem, key_scr, bias_scr, lg_scr, j_scr, *, n_pages, page, t_new, top_k):
    b = pl.program_id(0)
    n_b = pl.num_programs(0)
    slot = b % 2
    past = n_pages * page
    lc = past + page
    n_chunks = past // SAMPLE_CHUNK

    def page_copies(bb, sl, p):
        phys = pt_ref[bb * n_pages + p]
        dst = pl.ds(pl.multiple_of(p * page, page), page)
        return [pltpu.make_async_copy(src.at[phys], buf.at[sl, :, dst], sem.at[i, sl])
                for i, (src, buf) in enumerate(((ck_hbm, kbuf), (cv_hbm, vbuf), (cki_hbm, kibuf)))]

    def start_batch(bb, sl):
        def body(p, carry):
            for cp in page_copies(bb, sl, p):
                cp.start()
            return carry
        lax.fori_loop(0, n_pages, body, 0)

    def wait_batch(bb, sl):
        def body(p, carry):
            for cp in page_copies(bb, sl, p):
                cp.wait()
            return carry
        lax.fori_loop(0, n_pages, body, 0)

    @pl.when(b == 0)
    def _():
        start_batch(0, 0)

    @pl.when(b + 1 < n_b)
    def _():
        start_batch(b + 1, 1 - slot)

    wait_batch(b, slot)

    def head_sum(d):
        r = wi_ref[...] * jnp.maximum(d, 0.0)
        s = r[0:t_new]
        for h in range(1, IDX_HEADS):
            s = s + r[h * t_new:(h + 1) * t_new]
        return s

    def new_rows(ref):
        pad = jnp.zeros((page - t_new, ref.shape[1]), F32)
        return jnp.concatenate([ref[...], pad], axis=0).astype(BF16)

    qi = qi_ref[...]
    for c in range(n_chunks):
        sl = slice(c * SAMPLE_CHUNK, (c + 1) * SAMPLE_CHUNK)
        d = jnp.dot(qi, kibuf[slot, :, sl].astype(BF16), preferred_element_type=F32)
        key_scr[:, sl] = head_sum(d)
    d_new = lax.dot_general(qi, new_rows(kin_ref), NT_DIMS, preferred_element_type=F32)
    adm_new = lax.broadcasted_iota(I32, (t_new, page), 1) <= lax.broadcasted_iota(I32, (t_new, page), 0)
    key_scr[:, past:lc] = jnp.where(adm_new, head_sum(d_new), -jnp.inf)

    idx = lax.broadcasted_iota(I32, (t_new, lc), 1)
    trow = lax.broadcasted_iota(I32, (t_new, lc), 0)
    n_adm = past + 1 + lax.broadcasted_iota(I32, (t_new, 1), 0)
    bias_scr[...] = _topk_bias(key_scr, j_scr, idx - past <= trow, n_adm, lc, top_k)

    q = q_ref[...]

    def bias_rows(sl):
        return jnp.concatenate([bias_scr[:, sl]] * N_HEADS, axis=0)

    m = jnp.full((N_HEADS * t_new, 1), -jnp.inf, F32)
    for c in range(n_chunks):
        sl = slice(c * SAMPLE_CHUNK, (c + 1) * SAMPLE_CHUNK)
        lg = jnp.dot(q, kbuf[slot, :, sl].astype(BF16), preferred_element_type=F32) + bias_rows(sl)
        lg_scr[:, sl] = lg
        m = jnp.maximum(m, jnp.max(lg, axis=1, keepdims=True))
    lg_new = lax.dot_general(q, new_rows(kn_ref), NT_DIMS, preferred_element_type=F32) + bias_rows(slice(past, lc))
    m = jnp.maximum(m, jnp.max(lg_new, axis=1, keepdims=True))

    p_new = jnp.exp2(lg_new - m)
    l = jnp.sum(p_new, axis=1, keepdims=True)
    o = jnp.dot(p_new.astype(BF16), new_rows(vn_ref), preferred_element_type=F32)
    for c in range(n_chunks):
        sl = slice(c * SAMPLE_CHUNK, (c + 1) * SAMPLE_CHUNK)
        pr = jnp.exp2(lg_scr[:, sl] - m)
        l = l + jnp.sum(pr, axis=1, keepdims=True)
        o = o + lax.dot_general(pr.astype(BF16), vbuf[slot, :, sl].astype(BF16), NT_DIMS,
                                preferred_element_type=F32)
    o_ref[...] = o / l


def _attn_sample(page_table, q_hq, qi_hq, wi_hq, k_new, v_new, ki_new, cache_kt, cache_vt, cache_kit):
    db, n_pages = page_table.shape
    page = cache_kt.shape[2]
    t_new = k_new.shape[1]
    past = n_pages * page
    lc = past + page
    top_k = min(TOP_K_MAX, (past + t_new) // 4)
    per_b = lambda r, w: pl.BlockSpec((None, r, w), lambda b, pt: (b, 0, 0))
    hbm = pl.BlockSpec(memory_space=pl.ANY)
    kern = functools.partial(_attn_sample_kernel, n_pages=n_pages, page=page, t_new=t_new, top_k=top_k)
    slab = pltpu.VMEM((2, HEAD_DIM, past), F32)
    grid_spec = pltpu.PrefetchScalarGridSpec(
        num_scalar_prefetch=1,
        grid=(db,),
        in_specs=[per_b(N_HEADS * t_new, HEAD_DIM), per_b(IDX_HEADS * t_new, IDX_DIM), per_b(IDX_HEADS * t_new, 1),
                  per_b(t_new, HEAD_DIM), per_b(t_new, HEAD_DIM), per_b(t_new, IDX_DIM),
                  hbm, hbm, hbm],
        out_specs=per_b(N_HEADS * t_new, HEAD_DIM),
        scratch_shapes=[slab, slab, slab, pltpu.SemaphoreType.DMA((3, 2)),
                        pltpu.VMEM((t_new, lc), F32), pltpu.VMEM((t_new, lc), F32),
                        pltpu.VMEM((N_HEADS * t_new, past), F32), pltpu.VMEM((t_new, 1), I32)],
    )
    return pl.pallas_call(
        kern,
        grid_spec=grid_spec,
        out_shape=jax.ShapeDtypeStruct((db, N_HEADS * t_new, HEAD_DIM), F32),
        compiler_params=_params(("arbitrary",)),
        name="attn_sample",
    )(page_table.reshape(-1), q_hq, qi_hq, wi_hq, k_new, v_new, ki_new, cache_kt, cache_vt, cache_kit)


PREV_ROWS = 16


def _pool_kernel(prev_ref, u_ref, wg_ref, sc_ref, o_ref, ext_ref, *, pos0):
    per_step, t_len, _ = u_ref.shape
    pos = pos0 + lax.broadcasted_iota(I32, (t_len, 1), 0)
    for b in range(per_step):
        ext_ref[0:PREV_ROWS, :] = prev_ref[b]
        ext_ref[PREV_ROWS:PREV_ROWS + t_len, :] = u_ref[b]
        for g, w in enumerate(POOL_WINDOWS):
            sl = slice(g * POOL_GW, (g + 1) * POOL_GW)
            u_new = ext_ref[PREV_ROWS:PREV_ROWS + t_len, sl]
            win = u_new
            for back in range(1, w):
                win = win + ext_ref[PREV_ROWS - back:PREV_ROWS - back + t_len, sl]
            count = jnp.minimum(pos + 1, w).astype(F32)
            r = win / count - u_new
            mixed = jnp.dot(r.astype(BF16), wg_ref[g], preferred_element_type=F32) * sc_ref[:, sl]
            o_ref[b, :, sl] = mixed.astype(BF16)


def _pool(prev, u, w_grp, scale, pos0, per_step):
    nb, t_len, _ = u.shape
    seqs = lambda rows: pl.BlockSpec((per_step, rows, POOL_WIDTH), lambda b: (b, 0, 0))
    return pl.pallas_call(
        functools.partial(_pool_kernel, pos0=pos0),
        grid=(nb // per_step,),
        in_specs=[seqs(PREV_ROWS), seqs(t_len),
                  pl.BlockSpec((POOL_GROUPS, POOL_GW, POOL_GW), lambda b: (0, 0, 0)),
                  pl.BlockSpec((1, POOL_WIDTH), lambda b: (0, 0))],
        out_specs=seqs(t_len),
        out_shape=jax.ShapeDtypeStruct((nb, t_len, POOL_WIDTH), BF16),
        scratch_shapes=[pltpu.VMEM((PREV_ROWS + t_len, POOL_WIDTH), F32)],
        compiler_params=_params(("parallel",)),
        name="pool",
    )(prev, u, w_grp, scale)


def _merge_kernel(x_ref, a_ref, p_ref, wga_ref, wgb_ref, wao_ref, wpo_ref, wo_ref, g_ref, b_ref, h_ref, hp_ref, *,
                  alpha):
    x = x_ref[...]
    xb = x.astype(BF16)
    ga = jnp.dot(xb, wga_ref[...], preferred_element_type=F32)
    gb = jnp.dot(xb, wgb_ref[...], preferred_element_type=F32)
    ya = jnp.dot(a_ref[...], wao_ref[...], preferred_element_type=F32)
    yp = jnp.dot(p_ref[...], wpo_ref[...], preferred_element_type=F32)
    mix = jax.nn.sigmoid(ga) * ya + jax.nn.sigmoid(gb) * yp
    out = jnp.dot(mix.astype(BF16), wo_ref[...], preferred_element_type=F32)
    h = _layer_norm(alpha * x + out, g_ref[...], b_ref[...])
    h_ref[...] = h
    hp_ref[...] = _pack_rows(h)


def _merge(x, attn, pool, wga, wgb, wao, wpo, wo, g, b, tm, alpha):
    n = x.shape[0]
    row = lambda w: pl.BlockSpec((tm, w), lambda i: (i, 0))
    full = lambda r, c: pl.BlockSpec((r, c), lambda i: (0, 0), pipeline_mode=pl.Buffered(1))
    return pl.pallas_call(
        functools.partial(_merge_kernel, alpha=alpha),
        grid=(n // tm,),
        in_specs=[row(D_MODEL), row(ATT_WIDTH), row(POOL_WIDTH), full(D_MODEL, D_MODEL), full(D_MODEL, D_MODEL),
                  full(ATT_WIDTH, D_MODEL), full(POOL_WIDTH, D_MODEL), full(D_MODEL, D_MODEL),
                  full(1, D_MODEL), full(1, D_MODEL)],
        out_specs=(row(D_MODEL), row(PACKED)),
        out_shape=(jax.ShapeDtypeStruct((n, D_MODEL), F32), jax.ShapeDtypeStruct((n, PACKED), I32)),
        compiler_params=_params(("parallel",)),
        name="merge",
    )(x, attn, pool, wga, wgb, wao, wpo, wo, g, b)


def _route(h, wr_t, bias_col):
    tm = h.shape[0]
    logits = lax.dot_general(wr_t, h.astype(BF16), NT_DIMS, preferred_element_type=F32)
    s = jax.nn.sigmoid(logits)
    sb = s + bias_col
    neg_inf = -jnp.inf

    rows = []
    for g in range(N_GROUPS):
        blk = sb[g * GROUP_SIZE:(g + 1) * GROUP_SIZE, :]
        m1 = jnp.max(blk, axis=0, keepdims=True)
        is_m1 = blk == m1
        n_m1 = jnp.sum(is_m1.astype(F32), axis=0, keepdims=True)
        m2 = jnp.max(jnp.where(is_m1, neg_inf, blk), axis=0, keepdims=True)
        rows.append(m1 + jnp.where(n_m1 >= 2.0, m1, m2))
    gs = jnp.concatenate(rows, axis=0)

    gi = lax.broadcasted_iota(I32, (N_GROUPS, tm), 0)
    rank = jnp.zeros((N_GROUPS, tm), F32)
    for g in range(N_GROUPS):
        row = gs[g:g + 1, :]
        beats = jnp.logical_or(row > gs, jnp.logical_and(row == gs, g < gi))
        rank = rank + beats.astype(F32)
    gkeep = rank < float(TOPK_GROUPS)
    emask = jnp.concatenate(
        [jnp.broadcast_to(gkeep[g:g + 1, :], (GROUP_SIZE, tm)) for g in range(N_GROUPS)], axis=0)

    ei = lax.broadcasted_iota(I32, (N_EXPERTS, tm), 0)
    x = jnp.where(emask, sb, neg_inf)
    sel = jnp.zeros((N_EXPERTS, tm), jnp.bool_)
    picks = []
    for _ in range(TOP_K_EXPERTS):
        m = jnp.max(x, axis=0, keepdims=True)
        first = jnp.min(jnp.where(x == m, ei, N_EXPERTS), axis=0, keepdims=True)
        pick = ei == first
        sel = jnp.logical_or(sel, pick)
        x = jnp.where(pick, neg_inf, x)
        picks.append(first)

    gate = jnp.where(sel, s, 0.0)
    comb = gate / jnp.sum(gate, axis=0, keepdims=True) * ROUTED_SCALE
    return comb, sel, picks


def _router_kernel(h_ref, wr_ref, bias_ref, c_ref):
    comb, _, _ = _route(h_ref[...], wr_ref[...], bias_ref[...])
    comb = jnp.concatenate([comb, jnp.zeros((LANES - N_EXPERTS, comb.shape[1]), F32)], axis=0)
    c_ref[...] = comb.T


def _router(h, wr_t, bias_col, tm):
    n = h.shape[0]
    return pl.pallas_call(
        _router_kernel,
        grid=(n // tm,),
        in_specs=[pl.BlockSpec((tm, D_MODEL), lambda i: (i, 0)),
                  pl.BlockSpec((N_EXPERTS, D_MODEL), lambda i: (0, 0)),
                  pl.BlockSpec((N_EXPERTS, 1), lambda i: (0, 0))],
        out_specs=pl.BlockSpec((tm, LANES), lambda i: (i, 0)),
        out_shape=jax.ShapeDtypeStruct((n, LANES), F32),
        compiler_params=_params(("parallel",)),
        name="router",
    )(h, wr_t, bias_col)


def _swiglu(xb, w13, w2, hidden):
    ab = jnp.dot(xb, w13, preferred_element_type=F32)
    act = jax.nn.silu(ab[:, 0:hidden]) * ab[:, hidden:2 * hidden]
    return jnp.dot(act.astype(BF16), w2, preferred_element_type=F32)


def _moe_kernel(h_ref, c_ref, ws13_ref, ws2_ref, w13_ref, w2_ref, y_ref, hb_ref):
    e = pl.program_id(1)

    @pl.when(e == 0)
    def _():
        hb_ref[...] = h_ref[...].astype(BF16)
        y_ref[...] = _swiglu(hb_ref[...], ws13_ref[...], ws2_ref[...], SHARED_DIM)

    ye = _swiglu(hb_ref[...], w13_ref[...].astype(BF16), w2_ref[...].astype(BF16), EXPERT_DIM)
    lane = lax.broadcasted_iota(I32, c_ref.shape, 1)
    ce = jnp.sum(jnp.where(lane == e, c_ref[...], 0.0), axis=1, keepdims=True)
    y_ref[...] += ce * ye


def _moe(h, comb, ws13, ws2, w13, w2, tm):
    n = h.shape[0]
    return pl.pallas_call(
        _moe_kernel,
        grid=(n // tm, N_EXPERTS),
        in_specs=[pl.BlockSpec((tm, D_MODEL), lambda i, e: (i, 0)),
                  pl.BlockSpec((tm, LANES), lambda i, e: (i, 0)),
                  pl.BlockSpec((D_MODEL, 2 * SHARED_DIM), lambda i, e: (0, 0)),
                  pl.BlockSpec((SHARED_DIM, D_MODEL), lambda i, e: (0, 0)),
                  pl.BlockSpec((None, D_MODEL, 2 * EXPERT_DIM), lambda i, e: (e, 0, 0)),
                  pl.BlockSpec((None, EXPERT_DIM, D_MODEL), lambda i, e: (e, 0, 0))],
        out_specs=pl.BlockSpec((tm, D_MODEL), lambda i, e: (i, 0)),
        out_shape=jax.ShapeDtypeStruct((n, D_MODEL), F32),
        scratch_shapes=[pltpu.VMEM((tm, D_MODEL), BF16)],
        compiler_params=_params(("parallel", "arbitrary")),
        name="moe",
    )(h, comb, ws13, ws2, w13, w2)


def _final_kernel(h_ref, y_ref, pe_ref, g_ref, b_ref, wpg_ref, wpi_ref, o_ref, *, alpha):
    z = _layer_norm(alpha * h_ref[...] + y_ref[...], g_ref[...], b_ref[...])
    gate = jax.nn.sigmoid(jnp.dot(z.astype(BF16), wpg_ref[...], preferred_element_type=F32))
    emb = jnp.dot(pe_ref[...].astype(BF16), wpi_ref[...], preferred_element_type=F32)
    o_ref[...] = z + gate * emb


def _final(h, y, pe, g, b, wpg, wpi, tm, alpha):
    n = h.shape[0]
    row = lambda w: pl.BlockSpec((tm, w), lambda i: (i, 0))
    full = lambda r, c: pl.BlockSpec((r, c), lambda i: (0, 0))
    return pl.pallas_call(
        functools.partial(_final_kernel, alpha=alpha),
        grid=(n // tm,),
        in_specs=[row(D_MODEL), row(D_MODEL), row(PLE_DIM), full(1, D_MODEL), full(1, D_MODEL),
                  full(D_MODEL, D_MODEL), full(PLE_DIM, D_MODEL)],
        out_specs=row(D_MODEL),
        out_shape=jax.ShapeDtypeStruct((n, D_MODEL), F32),
        compiler_params=_params(("parallel",)),
        name="final",
    )(h, y, pe, g, b, wpg, wpi)


MOE_BLOCK = 512
BLOCKS_PER_STEP = 2


def _sorted_rows(n_tokens):
    worst = n_tokens * TOP_K_EXPERTS + N_EXPERTS * (MOE_BLOCK - 1)
    step_rows = MOE_BLOCK * BLOCKS_PER_STEP
    return -(-worst // step_rows) * step_rows


def _dispatch_kernel(h_ref, wr_ref, bias_ref, tri_ref, pos_ref, gate_ref, blk_ref, used_ref,
                     eidx_s, rank_s, gate_s, cnt_s):
    p = pl.program_id(0)
    i = pl.program_id(1)
    tm = h_ref.shape[0]
    ei = lax.broadcasted_iota(I32, (N_EXPERTS, tm), 0)

    @pl.when(p == 0)
    def _():
        comb, sel, picks = _route(h_ref[...], wr_ref[...], bias_ref[...])
        before = jnp.dot(sel.astype(BF16), tri_ref[...], preferred_element_type=F32)
        ranks, gates = [], []
        for first in picks:
            pick = ei == first
            ranks.append(jnp.sum(jnp.where(pick, before, 0.0), axis=0, keepdims=True))
            gates.append(jnp.sum(jnp.where(pick, comb, 0.0), axis=0, keepdims=True))
        eidx_s[i] = jnp.concatenate(picks, axis=0)
        rank_s[i] = jnp.concatenate(ranks, axis=0)
        gate_s[i] = jnp.concatenate(gates, axis=0)
        cnt_s[i] = jnp.broadcast_to(jnp.sum(sel.astype(F32), axis=1, keepdims=True), (N_EXPERTS, LANES))

    @pl.when(p == 1)
    def _():
        cnt = cnt_s[...]
        tile_id = lax.broadcasted_iota(I32, cnt.shape, 0)
        total = jnp.sum(cnt, axis=0)
        prior = jnp.sum(jnp.where(tile_id < i, cnt, 0.0), axis=0)
        seg = jnp.ceil(total * (1.0 / MOE_BLOCK)) * MOE_BLOCK
        lower = (lax.broadcasted_iota(I32, (N_EXPERTS, N_EXPERTS), 1)
                 < lax.broadcasted_iota(I32, (N_EXPERTS, N_EXPERTS), 0)).astype(F32)
        seg_off = jnp.dot(lower, seg, precision=lax.Precision.HIGHEST, preferred_element_type=F32)
        base = (seg_off + prior)[:, 0:1]
        eidx = eidx_s[i]
        rank = rank_s[i]
        rows = []
        for k in range(TOP_K_EXPERTS):
            pick = ei == eidx[k:k + 1, :]
            rows.append(rank[k:k + 1, :] + jnp.sum(jnp.where(pick, base, 0.0), axis=0, keepdims=True))
        pos_ref[...] = jnp.concatenate(rows, axis=0).astype(I32)
        gate_ref[...] = jnp.concatenate([gate_s[i], jnp.zeros((LANES - TOP_K_EXPERTS, tm), F32)], axis=0).T

        seg_end = (seg_off + seg)[:, 0:1]
        n_blk = blk_ref.shape[1]
        blk_start = (lax.broadcasted_iota(I32, (N_EXPERTS, n_blk), 1) * MOE_BLOCK).astype(F32)
        owner = jnp.sum((seg_end <= blk_start).astype(F32), axis=0, keepdims=True)
        blk_ref[...] = jnp.minimum(owner, N_EXPERTS - 1.0).astype(I32)
        used = seg_end[N_EXPERTS - 1:N_EXPERTS, :] * (1.0 / MOE_BLOCK)
        used_ref[...] = jnp.broadcast_to(used, used_ref.shape).astype(I32)


def _dispatch(h, wr_t, bias_col, tm):
    n = h.shape[0]
    n_tiles = n // tm
    n_blk = _sorted_rows(n) // MOE_BLOCK
    n_blk_pad = -(-n_blk // LANES) * LANES
    tri = jnp.triu(jnp.ones((tm, tm), BF16), k=1)
    const = lambda r, c: pl.BlockSpec((r, c), lambda p, i: (0, 0))
    per_tile = lambda dt: pltpu.VMEM((n_tiles, TOP_K_EXPERTS, tm), dt)
    return pl.pallas_call(
        _dispatch_kernel,
        grid=(2, n_tiles),
        in_specs=[pl.BlockSpec((tm, D_MODEL), lambda p, i: (i * (1 - p), 0)),
                  const(N_EXPERTS, D_MODEL), const(N_EXPERTS, 1), const(tm, tm)],
        out_specs=(pl.BlockSpec((TOP_K_EXPERTS, tm), lambda p, i: (0, i * p)),
                   pl.BlockSpec((tm, LANES), lambda p, i: (i * p, 0)),
                   const(1, n_blk_pad), const(1, LANES)),
        out_shape=(jax.ShapeDtypeStruct((TOP_K_EXPERTS, n), I32), jax.ShapeDtypeStruct((n, LANES), F32),
                   jax.ShapeDtypeStruct((1, n_blk_pad), I32), jax.ShapeDtypeStruct((1, LANES), I32)),
        scratch_shapes=[per_tile(I32), per_tile(F32), per_tile(F32), pltpu.VMEM((n_tiles, N_EXPERTS, LANES), F32)],
        compiler_params=_params(("arbitrary", "arbitrary")),
        name="dispatch",
    )(h, wr_t, bias_col, tri)


PACKED = D_MODEL // 2


def _pack_rows(x):
    lo = pltpu.bitcast(x[:, 0:PACKED].astype(BF16).astype(F32), I32)
    hi = pltpu.bitcast(x[:, PACKED:D_MODEL].astype(BF16).astype(F32), I32)
    return jnp.bitwise_or(hi, lax.shift_right_logical(lo, 16))


def _unpack_rows(w):
    lo = pltpu.bitcast(lax.shift_left(w, 16), F32)
    hi = pltpu.bitcast(jnp.bitwise_and(w, -65536), F32)
    return jnp.concatenate([lo, hi], axis=1).astype(BF16)


def _grouped_kernel(blk_ref, used_ref, xs_ref, *refs):
    ys_ref = refs[-1]
    for j in range(BLOCKS_PER_STEP):
        w13_ref, w2_ref = refs[2 * j], refs[2 * j + 1]
        rows = slice(j * MOE_BLOCK, (j + 1) * MOE_BLOCK)
        b = pl.program_id(0) * BLOCKS_PER_STEP + j

        @pl.when(b < used_ref[0])
        def _(w13_ref=w13_ref, w2_ref=w2_ref, rows=rows):
            ys = _swiglu(_unpack_rows(xs_ref[rows, :]), w13_ref[...].astype(BF16), w2_ref[...].astype(BF16),
                         EXPERT_DIM)
            ys_ref[rows, :] = _pack_rows(ys)

        @pl.when(b >= used_ref[0])
        def _(rows=rows):
            ys_ref[rows, :] = jnp.zeros((MOE_BLOCK, PACKED), I32)


def _grouped(blk, used, xs, w13, w2):
    ns = xs.shape[0]
    step_rows = MOE_BLOCK * BLOCKS_PER_STEP
    weights = []
    for j in range(BLOCKS_PER_STEP):
        expert = lambda s, blk, used, j=j: (blk[s * BLOCKS_PER_STEP + j], 0, 0)
        weights += [pl.BlockSpec((None, D_MODEL, 2 * EXPERT_DIM), expert),
                    pl.BlockSpec((None, EXPERT_DIM, D_MODEL), expert)]
    grid_spec = pltpu.PrefetchScalarGridSpec(
        num_scalar_prefetch=2,
        grid=(ns // step_rows,),
        in_specs=[pl.BlockSpec((step_rows, PACKED), lambda s, blk, used: (s, 0))] + weights,
        out_specs=pl.BlockSpec((step_rows, PACKED), lambda s, blk, used: (s, 0)),
    )
    return pl.pallas_call(
        _grouped_kernel,
        grid_spec=grid_spec,
        out_shape=jax.ShapeDtypeStruct((ns, PACKED), I32),
        compiler_params=_params(("arbitrary",)),
        name="grouped",
    )(blk, used, xs, *([w13, w2] * BLOCKS_PER_STEP))


SC_WINDOW = 128


def _sc_mesh():
    return plsc.VectorSubcoreMesh(core_axis_name="core", subcore_axis_name="subcore")


def _sc_worker(n_items):
    info = plsc.get_sparse_core_info()
    n_workers = info.num_cores * info.num_subcores
    wid = lax.axis_index("subcore") * info.num_cores + lax.axis_index("core")
    return wid, n_items // (SC_WINDOW * n_workers)


def _scatter_rows(x, pos, n_out):
    n, width = x.shape
    picks = pos.shape[0]

    @functools.partial(
        pl.kernel, mesh=_sc_mesh(), out_type=jax.ShapeDtypeStruct((n_out, width), I32),
        scratch_types=[pltpu.VMEM((picks, SC_WINDOW), I32), pltpu.VMEM((SC_WINDOW, width), I32)],
        name="scatter_rows")
    def scatter(x_hbm, pos_hbm, out_hbm, idx_v, rows_v):
        wid, n_win = _sc_worker(n)

        @pl.loop(0, n_win)
        def _(j):
            base = (wid * n_win + j) * SC_WINDOW
            pltpu.sync_copy(pos_hbm.at[:, pl.ds(base, SC_WINDOW)], idx_v)
            pltpu.sync_copy(x_hbm.at[pl.ds(base, SC_WINDOW)], rows_v)
            for k in range(picks):
                pltpu.sync_copy(rows_v, out_hbm.at[idx_v.at[k]])

    return scatter(x, pos)


def _gather_rows(src, pos):
    width = src.shape[1]
    picks, n = pos.shape

    @functools.partial(
        pl.kernel, mesh=_sc_mesh(), out_type=jax.ShapeDtypeStruct((picks * n, width), I32),
        scratch_types=[pltpu.VMEM((SC_WINDOW,), I32), pltpu.VMEM((SC_WINDOW, width), I32)],
        name="gather_rows")
    def gather(src_hbm, pos_hbm, out_hbm, idx_v, rows_v):
        wid, n_win = _sc_worker(picks * n)

        @pl.loop(0, n_win)
        def _(j):
            base = (wid * n_win + j) * SC_WINDOW
            pltpu.sync_copy(pos_hbm.at[pl.ds(base, SC_WINDOW)], idx_v)
            pltpu.sync_copy(src_hbm.at[idx_v], rows_v)
            pltpu.sync_copy(rows_v, out_hbm.at[pl.ds(base, SC_WINDOW)])

    return gather(src, pos.reshape(-1)).reshape(picks, n, width)


def _combine_kernel(h_ref, g_ref, gate_ref, pe_ref, ws13_ref, ws2_ref, ln_g_ref, ln_b_ref, wpg_ref, wpi_ref, o_ref, *,
                    alpha):
    h = h_ref[...]
    y = _swiglu(h.astype(BF16), ws13_ref[...], ws2_ref[...], SHARED_DIM)
    gate = gate_ref[...]
    for k in range(TOP_K_EXPERTS):
        y = y + gate[:, k:k + 1] * _unpack_rows(g_ref[k]).astype(F32)
    z = _layer_norm(alpha * h + y, ln_g_ref[...], ln_b_ref[...])
    ple_gate = jax.nn.sigmoid(jnp.dot(z.astype(BF16), wpg_ref[...], preferred_element_type=F32))
    emb = jnp.dot(pe_ref[...].astype(BF16), wpi_ref[...], preferred_element_type=F32)
    o_ref[...] = z + ple_gate * emb


def _combine(h, gathered, gate, pe, ws13, ws2, g, b, wpg, wpi, tm, alpha):
    n = h.shape[0]
    row = lambda w: pl.BlockSpec((tm, w), lambda i: (i, 0))
    full = lambda r, c: pl.BlockSpec((r, c), lambda i: (0, 0))
    return pl.pallas_call(
        functools.partial(_combine_kernel, alpha=alpha),
        grid=(n // tm,),
        in_specs=[row(D_MODEL), pl.BlockSpec((TOP_K_EXPERTS, tm, PACKED), lambda i: (0, i, 0)), row(LANES),
                  row(PLE_DIM), full(D_MODEL, 2 * SHARED_DIM), full(SHARED_DIM, D_MODEL),
                  full(1, D_MODEL), full(1, D_MODEL), full(D_MODEL, D_MODEL), full(PLE_DIM, D_MODEL)],
        out_specs=row(D_MODEL),
        out_shape=jax.ShapeDtypeStruct((n, D_MODEL), F32),
        compiler_params=_params(("parallel",)),
        name="combine",
    )(h, gathered, gate, pe, ws13, ws2, g, b, wpg, wpi)


def _rope_table(pos):
    inv = ROPE_THETA ** (-jnp.arange(0, HEAD_DIM, 2, dtype=F32) / HEAD_DIM)
    ang = pos.astype(F32)[:, None] * inv[None, :]
    return jnp.concatenate([jnp.tile(jnp.cos(ang), (1, 4)), jnp.tile(jnp.sin(ang), (1, 4))], axis=1)


def _rotate_half_cols(w, n_heads):
    w3 = w.reshape(w.shape[0], n_heads, HEAD_DIM)
    half = HEAD_DIM // 2
    return jnp.concatenate([-w3[..., half:], w3[..., :half]], axis=-1).reshape(w.shape)


def _fused_in_weight(w_in):
    offs = np.cumsum(IN_SIZES)[:-1].tolist()
    wq, wk, wv, wqi, wki, wwi, wu, wga, wgb = jnp.split(w_in, offs, axis=1)
    pad = jnp.zeros((D_MODEL, LANES - HEAD_DIM - IDX_HEADS), w_in.dtype)
    w_big = jnp.concatenate(
        [wq, _rotate_half_cols(wq, N_HEADS), wqi, _rotate_half_cols(wqi, IDX_HEADS),
         wk, wki, _rotate_half_cols(wk, 1), _rotate_half_cols(wki, 1), wv, wwi, pad, wu], axis=1).astype(BF16)
    return w_big, w_big[:, 0:C_U].T, wga.astype(BF16), wgb.astype(BF16)


def _pages_transposed(cache):
    return jnp.transpose(cache[0], (0, 2, 1))


def _heads_major(a, n_heads):
    b, t, w = a.shape
    d = w // n_heads
    return a.reshape(b, t, n_heads, d).transpose(0, 2, 1, 3).reshape(b, n_heads * t, d)


def kernel(x_prompt, x_sample, cache_k, cache_v, cache_kidx, state_pool, page_table, p_prompt, p_sample, w_in, w_att_out, w_pool_grp, pool_scale, w_pool_out, w_out, ln1_g, ln1_b, w_router, router_bias, w_exp13, w_exp2, w_sh13, w_sh2, ln2_g, ln2_b, w_ple_in, w_ple_gate):
    B, S, D = x_prompt.shape
    DB, T, _ = x_sample.shape
    depth = w_in.shape[0]
    assert depth == 1, "single layer step"
    page = cache_k.shape[2]
    past = page_table.shape[1] * page
    alpha = (2 * depth) ** 0.25
    n_p, n_s = B * S, DB * T

    w_big, w_t, wga, wgb = _fused_in_weight(w_in[0])
    wao, wpo, wo = w_att_out[0].astype(BF16), w_pool_out[0].astype(BF16), w_out[0].astype(BF16)
    wgrp = w_pool_grp[0].astype(BF16)
    pscale = pool_scale[0].reshape(1, POOL_WIDTH)
    g1, b1 = ln1_g[0].reshape(1, D), ln1_b[0].reshape(1, D)
    g2, b2 = ln2_g[0].reshape(1, D), ln2_b[0].reshape(1, D)
    wr_t = w_router[0].T.astype(BF16)
    rbias = router_bias[0].reshape(N_EXPERTS, 1)
    w13, w2 = w_exp13[0], w_exp2[0]
    ws13, ws2 = w_sh13[0].astype(BF16), w_sh2[0].astype(BF16)
    wpg, wpi = w_ple_gate[0].astype(BF16), w_ple_in[0].astype(BF16)

    cs_p = _rope_table(jnp.arange(S, dtype=I32))
    cs_s = jnp.tile(_rope_table(past + jnp.arange(T, dtype=I32)), (DB, 1))

    xp = x_prompt.reshape(n_p, D)
    qt, qit, wit, kb, kib, vbt, kt, vt, kit, u = _proj_prompt(xp, w_big, w_t, cs_p, S, 512)
    attn_p = _attn_prompt(qt, qit, wit, kb, kib, vbt)
    u3 = u.reshape(B, S, POOL_WIDTH)
    pool_p = _pool(jnp.zeros((B, PREV_ROWS, POOL_WIDTH), F32), u3, wgrp, pscale, 0, 1).reshape(n_p, POOL_WIDTH)
    h_p, hp_p = _merge(xp, attn_p, pool_p, wga, wgb, wao, wpo, wo, g1, b1, 1024, alpha)

    xs = x_sample.reshape(n_s, D)
    qs, qis, ks, vs, kis, wis, us = _proj_sample(xs, w_big, cs_s)
    q_hq = _heads_major(qs.reshape(DB, T, ATT_WIDTH), N_HEADS)
    qi_hq = _heads_major(qis.reshape(DB, T, IDX_HEADS * IDX_DIM), IDX_HEADS)
    wi_hq = wis.reshape(DB, T, IDX_HEADS).transpose(0, 2, 1).reshape(DB, IDX_HEADS * T, 1)
    o_hq = _attn_sample(page_table, q_hq, qi_hq, wi_hq, ks.reshape(DB, T, HEAD_DIM), vs.reshape(DB, T, HEAD_DIM),
                        kis.reshape(DB, T, IDX_DIM), _pages_transposed(cache_k), _pages_transposed(cache_v),
                        _pages_transposed(cache_kidx))
    attn_s = o_hq.reshape(DB, N_HEADS, T, HEAD_DIM).transpose(0, 2, 1, 3).reshape(n_s, ATT_WIDTH).astype(BF16)
    us3 = us.reshape(DB, T, POOL_WIDTH)
    prev_s = jnp.concatenate([jnp.zeros((DB, PREV_ROWS - POOL_STATE, POOL_WIDTH), F32), state_pool[0]], axis=1)
    pool_s = _pool(prev_s, us3, wgrp, pscale, past, DB).reshape(n_s, POOL_WIDTH)
    h_s, _ = _merge(xs, attn_s, pool_s, wga, wgb, wao, wpo, wo, g1, b1, n_s, alpha)

    def tail(h, pe, tm_r, tm_m, tm_f):
        comb = _router(h, wr_t, rbias, tm_r)
        y = _moe(h, comb, ws13, ws2, w13, w2, tm_m)
        return _final(h, y, pe, g2, b2, wpg, wpi, tm_f, alpha)

    y_s = tail(h_s, p_sample[0].reshape(n_s, PLE_DIM), n_s, n_s, n_s)

    pos, gate, blk, used = _dispatch(h_p, wr_t, rbias, 1024)
    sorted_in = _scatter_rows(hp_p, pos, _sorted_rows(n_p))
    sorted_out = _grouped(blk.reshape(-1), used.reshape(-1), sorted_in, w13, w2)
    gathered = _gather_rows(sorted_out, pos)
    y_p = _combine(h_p, gathered, gate, p_prompt[0].reshape(n_p, PLE_DIM), ws13, ws2, g2, b2, wpg, wpi, 512, alpha)

    ext_s = jnp.concatenate([state_pool[0], us3], axis=1)
    return (y_p.reshape(B, S, D), y_s.reshape(DB, T, D),
            jnp.transpose(kt, (0, 2, 1))[None], jnp.transpose(vt, (0, 2, 1))[None],
            jnp.transpose(kit, (0, 2, 1))[None],
            u3[:, S - POOL_STATE:][None],
            ks.reshape(1, DB, T, HEAD_DIM), vs.reshape(1, DB, T, HEAD_DIM), kis.reshape(1, DB, T, IDX_DIM),
            ext_s[:, T:][None])
```

```python
import functools

import numpy as np
import jax
import jax.numpy as jnp
from jax import lax
from jax.experimental import pallas as pl
from jax.experimental.pallas import tpu as pltpu
from jax.experimental.pallas import tpu_sc as plsc

F32 = jnp.float32
BF16 = jnp.bfloat16
I32 = jnp.int32

D_MODEL = 1024
N_HEADS = 8
HEAD_DIM = 64
ATT_WIDTH = N_HEADS * HEAD_DIM
IDX_HEADS = 4
IDX_DIM = 64
TOP_K_MAX = 256
Q_BLOCK = 128
ROPE_THETA = 10000.0
POOL_WINDOWS = (2, 4, 8, 16)
POOL_GROUPS = 4
POOL_WIDTH = 512
POOL_GW = POOL_WIDTH // POOL_GROUPS
POOL_STATE = 15
N_EXPERTS = 64
TOP_K_EXPERTS = 8
N_GROUPS = 8
GROUP_SIZE = N_EXPERTS // N_GROUPS
TOPK_GROUPS = 4
EXPERT_DIM = 256
SHARED_DIM = 256
ROUTED_SCALE = 2.5
PLE_DIM = 256
LN_EPS = 1e-5
IN_SIZES = (ATT_WIDTH, HEAD_DIM, HEAD_DIM, IDX_HEADS * IDX_DIM, IDX_DIM, IDX_HEADS, POOL_WIDTH, D_MODEL, D_MODEL)

LANES = 128
SUBLANES = 8
INT_MIN = -2147483648
NEG_BIG = -1e30
VMEM_LIMIT = 56 * 1024 * 1024

C_Q, C_QR = 0, 512
C_QI, C_QIR = 1024, 1280
C_KK, C_KKR = 1536, 1664
C_VW = 1792
C_U = 1920
C_END = 2432

NT_DIMS = (((1,), (1,)), ((), ()))

Q_SCALE = HEAD_DIM ** -0.5 * float(np.log2(np.e))
QI_SCALE = IDX_DIM ** -0.5


def _params(sem):
    return pltpu.CompilerParams(dimension_semantics=sem, vmem_limit_bytes=VMEM_LIMIT)


def _layer_norm(x, g, b):
    mu = jnp.mean(x, axis=-1, keepdims=True)
    xc = x - mu
    var = jnp.mean(xc * xc, axis=-1, keepdims=True)
    return xc * lax.rsqrt(var + LN_EPS) * g + b


def _proj_sample_kernel(x_ref, w_ref, cs_ref, q_ref, qi_ref, k_ref, v_ref, ki_ref, wi_ref, u_ref):
    xb = x_ref[...].astype(BF16)
    cos = cs_ref[:, 0:LANES]
    sin = cs_ref[:, LANES:2 * LANES]

    def mm(c0, n):
        return jnp.dot(xb, w_ref[:, c0:c0 + n], preferred_element_type=F32)

    def rope(c0, c0r, n):
        reps = n // LANES
        cosn = jnp.concatenate([cos] * reps, axis=1) if reps > 1 else cos
        sinn = jnp.concatenate([sin] * reps, axis=1) if reps > 1 else sin
        return mm(c0, n) * cosn + mm(c0r, n) * sinn

    q_ref[...] = (rope(C_Q, C_QR, ATT_WIDTH) * Q_SCALE).astype(BF16)
    qi_ref[...] = (rope(C_QI, C_QIR, IDX_HEADS * IDX_DIM) * QI_SCALE).astype(BF16)
    kk = rope(C_KK, C_KKR, LANES)
    k_ref[...] = kk[:, 0:HEAD_DIM]
    ki_ref[...] = kk[:, HEAD_DIM:2 * HEAD_DIM]
    vw = mm(C_VW, LANES)
    v_ref[...] = vw[:, 0:HEAD_DIM]
    wi_ref[...] = vw[:, HEAD_DIM:HEAD_DIM + IDX_HEADS] * (IDX_HEADS ** -0.5)
    u_ref[...] = mm(C_U, POOL_WIDTH)


def _proj_sample(x, w_big, cs):
    n = x.shape[0]
    full = lambda r, c: pl.BlockSpec((r, c), lambda i: (0, 0))
    widths = (ATT_WIDTH, IDX_HEADS * IDX_DIM, HEAD_DIM, HEAD_DIM, IDX_DIM, IDX_HEADS, POOL_WIDTH)
    dtypes = (BF16, BF16, F32, F32, F32, F32, F32)
    return pl.pallas_call(
        _proj_sample_kernel,
        grid=(1,),
        in_specs=[full(n, D_MODEL), full(D_MODEL, C_END), full(n, 2 * LANES)],
        out_specs=tuple(full(n, w) for w in widths),
        out_shape=tuple(jax.ShapeDtypeStruct((n, w), dt) for w, dt in zip(widths, dtypes)),
        compiler_params=_params(("arbitrary",)),
        name="proj_sample",
    )(x, w_big, cs)


def _proj_prompt_kernel(x_ref, w_ref, wt_ref, cs_ref, cst_ref, qt_ref, qit_ref, wit_ref, kb_ref, kib_ref, vbt_ref,
                        kt_ref, vt_ref, kit_ref, u_ref):
    xb = x_ref[...].astype(BF16)
    tm = xb.shape[0]
    cos = cs_ref[:, 0:LANES]
    sin = cs_ref[:, LANES:2 * LANES]
    cos_t = cst_ref[0:HEAD_DIM, :]
    sin_t = cst_ref[LANES:LANES + HEAD_DIM, :]

    def mm(c0, n):
        return jnp.dot(xb, w_ref[:, c0:c0 + n], preferred_element_type=F32)

    def mm_t(c0, n):
        return lax.dot_general(wt_ref[c0:c0 + n, :], xb, NT_DIMS, preferred_element_type=F32)

    def rope_t(c0, c0r, heads):
        cosn = jnp.concatenate([cos_t] * heads, axis=0) if heads > 1 else cos_t
        sinn = jnp.concatenate([sin_t] * heads, axis=0) if heads > 1 else sin_t
        return mm_t(c0, heads * HEAD_DIM) * cosn + mm_t(c0r, heads * HEAD_DIM) * sinn

    kk = mm(C_KK, LANES) * cos + mm(C_KKR, LANES) * sin
    kb_ref[...] = kk[:, 0:HEAD_DIM].astype(BF16)
    kib_ref[...] = kk[:, HEAD_DIM:2 * HEAD_DIM].astype(BF16)
    u_ref[...] = mm(C_U, POOL_WIDTH)

    qt = (rope_t(C_Q, C_QR, N_HEADS) * Q_SCALE).astype(BF16)
    qit = (rope_t(C_QI, C_QIR, IDX_HEADS) * QI_SCALE).astype(BF16)
    for blk in range(tm // Q_BLOCK):
        cols = slice(blk * Q_BLOCK, (blk + 1) * Q_BLOCK)
        for h in range(N_HEADS):
            qt_ref[blk, :, h * Q_BLOCK:(h + 1) * Q_BLOCK] = qt[h * HEAD_DIM:(h + 1) * HEAD_DIM, cols]
        for h in range(IDX_HEADS):
            qit_ref[blk, :, h * Q_BLOCK:(h + 1) * Q_BLOCK] = qit[h * IDX_DIM:(h + 1) * IDX_DIM, cols]

    kkt = rope_t(C_KK, C_KKR, 2)
    kt_ref[...] = kkt[0:HEAD_DIM, :]
    kit_ref[...] = kkt[HEAD_DIM:2 * HEAD_DIM, :]
    vwt = mm_t(C_VW, LANES)
    vt_ref[...] = vwt[0:HEAD_DIM, :]
    vbt_ref[...] = vwt[0:HEAD_DIM, :].astype(BF16)
    wit_ref[...] = vwt[HEAD_DIM:HEAD_DIM + SUBLANES, :] * (IDX_HEADS ** -0.5)


def _proj_prompt(x, w_big, w_t, cs, seq, tm):
    n = x.shape[0]
    nb = seq // tm
    qb = tm // Q_BLOCK
    row = lambda w: pl.BlockSpec((tm, w), lambda i: (i, 0))
    col = lambda r: pl.BlockSpec((None, r, tm), lambda i: (i // nb, 0, i % nb))
    slab = lambda heads: pl.BlockSpec((qb, HEAD_DIM, heads * Q_BLOCK), lambda i: (i, 0, 0))
    pm = lambda r, dt: jax.ShapeDtypeStruct((n // seq, r, seq), dt)
    out_shape = (
        jax.ShapeDtypeStruct((n // Q_BLOCK, HEAD_DIM, N_HEADS * Q_BLOCK), BF16),
        jax.ShapeDtypeStruct((n // Q_BLOCK, IDX_DIM, IDX_HEADS * Q_BLOCK), BF16),
        pm(SUBLANES, F32),
        jax.ShapeDtypeStruct((n, HEAD_DIM), BF16), jax.ShapeDtypeStruct((n, IDX_DIM), BF16),
        pm(HEAD_DIM, BF16),
        pm(HEAD_DIM, F32), pm(HEAD_DIM, F32), pm(IDX_DIM, F32),
        jax.ShapeDtypeStruct((n, POOL_WIDTH), F32),
    )
    return pl.pallas_call(
        _proj_prompt_kernel,
        grid=(n // tm,),
        in_specs=[
            row(D_MODEL),
            pl.BlockSpec((D_MODEL, C_END), lambda i: (0, 0)),
            pl.BlockSpec((C_U, D_MODEL), lambda i: (0, 0)),
            pl.BlockSpec((tm, 2 * LANES), lambda i: (i % nb, 0)),
            pl.BlockSpec((2 * LANES, tm), lambda i: (0, i % nb)),
        ],
        out_specs=(slab(N_HEADS), slab(IDX_HEADS), col(SUBLANES), row(HEAD_DIM), row(IDX_DIM), col(HEAD_DIM),
                   col(HEAD_DIM), col(HEAD_DIM), col(IDX_DIM), row(POOL_WIDTH)),
        out_shape=out_shape,
        compiler_params=_params(("parallel",)),
        name="proj_prompt",
    )(x, w_big, w_t, cs, cs.T)


def _float_of_rank(u):
    key = u ^ INT_MIN
    bits = jnp.where(key < 0, INT_MIN - key, key)
    return pltpu.bitcast(bits, F32)


def _count(mask):
    return jnp.sum(mask.astype(F32), axis=1, keepdims=True)


def _topk_bias(sc_ref, j_ref, adm, n_adm, lc, k):
    rows = sc_ref.shape[0]
    kf = float(k)

    def value_step(i, t_u):
        hi = jnp.left_shift(jnp.int32(1), 31 - 2 * i)
        lo = jnp.left_shift(jnp.int32(1), 30 - 2 * i)
        for cand_u in (t_u | lo, t_u | hi, t_u | hi | lo):
            cnt = _count(sc_ref[:, 0:lc] >= _float_of_rank(cand_u))
            t_u = jnp.where(cnt >= kf, cand_u, t_u)
        return t_u

    t_u = lax.fori_loop(0, 16, value_step, jnp.zeros((rows, 1), I32))
    few = n_adm < k
    thr = jnp.where(few, -jnp.inf, _float_of_rank(t_u))
    sc = sc_ref[:, 0:lc]
    cnt_gt = _count(sc > thr)
    cnt_eq = _count(sc == thr)
    need = kf - cnt_gt
    cut_needed = jnp.logical_and(cnt_gt + cnt_eq > kf, jnp.logical_not(few))
    any_cut = jnp.max(cut_needed.astype(F32)) > 0.0
    idx = lax.broadcasted_iota(I32, (rows, lc), 1)
    nbits = int(np.ceil(np.log2(lc)))

    j_ref[...] = jnp.full((rows, 1), lc, I32)

    @pl.when(any_cut)
    def _():
        def index_step(i, j):
            cand = j | jnp.left_shift(jnp.int32(1), nbits - 1 - i)
            c = _count(jnp.logical_and(sc_ref[:, 0:lc] == thr, idx < cand))
            return jnp.where(c < need, cand, j)

        j_ref[...] = lax.fori_loop(0, nbits, index_step, jnp.zeros((rows, 1), I32))

    sel = jnp.logical_or(sc > thr, jnp.logical_and(sc == thr, idx <= j_ref[...]))
    return jnp.where(jnp.logical_and(sel, adm), 0.0, NEG_BIG)


ATTN_CHUNK = 256


def _attn_prompt_block(n_chunks, q0, top_k, qt_ref, qit_ref, wit_ref, kb_ref, kib_ref, vbt_ref, o_ref,
                       key_ref, bias_ref, lg_ref, j_ref):
    tq, ch = Q_BLOCK, ATTN_CHUNK
    seq = key_ref.shape[0]
    kf = float(top_k)
    kpos = lax.broadcasted_iota(I32, (ch, tq), 0)
    qpos = q0 + lax.broadcasted_iota(I32, (ch, tq), 1)

    def rows(c):
        return slice(c * ch, (c + 1) * ch)

    def fold(x, op):
        return op(x.reshape(ch // SUBLANES, SUBLANES, tq), axis=0)

    def head(x, h):
        return x[:, h * tq:(h + 1) * tq]

    qit = qit_ref[...]
    wit = wit_ref[...]
    for c in range(n_chunks if n_chunks * ch > top_k else 0):
        d = jnp.dot(kib_ref[rows(c), :], qit, preferred_element_type=F32)
        s = wit[0:1, :] * jnp.maximum(head(d, 0), 0.0)
        for h in range(1, IDX_HEADS):
            s = s + wit[h:h + 1, :] * jnp.maximum(head(d, h), 0.0)
        key_ref[rows(c), :] = jnp.where(c * ch + kpos <= qpos, s, -jnp.inf)

    def count(pred):
        acc = jnp.zeros((SUBLANES, tq), F32)
        for c in range(n_chunks):
            acc = acc + fold(pred(key_ref[rows(c), :], c).astype(F32), jnp.sum)
        return jnp.sum(acc, axis=0, keepdims=True)

    if n_chunks * ch <= top_k:
        for c in range(n_chunks):
            bias_ref[rows(c), :] = jnp.where(c * ch + kpos <= qpos, 0.0, NEG_BIG)
    else:
        def value_step(i, carry):
            t_u, n_ge = carry
            cand_u = t_u | jnp.left_shift(jnp.int32(1), 31 - i)
            cand = _float_of_rank(cand_u)
            cnt = count(lambda k, c: k >= cand)
            ok = cnt >= kf
            return jnp.where(ok, cand_u, t_u), jnp.where(ok, cnt, n_ge)

        t_u, n_ge = lax.fori_loop(0, 32, value_step,
                                  (jnp.zeros((1, tq), I32), jnp.full((1, tq), float(n_chunks * ch), F32)))
        few = qpos[0:1, :] + 1 <= top_k
        thr = jnp.where(few, -jnp.inf, _float_of_rank(t_u))
        cut_needed = jnp.logical_and(n_ge > kf, jnp.logical_not(few))
        any_cut = jnp.max(cut_needed.astype(F32)) > 0.0

        nbits = int(np.ceil(np.log2(seq)))
        j_ref[...] = jnp.full(j_ref.shape, seq, I32)

        @pl.when(any_cut)
        def _():
            need = kf - count(lambda k, c: k > thr)

            def index_step(i, j):
                cand = j | jnp.left_shift(jnp.int32(1), nbits - 1 - i)
                n_before = count(lambda k, c: jnp.logical_and(k == thr, c * ch + kpos < cand))
                return jnp.where(n_before < need, cand, j)

            j = lax.fori_loop(0, nbits, index_step, jnp.zeros((1, tq), I32))
            j_ref[...] = jnp.broadcast_to(j, j_ref.shape)

        j_cut = j_ref[0:1, :]
        for c in range(n_chunks):
            k = key_ref[rows(c), :]
            pos = c * ch + kpos
            sel = jnp.logical_or(k > thr, jnp.logical_and(k == thr, pos <= j_cut))
            bias_ref[rows(c), :] = jnp.where(jnp.logical_and(sel, pos <= qpos), 0.0, NEG_BIG)

    qt = qt_ref[...]
    mx = [jnp.full((SUBLANES, tq), -jnp.inf, F32) for _ in range(N_HEADS)]
    for c in range(n_chunks):
        lg = jnp.dot(kb_ref[rows(c), :], qt, preferred_element_type=F32)
        bias = bias_ref[rows(c), :]
        for h in range(N_HEADS):
            lgh = head(lg, h) + bias
            lg_ref[h, rows(c), :] = lgh
            mx[h] = jnp.maximum(mx[h], fold(lgh, jnp.max))

    outs = []
    for h in range(N_HEADS):
        m = jnp.max(mx[h], axis=0, keepdims=True)
        lsum = jnp.zeros((SUBLANES, tq), F32)
        ot = jnp.zeros((HEAD_DIM, tq), F32)
        for c in range(n_chunks):
            p = jnp.exp2(lg_ref[h, rows(c), :] - m)
            lsum = lsum + fold(p, jnp.sum)
            ot = ot + jnp.dot(vbt_ref[:, rows(c)], p.astype(BF16), preferred_element_type=F32)
        outs.append(ot / jnp.sum(lsum, axis=0, keepdims=True))
    o_ref[...] = jnp.concatenate(outs, axis=0).T.astype(BF16)


def _attn_prompt_kernel(qt_ref, qit_ref, wit_ref, kb_ref, kib_ref, vbt_ref, o_ref, key_ref, bias_ref, lg_ref, j_ref,
                        *, top_k):
    jq = pl.program_id(1)
    blocks_per_chunk = ATTN_CHUNK // Q_BLOCK
    n_classes = key_ref.shape[0] // ATTN_CHUNK
    for cls in range(n_classes):
        @pl.when(jq // blocks_per_chunk == cls)
        def _(cls=cls):
            _attn_prompt_block(cls + 1, jq * Q_BLOCK, top_k, qt_ref, qit_ref, wit_ref, kb_ref, kib_ref, vbt_ref,
                               o_ref, key_ref, bias_ref, lg_ref, j_ref)


def _attn_prompt(qt, qit, wit, kb, kib, vbt):
    batch, _, seq = vbt.shape
    nb = seq // Q_BLOCK
    top_k = min(TOP_K_MAX, seq // 4)
    slab = lambda heads: pl.BlockSpec((None, HEAD_DIM, heads * Q_BLOCK), lambda b, j: (b * nb + j, 0, 0))
    keys = pl.BlockSpec((seq, HEAD_DIM), lambda b, j: (b, 0))
    return pl.pallas_call(
        functools.partial(_attn_prompt_kernel, top_k=top_k),
        grid=(batch, nb),
        in_specs=[slab(N_HEADS), slab(IDX_HEADS), pl.BlockSpec((None, SUBLANES, Q_BLOCK), lambda b, j: (b, 0, j)),
                  keys, keys, pl.BlockSpec((None, HEAD_DIM, seq), lambda b, j: (b, 0, 0))],
        out_specs=pl.BlockSpec((Q_BLOCK, ATT_WIDTH), lambda b, j: (b * nb + j, 0)),
        out_shape=jax.ShapeDtypeStruct((batch * seq, ATT_WIDTH), BF16),
        scratch_shapes=[pltpu.VMEM((seq, Q_BLOCK), F32), pltpu.VMEM((seq, Q_BLOCK), F32),
                        pltpu.VMEM((N_HEADS, seq, Q_BLOCK), F32), pltpu.VMEM((SUBLANES, Q_BLOCK), I32)],
        compiler_params=_params(("parallel", "arbitrary")),
        name="attn_prompt",
    )(qt, qit, wit, kb, kib, vbt)


SAMPLE_CHUNK = 1024


def _attn_sample_kernel(pt_ref, q_ref, qi_ref, wi_ref, kn_ref, vn_ref, kin_ref, ck_hbm, cv_hbm, cki_hbm, o_ref,
                        kbuf, vbuf, kibuf, sem, key_scr, bias_scr, lg_scr, j_scr, *, n_pages, page, t_new, top_k):
    b = pl.program_id(0)
    n_b = pl.num_programs(0)
    slot = b % 2
    past = n_pages * page
    lc = past + page
    n_chunks = past // SAMPLE_CHUNK

    def page_copies(bb, sl, p):
        phys = pt_ref[bb * n_pages + p]
        dst = pl.ds(pl.multiple_of(p * page, page), page)
        return [pltpu.make_async_copy(src.at[phys], buf.at[sl, :, dst], sem.at[i, sl])
                for i, (src, buf) in enumerate(((ck_hbm, kbuf), (cv_hbm, vbuf), (cki_hbm, kibuf)))]

    def start_batch(bb, sl):
        def body(p, carry):
            for cp in page_copies(bb, sl, p):
                cp.start()
            return carry
        lax.fori_loop(0, n_pages, body, 0)

    def wait_batch(bb, sl):
        def body(p, carry):
            for cp in page_copies(bb, sl, p):
                cp.wait()
            return carry
        lax.fori_loop(0, n_pages, body, 0)

    @pl.when(b == 0)
    def _():
        start_batch(0, 0)

    @pl.when(b + 1 < n_b)
    def _():
        start_batch(b + 1, 1 - slot)

    wait_batch(b, slot)

    def head_sum(d):
        r = wi_ref[...] * jnp.maximum(d, 0.0)
        s = r[0:t_new]
        for h in range(1, IDX_HEADS):
            s = s + r[h * t_new:(h + 1) * t_new]
        return s

    def new_rows(ref):
        pad = jnp.zeros((page - t_new, ref.shape[1]), F32)
        return jnp.concatenate([ref[...], pad], axis=0).astype(BF16)

    qi = qi_ref[...]
    for c in range(n_chunks):
        sl = slice(c * SAMPLE_CHUNK, (c + 1) * SAMPLE_CHUNK)
        d = jnp.dot(qi, kibuf[slot, :, sl].astype(BF16), preferred_element_type=F32)
        key_scr[:, sl] = head_sum(d)
    d_new = lax.dot_general(qi, new_rows(kin_ref), NT_DIMS, preferred_element_type=F32)
    adm_new = lax.broadcasted_iota(I32, (t_new, page), 1) <= lax.broadcasted_iota(I32, (t_new, page), 0)
    key_scr[:, past:lc] = jnp.where(adm_new, head_sum(d_new), -jnp.inf)

    idx = lax.broadcasted_iota(I32, (t_new, lc), 1)
    trow = lax.broadcasted_iota(I32, (t_new, lc), 0)
    n_adm = past + 1 + lax.broadcasted_iota(I32, (t_new, 1), 0)
    bias_scr[...] = _topk_bias(key_scr, j_scr, idx - past <= trow, n_adm, lc, top_k)

    q = q_ref[...]

    def bias_rows(sl):
        return jnp.concatenate([bias_scr[:, sl]] * N_HEADS, axis=0)

    m = jnp.full((N_HEADS * t_new, 1), -jnp.inf, F32)
    for c in range(n_chunks):
        sl = slice(c * SAMPLE_CHUNK, (c + 1) * SAMPLE_CHUNK)
        lg = jnp.dot(q, kbuf[slot, :, sl].astype(BF16), preferred_element_type=F32) + bias_rows(sl)
        lg_scr[:, sl] = lg
        m = jnp.maximum(m, jnp.max(lg, axis=1, keepdims=True))
    lg_new = lax.dot_general(q, new_rows(kn_ref), NT_DIMS, preferred_element_type=F32) + bias_rows(slice(past, lc))
    m = jnp.maximum(m, jnp.max(lg_new, axis=1, keepdims=True))

    p_new = jnp.exp2(lg_new - m)
    l = jnp.sum(p_new, axis=1, keepdims=True)
    o = jnp.dot(p_new.astype(BF16), new_rows(vn_ref), preferred_element_type=F32)
    for c in range(n_chunks):
        sl = slice(c * SAMPLE_CHUNK, (c + 1) * SAMPLE_CHUNK)
        pr = jnp.exp2(lg_scr[:, sl] - m)
        l = l + jnp.sum(pr, axis=1, keepdims=True)
        o = o + lax.dot_general(pr.astype(BF16), vbuf[slot, :, sl].astype(BF16), NT_DIMS,
                                preferred_element_type=F32)
    o_ref[...] = o / l


def _attn_sample(page_table, q_hq, qi_hq, wi_hq, k_new, v_new, ki_new, cache_kt, cache_vt, cache_kit):
    db, n_pages = page_table.shape
    page = cache_kt.shape[2]
    t_new = k_new.shape[1]
    past = n_pages * page
    lc = past + page
    top_k = min(TOP_K_MAX, (past + t_new) // 4)
    per_b = lambda r, w: pl.BlockSpec((None, r, w), lambda b, pt: (b, 0, 0))
    hbm = pl.BlockSpec(memory_space=pl.ANY)
    kern = functools.partial(_attn_sample_kernel, n_pages=n_pages, page=page, t_new=t_new, top_k=top_k)
    slab = pltpu.VMEM((2, HEAD_DIM, past), F32)
    grid_spec = pltpu.PrefetchScalarGridSpec(
        num_scalar_prefetch=1,
        grid=(db,),
        in_specs=[per_b(N_HEADS * t_new, HEAD_DIM), per_b(IDX_HEADS * t_new, IDX_DIM), per_b(IDX_HEADS * t_new, 1),
                  per_b(t_new, HEAD_DIM), per_b(t_new, HEAD_DIM), per_b(t_new, IDX_DIM),
                  hbm, hbm, hbm],
        out_specs=per_b(N_HEADS * t_new, HEAD_DIM),
        scratch_shapes=[slab, slab, slab, pltpu.SemaphoreType.DMA((3, 2)),
                        pltpu.VMEM((t_new, lc), F32), pltpu.VMEM((t_new, lc), F32),
                        pltpu.VMEM((N_HEADS * t_new, past), F32), pltpu.VMEM((t_new, 1), I32)],
    )
    return pl.pallas_call(
        kern,
        grid_spec=grid_spec,
        out_shape=jax.ShapeDtypeStruct((db, N_HEADS * t_new, HEAD_DIM), F32),
        compiler_params=_params(("arbitrary",)),
        name="attn_sample",
    )(page_table.reshape(-1), q_hq, qi_hq, wi_hq, k_new, v_new, ki_new, cache_kt, cache_vt, cache_kit)


PREV_ROWS = 16


def _pool_kernel(prev_ref, u_ref, wg_ref, sc_ref, o_ref, ext_ref, *, pos0):
    per_step, t_len, _ = u_ref.shape
    pos = pos0 + lax.broadcasted_iota(I32, (t_len, 1), 0)
    for b in range(per_step):
        ext_ref[0:PREV_ROWS, :] = prev_ref[b]
        ext_ref[PREV_ROWS:PREV_ROWS + t_len, :] = u_ref[b]
        for g, w in enumerate(POOL_WINDOWS):
            sl = slice(g * POOL_GW, (g + 1) * POOL_GW)
            u_new = ext_ref[PREV_ROWS:PREV_ROWS + t_len, sl]
            win = u_new
            for back in range(1, w):
                win = win + ext_ref[PREV_ROWS - back:PREV_ROWS - back + t_len, sl]
            count = jnp.minimum(pos + 1, w).astype(F32)
            r = win / count - u_new
            mixed = jnp.dot(r.astype(BF16), wg_ref[g], preferred_element_type=F32) * sc_ref[:, sl]
            o_ref[b, :, sl] = mixed.astype(BF16)


def _pool(prev, u, w_grp, scale, pos0, per_step):
    nb, t_len, _ = u.shape
    seqs = lambda rows: pl.BlockSpec((per_step, rows, POOL_WIDTH), lambda b: (b, 0, 0))
    return pl.pallas_call(
        functools.partial(_pool_kernel, pos0=pos0),
        grid=(nb // per_step,),
        in_specs=[seqs(PREV_ROWS), seqs(t_len),
                  pl.BlockSpec((POOL_GROUPS, POOL_GW, POOL_GW), lambda b: (0, 0, 0)),
                  pl.BlockSpec((1, POOL_WIDTH), lambda b: (0, 0))],
        out_specs=seqs(t_len),
        out_shape=jax.ShapeDtypeStruct((nb, t_len, POOL_WIDTH), BF16),
        scratch_shapes=[pltpu.VMEM((PREV_ROWS + t_len, POOL_WIDTH), F32)],
        compiler_params=_params(("parallel",)),
        name="pool",
    )(prev, u, w_grp, scale)


def _merge_kernel(x_ref, a_ref, p_ref, wga_ref, wgb_ref, wao_ref, wpo_ref, wo_ref, g_ref, b_ref, h_ref, hp_ref, *,
                  alpha):
    x = x_ref[...]
    xb = x.astype(BF16)
    ga = jnp.dot(xb, wga_ref[...], preferred_element_type=F32)
    gb = jnp.dot(xb, wgb_ref[...], preferred_element_type=F32)
    ya = jnp.dot(a_ref[...], wao_ref[...], preferred_element_type=F32)
    yp = jnp.dot(p_ref[...], wpo_ref[...], preferred_element_type=F32)
    mix = jax.nn.sigmoid(ga) * ya + jax.nn.sigmoid(gb) * yp
    out = jnp.dot(mix.astype(BF16), wo_ref[...], preferred_element_type=F32)
    h = _layer_norm(alpha * x + out, g_ref[...], b_ref[...])
    h_ref[...] = h
    hp_ref[...] = _pack_rows(h)


def _merge(x, attn, pool, wga, wgb, wao, wpo, wo, g, b, tm, alpha):
    n = x.shape[0]
    row = lambda w: pl.BlockSpec((tm, w), lambda i: (i, 0))
    full = lambda r, c: pl.BlockSpec((r, c), lambda i: (0, 0), pipeline_mode=pl.Buffered(1))
    return pl.pallas_call(
        functools.partial(_merge_kernel, alpha=alpha),
        grid=(n // tm,),
        in_specs=[row(D_MODEL), row(ATT_WIDTH), row(POOL_WIDTH), full(D_MODEL, D_MODEL), full(D_MODEL, D_MODEL),
                  full(ATT_WIDTH, D_MODEL), full(POOL_WIDTH, D_MODEL), full(D_MODEL, D_MODEL),
                  full(1, D_MODEL), full(1, D_MODEL)],
        out_specs=(row(D_MODEL), row(PACKED)),
        out_shape=(jax.ShapeDtypeStruct((n, D_MODEL), F32), jax.ShapeDtypeStruct((n, PACKED), I32)),
        compiler_params=_params(("parallel",)),
        name="merge",
    )(x, attn, pool, wga, wgb, wao, wpo, wo, g, b)


def _route(h, wr_t, bias_col):
    tm = h.shape[0]
    logits = lax.dot_general(wr_t, h.astype(BF16), NT_DIMS, preferred_element_type=F32)
    s = jax.nn.sigmoid(logits)
    sb = s + bias_col
    neg_inf = -jnp.inf

    rows = []
    for g in range(N_GROUPS):
        blk = sb[g * GROUP_SIZE:(g + 1) * GROUP_SIZE, :]
        m1 = jnp.max(blk, axis=0, keepdims=True)
        is_m1 = blk == m1
        n_m1 = jnp.sum(is_m1.astype(F32), axis=0, keepdims=True)
        m2 = jnp.max(jnp.where(is_m1, neg_inf, blk), axis=0, keepdims=True)
        rows.append(m1 + jnp.where(n_m1 >= 2.0, m1, m2))
    gs = jnp.concatenate(rows, axis=0)

    gi = lax.broadcasted_iota(I32, (N_GROUPS, tm), 0)
    rank = jnp.zeros((N_GROUPS, tm), F32)
    for g in range(N_GROUPS):
        row = gs[g:g + 1, :]
        beats = jnp.logical_or(row > gs, jnp.logical_and(row == gs, g < gi))
        rank = rank + beats.astype(F32)
    gkeep = rank < float(TOPK_GROUPS)
    emask = jnp.concatenate(
        [jnp.broadcast_to(gkeep[g:g + 1, :], (GROUP_SIZE, tm)) for g in range(N_GROUPS)], axis=0)

    ei = lax.broadcasted_iota(I32, (N_EXPERTS, tm), 0)
    x = jnp.where(emask, sb, neg_inf)
    sel = jnp.zeros((N_EXPERTS, tm), jnp.bool_)
    picks = []
    for _ in range(TOP_K_EXPERTS):
        m = jnp.max(x, axis=0, keepdims=True)
        first = jnp.min(jnp.where(x == m, ei, N_EXPERTS), axis=0, keepdims=True)
        pick = ei == first
        sel = jnp.logical_or(sel, pick)
        x = jnp.where(pick, neg_inf, x)
        picks.append(first)

    gate = jnp.where(sel, s, 0.0)
    comb = gate / jnp.sum(gate, axis=0, keepdims=True) * ROUTED_SCALE
    return comb, sel, picks


def _router_kernel(h_ref, wr_ref, bias_ref, c_ref):
    comb, _, _ = _route(h_ref[...], wr_ref[...], bias_ref[...])
    comb = jnp.concatenate([comb, jnp.zeros((LANES - N_EXPERTS, comb.shape[1]), F32)], axis=0)
    c_ref[...] = comb.T


def _router(h, wr_t, bias_col, tm):
    n = h.shape[0]
    return pl.pallas_call(
        _router_kernel,
        grid=(n // tm,),
        in_specs=[pl.BlockSpec((tm, D_MODEL), lambda i: (i, 0)),
                  pl.BlockSpec((N_EXPERTS, D_MODEL), lambda i: (0, 0)),
                  pl.BlockSpec((N_EXPERTS, 1), lambda i: (0, 0))],
        out_specs=pl.BlockSpec((tm, LANES), lambda i: (i, 0)),
        out_shape=jax.ShapeDtypeStruct((n, LANES), F32),
        compiler_params=_params(("parallel",)),
        name="router",
    )(h, wr_t, bias_col)


def _swiglu(xb, w13, w2, hidden):
    ab = jnp.dot(xb, w13, preferred_element_type=F32)
    act = jax.nn.silu(ab[:, 0:hidden]) * ab[:, hidden:2 * hidden]
    return jnp.dot(act.astype(BF16), w2, preferred_element_type=F32)


def _moe_kernel(h_ref, c_ref, ws13_ref, ws2_ref, w13_ref, w2_ref, y_ref, hb_ref):
    e = pl.program_id(1)

    @pl.when(e == 0)
    def _():
        hb_ref[...] = h_ref[...].astype(BF16)
        y_ref[...] = _swiglu(hb_ref[...], ws13_ref[...], ws2_ref[...], SHARED_DIM)

    ye = _swiglu(hb_ref[...], w13_ref[...].astype(BF16), w2_ref[...].astype(BF16), EXPERT_DIM)
    lane = lax.broadcasted_iota(I32, c_ref.shape, 1)
    ce = jnp.sum(jnp.where(lane == e, c_ref[...], 0.0), axis=1, keepdims=True)
    y_ref[...] += ce * ye


def _moe(h, comb, ws13, ws2, w13, w2, tm):
    n = h.shape[0]
    return pl.pallas_call(
        _moe_kernel,
        grid=(n // tm, N_EXPERTS),
        in_specs=[pl.BlockSpec((tm, D_MODEL), lambda i, e: (i, 0)),
                  pl.BlockSpec((tm, LANES), lambda i, e: (i, 0)),
                  pl.BlockSpec((D_MODEL, 2 * SHARED_DIM), lambda i, e: (0, 0)),
                  pl.BlockSpec((SHARED_DIM, D_MODEL), lambda i, e: (0, 0)),
                  pl.BlockSpec((None, D_MODEL, 2 * EXPERT_DIM), lambda i, e: (e, 0, 0)),
                  pl.BlockSpec((None, EXPERT_DIM, D_MODEL), lambda i, e: (e, 0, 0))],
        out_specs=pl.BlockSpec((tm, D_MODEL), lambda i, e: (i, 0)),
        out_shape=jax.ShapeDtypeStruct((n, D_MODEL), F32),
        scratch_shapes=[pltpu.VMEM((tm, D_MODEL), BF16)],
        compiler_params=_params(("parallel", "arbitrary")),
        name="moe",
    )(h, comb, ws13, ws2, w13, w2)


def _final_kernel(h_ref, y_ref, pe_ref, g_ref, b_ref, wpg_ref, wpi_ref, o_ref, *, alpha):
    z = _layer_norm(alpha * h_ref[...] + y_ref[...], g_ref[...], b_ref[...])
    gate = jax.nn.sigmoid(jnp.dot(z.astype(BF16), wpg_ref[...], preferred_element_type=F32))
    emb = jnp.dot(pe_ref[...].astype(BF16), wpi_ref[...], preferred_element_type=F32)
    o_ref[...] = z + gate * emb


def _final(h, y, pe, g, b, wpg, wpi, tm, alpha):
    n = h.shape[0]
    row = lambda w: pl.BlockSpec((tm, w), lambda i: (i, 0))
    full = lambda r, c: pl.BlockSpec((r, c), lambda i: (0, 0))
    return pl.pallas_call(
        functools.partial(_final_kernel, alpha=alpha),
        grid=(n // tm,),
        in_specs=[row(D_MODEL), row(D_MODEL), row(PLE_DIM), full(1, D_MODEL), full(1, D_MODEL),
                  full(D_MODEL, D_MODEL), full(PLE_DIM, D_MODEL)],
        out_specs=row(D_MODEL),
        out_shape=jax.ShapeDtypeStruct((n, D_MODEL), F32),
        compiler_params=_params(("parallel",)),
        name="final",
    )(h, y, pe, g, b, wpg, wpi)


MOE_BLOCK = 1024


def _sorted_rows(n_tokens):
    worst = n_tokens * TOP_K_EXPERTS + N_EXPERTS * (MOE_BLOCK - 1)
    return -(-worst // MOE_BLOCK) * MOE_BLOCK


def _dispatch_kernel(h_ref, wr_ref, bias_ref, tri_ref, pos_ref, gate_ref, blk_ref, used_ref,
                     eidx_s, rank_s, gate_s, cnt_s):
    p = pl.program_id(0)
    i = pl.program_id(1)
    tm = h_ref.shape[0]
    ei = lax.broadcasted_iota(I32, (N_EXPERTS, tm), 0)

    @pl.when(p == 0)
    def _():
        comb, sel, picks = _route(h_ref[...], wr_ref[...], bias_ref[...])
        before = jnp.dot(sel.astype(BF16), tri_ref[...], preferred_element_type=F32)
        ranks, gates = [], []
        for first in picks:
            pick = ei == first
            ranks.append(jnp.sum(jnp.where(pick, before, 0.0), axis=0, keepdims=True))
            gates.append(jnp.sum(jnp.where(pick, comb, 0.0), axis=0, keepdims=True))
        eidx_s[i] = jnp.concatenate(picks, axis=0)
        rank_s[i] = jnp.concatenate(ranks, axis=0)
        gate_s[i] = jnp.concatenate(gates, axis=0)
        cnt_s[i] = jnp.broadcast_to(jnp.sum(sel.astype(F32), axis=1, keepdims=True), (N_EXPERTS, LANES))

    @pl.when(p == 1)
    def _():
        cnt = cnt_s[...]
        tile_id = lax.broadcasted_iota(I32, cnt.shape, 0)
        total = jnp.sum(cnt, axis=0)
        prior = jnp.sum(jnp.where(tile_id < i, cnt, 0.0), axis=0)
        seg = jnp.ceil(total * (1.0 / MOE_BLOCK)) * MOE_BLOCK
        lower = (lax.broadcasted_iota(I32, (N_EXPERTS, N_EXPERTS), 1)
                 < lax.broadcasted_iota(I32, (N_EXPERTS, N_EXPERTS), 0)).astype(F32)
        seg_off = jnp.dot(lower, seg, precision=lax.Precision.HIGHEST, preferred_element_type=F32)
        base = (seg_off + prior)[:, 0:1]
        eidx = eidx_s[i]
        rank = rank_s[i]
        rows = []
        for k in range(TOP_K_EXPERTS):
            pick = ei == eidx[k:k + 1, :]
            rows.append(rank[k:k + 1, :] + jnp.sum(jnp.where(pick, base, 0.0), axis=0, keepdims=True))
        pos_ref[...] = jnp.concatenate(rows, axis=0).astype(I32)
        gate_ref[...] = jnp.concatenate([gate_s[i], jnp.zeros((LANES - TOP_K_EXPERTS, tm), F32)], axis=0).T

        seg_end = (seg_off + seg)[:, 0:1]
        n_blk = blk_ref.shape[1]
        blk_start = (lax.broadcasted_iota(I32, (N_EXPERTS, n_blk), 1) * MOE_BLOCK).astype(F32)
        owner = jnp.sum((seg_end <= blk_start).astype(F32), axis=0, keepdims=True)
        blk_ref[...] = jnp.minimum(owner, N_EXPERTS - 1.0).astype(I32)
        used = seg_end[N_EXPERTS - 1:N_EXPERTS, :] * (1.0 / MOE_BLOCK)
        used_ref[...] = jnp.broadcast_to(used, used_ref.shape).astype(I32)


def _dispatch(h, wr_t, bias_col, tm):
    n = h.shape[0]
    n_tiles = n // tm
    n_blk = _sorted_rows(n) // MOE_BLOCK
    n_blk_pad = -(-n_blk // LANES) * LANES
    tri = jnp.triu(jnp.ones((tm, tm), BF16), k=1)
    const = lambda r, c: pl.BlockSpec((r, c), lambda p, i: (0, 0))
    per_tile = lambda dt: pltpu.VMEM((n_tiles, TOP_K_EXPERTS, tm), dt)
    return pl.pallas_call(
        _dispatch_kernel,
        grid=(2, n_tiles),
        in_specs=[pl.BlockSpec((tm, D_MODEL), lambda p, i: (i * (1 - p), 0)),
                  const(N_EXPERTS, D_MODEL), const(N_EXPERTS, 1), const(tm, tm)],
        out_specs=(pl.BlockSpec((TOP_K_EXPERTS, tm), lambda p, i: (0, i * p)),
                   pl.BlockSpec((tm, LANES), lambda p, i: (i * p, 0)),
                   const(1, n_blk_pad), const(1, LANES)),
        out_shape=(jax.ShapeDtypeStruct((TOP_K_EXPERTS, n), I32), jax.ShapeDtypeStruct((n, LANES), F32),
                   jax.ShapeDtypeStruct((1, n_blk_pad), I32), jax.ShapeDtypeStruct((1, LANES), I32)),
        scratch_shapes=[per_tile(I32), per_tile(F32), per_tile(F32), pltpu.VMEM((n_tiles, N_EXPERTS, LANES), F32)],
        compiler_params=_params(("arbitrary", "arbitrary")),
        name="dispatch",
    )(h, wr_t, bias_col, tri)


PACKED = D_MODEL // 2


def _pack_rows(x):
    lo = pltpu.bitcast(x[:, 0:PACKED].astype(BF16).astype(F32), I32)
    hi = pltpu.bitcast(x[:, PACKED:D_MODEL].astype(BF16).astype(F32), I32)
    return jnp.bitwise_or(hi, lax.shift_right_logical(lo, 16))


def _unpack_rows(w):
    lo = pltpu.bitcast(lax.shift_left(w, 16), F32)
    hi = pltpu.bitcast(jnp.bitwise_and(w, -65536), F32)
    return jnp.concatenate([lo, hi], axis=1).astype(BF16)


def _grouped_kernel(blk_ref, used_ref, anchor_ref, xs_ref, w13_ref, w2_ref, ys_ref):
    @pl.when(pl.program_id(0) < used_ref[0])
    def _():
        ys = _swiglu(_unpack_rows(xs_ref[...]), w13_ref[...].astype(BF16), w2_ref[...].astype(BF16), EXPERT_DIM)
        ys_ref[...] = _pack_rows(ys)


def _grouped(blk, used, anchor, xs, w13, w2):
    ns = xs.shape[0]
    row_blk = lambda b, blk, used, anchor: (jnp.minimum(b, used[0] - 1), 0)
    expert = lambda b, blk, used, anchor: (blk[b], 0, 0)
    grid_spec = pltpu.PrefetchScalarGridSpec(
        num_scalar_prefetch=3,
        grid=(ns // MOE_BLOCK,),
        in_specs=[pl.BlockSpec((MOE_BLOCK, PACKED), row_blk),
                  pl.BlockSpec((None, D_MODEL, 2 * EXPERT_DIM), expert),
                  pl.BlockSpec((None, EXPERT_DIM, D_MODEL), expert)],
        out_specs=pl.BlockSpec((MOE_BLOCK, PACKED), row_blk),
    )
    return pl.pallas_call(
        _grouped_kernel,
        grid_spec=grid_spec,
        out_shape=jax.ShapeDtypeStruct((ns, PACKED), I32),
        compiler_params=_params(("arbitrary",)),
        name="grouped",
    )(blk, used, anchor, xs, w13, w2)


SC_WINDOW = 128


def _sc_mesh():
    return plsc.VectorSubcoreMesh(core_axis_name="core", subcore_axis_name="subcore")


def _sc_worker(n_items):
    info = plsc.get_sparse_core_info()
    n_workers = info.num_cores * info.num_subcores
    wid = lax.axis_index("subcore") * info.num_cores + lax.axis_index("core")
    return wid, n_items // (SC_WINDOW * n_workers)


def _scatter_rows(x, pos, n_out):
    n, width = x.shape
    picks = pos.shape[0]

    @functools.partial(
        pl.kernel, mesh=_sc_mesh(), out_type=jax.ShapeDtypeStruct((n_out, width), I32),
        scratch_types=[pltpu.VMEM((picks, SC_WINDOW), I32), pltpu.VMEM((SC_WINDOW, width), I32)],
        name="scatter_rows")
    def scatter(x_hbm, pos_hbm, out_hbm, idx_v, rows_v):
        wid, n_win = _sc_worker(n)

        @pl.loop(0, n_win)
        def _(j):
            base = (wid * n_win + j) * SC_WINDOW
            pltpu.sync_copy(pos_hbm.at[:, pl.ds(base, SC_WINDOW)], idx_v)
            pltpu.sync_copy(x_hbm.at[pl.ds(base, SC_WINDOW)], rows_v)
            for k in range(picks):
                pltpu.sync_copy(rows_v, out_hbm.at[idx_v.at[k]])

    return scatter(x, pos)


def _gather_rows(src, pos):
    width = src.shape[1]
    picks, n = pos.shape

    @functools.partial(
        pl.kernel, mesh=_sc_mesh(), out_type=jax.ShapeDtypeStruct((picks * n, width), I32),
        scratch_types=[pltpu.VMEM((SC_WINDOW,), I32), pltpu.VMEM((SC_WINDOW, width), I32)],
        name="gather_rows")
    def gather(src_hbm, pos_hbm, out_hbm, idx_v, rows_v):
        wid, n_win = _sc_worker(picks * n)

        @pl.loop(0, n_win)
        def _(j):
            base = (wid * n_win + j) * SC_WINDOW
            pltpu.sync_copy(pos_hbm.at[pl.ds(base, SC_WINDOW)], idx_v)
            pltpu.sync_copy(src_hbm.at[idx_v], rows_v)
            pltpu.sync_copy(rows_v, out_hbm.at[pl.ds(base, SC_WINDOW)])

    return gather(src, pos.reshape(-1)).reshape(picks, n, width)


def _combine_kernel(h_ref, g_ref, gate_ref, pe_ref, ws13_ref, ws2_ref, ln_g_ref, ln_b_ref, wpg_ref, wpi_ref, o_ref, *,
                    alpha):
    h = h_ref[...]
    y = _swiglu(h.astype(BF16), ws13_ref[...], ws2_ref[...], SHARED_DIM)
    gate = gate_ref[...]
    for k in range(TOP_K_EXPERTS):
        y = y + gate[:, k:k + 1] * _unpack_rows(g_ref[k]).astype(F32)
    z = _layer_norm(alpha * h + y, ln_g_ref[...], ln_b_ref[...])
    ple_gate = jax.nn.sigmoid(jnp.dot(z.astype(BF16), wpg_ref[...], preferred_element_type=F32))
    emb = jnp.dot(pe_ref[...].astype(BF16), wpi_ref[...], preferred_element_type=F32)
    o_ref[...] = z + ple_gate * emb


def _combine(h, gathered, gate, pe, ws13, ws2, g, b, wpg, wpi, tm, alpha):
    n = h.shape[0]
    row = lambda w: pl.BlockSpec((tm, w), lambda i: (i, 0))
    full = lambda r, c: pl.BlockSpec((r, c), lambda i: (0, 0))
    return pl.pallas_call(
        functools.partial(_combine_kernel, alpha=alpha),
        grid=(n // tm,),
        in_specs=[row(D_MODEL), pl.BlockSpec((TOP_K_EXPERTS, tm, PACKED), lambda i: (0, i, 0)), row(LANES),
                  row(PLE_DIM), full(D_MODEL, 2 * SHARED_DIM), full(SHARED_DIM, D_MODEL),
                  full(1, D_MODEL), full(1, D_MODEL), full(D_MODEL, D_MODEL), full(PLE_DIM, D_MODEL)],
        out_specs=row(D_MODEL),
        out_shape=jax.ShapeDtypeStruct((n, D_MODEL), F32),
        compiler_params=_params(("parallel",)),
        name="combine",
    )(h, gathered, gate, pe, ws13, ws2, g, b, wpg, wpi)


def _rope_table(pos):
    inv = ROPE_THETA ** (-jnp.arange(0, HEAD_DIM, 2, dtype=F32) / HEAD_DIM)
    ang = pos.astype(F32)[:, None] * inv[None, :]
    return jnp.concatenate([jnp.tile(jnp.cos(ang), (1, 4)), jnp.tile(jnp.sin(ang), (1, 4))], axis=1)


def _rotate_half_cols(w, n_heads):
    w3 = w.reshape(w.shape[0], n_heads, HEAD_DIM)
    half = HEAD_DIM // 2
    return jnp.concatenate([-w3[..., half:], w3[..., :half]], axis=-1).reshape(w.shape)


def _fused_in_weight(w_in):
    offs = np.cumsum(IN_SIZES)[:-1].tolist()
    wq, wk, wv, wqi, wki, wwi, wu, wga, wgb = jnp.split(w_in, offs, axis=1)
    pad = jnp.zeros((D_MODEL, LANES - HEAD_DIM - IDX_HEADS), w_in.dtype)
    w_big = jnp.concatenate(
        [wq, _rotate_half_cols(wq, N_HEADS), wqi, _rotate_half_cols(wqi, IDX_HEADS),
         wk, wki, _rotate_half_cols(wk, 1), _rotate_half_cols(wki, 1), wv, wwi, pad, wu], axis=1).astype(BF16)
    return w_big, w_big[:, 0:C_U].T, wga.astype(BF16), wgb.astype(BF16)


def _pages_transposed(cache):
    return jnp.transpose(cache[0], (0, 2, 1))


def _heads_major(a, n_heads):
    b, t, w = a.shape
    d = w // n_heads
    return a.reshape(b, t, n_heads, d).transpose(0, 2, 1, 3).reshape(b, n_heads * t, d)


def kernel(x_prompt, x_sample, cache_k, cache_v, cache_kidx, state_pool, page_table, p_prompt, p_sample, w_in, w_att_out, w_pool_grp, pool_scale, w_pool_out, w_out, ln1_g, ln1_b, w_router, router_bias, w_exp13, w_exp2, w_sh13, w_sh2, ln2_g, ln2_b, w_ple_in, w_ple_gate):
    B, S, D = x_prompt.shape
    DB, T, _ = x_sample.shape
    depth = w_in.shape[0]
    assert depth == 1, "single layer step"
    page = cache_k.shape[2]
    past = page_table.shape[1] * page
    alpha = (2 * depth) ** 0.25
    n_p, n_s = B * S, DB * T

    w_big, w_t, wga, wgb = _fused_in_weight(w_in[0])
    wao, wpo, wo = w_att_out[0].astype(BF16), w_pool_out[0].astype(BF16), w_out[0].astype(BF16)
    wgrp = w_pool_grp[0].astype(BF16)
    pscale = pool_scale[0].reshape(1, POOL_WIDTH)
    g1, b1 = ln1_g[0].reshape(1, D), ln1_b[0].reshape(1, D)
    g2, b2 = ln2_g[0].reshape(1, D), ln2_b[0].reshape(1, D)
    wr_t = w_router[0].T.astype(BF16)
    rbias = router_bias[0].reshape(N_EXPERTS, 1)
    w13, w2 = w_exp13[0], w_exp2[0]
    ws13, ws2 = w_sh13[0].astype(BF16), w_sh2[0].astype(BF16)
    wpg, wpi = w_ple_gate[0].astype(BF16), w_ple_in[0].astype(BF16)

    cs_p = _rope_table(jnp.arange(S, dtype=I32))
    cs_s = jnp.tile(_rope_table(past + jnp.arange(T, dtype=I32)), (DB, 1))

    xp = x_prompt.reshape(n_p, D)
    qt, qit, wit, kb, kib, vbt, kt, vt, kit, u = _proj_prompt(xp, w_big, w_t, cs_p, S, 512)
    attn_p = _attn_prompt(qt, qit, wit, kb, kib, vbt)
    u3 = u.reshape(B, S, POOL_WIDTH)
    pool_p = _pool(jnp.zeros((B, PREV_ROWS, POOL_WIDTH), F32), u3, wgrp, pscale, 0, 1).reshape(n_p, POOL_WIDTH)
    h_p, hp_p = _merge(xp, attn_p, pool_p, wga, wgb, wao, wpo, wo, g1, b1, 1024, alpha)

    xs = x_sample.reshape(n_s, D)
    qs, qis, ks, vs, kis, wis, us = _proj_sample(xs, w_big, cs_s)
    q_hq = _heads_major(qs.reshape(DB, T, ATT_WIDTH), N_HEADS)
    qi_hq = _heads_major(qis.reshape(DB, T, IDX_HEADS * IDX_DIM), IDX_HEADS)
    wi_hq = wis.reshape(DB, T, IDX_HEADS).transpose(0, 2, 1).reshape(DB, IDX_HEADS * T, 1)
    caches = (_pages_transposed(cache_k), _pages_transposed(cache_v), _pages_transposed(cache_kidx))
    new_rows = (ks.reshape(DB, T, HEAD_DIM), vs.reshape(DB, T, HEAD_DIM), kis.reshape(DB, T, IDX_DIM))
    half = DB // 2
    o_halves = [_attn_sample(page_table[sl], q_hq[sl], qi_hq[sl], wi_hq[sl], *(a[sl] for a in new_rows), *caches)
                for sl in (slice(0, half), slice(half, DB))]
    o_hq = jnp.concatenate(o_halves, axis=0)
    attn_s = o_hq.reshape(DB, N_HEADS, T, HEAD_DIM).transpose(0, 2, 1, 3).reshape(n_s, ATT_WIDTH).astype(BF16)
    us3 = us.reshape(DB, T, POOL_WIDTH)
    prev_s = jnp.concatenate([jnp.zeros((DB, PREV_ROWS - POOL_STATE, POOL_WIDTH), F32), state_pool[0]], axis=1)
    pool_s = _pool(prev_s, us3, wgrp, pscale, past, DB).reshape(n_s, POOL_WIDTH)
    h_s, _ = _merge(xs, attn_s, pool_s, wga, wgb, wao, wpo, wo, g1, b1, n_s, alpha)

    def tail(h, pe, tm_r, tm_m, tm_f):
        comb = _router(h, wr_t, rbias, tm_r)
        y = _moe(h, comb, ws13, ws2, w13, w2, tm_m)
        return _final(h, y, pe, g2, b2, wpg, wpi, tm_f, alpha)

    y_s = tail(h_s, p_sample[0].reshape(n_s, PLE_DIM), n_s, n_s, n_s)

    pos, gate, blk, used = _dispatch(h_p, wr_t, rbias, 1024)
    sorted_in = _scatter_rows(hp_p, pos, _sorted_rows(n_p))
    anchor = lax.bitcast_convert_type(o_halves[0][0, 0, 0:1], I32)
    sorted_out = _grouped(blk.reshape(-1), used.reshape(-1), anchor, sorted_in, w13, w2)
    gathered = _gather_rows(sorted_out, pos)
    y_p = _combine(h_p, gathered, gate, p_prompt[0].reshape(n_p, PLE_DIM), ws13, ws2, g2, b2, wpg, wpi, 512, alpha)

    ext_s = jnp.concatenate([state_pool[0], us3], axis=1)
    return (y_p.reshape(B, S, D), y_s.reshape(DB, T, D),
            jnp.transpose(kt, (0, 2, 1))[None], jnp.transpose(vt, (0, 2, 1))[None],
            jnp.transpose(kit, (0, 2, 1))[None],
            u3[:, S - POOL_STATE:][None],
            ks.reshape(1, DB, T, HEAD_DIM), vs.reshape(1, DB, T, HEAD_DIM), kis.reshape(1, DB, T, IDX_DIM),
            ext_s[:, T:][None])
```

```python
import functools

import numpy as np
import jax
import jax.numpy as jnp
from jax import lax
from jax.experimental import pallas as pl
from jax.experimental.pallas import tpu as pltpu
from jax.experimental.pallas import tpu_sc as plsc

F32 = jnp.float32
BF16 = jnp.bfloat16
I32 = jnp.int32

D_MODEL = 1024
N_HEADS = 8
HEAD_DIM = 64
ATT_WIDTH = N_HEADS * HEAD_DIM
IDX_HEADS = 4
IDX_DIM = 64
TOP_K_MAX = 256
Q_BLOCK = 128
ROPE_THETA = 10000.0
POOL_WINDOWS = (2, 4, 8, 16)
POOL_GROUPS = 4
POOL_WIDTH = 512
POOL_GW = POOL_WIDTH // POOL_GROUPS
POOL_STATE = 15
N_EXPERTS = 64
TOP_K_EXPERTS = 8
N_GROUPS = 8
GROUP_SIZE = N_EXPERTS // N_GROUPS
TOPK_GROUPS = 4
EXPERT_DIM = 256
SHARED_DIM = 256
ROUTED_SCALE = 2.5
PLE_DIM = 256
LN_EPS = 1e-5
IN_SIZES = (ATT_WIDTH, HEAD_DIM, HEAD_DIM, IDX_HEADS * IDX_DIM, IDX_DIM, IDX_HEADS, POOL_WIDTH, D_MODEL, D_MODEL)

LANES = 128
SUBLANES = 8
INT_MIN = -2147483648
NEG_BIG = -1e30
VMEM_LIMIT = 56 * 1024 * 1024

C_Q = 0
C_QI = 512
C_KK = 768
C_VW = 896
C_U = 1024
C_END = 1536
HALF = HEAD_DIM // 2

NT_DIMS = (((1,), (1,)), ((), ()))

Q_SCALE = HEAD_DIM ** -0.5 * float(np.log2(np.e))
QI_SCALE = IDX_DIM ** -0.5


def _params(sem):
    return pltpu.CompilerParams(dimension_semantics=sem, vmem_limit_bytes=VMEM_LIMIT)


def _layer_norm(x, g, b):
    mu = jnp.mean(x, axis=-1, keepdims=True)
    xc = x - mu
    var = jnp.mean(xc * xc, axis=-1, keepdims=True)
    return xc * lax.rsqrt(var + LN_EPS) * g + b


def _rope_rows(a, cos, sin):
    first_half = lax.broadcasted_iota(I32, (a.shape[0], LANES), 1) % HEAD_DIM < HALF
    out = []
    for s in range(a.shape[1] // LANES):
        x = a[:, s * LANES:(s + 1) * LANES]
        rot = jnp.where(first_half, -pltpu.roll(x, LANES - HALF, axis=1), pltpu.roll(x, HALF, axis=1))
        out.append(x * cos + rot * sin)
    return out[0] if len(out) == 1 else jnp.concatenate(out, axis=1)


def _proj_sample_kernel(x_ref, w_ref, cs_ref, q_ref, qi_ref, k_ref, v_ref, ki_ref, wi_ref, u_ref):
    xb = x_ref[...].astype(BF16)
    cos = cs_ref[:, 0:LANES]
    sin = cs_ref[:, LANES:2 * LANES]

    def mm(c0, n):
        return jnp.dot(xb, w_ref[:, c0:c0 + n], preferred_element_type=F32)

    def rope(c0, n):
        return _rope_rows(mm(c0, n), cos, sin)

    q_ref[...] = (rope(C_Q, ATT_WIDTH) * Q_SCALE).astype(BF16)
    qi_ref[...] = (rope(C_QI, IDX_HEADS * IDX_DIM) * QI_SCALE).astype(BF16)
    kk = rope(C_KK, LANES)
    k_ref[...] = kk[:, 0:HEAD_DIM]
    ki_ref[...] = kk[:, HEAD_DIM:2 * HEAD_DIM]
    vw = mm(C_VW, LANES)
    v_ref[...] = vw[:, 0:HEAD_DIM]
    wi_ref[...] = vw[:, HEAD_DIM:HEAD_DIM + IDX_HEADS] * (IDX_HEADS ** -0.5)
    u_ref[...] = mm(C_U, POOL_WIDTH)


def _proj_sample(x, w_big, cs):
    n = x.shape[0]
    full = lambda r, c: pl.BlockSpec((r, c), lambda i: (0, 0))
    widths = (ATT_WIDTH, IDX_HEADS * IDX_DIM, HEAD_DIM, HEAD_DIM, IDX_DIM, IDX_HEADS, POOL_WIDTH)
    dtypes = (BF16, BF16, F32, F32, F32, F32, F32)
    return pl.pallas_call(
        _proj_sample_kernel,
        grid=(1,),
        in_specs=[full(n, D_MODEL), full(D_MODEL, C_END), full(n, 2 * LANES)],
        out_specs=tuple(full(n, w) for w in widths),
        out_shape=tuple(jax.ShapeDtypeStruct((n, w), dt) for w, dt in zip(widths, dtypes)),
        compiler_params=_params(("arbitrary",)),
        name="proj_sample",
    )(x, w_big, cs)


def _proj_prompt_kernel(x_ref, w_ref, wt_ref, cs_ref, cst_ref, qt_ref, qit_ref, wit_ref, kb_ref, kib_ref, vbt_ref,
                        kt_ref, vt_ref, kit_ref, u_ref):
    xb = x_ref[...].astype(BF16)
    tm = xb.shape[0]
    cos = cs_ref[:, 0:LANES]
    sin = cs_ref[:, LANES:2 * LANES]
    cos_t = cst_ref[0:HEAD_DIM, :]
    sin_t = cst_ref[LANES:LANES + HEAD_DIM, :]

    def mm(c0, n):
        return jnp.dot(xb, w_ref[:, c0:c0 + n], preferred_element_type=F32)

    def mm_t(c0, n):
        return lax.dot_general(wt_ref[c0:c0 + n, :], xb, NT_DIMS, preferred_element_type=F32)

    def rope_t(c0, heads):
        a = mm_t(c0, heads * HEAD_DIM)
        parts = []
        for h in range(heads):
            x1 = a[h * HEAD_DIM:h * HEAD_DIM + HALF, :]
            x2 = a[h * HEAD_DIM + HALF:(h + 1) * HEAD_DIM, :]
            rot = jnp.concatenate([-x2, x1], axis=0)
            parts.append(a[h * HEAD_DIM:(h + 1) * HEAD_DIM, :] * cos_t + rot * sin_t)
        return parts[0] if heads == 1 else jnp.concatenate(parts, axis=0)

    kk = _rope_rows(mm(C_KK, LANES), cos, sin)
    kb_ref[...] = kk[:, 0:HEAD_DIM].astype(BF16)
    kib_ref[...] = kk[:, HEAD_DIM:2 * HEAD_DIM].astype(BF16)
    u_ref[...] = mm(C_U, POOL_WIDTH)

    qt = (rope_t(C_Q, N_HEADS) * Q_SCALE).astype(BF16)
    qit = (rope_t(C_QI, IDX_HEADS) * QI_SCALE).astype(BF16)
    for blk in range(tm // Q_BLOCK):
        cols = slice(blk * Q_BLOCK, (blk + 1) * Q_BLOCK)
        for h in range(N_HEADS):
            qt_ref[blk, :, h * Q_BLOCK:(h + 1) * Q_BLOCK] = qt[h * HEAD_DIM:(h + 1) * HEAD_DIM, cols]
        for h in range(IDX_HEADS):
            qit_ref[blk, :, h * Q_BLOCK:(h + 1) * Q_BLOCK] = qit[h * IDX_DIM:(h + 1) * IDX_DIM, cols]

    kkt = rope_t(C_KK, 2)
    kt_ref[...] = kkt[0:HEAD_DIM, :]
    kit_ref[...] = kkt[HEAD_DIM:2 * HEAD_DIM, :]
    vwt = mm_t(C_VW, LANES)
    vt_ref[...] = vwt[0:HEAD_DIM, :]
    vbt_ref[...] = vwt[0:HEAD_DIM, :].astype(BF16)
    wit_ref[...] = vwt[HEAD_DIM:HEAD_DIM + SUBLANES, :] * (IDX_HEADS ** -0.5)


def _proj_prompt(x, w_big, w_t, cs, seq, tm):
    n = x.shape[0]
    nb = seq // tm
    qb = tm // Q_BLOCK
    row = lambda w: pl.BlockSpec((tm, w), lambda i: (i, 0))
    col = lambda r: pl.BlockSpec((None, r, tm), lambda i: (i // nb, 0, i % nb))
    slab = lambda heads: pl.BlockSpec((qb, HEAD_DIM, heads * Q_BLOCK), lambda i: (i, 0, 0))
    pm = lambda r, dt: jax.ShapeDtypeStruct((n // seq, r, seq), dt)
    out_shape = (
        jax.ShapeDtypeStruct((n // Q_BLOCK, HEAD_DIM, N_HEADS * Q_BLOCK), BF16),
        jax.ShapeDtypeStruct((n // Q_BLOCK, IDX_DIM, IDX_HEADS * Q_BLOCK), BF16),
        pm(SUBLANES, F32),
        jax.ShapeDtypeStruct((n, HEAD_DIM), BF16), jax.ShapeDtypeStruct((n, IDX_DIM), BF16),
        pm(HEAD_DIM, BF16),
        pm(HEAD_DIM, F32), pm(HEAD_DIM, F32), pm(IDX_DIM, F32),
        jax.ShapeDtypeStruct((n, POOL_WIDTH), F32),
    )
    return pl.pallas_call(
        _proj_prompt_kernel,
        grid=(n // tm,),
        in_specs=[
            row(D_MODEL),
            pl.BlockSpec((D_MODEL, C_END), lambda i: (0, 0)),
            pl.BlockSpec((C_U, D_MODEL), lambda i: (0, 0)),
            pl.BlockSpec((tm, 2 * LANES), lambda i: (i % nb, 0)),
            pl.BlockSpec((2 * LANES, tm), lambda i: (0, i % nb)),
        ],
        out_specs=(slab(N_HEADS), slab(IDX_HEADS), col(SUBLANES), row(HEAD_DIM), row(IDX_DIM), col(HEAD_DIM),
                   col(HEAD_DIM), col(HEAD_DIM), col(IDX_DIM), row(POOL_WIDTH)),
        out_shape=out_shape,
        compiler_params=_params(("parallel",)),
        name="proj_prompt",
    )(x, w_big, w_t, cs, cs.T)


def _float_of_rank(u):
    key = u ^ INT_MIN
    bits = jnp.where(key < 0, INT_MIN - key, key)
    return pltpu.bitcast(bits, F32)


def _count(mask):
    return jnp.sum(mask.astype(F32), axis=1, keepdims=True)


def _topk_bias(sc_ref, j_ref, adm, n_adm, lc, k):
    rows = sc_ref.shape[0]
    kf = float(k)

    def value_step(i, t_u):
        hi = jnp.left_shift(jnp.int32(1), 31 - 2 * i)
        lo = jnp.left_shift(jnp.int32(1), 30 - 2 * i)
        for cand_u in (t_u | lo, t_u | hi, t_u | hi | lo):
            cnt = _count(sc_ref[:, 0:lc] >= _float_of_rank(cand_u))
            t_u = jnp.where(cnt >= kf, cand_u, t_u)
        return t_u

    t_u = lax.fori_loop(0, 16, value_step, jnp.zeros((rows, 1), I32))
    few = n_adm < k
    thr = jnp.where(few, -jnp.inf, _float_of_rank(t_u))
    sc = sc_ref[:, 0:lc]
    cnt_gt = _count(sc > thr)
    cnt_eq = _count(sc == thr)
    need = kf - cnt_gt
    cut_needed = jnp.logical_and(cnt_gt + cnt_eq > kf, jnp.logical_not(few))
    any_cut = jnp.max(cut_needed.astype(F32)) > 0.0
    idx = lax.broadcasted_iota(I32, (rows, lc), 1)
    nbits = int(np.ceil(np.log2(lc)))

    j_ref[...] = jnp.full((rows, 1), lc, I32)

    @pl.when(any_cut)
    def _():
        def index_step(i, j):
            cand = j | jnp.left_shift(jnp.int32(1), nbits - 1 - i)
            c = _count(jnp.logical_and(sc_ref[:, 0:lc] == thr, idx < cand))
            return jnp.where(c < need, cand, j)

        j_ref[...] = lax.fori_loop(0, nbits, index_step, jnp.zeros((rows, 1), I32))

    sel = jnp.logical_or(sc > thr, jnp.logical_and(sc == thr, idx <= j_ref[...]))
    return jnp.where(jnp.logical_and(sel, adm), 0.0, NEG_BIG)


ATTN_CHUNK = 256


def _attn_prompt_block(n_chunks, q0, top_k, qt_ref, qit_ref, wit_ref, kb_ref, kib_ref, vbt_ref, o_ref,
                       key_ref, bias_ref, lg_ref, j_ref):
    tq, ch = Q_BLOCK, ATTN_CHUNK
    seq = key_ref.shape[0]
    kf = float(top_k)
    kpos = lax.broadcasted_iota(I32, (ch, tq), 0)
    qpos = q0 + lax.broadcasted_iota(I32, (ch, tq), 1)

    def rows(c):
        return slice(c * ch, (c + 1) * ch)

    def fold(x, op):
        return op(x.reshape(ch // SUBLANES, SUBLANES, tq), axis=0)

    def head(x, h):
        return x[:, h * tq:(h + 1) * tq]

    qit = qit_ref[...]
    wit = wit_ref[...]
    for c in range(n_chunks if n_chunks * ch > top_k else 0):
        d = jnp.dot(kib_ref[rows(c), :], qit, preferred_element_type=F32)
        s = wit[0:1, :] * jnp.maximum(head(d, 0), 0.0)
        for h in range(1, IDX_HEADS):
            s = s + wit[h:h + 1, :] * jnp.maximum(head(d, h), 0.0)
        key_ref[rows(c), :] = jnp.where(c * ch + kpos <= qpos, s, -jnp.inf)

    def count(pred):
        acc = jnp.zeros((SUBLANES, tq), F32)
        for c in range(n_chunks):
            acc = acc + fold(pred(key_ref[rows(c), :], c).astype(F32), jnp.sum)
        return jnp.sum(acc, axis=0, keepdims=True)

    if n_chunks * ch <= top_k:
        for c in range(n_chunks):
            bias_ref[rows(c), :] = jnp.where(c * ch + kpos <= qpos, 0.0, NEG_BIG)
    else:
        def value_step(i, carry):
            t_u, n_ge = carry
            cand_u = t_u | jnp.left_shift(jnp.int32(1), 31 - i)
            cand = _float_of_rank(cand_u)
            cnt = count(lambda k, c: k >= cand)
            ok = cnt >= kf
            return jnp.where(ok, cand_u, t_u), jnp.where(ok, cnt, n_ge)

        t_u, n_ge = lax.fori_loop(0, 32, value_step,
                                  (jnp.zeros((1, tq), I32), jnp.full((1, tq), float(n_chunks * ch), F32)))
        few = qpos[0:1, :] + 1 <= top_k
        thr = jnp.where(few, -jnp.inf, _float_of_rank(t_u))
        cut_needed = jnp.logical_and(n_ge > kf, jnp.logical_not(few))
        any_cut = jnp.max(cut_needed.astype(F32)) > 0.0

        nbits = int(np.ceil(np.log2(seq)))
        j_ref[...] = jnp.full(j_ref.shape, seq, I32)

        @pl.when(any_cut)
        def _():
            need = kf - count(lambda k, c: k > thr)

            def index_step(i, j):
                cand = j | jnp.left_shift(jnp.int32(1), nbits - 1 - i)
                n_before = count(lambda k, c: jnp.logical_and(k == thr, c * ch + kpos < cand))
                return jnp.where(n_before < need, cand, j)

            j = lax.fori_loop(0, nbits, index_step, jnp.zeros((1, tq), I32))
            j_ref[...] = jnp.broadcast_to(j, j_ref.shape)

        j_cut = j_ref[0:1, :]
        for c in range(n_chunks):
            k = key_ref[rows(c), :]
            pos = c * ch + kpos
            sel = jnp.logical_or(k > thr, jnp.logical_and(k == thr, pos <= j_cut))
            bias_ref[rows(c), :] = jnp.where(jnp.logical_and(sel, pos <= qpos), 0.0, NEG_BIG)

    qt = qt_ref[...]
    mx = [jnp.full((SUBLANES, tq), -jnp.inf, F32) for _ in range(N_HEADS)]
    for c in range(n_chunks):
        lg = jnp.dot(kb_ref[rows(c), :], qt, preferred_element_type=F32)
        bias = bias_ref[rows(c), :]
        for h in range(N_HEADS):
            lgh = head(lg, h) + bias
            lg_ref[h, rows(c), :] = lgh
            mx[h] = jnp.maximum(mx[h], fold(lgh, jnp.max))

    outs = []
    for h in range(N_HEADS):
        m = jnp.max(mx[h], axis=0, keepdims=True)
        lsum = jnp.zeros((SUBLANES, tq), F32)
        ot = jnp.zeros((HEAD_DIM, tq), F32)
        for c in range(n_chunks):
            p = jnp.exp2(lg_ref[h, rows(c), :] - m)
            lsum = lsum + fold(p, jnp.sum)
            ot = ot + jnp.dot(vbt_ref[:, rows(c)], p.astype(BF16), preferred_element_type=F32)
        outs.append(ot / jnp.sum(lsum, axis=0, keepdims=True))
    o_ref[...] = jnp.concatenate(outs, axis=0).T.astype(BF16)


def _attn_prompt_kernel(qt_ref, qit_ref, wit_ref, kb_ref, kib_ref, vbt_ref, o_ref, key_ref, bias_ref, lg_ref, j_ref,
                        *, top_k):
    jq = pl.program_id(1)
    blocks_per_chunk = ATTN_CHUNK // Q_BLOCK
    n_classes = key_ref.shape[0] // ATTN_CHUNK
    for cls in range(n_classes):
        @pl.when(jq // blocks_per_chunk == cls)
        def _(cls=cls):
            _attn_prompt_block(cls + 1, jq * Q_BLOCK, top_k, qt_ref, qit_ref, wit_ref, kb_ref, kib_ref, vbt_ref,
                               o_ref, key_ref, bias_ref, lg_ref, j_ref)


def _attn_prompt(qt, qit, wit, kb, kib, vbt):
    batch, _, seq = vbt.shape
    nb = seq // Q_BLOCK
    top_k = min(TOP_K_MAX, seq // 4)
    slab = lambda heads: pl.BlockSpec((None, HEAD_DIM, heads * Q_BLOCK), lambda b, j: (b * nb + j, 0, 0))
    keys = pl.BlockSpec((seq, HEAD_DIM), lambda b, j: (b, 0))
    return pl.pallas_call(
        functools.partial(_attn_prompt_kernel, top_k=top_k),
        grid=(batch, nb),
        in_specs=[slab(N_HEADS), slab(IDX_HEADS), pl.BlockSpec((None, SUBLANES, Q_BLOCK), lambda b, j: (b, 0, j)),
                  keys, keys, pl.BlockSpec((None, HEAD_DIM, seq), lambda b, j: (b, 0, 0))],
        out_specs=pl.BlockSpec((Q_BLOCK, ATT_WIDTH), lambda b, j: (b * nb + j, 0)),
        out_shape=jax.ShapeDtypeStruct((batch * seq, ATT_WIDTH), BF16),
        scratch_shapes=[pltpu.VMEM((seq, Q_BLOCK), F32), pltpu.VMEM((seq, Q_BLOCK), F32),
                        pltpu.VMEM((N_HEADS, seq, Q_BLOCK), F32), pltpu.VMEM((SUBLANES, Q_BLOCK), I32)],
        compiler_params=_params(("parallel", "arbitrary")),
        name="attn_prompt",
    )(qt, qit, wit, kb, kib, vbt)


SAMPLE_CHUNK = 1024


def _attn_sample_kernel(pt_ref, q_ref, qi_ref, wi_ref, kn_ref, vn_ref, kin_ref, ck_hbm, cv_hbm, cki_hbm, o_ref,
                        kbuf, vbuf, kibuf, sem, key_scr, bias_scr, lg_scr, j_scr, *, n_pages, page, t_new, top_k):
    b = pl.program_id(0)
    n_b = pl.num_programs(0)
    slot = b % 2
    past = n_pages * page
    lc = past + page
    n_chunks = past // SAMPLE_CHUNK

    def page_copies(bb, sl, p):
        phys = pt_ref[bb * n_pages + p]
        dst = pl.ds(pl.multiple_of(p * page, page), page)
        return [pltpu.make_async_copy(src.at[phys], buf.at[sl, :, dst], sem.at[i, sl])
                for i, (src, buf) in enumerate(((ck_hbm, kbuf), (cv_hbm, vbuf), (cki_hbm, kibuf)))]

    def start_batch(bb, sl):
        def body(p, carry):
            for cp in page_copies(bb, sl, p):
                cp.start()
            return carry
        lax.fori_loop(0, n_pages, body, 0)

    def wait_batch(bb, sl):
        def body(p, carry):
            for cp in page_copies(bb, sl, p):
                cp.wait()
            return carry
        lax.fori_loop(0, n_pages, body, 0)

    @pl.when(b == 0)
    def _():
        start_batch(0, 0)

    @pl.when(b + 1 < n_b)
    def _():
        start_batch(b + 1, 1 - slot)

    wait_batch(b, slot)

    def head_sum(d):
        r = wi_ref[...] * jnp.maximum(d, 0.0)
        s = r[0:t_new]
        for h in range(1, IDX_HEADS):
            s = s + r[h * t_new:(h + 1) * t_new]
        return s

    def new_rows(ref):
        pad = jnp.zeros((page - t_new, ref.shape[1]), F32)
        return jnp.concatenate([ref[...], pad], axis=0).astype(BF16)

    qi = qi_ref[...]
    for c in range(n_chunks):
        sl = slice(c * SAMPLE_CHUNK, (c + 1) * SAMPLE_CHUNK)
        d = jnp.dot(qi, kibuf[slot, :, sl].astype(BF16), preferred_element_type=F32)
        key_scr[:, sl] = head_sum(d)
    d_new = lax.dot_general(qi, new_rows(kin_ref), NT_DIMS, preferred_element_type=F32)
    adm_new = lax.broadcasted_iota(I32, (t_new, page), 1) <= lax.broadcasted_iota(I32, (t_new, page), 0)
    key_scr[:, past:lc] = jnp.where(adm_new, head_sum(d_new), -jnp.inf)

    idx = lax.broadcasted_iota(I32, (t_new, lc), 1)
    trow = lax.broadcasted_iota(I32, (t_new, lc), 0)
    n_adm = past + 1 + lax.broadcasted_iota(I32, (t_new, 1), 0)
    bias_scr[...] = _topk_bias(key_scr, j_scr, idx - past <= trow, n_adm, lc, top_k)

    q = q_ref[...]

    def bias_rows(sl):
        return jnp.concatenate([bias_scr[:, sl]] * N_HEADS, axis=0)

    m = jnp.full((N_HEADS * t_new, 1), -jnp.inf, F32)
    for c in range(n_chunks):
        sl = slice(c * SAMPLE_CHUNK, (c + 1) * SAMPLE_CHUNK)
        lg = jnp.dot(q, kbuf[slot, :, sl].astype(BF16), preferred_element_type=F32) + bias_rows(sl)
        lg_scr[:, sl] = lg
        m = jnp.maximum(m, jnp.max(lg, axis=1, keepdims=True))
    lg_new = lax.dot_general(q, new_rows(kn_ref), NT_DIMS, preferred_element_type=F32) + bias_rows(slice(past, lc))
    m = jnp.maximum(m, jnp.max(lg_new, axis=1, keepdims=True))

    p_new = jnp.exp2(lg_new - m)
    l = jnp.sum(p_new, axis=1, keepdims=True)
    o = jnp.dot(p_new.astype(BF16), new_rows(vn_ref), preferred_element_type=F32)
    for c in range(n_chunks):
        sl = slice(c * SAMPLE_CHUNK, (c + 1) * SAMPLE_CHUNK)
        pr = jnp.exp2(lg_scr[:, sl] - m)
        l = l + jnp.sum(pr, axis=1, keepdims=True)
        o = o + lax.dot_general(pr.astype(BF16), vbuf[slot, :, sl].astype(BF16), NT_DIMS,
                                preferred_element_type=F32)
    o_ref[...] = o / l


def _attn_sample(page_table, q_hq, qi_hq, wi_hq, k_new, v_new, ki_new, cache_kt, cache_vt, cache_kit):
    db, n_pages = page_table.shape
    page = cache_kt.shape[2]
    t_new = k_new.shape[1]
    past = n_pages * page
    lc = past + page
    top_k = min(TOP_K_MAX, (past + t_new) // 4)
    per_b = lambda r, w: pl.BlockSpec((None, r, w), lambda b, pt: (b, 0, 0))
    hbm = pl.BlockSpec(memory_space=pl.ANY)
    kern = functools.partial(_attn_sample_kernel, n_pages=n_pages, page=page, t_new=t_new, top_k=top_k)
    slab = pltpu.VMEM((2, HEAD_DIM, past), F32)
    grid_spec = pltpu.PrefetchScalarGridSpec(
        num_scalar_prefetch=1,
        grid=(db,),
        in_specs=[per_b(N_HEADS * t_new, HEAD_DIM), per_b(IDX_HEADS * t_new, IDX_DIM), per_b(IDX_HEADS * t_new, 1),
                  per_b(t_new, HEAD_DIM), per_b(t_new, HEAD_DIM), per_b(t_new, IDX_DIM),
                  hbm, hbm, hbm],
        out_specs=per_b(N_HEADS * t_new, HEAD_DIM),
        scratch_shapes=[slab, slab, slab, pltpu.SemaphoreType.DMA((3, 2)),
                        pltpu.VMEM((t_new, lc), F32), pltpu.VMEM((t_new, lc), F32),
                        pltpu.VMEM((N_HEADS * t_new, past), F32), pltpu.VMEM((t_new, 1), I32)],
    )
    return pl.pallas_call(
        kern,
        grid_spec=grid_spec,
        out_shape=jax.ShapeDtypeStruct((db, N_HEADS * t_new, HEAD_DIM), F32),
        compiler_params=_params(("arbitrary",)),
        name="attn_sample",
    )(page_table.reshape(-1), q_hq, qi_hq, wi_hq, k_new, v_new, ki_new, cache_kt, cache_vt, cache_kit)


PREV_ROWS = 16


def _pool_kernel(prev_ref, u_ref, wg_ref, sc_ref, o_ref, ext_ref, *, pos0):
    per_step, t_len, _ = u_ref.shape
    pos = pos0 + lax.broadcasted_iota(I32, (t_len, 1), 0)
    for b in range(per_step):
        ext_ref[0:PREV_ROWS, :] = prev_ref[b]
        ext_ref[PREV_ROWS:PREV_ROWS + t_len, :] = u_ref[b]
        for g, w in enumerate(POOL_WINDOWS):
            sl = slice(g * POOL_GW, (g + 1) * POOL_GW)
            u_new = ext_ref[PREV_ROWS:PREV_ROWS + t_len, sl]
            win = u_new
            for back in range(1, w):
                win = win + ext_ref[PREV_ROWS - back:PREV_ROWS - back + t_len, sl]
            count = jnp.minimum(pos + 1, w).astype(F32)
            r = win / count - u_new
            mixed = jnp.dot(r.astype(BF16), wg_ref[g], preferred_element_type=F32) * sc_ref[:, sl]
            o_ref[b, :, sl] = mixed.astype(BF16)


def _pool(prev, u, w_grp, scale, pos0, per_step):
    nb, t_len, _ = u.shape
    seqs = lambda rows: pl.BlockSpec((per_step, rows, POOL_WIDTH), lambda b: (b, 0, 0))
    return pl.pallas_call(
        functools.partial(_pool_kernel, pos0=pos0),
        grid=(nb // per_step,),
        in_specs=[seqs(PREV_ROWS), seqs(t_len),
                  pl.BlockSpec((POOL_GROUPS, POOL_GW, POOL_GW), lambda b: (0, 0, 0)),
                  pl.BlockSpec((1, POOL_WIDTH), lambda b: (0, 0))],
        out_specs=seqs(t_len),
        out_shape=jax.ShapeDtypeStruct((nb, t_len, POOL_WIDTH), BF16),
        scratch_shapes=[pltpu.VMEM((PREV_ROWS + t_len, POOL_WIDTH), F32)],
        compiler_params=_params(("parallel",)),
        name="pool",
    )(prev, u, w_grp, scale)


def _merge_kernel(x_ref, a_ref, p_ref, wga_ref, wgb_ref, wao_ref, wpo_ref, wo_ref, g_ref, b_ref, h_ref, hp_ref, *,
                  alpha):
    x = x_ref[...]
    xb = x.astype(BF16)
    ga = jnp.dot(xb, wga_ref[...], preferred_element_type=F32)
    gb = jnp.dot(xb, wgb_ref[...], preferred_element_type=F32)
    ya = jnp.dot(a_ref[...], wao_ref[...], preferred_element_type=F32)
    yp = jnp.dot(p_ref[...], wpo_ref[...], preferred_element_type=F32)
    mix = jax.nn.sigmoid(ga) * ya + jax.nn.sigmoid(gb) * yp
    out = jnp.dot(mix.astype(BF16), wo_ref[...], preferred_element_type=F32)
    h = _layer_norm(alpha * x + out, g_ref[...], b_ref[...])
    h_ref[...] = h
    hp_ref[...] = _pack_rows(h)


def _merge(x, attn, pool, wga, wgb, wao, wpo, wo, g, b, tm, alpha):
    n = x.shape[0]
    row = lambda w: pl.BlockSpec((tm, w), lambda i: (i, 0))
    full = lambda r, c: pl.BlockSpec((r, c), lambda i: (0, 0), pipeline_mode=pl.Buffered(1))
    return pl.pallas_call(
        functools.partial(_merge_kernel, alpha=alpha),
        grid=(n // tm,),
        in_specs=[row(D_MODEL), row(ATT_WIDTH), row(POOL_WIDTH), full(D_MODEL, D_MODEL), full(D_MODEL, D_MODEL),
                  full(ATT_WIDTH, D_MODEL), full(POOL_WIDTH, D_MODEL), full(D_MODEL, D_MODEL),
                  full(1, D_MODEL), full(1, D_MODEL)],
        out_specs=(row(D_MODEL), row(PACKED)),
        out_shape=(jax.ShapeDtypeStruct((n, D_MODEL), F32), jax.ShapeDtypeStruct((n, PACKED), I32)),
        compiler_params=_params(("parallel",)),
        name="merge",
    )(x, attn, pool, wga, wgb, wao, wpo, wo, g, b)


def _route(h, wr_t, bias_col):
    tm = h.shape[0]
    logits = lax.dot_general(wr_t, h.astype(BF16), NT_DIMS, preferred_element_type=F32)
    s = jax.nn.sigmoid(logits)
    sb = s + bias_col
    neg_inf = -jnp.inf

    rows = []
    for g in range(N_GROUPS):
        blk = sb[g * GROUP_SIZE:(g + 1) * GROUP_SIZE, :]
        m1 = jnp.max(blk, axis=0, keepdims=True)
        is_m1 = blk == m1
        n_m1 = jnp.sum(is_m1.astype(F32), axis=0, keepdims=True)
        m2 = jnp.max(jnp.where(is_m1, neg_inf, blk), axis=0, keepdims=True)
        rows.append(m1 + jnp.where(n_m1 >= 2.0, m1, m2))
    gs = jnp.concatenate(rows, axis=0)

    gi = lax.broadcasted_iota(I32, (N_GROUPS, tm), 0)
    rank = jnp.zeros((N_GROUPS, tm), F32)
    for g in range(N_GROUPS):
        row = gs[g:g + 1, :]
        beats = jnp.logical_or(row > gs, jnp.logical_and(row == gs, g < gi))
        rank = rank + beats.astype(F32)
    gkeep = rank < float(TOPK_GROUPS)
    emask = jnp.concatenate(
        [jnp.broadcast_to(gkeep[g:g + 1, :], (GROUP_SIZE, tm)) for g in range(N_GROUPS)], axis=0)

    ei = lax.broadcasted_iota(I32, (N_EXPERTS, tm), 0)
    x = jnp.where(emask, sb, neg_inf)
    sel = jnp.zeros((N_EXPERTS, tm), jnp.bool_)
    picks = []
    for _ in range(TOP_K_EXPERTS):
        m = jnp.max(x, axis=0, keepdims=True)
        first = jnp.min(jnp.where(x == m, ei, N_EXPERTS), axis=0, keepdims=True)
        pick = ei == first
        sel = jnp.logical_or(sel, pick)
        x = jnp.where(pick, neg_inf, x)
        picks.append(first)

    gate = jnp.where(sel, s, 0.0)
    comb = gate / jnp.sum(gate, axis=0, keepdims=True) * ROUTED_SCALE
    return comb, sel, picks


def _router_kernel(h_ref, wr_ref, bias_ref, c_ref):
    comb, _, _ = _route(h_ref[...], wr_ref[...], bias_ref[...])
    comb = jnp.concatenate([comb, jnp.zeros((LANES - N_EXPERTS, comb.shape[1]), F32)], axis=0)
    c_ref[...] = comb.T


def _router(h, wr_t, bias_col, tm):
    n = h.shape[0]
    return pl.pallas_call(
        _router_kernel,
        grid=(n // tm,),
        in_specs=[pl.BlockSpec((tm, D_MODEL), lambda i: (i, 0)),
                  pl.BlockSpec((N_EXPERTS, D_MODEL), lambda i: (0, 0)),
                  pl.BlockSpec((N_EXPERTS, 1), lambda i: (0, 0))],
        out_specs=pl.BlockSpec((tm, LANES), lambda i: (i, 0)),
        out_shape=jax.ShapeDtypeStruct((n, LANES), F32),
        compiler_params=_params(("parallel",)),
        name="router",
    )(h, wr_t, bias_col)


def _swiglu(xb, w13, w2, hidden):
    ab = jnp.dot(xb, w13, preferred_element_type=F32)
    act = jax.nn.silu(ab[:, 0:hidden]) * ab[:, hidden:2 * hidden]
    return jnp.dot(act.astype(BF16), w2, preferred_element_type=F32)


def _moe_kernel(h_ref, c_ref, ws13_ref, ws2_ref, w13_ref, w2_ref, y_ref, hb_ref):
    e = pl.program_id(1)

    @pl.when(e == 0)
    def _():
        hb_ref[...] = h_ref[...].astype(BF16)
        y_ref[...] = _swiglu(hb_ref[...], ws13_ref[...], ws2_ref[...], SHARED_DIM)

    ye = _swiglu(hb_ref[...], w13_ref[...].astype(BF16), w2_ref[...].astype(BF16), EXPERT_DIM)
    lane = lax.broadcasted_iota(I32, c_ref.shape, 1)
    ce = jnp.sum(jnp.where(lane == e, c_ref[...], 0.0), axis=1, keepdims=True)
    y_ref[...] += ce * ye


def _moe(h, comb, ws13, ws2, w13, w2, tm):
    n = h.shape[0]
    return pl.pallas_call(
        _moe_kernel,
        grid=(n // tm, N_EXPERTS),
        in_specs=[pl.BlockSpec((tm, D_MODEL), lambda i, e: (i, 0)),
                  pl.BlockSpec((tm, LANES), lambda i, e: (i, 0)),
                  pl.BlockSpec((D_MODEL, 2 * SHARED_DIM), lambda i, e: (0, 0)),
                  pl.BlockSpec((SHARED_DIM, D_MODEL), lambda i, e: (0, 0)),
                  pl.BlockSpec((None, D_MODEL, 2 * EXPERT_DIM), lambda i, e: (e, 0, 0)),
                  pl.BlockSpec((None, EXPERT_DIM, D_MODEL), lambda i, e: (e, 0, 0))],
        out_specs=pl.BlockSpec((tm, D_MODEL), lambda i, e: (i, 0)),
        out_shape=jax.ShapeDtypeStruct((n, D_MODEL), F32),
        scratch_shapes=[pltpu.VMEM((tm, D_MODEL), BF16)],
        compiler_params=_params(("parallel", "arbitrary")),
        name="moe",
    )(h, comb, ws13, ws2, w13, w2)


def _final_kernel(h_ref, y_ref, pe_ref, g_ref, b_ref, wpg_ref, wpi_ref, o_ref, *, alpha):
    z = _layer_norm(alpha * h_ref[...] + y_ref[...], g_ref[...], b_ref[...])
    gate = jax.nn.sigmoid(jnp.dot(z.astype(BF16), wpg_ref[...], preferred_element_type=F32))
    emb = jnp.dot(pe_ref[...].astype(BF16), wpi_ref[...], preferred_element_type=F32)
    o_ref[...] = z + gate * emb


def _final(h, y, pe, g, b, wpg, wpi, tm, alpha):
    n = h.shape[0]
    row = lambda w: pl.BlockSpec((tm, w), lambda i: (i, 0))
    full = lambda r, c: pl.BlockSpec((r, c), lambda i: (0, 0))
    return pl.pallas_call(
        functools.partial(_final_kernel, alpha=alpha),
        grid=(n // tm,),
        in_specs=[row(D_MODEL), row(D_MODEL), row(PLE_DIM), full(1, D_MODEL), full(1, D_MODEL),
                  full(D_MODEL, D_MODEL), full(PLE_DIM, D_MODEL)],
        out_specs=row(D_MODEL),
        out_shape=jax.ShapeDtypeStruct((n, D_MODEL), F32),
        compiler_params=_params(("parallel",)),
        name="final",
    )(h, y, pe, g, b, wpg, wpi)


MOE_BLOCK = 1024


def _sorted_rows(n_tokens):
    worst = n_tokens * TOP_K_EXPERTS + N_EXPERTS * (MOE_BLOCK - 1)
    return -(-worst // MOE_BLOCK) * MOE_BLOCK


def _dispatch_kernel(h_ref, wr_ref, bias_ref, tri_ref, pos_ref, gate_ref, blk_ref, used_ref,
                     eidx_s, rank_s, gate_s, cnt_s):
    p = pl.program_id(0)
    i = pl.program_id(1)
    tm = h_ref.shape[0]
    ei = lax.broadcasted_iota(I32, (N_EXPERTS, tm), 0)

    @pl.when(p == 0)
    def _():
        comb, sel, picks = _route(h_ref[...], wr_ref[...], bias_ref[...])
        before = jnp.dot(sel.astype(BF16), tri_ref[...], preferred_element_type=F32)
        ranks, gates = [], []
        for first in picks:
            pick = ei == first
            ranks.append(jnp.sum(jnp.where(pick, before, 0.0), axis=0, keepdims=True))
            gates.append(jnp.sum(jnp.where(pick, comb, 0.0), axis=0, keepdims=True))
        eidx_s[i] = jnp.concatenate(picks, axis=0)
        rank_s[i] = jnp.concatenate(ranks, axis=0)
        gate_s[i] = jnp.concatenate(gates, axis=0)
        cnt_s[i] = jnp.broadcast_to(jnp.sum(sel.astype(F32), axis=1, keepdims=True), (N_EXPERTS, LANES))

    @pl.when(p == 1)
    def _():
        cnt = cnt_s[...]
        tile_id = lax.broadcasted_iota(I32, cnt.shape, 0)
        total = jnp.sum(cnt, axis=0)
        prior = jnp.sum(jnp.where(tile_id < i, cnt, 0.0), axis=0)
        seg = jnp.ceil(total * (1.0 / MOE_BLOCK)) * MOE_BLOCK
        lower = (lax.broadcasted_iota(I32, (N_EXPERTS, N_EXPERTS), 1)
                 < lax.broadcasted_iota(I32, (N_EXPERTS, N_EXPERTS), 0)).astype(F32)
        seg_off = jnp.dot(lower, seg, precision=lax.Precision.HIGHEST, preferred_element_type=F32)
        base = (seg_off + prior)[:, 0:1]
        eidx = eidx_s[i]
        rank = rank_s[i]
        rows = []
        for k in range(TOP_K_EXPERTS):
            pick = ei == eidx[k:k + 1, :]
            rows.append(rank[k:k + 1, :] + jnp.sum(jnp.where(pick, base, 0.0), axis=0, keepdims=True))
        pos_ref[...] = jnp.concatenate(rows, axis=0).astype(I32)
        gate_ref[...] = jnp.concatenate([gate_s[i], jnp.zeros((LANES - TOP_K_EXPERTS, tm), F32)], axis=0).T

        seg_end = (seg_off + seg)[:, 0:1]
        n_blk = blk_ref.shape[1]
        blk_start = (lax.broadcasted_iota(I32, (N_EXPERTS, n_blk), 1) * MOE_BLOCK).astype(F32)
        owner = jnp.sum((seg_end <= blk_start).astype(F32), axis=0, keepdims=True)
        blk_ref[...] = jnp.minimum(owner, N_EXPERTS - 1.0).astype(I32)
        used = seg_end[N_EXPERTS - 1:N_EXPERTS, :] * (1.0 / MOE_BLOCK)
        used_ref[...] = jnp.broadcast_to(used, used_ref.shape).astype(I32)


def _dispatch(h, wr_t, bias_col, tm):
    n = h.shape[0]
    n_tiles = n // tm
    n_blk = _sorted_rows(n) // MOE_BLOCK
    n_blk_pad = -(-n_blk // LANES) * LANES
    tri = jnp.triu(jnp.ones((tm, tm), BF16), k=1)
    const = lambda r, c: pl.BlockSpec((r, c), lambda p, i: (0, 0))
    per_tile = lambda dt: pltpu.VMEM((n_tiles, TOP_K_EXPERTS, tm), dt)
    return pl.pallas_call(
        _dispatch_kernel,
        grid=(2, n_tiles),
        in_specs=[pl.BlockSpec((tm, D_MODEL), lambda p, i: (i * (1 - p), 0)),
                  const(N_EXPERTS, D_MODEL), const(N_EXPERTS, 1), const(tm, tm)],
        out_specs=(pl.BlockSpec((TOP_K_EXPERTS, tm), lambda p, i: (0, i * p)),
                   pl.BlockSpec((tm, LANES), lambda p, i: (i * p, 0)),
                   const(1, n_blk_pad), const(1, LANES)),
        out_shape=(jax.ShapeDtypeStruct((TOP_K_EXPERTS, n), I32), jax.ShapeDtypeStruct((n, LANES), F32),
                   jax.ShapeDtypeStruct((1, n_blk_pad), I32), jax.ShapeDtypeStruct((1, LANES), I32)),
        scratch_shapes=[per_tile(I32), per_tile(F32), per_tile(F32), pltpu.VMEM((n_tiles, N_EXPERTS, LANES), F32)],
        compiler_params=_params(("arbitrary", "arbitrary")),
        name="dispatch",
    )(h, wr_t, bias_col, tri)


PACKED = D_MODEL // 2


def _pack_rows(x):
    lo = pltpu.bitcast(x[:, 0:PACKED].astype(BF16).astype(F32), I32)
    hi = pltpu.bitcast(x[:, PACKED:D_MODEL].astype(BF16).astype(F32), I32)
    return jnp.bitwise_or(hi, lax.shift_right_logical(lo, 16))


def _unpack_rows_f32(w):
    lo = pltpu.bitcast(lax.shift_left(w, 16), F32)
    hi = pltpu.bitcast(jnp.bitwise_and(w, -65536), F32)
    return jnp.concatenate([lo, hi], axis=1)


def _unpack_rows(w):
    return _unpack_rows_f32(w).astype(BF16)


def _grouped_kernel(blk_ref, used_ref, anchor_ref, xs_ref, w13_ref, w2_ref, ys_ref):
    @pl.when(pl.program_id(0) < used_ref[0])
    def _():
        ys = _swiglu(_unpack_rows(xs_ref[...]), w13_ref[...].astype(BF16), w2_ref[...].astype(BF16), EXPERT_DIM)
        ys_ref[...] = _pack_rows(ys)


def _grouped(blk, used, anchor, xs, w13, w2):
    ns = xs.shape[0]
    row_blk = lambda b, blk, used, anchor: (jnp.minimum(b, used[0] - 1), 0)
    expert = lambda b, blk, used, anchor: (blk[b], 0, 0)
    grid_spec = pltpu.PrefetchScalarGridSpec(
        num_scalar_prefetch=3,
        grid=(ns // MOE_BLOCK,),
        in_specs=[pl.BlockSpec((MOE_BLOCK, PACKED), row_blk),
                  pl.BlockSpec((None, D_MODEL, 2 * EXPERT_DIM), expert),
                  pl.BlockSpec((None, EXPERT_DIM, D_MODEL), expert)],
        out_specs=pl.BlockSpec((MOE_BLOCK, PACKED), row_blk),
    )
    return pl.pallas_call(
        _grouped_kernel,
        grid_spec=grid_spec,
        out_shape=jax.ShapeDtypeStruct((ns, PACKED), I32),
        compiler_params=_params(("arbitrary",)),
        name="grouped",
    )(blk, used, anchor, xs, w13, w2)


SC_WINDOW = 128


def _sc_mesh():
    return plsc.VectorSubcoreMesh(core_axis_name="core", subcore_axis_name="subcore")


def _sc_worker(n_items):
    info = plsc.get_sparse_core_info()
    n_workers = info.num_cores * info.num_subcores
    wid = lax.axis_index("subcore") * info.num_cores + lax.axis_index("core")
    return wid, n_items // (SC_WINDOW * n_workers)


def _scatter_rows(x, pos, n_out):
    n, width = x.shape
    picks = pos.shape[0]

    @functools.partial(
        pl.kernel, mesh=_sc_mesh(), out_type=jax.ShapeDtypeStruct((n_out, width), I32),
        scratch_types=[pltpu.VMEM((picks, SC_WINDOW), I32), pltpu.VMEM((SC_WINDOW, width), I32)],
        name="scatter_rows")
    def scatter(x_hbm, pos_hbm, out_hbm, idx_v, rows_v):
        wid, n_win = _sc_worker(n)

        @pl.loop(0, n_win)
        def _(j):
            base = (wid * n_win + j) * SC_WINDOW
            pltpu.sync_copy(pos_hbm.at[:, pl.ds(base, SC_WINDOW)], idx_v)
            pltpu.sync_copy(x_hbm.at[pl.ds(base, SC_WINDOW)], rows_v)
            for k in range(picks):
                pltpu.sync_copy(rows_v, out_hbm.at[idx_v.at[k]])

    return scatter(x, pos)


def _gather_rows(src, pos):
    width = src.shape[1]
    picks, n = pos.shape

    @functools.partial(
        pl.kernel, mesh=_sc_mesh(), out_type=jax.ShapeDtypeStruct((picks * n, width), I32),
        scratch_types=[pltpu.VMEM((SC_WINDOW,), I32), pltpu.VMEM((SC_WINDOW, width), I32)],
        name="gather_rows")
    def gather(src_hbm, pos_hbm, out_hbm, idx_v, rows_v):
        wid, n_win = _sc_worker(picks * n)

        @pl.loop(0, n_win)
        def _(j):
            base = (wid * n_win + j) * SC_WINDOW
            pltpu.sync_copy(pos_hbm.at[pl.ds(base, SC_WINDOW)], idx_v)
            pltpu.sync_copy(src_hbm.at[idx_v], rows_v)
            pltpu.sync_copy(rows_v, out_hbm.at[pl.ds(base, SC_WINDOW)])

    return gather(src, pos.reshape(-1)).reshape(picks, n, width)


def _combine_kernel(h_ref, g_ref, gate_ref, pe_ref, ws13_ref, ws2_ref, ln_g_ref, ln_b_ref, wpg_ref, wpi_ref, o_ref, *,
                    alpha):
    h = h_ref[...]
    y = _swiglu(h.astype(BF16), ws13_ref[...], ws2_ref[...], SHARED_DIM)
    gate = gate_ref[...]
    for k in range(TOP_K_EXPERTS):
        y = y + gate[:, k:k + 1] * _unpack_rows_f32(g_ref[k])
    z = _layer_norm(alpha * h + y, ln_g_ref[...], ln_b_ref[...])
    ple_gate = jax.nn.sigmoid(jnp.dot(z.astype(BF16), wpg_ref[...], preferred_element_type=F32))
    emb = jnp.dot(pe_ref[...].astype(BF16), wpi_ref[...], preferred_element_type=F32)
    o_ref[...] = z + ple_gate * emb


def _combine(h, gathered, gate, pe, ws13, ws2, g, b, wpg, wpi, tm, alpha):
    n = h.shape[0]
    row = lambda w: pl.BlockSpec((tm, w), lambda i: (i, 0))
    full = lambda r, c: pl.BlockSpec((r, c), lambda i: (0, 0))
    return pl.pallas_call(
        functools.partial(_combine_kernel, alpha=alpha),
        grid=(n // tm,),
        in_specs=[row(D_MODEL), pl.BlockSpec((TOP_K_EXPERTS, tm, PACKED), lambda i: (0, i, 0)), row(LANES),
                  row(PLE_DIM), full(D_MODEL, 2 * SHARED_DIM), full(SHARED_DIM, D_MODEL),
                  full(1, D_MODEL), full(1, D_MODEL), full(D_MODEL, D_MODEL), full(PLE_DIM, D_MODEL)],
        out_specs=row(D_MODEL),
        out_shape=jax.ShapeDtypeStruct((n, D_MODEL), F32),
        compiler_params=_params(("parallel",)),
        name="combine",
    )(h, gathered, gate, pe, ws13, ws2, g, b, wpg, wpi)


def _rope_table(pos):
    inv = ROPE_THETA ** (-jnp.arange(0, HEAD_DIM, 2, dtype=F32) / HEAD_DIM)
    ang = pos.astype(F32)[:, None] * inv[None, :]
    return jnp.concatenate([jnp.tile(jnp.cos(ang), (1, 4)), jnp.tile(jnp.sin(ang), (1, 4))], axis=1)


def _fused_in_weight(w_in):
    offs = np.cumsum(IN_SIZES)[:-1].tolist()
    wq, wk, wv, wqi, wki, wwi, wu, wga, wgb = jnp.split(w_in, offs, axis=1)
    pad = jnp.zeros((D_MODEL, LANES - HEAD_DIM - IDX_HEADS), w_in.dtype)
    w_big = jnp.concatenate([wq, wqi, wk, wki, wv, wwi, pad, wu], axis=1).astype(BF16)
    return w_big, w_big[:, 0:C_U].T, wga.astype(BF16), wgb.astype(BF16)


def _pages_transposed(cache):
    return jnp.transpose(cache[0], (0, 2, 1))


def _heads_major(a, n_heads):
    b, t, w = a.shape
    d = w // n_heads
    return a.reshape(b, t, n_heads, d).transpose(0, 2, 1, 3).reshape(b, n_heads * t, d)


def kernel(x_prompt, x_sample, cache_k, cache_v, cache_kidx, state_pool, page_table, p_prompt, p_sample, w_in, w_att_out, w_pool_grp, pool_scale, w_pool_out, w_out, ln1_g, ln1_b, w_router, router_bias, w_exp13, w_exp2, w_sh13, w_sh2, ln2_g, ln2_b, w_ple_in, w_ple_gate):
    B, S, D = x_prompt.shape
    DB, T, _ = x_sample.shape
    depth = w_in.shape[0]
    assert depth == 1, "single layer step"
    page = cache_k.shape[2]
    past = page_table.shape[1] * page
    alpha = (2 * depth) ** 0.25
    n_p, n_s = B * S, DB * T

    w_big, w_t, wga, wgb = _fused_in_weight(w_in[0])
    wao, wpo, wo = w_att_out[0].astype(BF16), w_pool_out[0].astype(BF16), w_out[0].astype(BF16)
    wgrp = w_pool_grp[0].astype(BF16)
    pscale = pool_scale[0].reshape(1, POOL_WIDTH)
    g1, b1 = ln1_g[0].reshape(1, D), ln1_b[0].reshape(1, D)
    g2, b2 = ln2_g[0].reshape(1, D), ln2_b[0].reshape(1, D)
    wr_t = w_router[0].T.astype(BF16)
    rbias = router_bias[0].reshape(N_EXPERTS, 1)
    w13, w2 = w_exp13[0], w_exp2[0]
    ws13, ws2 = w_sh13[0].astype(BF16), w_sh2[0].astype(BF16)
    wpg, wpi = w_ple_gate[0].astype(BF16), w_ple_in[0].astype(BF16)

    cs_p = _rope_table(jnp.arange(S, dtype=I32))
    cs_s = jnp.tile(_rope_table(past + jnp.arange(T, dtype=I32)), (DB, 1))

    xp = x_prompt.reshape(n_p, D)
    qt, qit, wit, kb, kib, vbt, kt, vt, kit, u = _proj_prompt(xp, w_big, w_t, cs_p, S, 512)
    attn_p = _attn_prompt(qt, qit, wit, kb, kib, vbt)
    u3 = u.reshape(B, S, POOL_WIDTH)
    pool_p = _pool(jnp.zeros((B, PREV_ROWS, POOL_WIDTH), F32), u3, wgrp, pscale, 0, 1).reshape(n_p, POOL_WIDTH)
    h_p, hp_p = _merge(xp, attn_p, pool_p, wga, wgb, wao, wpo, wo, g1, b1, 1024, alpha)

    xs = x_sample.reshape(n_s, D)
    qs, qis, ks, vs, kis, wis, us = _proj_sample(xs, w_big, cs_s)
    q_hq = _heads_major(qs.reshape(DB, T, ATT_WIDTH), N_HEADS)
    qi_hq = _heads_major(qis.reshape(DB, T, IDX_HEADS * IDX_DIM), IDX_HEADS)
    wi_hq = wis.reshape(DB, T, IDX_HEADS).transpose(0, 2, 1).reshape(DB, IDX_HEADS * T, 1)
    caches = (_pages_transposed(cache_k), _pages_transposed(cache_v), _pages_transposed(cache_kidx))
    new_rows = (ks.reshape(DB, T, HEAD_DIM), vs.reshape(DB, T, HEAD_DIM), kis.reshape(DB, T, IDX_DIM))
    half = DB // 2
    o_halves = [_attn_sample(page_table[sl], q_hq[sl], qi_hq[sl], wi_hq[sl], *(a[sl] for a in new_rows), *caches)
                for sl in (slice(0, half), slice(half, DB))]
    o_hq = jnp.concatenate(o_halves, axis=0)
    attn_s = o_hq.reshape(DB, N_HEADS, T, HEAD_DIM).transpose(0, 2, 1, 3).reshape(n_s, ATT_WIDTH).astype(BF16)
    us3 = us.reshape(DB, T, POOL_WIDTH)
    prev_s = jnp.concatenate([jnp.zeros((DB, PREV_ROWS - POOL_STATE, POOL_WIDTH), F32), state_pool[0]], axis=1)
    pool_s = _pool(prev_s, us3, wgrp, pscale, past, DB).reshape(n_s, POOL_WIDTH)
    h_s, _ = _merge(xs, attn_s, pool_s, wga, wgb, wao, wpo, wo, g1, b1, n_s, alpha)

    def tail(h, pe, tm_r, tm_m, tm_f):
        comb = _router(h, wr_t, rbias, tm_r)
        y = _moe(h, comb, ws13, ws2, w13, w2, tm_m)
        return _final(h, y, pe, g2, b2, wpg, wpi, tm_f, alpha)

    y_s = tail(h_s, p_sample[0].reshape(n_s, PLE_DIM), n_s, n_s, n_s)

    pos, gate, blk, used = _dispatch(h_p, wr_t, rbias, 1024)
    sorted_in = _scatter_rows(hp_p, pos, _sorted_rows(n_p))
    anchor = lax.bitcast_convert_type(o_halves[0][0, 0, 0:1], I32)
    sorted_out = _grouped(blk.reshape(-1), used.reshape(-1), anchor, sorted_in, w13, w2)
    gathered = _gather_rows(sorted_out, pos)
    y_p = _combine(h_p, gathered, gate, p_prompt[0].reshape(n_p, PLE_DIM), ws13, ws2, g2, b2, wpg, wpi, 512, alpha)

    ext_s = jnp.concatenate([state_pool[0], us3], axis=1)
    return (y_p.reshape(B, S, D), y_s.reshape(DB, T, D),
            jnp.transpose(kt, (0, 2, 1))[None], jnp.transpose(vt, (0, 2, 1))[None],
            jnp.transpose(kit, (0, 2, 1))[None],
            u3[:, S - POOL_STATE:][None],
            ks.reshape(1, DB, T, HEAD_DIM), vs.reshape(1, DB, T, HEAD_DIM), kis.reshape(1, DB, T, IDX_DIM),
            ext_s[:, T:][None])
```

```python
import functools

import numpy as np
import jax
import jax.numpy as jnp
from jax import lax
from jax.experimental import pallas as pl
from jax.experimental.pallas import tpu as pltpu
from jax.experimental.pallas import tpu_sc as plsc

F32 = jnp.float32
BF16 = jnp.bfloat16
I32 = jnp.int32

D_MODEL = 1024
N_HEADS = 8
HEAD_DIM = 64
ATT_WIDTH = N_HEADS * HEAD_DIM
IDX_HEADS = 4
IDX_DIM = 64
TOP_K_MAX = 256
Q_BLOCK = 256
ROPE_THETA = 10000.0
POOL_WINDOWS = (2, 4, 8, 16)
POOL_GROUPS = 4
POOL_WIDTH = 512
POOL_GW = POOL_WIDTH // POOL_GROUPS
POOL_STATE = 15
N_EXPERTS = 64
TOP_K_EXPERTS = 8
N_GROUPS = 8
GROUP_SIZE = N_EXPERTS // N_GROUPS
TOPK_GROUPS = 4
EXPERT_DIM = 256
SHARED_DIM = 256
ROUTED_SCALE = 2.5
PLE_DIM = 256
LN_EPS = 1e-5
IN_SIZES = (ATT_WIDTH, HEAD_DIM, HEAD_DIM, IDX_HEADS * IDX_DIM, IDX_DIM, IDX_HEADS, POOL_WIDTH, D_MODEL, D_MODEL)

LANES = 128
SUBLANES = 8
INT_MIN = -2147483648
NEG_BIG = -1e30
VMEM_LIMIT = 56 * 1024 * 1024

C_Q = 0
C_QI = 512
C_KK = 768
C_VW = 896
C_U = 1024
C_END = 1536
HALF = HEAD_DIM // 2

NT_DIMS = (((1,), (1,)), ((), ()))

Q_SCALE = HEAD_DIM ** -0.5 * float(np.log2(np.e))
QI_SCALE = IDX_DIM ** -0.5


def _params(sem):
    return pltpu.CompilerParams(dimension_semantics=sem, vmem_limit_bytes=VMEM_LIMIT)


def _layer_norm(x, g, b):
    mu = jnp.mean(x, axis=-1, keepdims=True)
    xc = x - mu
    var = jnp.mean(xc * xc, axis=-1, keepdims=True)
    return xc * lax.rsqrt(var + LN_EPS) * g + b


def _rope_rows(a, cos, sin):
    first_half = lax.broadcasted_iota(I32, (a.shape[0], LANES), 1) % HEAD_DIM < HALF
    out = []
    for s in range(a.shape[1] // LANES):
        x = a[:, s * LANES:(s + 1) * LANES]
        rot = jnp.where(first_half, -pltpu.roll(x, LANES - HALF, axis=1), pltpu.roll(x, HALF, axis=1))
        out.append(x * cos + rot * sin)
    return out[0] if len(out) == 1 else jnp.concatenate(out, axis=1)


def _proj_sample_kernel(x_ref, w_ref, cs_ref, q_ref, qi_ref, k_ref, v_ref, ki_ref, wi_ref, u_ref):
    xb = x_ref[...].astype(BF16)
    cos = cs_ref[:, 0:LANES]
    sin = cs_ref[:, LANES:2 * LANES]

    def mm(c0, n):
        return jnp.dot(xb, w_ref[:, c0:c0 + n], preferred_element_type=F32)

    def rope(c0, n):
        return _rope_rows(mm(c0, n), cos, sin)

    q_ref[...] = (rope(C_Q, ATT_WIDTH) * Q_SCALE).astype(BF16)
    qi_ref[...] = (rope(C_QI, IDX_HEADS * IDX_DIM) * QI_SCALE).astype(BF16)
    kk = rope(C_KK, LANES)
    k_ref[...] = kk[:, 0:HEAD_DIM]
    ki_ref[...] = kk[:, HEAD_DIM:2 * HEAD_DIM]
    vw = mm(C_VW, LANES)
    v_ref[...] = vw[:, 0:HEAD_DIM]
    wi_ref[...] = vw[:, HEAD_DIM:HEAD_DIM + IDX_HEADS] * (IDX_HEADS ** -0.5)
    u_ref[...] = mm(C_U, POOL_WIDTH)


def _proj_sample(x, w_big, cs):
    n = x.shape[0]
    full = lambda r, c: pl.BlockSpec((r, c), lambda i: (0, 0))
    widths = (ATT_WIDTH, IDX_HEADS * IDX_DIM, HEAD_DIM, HEAD_DIM, IDX_DIM, IDX_HEADS, POOL_WIDTH)
    dtypes = (BF16, BF16, F32, F32, F32, F32, F32)
    return pl.pallas_call(
        _proj_sample_kernel,
        grid=(1,),
        in_specs=[full(n, D_MODEL), full(D_MODEL, C_END), full(n, 2 * LANES)],
        out_specs=tuple(full(n, w) for w in widths),
        out_shape=tuple(jax.ShapeDtypeStruct((n, w), dt) for w, dt in zip(widths, dtypes)),
        compiler_params=_params(("arbitrary",)),
        name="proj_sample",
    )(x, w_big, cs)


def _proj_prompt_kernel(x_ref, w_ref, wt_ref, cs_ref, cst_ref, qt_ref, qit_ref, wit_ref, kb_ref, kib_ref, vbt_ref,
                        kt_ref, vt_ref, kit_ref, u_ref):
    xb = x_ref[...].astype(BF16)
    tm = xb.shape[0]
    cos = cs_ref[:, 0:LANES]
    sin = cs_ref[:, LANES:2 * LANES]
    cos_t = cst_ref[0:HEAD_DIM, :]
    sin_t = cst_ref[LANES:LANES + HEAD_DIM, :]

    def mm(c0, n):
        return jnp.dot(xb, w_ref[:, c0:c0 + n], preferred_element_type=F32)

    def mm_t(c0, n):
        return lax.dot_general(wt_ref[c0:c0 + n, :], xb, NT_DIMS, preferred_element_type=F32)

    def rope_t(c0, heads):
        a = mm_t(c0, heads * HEAD_DIM)
        parts = []
        for h in range(heads):
            x1 = a[h * HEAD_DIM:h * HEAD_DIM + HALF, :]
            x2 = a[h * HEAD_DIM + HALF:(h + 1) * HEAD_DIM, :]
            rot = jnp.concatenate([-x2, x1], axis=0)
            parts.append(a[h * HEAD_DIM:(h + 1) * HEAD_DIM, :] * cos_t + rot * sin_t)
        return parts[0] if heads == 1 else jnp.concatenate(parts, axis=0)

    kk = _rope_rows(mm(C_KK, LANES), cos, sin)
    kb_ref[...] = kk[:, 0:HEAD_DIM].astype(BF16)
    kib_ref[...] = kk[:, HEAD_DIM:2 * HEAD_DIM].astype(BF16)
    u_ref[...] = mm(C_U, POOL_WIDTH)

    qt = (rope_t(C_Q, N_HEADS) * Q_SCALE).astype(BF16)
    qit = (rope_t(C_QI, IDX_HEADS) * QI_SCALE).astype(BF16)
    for blk in range(tm // Q_BLOCK):
        cols = slice(blk * Q_BLOCK, (blk + 1) * Q_BLOCK)
        for h in range(N_HEADS):
            qt_ref[blk, :, h * Q_BLOCK:(h + 1) * Q_BLOCK] = qt[h * HEAD_DIM:(h + 1) * HEAD_DIM, cols]
        for h in range(IDX_HEADS):
            qit_ref[blk, :, h * Q_BLOCK:(h + 1) * Q_BLOCK] = qit[h * IDX_DIM:(h + 1) * IDX_DIM, cols]

    kkt = rope_t(C_KK, 2)
    kt_ref[...] = kkt[0:HEAD_DIM, :]
    kit_ref[...] = kkt[HEAD_DIM:2 * HEAD_DIM, :]
    vwt = mm_t(C_VW, LANES)
    vt_ref[...] = vwt[0:HEAD_DIM, :]
    vbt_ref[...] = vwt[0:HEAD_DIM, :].astype(BF16)
    wit_ref[...] = vwt[HEAD_DIM:HEAD_DIM + SUBLANES, :] * (IDX_HEADS ** -0.5)


def _proj_prompt(x, w_big, w_t, cs, seq, tm):
    n = x.shape[0]
    nb = seq // tm
    qb = tm // Q_BLOCK
    row = lambda w: pl.BlockSpec((tm, w), lambda i: (i, 0))
    col = lambda r: pl.BlockSpec((None, r, tm), lambda i: (i // nb, 0, i % nb))
    slab = lambda heads: pl.BlockSpec((qb, HEAD_DIM, heads * Q_BLOCK), lambda i: (i, 0, 0))
    pm = lambda r, dt: jax.ShapeDtypeStruct((n // seq, r, seq), dt)
    out_shape = (
        jax.ShapeDtypeStruct((n // Q_BLOCK, HEAD_DIM, N_HEADS * Q_BLOCK), BF16),
        jax.ShapeDtypeStruct((n // Q_BLOCK, IDX_DIM, IDX_HEADS * Q_BLOCK), BF16),
        pm(SUBLANES, F32),
        jax.ShapeDtypeStruct((n, HEAD_DIM), BF16), jax.ShapeDtypeStruct((n, IDX_DIM), BF16),
        pm(HEAD_DIM, BF16),
        pm(HEAD_DIM, F32), pm(HEAD_DIM, F32), pm(IDX_DIM, F32),
        jax.ShapeDtypeStruct((n, POOL_WIDTH), F32),
    )
    return pl.pallas_call(
        _proj_prompt_kernel,
        grid=(n // tm,),
        in_specs=[
            row(D_MODEL),
            pl.BlockSpec((D_MODEL, C_END), lambda i: (0, 0)),
            pl.BlockSpec((C_U, D_MODEL), lambda i: (0, 0)),
            pl.BlockSpec((tm, 2 * LANES), lambda i: (i % nb, 0)),
            pl.BlockSpec((2 * LANES, tm), lambda i: (0, i % nb)),
        ],
        out_specs=(slab(N_HEADS), slab(IDX_HEADS), col(SUBLANES), row(HEAD_DIM), row(IDX_DIM), col(HEAD_DIM),
                   col(HEAD_DIM), col(HEAD_DIM), col(IDX_DIM), row(POOL_WIDTH)),
        out_shape=out_shape,
        compiler_params=_params(("parallel",)),
        name="proj_prompt",
    )(x, w_big, w_t, cs, cs.T)


def _float_of_rank(u):
    key = u ^ INT_MIN
    bits = jnp.where(key < 0, INT_MIN - key, key)
    return pltpu.bitcast(bits, F32)


def _count(mask):
    return jnp.sum(mask.astype(F32), axis=1, keepdims=True)


def _topk_bias(sc_ref, j_ref, adm, n_adm, lc, k):
    rows = sc_ref.shape[0]
    kf = float(k)

    def value_step(i, t_u):
        hi = jnp.left_shift(jnp.int32(1), 31 - 2 * i)
        lo = jnp.left_shift(jnp.int32(1), 30 - 2 * i)
        for cand_u in (t_u | lo, t_u | hi, t_u | hi | lo):
            cnt = _count(sc_ref[:, 0:lc] >= _float_of_rank(cand_u))
            t_u = jnp.where(cnt >= kf, cand_u, t_u)
        return t_u

    t_u = lax.fori_loop(0, 16, value_step, jnp.zeros((rows, 1), I32))
    few = n_adm < k
    thr = jnp.where(few, -jnp.inf, _float_of_rank(t_u))
    sc = sc_ref[:, 0:lc]
    cnt_gt = _count(sc > thr)
    cnt_eq = _count(sc == thr)
    need = kf - cnt_gt
    cut_needed = jnp.logical_and(cnt_gt + cnt_eq > kf, jnp.logical_not(few))
    any_cut = jnp.max(cut_needed.astype(F32)) > 0.0
    idx = lax.broadcasted_iota(I32, (rows, lc), 1)
    nbits = int(np.ceil(np.log2(lc)))

    j_ref[...] = jnp.full((rows, 1), lc, I32)

    @pl.when(any_cut)
    def _():
        def index_step(i, j):
            cand = j | jnp.left_shift(jnp.int32(1), nbits - 1 - i)
            c = _count(jnp.logical_and(sc_ref[:, 0:lc] == thr, idx < cand))
            return jnp.where(c < need, cand, j)

        j_ref[...] = lax.fori_loop(0, nbits, index_step, jnp.zeros((rows, 1), I32))

    sel = jnp.logical_or(sc > thr, jnp.logical_and(sc == thr, idx <= j_ref[...]))
    return jnp.where(jnp.logical_and(sel, adm), 0.0, NEG_BIG)


ATTN_CHUNK = 256


def _attn_prompt_block(n_chunks, q0, top_k, qt_ref, qit_ref, wit_ref, kb_ref, kib_ref, vbt_ref, o_ref,
                       key_ref, bias_ref, lg_ref, j_ref):
    tq, ch = Q_BLOCK, ATTN_CHUNK
    seq = key_ref.shape[0]
    kf = float(top_k)
    kpos = lax.broadcasted_iota(I32, (ch, tq), 0)
    qpos = q0 + lax.broadcasted_iota(I32, (ch, tq), 1)

    def rows(c):
        return slice(c * ch, (c + 1) * ch)

    def fold(x, op):
        return op(x.reshape(ch // SUBLANES, SUBLANES, tq), axis=0)

    def head(x, h):
        return x[:, h * tq:(h + 1) * tq]

    qit = qit_ref[...]
    wit = wit_ref[...]
    for c in range(n_chunks if n_chunks * ch > top_k else 0):
        d = jnp.dot(kib_ref[rows(c), :], qit, preferred_element_type=F32)
        s = wit[0:1, :] * jnp.maximum(head(d, 0), 0.0)
        for h in range(1, IDX_HEADS):
            s = s + wit[h:h + 1, :] * jnp.maximum(head(d, h), 0.0)
        key_ref[rows(c), :] = jnp.where(c * ch + kpos <= qpos, s, -jnp.inf)

    def count(pred):
        acc = jnp.zeros((SUBLANES, tq), F32)
        for c in range(n_chunks):
            acc = acc + fold(pred(key_ref[rows(c), :], c).astype(F32), jnp.sum)
        return jnp.sum(acc, axis=0, keepdims=True)

    if n_chunks * ch <= top_k:
        for c in range(n_chunks):
            bias_ref[rows(c), :] = jnp.where(c * ch + kpos <= qpos, 0.0, NEG_BIG)
    else:
        def value_step(i, carry):
            t_u, n_ge = carry
            cand_u = t_u | jnp.left_shift(jnp.int32(1), 31 - i)
            cand = _float_of_rank(cand_u)
            cnt = count(lambda k, c: k >= cand)
            ok = cnt >= kf
            return jnp.where(ok, cand_u, t_u), jnp.where(ok, cnt, n_ge)

        t_u, n_ge = lax.fori_loop(0, 32, value_step,
                                  (jnp.zeros((1, tq), I32), jnp.full((1, tq), float(n_chunks * ch), F32)))
        few = qpos[0:1, :] + 1 <= top_k
        thr = jnp.where(few, -jnp.inf, _float_of_rank(t_u))
        cut_needed = jnp.logical_and(n_ge > kf, jnp.logical_not(few))
        any_cut = jnp.max(cut_needed.astype(F32)) > 0.0

        nbits = int(np.ceil(np.log2(seq)))
        j_ref[...] = jnp.full(j_ref.shape, seq, I32)

        @pl.when(any_cut)
        def _():
            need = kf - count(lambda k, c: k > thr)

            def index_step(i, j):
                cand = j | jnp.left_shift(jnp.int32(1), nbits - 1 - i)
                n_before = count(lambda k, c: jnp.logical_and(k == thr, c * ch + kpos < cand))
                return jnp.where(n_before < need, cand, j)

            j = lax.fori_loop(0, nbits, index_step, jnp.zeros((1, tq), I32))
            j_ref[...] = jnp.broadcast_to(j, j_ref.shape)

        j_cut = j_ref[0:1, :]
        for c in range(n_chunks):
            k = key_ref[rows(c), :]
            pos = c * ch + kpos
            sel = jnp.logical_or(k > thr, jnp.logical_and(k == thr, pos <= j_cut))
            bias_ref[rows(c), :] = jnp.where(jnp.logical_and(sel, pos <= qpos), 0.0, NEG_BIG)

    qt = qt_ref[...]
    mx = [jnp.full((SUBLANES, tq), -jnp.inf, F32) for _ in range(N_HEADS)]
    for c in range(n_chunks):
        lg = jnp.dot(kb_ref[rows(c), :], qt, preferred_element_type=F32)
        bias = bias_ref[rows(c), :]
        for h in range(N_HEADS):
            lgh = head(lg, h) + bias
            lg_ref[h, rows(c), :] = lgh
            mx[h] = jnp.maximum(mx[h], fold(lgh, jnp.max))

    outs = []
    for h in range(N_HEADS):
        m = jnp.max(mx[h], axis=0, keepdims=True)
        lsum = jnp.zeros((SUBLANES, tq), F32)
        ot = jnp.zeros((HEAD_DIM, tq), F32)
        for c in range(n_chunks):
            p = jnp.exp2(lg_ref[h, rows(c), :] - m)
            lsum = lsum + fold(p, jnp.sum)
            ot = ot + jnp.dot(vbt_ref[:, rows(c)], p.astype(BF16), preferred_element_type=F32)
        outs.append(ot / jnp.sum(lsum, axis=0, keepdims=True))
    o_ref[...] = jnp.concatenate(outs, axis=0).T.astype(BF16)


def _attn_prompt_kernel(qt_ref, qit_ref, wit_ref, kb_ref, kib_ref, vbt_ref, o_ref, key_ref, bias_ref, lg_ref, j_ref,
                        *, top_k):
    jq = pl.program_id(1)
    blocks_per_chunk = ATTN_CHUNK // Q_BLOCK
    n_classes = key_ref.shape[0] // ATTN_CHUNK
    for cls in range(n_classes):
        @pl.when(jq // blocks_per_chunk == cls)
        def _(cls=cls):
            _attn_prompt_block(cls + 1, jq * Q_BLOCK, top_k, qt_ref, qit_ref, wit_ref, kb_ref, kib_ref, vbt_ref,
                               o_ref, key_ref, bias_ref, lg_ref, j_ref)


def _attn_prompt(qt, qit, wit, kb, kib, vbt):
    batch, _, seq = vbt.shape
    nb = seq // Q_BLOCK
    top_k = min(TOP_K_MAX, seq // 4)
    slab = lambda heads: pl.BlockSpec((None, HEAD_DIM, heads * Q_BLOCK), lambda b, j: (b * nb + j, 0, 0))
    keys = pl.BlockSpec((seq, HEAD_DIM), lambda b, j: (b, 0))
    return pl.pallas_call(
        functools.partial(_attn_prompt_kernel, top_k=top_k),
        grid=(batch, nb),
        in_specs=[slab(N_HEADS), slab(IDX_HEADS), pl.BlockSpec((None, SUBLANES, Q_BLOCK), lambda b, j: (b, 0, j)),
                  keys, keys, pl.BlockSpec((None, HEAD_DIM, seq), lambda b, j: (b, 0, 0))],
        out_specs=pl.BlockSpec((Q_BLOCK, ATT_WIDTH), lambda b, j: (b * nb + j, 0)),
        out_shape=jax.ShapeDtypeStruct((batch * seq, ATT_WIDTH), BF16),
        scratch_shapes=[pltpu.VMEM((seq, Q_BLOCK), F32), pltpu.VMEM((seq, Q_BLOCK), F32),
                        pltpu.VMEM((N_HEADS, seq, Q_BLOCK), F32), pltpu.VMEM((SUBLANES, Q_BLOCK), I32)],
        compiler_params=_params(("parallel", "arbitrary")),
        name="attn_prompt",
    )(qt, qit, wit, kb, kib, vbt)


SAMPLE_CHUNK = 1024


def _attn_sample_kernel(pt_ref, q_ref, qi_ref, wi_ref, kn_ref, vn_ref, kin_ref, ck_hbm, cv_hbm, cki_hbm, o_ref,
                        kbuf, vbuf, kibuf, sem, key_scr, bias_scr, lg_scr, j_scr, *, n_pages, page, t_new, top_k):
    b = pl.program_id(0)
    n_b = pl.num_programs(0)
    slot = b % 2
    past = n_pages * page
    lc = past + page
    n_chunks = past // SAMPLE_CHUNK

    def page_copies(bb, sl, p):
        phys = pt_ref[bb * n_pages + p]
        dst = pl.ds(pl.multiple_of(p * page, page), page)
        return [pltpu.make_async_copy(src.at[phys], buf.at[sl, :, dst], sem.at[i, sl])
                for i, (src, buf) in enumerate(((ck_hbm, kbuf), (cv_hbm, vbuf), (cki_hbm, kibuf)))]

    def start_batch(bb, sl):
        def body(p, carry):
            for cp in page_copies(bb, sl, p):
                cp.start()
            return carry
        lax.fori_loop(0, n_pages, body, 0)

    def wait_batch(bb, sl):
        def body(p, carry):
            for cp in page_copies(bb, sl, p):
                cp.wait()
            return carry
        lax.fori_loop(0, n_pages, body, 0)

    @pl.when(b == 0)
    def _():
        start_batch(0, 0)

    @pl.when(b + 1 < n_b)
    def _():
        start_batch(b + 1, 1 - slot)

    wait_batch(b, slot)

    def head_sum(d):
        r = wi_ref[...] * jnp.maximum(d, 0.0)
        s = r[0:t_new]
        for h in range(1, IDX_HEADS):
            s = s + r[h * t_new:(h + 1) * t_new]
        return s

    def new_rows(ref):
        pad = jnp.zeros((page - t_new, ref.shape[1]), F32)
        return jnp.concatenate([ref[...], pad], axis=0).astype(BF16)

    qi = qi_ref[...]
    for c in range(n_chunks):
        sl = slice(c * SAMPLE_CHUNK, (c + 1) * SAMPLE_CHUNK)
        d = jnp.dot(qi, kibuf[slot, :, sl].astype(BF16), preferred_element_type=F32)
        key_scr[:, sl] = head_sum(d)
    d_new = lax.dot_general(qi, new_rows(kin_ref), NT_DIMS, preferred_element_type=F32)
    adm_new = lax.broadcasted_iota(I32, (t_new, page), 1) <= lax.broadcasted_iota(I32, (t_new, page), 0)
    key_scr[:, past:lc] = jnp.where(adm_new, head_sum(d_new), -jnp.inf)

    idx = lax.broadcasted_iota(I32, (t_new, lc), 1)
    trow = lax.broadcasted_iota(I32, (t_new, lc), 0)
    n_adm = past + 1 + lax.broadcasted_iota(I32, (t_new, 1), 0)
    bias_scr[...] = _topk_bias(key_scr, j_scr, idx - past <= trow, n_adm, lc, top_k)

    q = q_ref[...]

    def bias_rows(sl):
        return jnp.concatenate([bias_scr[:, sl]] * N_HEADS, axis=0)

    m = jnp.full((N_HEADS * t_new, 1), -jnp.inf, F32)
    for c in range(n_chunks):
        sl = slice(c * SAMPLE_CHUNK, (c + 1) * SAMPLE_CHUNK)
        lg = jnp.dot(q, kbuf[slot, :, sl].astype(BF16), preferred_element_type=F32) + bias_rows(sl)
        lg_scr[:, sl] = lg
        m = jnp.maximum(m, jnp.max(lg, axis=1, keepdims=True))
    lg_new = lax.dot_general(q, new_rows(kn_ref), NT_DIMS, preferred_element_type=F32) + bias_rows(slice(past, lc))
    m = jnp.maximum(m, jnp.max(lg_new, axis=1, keepdims=True))

    p_new = jnp.exp2(lg_new - m)
    l = jnp.sum(p_new, axis=1, keepdims=True)
    o = jnp.dot(p_new.astype(BF16), new_rows(vn_ref), preferred_element_type=F32)
    for c in range(n_chunks):
        sl = slice(c * SAMPLE_CHUNK, (c + 1) * SAMPLE_CHUNK)
        pr = jnp.exp2(lg_scr[:, sl] - m)
        l = l + jnp.sum(pr, axis=1, keepdims=True)
        o = o + lax.dot_general(pr.astype(BF16), vbuf[slot, :, sl].astype(BF16), NT_DIMS,
                                preferred_element_type=F32)
    o_ref[...] = o / l


def _attn_sample(page_table, q_hq, qi_hq, wi_hq, k_new, v_new, ki_new, cache_kt, cache_vt, cache_kit):
    db, n_pages = page_table.shape
    page = cache_kt.shape[2]
    t_new = k_new.shape[1]
    past = n_pages * page
    lc = past + page
    top_k = min(TOP_K_MAX, (past + t_new) // 4)
    per_b = lambda r, w: pl.BlockSpec((None, r, w), lambda b, pt: (b, 0, 0))
    hbm = pl.BlockSpec(memory_space=pl.ANY)
    kern = functools.partial(_attn_sample_kernel, n_pages=n_pages, page=page, t_new=t_new, top_k=top_k)
    slab = pltpu.VMEM((2, HEAD_DIM, past), F32)
    grid_spec = pltpu.PrefetchScalarGridSpec(
        num_scalar_prefetch=1,
        grid=(db,),
        in_specs=[per_b(N_HEADS * t_new, HEAD_DIM), per_b(IDX_HEADS * t_new, IDX_DIM), per_b(IDX_HEADS * t_new, 1),
                  per_b(t_new, HEAD_DIM), per_b(t_new, HEAD_DIM), per_b(t_new, IDX_DIM),
                  hbm, hbm, hbm],
        out_specs=per_b(N_HEADS * t_new, HEAD_DIM),
        scratch_shapes=[slab, slab, slab, pltpu.SemaphoreType.DMA((3, 2)),
                        pltpu.VMEM((t_new, lc), F32), pltpu.VMEM((t_new, lc), F32),
                        pltpu.VMEM((N_HEADS * t_new, past), F32), pltpu.VMEM((t_new, 1), I32)],
    )
    return pl.pallas_call(
        kern,
        grid_spec=grid_spec,
        out_shape=jax.ShapeDtypeStruct((db, N_HEADS * t_new, HEAD_DIM), F32),
        compiler_params=_params(("arbitrary",)),
        name="attn_sample",
    )(page_table.reshape(-1), q_hq, qi_hq, wi_hq, k_new, v_new, ki_new, cache_kt, cache_vt, cache_kit)


PREV_ROWS = 16


def _pool_kernel(prev_ref, u_ref, wg_ref, sc_ref, o_ref, ext_ref, *, pos0):
    per_step, t_len, _ = u_ref.shape
    pos = pos0 + lax.broadcasted_iota(I32, (t_len, 1), 0)
    for b in range(per_step):
        ext_ref[0:PREV_ROWS, :] = prev_ref[b]
        ext_ref[PREV_ROWS:PREV_ROWS + t_len, :] = u_ref[b]
        for g, w in enumerate(POOL_WINDOWS):
            sl = slice(g * POOL_GW, (g + 1) * POOL_GW)
            u_new = ext_ref[PREV_ROWS:PREV_ROWS + t_len, sl]
            win = u_new
            for back in range(1, w):
                win = win + ext_ref[PREV_ROWS - back:PREV_ROWS - back + t_len, sl]
            count = jnp.minimum(pos + 1, w).astype(F32)
            r = win / count - u_new
            mixed = jnp.dot(r.astype(BF16), wg_ref[g], preferred_element_type=F32) * sc_ref[:, sl]
            o_ref[b, :, sl] = mixed.astype(BF16)


def _pool(prev, u, w_grp, scale, pos0, per_step):
    nb, t_len, _ = u.shape
    seqs = lambda rows: pl.BlockSpec((per_step, rows, POOL_WIDTH), lambda b: (b, 0, 0))
    return pl.pallas_call(
        functools.partial(_pool_kernel, pos0=pos0),
        grid=(nb // per_step,),
        in_specs=[seqs(PREV_ROWS), seqs(t_len),
                  pl.BlockSpec((POOL_GROUPS, POOL_GW, POOL_GW), lambda b: (0, 0, 0)),
                  pl.BlockSpec((1, POOL_WIDTH), lambda b: (0, 0))],
        out_specs=seqs(t_len),
        out_shape=jax.ShapeDtypeStruct((nb, t_len, POOL_WIDTH), BF16),
        scratch_shapes=[pltpu.VMEM((PREV_ROWS + t_len, POOL_WIDTH), F32)],
        compiler_params=_params(("parallel",)),
        name="pool",
    )(prev, u, w_grp, scale)


def _merge_kernel(x_ref, a_ref, p_ref, wga_ref, wgb_ref, wao_ref, wpo_ref, wo_ref, g_ref, b_ref, h_ref, hp_ref, *,
                  alpha):
    x = x_ref[...]
    xb = x.astype(BF16)
    ga = jnp.dot(xb, wga_ref[...], preferred_element_type=F32)
    gb = jnp.dot(xb, wgb_ref[...], preferred_element_type=F32)
    ya = jnp.dot(a_ref[...], wao_ref[...], preferred_element_type=F32)
    yp = jnp.dot(p_ref[...], wpo_ref[...], preferred_element_type=F32)
    mix = jax.nn.sigmoid(ga) * ya + jax.nn.sigmoid(gb) * yp
    out = jnp.dot(mix.astype(BF16), wo_ref[...], preferred_element_type=F32)
    h = _layer_norm(alpha * x + out, g_ref[...], b_ref[...])
    h_ref[...] = h
    hp_ref[...] = _pack_rows(h)


def _merge(x, attn, pool, wga, wgb, wao, wpo, wo, g, b, tm, alpha):
    n = x.shape[0]
    row = lambda w: pl.BlockSpec((tm, w), lambda i: (i, 0))
    full = lambda r, c: pl.BlockSpec((r, c), lambda i: (0, 0), pipeline_mode=pl.Buffered(1))
    return pl.pallas_call(
        functools.partial(_merge_kernel, alpha=alpha),
        grid=(n // tm,),
        in_specs=[row(D_MODEL), row(ATT_WIDTH), row(POOL_WIDTH), full(D_MODEL, D_MODEL), full(D_MODEL, D_MODEL),
                  full(ATT_WIDTH, D_MODEL), full(POOL_WIDTH, D_MODEL), full(D_MODEL, D_MODEL),
                  full(1, D_MODEL), full(1, D_MODEL)],
        out_specs=(row(D_MODEL), row(PACKED)),
        out_shape=(jax.ShapeDtypeStruct((n, D_MODEL), F32), jax.ShapeDtypeStruct((n, PACKED), I32)),
        compiler_params=_params(("parallel",)),
        name="merge",
    )(x, attn, pool, wga, wgb, wao, wpo, wo, g, b)


def _route(h, wr_t, bias_col):
    tm = h.shape[0]
    logits = lax.dot_general(wr_t, h.astype(BF16), NT_DIMS, preferred_element_type=F32)
    s = jax.nn.sigmoid(logits)
    sb = s + bias_col
    neg_inf = -jnp.inf

    rows = []
    for g in range(N_GROUPS):
        blk = sb[g * GROUP_SIZE:(g + 1) * GROUP_SIZE, :]
        m1 = jnp.max(blk, axis=0, keepdims=True)
        is_m1 = blk == m1
        n_m1 = jnp.sum(is_m1.astype(F32), axis=0, keepdims=True)
        m2 = jnp.max(jnp.where(is_m1, neg_inf, blk), axis=0, keepdims=True)
        rows.append(m1 + jnp.where(n_m1 >= 2.0, m1, m2))
    gs = jnp.concatenate(rows, axis=0)

    gi = lax.broadcasted_iota(I32, (N_GROUPS, tm), 0)
    rank = jnp.zeros((N_GROUPS, tm), F32)
    for g in range(N_GROUPS):
        row = gs[g:g + 1, :]
        beats = jnp.logical_or(row > gs, jnp.logical_and(row == gs, g < gi))
        rank = rank + beats.astype(F32)
    gkeep = rank < float(TOPK_GROUPS)
    emask = jnp.concatenate(
        [jnp.broadcast_to(gkeep[g:g + 1, :], (GROUP_SIZE, tm)) for g in range(N_GROUPS)], axis=0)

    ei = lax.broadcasted_iota(I32, (N_EXPERTS, tm), 0)
    x = jnp.where(emask, sb, neg_inf)
    sel = jnp.zeros((N_EXPERTS, tm), jnp.bool_)
    picks = []
    for _ in range(TOP_K_EXPERTS):
        m = jnp.max(x, axis=0, keepdims=True)
        first = jnp.min(jnp.where(x == m, ei, N_EXPERTS), axis=0, keepdims=True)
        pick = ei == first
        sel = jnp.logical_or(sel, pick)
        x = jnp.where(pick, neg_inf, x)
        picks.append(first)

    gate = jnp.where(sel, s, 0.0)
    comb = gate / jnp.sum(gate, axis=0, keepdims=True) * ROUTED_SCALE
    return comb, sel, picks


def _router_kernel(h_ref, wr_ref, bias_ref, c_ref):
    comb, _, _ = _route(h_ref[...], wr_ref[...], bias_ref[...])
    comb = jnp.concatenate([comb, jnp.zeros((LANES - N_EXPERTS, comb.shape[1]), F32)], axis=0)
    c_ref[...] = comb.T


def _router(h, wr_t, bias_col, tm):
    n = h.shape[0]
    return pl.pallas_call(
        _router_kernel,
        grid=(n // tm,),
        in_specs=[pl.BlockSpec((tm, D_MODEL), lambda i: (i, 0)),
                  pl.BlockSpec((N_EXPERTS, D_MODEL), lambda i: (0, 0)),
                  pl.BlockSpec((N_EXPERTS, 1), lambda i: (0, 0))],
        out_specs=pl.BlockSpec((tm, LANES), lambda i: (i, 0)),
        out_shape=jax.ShapeDtypeStruct((n, LANES), F32),
        compiler_params=_params(("parallel",)),
        name="router",
    )(h, wr_t, bias_col)


def _swiglu(xb, w13, w2, hidden):
    ab = jnp.dot(xb, w13, preferred_element_type=F32)
    act = jax.nn.silu(ab[:, 0:hidden]) * ab[:, hidden:2 * hidden]
    return jnp.dot(act.astype(BF16), w2, preferred_element_type=F32)


def _moe_kernel(h_ref, c_ref, ws13_ref, ws2_ref, w13_ref, w2_ref, y_ref, hb_ref):
    e = pl.program_id(1)

    @pl.when(e == 0)
    def _():
        hb_ref[...] = h_ref[...].astype(BF16)
        y_ref[...] = _swiglu(hb_ref[...], ws13_ref[...], ws2_ref[...], SHARED_DIM)

    ye = _swiglu(hb_ref[...], w13_ref[...].astype(BF16), w2_ref[...].astype(BF16), EXPERT_DIM)
    lane = lax.broadcasted_iota(I32, c_ref.shape, 1)
    ce = jnp.sum(jnp.where(lane == e, c_ref[...], 0.0), axis=1, keepdims=True)
    y_ref[...] += ce * ye


def _moe(h, comb, ws13, ws2, w13, w2, tm):
    n = h.shape[0]
    return pl.pallas_call(
        _moe_kernel,
        grid=(n // tm, N_EXPERTS),
        in_specs=[pl.BlockSpec((tm, D_MODEL), lambda i, e: (i, 0)),
                  pl.BlockSpec((tm, LANES), lambda i, e: (i, 0)),
                  pl.BlockSpec((D_MODEL, 2 * SHARED_DIM), lambda i, e: (0, 0)),
                  pl.BlockSpec((SHARED_DIM, D_MODEL), lambda i, e: (0, 0)),
                  pl.BlockSpec((None, D_MODEL, 2 * EXPERT_DIM), lambda i, e: (e, 0, 0)),
                  pl.BlockSpec((None, EXPERT_DIM, D_MODEL), lambda i, e: (e, 0, 0))],
        out_specs=pl.BlockSpec((tm, D_MODEL), lambda i, e: (i, 0)),
        out_shape=jax.ShapeDtypeStruct((n, D_MODEL), F32),
        scratch_shapes=[pltpu.VMEM((tm, D_MODEL), BF16)],
        compiler_params=_params(("parallel", "arbitrary")),
        name="moe",
    )(h, comb, ws13, ws2, w13, w2)


def _final_kernel(h_ref, y_ref, pe_ref, g_ref, b_ref, wpg_ref, wpi_ref, o_ref, *, alpha):
    z = _layer_norm(alpha * h_ref[...] + y_ref[...], g_ref[...], b_ref[...])
    gate = jax.nn.sigmoid(jnp.dot(z.astype(BF16), wpg_ref[...], preferred_element_type=F32))
    emb = jnp.dot(pe_ref[...].astype(BF16), wpi_ref[...], preferred_element_type=F32)
    o_ref[...] = z + gate * emb


def _final(h, y, pe, g, b, wpg, wpi, tm, alpha):
    n = h.shape[0]
    row = lambda w: pl.BlockSpec((tm, w), lambda i: (i, 0))
    full = lambda r, c: pl.BlockSpec((r, c), lambda i: (0, 0))
    return pl.pallas_call(
        functools.partial(_final_kernel, alpha=alpha),
        grid=(n // tm,),
        in_specs=[row(D_MODEL), row(D_MODEL), row(PLE_DIM), full(1, D_MODEL), full(1, D_MODEL),
                  full(D_MODEL, D_MODEL), full(PLE_DIM, D_MODEL)],
        out_specs=row(D_MODEL),
        out_shape=jax.ShapeDtypeStruct((n, D_MODEL), F32),
        compiler_params=_params(("parallel",)),
        name="final",
    )(h, y, pe, g, b, wpg, wpi)


MOE_BLOCK = 1024


def _sorted_rows(n_tokens):
    worst = n_tokens * TOP_K_EXPERTS + N_EXPERTS * (MOE_BLOCK - 1)
    return -(-worst // MOE_BLOCK) * MOE_BLOCK


def _dispatch_kernel(h_ref, wr_ref, bias_ref, tri_ref, pos_ref, gate_ref, blk_ref, used_ref,
                     eidx_s, rank_s, gate_s, cnt_s):
    p = pl.program_id(0)
    i = pl.program_id(1)
    tm = h_ref.shape[0]
    ei = lax.broadcasted_iota(I32, (N_EXPERTS, tm), 0)

    @pl.when(p == 0)
    def _():
        comb, sel, picks = _route(h_ref[...], wr_ref[...], bias_ref[...])
        before = jnp.dot(sel.astype(BF16), tri_ref[...], preferred_element_type=F32)
        ranks, gates = [], []
        for first in picks:
            pick = ei == first
            ranks.append(jnp.sum(jnp.where(pick, before, 0.0), axis=0, keepdims=True))
            gates.append(jnp.sum(jnp.where(pick, comb, 0.0), axis=0, keepdims=True))
        eidx_s[i] = jnp.concatenate(picks, axis=0)
        rank_s[i] = jnp.concatenate(ranks, axis=0)
        gate_s[i] = jnp.concatenate(gates, axis=0)
        cnt_s[i] = jnp.broadcast_to(jnp.sum(sel.astype(F32), axis=1, keepdims=True), (N_EXPERTS, LANES))

    @pl.when(p == 1)
    def _():
        cnt = cnt_s[...]
        tile_id = lax.broadcasted_iota(I32, cnt.shape, 0)
        total = jnp.sum(cnt, axis=0)
        prior = jnp.sum(jnp.where(tile_id < i, cnt, 0.0), axis=0)
        seg = jnp.ceil(total * (1.0 / MOE_BLOCK)) * MOE_BLOCK
        lower = (lax.broadcasted_iota(I32, (N_EXPERTS, N_EXPERTS), 1)
                 < lax.broadcasted_iota(I32, (N_EXPERTS, N_EXPERTS), 0)).astype(F32)
        seg_off = jnp.dot(lower, seg, precision=lax.Precision.HIGHEST, preferred_element_type=F32)
        base = (seg_off + prior)[:, 0:1]
        eidx = eidx_s[i]
        rank = rank_s[i]
        rows = []
        for k in range(TOP_K_EXPERTS):
            pick = ei == eidx[k:k + 1, :]
            rows.append(rank[k:k + 1, :] + jnp.sum(jnp.where(pick, base, 0.0), axis=0, keepdims=True))
        pos_ref[...] = jnp.concatenate(rows, axis=0).astype(I32)
        gate_ref[...] = jnp.concatenate([gate_s[i], jnp.zeros((LANES - TOP_K_EXPERTS, tm), F32)], axis=0).T

        seg_end = (seg_off + seg)[:, 0:1]
        n_blk = blk_ref.shape[1]
        blk_start = (lax.broadcasted_iota(I32, (N_EXPERTS, n_blk), 1) * MOE_BLOCK).astype(F32)
        owner = jnp.sum((seg_end <= blk_start).astype(F32), axis=0, keepdims=True)
        blk_ref[...] = jnp.minimum(owner, N_EXPERTS - 1.0).astype(I32)
        used = seg_end[N_EXPERTS - 1:N_EXPERTS, :] * (1.0 / MOE_BLOCK)
        used_ref[...] = jnp.broadcast_to(used, used_ref.shape).astype(I32)


def _dispatch(h, wr_t, bias_col, tm):
    n = h.shape[0]
    n_tiles = n // tm
    n_blk = _sorted_rows(n) // MOE_BLOCK
    n_blk_pad = -(-n_blk // LANES) * LANES
    tri = jnp.triu(jnp.ones((tm, tm), BF16), k=1)
    const = lambda r, c: pl.BlockSpec((r, c), lambda p, i: (0, 0))
    per_tile = lambda dt: pltpu.VMEM((n_tiles, TOP_K_EXPERTS, tm), dt)
    return pl.pallas_call(
        _dispatch_kernel,
        grid=(2, n_tiles),
        in_specs=[pl.BlockSpec((tm, D_MODEL), lambda p, i: (i * (1 - p), 0)),
                  const(N_EXPERTS, D_MODEL), const(N_EXPERTS, 1), const(tm, tm)],
        out_specs=(pl.BlockSpec((TOP_K_EXPERTS, tm), lambda p, i: (0, i * p)),
                   pl.BlockSpec((tm, LANES), lambda p, i: (i * p, 0)),
                   const(1, n_blk_pad), const(1, LANES)),
        out_shape=(jax.ShapeDtypeStruct((TOP_K_EXPERTS, n), I32), jax.ShapeDtypeStruct((n, LANES), F32),
                   jax.ShapeDtypeStruct((1, n_blk_pad), I32), jax.ShapeDtypeStruct((1, LANES), I32)),
        scratch_shapes=[per_tile(I32), per_tile(F32), per_tile(F32), pltpu.VMEM((n_tiles, N_EXPERTS, LANES), F32)],
        compiler_params=_params(("arbitrary", "arbitrary")),
        name="dispatch",
    )(h, wr_t, bias_col, tri)


PACKED = D_MODEL // 2


def _pack_rows(x):
    lo = pltpu.bitcast(x[:, 0:PACKED].astype(BF16).astype(F32), I32)
    hi = pltpu.bitcast(x[:, PACKED:D_MODEL].astype(BF16).astype(F32), I32)
    return jnp.bitwise_or(hi, lax.shift_right_logical(lo, 16))


def _unpack_rows_f32(w):
    lo = pltpu.bitcast(lax.shift_left(w, 16), F32)
    hi = pltpu.bitcast(jnp.bitwise_and(w, -65536), F32)
    return jnp.concatenate([lo, hi], axis=1)


def _unpack_rows(w):
    return _unpack_rows_f32(w).astype(BF16)


def _grouped_kernel(blk_ref, used_ref, anchor_ref, xs_ref, w13_ref, w2_ref, ys_ref):
    @pl.when(pl.program_id(0) < used_ref[0])
    def _():
        ys = _swiglu(_unpack_rows(xs_ref[...]), w13_ref[...].astype(BF16), w2_ref[...].astype(BF16), EXPERT_DIM)
        ys_ref[...] = _pack_rows(ys)


def _grouped(blk, used, anchor, xs, w13, w2):
    ns = xs.shape[0]
    row_blk = lambda b, blk, used, anchor: (jnp.minimum(b, used[0] - 1), 0)
    expert = lambda b, blk, used, anchor: (blk[b], 0, 0)
    grid_spec = pltpu.PrefetchScalarGridSpec(
        num_scalar_prefetch=3,
        grid=(ns // MOE_BLOCK,),
        in_specs=[pl.BlockSpec((MOE_BLOCK, PACKED), row_blk),
                  pl.BlockSpec((None, D_MODEL, 2 * EXPERT_DIM), expert),
                  pl.BlockSpec((None, EXPERT_DIM, D_MODEL), expert)],
        out_specs=pl.BlockSpec((MOE_BLOCK, PACKED), row_blk),
    )
    return pl.pallas_call(
        _grouped_kernel,
        grid_spec=grid_spec,
        out_shape=jax.ShapeDtypeStruct((ns, PACKED), I32),
        compiler_params=_params(("arbitrary",)),
        name="grouped",
    )(blk, used, anchor, xs, w13, w2)


SC_WINDOW = 128


def _sc_mesh():
    return plsc.VectorSubcoreMesh(core_axis_name="core", subcore_axis_name="subcore")


def _sc_worker(n_items):
    info = plsc.get_sparse_core_info()
    n_workers = info.num_cores * info.num_subcores
    wid = lax.axis_index("subcore") * info.num_cores + lax.axis_index("core")
    return wid, n_items // (SC_WINDOW * n_workers)


def _scatter_rows(x, pos, n_out):
    n, width = x.shape
    picks = pos.shape[0]

    @functools.partial(
        pl.kernel, mesh=_sc_mesh(), out_type=jax.ShapeDtypeStruct((n_out, width), I32),
        scratch_types=[pltpu.VMEM((picks, SC_WINDOW), I32), pltpu.VMEM((SC_WINDOW, width), I32)],
        name="scatter_rows")
    def scatter(x_hbm, pos_hbm, out_hbm, idx_v, rows_v):
        wid, n_win = _sc_worker(n)

        @pl.loop(0, n_win)
        def _(j):
            base = (wid * n_win + j) * SC_WINDOW
            pltpu.sync_copy(pos_hbm.at[:, pl.ds(base, SC_WINDOW)], idx_v)
            pltpu.sync_copy(x_hbm.at[pl.ds(base, SC_WINDOW)], rows_v)
            for k in range(picks):
                pltpu.sync_copy(rows_v, out_hbm.at[idx_v.at[k]])

    return scatter(x, pos)


def _gather_rows(src, pos):
    width = src.shape[1]
    picks, n = pos.shape

    @functools.partial(
        pl.kernel, mesh=_sc_mesh(), out_type=jax.ShapeDtypeStruct((picks * n, width), I32),
        scratch_types=[pltpu.VMEM((SC_WINDOW,), I32), pltpu.VMEM((SC_WINDOW, width), I32)],
        name="gather_rows")
    def gather(src_hbm, pos_hbm, out_hbm, idx_v, rows_v):
        wid, n_win = _sc_worker(picks * n)

        @pl.loop(0, n_win)
        def _(j):
            base = (wid * n_win + j) * SC_WINDOW
            pltpu.sync_copy(pos_hbm.at[pl.ds(base, SC_WINDOW)], idx_v)
            pltpu.sync_copy(src_hbm.at[idx_v], rows_v)
            pltpu.sync_copy(rows_v, out_hbm.at[pl.ds(base, SC_WINDOW)])

    return gather(src, pos.reshape(-1)).reshape(picks, n, width)


def _combine_kernel(h_ref, g_ref, gate_ref, pe_ref, ws13_ref, ws2_ref, ln_g_ref, ln_b_ref, wpg_ref, wpi_ref, o_ref, *,
                    alpha):
    h = h_ref[...]
    y = _swiglu(h.astype(BF16), ws13_ref[...], ws2_ref[...], SHARED_DIM)
    gate = gate_ref[...]
    for k in range(TOP_K_EXPERTS):
        y = y + gate[:, k:k + 1] * _unpack_rows_f32(g_ref[k])
    z = _layer_norm(alpha * h + y, ln_g_ref[...], ln_b_ref[...])
    ple_gate = jax.nn.sigmoid(jnp.dot(z.astype(BF16), wpg_ref[...], preferred_element_type=F32))
    emb = jnp.dot(pe_ref[...].astype(BF16), wpi_ref[...], preferred_element_type=F32)
    o_ref[...] = z + ple_gate * emb


def _combine(h, gathered, gate, pe, ws13, ws2, g, b, wpg, wpi, tm, alpha):
    n = h.shape[0]
    row = lambda w: pl.BlockSpec((tm, w), lambda i: (i, 0))
    full = lambda r, c: pl.BlockSpec((r, c), lambda i: (0, 0))
    return pl.pallas_call(
        functools.partial(_combine_kernel, alpha=alpha),
        grid=(n // tm,),
        in_specs=[row(D_MODEL), pl.BlockSpec((TOP_K_EXPERTS, tm, PACKED), lambda i: (0, i, 0)), row(LANES),
                  row(PLE_DIM), full(D_MODEL, 2 * SHARED_DIM), full(SHARED_DIM, D_MODEL),
                  full(1, D_MODEL), full(1, D_MODEL), full(D_MODEL, D_MODEL), full(PLE_DIM, D_MODEL)],
        out_specs=row(D_MODEL),
        out_shape=jax.ShapeDtypeStruct((n, D_MODEL), F32),
        compiler_params=_params(("parallel",)),
        name="combine",
    )(h, gathered, gate, pe, ws13, ws2, g, b, wpg, wpi)


def _rope_table(pos):
    inv = ROPE_THETA ** (-jnp.arange(0, HEAD_DIM, 2, dtype=F32) / HEAD_DIM)
    ang = pos.astype(F32)[:, None] * inv[None, :]
    return jnp.concatenate([jnp.tile(jnp.cos(ang), (1, 4)), jnp.tile(jnp.sin(ang), (1, 4))], axis=1)


def _fused_in_weight(w_in):
    offs = np.cumsum(IN_SIZES)[:-1].tolist()
    wq, wk, wv, wqi, wki, wwi, wu, wga, wgb = jnp.split(w_in, offs, axis=1)
    pad = jnp.zeros((D_MODEL, LANES - HEAD_DIM - IDX_HEADS), w_in.dtype)
    w_big = jnp.concatenate([wq, wqi, wk, wki, wv, wwi, pad, wu], axis=1).astype(BF16)
    return w_big, w_big[:, 0:C_U].T, wga.astype(BF16), wgb.astype(BF16)


def _pages_transposed(cache):
    return jnp.transpose(cache[0], (0, 2, 1))


def _heads_major(a, n_heads):
    b, t, w = a.shape
    d = w // n_heads
    return a.reshape(b, t, n_heads, d).transpose(0, 2, 1, 3).reshape(b, n_heads * t, d)


def kernel(x_prompt, x_sample, cache_k, cache_v, cache_kidx, state_pool, page_table, p_prompt, p_sample, w_in, w_att_out, w_pool_grp, pool_scale, w_pool_out, w_out, ln1_g, ln1_b, w_router, router_bias, w_exp13, w_exp2, w_sh13, w_sh2, ln2_g, ln2_b, w_ple_in, w_ple_gate):
    B, S, D = x_prompt.shape
    DB, T, _ = x_sample.shape
    depth = w_in.shape[0]
    assert depth == 1, "single layer step"
    page = cache_k.shape[2]
    past = page_table.shape[1] * page
    alpha = (2 * depth) ** 0.25
    n_p, n_s = B * S, DB * T

    w_big, w_t, wga, wgb = _fused_in_weight(w_in[0])
    wao, wpo, wo = w_att_out[0].astype(BF16), w_pool_out[0].astype(BF16), w_out[0].astype(BF16)
    wgrp = w_pool_grp[0].astype(BF16)
    pscale = pool_scale[0].reshape(1, POOL_WIDTH)
    g1, b1 = ln1_g[0].reshape(1, D), ln1_b[0].reshape(1, D)
    g2, b2 = ln2_g[0].reshape(1, D), ln2_b[0].reshape(1, D)
    wr_t = w_router[0].T.astype(BF16)
    rbias = router_bias[0].reshape(N_EXPERTS, 1)
    w13, w2 = w_exp13[0], w_exp2[0]
    ws13, ws2 = w_sh13[0].astype(BF16), w_sh2[0].astype(BF16)
    wpg, wpi = w_ple_gate[0].astype(BF16), w_ple_in[0].astype(BF16)

    cs_p = _rope_table(jnp.arange(S, dtype=I32))
    cs_s = jnp.tile(_rope_table(past + jnp.arange(T, dtype=I32)), (DB, 1))

    xp = x_prompt.reshape(n_p, D)
    qt, qit, wit, kb, kib, vbt, kt, vt, kit, u = _proj_prompt(xp, w_big, w_t, cs_p, S, 512)
    attn_p = _attn_prompt(qt, qit, wit, kb, kib, vbt)
    u3 = u.reshape(B, S, POOL_WIDTH)
    pool_p = _pool(jnp.zeros((B, PREV_ROWS, POOL_WIDTH), F32), u3, wgrp, pscale, 0, 1).reshape(n_p, POOL_WIDTH)
    h_p, hp_p = _merge(xp, attn_p, pool_p, wga, wgb, wao, wpo, wo, g1, b1, 1024, alpha)

    xs = x_sample.reshape(n_s, D)
    qs, qis, ks, vs, kis, wis, us = _proj_sample(xs, w_big, cs_s)
    q_hq = _heads_major(qs.reshape(DB, T, ATT_WIDTH), N_HEADS)
    qi_hq = _heads_major(qis.reshape(DB, T, IDX_HEADS * IDX_DIM), IDX_HEADS)
    wi_hq = wis.reshape(DB, T, IDX_HEADS).transpose(0, 2, 1).reshape(DB, IDX_HEADS * T, 1)
    caches = (_pages_transposed(cache_k), _pages_transposed(cache_v), _pages_transposed(cache_kidx))
    new_rows = (ks.reshape(DB, T, HEAD_DIM), vs.reshape(DB, T, HEAD_DIM), kis.reshape(DB, T, IDX_DIM))
    half = DB // 2
    o_halves = [_attn_sample(page_table[sl], q_hq[sl], qi_hq[sl], wi_hq[sl], *(a[sl] for a in new_rows), *caches)
                for sl in (slice(0, half), slice(half, DB))]
    o_hq = jnp.concatenate(o_halves, axis=0)
    attn_s = o_hq.reshape(DB, N_HEADS, T, HEAD_DIM).transpose(0, 2, 1, 3).reshape(n_s, ATT_WIDTH).astype(BF16)
    us3 = us.reshape(DB, T, POOL_WIDTH)
    prev_s = jnp.concatenate([jnp.zeros((DB, PREV_ROWS - POOL_STATE, POOL_WIDTH), F32), state_pool[0]], axis=1)
    pool_s = _pool(prev_s, us3, wgrp, pscale, past, DB).reshape(n_s, POOL_WIDTH)
    h_s, _ = _merge(xs, attn_s, pool_s, wga, wgb, wao, wpo, wo, g1, b1, n_s, alpha)

    def tail(h, pe, tm_r, tm_m, tm_f):
        comb = _router(h, wr_t, rbias, tm_r)
        y = _moe(h, comb, ws13, ws2, w13, w2, tm_m)
        return _final(h, y, pe, g2, b2, wpg, wpi, tm_f, alpha)

    y_s = tail(h_s, p_sample[0].reshape(n_s, PLE_DIM), n_s, n_s, n_s)

    pos, gate, blk, used = _dispatch(h_p, wr_t, rbias, 1024)
    sorted_in = _scatter_rows(hp_p, pos, _sorted_rows(n_p))
    anchor = lax.bitcast_convert_type(o_halves[0][0, 0, 0:1], I32)
    sorted_out = _grouped(blk.reshape(-1), used.reshape(-1), anchor, sorted_in, w13, w2)
    gathered = _gather_rows(sorted_out, pos)
    y_p = _combine(h_p, gathered, gate, p_prompt[0].reshape(n_p, PLE_DIM), ws13, ws2, g2, b2, wpg, wpi, 512, alpha)

    ext_s = jnp.concatenate([state_pool[0], us3], axis=1)
    return (y_p.reshape(B, S, D), y_s.reshape(DB, T, D),
            jnp.transpose(kt, (0, 2, 1))[None], jnp.transpose(vt, (0, 2, 1))[None],
            jnp.transpose(kit, (0, 2, 1))[None],
            u3[:, S - POOL_STATE:][None],
            ks.reshape(1, DB, T, HEAD_DIM), vs.reshape(1, DB, T, HEAD_DIM), kis.reshape(1, DB, T, IDX_DIM),
            ext_s[:, T:][None])
```

```python
import functools

import numpy as np
import jax
import jax.numpy as jnp
from jax import lax
from jax.experimental import pallas as pl
from jax.experimental.pallas import tpu as pltpu
from jax.experimental.pallas import tpu_sc as plsc

F32 = jnp.float32
BF16 = jnp.bfloat16
I32 = jnp.int32

D_MODEL = 1024
N_HEADS = 8
HEAD_DIM = 64
ATT_WIDTH = N_HEADS * HEAD_DIM
IDX_HEADS = 4
IDX_DIM = 64
TOP_K_MAX = 256
Q_BLOCK = 256
ROPE_THETA = 10000.0
POOL_WINDOWS = (2, 4, 8, 16)
POOL_GROUPS = 4
POOL_WIDTH = 512
POOL_GW = POOL_WIDTH // POOL_GROUPS
POOL_STATE = 15
N_EXPERTS = 64
TOP_K_EXPERTS = 8
N_GROUPS = 8
GROUP_SIZE = N_EXPERTS // N_GROUPS
TOPK_GROUPS = 4
EXPERT_DIM = 256
SHARED_DIM = 256
ROUTED_SCALE = 2.5
PLE_DIM = 256
LN_EPS = 1e-5
IN_SIZES = (ATT_WIDTH, HEAD_DIM, HEAD_DIM, IDX_HEADS * IDX_DIM, IDX_DIM, IDX_HEADS, POOL_WIDTH, D_MODEL, D_MODEL)

LANES = 128
SUBLANES = 8
INT_MIN = -2147483648
NEG_BIG = -1e30
VMEM_LIMIT = 56 * 1024 * 1024
PROJ_TILE = 512
MERGE_TILE = 1024
DISPATCH_TILE = 1024
COMBINE_TILE = 512

C_Q = 0
C_QI = 512
C_KK = 768
C_VW = 896
C_U = 1024
C_END = 1536
HALF = HEAD_DIM // 2

NT_DIMS = (((1,), (1,)), ((), ()))

Q_SCALE = HEAD_DIM ** -0.5 * float(np.log2(np.e))
QI_SCALE = IDX_DIM ** -0.5


def _params(sem):
    return pltpu.CompilerParams(dimension_semantics=sem, vmem_limit_bytes=VMEM_LIMIT)


def _layer_norm(x, g, b):
    mu = jnp.mean(x, axis=-1, keepdims=True)
    xc = x - mu
    var = jnp.mean(xc * xc, axis=-1, keepdims=True)
    return xc * lax.rsqrt(var + LN_EPS) * g + b


def _rope_rows(a, cos, sin):
    first_half = lax.broadcasted_iota(I32, (a.shape[0], LANES), 1) % HEAD_DIM < HALF
    out = []
    for s in range(a.shape[1] // LANES):
        x = a[:, s * LANES:(s + 1) * LANES]
        rot = jnp.where(first_half, -pltpu.roll(x, LANES - HALF, axis=1), pltpu.roll(x, HALF, axis=1))
        out.append(x * cos + rot * sin)
    return out[0] if len(out) == 1 else jnp.concatenate(out, axis=1)


def _proj_sample_kernel(x_ref, w_ref, cs_ref, q_ref, qi_ref, k_ref, v_ref, ki_ref, wi_ref, u_ref):
    xb = x_ref[...].astype(BF16)
    cos = cs_ref[:, 0:LANES]
    sin = cs_ref[:, LANES:2 * LANES]

    def mm(c0, n):
        return jnp.dot(xb, w_ref[:, c0:c0 + n], preferred_element_type=F32)

    def rope(c0, n):
        return _rope_rows(mm(c0, n), cos, sin)

    q_ref[...] = (rope(C_Q, ATT_WIDTH) * Q_SCALE).astype(BF16)
    qi_ref[...] = (rope(C_QI, IDX_HEADS * IDX_DIM) * QI_SCALE).astype(BF16)
    kk = rope(C_KK, LANES)
    k_ref[...] = kk[:, 0:HEAD_DIM]
    ki_ref[...] = kk[:, HEAD_DIM:2 * HEAD_DIM]
    vw = mm(C_VW, LANES)
    v_ref[...] = vw[:, 0:HEAD_DIM]
    wi_ref[...] = vw[:, HEAD_DIM:HEAD_DIM + IDX_HEADS] * (IDX_HEADS ** -0.5)
    u_ref[...] = mm(C_U, POOL_WIDTH)


def _proj_sample(x, w_big, cs):
    n = x.shape[0]
    full = lambda r, c: pl.BlockSpec((r, c), lambda i: (0, 0))
    widths = (ATT_WIDTH, IDX_HEADS * IDX_DIM, HEAD_DIM, HEAD_DIM, IDX_DIM, IDX_HEADS, POOL_WIDTH)
    dtypes = (BF16, BF16, F32, F32, F32, F32, F32)
    return pl.pallas_call(
        _proj_sample_kernel,
        grid=(1,),
        in_specs=[full(n, D_MODEL), full(D_MODEL, C_END), full(n, 2 * LANES)],
        out_specs=tuple(full(n, w) for w in widths),
        out_shape=tuple(jax.ShapeDtypeStruct((n, w), dt) for w, dt in zip(widths, dtypes)),
        compiler_params=_params(("arbitrary",)),
        name="proj_sample",
    )(x, w_big, cs)


def _proj_prompt_kernel(x_ref, w_ref, wt_ref, cs_ref, cst_ref, qt_ref, qit_ref, wit_ref, kb_ref, kib_ref, vbt_ref,
                        kt_ref, vt_ref, kit_ref, u_ref):
    xb = x_ref[...].astype(BF16)
    tm = xb.shape[0]
    cos = cs_ref[:, 0:LANES]
    sin = cs_ref[:, LANES:2 * LANES]
    cos_t = cst_ref[0:HEAD_DIM, :]
    sin_t = cst_ref[LANES:LANES + HEAD_DIM, :]

    def mm(c0, n):
        return jnp.dot(xb, w_ref[:, c0:c0 + n], preferred_element_type=F32)

    def mm_t(c0, n):
        return lax.dot_general(wt_ref[c0:c0 + n, :], xb, NT_DIMS, preferred_element_type=F32)

    def rope_t(c0, heads):
        a = mm_t(c0, heads * HEAD_DIM)
        parts = []
        for h in range(heads):
            x1 = a[h * HEAD_DIM:h * HEAD_DIM + HALF, :]
            x2 = a[h * HEAD_DIM + HALF:(h + 1) * HEAD_DIM, :]
            rot = jnp.concatenate([-x2, x1], axis=0)
            parts.append(a[h * HEAD_DIM:(h + 1) * HEAD_DIM, :] * cos_t + rot * sin_t)
        return parts[0] if heads == 1 else jnp.concatenate(parts, axis=0)

    kk = _rope_rows(mm(C_KK, LANES), cos, sin)
    kb_ref[...] = kk[:, 0:HEAD_DIM].astype(BF16)
    kib_ref[...] = kk[:, HEAD_DIM:2 * HEAD_DIM].astype(BF16)
    u_ref[...] = mm(C_U, POOL_WIDTH)

    qt = (rope_t(C_Q, N_HEADS) * Q_SCALE).astype(BF16)
    qit = (rope_t(C_QI, IDX_HEADS) * QI_SCALE).astype(BF16)
    for blk in range(tm // Q_BLOCK):
        cols = slice(blk * Q_BLOCK, (blk + 1) * Q_BLOCK)
        for h in range(N_HEADS):
            qt_ref[blk, :, h * Q_BLOCK:(h + 1) * Q_BLOCK] = qt[h * HEAD_DIM:(h + 1) * HEAD_DIM, cols]
        for h in range(IDX_HEADS):
            qit_ref[blk, :, h * Q_BLOCK:(h + 1) * Q_BLOCK] = qit[h * IDX_DIM:(h + 1) * IDX_DIM, cols]

    kkt = rope_t(C_KK, 2)
    kt_ref[...] = kkt[0:HEAD_DIM, :]
    kit_ref[...] = kkt[HEAD_DIM:2 * HEAD_DIM, :]
    vwt = mm_t(C_VW, LANES)
    vt_ref[...] = vwt[0:HEAD_DIM, :]
    vbt_ref[...] = vwt[0:HEAD_DIM, :].astype(BF16)
    wit_ref[...] = vwt[HEAD_DIM:HEAD_DIM + SUBLANES, :] * (IDX_HEADS ** -0.5)


def _proj_prompt(x, w_big, w_t, cs, seq, tm):
    n = x.shape[0]
    nb = seq // tm
    qb = tm // Q_BLOCK
    row = lambda w: pl.BlockSpec((tm, w), lambda i: (i, 0))
    col = lambda r: pl.BlockSpec((None, r, tm), lambda i: (i // nb, 0, i % nb))
    slab = lambda heads: pl.BlockSpec((qb, HEAD_DIM, heads * Q_BLOCK), lambda i: (i, 0, 0))
    pm = lambda r, dt: jax.ShapeDtypeStruct((n // seq, r, seq), dt)
    out_shape = (
        jax.ShapeDtypeStruct((n // Q_BLOCK, HEAD_DIM, N_HEADS * Q_BLOCK), BF16),
        jax.ShapeDtypeStruct((n // Q_BLOCK, IDX_DIM, IDX_HEADS * Q_BLOCK), BF16),
        pm(SUBLANES, F32),
        jax.ShapeDtypeStruct((n, HEAD_DIM), BF16), jax.ShapeDtypeStruct((n, IDX_DIM), BF16),
        pm(HEAD_DIM, BF16),
        pm(HEAD_DIM, F32), pm(HEAD_DIM, F32), pm(IDX_DIM, F32),
        jax.ShapeDtypeStruct((n, POOL_WIDTH), F32),
    )
    return pl.pallas_call(
        _proj_prompt_kernel,
        grid=(n // tm,),
        in_specs=[
            row(D_MODEL),
            pl.BlockSpec((D_MODEL, C_END), lambda i: (0, 0)),
            pl.BlockSpec((C_U, D_MODEL), lambda i: (0, 0)),
            pl.BlockSpec((tm, 2 * LANES), lambda i: (i % nb, 0)),
            pl.BlockSpec((2 * LANES, tm), lambda i: (0, i % nb)),
        ],
        out_specs=(slab(N_HEADS), slab(IDX_HEADS), col(SUBLANES), row(HEAD_DIM), row(IDX_DIM), col(HEAD_DIM),
                   col(HEAD_DIM), col(HEAD_DIM), col(IDX_DIM), row(POOL_WIDTH)),
        out_shape=out_shape,
        compiler_params=_params(("parallel",)),
        name="proj_prompt",
    )(x, w_big, w_t, cs, cs.T)


def _float_of_rank(u):
    key = u ^ INT_MIN
    bits = jnp.where(key < 0, INT_MIN - key, key)
    return pltpu.bitcast(bits, F32)


def _count(mask):
    return jnp.sum(mask.astype(F32), axis=1, keepdims=True)


def _topk_bias(sc_ref, j_ref, adm, n_adm, lc, k):
    rows = sc_ref.shape[0]
    kf = float(k)

    def value_step(i, t_u):
        hi = jnp.left_shift(jnp.int32(1), 31 - 2 * i)
        lo = jnp.left_shift(jnp.int32(1), 30 - 2 * i)
        for cand_u in (t_u | lo, t_u | hi, t_u | hi | lo):
            cnt = _count(sc_ref[:, 0:lc] >= _float_of_rank(cand_u))
            t_u = jnp.where(cnt >= kf, cand_u, t_u)
        return t_u

    t_u = lax.fori_loop(0, 16, value_step, jnp.zeros((rows, 1), I32))
    few = n_adm < k
    thr = jnp.where(few, -jnp.inf, _float_of_rank(t_u))
    sc = sc_ref[:, 0:lc]
    cnt_gt = _count(sc > thr)
    cnt_eq = _count(sc == thr)
    need = kf - cnt_gt
    cut_needed = jnp.logical_and(cnt_gt + cnt_eq > kf, jnp.logical_not(few))
    any_cut = jnp.max(cut_needed.astype(F32)) > 0.0
    idx = lax.broadcasted_iota(I32, (rows, lc), 1)
    nbits = int(np.ceil(np.log2(lc)))

    j_ref[...] = jnp.full((rows, 1), lc, I32)

    @pl.when(any_cut)
    def _():
        def index_step(i, j):
            cand = j | jnp.left_shift(jnp.int32(1), nbits - 1 - i)
            c = _count(jnp.logical_and(sc_ref[:, 0:lc] == thr, idx < cand))
            return jnp.where(c < need, cand, j)

        j_ref[...] = lax.fori_loop(0, nbits, index_step, jnp.zeros((rows, 1), I32))

    sel = jnp.logical_or(sc > thr, jnp.logical_and(sc == thr, idx <= j_ref[...]))
    return jnp.where(jnp.logical_and(sel, adm), 0.0, NEG_BIG)


ATTN_CHUNK = 256


def _attn_prompt_block(n_chunks, q0, top_k, qt_ref, qit_ref, wit_ref, kb_ref, kib_ref, vbt_ref, o_ref,
                       key_ref, bias_ref, lg_ref, j_ref):
    tq, ch = Q_BLOCK, ATTN_CHUNK
    kf = float(top_k)
    kpos = lax.broadcasted_iota(I32, (ch, tq), 0)
    qpos = q0 + lax.broadcasted_iota(I32, (ch, tq), 1)

    def rows(c):
        return slice(c * ch, (c + 1) * ch)

    def fold(x, op):
        return op(x.reshape(ch // SUBLANES, SUBLANES, tq), axis=0)

    def head(x, h):
        return x[:, h * tq:(h + 1) * tq]

    qit = qit_ref[...]
    wit = wit_ref[...]
    for c in range(n_chunks if n_chunks * ch > top_k else 0):
        d = jnp.dot(kib_ref[rows(c), :], qit, preferred_element_type=F32)
        s = wit[0:1, :] * jnp.maximum(head(d, 0), 0.0)
        for h in range(1, IDX_HEADS):
            s = s + wit[h:h + 1, :] * jnp.maximum(head(d, h), 0.0)
        key_ref[rows(c), :] = jnp.where(c * ch + kpos <= qpos, s, -jnp.inf)

    def count(pred):
        acc = jnp.zeros((SUBLANES, tq), F32)
        for c in range(n_chunks):
            acc = acc + fold(pred(key_ref[rows(c), :], c).astype(F32), jnp.sum)
        return jnp.sum(acc, axis=0, keepdims=True)

    if n_chunks * ch <= top_k:
        for c in range(n_chunks):
            bias_ref[rows(c), :] = jnp.where(c * ch + kpos <= qpos, 0.0, NEG_BIG)
    else:
        def value_step(i, carry):
            t_u, n_ge = carry
            cand_u = t_u | jnp.left_shift(jnp.int32(1), 31 - i)
            cand = _float_of_rank(cand_u)
            cnt = count(lambda k, c: k >= cand)
            ok = cnt >= kf
            return jnp.where(ok, cand_u, t_u), jnp.where(ok, cnt, n_ge)

        t_u, n_ge = lax.fori_loop(0, 32, value_step,
                                  (jnp.zeros((1, tq), I32), jnp.full((1, tq), float(n_chunks * ch), F32)))
        few = qpos[0:1, :] + 1 <= top_k
        thr = jnp.where(few, -jnp.inf, _float_of_rank(t_u))
        cut_needed = jnp.logical_and(n_ge > kf, jnp.logical_not(few))
        any_cut = jnp.max(cut_needed.astype(F32)) > 0.0

        nbits = int(np.ceil(np.log2(n_chunks * ch)))
        j_ref[...] = jnp.full(j_ref.shape, n_chunks * ch, I32)

        @pl.when(any_cut)
        def _():
            need = kf - count(lambda k, c: k > thr)

            def index_step(i, j):
                cand = j | jnp.left_shift(jnp.int32(1), nbits - 1 - i)
                n_before = count(lambda k, c: jnp.logical_and(k == thr, c * ch + kpos < cand))
                return jnp.where(n_before < need, cand, j)

            j = lax.fori_loop(0, nbits, index_step, jnp.zeros((1, tq), I32))
            j_ref[...] = jnp.broadcast_to(j, j_ref.shape)

        j_cut = j_ref[0:1, :]
        for c in range(n_chunks):
            k = key_ref[rows(c), :]
            pos = c * ch + kpos
            sel = jnp.logical_or(k > thr, jnp.logical_and(k == thr, pos <= j_cut))
            bias_ref[rows(c), :] = jnp.where(jnp.logical_and(sel, pos <= qpos), 0.0, NEG_BIG)

    qt = qt_ref[...]
    mx = [jnp.full((SUBLANES, tq), -jnp.inf, F32) for _ in range(N_HEADS)]
    for c in range(n_chunks):
        lg = jnp.dot(kb_ref[rows(c), :], qt, preferred_element_type=F32)
        bias = bias_ref[rows(c), :]
        for h in range(N_HEADS):
            lgh = head(lg, h) + bias
            lg_ref[h, rows(c), :] = lgh
            mx[h] = jnp.maximum(mx[h], fold(lgh, jnp.max))

    outs = []
    for h in range(N_HEADS):
        m = jnp.max(mx[h], axis=0, keepdims=True)
        lsum = jnp.zeros((SUBLANES, tq), F32)
        ot = jnp.zeros((HEAD_DIM, tq), F32)
        for c in range(n_chunks):
            p = jnp.exp2(lg_ref[h, rows(c), :] - m)
            lsum = lsum + fold(p, jnp.sum)
            ot = ot + jnp.dot(vbt_ref[:, rows(c)], p.astype(BF16), preferred_element_type=F32)
        outs.append(ot / jnp.sum(lsum, axis=0, keepdims=True))
    o_ref[...] = jnp.concatenate(outs, axis=0).T.astype(BF16)


def _attn_prompt_kernel(qt_ref, qit_ref, wit_ref, kb_ref, kib_ref, vbt_ref, o_ref, key_ref, bias_ref, lg_ref, j_ref,
                        *, top_k):
    jq = pl.program_id(1)
    blocks_per_chunk = ATTN_CHUNK // Q_BLOCK
    n_classes = key_ref.shape[0] // ATTN_CHUNK
    for cls in range(n_classes):
        @pl.when(jq // blocks_per_chunk == cls)
        def _(cls=cls):
            _attn_prompt_block(cls + 1, jq * Q_BLOCK, top_k, qt_ref, qit_ref, wit_ref, kb_ref, kib_ref, vbt_ref,
                               o_ref, key_ref, bias_ref, lg_ref, j_ref)


def _attn_prompt(qt, qit, wit, kb, kib, vbt):
    batch, _, seq = vbt.shape
    nb = seq // Q_BLOCK
    top_k = min(TOP_K_MAX, seq // 4)
    slab = lambda heads: pl.BlockSpec((None, HEAD_DIM, heads * Q_BLOCK), lambda b, j: (b * nb + j, 0, 0))
    keys = pl.BlockSpec((seq, HEAD_DIM), lambda b, j: (b, 0))
    return pl.pallas_call(
        functools.partial(_attn_prompt_kernel, top_k=top_k),
        grid=(batch, nb),
        in_specs=[slab(N_HEADS), slab(IDX_HEADS), pl.BlockSpec((None, SUBLANES, Q_BLOCK), lambda b, j: (b, 0, j)),
                  keys, keys, pl.BlockSpec((None, HEAD_DIM, seq), lambda b, j: (b, 0, 0))],
        out_specs=pl.BlockSpec((Q_BLOCK, ATT_WIDTH), lambda b, j: (b * nb + j, 0)),
        out_shape=jax.ShapeDtypeStruct((batch * seq, ATT_WIDTH), BF16),
        scratch_shapes=[pltpu.VMEM((seq, Q_BLOCK), F32), pltpu.VMEM((seq, Q_BLOCK), F32),
                        pltpu.VMEM((N_HEADS, seq, Q_BLOCK), F32), pltpu.VMEM((SUBLANES, Q_BLOCK), I32)],
        compiler_params=_params(("parallel", "arbitrary")),
        name="attn_prompt",
    )(qt, qit, wit, kb, kib, vbt)


SAMPLE_CHUNK = 1024


def _attn_sample_kernel(pt_ref, q_ref, qi_ref, wi_ref, kn_ref, vn_ref, kin_ref, ck_hbm, cv_hbm, cki_hbm, o_ref,
                        kbuf, vbuf, kibuf, sem, key_scr, bias_scr, lg_scr, j_scr, *, n_pages, page, t_new, top_k):
    b = pl.program_id(0)
    n_b = pl.num_programs(0)
    slot = b % 2
    past = n_pages * page
    lc = past + page
    n_chunks = past // SAMPLE_CHUNK

    def page_copies(bb, sl, p):
        phys = pt_ref[bb * n_pages + p]
        dst = pl.ds(pl.multiple_of(p * page, page), page)
        return [pltpu.make_async_copy(src.at[phys], buf.at[sl, :, dst], sem.at[i, sl])
                for i, (src, buf) in enumerate(((ck_hbm, kbuf), (cv_hbm, vbuf), (cki_hbm, kibuf)))]

    def start_batch(bb, sl):
        def body(p, carry):
            for cp in page_copies(bb, sl, p):
                cp.start()
            return carry
        lax.fori_loop(0, n_pages, body, 0)

    def wait_batch(bb, sl):
        def body(p, carry):
            for cp in page_copies(bb, sl, p):
                cp.wait()
            return carry
        lax.fori_loop(0, n_pages, body, 0)

    @pl.when(b == 0)
    def _():
        start_batch(0, 0)

    @pl.when(b + 1 < n_b)
    def _():
        start_batch(b + 1, 1 - slot)

    wait_batch(b, slot)

    def head_sum(d):
        r = wi_ref[...] * jnp.maximum(d, 0.0)
        s = r[0:t_new]
        for h in range(1, IDX_HEADS):
            s = s + r[h * t_new:(h + 1) * t_new]
        return s

    def new_rows(ref):
        pad = jnp.zeros((page - t_new, ref.shape[1]), F32)
        return jnp.concatenate([ref[...], pad], axis=0).astype(BF16)

    qi = qi_ref[...]
    for c in range(n_chunks):
        sl = slice(c * SAMPLE_CHUNK, (c + 1) * SAMPLE_CHUNK)
        d = jnp.dot(qi, kibuf[slot, :, sl].astype(BF16), preferred_element_type=F32)
        key_scr[:, sl] = head_sum(d)
    d_new = lax.dot_general(qi, new_rows(kin_ref), NT_DIMS, preferred_element_type=F32)
    adm_new = lax.broadcasted_iota(I32, (t_new, page), 1) <= lax.broadcasted_iota(I32, (t_new, page), 0)
    key_scr[:, past:lc] = jnp.where(adm_new, head_sum(d_new), -jnp.inf)

    idx = lax.broadcasted_iota(I32, (t_new, lc), 1)
    trow = lax.broadcasted_iota(I32, (t_new, lc), 0)
    n_adm = past + 1 + lax.broadcasted_iota(I32, (t_new, 1), 0)
    bias_scr[...] = _topk_bias(key_scr, j_scr, idx - past <= trow, n_adm, lc, top_k)

    q = q_ref[...]

    def bias_rows(sl):
        return jnp.concatenate([bias_scr[:, sl]] * N_HEADS, axis=0)

    m = jnp.full((N_HEADS * t_new, 1), -jnp.inf, F32)
    for c in range(n_chunks):
        sl = slice(c * SAMPLE_CHUNK, (c + 1) * SAMPLE_CHUNK)
        lg = jnp.dot(q, kbuf[slot, :, sl].astype(BF16), preferred_element_type=F32) + bias_rows(sl)
        lg_scr[:, sl] = lg
        m = jnp.maximum(m, jnp.max(lg, axis=1, keepdims=True))
    lg_new = lax.dot_general(q, new_rows(kn_ref), NT_DIMS, preferred_element_type=F32) + bias_rows(slice(past, lc))
    m = jnp.maximum(m, jnp.max(lg_new, axis=1, keepdims=True))

    p_new = jnp.exp2(lg_new - m)
    l = jnp.sum(p_new, axis=1, keepdims=True)
    o = jnp.dot(p_new.astype(BF16), new_rows(vn_ref), preferred_element_type=F32)
    for c in range(n_chunks):
        sl = slice(c * SAMPLE_CHUNK, (c + 1) * SAMPLE_CHUNK)
        pr = jnp.exp2(lg_scr[:, sl] - m)
        l = l + jnp.sum(pr, axis=1, keepdims=True)
        o = o + lax.dot_general(pr.astype(BF16), vbuf[slot, :, sl].astype(BF16), NT_DIMS,
                                preferred_element_type=F32)
    o_ref[...] = o / l


def _attn_sample(page_table, q_hq, qi_hq, wi_hq, k_new, v_new, ki_new, cache_kt, cache_vt, cache_kit):
    db, n_pages = page_table.shape
    page = cache_kt.shape[2]
    t_new = k_new.shape[1]
    past = n_pages * page
    lc = past + page
    top_k = min(TOP_K_MAX, (past + t_new) // 4)
    per_b = lambda r, w: pl.BlockSpec((None, r, w), lambda b, pt: (b, 0, 0))
    hbm = pl.BlockSpec(memory_space=pl.ANY)
    kern = functools.partial(_attn_sample_kernel, n_pages=n_pages, page=page, t_new=t_new, top_k=top_k)
    slab = pltpu.VMEM((2, HEAD_DIM, past), F32)
    grid_spec = pltpu.PrefetchScalarGridSpec(
        num_scalar_prefetch=1,
        grid=(db,),
        in_specs=[per_b(N_HEADS * t_new, HEAD_DIM), per_b(IDX_HEADS * t_new, IDX_DIM), per_b(IDX_HEADS * t_new, 1),
                  per_b(t_new, HEAD_DIM), per_b(t_new, HEAD_DIM), per_b(t_new, IDX_DIM),
                  hbm, hbm, hbm],
        out_specs=per_b(N_HEADS * t_new, HEAD_DIM),
        scratch_shapes=[slab, slab, slab, pltpu.SemaphoreType.DMA((3, 2)),
                        pltpu.VMEM((t_new, lc), F32), pltpu.VMEM((t_new, lc), F32),
                        pltpu.VMEM((N_HEADS * t_new, past), F32), pltpu.VMEM((t_new, 1), I32)],
    )
    return pl.pallas_call(
        kern,
        grid_spec=grid_spec,
        out_shape=jax.ShapeDtypeStruct((db, N_HEADS * t_new, HEAD_DIM), F32),
        compiler_params=_params(("arbitrary",)),
        name="attn_sample",
    )(page_table.reshape(-1), q_hq, qi_hq, wi_hq, k_new, v_new, ki_new, cache_kt, cache_vt, cache_kit)


PREV_ROWS = 16


def _pool_kernel(prev_ref, u_ref, wg_ref, sc_ref, o_ref, ext_ref, *, pos0):
    per_step, t_len, _ = u_ref.shape
    pos = pos0 + lax.broadcasted_iota(I32, (t_len, 1), 0)
    for b in range(per_step):
        ext_ref[0:PREV_ROWS, :] = prev_ref[b]
        ext_ref[PREV_ROWS:PREV_ROWS + t_len, :] = u_ref[b]
        for g, w in enumerate(POOL_WINDOWS):
            sl = slice(g * POOL_GW, (g + 1) * POOL_GW)
            u_new = ext_ref[PREV_ROWS:PREV_ROWS + t_len, sl]
            win = u_new
            for back in range(1, w):
                win = win + ext_ref[PREV_ROWS - back:PREV_ROWS - back + t_len, sl]
            count = jnp.minimum(pos + 1, w).astype(F32)
            r = win / count - u_new
            mixed = jnp.dot(r.astype(BF16), wg_ref[g], preferred_element_type=F32) * sc_ref[:, sl]
            o_ref[b, :, sl] = mixed.astype(BF16)


def _pool(prev, u, w_grp, scale, pos0, per_step):
    nb, t_len, _ = u.shape
    seqs = lambda rows: pl.BlockSpec((per_step, rows, POOL_WIDTH), lambda b: (b, 0, 0))
    return pl.pallas_call(
        functools.partial(_pool_kernel, pos0=pos0),
        grid=(nb // per_step,),
        in_specs=[seqs(PREV_ROWS), seqs(t_len),
                  pl.BlockSpec((POOL_GROUPS, POOL_GW, POOL_GW), lambda b: (0, 0, 0)),
                  pl.BlockSpec((1, POOL_WIDTH), lambda b: (0, 0))],
        out_specs=seqs(t_len),
        out_shape=jax.ShapeDtypeStruct((nb, t_len, POOL_WIDTH), BF16),
        scratch_shapes=[pltpu.VMEM((PREV_ROWS + t_len, POOL_WIDTH), F32)],
        compiler_params=_params(("parallel",)),
        name="pool",
    )(prev, u, w_grp, scale)


def _merge_kernel(x_ref, a_ref, p_ref, wga_ref, wgb_ref, wao_ref, wpo_ref, wo_ref, g_ref, b_ref, h_ref, hp_ref, *,
                  alpha):
    x = x_ref[...]
    xb = x.astype(BF16)
    ga = jnp.dot(xb, wga_ref[...], preferred_element_type=F32)
    gb = jnp.dot(xb, wgb_ref[...], preferred_element_type=F32)
    ya = jnp.dot(a_ref[...], wao_ref[...], preferred_element_type=F32)
    yp = jnp.dot(p_ref[...], wpo_ref[...], preferred_element_type=F32)
    mix = jax.nn.sigmoid(ga) * ya + jax.nn.sigmoid(gb) * yp
    out = jnp.dot(mix.astype(BF16), wo_ref[...], preferred_element_type=F32)
    h = _layer_norm(alpha * x + out, g_ref[...], b_ref[...])
    h_ref[...] = h
    hp_ref[...] = _pack_rows(h)


def _merge(x, attn, pool, wga, wgb, wao, wpo, wo, g, b, tm, alpha):
    n = x.shape[0]
    row = lambda w: pl.BlockSpec((tm, w), lambda i: (i, 0))
    full = lambda r, c: pl.BlockSpec((r, c), lambda i: (0, 0), pipeline_mode=pl.Buffered(1))
    return pl.pallas_call(
        functools.partial(_merge_kernel, alpha=alpha),
        grid=(n // tm,),
        in_specs=[row(D_MODEL), row(ATT_WIDTH), row(POOL_WIDTH), full(D_MODEL, D_MODEL), full(D_MODEL, D_MODEL),
                  full(ATT_WIDTH, D_MODEL), full(POOL_WIDTH, D_MODEL), full(D_MODEL, D_MODEL),
                  full(1, D_MODEL), full(1, D_MODEL)],
        out_specs=(row(D_MODEL), row(PACKED)),
        out_shape=(jax.ShapeDtypeStruct((n, D_MODEL), F32), jax.ShapeDtypeStruct((n, PACKED), I32)),
        compiler_params=_params(("parallel",)),
        name="merge",
    )(x, attn, pool, wga, wgb, wao, wpo, wo, g, b)


def _route(h, wr_t, bias_col):
    tm = h.shape[0]
    logits = lax.dot_general(wr_t, h.astype(BF16), NT_DIMS, preferred_element_type=F32)
    s = jax.nn.sigmoid(logits)
    sb = s + bias_col
    neg_inf = -jnp.inf

    rows = []
    for g in range(N_GROUPS):
        blk = sb[g * GROUP_SIZE:(g + 1) * GROUP_SIZE, :]
        m1 = jnp.max(blk, axis=0, keepdims=True)
        is_m1 = blk == m1
        n_m1 = jnp.sum(is_m1.astype(F32), axis=0, keepdims=True)
        m2 = jnp.max(jnp.where(is_m1, neg_inf, blk), axis=0, keepdims=True)
        rows.append(m1 + jnp.where(n_m1 >= 2.0, m1, m2))
    gs = jnp.concatenate(rows, axis=0)

    gi = lax.broadcasted_iota(I32, (N_GROUPS, tm), 0)
    rank = jnp.zeros((N_GROUPS, tm), F32)
    for g in range(N_GROUPS):
        row = gs[g:g + 1, :]
        beats = jnp.logical_or(row > gs, jnp.logical_and(row == gs, g < gi))
        rank = rank + beats.astype(F32)
    gkeep = rank < float(TOPK_GROUPS)
    emask = jnp.concatenate(
        [jnp.broadcast_to(gkeep[g:g + 1, :], (GROUP_SIZE, tm)) for g in range(N_GROUPS)], axis=0)

    ei = lax.broadcasted_iota(I32, (N_EXPERTS, tm), 0)
    x = jnp.where(emask, sb, neg_inf)
    sel = jnp.zeros((N_EXPERTS, tm), jnp.bool_)
    picks = []
    for _ in range(TOP_K_EXPERTS):
        m = jnp.max(x, axis=0, keepdims=True)
        first = jnp.min(jnp.where(x == m, ei, N_EXPERTS), axis=0, keepdims=True)
        pick = ei == first
        sel = jnp.logical_or(sel, pick)
        x = jnp.where(pick, neg_inf, x)
        picks.append(first)

    gate = jnp.where(sel, s, 0.0)
    comb = gate / jnp.sum(gate, axis=0, keepdims=True) * ROUTED_SCALE
    return comb, sel, picks


def _router_kernel(h_ref, wr_ref, bias_ref, c_ref):
    comb, _, _ = _route(h_ref[...], wr_ref[...], bias_ref[...])
    comb = jnp.concatenate([comb, jnp.zeros((LANES - N_EXPERTS, comb.shape[1]), F32)], axis=0)
    c_ref[...] = comb.T


def _router(h, wr_t, bias_col, tm):
    n = h.shape[0]
    return pl.pallas_call(
        _router_kernel,
        grid=(n // tm,),
        in_specs=[pl.BlockSpec((tm, D_MODEL), lambda i: (i, 0)),
                  pl.BlockSpec((N_EXPERTS, D_MODEL), lambda i: (0, 0)),
                  pl.BlockSpec((N_EXPERTS, 1), lambda i: (0, 0))],
        out_specs=pl.BlockSpec((tm, LANES), lambda i: (i, 0)),
        out_shape=jax.ShapeDtypeStruct((n, LANES), F32),
        compiler_params=_params(("parallel",)),
        name="router",
    )(h, wr_t, bias_col)


def _swiglu(xb, w13, w2, hidden):
    ab = jnp.dot(xb, w13, preferred_element_type=F32)
    act = jax.nn.silu(ab[:, 0:hidden]) * ab[:, hidden:2 * hidden]
    return jnp.dot(act.astype(BF16), w2, preferred_element_type=F32)


def _moe_kernel(h_ref, c_ref, ws13_ref, ws2_ref, w13_ref, w2_ref, y_ref, hb_ref):
    e = pl.program_id(1)

    @pl.when(e == 0)
    def _():
        hb_ref[...] = h_ref[...].astype(BF16)
        y_ref[...] = _swiglu(hb_ref[...], ws13_ref[...], ws2_ref[...], SHARED_DIM)

    ye = _swiglu(hb_ref[...], w13_ref[...].astype(BF16), w2_ref[...].astype(BF16), EXPERT_DIM)
    lane = lax.broadcasted_iota(I32, c_ref.shape, 1)
    ce = jnp.sum(jnp.where(lane == e, c_ref[...], 0.0), axis=1, keepdims=True)
    y_ref[...] += ce * ye


def _moe(h, comb, ws13, ws2, w13, w2, tm):
    n = h.shape[0]
    return pl.pallas_call(
        _moe_kernel,
        grid=(n // tm, N_EXPERTS),
        in_specs=[pl.BlockSpec((tm, D_MODEL), lambda i, e: (i, 0)),
                  pl.BlockSpec((tm, LANES), lambda i, e: (i, 0)),
                  pl.BlockSpec((D_MODEL, 2 * SHARED_DIM), lambda i, e: (0, 0)),
                  pl.BlockSpec((SHARED_DIM, D_MODEL), lambda i, e: (0, 0)),
                  pl.BlockSpec((None, D_MODEL, 2 * EXPERT_DIM), lambda i, e: (e, 0, 0)),
                  pl.BlockSpec((None, EXPERT_DIM, D_MODEL), lambda i, e: (e, 0, 0))],
        out_specs=pl.BlockSpec((tm, D_MODEL), lambda i, e: (i, 0)),
        out_shape=jax.ShapeDtypeStruct((n, D_MODEL), F32),
        scratch_shapes=[pltpu.VMEM((tm, D_MODEL), BF16)],
        compiler_params=_params(("parallel", "arbitrary")),
        name="moe",
    )(h, comb, ws13, ws2, w13, w2)


def _final_kernel(h_ref, y_ref, pe_ref, g_ref, b_ref, wpg_ref, wpi_ref, o_ref, *, alpha):
    z = _layer_norm(alpha * h_ref[...] + y_ref[...], g_ref[...], b_ref[...])
    gate = jax.nn.sigmoid(jnp.dot(z.astype(BF16), wpg_ref[...], preferred_element_type=F32))
    emb = jnp.dot(pe_ref[...].astype(BF16), wpi_ref[...], preferred_element_type=F32)
    o_ref[...] = z + gate * emb


def _final(h, y, pe, g, b, wpg, wpi, tm, alpha):
    n = h.shape[0]
    row = lambda w: pl.BlockSpec((tm, w), lambda i: (i, 0))
    full = lambda r, c: pl.BlockSpec((r, c), lambda i: (0, 0))
    return pl.pallas_call(
        functools.partial(_final_kernel, alpha=alpha),
        grid=(n // tm,),
        in_specs=[row(D_MODEL), row(D_MODEL), row(PLE_DIM), full(1, D_MODEL), full(1, D_MODEL),
                  full(D_MODEL, D_MODEL), full(PLE_DIM, D_MODEL)],
        out_specs=row(D_MODEL),
        out_shape=jax.ShapeDtypeStruct((n, D_MODEL), F32),
        compiler_params=_params(("parallel",)),
        name="final",
    )(h, y, pe, g, b, wpg, wpi)


MOE_BLOCK = 2176


def _sorted_rows(n_tokens):
    worst = n_tokens * TOP_K_EXPERTS + N_EXPERTS * (MOE_BLOCK - 1)
    return -(-worst // MOE_BLOCK) * MOE_BLOCK


def _dispatch_kernel(h_ref, wr_ref, bias_ref, tri_ref, pos_ref, gate_ref, blk_ref, used_ref,
                     eidx_s, rank_s, gate_s, cnt_s):
    p = pl.program_id(0)
    i = pl.program_id(1)
    tm = h_ref.shape[0]
    ei = lax.broadcasted_iota(I32, (N_EXPERTS, tm), 0)

    @pl.when(p == 0)
    def _():
        comb, sel, picks = _route(h_ref[...], wr_ref[...], bias_ref[...])
        before = jnp.dot(sel.astype(BF16), tri_ref[...], preferred_element_type=F32)
        ranks, gates = [], []
        for first in picks:
            pick = ei == first
            ranks.append(jnp.sum(jnp.where(pick, before, 0.0), axis=0, keepdims=True))
            gates.append(jnp.sum(jnp.where(pick, comb, 0.0), axis=0, keepdims=True))
        eidx_s[i] = jnp.concatenate(picks, axis=0)
        rank_s[i] = jnp.concatenate(ranks, axis=0)
        gate_s[i] = jnp.concatenate(gates, axis=0)
        cnt_s[i] = jnp.broadcast_to(jnp.sum(sel.astype(F32), axis=1, keepdims=True), (N_EXPERTS, LANES))

    @pl.when(p == 1)
    def _():
        cnt = cnt_s[...]
        tile_id = lax.broadcasted_iota(I32, cnt.shape, 0)
        total = jnp.sum(cnt, axis=0)
        prior = jnp.sum(jnp.where(tile_id < i, cnt, 0.0), axis=0)
        seg_blk = jnp.ceil(total * (1.0 / MOE_BLOCK) - 0.25 / MOE_BLOCK)
        lower = (lax.broadcasted_iota(I32, (N_EXPERTS, N_EXPERTS), 1)
                 < lax.broadcasted_iota(I32, (N_EXPERTS, N_EXPERTS), 0)).astype(F32)
        off_blk = jnp.dot(lower, seg_blk, precision=lax.Precision.HIGHEST, preferred_element_type=F32)
        seg_off = off_blk * MOE_BLOCK
        base = (seg_off + prior)[:, 0:1]
        eidx = eidx_s[i]
        rank = rank_s[i]
        rows = []
        for k in range(TOP_K_EXPERTS):
            pick = ei == eidx[k:k + 1, :]
            rows.append(rank[k:k + 1, :] + jnp.sum(jnp.where(pick, base, 0.0), axis=0, keepdims=True))
        pos_ref[...] = jnp.concatenate(rows, axis=0).astype(I32)
        gate_ref[...] = jnp.concatenate([gate_s[i], jnp.zeros((LANES - TOP_K_EXPERTS, tm), F32)], axis=0).T

        end_blk = (off_blk + seg_blk)[:, 0:1]
        n_blk = blk_ref.shape[1]
        blk_id = lax.broadcasted_iota(I32, (N_EXPERTS, n_blk), 1).astype(F32)
        owner = jnp.sum((end_blk <= blk_id).astype(F32), axis=0, keepdims=True)
        blk_ref[...] = jnp.minimum(owner, N_EXPERTS - 1.0).astype(I32)
        used_ref[...] = jnp.broadcast_to(end_blk[N_EXPERTS - 1:N_EXPERTS, :], used_ref.shape).astype(I32)


def _dispatch(h, wr_t, bias_col, tm):
    n = h.shape[0]
    n_tiles = n // tm
    n_blk = _sorted_rows(n) // MOE_BLOCK
    n_blk_pad = -(-n_blk // LANES) * LANES
    tri = jnp.triu(jnp.ones((tm, tm), BF16), k=1)
    const = lambda r, c: pl.BlockSpec((r, c), lambda p, i: (0, 0))
    per_tile = lambda dt: pltpu.VMEM((n_tiles, TOP_K_EXPERTS, tm), dt)
    return pl.pallas_call(
        _dispatch_kernel,
        grid=(2, n_tiles),
        in_specs=[pl.BlockSpec((tm, D_MODEL), lambda p, i: (i * (1 - p), 0)),
                  const(N_EXPERTS, D_MODEL), const(N_EXPERTS, 1), const(tm, tm)],
        out_specs=(pl.BlockSpec((TOP_K_EXPERTS, tm), lambda p, i: (0, i * p)),
                   pl.BlockSpec((tm, LANES), lambda p, i: (i * p, 0)),
                   const(1, n_blk_pad), const(1, LANES)),
        out_shape=(jax.ShapeDtypeStruct((TOP_K_EXPERTS, n), I32), jax.ShapeDtypeStruct((n, LANES), F32),
                   jax.ShapeDtypeStruct((1, n_blk_pad), I32), jax.ShapeDtypeStruct((1, LANES), I32)),
        scratch_shapes=[per_tile(I32), per_tile(F32), per_tile(F32), pltpu.VMEM((n_tiles, N_EXPERTS, LANES), F32)],
        compiler_params=_params(("arbitrary", "arbitrary")),
        name="dispatch",
    )(h, wr_t, bias_col, tri)


PACKED = D_MODEL // 2


def _pack_rows(x):
    lo = pltpu.bitcast(x[:, 0:PACKED].astype(BF16).astype(F32), I32)
    hi = pltpu.bitcast(x[:, PACKED:D_MODEL].astype(BF16).astype(F32), I32)
    return jnp.bitwise_or(hi, lax.shift_right_logical(lo, 16))


def _unpack_rows_f32(w):
    lo = pltpu.bitcast(lax.shift_left(w, 16), F32)
    hi = pltpu.bitcast(jnp.bitwise_and(w, -65536), F32)
    return jnp.concatenate([lo, hi], axis=1)


def _unpack_rows(w):
    return _unpack_rows_f32(w).astype(BF16)


def _grouped_kernel(blk_ref, used_ref, anchor_ref, xs_ref, w13_ref, w2_ref, ys_ref):
    @pl.when(pl.program_id(0) < used_ref[0])
    def _():
        ys = _swiglu(_unpack_rows(xs_ref[...]), w13_ref[...].astype(BF16), w2_ref[...].astype(BF16), EXPERT_DIM)
        ys_ref[...] = _pack_rows(ys)


def _grouped(blk, used, anchor, xs, w13, w2):
    ns = xs.shape[0]
    row_blk = lambda b, blk, used, anchor: (jnp.minimum(b, used[0] - 1), 0)
    expert = lambda b, blk, used, anchor: (blk[b], 0, 0)
    grid_spec = pltpu.PrefetchScalarGridSpec(
        num_scalar_prefetch=3,
        grid=(ns // MOE_BLOCK,),
        in_specs=[pl.BlockSpec((MOE_BLOCK, PACKED), row_blk),
                  pl.BlockSpec((None, D_MODEL, 2 * EXPERT_DIM), expert),
                  pl.BlockSpec((None, EXPERT_DIM, D_MODEL), expert)],
        out_specs=pl.BlockSpec((MOE_BLOCK, PACKED), row_blk),
    )
    return pl.pallas_call(
        _grouped_kernel,
        grid_spec=grid_spec,
        out_shape=jax.ShapeDtypeStruct((ns, PACKED), I32),
        compiler_params=_params(("arbitrary",)),
        name="grouped",
    )(blk, used, anchor, xs, w13, w2)


SC_WINDOW = 128


def _sc_mesh():
    return plsc.VectorSubcoreMesh(core_axis_name="core", subcore_axis_name="subcore")


def _sc_worker(n_items):
    info = plsc.get_sparse_core_info()
    n_workers = info.num_cores * info.num_subcores
    wid = lax.axis_index("subcore") * info.num_cores + lax.axis_index("core")
    return wid, n_items // (SC_WINDOW * n_workers)


def _scatter_rows(x, pos, n_out):
    n, width = x.shape
    picks = pos.shape[0]

    @functools.partial(
        pl.kernel, mesh=_sc_mesh(), out_type=jax.ShapeDtypeStruct((n_out, width), I32),
        scratch_types=[pltpu.VMEM((picks, SC_WINDOW), I32), pltpu.VMEM((SC_WINDOW, width), I32)],
        name="scatter_rows")
    def scatter(x_hbm, pos_hbm, out_hbm, idx_v, rows_v):
        wid, n_win = _sc_worker(n)

        @pl.loop(0, n_win)
        def _(j):
            base = (wid * n_win + j) * SC_WINDOW
            pltpu.sync_copy(pos_hbm.at[:, pl.ds(base, SC_WINDOW)], idx_v)
            pltpu.sync_copy(x_hbm.at[pl.ds(base, SC_WINDOW)], rows_v)
            for k in range(picks):
                pltpu.sync_copy(rows_v, out_hbm.at[idx_v.at[k]])

    return scatter(x, pos)


def _gather_rows(src, pos):
    width = src.shape[1]
    picks, n = pos.shape

    @functools.partial(
        pl.kernel, mesh=_sc_mesh(), out_type=jax.ShapeDtypeStruct((picks * n, width), I32),
        scratch_types=[pltpu.VMEM((SC_WINDOW,), I32), pltpu.VMEM((SC_WINDOW, width), I32)],
        name="gather_rows")
    def gather(src_hbm, pos_hbm, out_hbm, idx_v, rows_v):
        wid, n_win = _sc_worker(picks * n)

        @pl.loop(0, n_win)
        def _(j):
            base = (wid * n_win + j) * SC_WINDOW
            pltpu.sync_copy(pos_hbm.at[pl.ds(base, SC_WINDOW)], idx_v)
            pltpu.sync_copy(src_hbm.at[idx_v], rows_v)
            pltpu.sync_copy(rows_v, out_hbm.at[pl.ds(base, SC_WINDOW)])

    return gather(src, pos.reshape(-1)).reshape(picks, n, width)


def _combine_kernel(h_ref, g_ref, gate_ref, pe_ref, ws13_ref, ws2_ref, ln_g_ref, ln_b_ref, wpg_ref, wpi_ref, o_ref, *,
                    alpha):
    h = h_ref[...]
    y = _swiglu(h.astype(BF16), ws13_ref[...], ws2_ref[...], SHARED_DIM)
    gate = gate_ref[...]
    for k in range(TOP_K_EXPERTS):
        y = y + gate[:, k:k + 1] * _unpack_rows_f32(g_ref[k])
    z = _layer_norm(alpha * h + y, ln_g_ref[...], ln_b_ref[...])
    ple_gate = jax.nn.sigmoid(jnp.dot(z.astype(BF16), wpg_ref[...], preferred_element_type=F32))
    emb = jnp.dot(pe_ref[...].astype(BF16), wpi_ref[...], preferred_element_type=F32)
    o_ref[...] = z + ple_gate * emb


def _combine(h, gathered, gate, pe, ws13, ws2, g, b, wpg, wpi, tm, alpha):
    n = h.shape[0]
    row = lambda w: pl.BlockSpec((tm, w), lambda i: (i, 0))
    full = lambda r, c: pl.BlockSpec((r, c), lambda i: (0, 0))
    return pl.pallas_call(
        functools.partial(_combine_kernel, alpha=alpha),
        grid=(n // tm,),
        in_specs=[row(D_MODEL), pl.BlockSpec((TOP_K_EXPERTS, tm, PACKED), lambda i: (0, i, 0)), row(LANES),
                  row(PLE_DIM), full(D_MODEL, 2 * SHARED_DIM), full(SHARED_DIM, D_MODEL),
                  full(1, D_MODEL), full(1, D_MODEL), full(D_MODEL, D_MODEL), full(PLE_DIM, D_MODEL)],
        out_specs=row(D_MODEL),
        out_shape=jax.ShapeDtypeStruct((n, D_MODEL), F32),
        compiler_params=_params(("parallel",)),
        name="combine",
    )(h, gathered, gate, pe, ws13, ws2, g, b, wpg, wpi)


def _rope_table(pos):
    inv = ROPE_THETA ** (-jnp.arange(0, HEAD_DIM, 2, dtype=F32) / HEAD_DIM)
    ang = pos.astype(F32)[:, None] * inv[None, :]
    return jnp.concatenate([jnp.tile(jnp.cos(ang), (1, 4)), jnp.tile(jnp.sin(ang), (1, 4))], axis=1)


def _fused_in_weight(w_in):
    offs = np.cumsum(IN_SIZES)[:-1].tolist()
    wq, wk, wv, wqi, wki, wwi, wu, wga, wgb = jnp.split(w_in, offs, axis=1)
    pad = jnp.zeros((D_MODEL, LANES - HEAD_DIM - IDX_HEADS), w_in.dtype)
    w_big = jnp.concatenate([wq, wqi, wk, wki, wv, wwi, pad, wu], axis=1).astype(BF16)
    return w_big, w_big[:, 0:C_U].T, wga.astype(BF16), wgb.astype(BF16)


def _pages_transposed(cache):
    return jnp.transpose(cache[0], (0, 2, 1))


def _heads_major(a, n_heads):
    b, t, w = a.shape
    d = w // n_heads
    return a.reshape(b, t, n_heads, d).transpose(0, 2, 1, 3).reshape(b, n_heads * t, d)


def kernel(x_prompt, x_sample, cache_k, cache_v, cache_kidx, state_pool, page_table, p_prompt, p_sample, w_in, w_att_out, w_pool_grp, pool_scale, w_pool_out, w_out, ln1_g, ln1_b, w_router, router_bias, w_exp13, w_exp2, w_sh13, w_sh2, ln2_g, ln2_b, w_ple_in, w_ple_gate):
    B, S, D = x_prompt.shape
    DB, T, _ = x_sample.shape
    depth = w_in.shape[0]
    assert depth == 1, "single layer step"
    page = cache_k.shape[2]
    past = page_table.shape[1] * page
    alpha = (2 * depth) ** 0.25
    n_p, n_s = B * S, DB * T

    w_big, w_t, wga, wgb = _fused_in_weight(w_in[0])
    wao, wpo, wo = w_att_out[0].astype(BF16), w_pool_out[0].astype(BF16), w_out[0].astype(BF16)
    wgrp = w_pool_grp[0].astype(BF16)
    pscale = pool_scale[0].reshape(1, POOL_WIDTH)
    g1, b1 = ln1_g[0].reshape(1, D), ln1_b[0].reshape(1, D)
    g2, b2 = ln2_g[0].reshape(1, D), ln2_b[0].reshape(1, D)
    wr_t = w_router[0].T.astype(BF16)
    rbias = router_bias[0].reshape(N_EXPERTS, 1)
    w13, w2 = w_exp13[0], w_exp2[0]
    ws13, ws2 = w_sh13[0].astype(BF16), w_sh2[0].astype(BF16)
    wpg, wpi = w_ple_gate[0].astype(BF16), w_ple_in[0].astype(BF16)

    cs_p = _rope_table(jnp.arange(S, dtype=I32))
    cs_s = jnp.tile(_rope_table(past + jnp.arange(T, dtype=I32)), (DB, 1))

    xp = x_prompt.reshape(n_p, D)
    qt, qit, wit, kb, kib, vbt, kt, vt, kit, u = _proj_prompt(xp, w_big, w_t, cs_p, S, PROJ_TILE)
    attn_p = _attn_prompt(qt, qit, wit, kb, kib, vbt)
    u3 = u.reshape(B, S, POOL_WIDTH)
    pool_p = _pool(jnp.zeros((B, PREV_ROWS, POOL_WIDTH), F32), u3, wgrp, pscale, 0, 1).reshape(n_p, POOL_WIDTH)
    h_p, hp_p = _merge(xp, attn_p, pool_p, wga, wgb, wao, wpo, wo, g1, b1, MERGE_TILE, alpha)

    xs = x_sample.reshape(n_s, D)
    qs, qis, ks, vs, kis, wis, us = _proj_sample(xs, w_big, cs_s)
    q_hq = _heads_major(qs.reshape(DB, T, ATT_WIDTH), N_HEADS)
    qi_hq = _heads_major(qis.reshape(DB, T, IDX_HEADS * IDX_DIM), IDX_HEADS)
    wi_hq = wis.reshape(DB, T, IDX_HEADS).transpose(0, 2, 1).reshape(DB, IDX_HEADS * T, 1)
    caches = (_pages_transposed(cache_k), _pages_transposed(cache_v), _pages_transposed(cache_kidx))
    new_rows = (ks.reshape(DB, T, HEAD_DIM), vs.reshape(DB, T, HEAD_DIM), kis.reshape(DB, T, IDX_DIM))
    half = DB // 2
    o_halves = [_attn_sample(page_table[sl], q_hq[sl], qi_hq[sl], wi_hq[sl], *(a[sl] for a in new_rows), *caches)
                for sl in (slice(0, half), slice(half, DB))]
    o_hq = jnp.concatenate(o_halves, axis=0)
    attn_s = o_hq.reshape(DB, N_HEADS, T, HEAD_DIM).transpose(0, 2, 1, 3).reshape(n_s, ATT_WIDTH).astype(BF16)
    us3 = us.reshape(DB, T, POOL_WIDTH)
    prev_s = jnp.concatenate([jnp.zeros((DB, PREV_ROWS - POOL_STATE, POOL_WIDTH), F32), state_pool[0]], axis=1)
    pool_s = _pool(prev_s, us3, wgrp, pscale, past, DB).reshape(n_s, POOL_WIDTH)
    h_s, _ = _merge(xs, attn_s, pool_s, wga, wgb, wao, wpo, wo, g1, b1, n_s, alpha)

    def tail(h, pe, tm_r, tm_m, tm_f):
        comb = _router(h, wr_t, rbias, tm_r)
        y = _moe(h, comb, ws13, ws2, w13, w2, tm_m)
        return _final(h, y, pe, g2, b2, wpg, wpi, tm_f, alpha)

    y_s = tail(h_s, p_sample[0].reshape(n_s, PLE_DIM), n_s, n_s, n_s)

    pos, gate, blk, used = _dispatch(h_p, wr_t, rbias, DISPATCH_TILE)
    sorted_in = _scatter_rows(hp_p, pos, _sorted_rows(n_p))
    anchor = lax.bitcast_convert_type(o_halves[0][0, 0, 0:1], I32)
    sorted_out = _grouped(blk.reshape(-1), used.reshape(-1), anchor, sorted_in, w13, w2)
    gathered = _gather_rows(sorted_out, pos)
    y_p = _combine(h_p, gathered, gate, p_prompt[0].reshape(n_p, PLE_DIM), ws13, ws2, g2, b2, wpg, wpi, COMBINE_TILE,
                   alpha)

    ext_s = jnp.concatenate([state_pool[0], us3], axis=1)
    return (y_p.reshape(B, S, D), y_s.reshape(DB, T, D),
            jnp.transpose(kt, (0, 2, 1))[None], jnp.transpose(vt, (0, 2, 1))[None],
            jnp.transpose(kit, (0, 2, 1))[None],
            u3[:, S - POOL_STATE:][None],
            ks.reshape(1, DB, T, HEAD_DIM), vs.reshape(1, DB, T, HEAD_DIM), kis.reshape(1, DB, T, IDX_DIM),
            ext_s[:, T:][None])
```

```python
import functools

import numpy as np
import jax
import jax.numpy as jnp
from jax import lax
from jax.experimental import pallas as pl
from jax.experimental.pallas import tpu as pltpu
from jax.experimental.pallas import tpu_sc as plsc

F32 = jnp.float32
BF16 = jnp.bfloat16
I32 = jnp.int32

D_MODEL = 1024
N_HEADS = 8
HEAD_DIM = 64
ATT_WIDTH = N_HEADS * HEAD_DIM
IDX_HEADS = 4
IDX_DIM = 64
TOP_K_MAX = 256
Q_BLOCK = 256
ROPE_THETA = 10000.0
POOL_WINDOWS = (2, 4, 8, 16)
POOL_GROUPS = 4
POOL_WIDTH = 512
POOL_GW = POOL_WIDTH // POOL_GROUPS
POOL_STATE = 15
N_EXPERTS = 64
TOP_K_EXPERTS = 8
N_GROUPS = 8
GROUP_SIZE = N_EXPERTS // N_GROUPS
TOPK_GROUPS = 4
EXPERT_DIM = 256
SHARED_DIM = 256
ROUTED_SCALE = 2.5
PLE_DIM = 256
LN_EPS = 1e-5
IN_SIZES = (ATT_WIDTH, HEAD_DIM, HEAD_DIM, IDX_HEADS * IDX_DIM, IDX_DIM, IDX_HEADS, POOL_WIDTH, D_MODEL, D_MODEL)

LANES = 128
SUBLANES = 8
INT_MIN = -2147483648
NEG_BIG = -1e30
VMEM_LIMIT = 56 * 1024 * 1024
PROJ_TILE = 512
MERGE_TILE = 1024
DISPATCH_TILE = 1024
COMBINE_TILE = 512

C_Q = 0
C_QI = 512
C_KK = 768
C_VW = 896
C_U = 1024
C_END = 1536
HALF = HEAD_DIM // 2

NT_DIMS = (((1,), (1,)), ((), ()))

Q_SCALE = HEAD_DIM ** -0.5 * float(np.log2(np.e))
QI_SCALE = IDX_DIM ** -0.5


def _params(sem):
    return pltpu.CompilerParams(dimension_semantics=sem, vmem_limit_bytes=VMEM_LIMIT)


def _layer_norm(x, g, b):
    mu = jnp.mean(x, axis=-1, keepdims=True)
    xc = x - mu
    var = jnp.mean(xc * xc, axis=-1, keepdims=True)
    return xc * lax.rsqrt(var + LN_EPS) * g + b


def _rope_rows(a, cos, sin):
    first_half = lax.broadcasted_iota(I32, (a.shape[0], LANES), 1) % HEAD_DIM < HALF
    out = []
    for s in range(a.shape[1] // LANES):
        x = a[:, s * LANES:(s + 1) * LANES]
        rot = jnp.where(first_half, -pltpu.roll(x, LANES - HALF, axis=1), pltpu.roll(x, HALF, axis=1))
        out.append(x * cos + rot * sin)
    return out[0] if len(out) == 1 else jnp.concatenate(out, axis=1)


def _proj_sample_kernel(x_ref, w_ref, cs_ref, q_ref, qi_ref, k_ref, v_ref, ki_ref, wi_ref, u_ref):
    xb = x_ref[...].astype(BF16)
    cos = cs_ref[:, 0:LANES]
    sin = cs_ref[:, LANES:2 * LANES]

    def mm(c0, n):
        return jnp.dot(xb, w_ref[:, c0:c0 + n], preferred_element_type=F32)

    def rope(c0, n):
        return _rope_rows(mm(c0, n), cos, sin)

    q_ref[...] = (rope(C_Q, ATT_WIDTH) * Q_SCALE).astype(BF16)
    qi_ref[...] = (rope(C_QI, IDX_HEADS * IDX_DIM) * QI_SCALE).astype(BF16)
    kk = rope(C_KK, LANES)
    k_ref[...] = kk[:, 0:HEAD_DIM]
    ki_ref[...] = kk[:, HEAD_DIM:2 * HEAD_DIM]
    vw = mm(C_VW, LANES)
    v_ref[...] = vw[:, 0:HEAD_DIM]
    wi_ref[...] = vw[:, HEAD_DIM:HEAD_DIM + IDX_HEADS] * (IDX_HEADS ** -0.5)
    u_ref[...] = mm(C_U, POOL_WIDTH)


def _proj_sample(x, w_big, cs):
    n = x.shape[0]
    full = lambda r, c: pl.BlockSpec((r, c), lambda i: (0, 0))
    widths = (ATT_WIDTH, IDX_HEADS * IDX_DIM, HEAD_DIM, HEAD_DIM, IDX_DIM, IDX_HEADS, POOL_WIDTH)
    dtypes = (BF16, BF16, F32, F32, F32, F32, F32)
    return pl.pallas_call(
        _proj_sample_kernel,
        grid=(1,),
        in_specs=[full(n, D_MODEL), full(D_MODEL, C_END), full(n, 2 * LANES)],
        out_specs=tuple(full(n, w) for w in widths),
        out_shape=tuple(jax.ShapeDtypeStruct((n, w), dt) for w, dt in zip(widths, dtypes)),
        compiler_params=_params(("arbitrary",)),
        name="proj_sample",
    )(x, w_big, cs)


def _proj_prompt_kernel(x_ref, w_ref, wt_ref, cs_ref, cst_ref, qt_ref, qit_ref, wit_ref, kb_ref, kib_ref, vbt_ref,
                        kt_ref, vt_ref, kit_ref, u_ref):
    xb = x_ref[...].astype(BF16)
    tm = xb.shape[0]
    cos = cs_ref[:, 0:LANES]
    sin = cs_ref[:, LANES:2 * LANES]
    cos_t = cst_ref[0:HEAD_DIM, :]
    sin_t = cst_ref[LANES:LANES + HEAD_DIM, :]

    def mm(c0, n):
        return jnp.dot(xb, w_ref[:, c0:c0 + n], preferred_element_type=F32)

    def mm_t(c0, n):
        return lax.dot_general(wt_ref[c0:c0 + n, :], xb, NT_DIMS, preferred_element_type=F32)

    def rope_t(c0, heads):
        a = mm_t(c0, heads * HEAD_DIM)
        parts = []
        for h in range(heads):
            x1 = a[h * HEAD_DIM:h * HEAD_DIM + HALF, :]
            x2 = a[h * HEAD_DIM + HALF:(h + 1) * HEAD_DIM, :]
            rot = jnp.concatenate([-x2, x1], axis=0)
            parts.append(a[h * HEAD_DIM:(h + 1) * HEAD_DIM, :] * cos_t + rot * sin_t)
        return parts[0] if heads == 1 else jnp.concatenate(parts, axis=0)

    kk = _rope_rows(mm(C_KK, LANES), cos, sin)
    kb_ref[...] = kk[:, 0:HEAD_DIM].astype(BF16)
    kib_ref[...] = kk[:, HEAD_DIM:2 * HEAD_DIM].astype(BF16)
    u_ref[...] = mm(C_U, POOL_WIDTH)

    qt = (rope_t(C_Q, N_HEADS) * Q_SCALE).astype(BF16)
    qit = (rope_t(C_QI, IDX_HEADS) * QI_SCALE).astype(BF16)
    for blk in range(tm // Q_BLOCK):
        cols = slice(blk * Q_BLOCK, (blk + 1) * Q_BLOCK)
        for h in range(N_HEADS):
            qt_ref[blk, :, h * Q_BLOCK:(h + 1) * Q_BLOCK] = qt[h * HEAD_DIM:(h + 1) * HEAD_DIM, cols]
        for h in range(IDX_HEADS):
            qit_ref[blk, :, h * Q_BLOCK:(h + 1) * Q_BLOCK] = qit[h * IDX_DIM:(h + 1) * IDX_DIM, cols]

    kkt = rope_t(C_KK, 2)
    kt_ref[...] = kkt[0:HEAD_DIM, :]
    kit_ref[...] = kkt[HEAD_DIM:2 * HEAD_DIM, :]
    vwt = mm_t(C_VW, LANES)
    vt_ref[...] = vwt[0:HEAD_DIM, :]
    vbt_ref[...] = vwt[0:HEAD_DIM, :].astype(BF16)
    wit_ref[...] = vwt[HEAD_DIM:HEAD_DIM + SUBLANES, :] * (IDX_HEADS ** -0.5)


def _proj_prompt(x, w_big, w_t, cs, seq, tm):
    n = x.shape[0]
    nb = seq // tm
    qb = tm // Q_BLOCK
    row = lambda w: pl.BlockSpec((tm, w), lambda i: (i, 0))
    col = lambda r: pl.BlockSpec((None, r, tm), lambda i: (i // nb, 0, i % nb))
    slab = lambda heads: pl.BlockSpec((qb, HEAD_DIM, heads * Q_BLOCK), lambda i: (i, 0, 0))
    pm = lambda r, dt: jax.ShapeDtypeStruct((n // seq, r, seq), dt)
    out_shape = (
        jax.ShapeDtypeStruct((n // Q_BLOCK, HEAD_DIM, N_HEADS * Q_BLOCK), BF16),
        jax.ShapeDtypeStruct((n // Q_BLOCK, IDX_DIM, IDX_HEADS * Q_BLOCK), BF16),
        pm(SUBLANES, F32),
        jax.ShapeDtypeStruct((n, HEAD_DIM), BF16), jax.ShapeDtypeStruct((n, IDX_DIM), BF16),
        pm(HEAD_DIM, BF16),
        pm(HEAD_DIM, F32), pm(HEAD_DIM, F32), pm(IDX_DIM, F32),
        jax.ShapeDtypeStruct((n, POOL_WIDTH), F32),
    )
    return pl.pallas_call(
        _proj_prompt_kernel,
        grid=(n // tm,),
        in_specs=[
            row(D_MODEL),
            pl.BlockSpec((D_MODEL, C_END), lambda i: (0, 0)),
            pl.BlockSpec((C_U, D_MODEL), lambda i: (0, 0)),
            pl.BlockSpec((tm, 2 * LANES), lambda i: (i % nb, 0)),
            pl.BlockSpec((2 * LANES, tm), lambda i: (0, i % nb)),
        ],
        out_specs=(slab(N_HEADS), slab(IDX_HEADS), col(SUBLANES), row(HEAD_DIM), row(IDX_DIM), col(HEAD_DIM),
                   col(HEAD_DIM), col(HEAD_DIM), col(IDX_DIM), row(POOL_WIDTH)),
        out_shape=out_shape,
        compiler_params=_params(("parallel",)),
        name="proj_prompt",
    )(x, w_big, w_t, cs, cs.T)


def _float_of_rank(u):
    key = u ^ INT_MIN
    bits = jnp.where(key < 0, INT_MIN - key, key)
    return pltpu.bitcast(bits, F32)


def _count(mask):
    return jnp.sum(mask.astype(F32), axis=1, keepdims=True)


def _topk_bias(sc_ref, j_ref, adm, n_adm, lc, k):
    rows = sc_ref.shape[0]
    kf = float(k)

    def value_step(i, t_u):
        hi = jnp.left_shift(jnp.int32(1), 31 - 2 * i)
        lo = jnp.left_shift(jnp.int32(1), 30 - 2 * i)
        for cand_u in (t_u | lo, t_u | hi, t_u | hi | lo):
            cnt = _count(sc_ref[:, 0:lc] >= _float_of_rank(cand_u))
            t_u = jnp.where(cnt >= kf, cand_u, t_u)
        return t_u

    t_u = lax.fori_loop(0, 16, value_step, jnp.zeros((rows, 1), I32))
    few = n_adm < k
    thr = jnp.where(few, -jnp.inf, _float_of_rank(t_u))
    sc = sc_ref[:, 0:lc]
    cnt_gt = _count(sc > thr)
    cnt_eq = _count(sc == thr)
    need = kf - cnt_gt
    cut_needed = jnp.logical_and(cnt_gt + cnt_eq > kf, jnp.logical_not(few))
    any_cut = jnp.max(cut_needed.astype(F32)) > 0.0
    idx = lax.broadcasted_iota(I32, (rows, lc), 1)
    nbits = int(np.ceil(np.log2(lc)))

    j_ref[...] = jnp.full((rows, 1), lc, I32)

    @pl.when(any_cut)
    def _():
        def index_step(i, j):
            cand = j | jnp.left_shift(jnp.int32(1), nbits - 1 - i)
            c = _count(jnp.logical_and(sc_ref[:, 0:lc] == thr, idx < cand))
            return jnp.where(c < need, cand, j)

        j_ref[...] = lax.fori_loop(0, nbits, index_step, jnp.zeros((rows, 1), I32))

    sel = jnp.logical_or(sc > thr, jnp.logical_and(sc == thr, idx <= j_ref[...]))
    return jnp.where(jnp.logical_and(sel, adm), 0.0, NEG_BIG)


ATTN_CHUNK = 256


def _attn_prompt_block(n_chunks, q0, top_k, qt_ref, qit_ref, wit_ref, kb_ref, kib_ref, vbt_ref, o_ref,
                       key_ref, bias_ref, lg_ref, j_ref):
    tq, ch = Q_BLOCK, ATTN_CHUNK
    kf = float(top_k)
    kpos = lax.broadcasted_iota(I32, (ch, tq), 0)
    qpos = q0 + lax.broadcasted_iota(I32, (ch, tq), 1)

    def rows(c):
        return slice(c * ch, (c + 1) * ch)

    def fold(x, op):
        return op(x.reshape(ch // SUBLANES, SUBLANES, tq), axis=0)

    def head(x, h):
        return x[:, h * tq:(h + 1) * tq]

    qit = qit_ref[...]
    wit = wit_ref[...]
    for c in range(n_chunks if n_chunks * ch > top_k else 0):
        d = jnp.dot(kib_ref[rows(c), :], qit, preferred_element_type=F32)
        s = wit[0:1, :] * jnp.maximum(head(d, 0), 0.0)
        for h in range(1, IDX_HEADS):
            s = s + wit[h:h + 1, :] * jnp.maximum(head(d, h), 0.0)
        key_ref[rows(c), :] = jnp.where(c * ch + kpos <= qpos, s, -jnp.inf)

    def count(pred):
        acc = jnp.zeros((SUBLANES, tq), F32)
        for c in range(n_chunks):
            acc = acc + fold(pred(key_ref[rows(c), :], c).astype(F32), jnp.sum)
        return jnp.sum(acc, axis=0, keepdims=True)

    if n_chunks * ch <= top_k:
        for c in range(n_chunks):
            bias_ref[rows(c), :] = jnp.where(c * ch + kpos <= qpos, 0.0, NEG_BIG)
    else:
        def value_step(i, carry):
            t_u, n_ge = carry
            cand_u = t_u | jnp.left_shift(jnp.int32(1), 31 - i)
            cand = _float_of_rank(cand_u)
            cnt = count(lambda k, c: k >= cand)
            ok = cnt >= kf
            return jnp.where(ok, cand_u, t_u), jnp.where(ok, cnt, n_ge)

        t_u, n_ge = lax.fori_loop(0, 32, value_step,
                                  (jnp.zeros((1, tq), I32), jnp.full((1, tq), float(n_chunks * ch), F32)))
        few = qpos[0:1, :] + 1 <= top_k
        thr = jnp.where(few, -jnp.inf, _float_of_rank(t_u))
        cut_needed = jnp.logical_and(n_ge > kf, jnp.logical_not(few))
        any_cut = jnp.max(cut_needed.astype(F32)) > 0.0

        nbits = int(np.ceil(np.log2(n_chunks * ch)))
        j_ref[...] = jnp.full(j_ref.shape, n_chunks * ch, I32)

        @pl.when(any_cut)
        def _():
            need = kf - count(lambda k, c: k > thr)

            def index_step(i, j):
                cand = j | jnp.left_shift(jnp.int32(1), nbits - 1 - i)
                n_before = count(lambda k, c: jnp.logical_and(k == thr, c * ch + kpos < cand))
                return jnp.where(n_before < need, cand, j)

            j = lax.fori_loop(0, nbits, index_step, jnp.zeros((1, tq), I32))
            j_ref[...] = jnp.broadcast_to(j, j_ref.shape)

        j_cut = j_ref[0:1, :]
        for c in range(n_chunks):
            k = key_ref[rows(c), :]
            pos = c * ch + kpos
            sel = jnp.logical_or(k > thr, jnp.logical_and(k == thr, pos <= j_cut))
            bias_ref[rows(c), :] = jnp.where(jnp.logical_and(sel, pos <= qpos), 0.0, NEG_BIG)

    qt = qt_ref[...]
    mx = [jnp.full((SUBLANES, tq), -jnp.inf, F32) for _ in range(N_HEADS)]
    for c in range(n_chunks):
        lg = jnp.dot(kb_ref[rows(c), :], qt, preferred_element_type=F32)
        bias = bias_ref[rows(c), :]
        for h in range(N_HEADS):
            lgh = head(lg, h) + bias
            lg_ref[h, rows(c), :] = lgh
            mx[h] = jnp.maximum(mx[h], fold(lgh, jnp.max))

    outs = []
    for h in range(N_HEADS):
        m = jnp.max(mx[h], axis=0, keepdims=True)
        lsum = jnp.zeros((SUBLANES, tq), F32)
        ot = jnp.zeros((HEAD_DIM, tq), F32)
        for c in range(n_chunks):
            p = jnp.exp2(lg_ref[h, rows(c), :] - m)
            lsum = lsum + fold(p, jnp.sum)
            ot = ot + jnp.dot(vbt_ref[:, rows(c)], p.astype(BF16), preferred_element_type=F32)
        outs.append(ot / jnp.sum(lsum, axis=0, keepdims=True))
    o_ref[...] = jnp.concatenate(outs, axis=0).T.astype(BF16)


def _attn_prompt_kernel(qt_ref, qit_ref, wit_ref, kb_ref, kib_ref, vbt_ref, o_ref, key_ref, bias_ref, lg_ref, j_ref,
                        *, top_k):
    jq = pl.program_id(1)
    blocks_per_chunk = ATTN_CHUNK // Q_BLOCK
    n_classes = key_ref.shape[0] // ATTN_CHUNK
    for cls in range(n_classes):
        @pl.when(jq // blocks_per_chunk == cls)
        def _(cls=cls):
            _attn_prompt_block(cls + 1, jq * Q_BLOCK, top_k, qt_ref, qit_ref, wit_ref, kb_ref, kib_ref, vbt_ref,
                               o_ref, key_ref, bias_ref, lg_ref, j_ref)


def _attn_prompt(qt, qit, wit, kb, kib, vbt):
    batch, _, seq = vbt.shape
    nb = seq // Q_BLOCK
    top_k = min(TOP_K_MAX, seq // 4)
    slab = lambda heads: pl.BlockSpec((None, HEAD_DIM, heads * Q_BLOCK), lambda b, j: (b * nb + j, 0, 0))
    keys = pl.BlockSpec((seq, HEAD_DIM), lambda b, j: (b, 0))
    return pl.pallas_call(
        functools.partial(_attn_prompt_kernel, top_k=top_k),
        grid=(batch, nb),
        in_specs=[slab(N_HEADS), slab(IDX_HEADS), pl.BlockSpec((None, SUBLANES, Q_BLOCK), lambda b, j: (b, 0, j)),
                  keys, keys, pl.BlockSpec((None, HEAD_DIM, seq), lambda b, j: (b, 0, 0))],
        out_specs=pl.BlockSpec((Q_BLOCK, ATT_WIDTH), lambda b, j: (b * nb + j, 0)),
        out_shape=jax.ShapeDtypeStruct((batch * seq, ATT_WIDTH), BF16),
        scratch_shapes=[pltpu.VMEM((seq, Q_BLOCK), F32), pltpu.VMEM((seq, Q_BLOCK), F32),
                        pltpu.VMEM((N_HEADS, seq, Q_BLOCK), F32), pltpu.VMEM((SUBLANES, Q_BLOCK), I32)],
        compiler_params=_params(("parallel", "arbitrary")),
        name="attn_prompt",
    )(qt, qit, wit, kb, kib, vbt)


SAMPLE_CHUNK = 1024
SAMPLE_ROWS_PER_STEP = 2


def _attn_sample_kernel(pt_ref, q_ref, qi_ref, wi_ref, kn_ref, vn_ref, kin_ref, ck_hbm, cv_hbm, cki_hbm, o_ref,
                        kbuf, vbuf, kibuf, sem, key_scr, bias_scr, lg_scr, j_scr, *, n_pages, page, t_new, top_k):
    b = pl.program_id(0)
    n_b = pl.num_programs(0)
    slot = b % 2
    per_step = q_ref.shape[0]
    past = n_pages * page
    lc = past + page
    n_chunks = past // SAMPLE_CHUNK

    def page_copies(step, sl, p):
        dst = pl.ds(pl.multiple_of(p * page, page), page)
        copies = []
        for r in range(per_step):
            phys = pt_ref[(step * per_step + r) * n_pages + p]
            copies += [pltpu.make_async_copy(src.at[phys], buf.at[sl, r, :, dst], sem.at[i, sl])
                       for i, (src, buf) in enumerate(((ck_hbm, kbuf), (cv_hbm, vbuf), (cki_hbm, kibuf)))]
        return copies

    def start_batch(bb, sl):
        def body(p, carry):
            for cp in page_copies(bb, sl, p):
                cp.start()
            return carry
        lax.fori_loop(0, n_pages, body, 0)

    def wait_batch(bb, sl):
        def body(p, carry):
            for cp in page_copies(bb, sl, p):
                cp.wait()
            return carry
        lax.fori_loop(0, n_pages, body, 0)

    @pl.when(b == 0)
    def _():
        start_batch(0, 0)

    @pl.when(b + 1 < n_b)
    def _():
        start_batch(b + 1, 1 - slot)

    wait_batch(b, slot)

    def head_sum(r, d):
        x = wi_ref[r] * jnp.maximum(d, 0.0)
        s = x[0:t_new]
        for h in range(1, IDX_HEADS):
            s = s + x[h * t_new:(h + 1) * t_new]
        return s

    def new_rows(ref, r):
        pad = jnp.zeros((page - t_new, ref.shape[2]), F32)
        return jnp.concatenate([ref[r], pad], axis=0).astype(BF16)

    adm_new = lax.broadcasted_iota(I32, (t_new, page), 1) <= lax.broadcasted_iota(I32, (t_new, page), 0)
    for r in range(per_step):
        qrows = slice(r * t_new, (r + 1) * t_new)
        qi = qi_ref[r]
        for c in range(n_chunks):
            sl = slice(c * SAMPLE_CHUNK, (c + 1) * SAMPLE_CHUNK)
            d = jnp.dot(qi, kibuf[slot, r, :, sl].astype(BF16), preferred_element_type=F32)
            key_scr[qrows, sl] = head_sum(r, d)
        d_new = lax.dot_general(qi, new_rows(kin_ref, r), NT_DIMS, preferred_element_type=F32)
        key_scr[qrows, past:lc] = jnp.where(adm_new, head_sum(r, d_new), -jnp.inf)

    n_q = per_step * t_new
    idx = lax.broadcasted_iota(I32, (n_q, lc), 1)
    trow = lax.broadcasted_iota(I32, (n_q, lc), 0) % t_new
    n_adm = past + 1 + lax.broadcasted_iota(I32, (n_q, 1), 0) % t_new
    bias_scr[...] = _topk_bias(key_scr, j_scr, idx - past <= trow, n_adm, lc, top_k)

    for r in range(per_step):
        q = q_ref[r]

        def bias_rows(sl, r=r):
            return jnp.concatenate([bias_scr[r * t_new:(r + 1) * t_new, sl]] * N_HEADS, axis=0)

        m = jnp.full((N_HEADS * t_new, 1), -jnp.inf, F32)
        for c in range(n_chunks):
            sl = slice(c * SAMPLE_CHUNK, (c + 1) * SAMPLE_CHUNK)
            lg = jnp.dot(q, kbuf[slot, r, :, sl].astype(BF16), preferred_element_type=F32) + bias_rows(sl)
            lg_scr[:, sl] = lg
            m = jnp.maximum(m, jnp.max(lg, axis=1, keepdims=True))
        lg_new = (lax.dot_general(q, new_rows(kn_ref, r), NT_DIMS, preferred_element_type=F32)
                  + bias_rows(slice(past, lc)))
        m = jnp.maximum(m, jnp.max(lg_new, axis=1, keepdims=True))

        p_new = jnp.exp2(lg_new - m)
        l = jnp.sum(p_new, axis=1, keepdims=True)
        o = jnp.dot(p_new.astype(BF16), new_rows(vn_ref, r), preferred_element_type=F32)
        for c in range(n_chunks):
            sl = slice(c * SAMPLE_CHUNK, (c + 1) * SAMPLE_CHUNK)
            pr = jnp.exp2(lg_scr[:, sl] - m)
            l = l + jnp.sum(pr, axis=1, keepdims=True)
            o = o + lax.dot_general(pr.astype(BF16), vbuf[slot, r, :, sl].astype(BF16), NT_DIMS,
                                    preferred_element_type=F32)
        o_ref[r] = o / l


def _attn_sample(page_table, q_hq, qi_hq, wi_hq, k_new, v_new, ki_new, cache_kt, cache_vt, cache_kit):
    db, n_pages = page_table.shape
    page = cache_kt.shape[2]
    t_new = k_new.shape[1]
    past = n_pages * page
    lc = past + page
    top_k = min(TOP_K_MAX, (past + t_new) // 4)
    rows = SAMPLE_ROWS_PER_STEP
    per_b = lambda r, w: pl.BlockSpec((rows, r, w), lambda b, pt: (b, 0, 0))
    hbm = pl.BlockSpec(memory_space=pl.ANY)
    kern = functools.partial(_attn_sample_kernel, n_pages=n_pages, page=page, t_new=t_new, top_k=top_k)
    slab = pltpu.VMEM((2, rows, HEAD_DIM, past), F32)
    grid_spec = pltpu.PrefetchScalarGridSpec(
        num_scalar_prefetch=1,
        grid=(db // rows,),
        in_specs=[per_b(N_HEADS * t_new, HEAD_DIM), per_b(IDX_HEADS * t_new, IDX_DIM), per_b(IDX_HEADS * t_new, 1),
                  per_b(t_new, HEAD_DIM), per_b(t_new, HEAD_DIM), per_b(t_new, IDX_DIM),
                  hbm, hbm, hbm],
        out_specs=per_b(N_HEADS * t_new, HEAD_DIM),
        scratch_shapes=[slab, slab, slab, pltpu.SemaphoreType.DMA((3, 2)),
                        pltpu.VMEM((rows * t_new, lc), F32), pltpu.VMEM((rows * t_new, lc), F32),
                        pltpu.VMEM((N_HEADS * t_new, past), F32), pltpu.VMEM((rows * t_new, 1), I32)],
    )
    return pl.pallas_call(
        kern,
        grid_spec=grid_spec,
        out_shape=jax.ShapeDtypeStruct((db, N_HEADS * t_new, HEAD_DIM), F32),
        compiler_params=_params(("arbitrary",)),
        name="attn_sample",
    )(page_table.reshape(-1), q_hq, qi_hq, wi_hq, k_new, v_new, ki_new, cache_kt, cache_vt, cache_kit)


PREV_ROWS = 16


def _pool_kernel(prev_ref, u_ref, wg_ref, sc_ref, o_ref, ext_ref, *, pos0):
    per_step, t_len, _ = u_ref.shape
    pos = pos0 + lax.broadcasted_iota(I32, (t_len, 1), 0)
    for b in range(per_step):
        ext_ref[0:PREV_ROWS, :] = prev_ref[b]
        ext_ref[PREV_ROWS:PREV_ROWS + t_len, :] = u_ref[b]
        for g, w in enumerate(POOL_WINDOWS):
            sl = slice(g * POOL_GW, (g + 1) * POOL_GW)
            u_new = ext_ref[PREV_ROWS:PREV_ROWS + t_len, sl]
            win = u_new
            for back in range(1, w):
                win = win + ext_ref[PREV_ROWS - back:PREV_ROWS - back + t_len, sl]
            count = jnp.minimum(pos + 1, w).astype(F32)
            r = win / count - u_new
            mixed = jnp.dot(r.astype(BF16), wg_ref[g], preferred_element_type=F32) * sc_ref[:, sl]
            o_ref[b, :, sl] = mixed.astype(BF16)


def _pool(prev, u, w_grp, scale, pos0, per_step):
    nb, t_len, _ = u.shape
    seqs = lambda rows: pl.BlockSpec((per_step, rows, POOL_WIDTH), lambda b: (b, 0, 0))
    return pl.pallas_call(
        functools.partial(_pool_kernel, pos0=pos0),
        grid=(nb // per_step,),
        in_specs=[seqs(PREV_ROWS), seqs(t_len),
                  pl.BlockSpec((POOL_GROUPS, POOL_GW, POOL_GW), lambda b: (0, 0, 0)),
                  pl.BlockSpec((1, POOL_WIDTH), lambda b: (0, 0))],
        out_specs=seqs(t_len),
        out_shape=jax.ShapeDtypeStruct((nb, t_len, POOL_WIDTH), BF16),
        scratch_shapes=[pltpu.VMEM((PREV_ROWS + t_len, POOL_WIDTH), F32)],
        compiler_params=_params(("parallel",)),
        name="pool",
    )(prev, u, w_grp, scale)


def _merge_kernel(x_ref, a_ref, p_ref, wga_ref, wgb_ref, wao_ref, wpo_ref, wo_ref, g_ref, b_ref, h_ref, hp_ref, *,
                  alpha):
    x = x_ref[...]
    xb = x.astype(BF16)
    ga = jnp.dot(xb, wga_ref[...], preferred_element_type=F32)
    gb = jnp.dot(xb, wgb_ref[...], preferred_element_type=F32)
    ya = jnp.dot(a_ref[...], wao_ref[...], preferred_element_type=F32)
    yp = jnp.dot(p_ref[...], wpo_ref[...], preferred_element_type=F32)
    mix = jax.nn.sigmoid(ga) * ya + jax.nn.sigmoid(gb) * yp
    out = jnp.dot(mix.astype(BF16), wo_ref[...], preferred_element_type=F32)
    h = _layer_norm(alpha * x + out, g_ref[...], b_ref[...])
    h_ref[...] = h
    hp_ref[...] = _pack_rows(h)


def _merge(x, attn, pool, wga, wgb, wao, wpo, wo, g, b, tm, alpha):
    n = x.shape[0]
    row = lambda w: pl.BlockSpec((tm, w), lambda i: (i, 0))
    full = lambda r, c: pl.BlockSpec((r, c), lambda i: (0, 0), pipeline_mode=pl.Buffered(1))
    return pl.pallas_call(
        functools.partial(_merge_kernel, alpha=alpha),
        grid=(n // tm,),
        in_specs=[row(D_MODEL), row(ATT_WIDTH), row(POOL_WIDTH), full(D_MODEL, D_MODEL), full(D_MODEL, D_MODEL),
                  full(ATT_WIDTH, D_MODEL), full(POOL_WIDTH, D_MODEL), full(D_MODEL, D_MODEL),
                  full(1, D_MODEL), full(1, D_MODEL)],
        out_specs=(row(D_MODEL), row(PACKED)),
        out_shape=(jax.ShapeDtypeStruct((n, D_MODEL), F32), jax.ShapeDtypeStruct((n, PACKED), I32)),
        compiler_params=_params(("parallel",)),
        name="merge",
    )(x, attn, pool, wga, wgb, wao, wpo, wo, g, b)


def _route(h, wr_t, bias_col):
    tm = h.shape[0]
    logits = lax.dot_general(wr_t, h.astype(BF16), NT_DIMS, preferred_element_type=F32)
    s = jax.nn.sigmoid(logits)
    sb = s + bias_col
    neg_inf = -jnp.inf

    rows = []
    for g in range(N_GROUPS):
        blk = sb[g * GROUP_SIZE:(g + 1) * GROUP_SIZE, :]
        m1 = jnp.max(blk, axis=0, keepdims=True)
        is_m1 = blk == m1
        n_m1 = jnp.sum(is_m1.astype(F32), axis=0, keepdims=True)
        m2 = jnp.max(jnp.where(is_m1, neg_inf, blk), axis=0, keepdims=True)
        rows.append(m1 + jnp.where(n_m1 >= 2.0, m1, m2))
    gs = jnp.concatenate(rows, axis=0)

    gi = lax.broadcasted_iota(I32, (N_GROUPS, tm), 0)
    rank = jnp.zeros((N_GROUPS, tm), F32)
    for g in range(N_GROUPS):
        row = gs[g:g + 1, :]
        beats = jnp.logical_or(row > gs, jnp.logical_and(row == gs, g < gi))
        rank = rank + beats.astype(F32)
    gkeep = rank < float(TOPK_GROUPS)
    emask = jnp.concatenate(
        [jnp.broadcast_to(gkeep[g:g + 1, :], (GROUP_SIZE, tm)) for g in range(N_GROUPS)], axis=0)

    ei = lax.broadcasted_iota(I32, (N_EXPERTS, tm), 0)
    x = jnp.where(emask, sb, neg_inf)
    sel = jnp.zeros((N_EXPERTS, tm), jnp.bool_)
    picks = []
    for _ in range(TOP_K_EXPERTS):
        m = jnp.max(x, axis=0, keepdims=True)
        first = jnp.min(jnp.where(x == m, ei, N_EXPERTS), axis=0, keepdims=True)
        pick = ei == first
        sel = jnp.logical_or(sel, pick)
        x = jnp.where(pick, neg_inf, x)
        picks.append(first)

    gate = jnp.where(sel, s, 0.0)
    comb = gate / jnp.sum(gate, axis=0, keepdims=True) * ROUTED_SCALE
    return comb, sel, picks


def _router_kernel(h_ref, wr_ref, bias_ref, c_ref):
    comb, _, _ = _route(h_ref[...], wr_ref[...], bias_ref[...])
    comb = jnp.concatenate([comb, jnp.zeros((LANES - N_EXPERTS, comb.shape[1]), F32)], axis=0)
    c_ref[...] = comb.T


def _router(h, wr_t, bias_col, tm):
    n = h.shape[0]
    return pl.pallas_call(
        _router_kernel,
        grid=(n // tm,),
        in_specs=[pl.BlockSpec((tm, D_MODEL), lambda i: (i, 0)),
                  pl.BlockSpec((N_EXPERTS, D_MODEL), lambda i: (0, 0)),
                  pl.BlockSpec((N_EXPERTS, 1), lambda i: (0, 0))],
        out_specs=pl.BlockSpec((tm, LANES), lambda i: (i, 0)),
        out_shape=jax.ShapeDtypeStruct((n, LANES), F32),
        compiler_params=_params(("parallel",)),
        name="router",
    )(h, wr_t, bias_col)


def _swiglu(xb, w13, w2, hidden):
    ab = jnp.dot(xb, w13, preferred_element_type=F32)
    act = jax.nn.silu(ab[:, 0:hidden]) * ab[:, hidden:2 * hidden]
    return jnp.dot(act.astype(BF16), w2, preferred_element_type=F32)


def _moe_kernel(h_ref, c_ref, ws13_ref, ws2_ref, w13_ref, w2_ref, y_ref, hb_ref):
    e = pl.program_id(1)

    @pl.when(e == 0)
    def _():
        hb_ref[...] = h_ref[...].astype(BF16)
        y_ref[...] = _swiglu(hb_ref[...], ws13_ref[...], ws2_ref[...], SHARED_DIM)

    ye = _swiglu(hb_ref[...], w13_ref[...].astype(BF16), w2_ref[...].astype(BF16), EXPERT_DIM)
    lane = lax.broadcasted_iota(I32, c_ref.shape, 1)
    ce = jnp.sum(jnp.where(lane == e, c_ref[...], 0.0), axis=1, keepdims=True)
    y_ref[...] += ce * ye


def _moe(h, comb, ws13, ws2, w13, w2, tm):
    n = h.shape[0]
    return pl.pallas_call(
        _moe_kernel,
        grid=(n // tm, N_EXPERTS),
        in_specs=[pl.BlockSpec((tm, D_MODEL), lambda i, e: (i, 0)),
                  pl.BlockSpec((tm, LANES), lambda i, e: (i, 0)),
                  pl.BlockSpec((D_MODEL, 2 * SHARED_DIM), lambda i, e: (0, 0)),
                  pl.BlockSpec((SHARED_DIM, D_MODEL), lambda i, e: (0, 0)),
                  pl.BlockSpec((None, D_MODEL, 2 * EXPERT_DIM), lambda i, e: (e, 0, 0)),
                  pl.BlockSpec((None, EXPERT_DIM, D_MODEL), lambda i, e: (e, 0, 0))],
        out_specs=pl.BlockSpec((tm, D_MODEL), lambda i, e: (i, 0)),
        out_shape=jax.ShapeDtypeStruct((n, D_MODEL), F32),
        scratch_shapes=[pltpu.VMEM((tm, D_MODEL), BF16)],
        compiler_params=_params(("parallel", "arbitrary")),
        name="moe",
    )(h, comb, ws13, ws2, w13, w2)


def _final_kernel(h_ref, y_ref, pe_ref, g_ref, b_ref, wpg_ref, wpi_ref, o_ref, *, alpha):
    z = _layer_norm(alpha * h_ref[...] + y_ref[...], g_ref[...], b_ref[...])
    gate = jax.nn.sigmoid(jnp.dot(z.astype(BF16), wpg_ref[...], preferred_element_type=F32))
    emb = jnp.dot(pe_ref[...].astype(BF16), wpi_ref[...], preferred_element_type=F32)
    o_ref[...] = z + gate * emb


def _final(h, y, pe, g, b, wpg, wpi, tm, alpha):
    n = h.shape[0]
    row = lambda w: pl.BlockSpec((tm, w), lambda i: (i, 0))
    full = lambda r, c: pl.BlockSpec((r, c), lambda i: (0, 0))
    return pl.pallas_call(
        functools.partial(_final_kernel, alpha=alpha),
        grid=(n // tm,),
        in_specs=[row(D_MODEL), row(D_MODEL), row(PLE_DIM), full(1, D_MODEL), full(1, D_MODEL),
                  full(D_MODEL, D_MODEL), full(PLE_DIM, D_MODEL)],
        out_specs=row(D_MODEL),
        out_shape=jax.ShapeDtypeStruct((n, D_MODEL), F32),
        compiler_params=_params(("parallel",)),
        name="final",
    )(h, y, pe, g, b, wpg, wpi)


MOE_BLOCK = 2176


def _sorted_rows(n_tokens):
    worst = n_tokens * TOP_K_EXPERTS + N_EXPERTS * (MOE_BLOCK - 1)
    return -(-worst // MOE_BLOCK) * MOE_BLOCK


def _dispatch_kernel(h_ref, wr_ref, bias_ref, tri_ref, pos_ref, gate_ref, blk_ref, used_ref,
                     eidx_s, rank_s, gate_s, cnt_s):
    p = pl.program_id(0)
    i = pl.program_id(1)
    tm = h_ref.shape[0]
    ei = lax.broadcasted_iota(I32, (N_EXPERTS, tm), 0)

    @pl.when(p == 0)
    def _():
        comb, sel, picks = _route(h_ref[...], wr_ref[...], bias_ref[...])
        before = jnp.dot(sel.astype(BF16), tri_ref[...], preferred_element_type=F32)
        ranks, gates = [], []
        for first in picks:
            pick = ei == first
            ranks.append(jnp.sum(jnp.where(pick, before, 0.0), axis=0, keepdims=True))
            gates.append(jnp.sum(jnp.where(pick, comb, 0.0), axis=0, keepdims=True))
        eidx_s[i] = jnp.concatenate(picks, axis=0)
        rank_s[i] = jnp.concatenate(ranks, axis=0)
        gate_s[i] = jnp.concatenate(gates, axis=0)
        cnt_s[i] = jnp.broadcast_to(jnp.sum(sel.astype(F32), axis=1, keepdims=True), (N_EXPERTS, LANES))

    @pl.when(p == 1)
    def _():
        cnt = cnt_s[...]
        tile_id = lax.broadcasted_iota(I32, cnt.shape, 0)
        total = jnp.sum(cnt, axis=0)
        prior = jnp.sum(jnp.where(tile_id < i, cnt, 0.0), axis=0)
        seg_blk = jnp.ceil(total * (1.0 / MOE_BLOCK) - 0.25 / MOE_BLOCK)
        lower = (lax.broadcasted_iota(I32, (N_EXPERTS, N_EXPERTS), 1)
                 < lax.broadcasted_iota(I32, (N_EXPERTS, N_EXPERTS), 0)).astype(F32)
        off_blk = jnp.dot(lower, seg_blk, precision=lax.Precision.HIGHEST, preferred_element_type=F32)
        seg_off = off_blk * MOE_BLOCK
        base = (seg_off + prior)[:, 0:1]
        eidx = eidx_s[i]
        rank = rank_s[i]
        rows = []
        for k in range(TOP_K_EXPERTS):
            pick = ei == eidx[k:k + 1, :]
            rows.append(rank[k:k + 1, :] + jnp.sum(jnp.where(pick, base, 0.0), axis=0, keepdims=True))
        pos_ref[...] = jnp.concatenate(rows, axis=0).astype(I32)
        gate_ref[...] = jnp.concatenate([gate_s[i], jnp.zeros((LANES - TOP_K_EXPERTS, tm), F32)], axis=0).T

        end_blk = (off_blk + seg_blk)[:, 0:1]
        n_blk = blk_ref.shape[1]
        blk_id = lax.broadcasted_iota(I32, (N_EXPERTS, n_blk), 1).astype(F32)
        owner = jnp.sum((end_blk <= blk_id).astype(F32), axis=0, keepdims=True)
        blk_ref[...] = jnp.minimum(owner, N_EXPERTS - 1.0).astype(I32)
        used_ref[...] = jnp.broadcast_to(end_blk[N_EXPERTS - 1:N_EXPERTS, :], used_ref.shape).astype(I32)


def _dispatch(h, wr_t, bias_col, tm):
    n = h.shape[0]
    n_tiles = n // tm
    n_blk = _sorted_rows(n) // MOE_BLOCK
    n_blk_pad = -(-n_blk // LANES) * LANES
    tri = jnp.triu(jnp.ones((tm, tm), BF16), k=1)
    const = lambda r, c: pl.BlockSpec((r, c), lambda p, i: (0, 0))
    per_tile = lambda dt: pltpu.VMEM((n_tiles, TOP_K_EXPERTS, tm), dt)
    return pl.pallas_call(
        _dispatch_kernel,
        grid=(2, n_tiles),
        in_specs=[pl.BlockSpec((tm, D_MODEL), lambda p, i: (i * (1 - p), 0)),
                  const(N_EXPERTS, D_MODEL), const(N_EXPERTS, 1), const(tm, tm)],
        out_specs=(pl.BlockSpec((TOP_K_EXPERTS, tm), lambda p, i: (0, i * p)),
                   pl.BlockSpec((tm, LANES), lambda p, i: (i * p, 0)),
                   const(1, n_blk_pad), const(1, LANES)),
        out_shape=(jax.ShapeDtypeStruct((TOP_K_EXPERTS, n), I32), jax.ShapeDtypeStruct((n, LANES), F32),
                   jax.ShapeDtypeStruct((1, n_blk_pad), I32), jax.ShapeDtypeStruct((1, LANES), I32)),
        scratch_shapes=[per_tile(I32), per_tile(F32), per_tile(F32), pltpu.VMEM((n_tiles, N_EXPERTS, LANES), F32)],
        compiler_params=_params(("arbitrary", "arbitrary")),
        name="dispatch",
    )(h, wr_t, bias_col, tri)


PACKED = D_MODEL // 2


def _pack_rows(x):
    lo = pltpu.bitcast(x[:, 0:PACKED].astype(BF16).astype(F32), I32)
    hi = pltpu.bitcast(x[:, PACKED:D_MODEL].astype(BF16).astype(F32), I32)
    return jnp.bitwise_or(hi, lax.shift_right_logical(lo, 16))


def _unpack_rows_f32(w):
    lo = pltpu.bitcast(lax.shift_left(w, 16), F32)
    hi = pltpu.bitcast(jnp.bitwise_and(w, -65536), F32)
    return jnp.concatenate([lo, hi], axis=1)


def _unpack_rows(w):
    return _unpack_rows_f32(w).astype(BF16)


def _grouped_kernel(blk_ref, used_ref, anchor_ref, xs_ref, w13_ref, w2_ref, ys_ref):
    @pl.when(pl.program_id(0) < used_ref[0])
    def _():
        ys = _swiglu(_unpack_rows(xs_ref[...]), w13_ref[...].astype(BF16), w2_ref[...].astype(BF16), EXPERT_DIM)
        ys_ref[...] = _pack_rows(ys)


def _grouped(blk, used, anchor, xs, w13, w2):
    ns = xs.shape[0]
    row_blk = lambda b, blk, used, anchor: (jnp.minimum(b, used[0] - 1), 0)
    expert = lambda b, blk, used, anchor: (blk[b], 0, 0)
    grid_spec = pltpu.PrefetchScalarGridSpec(
        num_scalar_prefetch=3,
        grid=(ns // MOE_BLOCK,),
        in_specs=[pl.BlockSpec((MOE_BLOCK, PACKED), row_blk),
                  pl.BlockSpec((None, D_MODEL, 2 * EXPERT_DIM), expert),
                  pl.BlockSpec((None, EXPERT_DIM, D_MODEL), expert)],
        out_specs=pl.BlockSpec((MOE_BLOCK, PACKED), row_blk),
    )
    return pl.pallas_call(
        _grouped_kernel,
        grid_spec=grid_spec,
        out_shape=jax.ShapeDtypeStruct((ns, PACKED), I32),
        compiler_params=_params(("arbitrary",)),
        name="grouped",
    )(blk, used, anchor, xs, w13, w2)


SC_WINDOW = 128


def _sc_mesh():
    return plsc.VectorSubcoreMesh(core_axis_name="core", subcore_axis_name="subcore")


def _sc_worker(n_items):
    info = plsc.get_sparse_core_info()
    n_workers = info.num_cores * info.num_subcores
    assert n_items % (SC_WINDOW * n_workers) == 0, "rows must split evenly into windows over the vector subcores"
    wid = lax.axis_index("subcore") * info.num_cores + lax.axis_index("core")
    return wid, n_items // (SC_WINDOW * n_workers)


def _scatter_rows(x, pos, n_out):
    n, width = x.shape
    picks = pos.shape[0]

    @functools.partial(
        pl.kernel, mesh=_sc_mesh(), out_type=jax.ShapeDtypeStruct((n_out, width), I32),
        scratch_types=[pltpu.VMEM((picks, SC_WINDOW), I32), pltpu.VMEM((SC_WINDOW, width), I32)],
        name="scatter_rows")
    def scatter(x_hbm, pos_hbm, out_hbm, idx_v, rows_v):
        wid, n_win = _sc_worker(n)

        @pl.loop(0, n_win)
        def _(j):
            base = (wid * n_win + j) * SC_WINDOW
            pltpu.sync_copy(pos_hbm.at[:, pl.ds(base, SC_WINDOW)], idx_v)
            pltpu.sync_copy(x_hbm.at[pl.ds(base, SC_WINDOW)], rows_v)
            for k in range(picks):
                pltpu.sync_copy(rows_v, out_hbm.at[idx_v.at[k]])

    return scatter(x, pos)


def _gather_rows(src, pos):
    width = src.shape[1]
    picks, n = pos.shape

    @functools.partial(
        pl.kernel, mesh=_sc_mesh(), out_type=jax.ShapeDtypeStruct((picks * n, width), I32),
        scratch_types=[pltpu.VMEM((SC_WINDOW,), I32), pltpu.VMEM((SC_WINDOW, width), I32)],
        name="gather_rows")
    def gather(src_hbm, pos_hbm, out_hbm, idx_v, rows_v):
        wid, n_win = _sc_worker(picks * n)

        @pl.loop(0, n_win)
        def _(j):
            base = (wid * n_win + j) * SC_WINDOW
            pltpu.sync_copy(pos_hbm.at[pl.ds(base, SC_WINDOW)], idx_v)
            pltpu.sync_copy(src_hbm.at[idx_v], rows_v)
            pltpu.sync_copy(rows_v, out_hbm.at[pl.ds(base, SC_WINDOW)])

    return gather(src, pos.reshape(-1)).reshape(picks, n, width)


def _combine_kernel(h_ref, g_ref, gate_ref, pe_ref, ws13_ref, ws2_ref, ln_g_ref, ln_b_ref, wpg_ref, wpi_ref, o_ref, *,
                    alpha):
    h = h_ref[...]
    y = _swiglu(h.astype(BF16), ws13_ref[...], ws2_ref[...], SHARED_DIM)
    gate = gate_ref[...]
    for k in range(TOP_K_EXPERTS):
        y = y + gate[:, k:k + 1] * _unpack_rows_f32(g_ref[k])
    z = _layer_norm(alpha * h + y, ln_g_ref[...], ln_b_ref[...])
    ple_gate = jax.nn.sigmoid(jnp.dot(z.astype(BF16), wpg_ref[...], preferred_element_type=F32))
    emb = jnp.dot(pe_ref[...].astype(BF16), wpi_ref[...], preferred_element_type=F32)
    o_ref[...] = z + ple_gate * emb


def _combine(h, gathered, gate, pe, ws13, ws2, g, b, wpg, wpi, tm, alpha):
    n = h.shape[0]
    row = lambda w: pl.BlockSpec((tm, w), lambda i: (i, 0))
    full = lambda r, c: pl.BlockSpec((r, c), lambda i: (0, 0))
    return pl.pallas_call(
        functools.partial(_combine_kernel, alpha=alpha),
        grid=(n // tm,),
        in_specs=[row(D_MODEL), pl.BlockSpec((TOP_K_EXPERTS, tm, PACKED), lambda i: (0, i, 0)), row(LANES),
                  row(PLE_DIM), full(D_MODEL, 2 * SHARED_DIM), full(SHARED_DIM, D_MODEL),
                  full(1, D_MODEL), full(1, D_MODEL), full(D_MODEL, D_MODEL), full(PLE_DIM, D_MODEL)],
        out_specs=row(D_MODEL),
        out_shape=jax.ShapeDtypeStruct((n, D_MODEL), F32),
        compiler_params=_params(("parallel",)),
        name="combine",
    )(h, gathered, gate, pe, ws13, ws2, g, b, wpg, wpi)


def _rope_table(pos):
    inv = ROPE_THETA ** (-jnp.arange(0, HEAD_DIM, 2, dtype=F32) / HEAD_DIM)
    ang = pos.astype(F32)[:, None] * inv[None, :]
    return jnp.concatenate([jnp.tile(jnp.cos(ang), (1, 4)), jnp.tile(jnp.sin(ang), (1, 4))], axis=1)


def _fused_in_weight(w_in):
    offs = np.cumsum(IN_SIZES)[:-1].tolist()
    wq, wk, wv, wqi, wki, wwi, wu, wga, wgb = jnp.split(w_in, offs, axis=1)
    pad = jnp.zeros((D_MODEL, LANES - HEAD_DIM - IDX_HEADS), w_in.dtype)
    w_big = jnp.concatenate([wq, wqi, wk, wki, wv, wwi, pad, wu], axis=1).astype(BF16)
    return w_big, w_big[:, 0:C_U].T, wga.astype(BF16), wgb.astype(BF16)


def _pages_transposed(cache):
    return jnp.transpose(cache[0], (0, 2, 1))


def _heads_major(a, n_heads):
    b, t, w = a.shape
    d = w // n_heads
    return a.reshape(b, t, n_heads, d).transpose(0, 2, 1, 3).reshape(b, n_heads * t, d)


def kernel(x_prompt, x_sample, cache_k, cache_v, cache_kidx, state_pool, page_table, p_prompt, p_sample, w_in, w_att_out, w_pool_grp, pool_scale, w_pool_out, w_out, ln1_g, ln1_b, w_router, router_bias, w_exp13, w_exp2, w_sh13, w_sh2, ln2_g, ln2_b, w_ple_in, w_ple_gate):
    B, S, D = x_prompt.shape
    DB, T, _ = x_sample.shape
    depth = w_in.shape[0]
    assert depth == 1, "single layer step"
    page = cache_k.shape[2]
    past = page_table.shape[1] * page
    alpha = (2 * depth) ** 0.25
    n_p, n_s = B * S, DB * T

    w_big, w_t, wga, wgb = _fused_in_weight(w_in[0])
    wao, wpo, wo = w_att_out[0].astype(BF16), w_pool_out[0].astype(BF16), w_out[0].astype(BF16)
    wgrp = w_pool_grp[0].astype(BF16)
    pscale = pool_scale[0].reshape(1, POOL_WIDTH)
    g1, b1 = ln1_g[0].reshape(1, D), ln1_b[0].reshape(1, D)
    g2, b2 = ln2_g[0].reshape(1, D), ln2_b[0].reshape(1, D)
    wr_t = w_router[0].T.astype(BF16)
    rbias = router_bias[0].reshape(N_EXPERTS, 1)
    w13, w2 = w_exp13[0], w_exp2[0]
    ws13, ws2 = w_sh13[0].astype(BF16), w_sh2[0].astype(BF16)
    wpg, wpi = w_ple_gate[0].astype(BF16), w_ple_in[0].astype(BF16)

    cs_p = _rope_table(jnp.arange(S, dtype=I32))
    cs_s = jnp.tile(_rope_table(past + jnp.arange(T, dtype=I32)), (DB, 1))

    xp = x_prompt.reshape(n_p, D)
    qt, qit, wit, kb, kib, vbt, kt, vt, kit, u = _proj_prompt(xp, w_big, w_t, cs_p, S, PROJ_TILE)
    attn_p = _attn_prompt(qt, qit, wit, kb, kib, vbt)
    u3 = u.reshape(B, S, POOL_WIDTH)
    pool_p = _pool(jnp.zeros((B, PREV_ROWS, POOL_WIDTH), F32), u3, wgrp, pscale, 0, 1).reshape(n_p, POOL_WIDTH)
    h_p, hp_p = _merge(xp, attn_p, pool_p, wga, wgb, wao, wpo, wo, g1, b1, MERGE_TILE, alpha)

    xs = x_sample.reshape(n_s, D)
    qs, qis, ks, vs, kis, wis, us = _proj_sample(xs, w_big, cs_s)
    q_hq = _heads_major(qs.reshape(DB, T, ATT_WIDTH), N_HEADS)
    qi_hq = _heads_major(qis.reshape(DB, T, IDX_HEADS * IDX_DIM), IDX_HEADS)
    wi_hq = wis.reshape(DB, T, IDX_HEADS).transpose(0, 2, 1).reshape(DB, IDX_HEADS * T, 1)
    caches = (_pages_transposed(cache_k), _pages_transposed(cache_v), _pages_transposed(cache_kidx))
    new_rows = (ks.reshape(DB, T, HEAD_DIM), vs.reshape(DB, T, HEAD_DIM), kis.reshape(DB, T, IDX_DIM))
    half = DB * 5 // 8 // SAMPLE_ROWS_PER_STEP * SAMPLE_ROWS_PER_STEP
    o_halves = [_attn_sample(page_table[sl], q_hq[sl], qi_hq[sl], wi_hq[sl], *(a[sl] for a in new_rows), *caches)
                for sl in (slice(0, half), slice(half, DB))]
    o_hq = jnp.concatenate(o_halves, axis=0)
    attn_s = o_hq.reshape(DB, N_HEADS, T, HEAD_DIM).transpose(0, 2, 1, 3).reshape(n_s, ATT_WIDTH).astype(BF16)
    us3 = us.reshape(DB, T, POOL_WIDTH)
    prev_s = jnp.concatenate([jnp.zeros((DB, PREV_ROWS - POOL_STATE, POOL_WIDTH), F32), state_pool[0]], axis=1)
    pool_s = _pool(prev_s, us3, wgrp, pscale, past, DB).reshape(n_s, POOL_WIDTH)
    h_s, _ = _merge(xs, attn_s, pool_s, wga, wgb, wao, wpo, wo, g1, b1, n_s, alpha)

    def tail(h, pe, tm_r, tm_m, tm_f):
        comb = _router(h, wr_t, rbias, tm_r)
        y = _moe(h, comb, ws13, ws2, w13, w2, tm_m)
        return _final(h, y, pe, g2, b2, wpg, wpi, tm_f, alpha)

    y_s = tail(h_s, p_sample[0].reshape(n_s, PLE_DIM), n_s, n_s, n_s)

    pos, gate, blk, used = _dispatch(h_p, wr_t, rbias, DISPATCH_TILE)
    sorted_in = _scatter_rows(hp_p, pos, _sorted_rows(n_p))
    anchor = lax.bitcast_convert_type(o_halves[0][0, 0, 0:1], I32)
    sorted_out = _grouped(blk.reshape(-1), used.reshape(-1), anchor, sorted_in, w13, w2)
    gathered = _gather_rows(sorted_out, pos)
    y_p = _combine(h_p, gathered, gate, p_prompt[0].reshape(n_p, PLE_DIM), ws13, ws2, g2, b2, wpg, wpi, COMBINE_TILE,
                   alpha)

    ext_s = jnp.concatenate([state_pool[0], us3], axis=1)
    return (y_p.reshape(B, S, D), y_s.reshape(DB, T, D),
            jnp.transpose(kt, (0, 2, 1))[None], jnp.transpose(vt, (0, 2, 1))[None],
            jnp.transpose(kit, (0, 2, 1))[None],
            u3[:, S - POOL_STATE:][None],
            ks.reshape(1, DB, T, HEAD_DIM), vs.reshape(1, DB, T, HEAD_DIM), kis.reshape(1, DB, T, IDX_DIM),
            ext_s[:, T:][None])
```

```python
import functools

import numpy as np
import jax
import jax.numpy as jnp
from jax import lax
from jax.experimental import pallas as pl
from jax.experimental.pallas import tpu as pltpu
from jax.experimental.pallas import tpu_sc as plsc

F32 = jnp.float32
BF16 = jnp.bfloat16
I32 = jnp.int32

D_MODEL = 1024
N_HEADS = 8
HEAD_DIM = 64
ATT_WIDTH = N_HEADS * HEAD_DIM
IDX_HEADS = 4
IDX_DIM = 64
TOP_K_MAX = 256
Q_BLOCK = 256
ROPE_THETA = 10000.0
POOL_WINDOWS = (2, 4, 8, 16)
POOL_GROUPS = 4
POOL_WIDTH = 512
POOL_GW = POOL_WIDTH // POOL_GROUPS
POOL_STATE = 15
N_EXPERTS = 64
TOP_K_EXPERTS = 8
N_GROUPS = 8
GROUP_SIZE = N_EXPERTS // N_GROUPS
TOPK_GROUPS = 4
EXPERT_DIM = 256
SHARED_DIM = 256
ROUTED_SCALE = 2.5
PLE_DIM = 256
LN_EPS = 1e-5
IN_SIZES = (ATT_WIDTH, HEAD_DIM, HEAD_DIM, IDX_HEADS * IDX_DIM, IDX_DIM, IDX_HEADS, POOL_WIDTH, D_MODEL, D_MODEL)

LANES = 128
SUBLANES = 8
INT_MIN = -2147483648
NEG_BIG = -1e30
VMEM_LIMIT = 56 * 1024 * 1024
PROJ_TILE = 512
MERGE_TILE = 1024
DISPATCH_TILE = 1024
COMBINE_TILE = 512

C_Q = 0
C_QI = 512
C_KK = 768
C_VW = 896
C_U = 1024
C_END = 1536
HALF = HEAD_DIM // 2

NT_DIMS = (((1,), (1,)), ((), ()))

Q_SCALE = HEAD_DIM ** -0.5 * float(np.log2(np.e))
QI_SCALE = IDX_DIM ** -0.5


def _params(sem):
    return pltpu.CompilerParams(dimension_semantics=sem, vmem_limit_bytes=VMEM_LIMIT)


def _layer_norm(x, g, b):
    mu = jnp.mean(x, axis=-1, keepdims=True)
    xc = x - mu
    var = jnp.mean(xc * xc, axis=-1, keepdims=True)
    return xc * lax.rsqrt(var + LN_EPS) * g + b


def _rope_rows(a, cos, sin):
    first_half = lax.broadcasted_iota(I32, (a.shape[0], LANES), 1) % HEAD_DIM < HALF
    out = []
    for s in range(a.shape[1] // LANES):
        x = a[:, s * LANES:(s + 1) * LANES]
        rot = jnp.where(first_half, -pltpu.roll(x, LANES - HALF, axis=1), pltpu.roll(x, HALF, axis=1))
        out.append(x * cos + rot * sin)
    return out[0] if len(out) == 1 else jnp.concatenate(out, axis=1)


def _proj_sample_kernel(x_ref, w_ref, cs_ref, q_ref, qi_ref, k_ref, v_ref, ki_ref, wi_ref, u_ref):
    xb = x_ref[...].astype(BF16)
    cos = cs_ref[:, 0:LANES]
    sin = cs_ref[:, LANES:2 * LANES]

    def mm(c0, n):
        return jnp.dot(xb, w_ref[:, c0:c0 + n], preferred_element_type=F32)

    def rope(c0, n):
        return _rope_rows(mm(c0, n), cos, sin)

    q_ref[...] = (rope(C_Q, ATT_WIDTH) * Q_SCALE).astype(BF16)
    qi_ref[...] = (rope(C_QI, IDX_HEADS * IDX_DIM) * QI_SCALE).astype(BF16)
    kk = rope(C_KK, LANES)
    k_ref[...] = kk[:, 0:HEAD_DIM]
    ki_ref[...] = kk[:, HEAD_DIM:2 * HEAD_DIM]
    vw = mm(C_VW, LANES)
    v_ref[...] = vw[:, 0:HEAD_DIM]
    wi_ref[...] = vw[:, HEAD_DIM:HEAD_DIM + IDX_HEADS] * (IDX_HEADS ** -0.5)
    u_ref[...] = mm(C_U, POOL_WIDTH)


def _proj_sample(x, w_big, cs):
    n = x.shape[0]
    full = lambda r, c: pl.BlockSpec((r, c), lambda i: (0, 0))
    widths = (ATT_WIDTH, IDX_HEADS * IDX_DIM, HEAD_DIM, HEAD_DIM, IDX_DIM, IDX_HEADS, POOL_WIDTH)
    dtypes = (BF16, BF16, F32, F32, F32, F32, F32)
    return pl.pallas_call(
        _proj_sample_kernel,
        grid=(1,),
        in_specs=[full(n, D_MODEL), full(D_MODEL, C_END), full(n, 2 * LANES)],
        out_specs=tuple(full(n, w) for w in widths),
        out_shape=tuple(jax.ShapeDtypeStruct((n, w), dt) for w, dt in zip(widths, dtypes)),
        compiler_params=_params(("arbitrary",)),
        name="proj_sample",
    )(x, w_big, cs)


def _proj_prompt_kernel(x_ref, w_ref, wt_ref, cs_ref, cst_ref, qt_ref, qit_ref, wit_ref, kb_ref, kib_ref, vbt_ref,
                        kt_ref, vt_ref, kit_ref, u_ref):
    xb = x_ref[...].astype(BF16)
    tm = xb.shape[0]
    cos = cs_ref[:, 0:LANES]
    sin = cs_ref[:, LANES:2 * LANES]
    cos_t = cst_ref[0:HEAD_DIM, :]
    sin_t = cst_ref[LANES:LANES + HEAD_DIM, :]

    def mm(c0, n):
        return jnp.dot(xb, w_ref[:, c0:c0 + n], preferred_element_type=F32)

    def mm_t(c0, n):
        return lax.dot_general(wt_ref[c0:c0 + n, :], xb, NT_DIMS, preferred_element_type=F32)

    def rope_t(c0, heads):
        a = mm_t(c0, heads * HEAD_DIM)
        parts = []
        for h in range(heads):
            x1 = a[h * HEAD_DIM:h * HEAD_DIM + HALF, :]
            x2 = a[h * HEAD_DIM + HALF:(h + 1) * HEAD_DIM, :]
            rot = jnp.concatenate([-x2, x1], axis=0)
            parts.append(a[h * HEAD_DIM:(h + 1) * HEAD_DIM, :] * cos_t + rot * sin_t)
        return parts[0] if heads == 1 else jnp.concatenate(parts, axis=0)

    kk = _rope_rows(mm(C_KK, LANES), cos, sin)
    kb_ref[...] = kk[:, 0:HEAD_DIM].astype(BF16)
    kib_ref[...] = kk[:, HEAD_DIM:2 * HEAD_DIM].astype(BF16)
    u_ref[...] = mm(C_U, POOL_WIDTH)

    qt = (rope_t(C_Q, N_HEADS) * Q_SCALE).astype(BF16)
    qit = (rope_t(C_QI, IDX_HEADS) * QI_SCALE).astype(BF16)
    for blk in range(tm // Q_BLOCK):
        cols = slice(blk * Q_BLOCK, (blk + 1) * Q_BLOCK)
        for h in range(N_HEADS):
            qt_ref[blk, :, h * Q_BLOCK:(h + 1) * Q_BLOCK] = qt[h * HEAD_DIM:(h + 1) * HEAD_DIM, cols]
        for h in range(IDX_HEADS):
            qit_ref[blk, :, h * Q_BLOCK:(h + 1) * Q_BLOCK] = qit[h * IDX_DIM:(h + 1) * IDX_DIM, cols]

    kkt = rope_t(C_KK, 2)
    kt_ref[...] = kkt[0:HEAD_DIM, :]
    kit_ref[...] = kkt[HEAD_DIM:2 * HEAD_DIM, :]
    vwt = mm_t(C_VW, LANES)
    vt_ref[...] = vwt[0:HEAD_DIM, :]
    vbt_ref[...] = vwt[0:HEAD_DIM, :].astype(BF16)
    wit_ref[...] = vwt[HEAD_DIM:HEAD_DIM + SUBLANES, :] * (IDX_HEADS ** -0.5)


def _proj_prompt(x, w_big, w_t, cs, seq, tm):
    n = x.shape[0]
    nb = seq // tm
    qb = tm // Q_BLOCK
    row = lambda w: pl.BlockSpec((tm, w), lambda i: (i, 0))
    col = lambda r: pl.BlockSpec((None, r, tm), lambda i: (i // nb, 0, i % nb))
    slab = lambda heads: pl.BlockSpec((qb, HEAD_DIM, heads * Q_BLOCK), lambda i: (i, 0, 0))
    pm = lambda r, dt: jax.ShapeDtypeStruct((n // seq, r, seq), dt)
    out_shape = (
        jax.ShapeDtypeStruct((n // Q_BLOCK, HEAD_DIM, N_HEADS * Q_BLOCK), BF16),
        jax.ShapeDtypeStruct((n // Q_BLOCK, IDX_DIM, IDX_HEADS * Q_BLOCK), BF16),
        pm(SUBLANES, F32),
        jax.ShapeDtypeStruct((n, HEAD_DIM), BF16), jax.ShapeDtypeStruct((n, IDX_DIM), BF16),
        pm(HEAD_DIM, BF16),
        pm(HEAD_DIM, F32), pm(HEAD_DIM, F32), pm(IDX_DIM, F32),
        jax.ShapeDtypeStruct((n, POOL_WIDTH), F32),
    )
    return pl.pallas_call(
        _proj_prompt_kernel,
        grid=(n // tm,),
        in_specs=[
            row(D_MODEL),
            pl.BlockSpec((D_MODEL, C_END), lambda i: (0, 0)),
            pl.BlockSpec((C_U, D_MODEL), lambda i: (0, 0)),
            pl.BlockSpec((tm, 2 * LANES), lambda i: (i % nb, 0)),
            pl.BlockSpec((2 * LANES, tm), lambda i: (0, i % nb)),
        ],
        out_specs=(slab(N_HEADS), slab(IDX_HEADS), col(SUBLANES), row(HEAD_DIM), row(IDX_DIM), col(HEAD_DIM),
                   col(HEAD_DIM), col(HEAD_DIM), col(IDX_DIM), row(POOL_WIDTH)),
        out_shape=out_shape,
        compiler_params=_params(("parallel",)),
        name="proj_prompt",
    )(x, w_big, w_t, cs, cs.T)


def _float_of_rank(u):
    key = u ^ INT_MIN
    bits = jnp.where(key < 0, INT_MIN - key, key)
    return pltpu.bitcast(bits, F32)


def _count(mask):
    return jnp.sum(mask.astype(F32), axis=1, keepdims=True)


def _topk_bias(sc_ref, j_ref, adm, n_adm, lc, k):
    rows = sc_ref.shape[0]
    kf = float(k)

    def value_step(i, t_u):
        hi = jnp.left_shift(jnp.int32(1), 31 - 2 * i)
        lo = jnp.left_shift(jnp.int32(1), 30 - 2 * i)
        for cand_u in (t_u | lo, t_u | hi, t_u | hi | lo):
            cnt = _count(sc_ref[:, 0:lc] >= _float_of_rank(cand_u))
            t_u = jnp.where(cnt >= kf, cand_u, t_u)
        return t_u

    t_u = lax.fori_loop(0, 16, value_step, jnp.zeros((rows, 1), I32))
    few = n_adm < k
    thr = jnp.where(few, -jnp.inf, _float_of_rank(t_u))
    sc = sc_ref[:, 0:lc]
    cnt_gt = _count(sc > thr)
    cnt_eq = _count(sc == thr)
    need = kf - cnt_gt
    cut_needed = jnp.logical_and(cnt_gt + cnt_eq > kf, jnp.logical_not(few))
    any_cut = jnp.max(cut_needed.astype(F32)) > 0.0
    idx = lax.broadcasted_iota(I32, (rows, lc), 1)
    nbits = int(np.ceil(np.log2(lc)))

    j_ref[...] = jnp.full((rows, 1), lc, I32)

    @pl.when(any_cut)
    def _():
        def index_step(i, j):
            cand = j | jnp.left_shift(jnp.int32(1), nbits - 1 - i)
            c = _count(jnp.logical_and(sc_ref[:, 0:lc] == thr, idx < cand))
            return jnp.where(c < need, cand, j)

        j_ref[...] = lax.fori_loop(0, nbits, index_step, jnp.zeros((rows, 1), I32))

    sel = jnp.logical_or(sc > thr, jnp.logical_and(sc == thr, idx <= j_ref[...]))
    return jnp.where(jnp.logical_and(sel, adm), 0.0, NEG_BIG)


ATTN_CHUNK = 256


def _attn_prompt_block(n_chunks, q0, top_k, qt_ref, qit_ref, wit_ref, kb_ref, kib_ref, vbt_ref, o_ref,
                       key_ref, bias_ref, lg_ref, j_ref):
    tq, ch = Q_BLOCK, ATTN_CHUNK
    kf = float(top_k)
    kpos = lax.broadcasted_iota(I32, (ch, tq), 0)
    qpos = q0 + lax.broadcasted_iota(I32, (ch, tq), 1)

    def rows(c):
        return slice(c * ch, (c + 1) * ch)

    def fold(x, op):
        return op(x.reshape(ch // SUBLANES, SUBLANES, tq), axis=0)

    def head(x, h):
        return x[:, h * tq:(h + 1) * tq]

    qit = qit_ref[...]
    wit = wit_ref[...]
    for c in range(n_chunks if n_chunks * ch > top_k else 0):
        d = jnp.dot(kib_ref[rows(c), :], qit, preferred_element_type=F32)
        s = wit[0:1, :] * jnp.maximum(head(d, 0), 0.0)
        for h in range(1, IDX_HEADS):
            s = s + wit[h:h + 1, :] * jnp.maximum(head(d, h), 0.0)
        key_ref[rows(c), :] = jnp.where(c * ch + kpos <= qpos, s, -jnp.inf)

    def count(pred):
        acc = jnp.zeros((SUBLANES, tq), F32)
        for c in range(n_chunks):
            acc = acc + fold(pred(key_ref[rows(c), :], c).astype(F32), jnp.sum)
        return jnp.sum(acc, axis=0, keepdims=True)

    if n_chunks * ch <= top_k:
        for c in range(n_chunks):
            bias_ref[rows(c), :] = jnp.where(c * ch + kpos <= qpos, 0.0, NEG_BIG)
    else:
        def value_step(i, carry):
            t_u, n_ge = carry
            cand_u = t_u | jnp.left_shift(jnp.int32(1), 31 - i)
            cand = _float_of_rank(cand_u)
            cnt = count(lambda k, c: k >= cand)
            ok = cnt >= kf
            return jnp.where(ok, cand_u, t_u), jnp.where(ok, cnt, n_ge)

        t_u, n_ge = lax.fori_loop(0, 32, value_step,
                                  (jnp.zeros((1, tq), I32), jnp.full((1, tq), float(n_chunks * ch), F32)))
        few = qpos[0:1, :] + 1 <= top_k
        thr = jnp.where(few, -jnp.inf, _float_of_rank(t_u))
        cut_needed = jnp.logical_and(n_ge > kf, jnp.logical_not(few))
        any_cut = jnp.max(cut_needed.astype(F32)) > 0.0

        nbits = int(np.ceil(np.log2(n_chunks * ch)))
        j_ref[...] = jnp.full(j_ref.shape, n_chunks * ch, I32)

        @pl.when(any_cut)
        def _():
            need = kf - count(lambda k, c: k > thr)

            def index_step(i, j):
                cand = j | jnp.left_shift(jnp.int32(1), nbits - 1 - i)
                n_before = count(lambda k, c: jnp.logical_and(k == thr, c * ch + kpos < cand))
                return jnp.where(n_before < need, cand, j)

            j = lax.fori_loop(0, nbits, index_step, jnp.zeros((1, tq), I32))
            j_ref[...] = jnp.broadcast_to(j, j_ref.shape)

        j_cut = j_ref[0:1, :]
        for c in range(n_chunks):
            k = key_ref[rows(c), :]
            pos = c * ch + kpos
            sel = jnp.logical_or(k > thr, jnp.logical_and(k == thr, pos <= j_cut))
            bias_ref[rows(c), :] = jnp.where(jnp.logical_and(sel, pos <= qpos), 0.0, NEG_BIG)

    qt = qt_ref[...]
    mx = [jnp.full((SUBLANES, tq), -jnp.inf, F32) for _ in range(N_HEADS)]
    for c in range(n_chunks):
        lg = jnp.dot(kb_ref[rows(c), :], qt, preferred_element_type=F32)
        bias = bias_ref[rows(c), :]
        for h in range(N_HEADS):
            lgh = head(lg, h) + bias
            lg_ref[h, rows(c), :] = lgh
            mx[h] = jnp.maximum(mx[h], fold(lgh, jnp.max))

    outs = []
    for h in range(N_HEADS):
        m = jnp.max(mx[h], axis=0, keepdims=True)
        lsum = jnp.zeros((SUBLANES, tq), F32)
        ot = jnp.zeros((HEAD_DIM, tq), F32)
        for c in range(n_chunks):
            p = jnp.exp2(lg_ref[h, rows(c), :] - m)
            lsum = lsum + fold(p, jnp.sum)
            ot = ot + jnp.dot(vbt_ref[:, rows(c)], p.astype(BF16), preferred_element_type=F32)
        outs.append(ot / jnp.sum(lsum, axis=0, keepdims=True))
    o_ref[...] = jnp.concatenate(outs, axis=0).T.astype(BF16)


def _attn_prompt_kernel(qt_ref, qit_ref, wit_ref, kb_ref, kib_ref, vbt_ref, o_ref, key_ref, bias_ref, lg_ref, j_ref,
                        *, top_k):
    jq = pl.program_id(1)
    blocks_per_chunk = ATTN_CHUNK // Q_BLOCK
    n_classes = key_ref.shape[0] // ATTN_CHUNK
    for cls in range(n_classes):
        @pl.when(jq // blocks_per_chunk == cls)
        def _(cls=cls):
            _attn_prompt_block(cls + 1, jq * Q_BLOCK, top_k, qt_ref, qit_ref, wit_ref, kb_ref, kib_ref, vbt_ref,
                               o_ref, key_ref, bias_ref, lg_ref, j_ref)


def _attn_prompt(qt, qit, wit, kb, kib, vbt):
    batch, _, seq = vbt.shape
    nb = seq // Q_BLOCK
    top_k = min(TOP_K_MAX, seq // 4)
    slab = lambda heads: pl.BlockSpec((None, HEAD_DIM, heads * Q_BLOCK), lambda b, j: (b * nb + j, 0, 0))
    keys = pl.BlockSpec((seq, HEAD_DIM), lambda b, j: (b, 0))
    return pl.pallas_call(
        functools.partial(_attn_prompt_kernel, top_k=top_k),
        grid=(batch, nb),
        in_specs=[slab(N_HEADS), slab(IDX_HEADS), pl.BlockSpec((None, SUBLANES, Q_BLOCK), lambda b, j: (b, 0, j)),
                  keys, keys, pl.BlockSpec((None, HEAD_DIM, seq), lambda b, j: (b, 0, 0))],
        out_specs=pl.BlockSpec((Q_BLOCK, ATT_WIDTH), lambda b, j: (b * nb + j, 0)),
        out_shape=jax.ShapeDtypeStruct((batch * seq, ATT_WIDTH), BF16),
        scratch_shapes=[pltpu.VMEM((seq, Q_BLOCK), F32), pltpu.VMEM((seq, Q_BLOCK), F32),
                        pltpu.VMEM((N_HEADS, seq, Q_BLOCK), F32), pltpu.VMEM((SUBLANES, Q_BLOCK), I32)],
        compiler_params=_params(("parallel", "arbitrary")),
        name="attn_prompt",
    )(qt, qit, wit, kb, kib, vbt)


SAMPLE_CHUNK = 1024
SAMPLE_ROWS_PER_STEP = 2


def _attn_sample_kernel(pt_ref, q_ref, qi_ref, wi_ref, kn_ref, vn_ref, kin_ref, ck_hbm, cv_hbm, cki_hbm, o_ref,
                        kbuf, vbuf, kibuf, sem, key_scr, bias_scr, lg_scr, j_scr, *, n_pages, page, t_new, top_k):
    b = pl.program_id(0)
    n_b = pl.num_programs(0)
    slot = b % 2
    per_step = q_ref.shape[0]
    past = n_pages * page
    lc = past + page
    n_chunks = past // SAMPLE_CHUNK

    def page_copies(step, sl, p):
        dst = pl.ds(pl.multiple_of(p * page, page), page)
        copies = []
        for r in range(per_step):
            phys = pt_ref[(step * per_step + r) * n_pages + p]
            copies += [pltpu.make_async_copy(src.at[phys], buf.at[sl, r, :, dst], sem.at[i, sl])
                       for i, (src, buf) in enumerate(((ck_hbm, kbuf), (cv_hbm, vbuf), (cki_hbm, kibuf)))]
        return copies

    def start_batch(bb, sl):
        def body(p, carry):
            for cp in page_copies(bb, sl, p):
                cp.start()
            return carry
        lax.fori_loop(0, n_pages, body, 0)

    def wait_batch(bb, sl):
        def body(p, carry):
            for cp in page_copies(bb, sl, p):
                cp.wait()
            return carry
        lax.fori_loop(0, n_pages, body, 0)

    @pl.when(b == 0)
    def _():
        start_batch(0, 0)

    @pl.when(b + 1 < n_b)
    def _():
        start_batch(b + 1, 1 - slot)

    wait_batch(b, slot)

    def head_sum(r, d):
        x = wi_ref[r] * jnp.maximum(d, 0.0)
        s = x[0:t_new]
        for h in range(1, IDX_HEADS):
            s = s + x[h * t_new:(h + 1) * t_new]
        return s

    def new_rows(ref, r):
        pad = jnp.zeros((page - t_new, ref.shape[2]), F32)
        return jnp.concatenate([ref[r], pad], axis=0).astype(BF16)

    adm_new = lax.broadcasted_iota(I32, (t_new, page), 1) <= lax.broadcasted_iota(I32, (t_new, page), 0)
    for r in range(per_step):
        qrows = slice(r * t_new, (r + 1) * t_new)
        qi = qi_ref[r]
        for c in range(n_chunks):
            sl = slice(c * SAMPLE_CHUNK, (c + 1) * SAMPLE_CHUNK)
            d = jnp.dot(qi, kibuf[slot, r, :, sl].astype(BF16), preferred_element_type=F32)
            key_scr[qrows, sl] = head_sum(r, d)
        d_new = lax.dot_general(qi, new_rows(kin_ref, r), NT_DIMS, preferred_element_type=F32)
        key_scr[qrows, past:lc] = jnp.where(adm_new, head_sum(r, d_new), -jnp.inf)

    n_q = per_step * t_new
    idx = lax.broadcasted_iota(I32, (n_q, lc), 1)
    trow = lax.broadcasted_iota(I32, (n_q, lc), 0) % t_new
    n_adm = past + 1 + lax.broadcasted_iota(I32, (n_q, 1), 0) % t_new
    bias_scr[...] = _topk_bias(key_scr, j_scr, idx - past <= trow, n_adm, lc, top_k)

    for r in range(per_step):
        q = q_ref[r]

        def bias_rows(sl, r=r):
            return jnp.concatenate([bias_scr[r * t_new:(r + 1) * t_new, sl]] * N_HEADS, axis=0)

        m = jnp.full((N_HEADS * t_new, 1), -jnp.inf, F32)
        for c in range(n_chunks):
            sl = slice(c * SAMPLE_CHUNK, (c + 1) * SAMPLE_CHUNK)
            lg = jnp.dot(q, kbuf[slot, r, :, sl].astype(BF16), preferred_element_type=F32) + bias_rows(sl)
            lg_scr[:, sl] = lg
            m = jnp.maximum(m, jnp.max(lg, axis=1, keepdims=True))
        lg_new = (lax.dot_general(q, new_rows(kn_ref, r), NT_DIMS, preferred_element_type=F32)
                  + bias_rows(slice(past, lc)))
        m = jnp.maximum(m, jnp.max(lg_new, axis=1, keepdims=True))

        p_new = jnp.exp2(lg_new - m)
        l = jnp.sum(p_new, axis=1, keepdims=True)
        o = jnp.dot(p_new.astype(BF16), new_rows(vn_ref, r), preferred_element_type=F32)
        for c in range(n_chunks):
            sl = slice(c * SAMPLE_CHUNK, (c + 1) * SAMPLE_CHUNK)
            pr = jnp.exp2(lg_scr[:, sl] - m)
            l = l + jnp.sum(pr, axis=1, keepdims=True)
            o = o + lax.dot_general(pr.astype(BF16), vbuf[slot, r, :, sl].astype(BF16), NT_DIMS,
                                    preferred_element_type=F32)
        o_ref[r] = o / l


def _attn_sample(page_table, q_hq, qi_hq, wi_hq, k_new, v_new, ki_new, cache_kt, cache_vt, cache_kit):
    db, n_pages = page_table.shape
    page = cache_kt.shape[2]
    t_new = k_new.shape[1]
    past = n_pages * page
    lc = past + page
    top_k = min(TOP_K_MAX, (past + t_new) // 4)
    rows = SAMPLE_ROWS_PER_STEP
    per_b = lambda r, w: pl.BlockSpec((rows, r, w), lambda b, pt: (b, 0, 0))
    hbm = pl.BlockSpec(memory_space=pl.ANY)
    kern = functools.partial(_attn_sample_kernel, n_pages=n_pages, page=page, t_new=t_new, top_k=top_k)
    slab = pltpu.VMEM((2, rows, HEAD_DIM, past), F32)
    grid_spec = pltpu.PrefetchScalarGridSpec(
        num_scalar_prefetch=1,
        grid=(db // rows,),
        in_specs=[per_b(N_HEADS * t_new, HEAD_DIM), per_b(IDX_HEADS * t_new, IDX_DIM), per_b(IDX_HEADS * t_new, 1),
                  per_b(t_new, HEAD_DIM), per_b(t_new, HEAD_DIM), per_b(t_new, IDX_DIM),
                  hbm, hbm, hbm],
        out_specs=per_b(N_HEADS * t_new, HEAD_DIM),
        scratch_shapes=[slab, slab, slab, pltpu.SemaphoreType.DMA((3, 2)),
                        pltpu.VMEM((rows * t_new, lc), F32), pltpu.VMEM((rows * t_new, lc), F32),
                        pltpu.VMEM((N_HEADS * t_new, past), F32), pltpu.VMEM((rows * t_new, 1), I32)],
    )
    return pl.pallas_call(
        kern,
        grid_spec=grid_spec,
        out_shape=jax.ShapeDtypeStruct((db, N_HEADS * t_new, HEAD_DIM), F32),
        compiler_params=_params(("arbitrary",)),
        name="attn_sample",
    )(page_table.reshape(-1), q_hq, qi_hq, wi_hq, k_new, v_new, ki_new, cache_kt, cache_vt, cache_kit)


PREV_ROWS = 16


def _pool_kernel(prev_ref, u_ref, wg_ref, sc_ref, o_ref, ext_ref, *, pos0):
    per_step, t_len, _ = u_ref.shape
    pos = pos0 + lax.broadcasted_iota(I32, (t_len, 1), 0)
    for b in range(per_step):
        ext_ref[0:PREV_ROWS, :] = prev_ref[b]
        ext_ref[PREV_ROWS:PREV_ROWS + t_len, :] = u_ref[b]
        for g, w in enumerate(POOL_WINDOWS):
            sl = slice(g * POOL_GW, (g + 1) * POOL_GW)
            u_new = ext_ref[PREV_ROWS:PREV_ROWS + t_len, sl]
            win = u_new
            for back in range(1, w):
                win = win + ext_ref[PREV_ROWS - back:PREV_ROWS - back + t_len, sl]
            count = jnp.minimum(pos + 1, w).astype(F32)
            r = win / count - u_new
            mixed = jnp.dot(r.astype(BF16), wg_ref[g], preferred_element_type=F32) * sc_ref[:, sl]
            o_ref[b, :, sl] = mixed.astype(BF16)


def _pool(prev, u, w_grp, scale, pos0, per_step):
    nb, t_len, _ = u.shape
    seqs = lambda rows: pl.BlockSpec((per_step, rows, POOL_WIDTH), lambda b: (b, 0, 0))
    return pl.pallas_call(
        functools.partial(_pool_kernel, pos0=pos0),
        grid=(nb // per_step,),
        in_specs=[seqs(PREV_ROWS), seqs(t_len),
                  pl.BlockSpec((POOL_GROUPS, POOL_GW, POOL_GW), lambda b: (0, 0, 0)),
                  pl.BlockSpec((1, POOL_WIDTH), lambda b: (0, 0))],
        out_specs=seqs(t_len),
        out_shape=jax.ShapeDtypeStruct((nb, t_len, POOL_WIDTH), BF16),
        scratch_shapes=[pltpu.VMEM((PREV_ROWS + t_len, POOL_WIDTH), F32)],
        compiler_params=_params(("parallel",)),
        name="pool",
    )(prev, u, w_grp, scale)


def _merge_kernel(x_ref, a_ref, p_ref, wga_ref, wgb_ref, wao_ref, wpo_ref, wo_ref, g_ref, b_ref, h_ref, hp_ref, *,
                  alpha):
    x = x_ref[...]
    xb = x.astype(BF16)
    ga = jnp.dot(xb, wga_ref[...], preferred_element_type=F32)
    gb = jnp.dot(xb, wgb_ref[...], preferred_element_type=F32)
    ya = jnp.dot(a_ref[...], wao_ref[...], preferred_element_type=F32)
    yp = jnp.dot(p_ref[...], wpo_ref[...], preferred_element_type=F32)
    mix = jax.nn.sigmoid(ga) * ya + jax.nn.sigmoid(gb) * yp
    out = jnp.dot(mix.astype(BF16), wo_ref[...], preferred_element_type=F32)
    h = _layer_norm(alpha * x + out, g_ref[...], b_ref[...])
    h_ref[...] = h
    hp_ref[...] = _pack_rows(h)


def _merge(x, attn, pool, wga, wgb, wao, wpo, wo, g, b, tm, alpha):
    n = x.shape[0]
    row = lambda w: pl.BlockSpec((tm, w), lambda i: (i, 0))
    full = lambda r, c: pl.BlockSpec((r, c), lambda i: (0, 0), pipeline_mode=pl.Buffered(1))
    return pl.pallas_call(
        functools.partial(_merge_kernel, alpha=alpha),
        grid=(n // tm,),
        in_specs=[row(D_MODEL), row(ATT_WIDTH), row(POOL_WIDTH), full(D_MODEL, D_MODEL), full(D_MODEL, D_MODEL),
                  full(ATT_WIDTH, D_MODEL), full(POOL_WIDTH, D_MODEL), full(D_MODEL, D_MODEL),
                  full(1, D_MODEL), full(1, D_MODEL)],
        out_specs=(row(D_MODEL), row(PACKED)),
        out_shape=(jax.ShapeDtypeStruct((n, D_MODEL), F32), jax.ShapeDtypeStruct((n, PACKED), I32)),
        compiler_params=_params(("parallel",)),
        name="merge",
    )(x, attn, pool, wga, wgb, wao, wpo, wo, g, b)


def _route(h, wr_t, bias_col):
    tm = h.shape[0]
    logits = lax.dot_general(wr_t, h.astype(BF16), NT_DIMS, preferred_element_type=F32)
    s = jax.nn.sigmoid(logits)
    sb = s + bias_col
    neg_inf = -jnp.inf

    rows = []
    for g in range(N_GROUPS):
        blk = sb[g * GROUP_SIZE:(g + 1) * GROUP_SIZE, :]
        m1 = jnp.max(blk, axis=0, keepdims=True)
        is_m1 = blk == m1
        n_m1 = jnp.sum(is_m1.astype(F32), axis=0, keepdims=True)
        m2 = jnp.max(jnp.where(is_m1, neg_inf, blk), axis=0, keepdims=True)
        rows.append(m1 + jnp.where(n_m1 >= 2.0, m1, m2))
    gs = jnp.concatenate(rows, axis=0)

    gi = lax.broadcasted_iota(I32, (N_GROUPS, tm), 0)
    rank = jnp.zeros((N_GROUPS, tm), F32)
    for g in range(N_GROUPS):
        row = gs[g:g + 1, :]
        beats = jnp.logical_or(row > gs, jnp.logical_and(row == gs, g < gi))
        rank = rank + beats.astype(F32)
    gkeep = rank < float(TOPK_GROUPS)
    emask = jnp.concatenate(
        [jnp.broadcast_to(gkeep[g:g + 1, :], (GROUP_SIZE, tm)) for g in range(N_GROUPS)], axis=0)

    ei = lax.broadcasted_iota(I32, (N_EXPERTS, tm), 0)
    x = jnp.where(emask, sb, neg_inf)
    sel = jnp.zeros((N_EXPERTS, tm), jnp.bool_)
    picks = []
    for _ in range(TOP_K_EXPERTS):
        m = jnp.max(x, axis=0, keepdims=True)
        first = jnp.min(jnp.where(x == m, ei, N_EXPERTS), axis=0, keepdims=True)
        pick = ei == first
        sel = jnp.logical_or(sel, pick)
        x = jnp.where(pick, neg_inf, x)
        picks.append(first)

    gate = jnp.where(sel, s, 0.0)
    comb = gate / jnp.sum(gate, axis=0, keepdims=True) * ROUTED_SCALE
    return comb, sel, picks


def _router_kernel(h_ref, wr_ref, bias_ref, c_ref):
    comb, _, _ = _route(h_ref[...], wr_ref[...], bias_ref[...])
    comb = jnp.concatenate([comb, jnp.zeros((LANES - N_EXPERTS, comb.shape[1]), F32)], axis=0)
    c_ref[...] = comb.T


def _router(h, wr_t, bias_col, tm):
    n = h.shape[0]
    return pl.pallas_call(
        _router_kernel,
        grid=(n // tm,),
        in_specs=[pl.BlockSpec((tm, D_MODEL), lambda i: (i, 0)),
                  pl.BlockSpec((N_EXPERTS, D_MODEL), lambda i: (0, 0)),
                  pl.BlockSpec((N_EXPERTS, 1), lambda i: (0, 0))],
        out_specs=pl.BlockSpec((tm, LANES), lambda i: (i, 0)),
        out_shape=jax.ShapeDtypeStruct((n, LANES), F32),
        compiler_params=_params(("parallel",)),
        name="router",
    )(h, wr_t, bias_col)


def _swiglu(xb, w13, w2, hidden):
    ab = jnp.dot(xb, w13, preferred_element_type=F32)
    act = jax.nn.silu(ab[:, 0:hidden]) * ab[:, hidden:2 * hidden]
    return jnp.dot(act.astype(BF16), w2, preferred_element_type=F32)


def _moe_kernel(h_ref, c_ref, ws13_ref, ws2_ref, w13_ref, w2_ref, y_ref, hb_ref):
    e = pl.program_id(1)

    @pl.when(e == 0)
    def _():
        hb_ref[...] = h_ref[...].astype(BF16)
        y_ref[...] = _swiglu(hb_ref[...], ws13_ref[...], ws2_ref[...], SHARED_DIM)

    ye = _swiglu(hb_ref[...], w13_ref[...].astype(BF16), w2_ref[...].astype(BF16), EXPERT_DIM)
    lane = lax.broadcasted_iota(I32, c_ref.shape, 1)
    ce = jnp.sum(jnp.where(lane == e, c_ref[...], 0.0), axis=1, keepdims=True)
    y_ref[...] += ce * ye


def _moe(h, comb, ws13, ws2, w13, w2, tm):
    n = h.shape[0]
    return pl.pallas_call(
        _moe_kernel,
        grid=(n // tm, N_EXPERTS),
        in_specs=[pl.BlockSpec((tm, D_MODEL), lambda i, e: (i, 0)),
                  pl.BlockSpec((tm, LANES), lambda i, e: (i, 0)),
                  pl.BlockSpec((D_MODEL, 2 * SHARED_DIM), lambda i, e: (0, 0)),
                  pl.BlockSpec((SHARED_DIM, D_MODEL), lambda i, e: (0, 0)),
                  pl.BlockSpec((None, D_MODEL, 2 * EXPERT_DIM), lambda i, e: (e, 0, 0)),
                  pl.BlockSpec((None, EXPERT_DIM, D_MODEL), lambda i, e: (e, 0, 0))],
        out_specs=pl.BlockSpec((tm, D_MODEL), lambda i, e: (i, 0)),
        out_shape=jax.ShapeDtypeStruct((n, D_MODEL), F32),
        scratch_shapes=[pltpu.VMEM((tm, D_MODEL), BF16)],
        compiler_params=_params(("parallel", "arbitrary")),
        name="moe",
    )(h, comb, ws13, ws2, w13, w2)


def _final_kernel(h_ref, y_ref, pe_ref, g_ref, b_ref, wpg_ref, wpi_ref, o_ref, *, alpha):
    z = _layer_norm(alpha * h_ref[...] + y_ref[...], g_ref[...], b_ref[...])
    gate = jax.nn.sigmoid(jnp.dot(z.astype(BF16), wpg_ref[...], preferred_element_type=F32))
    emb = jnp.dot(pe_ref[...].astype(BF16), wpi_ref[...], preferred_element_type=F32)
    o_ref[...] = z + gate * emb


def _final(h, y, pe, g, b, wpg, wpi, tm, alpha):
    n = h.shape[0]
    row = lambda w: pl.BlockSpec((tm, w), lambda i: (i, 0))
    full = lambda r, c: pl.BlockSpec((r, c), lambda i: (0, 0))
    return pl.pallas_call(
        functools.partial(_final_kernel, alpha=alpha),
        grid=(n // tm,),
        in_specs=[row(D_MODEL), row(D_MODEL), row(PLE_DIM), full(1, D_MODEL), full(1, D_MODEL),
                  full(D_MODEL, D_MODEL), full(PLE_DIM, D_MODEL)],
        out_specs=row(D_MODEL),
        out_shape=jax.ShapeDtypeStruct((n, D_MODEL), F32),
        compiler_params=_params(("parallel",)),
        name="final",
    )(h, y, pe, g, b, wpg, wpi)


MOE_BLOCK = 2176


def _sorted_rows(n_tokens):
    worst = n_tokens * TOP_K_EXPERTS + N_EXPERTS * (MOE_BLOCK - 1)
    return -(-worst // MOE_BLOCK) * MOE_BLOCK


def _dispatch_kernel(h_ref, wr_ref, bias_ref, tri_ref, pos_ref, gate_ref, blk_ref, used_ref,
                     eidx_s, rank_s, gate_s, cnt_s):
    p = pl.program_id(0)
    i = pl.program_id(1)
    tm = h_ref.shape[0]
    ei = lax.broadcasted_iota(I32, (N_EXPERTS, tm), 0)

    @pl.when(p == 0)
    def _():
        comb, sel, picks = _route(h_ref[...], wr_ref[...], bias_ref[...])
        before = jnp.dot(sel.astype(BF16), tri_ref[...], preferred_element_type=F32)
        ranks, gates = [], []
        for first in picks:
            pick = ei == first
            ranks.append(jnp.sum(jnp.where(pick, before, 0.0), axis=0, keepdims=True))
            gates.append(jnp.sum(jnp.where(pick, comb, 0.0), axis=0, keepdims=True))
        eidx_s[i] = jnp.concatenate(picks, axis=0)
        rank_s[i] = jnp.concatenate(ranks, axis=0)
        gate_s[i] = jnp.concatenate(gates, axis=0)
        cnt_s[i] = jnp.broadcast_to(jnp.sum(sel.astype(F32), axis=1, keepdims=True), (N_EXPERTS, LANES))

    @pl.when(p == 1)
    def _():
        cnt = cnt_s[...]
        tile_id = lax.broadcasted_iota(I32, cnt.shape, 0)
        total = jnp.sum(cnt, axis=0)
        prior = jnp.sum(jnp.where(tile_id < i, cnt, 0.0), axis=0)
        seg_blk = jnp.ceil(total * (1.0 / MOE_BLOCK) - 0.25 / MOE_BLOCK)
        lower = (lax.broadcasted_iota(I32, (N_EXPERTS, N_EXPERTS), 1)
                 < lax.broadcasted_iota(I32, (N_EXPERTS, N_EXPERTS), 0)).astype(F32)
        off_blk = jnp.dot(lower, seg_blk, precision=lax.Precision.HIGHEST, preferred_element_type=F32)
        seg_off = off_blk * MOE_BLOCK
        base = (seg_off + prior)[:, 0:1]
        eidx = eidx_s[i]
        rank = rank_s[i]
        rows = []
        for k in range(TOP_K_EXPERTS):
            pick = ei == eidx[k:k + 1, :]
            rows.append(rank[k:k + 1, :] + jnp.sum(jnp.where(pick, base, 0.0), axis=0, keepdims=True))
        pos_ref[...] = jnp.concatenate(rows, axis=0).astype(I32)
        gate_ref[...] = jnp.concatenate([gate_s[i], jnp.zeros((LANES - TOP_K_EXPERTS, tm), F32)], axis=0).T

        end_blk = (off_blk + seg_blk)[:, 0:1]
        n_blk = blk_ref.shape[1]
        blk_id = lax.broadcasted_iota(I32, (N_EXPERTS, n_blk), 1).astype(F32)
        owner = jnp.sum((end_blk <= blk_id).astype(F32), axis=0, keepdims=True)
        blk_ref[...] = jnp.minimum(owner, N_EXPERTS - 1.0).astype(I32)
        used_ref[...] = jnp.broadcast_to(end_blk[N_EXPERTS - 1:N_EXPERTS, :], used_ref.shape).astype(I32)


def _dispatch(h, wr_t, bias_col, tm):
    n = h.shape[0]
    n_tiles = n // tm
    n_blk = _sorted_rows(n) // MOE_BLOCK
    n_blk_pad = -(-n_blk // LANES) * LANES
    tri = jnp.triu(jnp.ones((tm, tm), BF16), k=1)
    const = lambda r, c: pl.BlockSpec((r, c), lambda p, i: (0, 0))
    per_tile = lambda dt: pltpu.VMEM((n_tiles, TOP_K_EXPERTS, tm), dt)
    return pl.pallas_call(
        _dispatch_kernel,
        grid=(2, n_tiles),
        in_specs=[pl.BlockSpec((tm, D_MODEL), lambda p, i: (i * (1 - p), 0)),
                  const(N_EXPERTS, D_MODEL), const(N_EXPERTS, 1), const(tm, tm)],
        out_specs=(pl.BlockSpec((TOP_K_EXPERTS, tm), lambda p, i: (0, i * p)),
                   pl.BlockSpec((tm, LANES), lambda p, i: (i * p, 0)),
                   const(1, n_blk_pad), const(1, LANES)),
        out_shape=(jax.ShapeDtypeStruct((TOP_K_EXPERTS, n), I32), jax.ShapeDtypeStruct((n, LANES), F32),
                   jax.ShapeDtypeStruct((1, n_blk_pad), I32), jax.ShapeDtypeStruct((1, LANES), I32)),
        scratch_shapes=[per_tile(I32), per_tile(F32), per_tile(F32), pltpu.VMEM((n_tiles, N_EXPERTS, LANES), F32)],
        compiler_params=_params(("arbitrary", "arbitrary")),
        name="dispatch",
    )(h, wr_t, bias_col, tri)


PACKED = D_MODEL // 2


def _pack_rows(x):
    lo = pltpu.bitcast(x[:, 0:PACKED].astype(BF16).astype(F32), I32)
    hi = pltpu.bitcast(x[:, PACKED:D_MODEL].astype(BF16).astype(F32), I32)
    return jnp.bitwise_or(hi, lax.shift_right_logical(lo, 16))


def _unpack_rows_f32(w):
    lo = pltpu.bitcast(lax.shift_left(w, 16), F32)
    hi = pltpu.bitcast(jnp.bitwise_and(w, -65536), F32)
    return jnp.concatenate([lo, hi], axis=1)


def _unpack_rows(w):
    return _unpack_rows_f32(w).astype(BF16)


def _grouped_kernel(blk_ref, used_ref, anchor_ref, xs_ref, w13_ref, w2_ref, ys_ref):
    @pl.when(pl.program_id(0) < used_ref[0])
    def _():
        ys = _swiglu(_unpack_rows(xs_ref[...]), w13_ref[...].astype(BF16), w2_ref[...].astype(BF16), EXPERT_DIM)
        ys_ref[...] = _pack_rows(ys)


def _grouped(blk, used, anchor, xs, w13, w2):
    ns = xs.shape[0]
    row_blk = lambda b, blk, used, anchor: (jnp.minimum(b, used[0] - 1), 0)
    expert = lambda b, blk, used, anchor: (blk[b], 0, 0)
    grid_spec = pltpu.PrefetchScalarGridSpec(
        num_scalar_prefetch=3,
        grid=(ns // MOE_BLOCK,),
        in_specs=[pl.BlockSpec((MOE_BLOCK, PACKED), row_blk),
                  pl.BlockSpec((None, D_MODEL, 2 * EXPERT_DIM), expert),
                  pl.BlockSpec((None, EXPERT_DIM, D_MODEL), expert)],
        out_specs=pl.BlockSpec((MOE_BLOCK, PACKED), row_blk),
    )
    return pl.pallas_call(
        _grouped_kernel,
        grid_spec=grid_spec,
        out_shape=jax.ShapeDtypeStruct((ns, PACKED), I32),
        compiler_params=_params(("arbitrary",)),
        name="grouped",
    )(blk, used, anchor, xs, w13, w2)


SC_WINDOW = 128


def _sc_mesh():
    return plsc.VectorSubcoreMesh(core_axis_name="core", subcore_axis_name="subcore")


def _sc_worker(n_items, window=SC_WINDOW):
    info = plsc.get_sparse_core_info()
    n_workers = info.num_cores * info.num_subcores
    assert n_items % (window * n_workers) == 0, "rows must split evenly into windows over the vector subcores"
    wid = lax.axis_index("subcore") * info.num_cores + lax.axis_index("core")
    return wid, n_items // (window * n_workers)


def _scatter_rows(x, pos, n_out):
    n, width = x.shape
    picks = pos.shape[0]

    @functools.partial(
        pl.kernel, mesh=_sc_mesh(), out_type=jax.ShapeDtypeStruct((n_out, width), I32),
        scratch_types=[pltpu.VMEM((picks, SC_WINDOW), I32), pltpu.VMEM((SC_WINDOW, width), I32)],
        name="scatter_rows")
    def scatter(x_hbm, pos_hbm, out_hbm, idx_v, rows_v):
        wid, n_win = _sc_worker(n)

        @pl.loop(0, n_win)
        def _(j):
            base = (wid * n_win + j) * SC_WINDOW
            pltpu.sync_copy(pos_hbm.at[:, pl.ds(base, SC_WINDOW)], idx_v)
            pltpu.sync_copy(x_hbm.at[pl.ds(base, SC_WINDOW)], rows_v)
            for k in range(picks):
                pltpu.sync_copy(rows_v, out_hbm.at[idx_v.at[k]])

    return scatter(x, pos)


def _gather_rows(src, pos):
    width = src.shape[1]
    picks, n = pos.shape
    win = SC_WINDOW // 2

    @functools.partial(
        pl.kernel, mesh=_sc_mesh(), out_type=jax.ShapeDtypeStruct((picks * n, width), I32),
        scratch_types=[pltpu.VMEM((win,), I32), pltpu.VMEM((2, win, width), I32), pltpu.SemaphoreType.DMA((2,))],
        name="gather_rows")
    def gather(src_hbm, pos_hbm, out_hbm, idx_v, rows_v, sem):
        wid, n_win = _sc_worker(picks * n, win)
        assert n_win % 2 == 0 and n_win >= 2
        first = wid * n_win

        def write_out(j, buf):
            return pltpu.make_async_copy(rows_v.at[buf], out_hbm.at[pl.ds((first + j) * win, win)], sem.at[buf])

        def window(j, buf, reuse):
            if reuse:
                write_out(j - 2, buf).wait()
            pltpu.sync_copy(pos_hbm.at[pl.ds((first + j) * win, win)], idx_v)
            pltpu.sync_copy(src_hbm.at[idx_v], rows_v.at[buf])
            write_out(j, buf).start()

        window(0, 0, False)
        window(1, 1, False)

        @pl.loop(2, n_win, step=2)
        def _(j):
            window(j, 0, True)
            window(j + 1, 1, True)

        write_out(n_win - 2, 0).wait()
        write_out(n_win - 1, 1).wait()

    return gather(src, pos.reshape(-1)).reshape(picks, n, width)


def _combine_kernel(h_ref, g_ref, gate_ref, pe_ref, ws13_ref, ws2_ref, ln_g_ref, ln_b_ref, wpg_ref, wpi_ref, o_ref, *,
                    alpha):
    h = h_ref[...]
    y = _swiglu(h.astype(BF16), ws13_ref[...], ws2_ref[...], SHARED_DIM)
    gate = gate_ref[...]
    for k in range(TOP_K_EXPERTS):
        y = y + gate[:, k:k + 1] * _unpack_rows_f32(g_ref[k])
    z = _layer_norm(alpha * h + y, ln_g_ref[...], ln_b_ref[...])
    ple_gate = jax.nn.sigmoid(jnp.dot(z.astype(BF16), wpg_ref[...], preferred_element_type=F32))
    emb = jnp.dot(pe_ref[...].astype(BF16), wpi_ref[...], preferred_element_type=F32)
    o_ref[...] = z + ple_gate * emb


def _combine(h, gathered, gate, pe, ws13, ws2, g, b, wpg, wpi, tm, alpha):
    n = h.shape[0]
    row = lambda w: pl.BlockSpec((tm, w), lambda i: (i, 0))
    full = lambda r, c: pl.BlockSpec((r, c), lambda i: (0, 0))
    return pl.pallas_call(
        functools.partial(_combine_kernel, alpha=alpha),
        grid=(n // tm,),
        in_specs=[row(D_MODEL), pl.BlockSpec((TOP_K_EXPERTS, tm, PACKED), lambda i: (0, i, 0)), row(LANES),
                  row(PLE_DIM), full(D_MODEL, 2 * SHARED_DIM), full(SHARED_DIM, D_MODEL),
                  full(1, D_MODEL), full(1, D_MODEL), full(D_MODEL, D_MODEL), full(PLE_DIM, D_MODEL)],
        out_specs=row(D_MODEL),
        out_shape=jax.ShapeDtypeStruct((n, D_MODEL), F32),
        compiler_params=_params(("parallel",)),
        name="combine",
    )(h, gathered, gate, pe, ws13, ws2, g, b, wpg, wpi)


def _rope_table(pos):
    inv = ROPE_THETA ** (-jnp.arange(0, HEAD_DIM, 2, dtype=F32) / HEAD_DIM)
    ang = pos.astype(F32)[:, None] * inv[None, :]
    return jnp.concatenate([jnp.tile(jnp.cos(ang), (1, 4)), jnp.tile(jnp.sin(ang), (1, 4))], axis=1)


def _fused_in_weight(w_in):
    offs = np.cumsum(IN_SIZES)[:-1].tolist()
    wq, wk, wv, wqi, wki, wwi, wu, wga, wgb = jnp.split(w_in, offs, axis=1)
    pad = jnp.zeros((D_MODEL, LANES - HEAD_DIM - IDX_HEADS), w_in.dtype)
    w_big = jnp.concatenate([wq, wqi, wk, wki, wv, wwi, pad, wu], axis=1).astype(BF16)
    return w_big, w_big[:, 0:C_U].T, wga.astype(BF16), wgb.astype(BF16)


def _pages_transposed(cache):
    return jnp.transpose(cache[0], (0, 2, 1))


def _heads_major(a, n_heads):
    b, t, w = a.shape
    d = w // n_heads
    return a.reshape(b, t, n_heads, d).transpose(0, 2, 1, 3).reshape(b, n_heads * t, d)


def kernel(x_prompt, x_sample, cache_k, cache_v, cache_kidx, state_pool, page_table, p_prompt, p_sample, w_in, w_att_out, w_pool_grp, pool_scale, w_pool_out, w_out, ln1_g, ln1_b, w_router, router_bias, w_exp13, w_exp2, w_sh13, w_sh2, ln2_g, ln2_b, w_ple_in, w_ple_gate):
    B, S, D = x_prompt.shape
    DB, T, _ = x_sample.shape
    depth = w_in.shape[0]
    assert depth == 1, "single layer step"
    page = cache_k.shape[2]
    past = page_table.shape[1] * page
    alpha = (2 * depth) ** 0.25
    n_p, n_s = B * S, DB * T

    w_big, w_t, wga, wgb = _fused_in_weight(w_in[0])
    wao, wpo, wo = w_att_out[0].astype(BF16), w_pool_out[0].astype(BF16), w_out[0].astype(BF16)
    wgrp = w_pool_grp[0].astype(BF16)
    pscale = pool_scale[0].reshape(1, POOL_WIDTH)
    g1, b1 = ln1_g[0].reshape(1, D), ln1_b[0].reshape(1, D)
    g2, b2 = ln2_g[0].reshape(1, D), ln2_b[0].reshape(1, D)
    wr_t = w_router[0].T.astype(BF16)
    rbias = router_bias[0].reshape(N_EXPERTS, 1)
    w13, w2 = w_exp13[0], w_exp2[0]
    ws13, ws2 = w_sh13[0].astype(BF16), w_sh2[0].astype(BF16)
    wpg, wpi = w_ple_gate[0].astype(BF16), w_ple_in[0].astype(BF16)

    cs_p = _rope_table(jnp.arange(S, dtype=I32))
    cs_s = jnp.tile(_rope_table(past + jnp.arange(T, dtype=I32)), (DB, 1))

    xp = x_prompt.reshape(n_p, D)
    qt, qit, wit, kb, kib, vbt, kt, vt, kit, u = _proj_prompt(xp, w_big, w_t, cs_p, S, PROJ_TILE)
    attn_p = _attn_prompt(qt, qit, wit, kb, kib, vbt)
    u3 = u.reshape(B, S, POOL_WIDTH)
    pool_p = _pool(jnp.zeros((B, PREV_ROWS, POOL_WIDTH), F32), u3, wgrp, pscale, 0, 1).reshape(n_p, POOL_WIDTH)
    h_p, hp_p = _merge(xp, attn_p, pool_p, wga, wgb, wao, wpo, wo, g1, b1, MERGE_TILE, alpha)

    xs = x_sample.reshape(n_s, D)
    qs, qis, ks, vs, kis, wis, us = _proj_sample(xs, w_big, cs_s)
    q_hq = _heads_major(qs.reshape(DB, T, ATT_WIDTH), N_HEADS)
    qi_hq = _heads_major(qis.reshape(DB, T, IDX_HEADS * IDX_DIM), IDX_HEADS)
    wi_hq = wis.reshape(DB, T, IDX_HEADS).transpose(0, 2, 1).reshape(DB, IDX_HEADS * T, 1)
    caches = (_pages_transposed(cache_k), _pages_transposed(cache_v), _pages_transposed(cache_kidx))
    new_rows = (ks.reshape(DB, T, HEAD_DIM), vs.reshape(DB, T, HEAD_DIM), kis.reshape(DB, T, IDX_DIM))
    half = DB * 5 // 8 // SAMPLE_ROWS_PER_STEP * SAMPLE_ROWS_PER_STEP
    o_halves = [_attn_sample(page_table[sl], q_hq[sl], qi_hq[sl], wi_hq[sl], *(a[sl] for a in new_rows), *caches)
                for sl in (slice(0, half), slice(half, DB))]
    o_hq = jnp.concatenate(o_halves, axis=0)
    attn_s = o_hq.reshape(DB, N_HEADS, T, HEAD_DIM).transpose(0, 2, 1, 3).reshape(n_s, ATT_WIDTH).astype(BF16)
    us3 = us.reshape(DB, T, POOL_WIDTH)
    prev_s = jnp.concatenate([jnp.zeros((DB, PREV_ROWS - POOL_STATE, POOL_WIDTH), F32), state_pool[0]], axis=1)
    pool_s = _pool(prev_s, us3, wgrp, pscale, past, DB).reshape(n_s, POOL_WIDTH)
    h_s, _ = _merge(xs, attn_s, pool_s, wga, wgb, wao, wpo, wo, g1, b1, n_s, alpha)

    def tail(h, pe, tm_r, tm_m, tm_f):
        comb = _router(h, wr_t, rbias, tm_r)
        y = _moe(h, comb, ws13, ws2, w13, w2, tm_m)
        return _final(h, y, pe, g2, b2, wpg, wpi, tm_f, alpha)

    y_s = tail(h_s, p_sample[0].reshape(n_s, PLE_DIM), n_s, n_s, n_s)

    pos, gate, blk, used = _dispatch(h_p, wr_t, rbias, DISPATCH_TILE)
    sorted_in = _scatter_rows(hp_p, pos, _sorted_rows(n_p))
    anchor = lax.bitcast_convert_type(o_halves[0][0, 0, 0:1], I32)
    sorted_out = _grouped(blk.reshape(-1), used.reshape(-1), anchor, sorted_in, w13, w2)
    gathered = _gather_rows(sorted_out, pos)
    y_p = _combine(h_p, gathered, gate, p_prompt[0].reshape(n_p, PLE_DIM), ws13, ws2, g2, b2, wpg, wpi, COMBINE_TILE,
                   alpha)

    ext_s = jnp.concatenate([state_pool[0], us3], axis=1)
    return (y_p.reshape(B, S, D), y_s.reshape(DB, T, D),
            jnp.transpose(kt, (0, 2, 1))[None], jnp.transpose(vt, (0, 2, 1))[None],
            jnp.transpose(kit, (0, 2, 1))[None],
            u3[:, S - POOL_STATE:][None],
            ks.reshape(1, DB, T, HEAD_DIM), vs.reshape(1, DB, T, HEAD_DIM), kis.reshape(1, DB, T, IDX_DIM),
            ext_s[:, T:][None])
```

```python
import functools

import numpy as np
import jax
import jax.numpy as jnp
from jax import lax
from jax.experimental import pallas as pl
from jax.experimental.pallas import tpu as pltpu
from jax.experimental.pallas import tpu_sc as plsc

F32 = jnp.float32
BF16 = jnp.bfloat16
I32 = jnp.int32

D_MODEL = 1024
N_HEADS = 8
HEAD_DIM = 64
ATT_WIDTH = N_HEADS * HEAD_DIM
IDX_HEADS = 4
IDX_DIM = 64
TOP_K_MAX = 256
Q_BLOCK = 256
ROPE_THETA = 10000.0
POOL_WINDOWS = (2, 4, 8, 16)
POOL_GROUPS = 4
POOL_WIDTH = 512
POOL_GW = POOL_WIDTH // POOL_GROUPS
POOL_STATE = 15
N_EXPERTS = 64
TOP_K_EXPERTS = 8
N_GROUPS = 8
GROUP_SIZE = N_EXPERTS // N_GROUPS
TOPK_GROUPS = 4
EXPERT_DIM = 256
SHARED_DIM = 256
ROUTED_SCALE = 2.5
PLE_DIM = 256
LN_EPS = 1e-5
IN_SIZES = (ATT_WIDTH, HEAD_DIM, HEAD_DIM, IDX_HEADS * IDX_DIM, IDX_DIM, IDX_HEADS, POOL_WIDTH, D_MODEL, D_MODEL)

LANES = 128
SUBLANES = 8
INT_MIN = -2147483648
NEG_BIG = -1e30
VMEM_LIMIT = 56 * 1024 * 1024
PROJ_TILE = 512
MERGE_TILE = 1024
DISPATCH_TILE = 1024
COMBINE_TILE = 512

C_Q = 0
C_QI = 512
C_KK = 768
C_VW = 896
C_U = 1024
C_END = 1536
HALF = HEAD_DIM // 2

NT_DIMS = (((1,), (1,)), ((), ()))

Q_SCALE = HEAD_DIM ** -0.5 * float(np.log2(np.e))
QI_SCALE = IDX_DIM ** -0.5


def _params(sem):
    return pltpu.CompilerParams(dimension_semantics=sem, vmem_limit_bytes=VMEM_LIMIT)


def _layer_norm(x, g, b):
    mu = jnp.mean(x, axis=-1, keepdims=True)
    xc = x - mu
    var = jnp.mean(xc * xc, axis=-1, keepdims=True)
    return xc * lax.rsqrt(var + LN_EPS) * g + b


def _rope_rows(a, cos, sin):
    first_half = lax.broadcasted_iota(I32, (a.shape[0], LANES), 1) % HEAD_DIM < HALF
    out = []
    for s in range(a.shape[1] // LANES):
        x = a[:, s * LANES:(s + 1) * LANES]
        rot = jnp.where(first_half, -pltpu.roll(x, LANES - HALF, axis=1), pltpu.roll(x, HALF, axis=1))
        out.append(x * cos + rot * sin)
    return out[0] if len(out) == 1 else jnp.concatenate(out, axis=1)


def _proj_sample_kernel(x_ref, w_ref, cs_ref, q_ref, qi_ref, k_ref, v_ref, ki_ref, wi_ref, u_ref):
    xb = x_ref[...].astype(BF16)
    cos = cs_ref[:, 0:LANES]
    sin = cs_ref[:, LANES:2 * LANES]

    def mm(c0, n):
        return jnp.dot(xb, w_ref[:, c0:c0 + n], preferred_element_type=F32)

    def rope(c0, n):
        return _rope_rows(mm(c0, n), cos, sin)

    q_ref[...] = (rope(C_Q, ATT_WIDTH) * Q_SCALE).astype(BF16)
    qi_ref[...] = (rope(C_QI, IDX_HEADS * IDX_DIM) * QI_SCALE).astype(BF16)
    kk = rope(C_KK, LANES)
    k_ref[...] = kk[:, 0:HEAD_DIM]
    ki_ref[...] = kk[:, HEAD_DIM:2 * HEAD_DIM]
    vw = mm(C_VW, LANES)
    v_ref[...] = vw[:, 0:HEAD_DIM]
    wi_ref[...] = vw[:, HEAD_DIM:HEAD_DIM + IDX_HEADS] * (IDX_HEADS ** -0.5)
    u_ref[...] = mm(C_U, POOL_WIDTH)


def _proj_sample(x, w_big, cs):
    n = x.shape[0]
    full = lambda r, c: pl.BlockSpec((r, c), lambda i: (0, 0))
    widths = (ATT_WIDTH, IDX_HEADS * IDX_DIM, HEAD_DIM, HEAD_DIM, IDX_DIM, IDX_HEADS, POOL_WIDTH)
    dtypes = (BF16, BF16, F32, F32, F32, F32, F32)
    return pl.pallas_call(
        _proj_sample_kernel,
        grid=(1,),
        in_specs=[full(n, D_MODEL), full(D_MODEL, C_END), full(n, 2 * LANES)],
        out_specs=tuple(full(n, w) for w in widths),
        out_shape=tuple(jax.ShapeDtypeStruct((n, w), dt) for w, dt in zip(widths, dtypes)),
        compiler_params=_params(("arbitrary",)),
        name="proj_sample",
    )(x, w_big, cs)


def _proj_prompt_kernel(x_ref, w_ref, wt_ref, cs_ref, cst_ref, qt_ref, qit_ref, wit_ref, kb_ref, kib_ref, vbt_ref,
                        kt_ref, vt_ref, kit_ref, u_ref):
    xb = x_ref[...].astype(BF16)
    tm = xb.shape[0]
    cos = cs_ref[:, 0:LANES]
    sin = cs_ref[:, LANES:2 * LANES]
    cos_t = cst_ref[0:HEAD_DIM, :]
    sin_t = cst_ref[LANES:LANES + HEAD_DIM, :]

    def mm(c0, n):
        return jnp.dot(xb, w_ref[:, c0:c0 + n], preferred_element_type=F32)

    def mm_t(c0, n):
        return lax.dot_general(wt_ref[c0:c0 + n, :], xb, NT_DIMS, preferred_element_type=F32)

    def rope_t(c0, heads):
        a = mm_t(c0, heads * HEAD_DIM)
        parts = []
        for h in range(heads):
            x1 = a[h * HEAD_DIM:h * HEAD_DIM + HALF, :]
            x2 = a[h * HEAD_DIM + HALF:(h + 1) * HEAD_DIM, :]
            rot = jnp.concatenate([-x2, x1], axis=0)
            parts.append(a[h * HEAD_DIM:(h + 1) * HEAD_DIM, :] * cos_t + rot * sin_t)
        return parts[0] if heads == 1 else jnp.concatenate(parts, axis=0)

    kk = _rope_rows(mm(C_KK, LANES), cos, sin)
    kb_ref[...] = kk[:, 0:HEAD_DIM].astype(BF16)
    kib_ref[...] = kk[:, HEAD_DIM:2 * HEAD_DIM].astype(BF16)
    u_ref[...] = mm(C_U, POOL_WIDTH)

    qt = (rope_t(C_Q, N_HEADS) * Q_SCALE).astype(BF16)
    qit = (rope_t(C_QI, IDX_HEADS) * QI_SCALE).astype(BF16)
    for blk in range(tm // Q_BLOCK):
        cols = slice(blk * Q_BLOCK, (blk + 1) * Q_BLOCK)
        for h in range(N_HEADS):
            qt_ref[blk, :, h * Q_BLOCK:(h + 1) * Q_BLOCK] = qt[h * HEAD_DIM:(h + 1) * HEAD_DIM, cols]
        for h in range(IDX_HEADS):
            qit_ref[blk, :, h * Q_BLOCK:(h + 1) * Q_BLOCK] = qit[h * IDX_DIM:(h + 1) * IDX_DIM, cols]

    kkt = rope_t(C_KK, 2)
    kt_ref[...] = kkt[0:HEAD_DIM, :]
    kit_ref[...] = kkt[HEAD_DIM:2 * HEAD_DIM, :]
    vwt = mm_t(C_VW, LANES)
    vt_ref[...] = vwt[0:HEAD_DIM, :]
    vbt_ref[...] = vwt[0:HEAD_DIM, :].astype(BF16)
    wit_ref[...] = vwt[HEAD_DIM:HEAD_DIM + SUBLANES, :] * (IDX_HEADS ** -0.5)


def _proj_prompt(x, w_big, w_t, cs, seq, tm):
    n = x.shape[0]
    nb = seq // tm
    qb = tm // Q_BLOCK
    row = lambda w: pl.BlockSpec((tm, w), lambda i: (i, 0))
    col = lambda r: pl.BlockSpec((None, r, tm), lambda i: (i // nb, 0, i % nb))
    slab = lambda heads: pl.BlockSpec((qb, HEAD_DIM, heads * Q_BLOCK), lambda i: (i, 0, 0))
    pm = lambda r, dt: jax.ShapeDtypeStruct((n // seq, r, seq), dt)
    out_shape = (
        jax.ShapeDtypeStruct((n // Q_BLOCK, HEAD_DIM, N_HEADS * Q_BLOCK), BF16),
        jax.ShapeDtypeStruct((n // Q_BLOCK, IDX_DIM, IDX_HEADS * Q_BLOCK), BF16),
        pm(SUBLANES, F32),
        jax.ShapeDtypeStruct((n, HEAD_DIM), BF16), jax.ShapeDtypeStruct((n, IDX_DIM), BF16),
        pm(HEAD_DIM, BF16),
        pm(HEAD_DIM, F32), pm(HEAD_DIM, F32), pm(IDX_DIM, F32),
        jax.ShapeDtypeStruct((n, POOL_WIDTH), F32),
    )
    return pl.pallas_call(
        _proj_prompt_kernel,
        grid=(n // tm,),
        in_specs=[
            row(D_MODEL),
            pl.BlockSpec((D_MODEL, C_END), lambda i: (0, 0)),
            pl.BlockSpec((C_U, D_MODEL), lambda i: (0, 0)),
            pl.BlockSpec((tm, 2 * LANES), lambda i: (i % nb, 0)),
            pl.BlockSpec((2 * LANES, tm), lambda i: (0, i % nb)),
        ],
        out_specs=(slab(N_HEADS), slab(IDX_HEADS), col(SUBLANES), row(HEAD_DIM), row(IDX_DIM), col(HEAD_DIM),
                   col(HEAD_DIM), col(HEAD_DIM), col(IDX_DIM), row(POOL_WIDTH)),
        out_shape=out_shape,
        compiler_params=_params(("parallel",)),
        name="proj_prompt",
    )(x, w_big, w_t, cs, cs.T)


def _float_of_rank(u):
    key = u ^ INT_MIN
    bits = jnp.where(key < 0, INT_MIN - key, key)
    return pltpu.bitcast(bits, F32)


def _count(mask):
    return jnp.sum(mask.astype(F32), axis=1, keepdims=True)


def _topk_bias(sc_ref, j_ref, adm, n_adm, lc, k):
    rows = sc_ref.shape[0]
    kf = float(k)

    def value_step(i, t_u):
        hi = jnp.left_shift(jnp.int32(1), 31 - 2 * i)
        lo = jnp.left_shift(jnp.int32(1), 30 - 2 * i)
        for cand_u in (t_u | lo, t_u | hi, t_u | hi | lo):
            cnt = _count(sc_ref[:, 0:lc] >= _float_of_rank(cand_u))
            t_u = jnp.where(cnt >= kf, cand_u, t_u)
        return t_u

    t_u = lax.fori_loop(0, 16, value_step, jnp.zeros((rows, 1), I32))
    few = n_adm < k
    thr = jnp.where(few, -jnp.inf, _float_of_rank(t_u))
    sc = sc_ref[:, 0:lc]
    cnt_gt = _count(sc > thr)
    cnt_eq = _count(sc == thr)
    need = kf - cnt_gt
    cut_needed = jnp.logical_and(cnt_gt + cnt_eq > kf, jnp.logical_not(few))
    any_cut = jnp.max(cut_needed.astype(F32)) > 0.0
    idx = lax.broadcasted_iota(I32, (rows, lc), 1)
    nbits = int(np.ceil(np.log2(lc)))

    j_ref[...] = jnp.full((rows, 1), lc, I32)

    @pl.when(any_cut)
    def _():
        def index_step(i, j):
            cand = j | jnp.left_shift(jnp.int32(1), nbits - 1 - i)
            c = _count(jnp.logical_and(sc_ref[:, 0:lc] == thr, idx < cand))
            return jnp.where(c < need, cand, j)

        j_ref[...] = lax.fori_loop(0, nbits, index_step, jnp.zeros((rows, 1), I32))

    sel = jnp.logical_or(sc > thr, jnp.logical_and(sc == thr, idx <= j_ref[...]))
    return jnp.where(jnp.logical_and(sel, adm), 0.0, NEG_BIG)


ATTN_CHUNK = 256


def _attn_prompt_block(n_chunks, q0, top_k, qt_ref, qit_ref, wit_ref, kb_ref, kib_ref, vbt_ref, o_ref,
                       key_ref, bias_ref, lg_ref, j_ref):
    tq, ch = Q_BLOCK, ATTN_CHUNK
    kf = float(top_k)
    kpos = lax.broadcasted_iota(I32, (ch, tq), 0)
    qpos = q0 + lax.broadcasted_iota(I32, (ch, tq), 1)

    def rows(c):
        return slice(c * ch, (c + 1) * ch)

    def fold(x, op):
        return op(x.reshape(ch // SUBLANES, SUBLANES, tq), axis=0)

    def head(x, h):
        return x[:, h * tq:(h + 1) * tq]

    qit = qit_ref[...]
    wit = wit_ref[...]
    for c in range(n_chunks if n_chunks * ch > top_k else 0):
        d = jnp.dot(kib_ref[rows(c), :], qit, preferred_element_type=F32)
        s = wit[0:1, :] * jnp.maximum(head(d, 0), 0.0)
        for h in range(1, IDX_HEADS):
            s = s + wit[h:h + 1, :] * jnp.maximum(head(d, h), 0.0)
        key_ref[rows(c), :] = jnp.where(c * ch + kpos <= qpos, s, -jnp.inf)

    def count(pred):
        acc = jnp.zeros((SUBLANES, tq), F32)
        for c in range(n_chunks):
            acc = acc + fold(pred(key_ref[rows(c), :], c).astype(F32), jnp.sum)
        return jnp.sum(acc, axis=0, keepdims=True)

    if n_chunks * ch <= top_k:
        for c in range(n_chunks):
            bias_ref[rows(c), :] = jnp.where(c * ch + kpos <= qpos, 0.0, NEG_BIG)
    else:
        def value_step(i, carry):
            t_u, n_ge = carry
            cand_u = t_u | jnp.left_shift(jnp.int32(1), 31 - i)
            cand = _float_of_rank(cand_u)
            cnt = count(lambda k, c: k >= cand)
            ok = cnt >= kf
            return jnp.where(ok, cand_u, t_u), jnp.where(ok, cnt, n_ge)

        t_u, n_ge = lax.fori_loop(0, 32, value_step,
                                  (jnp.zeros((1, tq), I32), jnp.full((1, tq), float(n_chunks * ch), F32)))
        few = qpos[0:1, :] + 1 <= top_k
        thr = jnp.where(few, -jnp.inf, _float_of_rank(t_u))
        cut_needed = jnp.logical_and(n_ge > kf, jnp.logical_not(few))
        any_cut = jnp.max(cut_needed.astype(F32)) > 0.0

        nbits = int(np.ceil(np.log2(n_chunks * ch)))
        j_ref[...] = jnp.full(j_ref.shape, n_chunks * ch, I32)

        @pl.when(any_cut)
        def _():
            need = kf - count(lambda k, c: k > thr)

            def index_step(i, j):
                cand = j | jnp.left_shift(jnp.int32(1), nbits - 1 - i)
                n_before = count(lambda k, c: jnp.logical_and(k == thr, c * ch + kpos < cand))
                return jnp.where(n_before < need, cand, j)

            j = lax.fori_loop(0, nbits, index_step, jnp.zeros((1, tq), I32))
            j_ref[...] = jnp.broadcast_to(j, j_ref.shape)

        j_cut = j_ref[0:1, :]
        for c in range(n_chunks):
            k = key_ref[rows(c), :]
            pos = c * ch + kpos
            sel = jnp.logical_or(k > thr, jnp.logical_and(k == thr, pos <= j_cut))
            bias_ref[rows(c), :] = jnp.where(jnp.logical_and(sel, pos <= qpos), 0.0, NEG_BIG)

    qt = qt_ref[...]
    mx = [jnp.full((SUBLANES, tq), -jnp.inf, F32) for _ in range(N_HEADS)]
    for c in range(n_chunks):
        lg = jnp.dot(kb_ref[rows(c), :], qt, preferred_element_type=F32)
        bias = bias_ref[rows(c), :]
        for h in range(N_HEADS):
            lgh = head(lg, h) + bias
            lg_ref[h, rows(c), :] = lgh
            mx[h] = jnp.maximum(mx[h], fold(lgh, jnp.max))

    outs = []
    for h in range(N_HEADS):
        m = jnp.max(mx[h], axis=0, keepdims=True)
        lsum = jnp.zeros((SUBLANES, tq), F32)
        ot = jnp.zeros((HEAD_DIM, tq), F32)
        for c in range(n_chunks):
            p = jnp.exp2(lg_ref[h, rows(c), :] - m)
            lsum = lsum + fold(p, jnp.sum)
            ot = ot + jnp.dot(vbt_ref[:, rows(c)], p.astype(BF16), preferred_element_type=F32)
        outs.append(ot / jnp.sum(lsum, axis=0, keepdims=True))
    o_ref[...] = jnp.concatenate(outs, axis=0).T.astype(BF16)


def _attn_prompt_kernel(qt_ref, qit_ref, wit_ref, kb_ref, kib_ref, vbt_ref, o_ref, key_ref, bias_ref, lg_ref, j_ref,
                        *, top_k):
    jq = pl.program_id(1)
    blocks_per_chunk = ATTN_CHUNK // Q_BLOCK
    n_classes = key_ref.shape[0] // ATTN_CHUNK
    for cls in range(n_classes):
        @pl.when(jq // blocks_per_chunk == cls)
        def _(cls=cls):
            _attn_prompt_block(cls + 1, jq * Q_BLOCK, top_k, qt_ref, qit_ref, wit_ref, kb_ref, kib_ref, vbt_ref,
                               o_ref, key_ref, bias_ref, lg_ref, j_ref)


def _attn_prompt(qt, qit, wit, kb, kib, vbt):
    batch, _, seq = vbt.shape
    nb = seq // Q_BLOCK
    top_k = min(TOP_K_MAX, seq // 4)
    slab = lambda heads: pl.BlockSpec((None, HEAD_DIM, heads * Q_BLOCK), lambda b, j: (b * nb + j, 0, 0))
    keys = pl.BlockSpec((seq, HEAD_DIM), lambda b, j: (b, 0))
    return pl.pallas_call(
        functools.partial(_attn_prompt_kernel, top_k=top_k),
        grid=(batch, nb),
        in_specs=[slab(N_HEADS), slab(IDX_HEADS), pl.BlockSpec((None, SUBLANES, Q_BLOCK), lambda b, j: (b, 0, j)),
                  keys, keys, pl.BlockSpec((None, HEAD_DIM, seq), lambda b, j: (b, 0, 0))],
        out_specs=pl.BlockSpec((Q_BLOCK, ATT_WIDTH), lambda b, j: (b * nb + j, 0)),
        out_shape=jax.ShapeDtypeStruct((batch * seq, ATT_WIDTH), BF16),
        scratch_shapes=[pltpu.VMEM((seq, Q_BLOCK), F32), pltpu.VMEM((seq, Q_BLOCK), F32),
                        pltpu.VMEM((N_HEADS, seq, Q_BLOCK), F32), pltpu.VMEM((SUBLANES, Q_BLOCK), I32)],
        compiler_params=_params(("parallel", "arbitrary")),
        name="attn_prompt",
    )(qt, qit, wit, kb, kib, vbt)


SAMPLE_CHUNK = 1024
SAMPLE_ROWS_PER_STEP = 2


def _attn_sample_kernel(pt_ref, q_ref, qi_ref, wi_ref, kn_ref, vn_ref, kin_ref, ck_hbm, cv_hbm, cki_hbm, o_ref,
                        kbuf, vbuf, kibuf, sem, key_scr, bias_scr, lg_scr, j_scr, *, n_pages, page, t_new, top_k):
    b = pl.program_id(0)
    n_b = pl.num_programs(0)
    slot = b % 2
    per_step = q_ref.shape[0]
    past = n_pages * page
    lc = past + page
    n_chunks = past // SAMPLE_CHUNK

    def page_copies(step, sl, p):
        dst = pl.ds(pl.multiple_of(p * page, page), page)
        copies = []
        for r in range(per_step):
            phys = pt_ref[(step * per_step + r) * n_pages + p]
            copies += [pltpu.make_async_copy(src.at[phys], buf.at[sl, r, :, dst], sem.at[i, sl])
                       for i, (src, buf) in enumerate(((ck_hbm, kbuf), (cv_hbm, vbuf), (cki_hbm, kibuf)))]
        return copies

    def start_batch(bb, sl):
        def body(p, carry):
            for cp in page_copies(bb, sl, p):
                cp.start()
            return carry
        lax.fori_loop(0, n_pages, body, 0)

    def wait_batch(bb, sl):
        def body(p, carry):
            for cp in page_copies(bb, sl, p):
                cp.wait()
            return carry
        lax.fori_loop(0, n_pages, body, 0)

    @pl.when(b == 0)
    def _():
        start_batch(0, 0)

    @pl.when(b + 1 < n_b)
    def _():
        start_batch(b + 1, 1 - slot)

    wait_batch(b, slot)

    def head_sum(r, d):
        x = wi_ref[r] * jnp.maximum(d, 0.0)
        s = x[0:t_new]
        for h in range(1, IDX_HEADS):
            s = s + x[h * t_new:(h + 1) * t_new]
        return s

    def new_rows(ref, r):
        pad = jnp.zeros((page - t_new, ref.shape[2]), F32)
        return jnp.concatenate([ref[r], pad], axis=0).astype(BF16)

    adm_new = lax.broadcasted_iota(I32, (t_new, page), 1) <= lax.broadcasted_iota(I32, (t_new, page), 0)
    for r in range(per_step):
        qrows = slice(r * t_new, (r + 1) * t_new)
        qi = qi_ref[r]
        for c in range(n_chunks):
            sl = slice(c * SAMPLE_CHUNK, (c + 1) * SAMPLE_CHUNK)
            d = jnp.dot(qi, kibuf[slot, r, :, sl].astype(BF16), preferred_element_type=F32)
            key_scr[qrows, sl] = head_sum(r, d)
        d_new = lax.dot_general(qi, new_rows(kin_ref, r), NT_DIMS, preferred_element_type=F32)
        key_scr[qrows, past:lc] = jnp.where(adm_new, head_sum(r, d_new), -jnp.inf)

    n_q = per_step * t_new
    idx = lax.broadcasted_iota(I32, (n_q, lc), 1)
    trow = lax.broadcasted_iota(I32, (n_q, lc), 0) % t_new
    n_adm = past + 1 + lax.broadcasted_iota(I32, (n_q, 1), 0) % t_new
    bias_scr[...] = _topk_bias(key_scr, j_scr, idx - past <= trow, n_adm, lc, top_k)

    for r in range(per_step):
        q = q_ref[r]

        def bias_rows(sl, r=r):
            return jnp.concatenate([bias_scr[r * t_new:(r + 1) * t_new, sl]] * N_HEADS, axis=0)

        m = jnp.full((N_HEADS * t_new, 1), -jnp.inf, F32)
        for c in range(n_chunks):
            sl = slice(c * SAMPLE_CHUNK, (c + 1) * SAMPLE_CHUNK)
            lg = jnp.dot(q, kbuf[slot, r, :, sl].astype(BF16), preferred_element_type=F32) + bias_rows(sl)
            lg_scr[:, sl] = lg
            m = jnp.maximum(m, jnp.max(lg, axis=1, keepdims=True))
        lg_new = (lax.dot_general(q, new_rows(kn_ref, r), NT_DIMS, preferred_element_type=F32)
                  + bias_rows(slice(past, lc)))
        m = jnp.maximum(m, jnp.max(lg_new, axis=1, keepdims=True))

        p_new = jnp.exp2(lg_new - m)
        l = jnp.sum(p_new, axis=1, keepdims=True)
        o = jnp.dot(p_new.astype(BF16), new_rows(vn_ref, r), preferred_element_type=F32)
        for c in range(n_chunks):
            sl = slice(c * SAMPLE_CHUNK, (c + 1) * SAMPLE_CHUNK)
            pr = jnp.exp2(lg_scr[:, sl] - m)
            l = l + jnp.sum(pr, axis=1, keepdims=True)
            o = o + lax.dot_general(pr.astype(BF16), vbuf[slot, r, :, sl].astype(BF16), NT_DIMS,
                                    preferred_element_type=F32)
        o_ref[r] = o / l


def _attn_sample(page_table, q_hq, qi_hq, wi_hq, k_new, v_new, ki_new, cache_kt, cache_vt, cache_kit):
    db, n_pages = page_table.shape
    page = cache_kt.shape[2]
    t_new = k_new.shape[1]
    past = n_pages * page
    lc = past + page
    top_k = min(TOP_K_MAX, (past + t_new) // 4)
    rows = SAMPLE_ROWS_PER_STEP
    per_b = lambda r, w: pl.BlockSpec((rows, r, w), lambda b, pt: (b, 0, 0))
    hbm = pl.BlockSpec(memory_space=pl.ANY)
    kern = functools.partial(_attn_sample_kernel, n_pages=n_pages, page=page, t_new=t_new, top_k=top_k)
    slab = pltpu.VMEM((2, rows, HEAD_DIM, past), F32)
    grid_spec = pltpu.PrefetchScalarGridSpec(
        num_scalar_prefetch=1,
        grid=(db // rows,),
        in_specs=[per_b(N_HEADS * t_new, HEAD_DIM), per_b(IDX_HEADS * t_new, IDX_DIM), per_b(IDX_HEADS * t_new, 1),
                  per_b(t_new, HEAD_DIM), per_b(t_new, HEAD_DIM), per_b(t_new, IDX_DIM),
                  hbm, hbm, hbm],
        out_specs=per_b(N_HEADS * t_new, HEAD_DIM),
        scratch_shapes=[slab, slab, slab, pltpu.SemaphoreType.DMA((3, 2)),
                        pltpu.VMEM((rows * t_new, lc), F32), pltpu.VMEM((rows * t_new, lc), F32),
                        pltpu.VMEM((N_HEADS * t_new, past), F32), pltpu.VMEM((rows * t_new, 1), I32)],
    )
    return pl.pallas_call(
        kern,
        grid_spec=grid_spec,
        out_shape=jax.ShapeDtypeStruct((db, N_HEADS * t_new, HEAD_DIM), F32),
        compiler_params=_params(("arbitrary",)),
        name="attn_sample",
    )(page_table.reshape(-1), q_hq, qi_hq, wi_hq, k_new, v_new, ki_new, cache_kt, cache_vt, cache_kit)


PREV_ROWS = 16


def _pool_kernel(prev_ref, u_ref, wg_ref, sc_ref, o_ref, ext_ref, *, pos0):
    per_step, t_len, _ = u_ref.shape
    pos = pos0 + lax.broadcasted_iota(I32, (t_len, 1), 0)
    for b in range(per_step):
        ext_ref[0:PREV_ROWS, :] = prev_ref[b]
        ext_ref[PREV_ROWS:PREV_ROWS + t_len, :] = u_ref[b]
        for g, w in enumerate(POOL_WINDOWS):
            sl = slice(g * POOL_GW, (g + 1) * POOL_GW)
            u_new = ext_ref[PREV_ROWS:PREV_ROWS + t_len, sl]
            win = u_new
            for back in range(1, w):
                win = win + ext_ref[PREV_ROWS - back:PREV_ROWS - back + t_len, sl]
            count = jnp.minimum(pos + 1, w).astype(F32)
            r = win / count - u_new
            mixed = jnp.dot(r.astype(BF16), wg_ref[g], preferred_element_type=F32) * sc_ref[:, sl]
            o_ref[b, :, sl] = mixed.astype(BF16)


def _pool(prev, u, w_grp, scale, pos0, per_step):
    nb, t_len, _ = u.shape
    seqs = lambda rows: pl.BlockSpec((per_step, rows, POOL_WIDTH), lambda b: (b, 0, 0))
    return pl.pallas_call(
        functools.partial(_pool_kernel, pos0=pos0),
        grid=(nb // per_step,),
        in_specs=[seqs(PREV_ROWS), seqs(t_len),
                  pl.BlockSpec((POOL_GROUPS, POOL_GW, POOL_GW), lambda b: (0, 0, 0)),
                  pl.BlockSpec((1, POOL_WIDTH), lambda b: (0, 0))],
        out_specs=seqs(t_len),
        out_shape=jax.ShapeDtypeStruct((nb, t_len, POOL_WIDTH), BF16),
        scratch_shapes=[pltpu.VMEM((PREV_ROWS + t_len, POOL_WIDTH), F32)],
        compiler_params=_params(("parallel",)),
        name="pool",
    )(prev, u, w_grp, scale)


def _merge_kernel(x_ref, a_ref, p_ref, wga_ref, wgb_ref, wao_ref, wpo_ref, wo_ref, g_ref, b_ref, h_ref, hp_ref, *,
                  alpha):
    x = x_ref[...]
    xb = x.astype(BF16)
    ga = jnp.dot(xb, wga_ref[...], preferred_element_type=F32)
    gb = jnp.dot(xb, wgb_ref[...], preferred_element_type=F32)
    ya = jnp.dot(a_ref[...], wao_ref[...], preferred_element_type=F32)
    yp = jnp.dot(p_ref[...], wpo_ref[...], preferred_element_type=F32)
    mix = jax.nn.sigmoid(ga) * ya + jax.nn.sigmoid(gb) * yp
    out = jnp.dot(mix.astype(BF16), wo_ref[...], preferred_element_type=F32)
    h = _layer_norm(alpha * x + out, g_ref[...], b_ref[...])
    h_ref[...] = h
    hp_ref[...] = _pack_rows(h)


def _merge(x, attn, pool, wga, wgb, wao, wpo, wo, g, b, tm, alpha):
    n = x.shape[0]
    row = lambda w: pl.BlockSpec((tm, w), lambda i: (i, 0))
    full = lambda r, c: pl.BlockSpec((r, c), lambda i: (0, 0), pipeline_mode=pl.Buffered(1))
    return pl.pallas_call(
        functools.partial(_merge_kernel, alpha=alpha),
        grid=(n // tm,),
        in_specs=[row(D_MODEL), row(ATT_WIDTH), row(POOL_WIDTH), full(D_MODEL, D_MODEL), full(D_MODEL, D_MODEL),
                  full(ATT_WIDTH, D_MODEL), full(POOL_WIDTH, D_MODEL), full(D_MODEL, D_MODEL),
                  full(1, D_MODEL), full(1, D_MODEL)],
        out_specs=(row(D_MODEL), row(PACKED)),
        out_shape=(jax.ShapeDtypeStruct((n, D_MODEL), F32), jax.ShapeDtypeStruct((n, PACKED), I32)),
        compiler_params=_params(("parallel",)),
        name="merge",
    )(x, attn, pool, wga, wgb, wao, wpo, wo, g, b)


def _route(h, wr_t, bias_col):
    tm = h.shape[0]
    logits = lax.dot_general(wr_t, h.astype(BF16), NT_DIMS, preferred_element_type=F32)
    s = jax.nn.sigmoid(logits)
    sb = s + bias_col
    neg_inf = -jnp.inf

    rows = []
    for g in range(N_GROUPS):
        blk = sb[g * GROUP_SIZE:(g + 1) * GROUP_SIZE, :]
        m1 = jnp.max(blk, axis=0, keepdims=True)
        is_m1 = blk == m1
        n_m1 = jnp.sum(is_m1.astype(F32), axis=0, keepdims=True)
        m2 = jnp.max(jnp.where(is_m1, neg_inf, blk), axis=0, keepdims=True)
        rows.append(m1 + jnp.where(n_m1 >= 2.0, m1, m2))
    gs = jnp.concatenate(rows, axis=0)

    gi = lax.broadcasted_iota(I32, (N_GROUPS, tm), 0)
    rank = jnp.zeros((N_GROUPS, tm), F32)
    for g in range(N_GROUPS):
        row = gs[g:g + 1, :]
        beats = jnp.logical_or(row > gs, jnp.logical_and(row == gs, g < gi))
        rank = rank + beats.astype(F32)
    gkeep = rank < float(TOPK_GROUPS)
    emask = jnp.concatenate(
        [jnp.broadcast_to(gkeep[g:g + 1, :], (GROUP_SIZE, tm)) for g in range(N_GROUPS)], axis=0)

    ei = lax.broadcasted_iota(I32, (N_EXPERTS, tm), 0)
    x = jnp.where(emask, sb, neg_inf)
    sel = jnp.zeros((N_EXPERTS, tm), jnp.bool_)
    picks = []
    for _ in range(TOP_K_EXPERTS):
        m = jnp.max(x, axis=0, keepdims=True)
        first = jnp.min(jnp.where(x == m, ei, N_EXPERTS), axis=0, keepdims=True)
        pick = ei == first
        sel = jnp.logical_or(sel, pick)
        x = jnp.where(pick, neg_inf, x)
        picks.append(first)

    gate = jnp.where(sel, s, 0.0)
    comb = gate / jnp.sum(gate, axis=0, keepdims=True) * ROUTED_SCALE
    return comb, sel, picks


def _router_kernel(h_ref, wr_ref, bias_ref, c_ref):
    comb, _, _ = _route(h_ref[...], wr_ref[...], bias_ref[...])
    comb = jnp.concatenate([comb, jnp.zeros((LANES - N_EXPERTS, comb.shape[1]), F32)], axis=0)
    c_ref[...] = comb.T


def _router(h, wr_t, bias_col, tm):
    n = h.shape[0]
    return pl.pallas_call(
        _router_kernel,
        grid=(n // tm,),
        in_specs=[pl.BlockSpec((tm, D_MODEL), lambda i: (i, 0)),
                  pl.BlockSpec((N_EXPERTS, D_MODEL), lambda i: (0, 0)),
                  pl.BlockSpec((N_EXPERTS, 1), lambda i: (0, 0))],
        out_specs=pl.BlockSpec((tm, LANES), lambda i: (i, 0)),
        out_shape=jax.ShapeDtypeStruct((n, LANES), F32),
        compiler_params=_params(("parallel",)),
        name="router",
    )(h, wr_t, bias_col)


def _swiglu(xb, w13, w2, hidden):
    ab = jnp.dot(xb, w13, preferred_element_type=F32)
    act = jax.nn.silu(ab[:, 0:hidden]) * ab[:, hidden:2 * hidden]
    return jnp.dot(act.astype(BF16), w2, preferred_element_type=F32)


def _moe_kernel(h_ref, c_ref, ws13_ref, ws2_ref, w13_ref, w2_ref, y_ref, hb_ref):
    e = pl.program_id(1)

    @pl.when(e == 0)
    def _():
        hb_ref[...] = h_ref[...].astype(BF16)
        y_ref[...] = _swiglu(hb_ref[...], ws13_ref[...], ws2_ref[...], SHARED_DIM)

    ye = _swiglu(hb_ref[...], w13_ref[...].astype(BF16), w2_ref[...].astype(BF16), EXPERT_DIM)
    lane = lax.broadcasted_iota(I32, c_ref.shape, 1)
    ce = jnp.sum(jnp.where(lane == e, c_ref[...], 0.0), axis=1, keepdims=True)
    y_ref[...] += ce * ye


def _moe(h, comb, ws13, ws2, w13, w2, tm):
    n = h.shape[0]
    return pl.pallas_call(
        _moe_kernel,
        grid=(n // tm, N_EXPERTS),
        in_specs=[pl.BlockSpec((tm, D_MODEL), lambda i, e: (i, 0)),
                  pl.BlockSpec((tm, LANES), lambda i, e: (i, 0)),
                  pl.BlockSpec((D_MODEL, 2 * SHARED_DIM), lambda i, e: (0, 0)),
                  pl.BlockSpec((SHARED_DIM, D_MODEL), lambda i, e: (0, 0)),
                  pl.BlockSpec((None, D_MODEL, 2 * EXPERT_DIM), lambda i, e: (e, 0, 0)),
                  pl.BlockSpec((None, EXPERT_DIM, D_MODEL), lambda i, e: (e, 0, 0))],
        out_specs=pl.BlockSpec((tm, D_MODEL), lambda i, e: (i, 0)),
        out_shape=jax.ShapeDtypeStruct((n, D_MODEL), F32),
        scratch_shapes=[pltpu.VMEM((tm, D_MODEL), BF16)],
        compiler_params=_params(("parallel", "arbitrary")),
        name="moe",
    )(h, comb, ws13, ws2, w13, w2)


def _final_kernel(h_ref, y_ref, pe_ref, g_ref, b_ref, wpg_ref, wpi_ref, o_ref, *, alpha):
    z = _layer_norm(alpha * h_ref[...] + y_ref[...], g_ref[...], b_ref[...])
    gate = jax.nn.sigmoid(jnp.dot(z.astype(BF16), wpg_ref[...], preferred_element_type=F32))
    emb = jnp.dot(pe_ref[...].astype(BF16), wpi_ref[...], preferred_element_type=F32)
    o_ref[...] = z + gate * emb


def _final(h, y, pe, g, b, wpg, wpi, tm, alpha):
    n = h.shape[0]
    row = lambda w: pl.BlockSpec((tm, w), lambda i: (i, 0))
    full = lambda r, c: pl.BlockSpec((r, c), lambda i: (0, 0))
    return pl.pallas_call(
        functools.partial(_final_kernel, alpha=alpha),
        grid=(n // tm,),
        in_specs=[row(D_MODEL), row(D_MODEL), row(PLE_DIM), full(1, D_MODEL), full(1, D_MODEL),
                  full(D_MODEL, D_MODEL), full(PLE_DIM, D_MODEL)],
        out_specs=row(D_MODEL),
        out_shape=jax.ShapeDtypeStruct((n, D_MODEL), F32),
        compiler_params=_params(("parallel",)),
        name="final",
    )(h, y, pe, g, b, wpg, wpi)


MOE_BLOCK = 2176


def _sorted_rows(n_tokens):
    worst = n_tokens * TOP_K_EXPERTS + N_EXPERTS * MOE_BLOCK
    return -(-worst // MOE_BLOCK) * MOE_BLOCK


def _dispatch_kernel(h_ref, wr_ref, bias_ref, tri_ref, pos_ref, gate_ref, blk_ref, used_ref,
                     eidx_s, rank_s, gate_s, cnt_s):
    p = pl.program_id(0)
    i = pl.program_id(1)
    tm = h_ref.shape[0]
    ei = lax.broadcasted_iota(I32, (N_EXPERTS, tm), 0)

    @pl.when(p == 0)
    def _():
        comb, sel, picks = _route(h_ref[...], wr_ref[...], bias_ref[...])
        before = jnp.dot(sel.astype(BF16), tri_ref[...], preferred_element_type=F32)
        ranks, gates = [], []
        for first in picks:
            pick = ei == first
            ranks.append(jnp.sum(jnp.where(pick, before, 0.0), axis=0, keepdims=True))
            gates.append(jnp.sum(jnp.where(pick, comb, 0.0), axis=0, keepdims=True))
        eidx_s[i] = jnp.concatenate(picks, axis=0)
        rank_s[i] = jnp.concatenate(ranks, axis=0)
        gate_s[i] = jnp.concatenate(gates, axis=0)
        cnt_s[i] = jnp.broadcast_to(jnp.sum(sel.astype(F32), axis=1, keepdims=True), (N_EXPERTS, LANES))

    @pl.when(p == 1)
    def _():
        cnt = cnt_s[...]
        tile_id = lax.broadcasted_iota(I32, cnt.shape, 0)
        total = jnp.sum(cnt, axis=0)
        prior = jnp.sum(jnp.where(tile_id < i, cnt, 0.0), axis=0)
        seg_blk = jnp.maximum(jnp.ceil(total * (1.0 / MOE_BLOCK) - 0.25 / MOE_BLOCK), 1.0)
        lower = (lax.broadcasted_iota(I32, (N_EXPERTS, N_EXPERTS), 1)
                 < lax.broadcasted_iota(I32, (N_EXPERTS, N_EXPERTS), 0)).astype(F32)
        off_blk = jnp.dot(lower, seg_blk, precision=lax.Precision.HIGHEST, preferred_element_type=F32)
        seg_off = off_blk * MOE_BLOCK
        base = (seg_off + prior)[:, 0:1]
        eidx = eidx_s[i]
        rank = rank_s[i]
        rows = []
        for k in range(TOP_K_EXPERTS):
            pick = ei == eidx[k:k + 1, :]
            rows.append(rank[k:k + 1, :] + jnp.sum(jnp.where(pick, base, 0.0), axis=0, keepdims=True))
        pos_ref[...] = jnp.concatenate(rows, axis=0).astype(I32)
        gate_ref[...] = jnp.concatenate([gate_s[i], jnp.zeros((LANES - TOP_K_EXPERTS, tm), F32)], axis=0).T

        end_blk = (off_blk + seg_blk)[:, 0:1]
        n_blk = blk_ref.shape[1]
        blk_id = lax.broadcasted_iota(I32, (N_EXPERTS, n_blk), 1).astype(F32)
        owner = jnp.sum((end_blk <= blk_id).astype(F32), axis=0, keepdims=True)
        blk_ref[...] = jnp.minimum(owner, N_EXPERTS - 1.0).astype(I32)
        used_ref[...] = jnp.broadcast_to(end_blk[N_EXPERTS - 1:N_EXPERTS, :], used_ref.shape).astype(I32)


def _dispatch(h, wr_t, bias_col, tm):
    n = h.shape[0]
    n_tiles = n // tm
    n_blk = _sorted_rows(n) // MOE_BLOCK
    n_blk_pad = -(-n_blk // LANES) * LANES
    tri = jnp.triu(jnp.ones((tm, tm), BF16), k=1)
    const = lambda r, c: pl.BlockSpec((r, c), lambda p, i: (0, 0))
    per_tile = lambda dt: pltpu.VMEM((n_tiles, TOP_K_EXPERTS, tm), dt)
    return pl.pallas_call(
        _dispatch_kernel,
        grid=(2, n_tiles),
        in_specs=[pl.BlockSpec((tm, D_MODEL), lambda p, i: (i * (1 - p), 0)),
                  const(N_EXPERTS, D_MODEL), const(N_EXPERTS, 1), const(tm, tm)],
        out_specs=(pl.BlockSpec((TOP_K_EXPERTS, tm), lambda p, i: (0, i * p)),
                   pl.BlockSpec((tm, LANES), lambda p, i: (i * p, 0)),
                   const(1, n_blk_pad), const(1, LANES)),
        out_shape=(jax.ShapeDtypeStruct((TOP_K_EXPERTS, n), I32), jax.ShapeDtypeStruct((n, LANES), F32),
                   jax.ShapeDtypeStruct((1, n_blk_pad), I32), jax.ShapeDtypeStruct((1, LANES), I32)),
        scratch_shapes=[per_tile(I32), per_tile(F32), per_tile(F32), pltpu.VMEM((n_tiles, N_EXPERTS, LANES), F32)],
        compiler_params=_params(("arbitrary", "arbitrary")),
        name="dispatch",
    )(h, wr_t, bias_col, tri)


PACKED = D_MODEL // 2


def _pack_rows(x):
    lo = pltpu.bitcast(x[:, 0:PACKED].astype(BF16).astype(F32), I32)
    hi = pltpu.bitcast(x[:, PACKED:D_MODEL].astype(BF16).astype(F32), I32)
    return jnp.bitwise_or(hi, lax.shift_right_logical(lo, 16))


def _unpack_rows_f32(w):
    lo = pltpu.bitcast(lax.shift_left(w, 16), F32)
    hi = pltpu.bitcast(jnp.bitwise_and(w, -65536), F32)
    return jnp.concatenate([lo, hi], axis=1)


def _unpack_rows(w):
    return _unpack_rows_f32(w).astype(BF16)


def _grouped_kernel(blk_ref, used_ref, anchor_ref, xs_ref, w13_ref, w2_ref, ys_ref, w13b_ref, w2b_ref):
    @pl.when(pl.program_id(0) < used_ref[0])
    def _():
        w13b_ref[...] = w13_ref[...].astype(BF16)
        w2b_ref[...] = w2_ref[...].astype(BF16)
        ys = _swiglu(_unpack_rows(xs_ref[...]), w13b_ref[...], w2b_ref[...], EXPERT_DIM)
        ys_ref[...] = _pack_rows(ys)


def _grouped(blk, used, anchor, xs, w13, w2):
    ns = xs.shape[0]
    row_blk = lambda b, blk, used, anchor: (jnp.minimum(b, used[0] - 1), 0)
    expert = lambda b, blk, used, anchor: (blk[b], 0, 0)
    grid_spec = pltpu.PrefetchScalarGridSpec(
        num_scalar_prefetch=3,
        grid=(ns // MOE_BLOCK,),
        in_specs=[pl.BlockSpec((MOE_BLOCK, PACKED), row_blk),
                  pl.BlockSpec((None, D_MODEL, 2 * EXPERT_DIM), expert),
                  pl.BlockSpec((None, EXPERT_DIM, D_MODEL), expert)],
        out_specs=(pl.BlockSpec((MOE_BLOCK, PACKED), row_blk),
                   pl.BlockSpec((None, D_MODEL, 2 * EXPERT_DIM), expert),
                   pl.BlockSpec((None, EXPERT_DIM, D_MODEL), expert)),
    )
    return pl.pallas_call(
        _grouped_kernel,
        grid_spec=grid_spec,
        out_shape=(jax.ShapeDtypeStruct((ns, PACKED), I32), jax.ShapeDtypeStruct(w13.shape, BF16),
                   jax.ShapeDtypeStruct(w2.shape, BF16)),
        compiler_params=_params(("arbitrary",)),
        name="grouped",
    )(blk, used, anchor, xs, w13, w2)


SC_WINDOW = 128


def _sc_mesh():
    return plsc.VectorSubcoreMesh(core_axis_name="core", subcore_axis_name="subcore")


def _sc_worker(n_items):
    info = plsc.get_sparse_core_info()
    n_workers = info.num_cores * info.num_subcores
    assert n_items % (SC_WINDOW * n_workers) == 0, "rows must split evenly into windows over the vector subcores"
    wid = lax.axis_index("subcore") * info.num_cores + lax.axis_index("core")
    return wid, n_items // (SC_WINDOW * n_workers)


def _scatter_rows(x, pos, n_out):
    n, width = x.shape
    picks = pos.shape[0]

    @functools.partial(
        pl.kernel, mesh=_sc_mesh(), out_type=jax.ShapeDtypeStruct((n_out, width), I32),
        scratch_types=[pltpu.VMEM((picks, SC_WINDOW), I32), pltpu.VMEM((SC_WINDOW, width), I32)],
        name="scatter_rows")
    def scatter(x_hbm, pos_hbm, out_hbm, idx_v, rows_v):
        wid, n_win = _sc_worker(n)

        @pl.loop(0, n_win)
        def _(j):
            base = (wid * n_win + j) * SC_WINDOW
            pltpu.sync_copy(pos_hbm.at[:, pl.ds(base, SC_WINDOW)], idx_v)
            pltpu.sync_copy(x_hbm.at[pl.ds(base, SC_WINDOW)], rows_v)
            for k in range(picks):
                pltpu.sync_copy(rows_v, out_hbm.at[idx_v.at[k]])

    return scatter(x, pos)


def _gather_rows(src, pos):
    width = src.shape[1]
    picks, n = pos.shape

    @functools.partial(
        pl.kernel, mesh=_sc_mesh(), out_type=jax.ShapeDtypeStruct((picks * n, width), I32),
        scratch_types=[pltpu.VMEM((SC_WINDOW,), I32), pltpu.VMEM((SC_WINDOW, width), I32)],
        name="gather_rows")
    def gather(src_hbm, pos_hbm, out_hbm, idx_v, rows_v):
        wid, n_win = _sc_worker(picks * n)

        @pl.loop(0, n_win)
        def _(j):
            base = (wid * n_win + j) * SC_WINDOW
            pltpu.sync_copy(pos_hbm.at[pl.ds(base, SC_WINDOW)], idx_v)
            pltpu.sync_copy(src_hbm.at[idx_v], rows_v)
            pltpu.sync_copy(rows_v, out_hbm.at[pl.ds(base, SC_WINDOW)])

    return gather(src, pos.reshape(-1)).reshape(picks, n, width)


def _combine_kernel(h_ref, g_ref, gate_ref, pe_ref, ws13_ref, ws2_ref, ln_g_ref, ln_b_ref, wpg_ref, wpi_ref, o_ref, *,
                    alpha):
    h = h_ref[...]
    y = _swiglu(h.astype(BF16), ws13_ref[...], ws2_ref[...], SHARED_DIM)
    gate = gate_ref[...]
    for k in range(TOP_K_EXPERTS):
        y = y + gate[:, k:k + 1] * _unpack_rows_f32(g_ref[k])
    z = _layer_norm(alpha * h + y, ln_g_ref[...], ln_b_ref[...])
    ple_gate = jax.nn.sigmoid(jnp.dot(z.astype(BF16), wpg_ref[...], preferred_element_type=F32))
    emb = jnp.dot(pe_ref[...].astype(BF16), wpi_ref[...], preferred_element_type=F32)
    o_ref[...] = z + ple_gate * emb


def _combine(h, gathered, gate, pe, ws13, ws2, g, b, wpg, wpi, tm, alpha):
    n = h.shape[0]
    row = lambda w: pl.BlockSpec((tm, w), lambda i: (i, 0))
    full = lambda r, c: pl.BlockSpec((r, c), lambda i: (0, 0))
    return pl.pallas_call(
        functools.partial(_combine_kernel, alpha=alpha),
        grid=(n // tm,),
        in_specs=[row(D_MODEL), pl.BlockSpec((TOP_K_EXPERTS, tm, PACKED), lambda i: (0, i, 0)), row(LANES),
                  row(PLE_DIM), full(D_MODEL, 2 * SHARED_DIM), full(SHARED_DIM, D_MODEL),
                  full(1, D_MODEL), full(1, D_MODEL), full(D_MODEL, D_MODEL), full(PLE_DIM, D_MODEL)],
        out_specs=row(D_MODEL),
        out_shape=jax.ShapeDtypeStruct((n, D_MODEL), F32),
        compiler_params=_params(("parallel",)),
        name="combine",
    )(h, gathered, gate, pe, ws13, ws2, g, b, wpg, wpi)


def _rope_table(pos):
    inv = ROPE_THETA ** (-jnp.arange(0, HEAD_DIM, 2, dtype=F32) / HEAD_DIM)
    ang = pos.astype(F32)[:, None] * inv[None, :]
    return jnp.concatenate([jnp.tile(jnp.cos(ang), (1, 4)), jnp.tile(jnp.sin(ang), (1, 4))], axis=1)


def _fused_in_weight(w_in):
    offs = np.cumsum(IN_SIZES)[:-1].tolist()
    wq, wk, wv, wqi, wki, wwi, wu, wga, wgb = jnp.split(w_in, offs, axis=1)
    pad = jnp.zeros((D_MODEL, LANES - HEAD_DIM - IDX_HEADS), w_in.dtype)
    w_big = jnp.concatenate([wq, wqi, wk, wki, wv, wwi, pad, wu], axis=1).astype(BF16)
    return w_big, w_big[:, 0:C_U].T, wga.astype(BF16), wgb.astype(BF16)


def _pages_transposed(cache):
    return jnp.transpose(cache[0], (0, 2, 1))


def _heads_major(a, n_heads):
    b, t, w = a.shape
    d = w // n_heads
    return a.reshape(b, t, n_heads, d).transpose(0, 2, 1, 3).reshape(b, n_heads * t, d)


def kernel(x_prompt, x_sample, cache_k, cache_v, cache_kidx, state_pool, page_table, p_prompt, p_sample, w_in, w_att_out, w_pool_grp, pool_scale, w_pool_out, w_out, ln1_g, ln1_b, w_router, router_bias, w_exp13, w_exp2, w_sh13, w_sh2, ln2_g, ln2_b, w_ple_in, w_ple_gate):
    B, S, D = x_prompt.shape
    DB, T, _ = x_sample.shape
    depth = w_in.shape[0]
    assert depth == 1, "single layer step"
    page = cache_k.shape[2]
    past = page_table.shape[1] * page
    alpha = (2 * depth) ** 0.25
    n_p, n_s = B * S, DB * T

    w_big, w_t, wga, wgb = _fused_in_weight(w_in[0])
    wao, wpo, wo = w_att_out[0].astype(BF16), w_pool_out[0].astype(BF16), w_out[0].astype(BF16)
    wgrp = w_pool_grp[0].astype(BF16)
    pscale = pool_scale[0].reshape(1, POOL_WIDTH)
    g1, b1 = ln1_g[0].reshape(1, D), ln1_b[0].reshape(1, D)
    g2, b2 = ln2_g[0].reshape(1, D), ln2_b[0].reshape(1, D)
    wr_t = w_router[0].T.astype(BF16)
    rbias = router_bias[0].reshape(N_EXPERTS, 1)
    w13, w2 = w_exp13[0], w_exp2[0]
    ws13, ws2 = w_sh13[0].astype(BF16), w_sh2[0].astype(BF16)
    wpg, wpi = w_ple_gate[0].astype(BF16), w_ple_in[0].astype(BF16)

    cs_p = _rope_table(jnp.arange(S, dtype=I32))
    cs_s = jnp.tile(_rope_table(past + jnp.arange(T, dtype=I32)), (DB, 1))

    xp = x_prompt.reshape(n_p, D)
    qt, qit, wit, kb, kib, vbt, kt, vt, kit, u = _proj_prompt(xp, w_big, w_t, cs_p, S, PROJ_TILE)
    attn_p = _attn_prompt(qt, qit, wit, kb, kib, vbt)
    u3 = u.reshape(B, S, POOL_WIDTH)
    pool_p = _pool(jnp.zeros((B, PREV_ROWS, POOL_WIDTH), F32), u3, wgrp, pscale, 0, 1).reshape(n_p, POOL_WIDTH)
    h_p, hp_p = _merge(xp, attn_p, pool_p, wga, wgb, wao, wpo, wo, g1, b1, MERGE_TILE, alpha)

    xs = x_sample.reshape(n_s, D)
    qs, qis, ks, vs, kis, wis, us = _proj_sample(xs, w_big, cs_s)
    q_hq = _heads_major(qs.reshape(DB, T, ATT_WIDTH), N_HEADS)
    qi_hq = _heads_major(qis.reshape(DB, T, IDX_HEADS * IDX_DIM), IDX_HEADS)
    wi_hq = wis.reshape(DB, T, IDX_HEADS).transpose(0, 2, 1).reshape(DB, IDX_HEADS * T, 1)
    caches = (_pages_transposed(cache_k), _pages_transposed(cache_v), _pages_transposed(cache_kidx))
    new_rows = (ks.reshape(DB, T, HEAD_DIM), vs.reshape(DB, T, HEAD_DIM), kis.reshape(DB, T, IDX_DIM))
    half = DB * 5 // 8 // SAMPLE_ROWS_PER_STEP * SAMPLE_ROWS_PER_STEP
    o_halves = [_attn_sample(page_table[sl], q_hq[sl], qi_hq[sl], wi_hq[sl], *(a[sl] for a in new_rows), *caches)
                for sl in (slice(0, half), slice(half, DB))]
    o_hq = jnp.concatenate(o_halves, axis=0)
    attn_s = o_hq.reshape(DB, N_HEADS, T, HEAD_DIM).transpose(0, 2, 1, 3).reshape(n_s, ATT_WIDTH).astype(BF16)
    us3 = us.reshape(DB, T, POOL_WIDTH)
    prev_s = jnp.concatenate([jnp.zeros((DB, PREV_ROWS - POOL_STATE, POOL_WIDTH), F32), state_pool[0]], axis=1)
    pool_s = _pool(prev_s, us3, wgrp, pscale, past, DB).reshape(n_s, POOL_WIDTH)
    h_s, _ = _merge(xs, attn_s, pool_s, wga, wgb, wao, wpo, wo, g1, b1, n_s, alpha)

    pos, gate, blk, used = _dispatch(h_p, wr_t, rbias, DISPATCH_TILE)
    sorted_in = _scatter_rows(hp_p, pos, _sorted_rows(n_p))
    anchor = lax.bitcast_convert_type(o_halves[0][0, 0, 0:1], I32)
    sorted_out, w13_b, w2_b = _grouped(blk.reshape(-1), used.reshape(-1), anchor, sorted_in, w13, w2)
    gathered = _gather_rows(sorted_out, pos)

    comb_s = _router(h_s, wr_t, rbias, n_s)
    y_s = _final(h_s, _moe(h_s, comb_s, ws13, ws2, w13_b, w2_b, n_s), p_sample[0].reshape(n_s, PLE_DIM), g2, b2, wpg,
                 wpi, n_s, alpha)
    y_p = _combine(h_p, gathered, gate, p_prompt[0].reshape(n_p, PLE_DIM), ws13, ws2, g2, b2, wpg, wpi, COMBINE_TILE,
                   alpha)

    ext_s = jnp.concatenate([state_pool[0], us3], axis=1)
    return (y_p.reshape(B, S, D), y_s.reshape(DB, T, D),
            jnp.transpose(kt, (0, 2, 1))[None], jnp.transpose(vt, (0, 2, 1))[None],
            jnp.transpose(kit, (0, 2, 1))[None],
            u3[:, S - POOL_STATE:][None],
            ks.reshape(1, DB, T, HEAD_DIM), vs.reshape(1, DB, T, HEAD_DIM), kis.reshape(1, DB, T, IDX_DIM),
            ext_s[:, T:][None])
```

```python
import functools

import numpy as np
import jax
import jax.numpy as jnp
from jax import lax
from jax.experimental import pallas as pl
from jax.experimental.pallas import tpu as pltpu
from jax.experimental.pallas import tpu_sc as plsc

F32 = jnp.float32
BF16 = jnp.bfloat16
I32 = jnp.int32

D_MODEL = 1024
N_HEADS = 8
HEAD_DIM = 64
ATT_WIDTH = N_HEADS * HEAD_DIM
IDX_HEADS = 4
IDX_DIM = 64
TOP_K_MAX = 256
Q_BLOCK = 256
ROPE_THETA = 10000.0
POOL_WINDOWS = (2, 4, 8, 16)
POOL_GROUPS = 4
POOL_WIDTH = 512
POOL_GW = POOL_WIDTH // POOL_GROUPS
POOL_STATE = 15
N_EXPERTS = 64
TOP_K_EXPERTS = 8
N_GROUPS = 8
GROUP_SIZE = N_EXPERTS // N_GROUPS
TOPK_GROUPS = 4
EXPERT_DIM = 256
SHARED_DIM = 256
ROUTED_SCALE = 2.5
PLE_DIM = 256
LN_EPS = 1e-5
IN_SIZES = (ATT_WIDTH, HEAD_DIM, HEAD_DIM, IDX_HEADS * IDX_DIM, IDX_DIM, IDX_HEADS, POOL_WIDTH, D_MODEL, D_MODEL)

LANES = 128
SUBLANES = 8
INT_MIN = -2147483648
NEG_BIG = -1e30
VMEM_LIMIT = 56 * 1024 * 1024
PROJ_TILE = 512
MERGE_TILE = 1024
DISPATCH_TILE = 1024
COMBINE_TILE = 512

C_Q = 0
C_QI = 512
C_KK = 768
C_VW = 896
C_U = 1024
C_END = 1536
HALF = HEAD_DIM // 2

NT_DIMS = (((1,), (1,)), ((), ()))

Q_SCALE = HEAD_DIM ** -0.5 * float(np.log2(np.e))
QI_SCALE = IDX_DIM ** -0.5


def _params(sem):
    return pltpu.CompilerParams(dimension_semantics=sem, vmem_limit_bytes=VMEM_LIMIT)


def _layer_norm(x, g, b):
    mu = jnp.mean(x, axis=-1, keepdims=True)
    xc = x - mu
    var = jnp.mean(xc * xc, axis=-1, keepdims=True)
    return xc * lax.rsqrt(var + LN_EPS) * g + b


def _rope_rows(a, cos, sin):
    first_half = lax.broadcasted_iota(I32, (a.shape[0], LANES), 1) % HEAD_DIM < HALF
    out = []
    for s in range(a.shape[1] // LANES):
        x = a[:, s * LANES:(s + 1) * LANES]
        rot = jnp.where(first_half, -pltpu.roll(x, LANES - HALF, axis=1), pltpu.roll(x, HALF, axis=1))
        out.append(x * cos + rot * sin)
    return out[0] if len(out) == 1 else jnp.concatenate(out, axis=1)


def _proj_sample_kernel(x_ref, w_ref, cs_ref, q_ref, qi_ref, k_ref, v_ref, ki_ref, wi_ref, u_ref):
    xb = x_ref[...].astype(BF16)
    cos = cs_ref[:, 0:LANES]
    sin = cs_ref[:, LANES:2 * LANES]

    def mm(c0, n):
        return jnp.dot(xb, w_ref[:, c0:c0 + n], preferred_element_type=F32)

    def rope(c0, n):
        return _rope_rows(mm(c0, n), cos, sin)

    q_ref[...] = (rope(C_Q, ATT_WIDTH) * Q_SCALE).astype(BF16)
    qi_ref[...] = (rope(C_QI, IDX_HEADS * IDX_DIM) * QI_SCALE).astype(BF16)
    kk = rope(C_KK, LANES)
    k_ref[...] = kk[:, 0:HEAD_DIM]
    ki_ref[...] = kk[:, HEAD_DIM:2 * HEAD_DIM]
    vw = mm(C_VW, LANES)
    v_ref[...] = vw[:, 0:HEAD_DIM]
    wi_ref[...] = vw[:, HEAD_DIM:HEAD_DIM + IDX_HEADS] * (IDX_HEADS ** -0.5)
    u_ref[...] = mm(C_U, POOL_WIDTH)


def _proj_sample(x, w_big, cs):
    n = x.shape[0]
    full = lambda r, c: pl.BlockSpec((r, c), lambda i: (0, 0))
    widths = (ATT_WIDTH, IDX_HEADS * IDX_DIM, HEAD_DIM, HEAD_DIM, IDX_DIM, IDX_HEADS, POOL_WIDTH)
    dtypes = (BF16, BF16, F32, F32, F32, F32, F32)
    return pl.pallas_call(
        _proj_sample_kernel,
        grid=(1,),
        in_specs=[full(n, D_MODEL), full(D_MODEL, C_END), full(n, 2 * LANES)],
        out_specs=tuple(full(n, w) for w in widths),
        out_shape=tuple(jax.ShapeDtypeStruct((n, w), dt) for w, dt in zip(widths, dtypes)),
        compiler_params=_params(("arbitrary",)),
        name="proj_sample",
    )(x, w_big, cs)


def _proj_prompt_kernel(x_ref, w_ref, wt_ref, cs_ref, cst_ref, qt_ref, qit_ref, wit_ref, kb_ref, kib_ref, vbt_ref,
                        kt_ref, vt_ref, kit_ref, u_ref):
    xb = x_ref[...].astype(BF16)
    tm = xb.shape[0]
    cos = cs_ref[:, 0:LANES]
    sin = cs_ref[:, LANES:2 * LANES]
    cos_t = cst_ref[0:HEAD_DIM, :]
    sin_t = cst_ref[LANES:LANES + HEAD_DIM, :]

    def mm(c0, n):
        return jnp.dot(xb, w_ref[:, c0:c0 + n], preferred_element_type=F32)

    def mm_t(c0, n):
        return lax.dot_general(wt_ref[c0:c0 + n, :], xb, NT_DIMS, preferred_element_type=F32)

    def rope_t(c0, heads):
        a = mm_t(c0, heads * HEAD_DIM)
        parts = []
        for h in range(heads):
            x1 = a[h * HEAD_DIM:h * HEAD_DIM + HALF, :]
            x2 = a[h * HEAD_DIM + HALF:(h + 1) * HEAD_DIM, :]
            rot = jnp.concatenate([-x2, x1], axis=0)
            parts.append(a[h * HEAD_DIM:(h + 1) * HEAD_DIM, :] * cos_t + rot * sin_t)
        return parts[0] if heads == 1 else jnp.concatenate(parts, axis=0)

    kk = _rope_rows(mm(C_KK, LANES), cos, sin)
    kb_ref[...] = kk[:, 0:HEAD_DIM].astype(BF16)
    kib_ref[...] = kk[:, HEAD_DIM:2 * HEAD_DIM].astype(BF16)
    u_ref[...] = mm(C_U, POOL_WIDTH)

    qt = (rope_t(C_Q, N_HEADS) * Q_SCALE).astype(BF16)
    qit = (rope_t(C_QI, IDX_HEADS) * QI_SCALE).astype(BF16)
    for blk in range(tm // Q_BLOCK):
        cols = slice(blk * Q_BLOCK, (blk + 1) * Q_BLOCK)
        for h in range(N_HEADS):
            qt_ref[blk, :, h * Q_BLOCK:(h + 1) * Q_BLOCK] = qt[h * HEAD_DIM:(h + 1) * HEAD_DIM, cols]
        for h in range(IDX_HEADS):
            qit_ref[blk, :, h * Q_BLOCK:(h + 1) * Q_BLOCK] = qit[h * IDX_DIM:(h + 1) * IDX_DIM, cols]

    kkt = rope_t(C_KK, 2)
    kt_ref[...] = kkt[0:HEAD_DIM, :]
    kit_ref[...] = kkt[HEAD_DIM:2 * HEAD_DIM, :]
    vwt = mm_t(C_VW, LANES)
    vt_ref[...] = vwt[0:HEAD_DIM, :]
    vbt_ref[...] = vwt[0:HEAD_DIM, :].astype(BF16)
    wit_ref[...] = vwt[HEAD_DIM:HEAD_DIM + SUBLANES, :] * (IDX_HEADS ** -0.5)


def _proj_prompt(x, w_big, w_t, cs, seq, tm):
    n = x.shape[0]
    nb = seq // tm
    qb = tm // Q_BLOCK
    row = lambda w: pl.BlockSpec((tm, w), lambda i: (i, 0))
    col = lambda r: pl.BlockSpec((None, r, tm), lambda i: (i // nb, 0, i % nb))
    slab = lambda heads: pl.BlockSpec((qb, HEAD_DIM, heads * Q_BLOCK), lambda i: (i, 0, 0))
    pm = lambda r, dt: jax.ShapeDtypeStruct((n // seq, r, seq), dt)
    out_shape = (
        jax.ShapeDtypeStruct((n // Q_BLOCK, HEAD_DIM, N_HEADS * Q_BLOCK), BF16),
        jax.ShapeDtypeStruct((n // Q_BLOCK, IDX_DIM, IDX_HEADS * Q_BLOCK), BF16),
        pm(SUBLANES, F32),
        jax.ShapeDtypeStruct((n, HEAD_DIM), BF16), jax.ShapeDtypeStruct((n, IDX_DIM), BF16),
        pm(HEAD_DIM, BF16),
        pm(HEAD_DIM, F32), pm(HEAD_DIM, F32), pm(IDX_DIM, F32),
        jax.ShapeDtypeStruct((n, POOL_WIDTH), F32),
    )
    return pl.pallas_call(
        _proj_prompt_kernel,
        grid=(n // tm,),
        in_specs=[
            row(D_MODEL),
            pl.BlockSpec((D_MODEL, C_END), lambda i: (0, 0)),
            pl.BlockSpec((C_U, D_MODEL), lambda i: (0, 0)),
            pl.BlockSpec((tm, 2 * LANES), lambda i: (i % nb, 0)),
            pl.BlockSpec((2 * LANES, tm), lambda i: (0, i % nb)),
        ],
        out_specs=(slab(N_HEADS), slab(IDX_HEADS), col(SUBLANES), row(HEAD_DIM), row(IDX_DIM), col(HEAD_DIM),
                   col(HEAD_DIM), col(HEAD_DIM), col(IDX_DIM), row(POOL_WIDTH)),
        out_shape=out_shape,
        compiler_params=_params(("parallel",)),
        name="proj_prompt",
    )(x, w_big, w_t, cs, cs.T)


def _float_of_rank(u):
    key = u ^ INT_MIN
    bits = jnp.where(key < 0, INT_MIN - key, key)
    return pltpu.bitcast(bits, F32)


def _count(mask):
    return jnp.sum(mask.astype(F32), axis=1, keepdims=True)


def _topk_bias(sc_ref, j_ref, adm, n_adm, lc, k):
    rows = sc_ref.shape[0]
    kf = float(k)

    def value_step(i, t_u):
        hi = jnp.left_shift(jnp.int32(1), 31 - 2 * i)
        lo = jnp.left_shift(jnp.int32(1), 30 - 2 * i)
        for cand_u in (t_u | lo, t_u | hi, t_u | hi | lo):
            cnt = _count(sc_ref[:, 0:lc] >= _float_of_rank(cand_u))
            t_u = jnp.where(cnt >= kf, cand_u, t_u)
        return t_u

    t_u = lax.fori_loop(0, 16, value_step, jnp.zeros((rows, 1), I32))
    few = n_adm < k
    thr = jnp.where(few, -jnp.inf, _float_of_rank(t_u))
    sc = sc_ref[:, 0:lc]
    cnt_gt = _count(sc > thr)
    cnt_eq = _count(sc == thr)
    need = kf - cnt_gt
    cut_needed = jnp.logical_and(cnt_gt + cnt_eq > kf, jnp.logical_not(few))
    any_cut = jnp.max(cut_needed.astype(F32)) > 0.0
    idx = lax.broadcasted_iota(I32, (rows, lc), 1)
    nbits = int(np.ceil(np.log2(lc)))

    j_ref[...] = jnp.full((rows, 1), lc, I32)

    @pl.when(any_cut)
    def _():
        def index_step(i, j):
            cand = j | jnp.left_shift(jnp.int32(1), nbits - 1 - i)
            c = _count(jnp.logical_and(sc_ref[:, 0:lc] == thr, idx < cand))
            return jnp.where(c < need, cand, j)

        j_ref[...] = lax.fori_loop(0, nbits, index_step, jnp.zeros((rows, 1), I32))

    sel = jnp.logical_or(sc > thr, jnp.logical_and(sc == thr, idx <= j_ref[...]))
    return jnp.where(jnp.logical_and(sel, adm), 0.0, NEG_BIG)


ATTN_CHUNK = 256


def _attn_prompt_block(n_chunks, q0, top_k, qt_ref, qit_ref, wit_ref, kb_ref, kib_ref, vbt_ref, o_ref,
                       key_ref, bias_ref, lg_ref, j_ref):
    tq, ch = Q_BLOCK, ATTN_CHUNK
    kf = float(top_k)
    kpos = lax.broadcasted_iota(I32, (ch, tq), 0)
    qpos = q0 + lax.broadcasted_iota(I32, (ch, tq), 1)

    def rows(c):
        return slice(c * ch, (c + 1) * ch)

    def fold(x, op):
        return op(x.reshape(ch // SUBLANES, SUBLANES, tq), axis=0)

    def head(x, h):
        return x[:, h * tq:(h + 1) * tq]

    qit = qit_ref[...]
    wit = wit_ref[...]
    for c in range(n_chunks if n_chunks * ch > top_k else 0):
        d = jnp.dot(kib_ref[rows(c), :], qit, preferred_element_type=F32)
        s = wit[0:1, :] * jnp.maximum(head(d, 0), 0.0)
        for h in range(1, IDX_HEADS):
            s = s + wit[h:h + 1, :] * jnp.maximum(head(d, h), 0.0)
        key_ref[rows(c), :] = jnp.where(c * ch + kpos <= qpos, s, -jnp.inf)

    def count(pred):
        acc = jnp.zeros((SUBLANES, tq), F32)
        for c in range(n_chunks):
            acc = acc + fold(pred(key_ref[rows(c), :], c).astype(F32), jnp.sum)
        return jnp.sum(acc, axis=0, keepdims=True)

    if n_chunks * ch <= top_k:
        for c in range(n_chunks):
            bias_ref[rows(c), :] = jnp.where(c * ch + kpos <= qpos, 0.0, NEG_BIG)
    else:
        def value_step(i, carry):
            t_u, n_ge = carry
            cand_u = t_u | jnp.left_shift(jnp.int32(1), 31 - i)
            cand = _float_of_rank(cand_u)
            cnt = count(lambda k, c: k >= cand)
            ok = cnt >= kf
            return jnp.where(ok, cand_u, t_u), jnp.where(ok, cnt, n_ge)

        t_u, n_ge = lax.fori_loop(0, 32, value_step,
                                  (jnp.zeros((1, tq), I32), jnp.full((1, tq), float(n_chunks * ch), F32)))
        few = qpos[0:1, :] + 1 <= top_k
        thr = jnp.where(few, -jnp.inf, _float_of_rank(t_u))
        cut_needed = jnp.logical_and(n_ge > kf, jnp.logical_not(few))
        any_cut = jnp.max(cut_needed.astype(F32)) > 0.0

        nbits = int(np.ceil(np.log2(n_chunks * ch)))
        j_ref[...] = jnp.full(j_ref.shape, n_chunks * ch, I32)

        @pl.when(any_cut)
        def _():
            need = kf - count(lambda k, c: k > thr)

            def index_step(i, j):
                cand = j | jnp.left_shift(jnp.int32(1), nbits - 1 - i)
                n_before = count(lambda k, c: jnp.logical_and(k == thr, c * ch + kpos < cand))
                return jnp.where(n_before < need, cand, j)

            j = lax.fori_loop(0, nbits, index_step, jnp.zeros((1, tq), I32))
            j_ref[...] = jnp.broadcast_to(j, j_ref.shape)

        j_cut = j_ref[0:1, :]
        for c in range(n_chunks):
            k = key_ref[rows(c), :]
            pos = c * ch + kpos
            sel = jnp.logical_or(k > thr, jnp.logical_and(k == thr, pos <= j_cut))
            bias_ref[rows(c), :] = jnp.where(jnp.logical_and(sel, pos <= qpos), 0.0, NEG_BIG)

    qt = qt_ref[...]
    mx = [jnp.full((SUBLANES, tq), -jnp.inf, F32) for _ in range(N_HEADS)]
    for c in range(n_chunks):
        lg = jnp.dot(kb_ref[rows(c), :], qt, preferred_element_type=F32)
        bias = bias_ref[rows(c), :]
        for h in range(N_HEADS):
            lgh = head(lg, h) + bias
            lg_ref[h, rows(c), :] = lgh
            mx[h] = jnp.maximum(mx[h], fold(lgh, jnp.max))

    outs = []
    for h in range(N_HEADS):
        m = jnp.max(mx[h], axis=0, keepdims=True)
        lsum = jnp.zeros((SUBLANES, tq), F32)
        ot = jnp.zeros((HEAD_DIM, tq), F32)
        for c in range(n_chunks):
            p = jnp.exp2(lg_ref[h, rows(c), :] - m)
            lsum = lsum + fold(p, jnp.sum)
            ot = ot + jnp.dot(vbt_ref[:, rows(c)], p.astype(BF16), preferred_element_type=F32)
        outs.append(ot / jnp.sum(lsum, axis=0, keepdims=True))
    o_ref[...] = jnp.concatenate(outs, axis=0).T.astype(BF16)


def _attn_prompt_kernel(qt_ref, qit_ref, wit_ref, kb_ref, kib_ref, vbt_ref, o_ref, key_ref, bias_ref, lg_ref, j_ref,
                        *, top_k):
    jq = pl.program_id(1)
    blocks_per_chunk = ATTN_CHUNK // Q_BLOCK
    n_classes = key_ref.shape[0] // ATTN_CHUNK
    for cls in range(n_classes):
        @pl.when(jq // blocks_per_chunk == cls)
        def _(cls=cls):
            _attn_prompt_block(cls + 1, jq * Q_BLOCK, top_k, qt_ref, qit_ref, wit_ref, kb_ref, kib_ref, vbt_ref,
                               o_ref, key_ref, bias_ref, lg_ref, j_ref)


def _attn_prompt(qt, qit, wit, kb, kib, vbt):
    batch, _, seq = vbt.shape
    nb = seq // Q_BLOCK
    top_k = min(TOP_K_MAX, seq // 4)
    slab = lambda heads: pl.BlockSpec((None, HEAD_DIM, heads * Q_BLOCK), lambda b, j: (b * nb + j, 0, 0))
    keys = pl.BlockSpec((seq, HEAD_DIM), lambda b, j: (b, 0))
    return pl.pallas_call(
        functools.partial(_attn_prompt_kernel, top_k=top_k),
        grid=(batch, nb),
        in_specs=[slab(N_HEADS), slab(IDX_HEADS), pl.BlockSpec((None, SUBLANES, Q_BLOCK), lambda b, j: (b, 0, j)),
                  keys, keys, pl.BlockSpec((None, HEAD_DIM, seq), lambda b, j: (b, 0, 0))],
        out_specs=pl.BlockSpec((Q_BLOCK, ATT_WIDTH), lambda b, j: (b * nb + j, 0)),
        out_shape=jax.ShapeDtypeStruct((batch * seq, ATT_WIDTH), BF16),
        scratch_shapes=[pltpu.VMEM((seq, Q_BLOCK), F32), pltpu.VMEM((seq, Q_BLOCK), F32),
                        pltpu.VMEM((N_HEADS, seq, Q_BLOCK), F32), pltpu.VMEM((SUBLANES, Q_BLOCK), I32)],
        compiler_params=_params(("parallel", "arbitrary")),
        name="attn_prompt",
    )(qt, qit, wit, kb, kib, vbt)


SAMPLE_CHUNK = 1024
SAMPLE_ROWS_PER_STEP = 2


def _attn_sample_kernel(pt_ref, q_ref, qi_ref, wi_ref, kn_ref, vn_ref, kin_ref, ck_hbm, cv_hbm, cki_hbm, o_ref,
                        kbuf, vbuf, kibuf, sem, key_scr, bias_scr, lg_scr, j_scr, *, n_pages, page, t_new, top_k):
    b = pl.program_id(0)
    n_b = pl.num_programs(0)
    slot = b % 2
    per_step = q_ref.shape[0]
    past = n_pages * page
    lc = past + page
    n_chunks = past // SAMPLE_CHUNK

    def page_copies(step, sl, p):
        dst = pl.ds(pl.multiple_of(p * page, page), page)
        copies = []
        for r in range(per_step):
            phys = pt_ref[(step * per_step + r) * n_pages + p]
            copies += [pltpu.make_async_copy(src.at[phys], buf.at[sl, r, :, dst], sem.at[i, sl])
                       for i, (src, buf) in enumerate(((ck_hbm, kbuf), (cv_hbm, vbuf), (cki_hbm, kibuf)))]
        return copies

    def start_batch(bb, sl):
        def body(p, carry):
            for cp in page_copies(bb, sl, p):
                cp.start()
            return carry
        lax.fori_loop(0, n_pages, body, 0)

    def wait_batch(bb, sl):
        def body(p, carry):
            for cp in page_copies(bb, sl, p):
                cp.wait()
            return carry
        lax.fori_loop(0, n_pages, body, 0)

    @pl.when(b == 0)
    def _():
        start_batch(0, 0)

    @pl.when(b + 1 < n_b)
    def _():
        start_batch(b + 1, 1 - slot)

    wait_batch(b, slot)

    def head_sum(r, d):
        x = wi_ref[r] * jnp.maximum(d, 0.0)
        s = x[0:t_new]
        for h in range(1, IDX_HEADS):
            s = s + x[h * t_new:(h + 1) * t_new]
        return s

    def new_rows(ref, r):
        pad = jnp.zeros((page - t_new, ref.shape[2]), F32)
        return jnp.concatenate([ref[r], pad], axis=0).astype(BF16)

    adm_new = lax.broadcasted_iota(I32, (t_new, page), 1) <= lax.broadcasted_iota(I32, (t_new, page), 0)
    for r in range(per_step):
        qrows = slice(r * t_new, (r + 1) * t_new)
        qi = qi_ref[r]
        for c in range(n_chunks):
            sl = slice(c * SAMPLE_CHUNK, (c + 1) * SAMPLE_CHUNK)
            d = jnp.dot(qi, kibuf[slot, r, :, sl].astype(BF16), preferred_element_type=F32)
            key_scr[qrows, sl] = head_sum(r, d)
        d_new = lax.dot_general(qi, new_rows(kin_ref, r), NT_DIMS, preferred_element_type=F32)
        key_scr[qrows, past:lc] = jnp.where(adm_new, head_sum(r, d_new), -jnp.inf)

    n_q = per_step * t_new
    idx = lax.broadcasted_iota(I32, (n_q, lc), 1)
    trow = lax.broadcasted_iota(I32, (n_q, lc), 0) % t_new
    n_adm = past + 1 + lax.broadcasted_iota(I32, (n_q, 1), 0) % t_new
    bias_scr[...] = _topk_bias(key_scr, j_scr, idx - past <= trow, n_adm, lc, top_k)

    for r in range(per_step):
        q = q_ref[r]

        def bias_rows(sl, r=r):
            return jnp.concatenate([bias_scr[r * t_new:(r + 1) * t_new, sl]] * N_HEADS, axis=0)

        m = jnp.full((N_HEADS * t_new, 1), -jnp.inf, F32)
        for c in range(n_chunks):
            sl = slice(c * SAMPLE_CHUNK, (c + 1) * SAMPLE_CHUNK)
            lg = jnp.dot(q, kbuf[slot, r, :, sl].astype(BF16), preferred_element_type=F32) + bias_rows(sl)
            lg_scr[:, sl] = lg
            m = jnp.maximum(m, jnp.max(lg, axis=1, keepdims=True))
        lg_new = (lax.dot_general(q, new_rows(kn_ref, r), NT_DIMS, preferred_element_type=F32)
                  + bias_rows(slice(past, lc)))
        m = jnp.maximum(m, jnp.max(lg_new, axis=1, keepdims=True))

        p_new = jnp.exp2(lg_new - m)
        l = jnp.sum(p_new, axis=1, keepdims=True)
        o = jnp.dot(p_new.astype(BF16), new_rows(vn_ref, r), preferred_element_type=F32)
        for c in range(n_chunks):
            sl = slice(c * SAMPLE_CHUNK, (c + 1) * SAMPLE_CHUNK)
            pr = jnp.exp2(lg_scr[:, sl] - m)
            l = l + jnp.sum(pr, axis=1, keepdims=True)
            o = o + lax.dot_general(pr.astype(BF16), vbuf[slot, r, :, sl].astype(BF16), NT_DIMS,
                                    preferred_element_type=F32)
        o_ref[r] = o / l


def _attn_sample(page_table, q_hq, qi_hq, wi_hq, k_new, v_new, ki_new, cache_kt, cache_vt, cache_kit):
    db, n_pages = page_table.shape
    page = cache_kt.shape[2]
    t_new = k_new.shape[1]
    past = n_pages * page
    lc = past + page
    top_k = min(TOP_K_MAX, (past + t_new) // 4)
    rows = SAMPLE_ROWS_PER_STEP
    per_b = lambda r, w: pl.BlockSpec((rows, r, w), lambda b, pt: (b, 0, 0))
    hbm = pl.BlockSpec(memory_space=pl.ANY)
    kern = functools.partial(_attn_sample_kernel, n_pages=n_pages, page=page, t_new=t_new, top_k=top_k)
    slab = pltpu.VMEM((2, rows, HEAD_DIM, past), F32)
    grid_spec = pltpu.PrefetchScalarGridSpec(
        num_scalar_prefetch=1,
        grid=(db // rows,),
        in_specs=[per_b(N_HEADS * t_new, HEAD_DIM), per_b(IDX_HEADS * t_new, IDX_DIM), per_b(IDX_HEADS * t_new, 1),
                  per_b(t_new, HEAD_DIM), per_b(t_new, HEAD_DIM), per_b(t_new, IDX_DIM),
                  hbm, hbm, hbm],
        out_specs=per_b(N_HEADS * t_new, HEAD_DIM),
        scratch_shapes=[slab, slab, slab, pltpu.SemaphoreType.DMA((3, 2)),
                        pltpu.VMEM((rows * t_new, lc), F32), pltpu.VMEM((rows * t_new, lc), F32),
                        pltpu.VMEM((N_HEADS * t_new, past), F32), pltpu.VMEM((rows * t_new, 1), I32)],
    )
    return pl.pallas_call(
        kern,
        grid_spec=grid_spec,
        out_shape=jax.ShapeDtypeStruct((db, N_HEADS * t_new, HEAD_DIM), F32),
        compiler_params=_params(("arbitrary",)),
        name="attn_sample",
    )(page_table.reshape(-1), q_hq, qi_hq, wi_hq, k_new, v_new, ki_new, cache_kt, cache_vt, cache_kit)


PREV_ROWS = 16


def _pool_kernel(prev_ref, u_ref, wg_ref, sc_ref, o_ref, ext_ref, *, pos0):
    per_step, t_len, _ = u_ref.shape
    pos = pos0 + lax.broadcasted_iota(I32, (t_len, 1), 0)
    for b in range(per_step):
        ext_ref[0:PREV_ROWS, :] = prev_ref[b]
        ext_ref[PREV_ROWS:PREV_ROWS + t_len, :] = u_ref[b]
        for g, w in enumerate(POOL_WINDOWS):
            sl = slice(g * POOL_GW, (g + 1) * POOL_GW)
            u_new = ext_ref[PREV_ROWS:PREV_ROWS + t_len, sl]
            win = u_new
            for back in range(1, w):
                win = win + ext_ref[PREV_ROWS - back:PREV_ROWS - back + t_len, sl]
            count = jnp.minimum(pos + 1, w).astype(F32)
            r = win / count - u_new
            mixed = jnp.dot(r.astype(BF16), wg_ref[g], preferred_element_type=F32) * sc_ref[:, sl]
            o_ref[b, :, sl] = mixed.astype(BF16)


def _pool(prev, u, w_grp, scale, pos0, per_step):
    nb, t_len, _ = u.shape
    seqs = lambda rows: pl.BlockSpec((per_step, rows, POOL_WIDTH), lambda b: (b, 0, 0))
    return pl.pallas_call(
        functools.partial(_pool_kernel, pos0=pos0),
        grid=(nb // per_step,),
        in_specs=[seqs(PREV_ROWS), seqs(t_len),
                  pl.BlockSpec((POOL_GROUPS, POOL_GW, POOL_GW), lambda b: (0, 0, 0)),
                  pl.BlockSpec((1, POOL_WIDTH), lambda b: (0, 0))],
        out_specs=seqs(t_len),
        out_shape=jax.ShapeDtypeStruct((nb, t_len, POOL_WIDTH), BF16),
        scratch_shapes=[pltpu.VMEM((PREV_ROWS + t_len, POOL_WIDTH), F32)],
        compiler_params=_params(("parallel",)),
        name="pool",
    )(prev, u, w_grp, scale)


def _merge_kernel(x_ref, a_ref, p_ref, wga_ref, wgb_ref, wao_ref, wpo_ref, wo_ref, g_ref, b_ref, h_ref, hp_ref, *,
                  alpha):
    x = x_ref[...]
    xb = x.astype(BF16)
    ga = jnp.dot(xb, wga_ref[...], preferred_element_type=F32)
    gb = jnp.dot(xb, wgb_ref[...], preferred_element_type=F32)
    ya = jnp.dot(a_ref[...], wao_ref[...], preferred_element_type=F32)
    yp = jnp.dot(p_ref[...], wpo_ref[...], preferred_element_type=F32)
    mix = jax.nn.sigmoid(ga) * ya + jax.nn.sigmoid(gb) * yp
    out = jnp.dot(mix.astype(BF16), wo_ref[...], preferred_element_type=F32)
    h = _layer_norm(alpha * x + out, g_ref[...], b_ref[...])
    h_ref[...] = h
    hp_ref[...] = _pack_rows(h)


def _merge(x, attn, pool, wga, wgb, wao, wpo, wo, g, b, tm, alpha):
    n = x.shape[0]
    row = lambda w: pl.BlockSpec((tm, w), lambda i: (i, 0))
    full = lambda r, c: pl.BlockSpec((r, c), lambda i: (0, 0), pipeline_mode=pl.Buffered(1))
    return pl.pallas_call(
        functools.partial(_merge_kernel, alpha=alpha),
        grid=(n // tm,),
        in_specs=[row(D_MODEL), row(ATT_WIDTH), row(POOL_WIDTH), full(D_MODEL, D_MODEL), full(D_MODEL, D_MODEL),
                  full(ATT_WIDTH, D_MODEL), full(POOL_WIDTH, D_MODEL), full(D_MODEL, D_MODEL),
                  full(1, D_MODEL), full(1, D_MODEL)],
        out_specs=(row(D_MODEL), row(PACKED)),
        out_shape=(jax.ShapeDtypeStruct((n, D_MODEL), F32), jax.ShapeDtypeStruct((n, PACKED), I32)),
        compiler_params=_params(("parallel",)),
        name="merge",
    )(x, attn, pool, wga, wgb, wao, wpo, wo, g, b)


def _route(h, wr_t, bias_col):
    tm = h.shape[0]
    logits = lax.dot_general(wr_t, h.astype(BF16), NT_DIMS, preferred_element_type=F32)
    s = jax.nn.sigmoid(logits)
    sb = s + bias_col
    neg_inf = -jnp.inf

    rows = []
    for g in range(N_GROUPS):
        blk = sb[g * GROUP_SIZE:(g + 1) * GROUP_SIZE, :]
        m1 = jnp.max(blk, axis=0, keepdims=True)
        is_m1 = blk == m1
        n_m1 = jnp.sum(is_m1.astype(F32), axis=0, keepdims=True)
        m2 = jnp.max(jnp.where(is_m1, neg_inf, blk), axis=0, keepdims=True)
        rows.append(m1 + jnp.where(n_m1 >= 2.0, m1, m2))
    gs = jnp.concatenate(rows, axis=0)

    gi = lax.broadcasted_iota(I32, (N_GROUPS, tm), 0)
    rank = jnp.zeros((N_GROUPS, tm), F32)
    for g in range(N_GROUPS):
        row = gs[g:g + 1, :]
        beats = jnp.logical_or(row > gs, jnp.logical_and(row == gs, g < gi))
        rank = rank + beats.astype(F32)
    gkeep = rank < float(TOPK_GROUPS)
    emask = jnp.concatenate(
        [jnp.broadcast_to(gkeep[g:g + 1, :], (GROUP_SIZE, tm)) for g in range(N_GROUPS)], axis=0)

    ei = lax.broadcasted_iota(I32, (N_EXPERTS, tm), 0)
    x = jnp.where(emask, sb, neg_inf)
    sel = jnp.zeros((N_EXPERTS, tm), jnp.bool_)
    picks = []
    for _ in range(TOP_K_EXPERTS):
        m = jnp.max(x, axis=0, keepdims=True)
        first = jnp.min(jnp.where(x == m, ei, N_EXPERTS), axis=0, keepdims=True)
        pick = ei == first
        sel = jnp.logical_or(sel, pick)
        x = jnp.where(pick, neg_inf, x)
        picks.append(first)

    gate = jnp.where(sel, s, 0.0)
    comb = gate / jnp.sum(gate, axis=0, keepdims=True) * ROUTED_SCALE
    return comb, sel, picks


def _router_kernel(h_ref, wr_ref, bias_ref, c_ref):
    comb, _, _ = _route(h_ref[...], wr_ref[...], bias_ref[...])
    comb = jnp.concatenate([comb, jnp.zeros((LANES - N_EXPERTS, comb.shape[1]), F32)], axis=0)
    c_ref[...] = comb.T


def _router(h, wr_t, bias_col, tm):
    n = h.shape[0]
    return pl.pallas_call(
        _router_kernel,
        grid=(n // tm,),
        in_specs=[pl.BlockSpec((tm, D_MODEL), lambda i: (i, 0)),
                  pl.BlockSpec((N_EXPERTS, D_MODEL), lambda i: (0, 0)),
                  pl.BlockSpec((N_EXPERTS, 1), lambda i: (0, 0))],
        out_specs=pl.BlockSpec((tm, LANES), lambda i: (i, 0)),
        out_shape=jax.ShapeDtypeStruct((n, LANES), F32),
        compiler_params=_params(("parallel",)),
        name="router",
    )(h, wr_t, bias_col)


def _swiglu(xb, w13, w2, hidden):
    ab = jnp.dot(xb, w13, preferred_element_type=F32)
    act = jax.nn.silu(ab[:, 0:hidden]) * ab[:, hidden:2 * hidden]
    return jnp.dot(act.astype(BF16), w2, preferred_element_type=F32)


def _moe_kernel(h_ref, c_ref, ws13_ref, ws2_ref, w13_ref, w2_ref, y_ref, hb_ref):
    e = pl.program_id(1)

    @pl.when(e == 0)
    def _():
        hb_ref[...] = h_ref[...].astype(BF16)
        y_ref[...] = _swiglu(hb_ref[...], ws13_ref[...], ws2_ref[...], SHARED_DIM)

    ye = _swiglu(hb_ref[...], w13_ref[...].astype(BF16), w2_ref[...].astype(BF16), EXPERT_DIM)
    lane = lax.broadcasted_iota(I32, c_ref.shape, 1)
    ce = jnp.sum(jnp.where(lane == e, c_ref[...], 0.0), axis=1, keepdims=True)
    y_ref[...] += ce * ye


def _moe(h, comb, ws13, ws2, w13, w2, tm):
    n = h.shape[0]
    return pl.pallas_call(
        _moe_kernel,
        grid=(n // tm, N_EXPERTS),
        in_specs=[pl.BlockSpec((tm, D_MODEL), lambda i, e: (i, 0)),
                  pl.BlockSpec((tm, LANES), lambda i, e: (i, 0)),
                  pl.BlockSpec((D_MODEL, 2 * SHARED_DIM), lambda i, e: (0, 0)),
                  pl.BlockSpec((SHARED_DIM, D_MODEL), lambda i, e: (0, 0)),
                  pl.BlockSpec((None, D_MODEL, 2 * EXPERT_DIM), lambda i, e: (e, 0, 0)),
                  pl.BlockSpec((None, EXPERT_DIM, D_MODEL), lambda i, e: (e, 0, 0))],
        out_specs=pl.BlockSpec((tm, D_MODEL), lambda i, e: (i, 0)),
        out_shape=jax.ShapeDtypeStruct((n, D_MODEL), F32),
        scratch_shapes=[pltpu.VMEM((tm, D_MODEL), BF16)],
        compiler_params=_params(("parallel", "arbitrary")),
        name="moe",
    )(h, comb, ws13, ws2, w13, w2)


def _final_kernel(h_ref, y_ref, pe_ref, g_ref, b_ref, wpg_ref, wpi_ref, o_ref, *, alpha):
    z = _layer_norm(alpha * h_ref[...] + y_ref[...], g_ref[...], b_ref[...])
    gate = jax.nn.sigmoid(jnp.dot(z.astype(BF16), wpg_ref[...], preferred_element_type=F32))
    emb = jnp.dot(pe_ref[...].astype(BF16), wpi_ref[...], preferred_element_type=F32)
    o_ref[...] = z + gate * emb


def _final(h, y, pe, g, b, wpg, wpi, tm, alpha):
    n = h.shape[0]
    row = lambda w: pl.BlockSpec((tm, w), lambda i: (i, 0))
    full = lambda r, c: pl.BlockSpec((r, c), lambda i: (0, 0))
    return pl.pallas_call(
        functools.partial(_final_kernel, alpha=alpha),
        grid=(n // tm,),
        in_specs=[row(D_MODEL), row(D_MODEL), row(PLE_DIM), full(1, D_MODEL), full(1, D_MODEL),
                  full(D_MODEL, D_MODEL), full(PLE_DIM, D_MODEL)],
        out_specs=row(D_MODEL),
        out_shape=jax.ShapeDtypeStruct((n, D_MODEL), F32),
        compiler_params=_params(("parallel",)),
        name="final",
    )(h, y, pe, g, b, wpg, wpi)


MOE_BLOCK = 2176


def _sorted_rows(n_tokens):
    worst = n_tokens * TOP_K_EXPERTS + N_EXPERTS * MOE_BLOCK
    return -(-worst // MOE_BLOCK) * MOE_BLOCK


def _dispatch_kernel(h_ref, wr_ref, bias_ref, tri_ref, pos_ref, gate_ref, blk_ref, used_ref,
                     eidx_s, rank_s, gate_s, cnt_s):
    p = pl.program_id(0)
    i = pl.program_id(1)
    tm = h_ref.shape[0]
    ei = lax.broadcasted_iota(I32, (N_EXPERTS, tm), 0)

    @pl.when(p == 0)
    def _():
        comb, sel, picks = _route(h_ref[...], wr_ref[...], bias_ref[...])
        before = jnp.dot(sel.astype(BF16), tri_ref[...], preferred_element_type=F32)
        ranks, gates = [], []
        for first in picks:
            pick = ei == first
            ranks.append(jnp.sum(jnp.where(pick, before, 0.0), axis=0, keepdims=True))
            gates.append(jnp.sum(jnp.where(pick, comb, 0.0), axis=0, keepdims=True))
        eidx_s[i] = jnp.concatenate(picks, axis=0)
        rank_s[i] = jnp.concatenate(ranks, axis=0)
        gate_s[i] = jnp.concatenate(gates, axis=0)
        cnt_s[i] = jnp.broadcast_to(jnp.sum(sel.astype(F32), axis=1, keepdims=True), (N_EXPERTS, LANES))

    @pl.when(p == 1)
    def _():
        cnt = cnt_s[...]
        tile_id = lax.broadcasted_iota(I32, cnt.shape, 0)
        total = jnp.sum(cnt, axis=0)
        prior = jnp.sum(jnp.where(tile_id < i, cnt, 0.0), axis=0)
        seg_blk = jnp.maximum(jnp.ceil(total * (1.0 / MOE_BLOCK) - 0.25 / MOE_BLOCK), 1.0)
        lower = (lax.broadcasted_iota(I32, (N_EXPERTS, N_EXPERTS), 1)
                 < lax.broadcasted_iota(I32, (N_EXPERTS, N_EXPERTS), 0)).astype(F32)
        off_blk = jnp.dot(lower, seg_blk, precision=lax.Precision.HIGHEST, preferred_element_type=F32)
        seg_off = off_blk * MOE_BLOCK
        base = (seg_off + prior)[:, 0:1]
        eidx = eidx_s[i]
        rank = rank_s[i]
        rows = []
        for k in range(TOP_K_EXPERTS):
            pick = ei == eidx[k:k + 1, :]
            rows.append(rank[k:k + 1, :] + jnp.sum(jnp.where(pick, base, 0.0), axis=0, keepdims=True))
        pos_ref[...] = jnp.concatenate(rows, axis=0).astype(I32)
        gate_ref[...] = jnp.concatenate([gate_s[i], jnp.zeros((LANES - TOP_K_EXPERTS, tm), F32)], axis=0).T

        end_blk = (off_blk + seg_blk)[:, 0:1]
        n_blk = blk_ref.shape[1]
        blk_id = lax.broadcasted_iota(I32, (N_EXPERTS, n_blk), 1).astype(F32)
        owner = jnp.sum((end_blk <= blk_id).astype(F32), axis=0, keepdims=True)
        blk_ref[...] = jnp.minimum(owner, N_EXPERTS - 1.0).astype(I32)
        used_ref[...] = jnp.broadcast_to(end_blk[N_EXPERTS - 1:N_EXPERTS, :], used_ref.shape).astype(I32)


def _dispatch(h, wr_t, bias_col, tm):
    n = h.shape[0]
    n_tiles = n // tm
    n_blk = _sorted_rows(n) // MOE_BLOCK
    n_blk_pad = -(-n_blk // LANES) * LANES
    tri = jnp.triu(jnp.ones((tm, tm), BF16), k=1)
    const = lambda r, c: pl.BlockSpec((r, c), lambda p, i: (0, 0))
    per_tile = lambda dt: pltpu.VMEM((n_tiles, TOP_K_EXPERTS, tm), dt)
    return pl.pallas_call(
        _dispatch_kernel,
        grid=(2, n_tiles),
        in_specs=[pl.BlockSpec((tm, D_MODEL), lambda p, i: (i * (1 - p), 0)),
                  const(N_EXPERTS, D_MODEL), const(N_EXPERTS, 1), const(tm, tm)],
        out_specs=(pl.BlockSpec((TOP_K_EXPERTS, tm), lambda p, i: (0, i * p)),
                   pl.BlockSpec((tm, LANES), lambda p, i: (i * p, 0)),
                   const(1, n_blk_pad), const(1, LANES)),
        out_shape=(jax.ShapeDtypeStruct((TOP_K_EXPERTS, n), I32), jax.ShapeDtypeStruct((n, LANES), F32),
                   jax.ShapeDtypeStruct((1, n_blk_pad), I32), jax.ShapeDtypeStruct((1, LANES), I32)),
        scratch_shapes=[per_tile(I32), per_tile(F32), per_tile(F32), pltpu.VMEM((n_tiles, N_EXPERTS, LANES), F32)],
        compiler_params=_params(("arbitrary", "arbitrary")),
        name="dispatch",
    )(h, wr_t, bias_col, tri)


PACKED = D_MODEL // 2


def _pack_rows(x):
    lo = pltpu.bitcast(x[:, 0:PACKED].astype(BF16).astype(F32), I32)
    hi = pltpu.bitcast(x[:, PACKED:D_MODEL].astype(BF16).astype(F32), I32)
    return jnp.bitwise_or(hi, lax.shift_right_logical(lo, 16))


def _unpack_rows_f32(w):
    lo = pltpu.bitcast(lax.shift_left(w, 16), F32)
    hi = pltpu.bitcast(jnp.bitwise_and(w, -65536), F32)
    return jnp.concatenate([lo, hi], axis=1)


def _unpack_rows(w):
    return _unpack_rows_f32(w).astype(BF16)


GROUPED_RING = 3


def _grouped_kernel(blk_ref, used_ref, anchor_ref, xs_hbm, w13_ref, w2_ref, ys_ref, w13b_ref, w2b_ref, xbuf, sem):
    s = pl.program_id(0)
    used = used_ref[0]

    def row_copy(b):
        slot = b % GROUPED_RING
        return pltpu.make_async_copy(xs_hbm.at[pl.ds(pl.multiple_of(b * MOE_BLOCK, MOE_BLOCK), MOE_BLOCK)],
                                     xbuf.at[slot], sem.at[slot])

    @pl.when(s == 0)
    def _():
        for b in range(GROUPED_RING - 1):
            @pl.when(b < used)
            def _(b=b):
                row_copy(b).start()

    @pl.when(s + GROUPED_RING - 1 < used)
    def _():
        row_copy(s + GROUPED_RING - 1).start()

    @pl.when(s < used)
    def _():
        row_copy(s).wait()
        w13b_ref[...] = w13_ref[...].astype(BF16)
        w2b_ref[...] = w2_ref[...].astype(BF16)
        ys = _swiglu(_unpack_rows(xbuf[s % GROUPED_RING]), w13b_ref[...], w2b_ref[...], EXPERT_DIM)
        ys_ref[...] = _pack_rows(ys)


def _grouped(blk, used, anchor, xs, w13, w2):
    ns = xs.shape[0]
    row_blk = lambda b, blk, used, anchor: (jnp.minimum(b, used[0] - 1), 0)
    expert = lambda b, blk, used, anchor: (blk[b], 0, 0)
    grid_spec = pltpu.PrefetchScalarGridSpec(
        num_scalar_prefetch=3,
        grid=(ns // MOE_BLOCK,),
        in_specs=[pl.BlockSpec(memory_space=pl.ANY),
                  pl.BlockSpec((None, D_MODEL, 2 * EXPERT_DIM), expert),
                  pl.BlockSpec((None, EXPERT_DIM, D_MODEL), expert)],
        out_specs=(pl.BlockSpec((MOE_BLOCK, PACKED), row_blk),
                   pl.BlockSpec((None, D_MODEL, 2 * EXPERT_DIM), expert),
                   pl.BlockSpec((None, EXPERT_DIM, D_MODEL), expert)),
        scratch_shapes=[pltpu.VMEM((GROUPED_RING, MOE_BLOCK, PACKED), I32), pltpu.SemaphoreType.DMA((GROUPED_RING,))],
    )
    return pl.pallas_call(
        _grouped_kernel,
        grid_spec=grid_spec,
        out_shape=(jax.ShapeDtypeStruct((ns, PACKED), I32), jax.ShapeDtypeStruct(w13.shape, BF16),
                   jax.ShapeDtypeStruct(w2.shape, BF16)),
        compiler_params=_params(("arbitrary",)),
        name="grouped",
    )(blk, used, anchor, xs, w13, w2)


SC_WINDOW = 128


def _sc_mesh():
    return plsc.VectorSubcoreMesh(core_axis_name="core", subcore_axis_name="subcore")


def _sc_worker(n_items):
    info = plsc.get_sparse_core_info()
    n_workers = info.num_cores * info.num_subcores
    assert n_items % (SC_WINDOW * n_workers) == 0, "rows must split evenly into windows over the vector subcores"
    wid = lax.axis_index("subcore") * info.num_cores + lax.axis_index("core")
    return wid, n_items // (SC_WINDOW * n_workers)


def _scatter_rows(x, pos, n_out):
    n, width = x.shape
    picks = pos.shape[0]

    @functools.partial(
        pl.kernel, mesh=_sc_mesh(), out_type=jax.ShapeDtypeStruct((n_out, width), I32),
        scratch_types=[pltpu.VMEM((picks, SC_WINDOW), I32), pltpu.VMEM((SC_WINDOW, width), I32)],
        name="scatter_rows")
    def scatter(x_hbm, pos_hbm, out_hbm, idx_v, rows_v):
        wid, n_win = _sc_worker(n)

        @pl.loop(0, n_win)
        def _(j):
            base = (wid * n_win + j) * SC_WINDOW
            pltpu.sync_copy(pos_hbm.at[:, pl.ds(base, SC_WINDOW)], idx_v)
            pltpu.sync_copy(x_hbm.at[pl.ds(base, SC_WINDOW)], rows_v)
            for k in range(picks):
                pltpu.sync_copy(rows_v, out_hbm.at[idx_v.at[k]])

    return scatter(x, pos)


def _gather_rows(src, pos):
    width = src.shape[1]
    picks, n = pos.shape

    @functools.partial(
        pl.kernel, mesh=_sc_mesh(), out_type=jax.ShapeDtypeStruct((picks * n, width), I32),
        scratch_types=[pltpu.VMEM((SC_WINDOW,), I32), pltpu.VMEM((SC_WINDOW, width), I32)],
        name="gather_rows")
    def gather(src_hbm, pos_hbm, out_hbm, idx_v, rows_v):
        wid, n_win = _sc_worker(picks * n)

        @pl.loop(0, n_win)
        def _(j):
            base = (wid * n_win + j) * SC_WINDOW
            pltpu.sync_copy(pos_hbm.at[pl.ds(base, SC_WINDOW)], idx_v)
            pltpu.sync_copy(src_hbm.at[idx_v], rows_v)
            pltpu.sync_copy(rows_v, out_hbm.at[pl.ds(base, SC_WINDOW)])

    return gather(src, pos.reshape(-1)).reshape(picks, n, width)


def _combine_kernel(h_ref, g_ref, gate_ref, pe_ref, ws13_ref, ws2_ref, ln_g_ref, ln_b_ref, wpg_ref, wpi_ref, o_ref, *,
                    alpha):
    h = h_ref[...]
    y = _swiglu(h.astype(BF16), ws13_ref[...], ws2_ref[...], SHARED_DIM)
    gate = gate_ref[...]
    for k in range(TOP_K_EXPERTS):
        y = y + gate[:, k:k + 1] * _unpack_rows_f32(g_ref[k])
    z = _layer_norm(alpha * h + y, ln_g_ref[...], ln_b_ref[...])
    ple_gate = jax.nn.sigmoid(jnp.dot(z.astype(BF16), wpg_ref[...], preferred_element_type=F32))
    emb = jnp.dot(pe_ref[...].astype(BF16), wpi_ref[...], preferred_element_type=F32)
    o_ref[...] = z + ple_gate * emb


def _combine(h, gathered, gate, pe, ws13, ws2, g, b, wpg, wpi, tm, alpha):
    n = h.shape[0]
    row = lambda w: pl.BlockSpec((tm, w), lambda i: (i, 0))
    full = lambda r, c: pl.BlockSpec((r, c), lambda i: (0, 0))
    return pl.pallas_call(
        functools.partial(_combine_kernel, alpha=alpha),
        grid=(n // tm,),
        in_specs=[row(D_MODEL), pl.BlockSpec((TOP_K_EXPERTS, tm, PACKED), lambda i: (0, i, 0)), row(LANES),
                  row(PLE_DIM), full(D_MODEL, 2 * SHARED_DIM), full(SHARED_DIM, D_MODEL),
                  full(1, D_MODEL), full(1, D_MODEL), full(D_MODEL, D_MODEL), full(PLE_DIM, D_MODEL)],
        out_specs=row(D_MODEL),
        out_shape=jax.ShapeDtypeStruct((n, D_MODEL), F32),
        compiler_params=_params(("parallel",)),
        name="combine",
    )(h, gathered, gate, pe, ws13, ws2, g, b, wpg, wpi)


def _rope_table(pos):
    inv = ROPE_THETA ** (-jnp.arange(0, HEAD_DIM, 2, dtype=F32) / HEAD_DIM)
    ang = pos.astype(F32)[:, None] * inv[None, :]
    return jnp.concatenate([jnp.tile(jnp.cos(ang), (1, 4)), jnp.tile(jnp.sin(ang), (1, 4))], axis=1)


def _fused_in_weight(w_in):
    offs = np.cumsum(IN_SIZES)[:-1].tolist()
    wq, wk, wv, wqi, wki, wwi, wu, wga, wgb = jnp.split(w_in, offs, axis=1)
    pad = jnp.zeros((D_MODEL, LANES - HEAD_DIM - IDX_HEADS), w_in.dtype)
    w_big = jnp.concatenate([wq, wqi, wk, wki, wv, wwi, pad, wu], axis=1).astype(BF16)
    return w_big, w_big[:, 0:C_U].T, wga.astype(BF16), wgb.astype(BF16)


def _pages_transposed(cache):
    return jnp.transpose(cache[0], (0, 2, 1))


def _heads_major(a, n_heads):
    b, t, w = a.shape
    d = w // n_heads
    return a.reshape(b, t, n_heads, d).transpose(0, 2, 1, 3).reshape(b, n_heads * t, d)


def kernel(x_prompt, x_sample, cache_k, cache_v, cache_kidx, state_pool, page_table, p_prompt, p_sample, w_in, w_att_out, w_pool_grp, pool_scale, w_pool_out, w_out, ln1_g, ln1_b, w_router, router_bias, w_exp13, w_exp2, w_sh13, w_sh2, ln2_g, ln2_b, w_ple_in, w_ple_gate):
    B, S, D = x_prompt.shape
    DB, T, _ = x_sample.shape
    depth = w_in.shape[0]
    assert depth == 1, "single layer step"
    page = cache_k.shape[2]
    past = page_table.shape[1] * page
    alpha = (2 * depth) ** 0.25
    n_p, n_s = B * S, DB * T

    w_big, w_t, wga, wgb = _fused_in_weight(w_in[0])
    wao, wpo, wo = w_att_out[0].astype(BF16), w_pool_out[0].astype(BF16), w_out[0].astype(BF16)
    wgrp = w_pool_grp[0].astype(BF16)
    pscale = pool_scale[0].reshape(1, POOL_WIDTH)
    g1, b1 = ln1_g[0].reshape(1, D), ln1_b[0].reshape(1, D)
    g2, b2 = ln2_g[0].reshape(1, D), ln2_b[0].reshape(1, D)
    wr_t = w_router[0].T.astype(BF16)
    rbias = router_bias[0].reshape(N_EXPERTS, 1)
    w13, w2 = w_exp13[0], w_exp2[0]
    ws13, ws2 = w_sh13[0].astype(BF16), w_sh2[0].astype(BF16)
    wpg, wpi = w_ple_gate[0].astype(BF16), w_ple_in[0].astype(BF16)

    cs_p = _rope_table(jnp.arange(S, dtype=I32))
    cs_s = jnp.tile(_rope_table(past + jnp.arange(T, dtype=I32)), (DB, 1))

    xp = x_prompt.reshape(n_p, D)
    qt, qit, wit, kb, kib, vbt, kt, vt, kit, u = _proj_prompt(xp, w_big, w_t, cs_p, S, PROJ_TILE)
    attn_p = _attn_prompt(qt, qit, wit, kb, kib, vbt)
    u3 = u.reshape(B, S, POOL_WIDTH)
    pool_p = _pool(jnp.zeros((B, PREV_ROWS, POOL_WIDTH), F32), u3, wgrp, pscale, 0, 1).reshape(n_p, POOL_WIDTH)
    h_p, hp_p = _merge(xp, attn_p, pool_p, wga, wgb, wao, wpo, wo, g1, b1, MERGE_TILE, alpha)

    xs = x_sample.reshape(n_s, D)
    qs, qis, ks, vs, kis, wis, us = _proj_sample(xs, w_big, cs_s)
    q_hq = _heads_major(qs.reshape(DB, T, ATT_WIDTH), N_HEADS)
    qi_hq = _heads_major(qis.reshape(DB, T, IDX_HEADS * IDX_DIM), IDX_HEADS)
    wi_hq = wis.reshape(DB, T, IDX_HEADS).transpose(0, 2, 1).reshape(DB, IDX_HEADS * T, 1)
    caches = (_pages_transposed(cache_k), _pages_transposed(cache_v), _pages_transposed(cache_kidx))
    new_rows = (ks.reshape(DB, T, HEAD_DIM), vs.reshape(DB, T, HEAD_DIM), kis.reshape(DB, T, IDX_DIM))
    half = DB * 5 // 8 // SAMPLE_ROWS_PER_STEP * SAMPLE_ROWS_PER_STEP
    o_halves = [_attn_sample(page_table[sl], q_hq[sl], qi_hq[sl], wi_hq[sl], *(a[sl] for a in new_rows), *caches)
                for sl in (slice(0, half), slice(half, DB))]
    o_hq = jnp.concatenate(o_halves, axis=0)
    attn_s = o_hq.reshape(DB, N_HEADS, T, HEAD_DIM).transpose(0, 2, 1, 3).reshape(n_s, ATT_WIDTH).astype(BF16)
    us3 = us.reshape(DB, T, POOL_WIDTH)
    prev_s = jnp.concatenate([jnp.zeros((DB, PREV_ROWS - POOL_STATE, POOL_WIDTH), F32), state_pool[0]], axis=1)
    pool_s = _pool(prev_s, us3, wgrp, pscale, past, DB).reshape(n_s, POOL_WIDTH)
    h_s, _ = _merge(xs, attn_s, pool_s, wga, wgb, wao, wpo, wo, g1, b1, n_s, alpha)

    pos, gate, blk, used = _dispatch(h_p, wr_t, rbias, DISPATCH_TILE)
    sorted_in = _scatter_rows(hp_p, pos, _sorted_rows(n_p))
    anchor = lax.bitcast_convert_type(o_halves[0][0, 0, 0:1], I32)
    sorted_out, w13_b, w2_b = _grouped(blk.reshape(-1), used.reshape(-1), anchor, sorted_in, w13, w2)
    gathered = _gather_rows(sorted_out, pos)

    comb_s = _router(h_s, wr_t, rbias, n_s)
    y_s = _final(h_s, _moe(h_s, comb_s, ws13, ws2, w13_b, w2_b, n_s), p_sample[0].reshape(n_s, PLE_DIM), g2, b2, wpg,
                 wpi, n_s, alpha)
    y_p = _combine(h_p, gathered, gate, p_prompt[0].reshape(n_p, PLE_DIM), ws13, ws2, g2, b2, wpg, wpi, COMBINE_TILE,
                   alpha)

    ext_s = jnp.concatenate([state_pool[0], us3], axis=1)
    return (y_p.reshape(B, S, D), y_s.reshape(DB, T, D),
            jnp.transpose(kt, (0, 2, 1))[None], jnp.transpose(vt, (0, 2, 1))[None],
            jnp.transpose(kit, (0, 2, 1))[None],
            u3[:, S - POOL_STATE:][None],
            ks.reshape(1, DB, T, HEAD_DIM), vs.reshape(1, DB, T, HEAD_DIM), kis.reshape(1, DB, T, IDX_DIM),
            ext_s[:, T:][None])
```

```python
import functools

import numpy as np
import jax
import jax.numpy as jnp
from jax import lax
from jax.experimental import pallas as pl
from jax.experimental.pallas import tpu as pltpu
from jax.experimental.pallas import tpu_sc as plsc

F32 = jnp.float32
BF16 = jnp.bfloat16
I32 = jnp.int32

D_MODEL = 1024
N_HEADS = 8
HEAD_DIM = 64
ATT_WIDTH = N_HEADS * HEAD_DIM
IDX_HEADS = 4
IDX_DIM = 64
TOP_K_MAX = 256
Q_BLOCK = 256
ROPE_THETA = 10000.0
POOL_WINDOWS = (2, 4, 8, 16)
POOL_GROUPS = 4
POOL_WIDTH = 512
POOL_GW = POOL_WIDTH // POOL_GROUPS
POOL_STATE = 15
N_EXPERTS = 64
TOP_K_EXPERTS = 8
N_GROUPS = 8
GROUP_SIZE = N_EXPERTS // N_GROUPS
TOPK_GROUPS = 4
EXPERT_DIM = 256
SHARED_DIM = 256
ROUTED_SCALE = 2.5
PLE_DIM = 256
LN_EPS = 1e-5
IN_SIZES = (ATT_WIDTH, HEAD_DIM, HEAD_DIM, IDX_HEADS * IDX_DIM, IDX_DIM, IDX_HEADS, POOL_WIDTH, D_MODEL, D_MODEL)

LANES = 128
SUBLANES = 8
INT_MIN = -2147483648
NEG_BIG = -1e30
VMEM_LIMIT = 56 * 1024 * 1024
PROJ_TILE = 512
MERGE_TILE = 1024
DISPATCH_TILE = 1024
COMBINE_TILE = 512

C_Q = 0
C_QI = 512
C_KK = 768
C_VW = 896
C_U = 1024
C_END = 1536
HALF = HEAD_DIM // 2

NT_DIMS = (((1,), (1,)), ((), ()))

Q_SCALE = HEAD_DIM ** -0.5 * float(np.log2(np.e))
QI_SCALE = IDX_DIM ** -0.5


def _params(sem):
    return pltpu.CompilerParams(dimension_semantics=sem, vmem_limit_bytes=VMEM_LIMIT)


def _layer_norm(x, g, b):
    mu = jnp.mean(x, axis=-1, keepdims=True)
    xc = x - mu
    var = jnp.mean(xc * xc, axis=-1, keepdims=True)
    return xc * lax.rsqrt(var + LN_EPS) * g + b


def _rope_rows(a, cos, sin):
    first_half = lax.broadcasted_iota(I32, (a.shape[0], LANES), 1) % HEAD_DIM < HALF
    out = []
    for s in range(a.shape[1] // LANES):
        x = a[:, s * LANES:(s + 1) * LANES]
        rot = jnp.where(first_half, -pltpu.roll(x, LANES - HALF, axis=1), pltpu.roll(x, HALF, axis=1))
        out.append(x * cos + rot * sin)
    return out[0] if len(out) == 1 else jnp.concatenate(out, axis=1)


def _proj_sample_kernel(x_ref, w_ref, cs_ref, q_ref, qi_ref, k_ref, v_ref, ki_ref, wi_ref, u_ref):
    xb = x_ref[...].astype(BF16)
    cos = cs_ref[:, 0:LANES]
    sin = cs_ref[:, LANES:2 * LANES]

    def mm(c0, n):
        return jnp.dot(xb, w_ref[:, c0:c0 + n], preferred_element_type=F32)

    def rope(c0, n):
        return _rope_rows(mm(c0, n), cos, sin)

    q_ref[...] = (rope(C_Q, ATT_WIDTH) * Q_SCALE).astype(BF16)
    qi_ref[...] = (rope(C_QI, IDX_HEADS * IDX_DIM) * QI_SCALE).astype(BF16)
    kk = rope(C_KK, LANES)
    k_ref[...] = kk[:, 0:HEAD_DIM]
    ki_ref[...] = kk[:, HEAD_DIM:2 * HEAD_DIM]
    vw = mm(C_VW, LANES)
    v_ref[...] = vw[:, 0:HEAD_DIM]
    wi_ref[...] = vw[:, HEAD_DIM:HEAD_DIM + IDX_HEADS] * (IDX_HEADS ** -0.5)
    u_ref[...] = mm(C_U, POOL_WIDTH)


def _proj_sample(x, w_big, cs):
    n = x.shape[0]
    full = lambda r, c: pl.BlockSpec((r, c), lambda i: (0, 0))
    widths = (ATT_WIDTH, IDX_HEADS * IDX_DIM, HEAD_DIM, HEAD_DIM, IDX_DIM, IDX_HEADS, POOL_WIDTH)
    dtypes = (BF16, BF16, F32, F32, F32, F32, F32)
    return pl.pallas_call(
        _proj_sample_kernel,
        grid=(1,),
        in_specs=[full(n, D_MODEL), full(D_MODEL, C_END), full(n, 2 * LANES)],
        out_specs=tuple(full(n, w) for w in widths),
        out_shape=tuple(jax.ShapeDtypeStruct((n, w), dt) for w, dt in zip(widths, dtypes)),
        compiler_params=_params(("arbitrary",)),
        name="proj_sample",
    )(x, w_big, cs)


def _proj_prompt_kernel(x_ref, w_ref, wt_ref, cs_ref, cst_ref, qt_ref, qit_ref, wit_ref, kb_ref, kib_ref, vbt_ref,
                        kt_ref, vt_ref, kit_ref, u_ref):
    xb = x_ref[...].astype(BF16)
    tm = xb.shape[0]
    cos = cs_ref[:, 0:LANES]
    sin = cs_ref[:, LANES:2 * LANES]
    cos_t = cst_ref[0:HEAD_DIM, :]
    sin_t = cst_ref[LANES:LANES + HEAD_DIM, :]

    def mm(c0, n):
        return jnp.dot(xb, w_ref[:, c0:c0 + n], preferred_element_type=F32)

    def mm_t(c0, n):
        return lax.dot_general(wt_ref[c0:c0 + n, :], xb, NT_DIMS, preferred_element_type=F32)

    def rope_t(c0, heads):
        a = mm_t(c0, heads * HEAD_DIM)
        parts = []
        for h in range(heads):
            x1 = a[h * HEAD_DIM:h * HEAD_DIM + HALF, :]
            x2 = a[h * HEAD_DIM + HALF:(h + 1) * HEAD_DIM, :]
            rot = jnp.concatenate([-x2, x1], axis=0)
            parts.append(a[h * HEAD_DIM:(h + 1) * HEAD_DIM, :] * cos_t + rot * sin_t)
        return parts[0] if heads == 1 else jnp.concatenate(parts, axis=0)

    kk = _rope_rows(mm(C_KK, LANES), cos, sin)
    kb_ref[...] = kk[:, 0:HEAD_DIM].astype(BF16)
    kib_ref[...] = kk[:, HEAD_DIM:2 * HEAD_DIM].astype(BF16)
    u_ref[...] = mm(C_U, POOL_WIDTH)

    qt = (rope_t(C_Q, N_HEADS) * Q_SCALE).astype(BF16)
    qit = (rope_t(C_QI, IDX_HEADS) * QI_SCALE).astype(BF16)
    for blk in range(tm // Q_BLOCK):
        cols = slice(blk * Q_BLOCK, (blk + 1) * Q_BLOCK)
        for h in range(N_HEADS):
            qt_ref[blk, :, h * Q_BLOCK:(h + 1) * Q_BLOCK] = qt[h * HEAD_DIM:(h + 1) * HEAD_DIM, cols]
        for h in range(IDX_HEADS):
            qit_ref[blk, :, h * Q_BLOCK:(h + 1) * Q_BLOCK] = qit[h * IDX_DIM:(h + 1) * IDX_DIM, cols]

    kkt = rope_t(C_KK, 2)
    kt_ref[...] = kkt[0:HEAD_DIM, :]
    kit_ref[...] = kkt[HEAD_DIM:2 * HEAD_DIM, :]
    vwt = mm_t(C_VW, LANES)
    vt_ref[...] = vwt[0:HEAD_DIM, :]
    vbt_ref[...] = vwt[0:HEAD_DIM, :].astype(BF16)
    wit_ref[...] = vwt[HEAD_DIM:HEAD_DIM + SUBLANES, :] * (IDX_HEADS ** -0.5)


def _proj_prompt(x, w_big, w_t, cs, seq, tm):
    n = x.shape[0]
    nb = seq // tm
    qb = tm // Q_BLOCK
    row = lambda w: pl.BlockSpec((tm, w), lambda i: (i, 0))
    col = lambda r: pl.BlockSpec((None, r, tm), lambda i: (i // nb, 0, i % nb))
    slab = lambda heads: pl.BlockSpec((qb, HEAD_DIM, heads * Q_BLOCK), lambda i: (i, 0, 0))
    pm = lambda r, dt: jax.ShapeDtypeStruct((n // seq, r, seq), dt)
    out_shape = (
        jax.ShapeDtypeStruct((n // Q_BLOCK, HEAD_DIM, N_HEADS * Q_BLOCK), BF16),
        jax.ShapeDtypeStruct((n // Q_BLOCK, IDX_DIM, IDX_HEADS * Q_BLOCK), BF16),
        pm(SUBLANES, F32),
        jax.ShapeDtypeStruct((n, HEAD_DIM), BF16), jax.ShapeDtypeStruct((n, IDX_DIM), BF16),
        pm(HEAD_DIM, BF16),
        pm(HEAD_DIM, F32), pm(HEAD_DIM, F32), pm(IDX_DIM, F32),
        jax.ShapeDtypeStruct((n, POOL_WIDTH), F32),
    )
    return pl.pallas_call(
        _proj_prompt_kernel,
        grid=(n // tm,),
        in_specs=[
            row(D_MODEL),
            pl.BlockSpec((D_MODEL, C_END), lambda i: (0, 0)),
            pl.BlockSpec((C_U, D_MODEL), lambda i: (0, 0)),
            pl.BlockSpec((tm, 2 * LANES), lambda i: (i % nb, 0)),
            pl.BlockSpec((2 * LANES, tm), lambda i: (0, i % nb)),
        ],
        out_specs=(slab(N_HEADS), slab(IDX_HEADS), col(SUBLANES), row(HEAD_DIM), row(IDX_DIM), col(HEAD_DIM),
                   col(HEAD_DIM), col(HEAD_DIM), col(IDX_DIM), row(POOL_WIDTH)),
        out_shape=out_shape,
        compiler_params=_params(("parallel",)),
        name="proj_prompt",
    )(x, w_big, w_t, cs, cs.T)


def _float_of_rank(u):
    key = u ^ INT_MIN
    bits = jnp.where(key < 0, INT_MIN - key, key)
    return pltpu.bitcast(bits, F32)


def _count(mask):
    return jnp.sum(mask.astype(F32), axis=1, keepdims=True)


def _topk_bias(sc_ref, j_ref, adm, n_adm, lc, k):
    rows = sc_ref.shape[0]
    kf = float(k)

    def value_step(i, t_u):
        hi = jnp.left_shift(jnp.int32(1), 31 - 2 * i)
        lo = jnp.left_shift(jnp.int32(1), 30 - 2 * i)
        for cand_u in (t_u | lo, t_u | hi, t_u | hi | lo):
            cnt = _count(sc_ref[:, 0:lc] >= _float_of_rank(cand_u))
            t_u = jnp.where(cnt >= kf, cand_u, t_u)
        return t_u

    t_u = lax.fori_loop(0, 16, value_step, jnp.zeros((rows, 1), I32))
    few = n_adm < k
    thr = jnp.where(few, -jnp.inf, _float_of_rank(t_u))
    sc = sc_ref[:, 0:lc]
    cnt_gt = _count(sc > thr)
    cnt_eq = _count(sc == thr)
    need = kf - cnt_gt
    cut_needed = jnp.logical_and(cnt_gt + cnt_eq > kf, jnp.logical_not(few))
    any_cut = jnp.max(cut_needed.astype(F32)) > 0.0
    idx = lax.broadcasted_iota(I32, (rows, lc), 1)
    nbits = int(np.ceil(np.log2(lc)))

    j_ref[...] = jnp.full((rows, 1), lc, I32)

    @pl.when(any_cut)
    def _():
        def index_step(i, j):
            cand = j | jnp.left_shift(jnp.int32(1), nbits - 1 - i)
            c = _count(jnp.logical_and(sc_ref[:, 0:lc] == thr, idx < cand))
            return jnp.where(c < need, cand, j)

        j_ref[...] = lax.fori_loop(0, nbits, index_step, jnp.zeros((rows, 1), I32))

    sel = jnp.logical_or(sc > thr, jnp.logical_and(sc == thr, idx <= j_ref[...]))
    return jnp.where(jnp.logical_and(sel, adm), 0.0, NEG_BIG)


ATTN_CHUNK = 256


def _attn_prompt_block(n_chunks, q0, top_k, qt_ref, qit_ref, wit_ref, kb_ref, kib_ref, vbt_ref, o_ref,
                       key_ref, bias_ref, lg_ref, j_ref):
    tq, ch = Q_BLOCK, ATTN_CHUNK
    kf = float(top_k)
    kpos = lax.broadcasted_iota(I32, (ch, tq), 0)
    qpos = q0 + lax.broadcasted_iota(I32, (ch, tq), 1)

    def rows(c):
        return slice(c * ch, (c + 1) * ch)

    def fold(x, op):
        return op(x.reshape(ch // SUBLANES, SUBLANES, tq), axis=0)

    def head(x, h):
        return x[:, h * tq:(h + 1) * tq]

    qit = qit_ref[...]
    wit = wit_ref[...]
    for c in range(n_chunks if n_chunks * ch > top_k else 0):
        d = jnp.dot(kib_ref[rows(c), :], qit, preferred_element_type=F32)
        s = wit[0:1, :] * jnp.maximum(head(d, 0), 0.0)
        for h in range(1, IDX_HEADS):
            s = s + wit[h:h + 1, :] * jnp.maximum(head(d, h), 0.0)
        key_ref[rows(c), :] = jnp.where(c * ch + kpos <= qpos, s, -jnp.inf)

    def count(pred):
        acc = jnp.zeros((SUBLANES, tq), F32)
        for c in range(n_chunks):
            acc = acc + fold(pred(key_ref[rows(c), :], c).astype(F32), jnp.sum)
        return jnp.sum(acc, axis=0, keepdims=True)

    if n_chunks * ch <= top_k:
        for c in range(n_chunks):
            bias_ref[rows(c), :] = jnp.where(c * ch + kpos <= qpos, 0.0, NEG_BIG)
    else:
        def value_step(i, carry):
            t_u, n_ge = carry
            cand_u = t_u | jnp.left_shift(jnp.int32(1), 31 - i)
            cand = _float_of_rank(cand_u)
            cnt = count(lambda k, c: k >= cand)
            ok = cnt >= kf
            return jnp.where(ok, cand_u, t_u), jnp.where(ok, cnt, n_ge)

        t_u, n_ge = lax.fori_loop(0, 32, value_step,
                                  (jnp.zeros((1, tq), I32), jnp.full((1, tq), float(n_chunks * ch), F32)))
        few = qpos[0:1, :] + 1 <= top_k
        thr = jnp.where(few, -jnp.inf, _float_of_rank(t_u))
        cut_needed = jnp.logical_and(n_ge > kf, jnp.logical_not(few))
        any_cut = jnp.max(cut_needed.astype(F32)) > 0.0

        nbits = int(np.ceil(np.log2(n_chunks * ch)))
        j_ref[...] = jnp.full(j_ref.shape, n_chunks * ch, I32)

        @pl.when(any_cut)
        def _():
            need = kf - count(lambda k, c: k > thr)

            def index_step(i, j):
                cand = j | jnp.left_shift(jnp.int32(1), nbits - 1 - i)
                n_before = count(lambda k, c: jnp.logical_and(k == thr, c * ch + kpos < cand))
                return jnp.where(n_before < need, cand, j)

            j = lax.fori_loop(0, nbits, index_step, jnp.zeros((1, tq), I32))
            j_ref[...] = jnp.broadcast_to(j, j_ref.shape)

        j_cut = j_ref[0:1, :]
        for c in range(n_chunks):
            k = key_ref[rows(c), :]
            pos = c * ch + kpos
            sel = jnp.logical_or(k > thr, jnp.logical_and(k == thr, pos <= j_cut))
            bias_ref[rows(c), :] = jnp.where(jnp.logical_and(sel, pos <= qpos), 0.0, NEG_BIG)

    qt = qt_ref[...]
    mx = [jnp.full((SUBLANES, tq), -jnp.inf, F32) for _ in range(N_HEADS)]
    for c in range(n_chunks):
        lg = jnp.dot(kb_ref[rows(c), :], qt, preferred_element_type=F32)
        bias = bias_ref[rows(c), :]
        for h in range(N_HEADS):
            lgh = head(lg, h) + bias
            lg_ref[h, rows(c), :] = lgh
            mx[h] = jnp.maximum(mx[h], fold(lgh, jnp.max))

    outs = []
    for h in range(N_HEADS):
        m = jnp.max(mx[h], axis=0, keepdims=True)
        lsum = jnp.zeros((SUBLANES, tq), F32)
        ot = jnp.zeros((HEAD_DIM, tq), F32)
        for c in range(n_chunks):
            p = jnp.exp2(lg_ref[h, rows(c), :] - m)
            lsum = lsum + fold(p, jnp.sum)
            ot = ot + jnp.dot(vbt_ref[:, rows(c)], p.astype(BF16), preferred_element_type=F32)
        outs.append(ot / jnp.sum(lsum, axis=0, keepdims=True))
    o_ref[...] = jnp.concatenate(outs, axis=0).T.astype(BF16)


def _attn_prompt_kernel(qt_ref, qit_ref, wit_ref, kb_ref, kib_ref, vbt_ref, o_ref, key_ref, bias_ref, lg_ref, j_ref,
                        *, top_k):
    jq = pl.program_id(1)
    blocks_per_chunk = ATTN_CHUNK // Q_BLOCK
    n_classes = key_ref.shape[0] // ATTN_CHUNK
    for cls in range(n_classes):
        @pl.when(jq // blocks_per_chunk == cls)
        def _(cls=cls):
            _attn_prompt_block(cls + 1, jq * Q_BLOCK, top_k, qt_ref, qit_ref, wit_ref, kb_ref, kib_ref, vbt_ref,
                               o_ref, key_ref, bias_ref, lg_ref, j_ref)


def _attn_prompt(qt, qit, wit, kb, kib, vbt):
    batch, _, seq = vbt.shape
    nb = seq // Q_BLOCK
    top_k = min(TOP_K_MAX, seq // 4)
    slab = lambda heads: pl.BlockSpec((None, HEAD_DIM, heads * Q_BLOCK), lambda b, j: (b * nb + j, 0, 0))
    keys = pl.BlockSpec((seq, HEAD_DIM), lambda b, j: (b, 0))
    return pl.pallas_call(
        functools.partial(_attn_prompt_kernel, top_k=top_k),
        grid=(batch, nb),
        in_specs=[slab(N_HEADS), slab(IDX_HEADS), pl.BlockSpec((None, SUBLANES, Q_BLOCK), lambda b, j: (b, 0, j)),
                  keys, keys, pl.BlockSpec((None, HEAD_DIM, seq), lambda b, j: (b, 0, 0))],
        out_specs=pl.BlockSpec((Q_BLOCK, ATT_WIDTH), lambda b, j: (b * nb + j, 0)),
        out_shape=jax.ShapeDtypeStruct((batch * seq, ATT_WIDTH), BF16),
        scratch_shapes=[pltpu.VMEM((seq, Q_BLOCK), F32), pltpu.VMEM((seq, Q_BLOCK), F32),
                        pltpu.VMEM((N_HEADS, seq, Q_BLOCK), F32), pltpu.VMEM((SUBLANES, Q_BLOCK), I32)],
        compiler_params=_params(("parallel", "arbitrary")),
        name="attn_prompt",
    )(qt, qit, wit, kb, kib, vbt)


SAMPLE_CHUNK = 1024
SAMPLE_ROWS_PER_STEP = 2


def _attn_sample_kernel(pt_ref, q_ref, qi_ref, wi_ref, kn_ref, vn_ref, kin_ref, ck_hbm, cv_hbm, cki_hbm, o_ref,
                        kbuf, vbuf, kibuf, sem, key_scr, bias_scr, lg_scr, j_scr, *, n_pages, page, t_new, top_k):
    b = pl.program_id(0)
    n_b = pl.num_programs(0)
    slot = b % 2
    per_step = q_ref.shape[0]
    past = n_pages * page
    lc = past + page
    n_chunks = past // SAMPLE_CHUNK

    def page_copies(step, sl, p):
        dst = pl.ds(pl.multiple_of(p * page, page), page)
        copies = []
        for r in range(per_step):
            phys = pt_ref[(step * per_step + r) * n_pages + p]
            copies += [pltpu.make_async_copy(src.at[phys], buf.at[sl, r, :, dst], sem.at[i, sl])
                       for i, (src, buf) in enumerate(((ck_hbm, kbuf), (cv_hbm, vbuf), (cki_hbm, kibuf)))]
        return copies

    def start_batch(bb, sl):
        def body(p, carry):
            for cp in page_copies(bb, sl, p):
                cp.start()
            return carry
        lax.fori_loop(0, n_pages, body, 0)

    def wait_batch(bb, sl):
        def body(p, carry):
            for cp in page_copies(bb, sl, p):
                cp.wait()
            return carry
        lax.fori_loop(0, n_pages, body, 0)

    @pl.when(b == 0)
    def _():
        start_batch(0, 0)

    @pl.when(b + 1 < n_b)
    def _():
        start_batch(b + 1, 1 - slot)

    wait_batch(b, slot)

    def head_sum(r, d):
        x = wi_ref[r] * jnp.maximum(d, 0.0)
        s = x[0:t_new]
        for h in range(1, IDX_HEADS):
            s = s + x[h * t_new:(h + 1) * t_new]
        return s

    def new_rows(ref, r):
        pad = jnp.zeros((page - t_new, ref.shape[2]), F32)
        return jnp.concatenate([ref[r], pad], axis=0).astype(BF16)

    adm_new = lax.broadcasted_iota(I32, (t_new, page), 1) <= lax.broadcasted_iota(I32, (t_new, page), 0)
    for r in range(per_step):
        qrows = slice(r * t_new, (r + 1) * t_new)
        qi = qi_ref[r]
        for c in range(n_chunks):
            sl = slice(c * SAMPLE_CHUNK, (c + 1) * SAMPLE_CHUNK)
            d = jnp.dot(qi, kibuf[slot, r, :, sl].astype(BF16), preferred_element_type=F32)
            key_scr[qrows, sl] = head_sum(r, d)
        d_new = lax.dot_general(qi, new_rows(kin_ref, r), NT_DIMS, preferred_element_type=F32)
        key_scr[qrows, past:lc] = jnp.where(adm_new, head_sum(r, d_new), -jnp.inf)

    n_q = per_step * t_new
    idx = lax.broadcasted_iota(I32, (n_q, lc), 1)
    trow = lax.broadcasted_iota(I32, (n_q, lc), 0) % t_new
    n_adm = past + 1 + lax.broadcasted_iota(I32, (n_q, 1), 0) % t_new
    bias_scr[...] = _topk_bias(key_scr, j_scr, idx - past <= trow, n_adm, lc, top_k)

    for r in range(per_step):
        q = q_ref[r]

        def bias_rows(sl, r=r):
            return jnp.concatenate([bias_scr[r * t_new:(r + 1) * t_new, sl]] * N_HEADS, axis=0)

        m = jnp.full((N_HEADS * t_new, 1), -jnp.inf, F32)
        for c in range(n_chunks):
            sl = slice(c * SAMPLE_CHUNK, (c + 1) * SAMPLE_CHUNK)
            lg = jnp.dot(q, kbuf[slot, r, :, sl].astype(BF16), preferred_element_type=F32) + bias_rows(sl)
            lg_scr[:, sl] = lg
            m = jnp.maximum(m, jnp.max(lg, axis=1, keepdims=True))
        lg_new = (lax.dot_general(q, new_rows(kn_ref, r), NT_DIMS, preferred_element_type=F32)
                  + bias_rows(slice(past, lc)))
        m = jnp.maximum(m, jnp.max(lg_new, axis=1, keepdims=True))

        p_new = jnp.exp2(lg_new - m)
        l = jnp.sum(p_new, axis=1, keepdims=True)
        o = jnp.dot(p_new.astype(BF16), new_rows(vn_ref, r), preferred_element_type=F32)
        for c in range(n_chunks):
            sl = slice(c * SAMPLE_CHUNK, (c + 1) * SAMPLE_CHUNK)
            pr = jnp.exp2(lg_scr[:, sl] - m)
            l = l + jnp.sum(pr, axis=1, keepdims=True)
            o = o + lax.dot_general(pr.astype(BF16), vbuf[slot, r, :, sl].astype(BF16), NT_DIMS,
                                    preferred_element_type=F32)
        o_ref[r] = o / l


def _attn_sample(page_table, q_hq, qi_hq, wi_hq, k_new, v_new, ki_new, cache_kt, cache_vt, cache_kit):
    db, n_pages = page_table.shape
    page = cache_kt.shape[2]
    t_new = k_new.shape[1]
    past = n_pages * page
    lc = past + page
    top_k = min(TOP_K_MAX, (past + t_new) // 4)
    rows = SAMPLE_ROWS_PER_STEP
    per_b = lambda r, w: pl.BlockSpec((rows, r, w), lambda b, pt: (b, 0, 0))
    hbm = pl.BlockSpec(memory_space=pl.ANY)
    kern = functools.partial(_attn_sample_kernel, n_pages=n_pages, page=page, t_new=t_new, top_k=top_k)
    slab = pltpu.VMEM((2, rows, HEAD_DIM, past), F32)
    grid_spec = pltpu.PrefetchScalarGridSpec(
        num_scalar_prefetch=1,
        grid=(db // rows,),
        in_specs=[per_b(N_HEADS * t_new, HEAD_DIM), per_b(IDX_HEADS * t_new, IDX_DIM), per_b(IDX_HEADS * t_new, 1),
                  per_b(t_new, HEAD_DIM), per_b(t_new, HEAD_DIM), per_b(t_new, IDX_DIM),
                  hbm, hbm, hbm],
        out_specs=per_b(N_HEADS * t_new, HEAD_DIM),
        scratch_shapes=[slab, slab, slab, pltpu.SemaphoreType.DMA((3, 2)),
                        pltpu.VMEM((rows * t_new, lc), F32), pltpu.VMEM((rows * t_new, lc), F32),
                        pltpu.VMEM((N_HEADS * t_new, past), F32), pltpu.VMEM((rows * t_new, 1), I32)],
    )
    return pl.pallas_call(
        kern,
        grid_spec=grid_spec,
        out_shape=jax.ShapeDtypeStruct((db, N_HEADS * t_new, HEAD_DIM), F32),
        compiler_params=_params(("arbitrary",)),
        name="attn_sample",
    )(page_table.reshape(-1), q_hq, qi_hq, wi_hq, k_new, v_new, ki_new, cache_kt, cache_vt, cache_kit)


PREV_ROWS = 16


def _pool_kernel(prev_ref, u_ref, wg_ref, sc_ref, o_ref, ext_ref, *, pos0):
    per_step, t_len, _ = u_ref.shape
    pos = pos0 + lax.broadcasted_iota(I32, (t_len, 1), 0)
    for b in range(per_step):
        ext_ref[0:PREV_ROWS, :] = prev_ref[b]
        ext_ref[PREV_ROWS:PREV_ROWS + t_len, :] = u_ref[b]
        for g, w in enumerate(POOL_WINDOWS):
            sl = slice(g * POOL_GW, (g + 1) * POOL_GW)
            u_new = ext_ref[PREV_ROWS:PREV_ROWS + t_len, sl]
            win = u_new
            for back in range(1, w):
                win = win + ext_ref[PREV_ROWS - back:PREV_ROWS - back + t_len, sl]
            count = jnp.minimum(pos + 1, w).astype(F32)
            r = win / count - u_new
            mixed = jnp.dot(r.astype(BF16), wg_ref[g], preferred_element_type=F32) * sc_ref[:, sl]
            o_ref[b, :, sl] = mixed.astype(BF16)


def _pool(prev, u, w_grp, scale, pos0, per_step):
    nb, t_len, _ = u.shape
    seqs = lambda rows: pl.BlockSpec((per_step, rows, POOL_WIDTH), lambda b: (b, 0, 0))
    return pl.pallas_call(
        functools.partial(_pool_kernel, pos0=pos0),
        grid=(nb // per_step,),
        in_specs=[seqs(PREV_ROWS), seqs(t_len),
                  pl.BlockSpec((POOL_GROUPS, POOL_GW, POOL_GW), lambda b: (0, 0, 0)),
                  pl.BlockSpec((1, POOL_WIDTH), lambda b: (0, 0))],
        out_specs=seqs(t_len),
        out_shape=jax.ShapeDtypeStruct((nb, t_len, POOL_WIDTH), BF16),
        scratch_shapes=[pltpu.VMEM((PREV_ROWS + t_len, POOL_WIDTH), F32)],
        compiler_params=_params(("parallel",)),
        name="pool",
    )(prev, u, w_grp, scale)


def _merge_kernel(x_ref, a_ref, p_ref, wga_ref, wgb_ref, wao_ref, wpo_ref, wo_ref, g_ref, b_ref, h_ref, hp_ref, *,
                  alpha):
    x = x_ref[...]
    xb = x.astype(BF16)
    ga = jnp.dot(xb, wga_ref[...], preferred_element_type=F32)
    gb = jnp.dot(xb, wgb_ref[...], preferred_element_type=F32)
    ya = jnp.dot(a_ref[...], wao_ref[...], preferred_element_type=F32)
    yp = jnp.dot(p_ref[...], wpo_ref[...], preferred_element_type=F32)
    mix = jax.nn.sigmoid(ga) * ya + jax.nn.sigmoid(gb) * yp
    out = jnp.dot(mix.astype(BF16), wo_ref[...], preferred_element_type=F32)
    h = _layer_norm(alpha * x + out, g_ref[...], b_ref[...])
    h_ref[...] = h
    hp_ref[...] = _pack_rows(h)


def _merge(x, attn, pool, wga, wgb, wao, wpo, wo, g, b, tm, alpha):
    n = x.shape[0]
    row = lambda w: pl.BlockSpec((tm, w), lambda i: (i, 0))
    full = lambda r, c: pl.BlockSpec((r, c), lambda i: (0, 0), pipeline_mode=pl.Buffered(1))
    return pl.pallas_call(
        functools.partial(_merge_kernel, alpha=alpha),
        grid=(n // tm,),
        in_specs=[row(D_MODEL), row(ATT_WIDTH), row(POOL_WIDTH), full(D_MODEL, D_MODEL), full(D_MODEL, D_MODEL),
                  full(ATT_WIDTH, D_MODEL), full(POOL_WIDTH, D_MODEL), full(D_MODEL, D_MODEL),
                  full(1, D_MODEL), full(1, D_MODEL)],
        out_specs=(row(D_MODEL), row(PACKED)),
        out_shape=(jax.ShapeDtypeStruct((n, D_MODEL), F32), jax.ShapeDtypeStruct((n, PACKED), I32)),
        compiler_params=_params(("parallel",)),
        name="merge",
    )(x, attn, pool, wga, wgb, wao, wpo, wo, g, b)


def _route(h, wr_t, bias_col):
    tm = h.shape[0]
    logits = lax.dot_general(wr_t, h.astype(BF16), NT_DIMS, preferred_element_type=F32)
    s = jax.nn.sigmoid(logits)
    sb = s + bias_col
    neg_inf = -jnp.inf

    rows = []
    for g in range(N_GROUPS):
        blk = sb[g * GROUP_SIZE:(g + 1) * GROUP_SIZE, :]
        m1 = jnp.max(blk, axis=0, keepdims=True)
        is_m1 = blk == m1
        n_m1 = jnp.sum(is_m1.astype(F32), axis=0, keepdims=True)
        m2 = jnp.max(jnp.where(is_m1, neg_inf, blk), axis=0, keepdims=True)
        rows.append(m1 + jnp.where(n_m1 >= 2.0, m1, m2))
    gs = jnp.concatenate(rows, axis=0)

    gi = lax.broadcasted_iota(I32, (N_GROUPS, tm), 0)
    rank = jnp.zeros((N_GROUPS, tm), F32)
    for g in range(N_GROUPS):
        row = gs[g:g + 1, :]
        beats = jnp.logical_or(row > gs, jnp.logical_and(row == gs, g < gi))
        rank = rank + beats.astype(F32)
    gkeep = rank < float(TOPK_GROUPS)
    emask = jnp.concatenate(
        [jnp.broadcast_to(gkeep[g:g + 1, :], (GROUP_SIZE, tm)) for g in range(N_GROUPS)], axis=0)

    ei = lax.broadcasted_iota(I32, (N_EXPERTS, tm), 0)
    x = jnp.where(emask, sb, neg_inf)
    sel = jnp.zeros((N_EXPERTS, tm), jnp.bool_)
    picks = []
    for _ in range(TOP_K_EXPERTS):
        m = jnp.max(x, axis=0, keepdims=True)
        first = jnp.min(jnp.where(x == m, ei, N_EXPERTS), axis=0, keepdims=True)
        pick = ei == first
        sel = jnp.logical_or(sel, pick)
        x = jnp.where(pick, neg_inf, x)
        picks.append(first)

    gate = jnp.where(sel, s, 0.0)
    comb = gate / jnp.sum(gate, axis=0, keepdims=True) * ROUTED_SCALE
    return comb, sel, picks


def _router_kernel(h_ref, wr_ref, bias_ref, c_ref):
    comb, _, _ = _route(h_ref[...], wr_ref[...], bias_ref[...])
    comb = jnp.concatenate([comb, jnp.zeros((LANES - N_EXPERTS, comb.shape[1]), F32)], axis=0)
    c_ref[...] = comb.T


def _router(h, wr_t, bias_col, tm):
    n = h.shape[0]
    return pl.pallas_call(
        _router_kernel,
        grid=(n // tm,),
        in_specs=[pl.BlockSpec((tm, D_MODEL), lambda i: (i, 0)),
                  pl.BlockSpec((N_EXPERTS, D_MODEL), lambda i: (0, 0)),
                  pl.BlockSpec((N_EXPERTS, 1), lambda i: (0, 0))],
        out_specs=pl.BlockSpec((tm, LANES), lambda i: (i, 0)),
        out_shape=jax.ShapeDtypeStruct((n, LANES), F32),
        compiler_params=_params(("parallel",)),
        name="router",
    )(h, wr_t, bias_col)


def _swiglu(xb, w13, w2, hidden):
    ab = jnp.dot(xb, w13, preferred_element_type=F32)
    act = jax.nn.silu(ab[:, 0:hidden]) * ab[:, hidden:2 * hidden]
    return jnp.dot(act.astype(BF16), w2, preferred_element_type=F32)


def _moe_kernel(h_ref, c_ref, ws13_ref, ws2_ref, w13_ref, w2_ref, y_ref, hb_ref):
    e = pl.program_id(1)

    @pl.when(e == 0)
    def _():
        hb_ref[...] = h_ref[...].astype(BF16)
        y_ref[...] = _swiglu(hb_ref[...], ws13_ref[...], ws2_ref[...], SHARED_DIM)

    ye = _swiglu(hb_ref[...], w13_ref[...].astype(BF16), w2_ref[...].astype(BF16), EXPERT_DIM)
    lane = lax.broadcasted_iota(I32, c_ref.shape, 1)
    ce = jnp.sum(jnp.where(lane == e, c_ref[...], 0.0), axis=1, keepdims=True)
    y_ref[...] += ce * ye


def _moe(h, comb, ws13, ws2, w13, w2, tm):
    n = h.shape[0]
    return pl.pallas_call(
        _moe_kernel,
        grid=(n // tm, N_EXPERTS),
        in_specs=[pl.BlockSpec((tm, D_MODEL), lambda i, e: (i, 0)),
                  pl.BlockSpec((tm, LANES), lambda i, e: (i, 0)),
                  pl.BlockSpec((D_MODEL, 2 * SHARED_DIM), lambda i, e: (0, 0)),
                  pl.BlockSpec((SHARED_DIM, D_MODEL), lambda i, e: (0, 0)),
                  pl.BlockSpec((None, D_MODEL, 2 * EXPERT_DIM), lambda i, e: (e, 0, 0)),
                  pl.BlockSpec((None, EXPERT_DIM, D_MODEL), lambda i, e: (e, 0, 0))],
        out_specs=pl.BlockSpec((tm, D_MODEL), lambda i, e: (i, 0)),
        out_shape=jax.ShapeDtypeStruct((n, D_MODEL), F32),
        scratch_shapes=[pltpu.VMEM((tm, D_MODEL), BF16)],
        compiler_params=_params(("parallel", "arbitrary")),
        name="moe",
    )(h, comb, ws13, ws2, w13, w2)


def _final_kernel(h_ref, y_ref, pe_ref, g_ref, b_ref, wpg_ref, wpi_ref, o_ref, *, alpha):
    z = _layer_norm(alpha * h_ref[...] + y_ref[...], g_ref[...], b_ref[...])
    gate = jax.nn.sigmoid(jnp.dot(z.astype(BF16), wpg_ref[...], preferred_element_type=F32))
    emb = jnp.dot(pe_ref[...].astype(BF16), wpi_ref[...], preferred_element_type=F32)
    o_ref[...] = z + gate * emb


def _final(h, y, pe, g, b, wpg, wpi, tm, alpha):
    n = h.shape[0]
    row = lambda w: pl.BlockSpec((tm, w), lambda i: (i, 0))
    full = lambda r, c: pl.BlockSpec((r, c), lambda i: (0, 0))
    return pl.pallas_call(
        functools.partial(_final_kernel, alpha=alpha),
        grid=(n // tm,),
        in_specs=[row(D_MODEL), row(D_MODEL), row(PLE_DIM), full(1, D_MODEL), full(1, D_MODEL),
                  full(D_MODEL, D_MODEL), full(PLE_DIM, D_MODEL)],
        out_specs=row(D_MODEL),
        out_shape=jax.ShapeDtypeStruct((n, D_MODEL), F32),
        compiler_params=_params(("parallel",)),
        name="final",
    )(h, y, pe, g, b, wpg, wpi)


MOE_BLOCK = 2176


def _sorted_rows(n_tokens):
    worst = n_tokens * TOP_K_EXPERTS + N_EXPERTS * MOE_BLOCK
    return -(-worst // MOE_BLOCK) * MOE_BLOCK


def _dispatch_kernel(h_ref, wr_ref, bias_ref, tri_ref, pos_ref, gate_ref, blk_ref, used_ref,
                     eidx_s, rank_s, gate_s, cnt_s):
    p = pl.program_id(0)
    i = pl.program_id(1)
    tm = h_ref.shape[0]
    ei = lax.broadcasted_iota(I32, (N_EXPERTS, tm), 0)

    @pl.when(p == 0)
    def _():
        comb, sel, picks = _route(h_ref[...], wr_ref[...], bias_ref[...])
        before = jnp.dot(sel.astype(BF16), tri_ref[...], preferred_element_type=F32)
        ranks, gates = [], []
        for first in picks:
            pick = ei == first
            ranks.append(jnp.sum(jnp.where(pick, before, 0.0), axis=0, keepdims=True))
            gates.append(jnp.sum(jnp.where(pick, comb, 0.0), axis=0, keepdims=True))
        eidx_s[i] = jnp.concatenate(picks, axis=0)
        rank_s[i] = jnp.concatenate(ranks, axis=0)
        gate_s[i] = jnp.concatenate(gates, axis=0)
        cnt_s[i] = jnp.broadcast_to(jnp.sum(sel.astype(F32), axis=1, keepdims=True), (N_EXPERTS, LANES))

    @pl.when(p == 1)
    def _():
        cnt = cnt_s[...]
        tile_id = lax.broadcasted_iota(I32, cnt.shape, 0)
        total = jnp.sum(cnt, axis=0)
        prior = jnp.sum(jnp.where(tile_id < i, cnt, 0.0), axis=0)
        seg_blk = jnp.maximum(jnp.ceil(total * (1.0 / MOE_BLOCK) - 0.25 / MOE_BLOCK), 1.0)
        lower = (lax.broadcasted_iota(I32, (N_EXPERTS, N_EXPERTS), 1)
                 < lax.broadcasted_iota(I32, (N_EXPERTS, N_EXPERTS), 0)).astype(F32)
        off_blk = jnp.dot(lower, seg_blk, precision=lax.Precision.HIGHEST, preferred_element_type=F32)
        seg_off = off_blk * MOE_BLOCK
        base = (seg_off + prior)[:, 0:1]
        eidx = eidx_s[i]
        rank = rank_s[i]
        rows = []
        for k in range(TOP_K_EXPERTS):
            pick = ei == eidx[k:k + 1, :]
            rows.append(rank[k:k + 1, :] + jnp.sum(jnp.where(pick, base, 0.0), axis=0, keepdims=True))
        pos_ref[...] = jnp.concatenate(rows, axis=0).astype(I32)
        gate_ref[...] = jnp.concatenate([gate_s[i], jnp.zeros((LANES - TOP_K_EXPERTS, tm), F32)], axis=0).T

        end_blk = (off_blk + seg_blk)[:, 0:1]
        n_blk = blk_ref.shape[1]
        blk_id = lax.broadcasted_iota(I32, (N_EXPERTS, n_blk), 1).astype(F32)
        owner = jnp.sum((end_blk <= blk_id).astype(F32), axis=0, keepdims=True)
        blk_ref[...] = jnp.minimum(owner, N_EXPERTS - 1.0).astype(I32)
        used_ref[...] = jnp.broadcast_to(end_blk[N_EXPERTS - 1:N_EXPERTS, :], used_ref.shape).astype(I32)


def _dispatch(h, wr_t, bias_col, tm):
    n = h.shape[0]
    n_tiles = n // tm
    n_blk = _sorted_rows(n) // MOE_BLOCK
    n_blk_pad = -(-n_blk // LANES) * LANES
    tri = jnp.triu(jnp.ones((tm, tm), BF16), k=1)
    const = lambda r, c: pl.BlockSpec((r, c), lambda p, i: (0, 0))
    per_tile = lambda dt: pltpu.VMEM((n_tiles, TOP_K_EXPERTS, tm), dt)
    return pl.pallas_call(
        _dispatch_kernel,
        grid=(2, n_tiles),
        in_specs=[pl.BlockSpec((tm, D_MODEL), lambda p, i: (i * (1 - p), 0)),
                  const(N_EXPERTS, D_MODEL), const(N_EXPERTS, 1), const(tm, tm)],
        out_specs=(pl.BlockSpec((TOP_K_EXPERTS, tm), lambda p, i: (0, i * p)),
                   pl.BlockSpec((tm, LANES), lambda p, i: (i * p, 0)),
                   const(1, n_blk_pad), const(1, LANES)),
        out_shape=(jax.ShapeDtypeStruct((TOP_K_EXPERTS, n), I32), jax.ShapeDtypeStruct((n, LANES), F32),
                   jax.ShapeDtypeStruct((1, n_blk_pad), I32), jax.ShapeDtypeStruct((1, LANES), I32)),
        scratch_shapes=[per_tile(I32), per_tile(F32), per_tile(F32), pltpu.VMEM((n_tiles, N_EXPERTS, LANES), F32)],
        compiler_params=_params(("arbitrary", "arbitrary")),
        name="dispatch",
    )(h, wr_t, bias_col, tri)


PACKED = D_MODEL // 2


def _pack_rows(x):
    lo = pltpu.bitcast(x[:, 0:PACKED].astype(BF16).astype(F32), I32)
    hi = pltpu.bitcast(x[:, PACKED:D_MODEL].astype(BF16).astype(F32), I32)
    return jnp.bitwise_or(hi, lax.shift_right_logical(lo, 16))


def _unpack_rows_f32(w):
    lo = pltpu.bitcast(lax.shift_left(w, 16), F32)
    hi = pltpu.bitcast(jnp.bitwise_and(w, -65536), F32)
    return jnp.concatenate([lo, hi], axis=1)


def _unpack_rows(w):
    return _unpack_rows_f32(w).astype(BF16)


GROUPED_RING = 3


def _grouped_kernel(blk_ref, used_ref, anchor_ref, xs_hbm, w13_ref, w2_ref, ys_ref, w13b_ref, w2b_ref, xbuf, sem):
    s = pl.program_id(0)
    used = used_ref[0]

    def row_copy(b):
        slot = b % GROUPED_RING
        return pltpu.make_async_copy(xs_hbm.at[pl.ds(pl.multiple_of(b * MOE_BLOCK, MOE_BLOCK), MOE_BLOCK)],
                                     xbuf.at[slot], sem.at[slot])

    @pl.when(s == 0)
    def _():
        for b in range(GROUPED_RING - 1):
            @pl.when(b < used)
            def _(b=b):
                row_copy(b).start()

    @pl.when(s + GROUPED_RING - 1 < used)
    def _():
        row_copy(s + GROUPED_RING - 1).start()

    @pl.when(s < used)
    def _():
        row_copy(s).wait()
        w13b_ref[...] = w13_ref[...].astype(BF16)
        w2b_ref[...] = w2_ref[...].astype(BF16)
        ys = _swiglu(_unpack_rows(xbuf[s % GROUPED_RING]), w13b_ref[...], w2b_ref[...], EXPERT_DIM)
        ys_ref[...] = _pack_rows(ys)


def _grouped(blk, used, anchor, xs, w13, w2):
    ns = xs.shape[0]
    row_blk = lambda b, blk, used, anchor: (jnp.minimum(b, used[0] - 1), 0)
    expert = lambda b, blk, used, anchor: (blk[b], 0, 0)
    grid_spec = pltpu.PrefetchScalarGridSpec(
        num_scalar_prefetch=3,
        grid=(ns // MOE_BLOCK,),
        in_specs=[pl.BlockSpec(memory_space=pl.ANY),
                  pl.BlockSpec((None, D_MODEL, 2 * EXPERT_DIM), expert),
                  pl.BlockSpec((None, EXPERT_DIM, D_MODEL), expert)],
        out_specs=(pl.BlockSpec((MOE_BLOCK, PACKED), row_blk),
                   pl.BlockSpec((None, D_MODEL, 2 * EXPERT_DIM), expert),
                   pl.BlockSpec((None, EXPERT_DIM, D_MODEL), expert)),
        scratch_shapes=[pltpu.VMEM((GROUPED_RING, MOE_BLOCK, PACKED), I32), pltpu.SemaphoreType.DMA((GROUPED_RING,))],
    )
    return pl.pallas_call(
        _grouped_kernel,
        grid_spec=grid_spec,
        out_shape=(jax.ShapeDtypeStruct((ns, PACKED), I32), jax.ShapeDtypeStruct(w13.shape, BF16),
                   jax.ShapeDtypeStruct(w2.shape, BF16)),
        compiler_params=_params(("arbitrary",)),
        name="grouped",
    )(blk, used, anchor, xs, w13, w2)


SC_WINDOW = 128


def _sc_mesh():
    return plsc.VectorSubcoreMesh(core_axis_name="core", subcore_axis_name="subcore")


def _sc_worker(n_items):
    info = plsc.get_sparse_core_info()
    n_workers = info.num_cores * info.num_subcores
    assert n_items % (SC_WINDOW * n_workers) == 0, "rows must split evenly into windows over the vector subcores"
    wid = lax.axis_index("subcore") * info.num_cores + lax.axis_index("core")
    return wid, n_items // (SC_WINDOW * n_workers)


def _scatter_rows(x, pos, n_out):
    n, width = x.shape
    picks = pos.shape[0]

    @functools.partial(
        pl.kernel, mesh=_sc_mesh(), out_type=jax.ShapeDtypeStruct((n_out, width), I32),
        scratch_types=[pltpu.VMEM((picks, SC_WINDOW), I32), pltpu.VMEM((SC_WINDOW, width), I32)],
        name="scatter_rows")
    def scatter(x_hbm, pos_hbm, out_hbm, idx_v, rows_v):
        wid, n_win = _sc_worker(n)

        @pl.loop(0, n_win)
        def _(j):
            base = (wid * n_win + j) * SC_WINDOW
            pltpu.sync_copy(pos_hbm.at[:, pl.ds(base, SC_WINDOW)], idx_v)
            pltpu.sync_copy(x_hbm.at[pl.ds(base, SC_WINDOW)], rows_v)
            for k in range(picks):
                pltpu.sync_copy(rows_v, out_hbm.at[idx_v.at[k]])

    return scatter(x, pos)


def _gather_rows(src, pos):
    width = src.shape[1]
    picks, n = pos.shape

    @functools.partial(
        pl.kernel, mesh=_sc_mesh(), out_type=jax.ShapeDtypeStruct((picks * n, width), I32),
        scratch_types=[pltpu.VMEM((SC_WINDOW,), I32), pltpu.VMEM((SC_WINDOW, width), I32)],
        name="gather_rows")
    def gather(src_hbm, pos_hbm, out_hbm, idx_v, rows_v):
        wid, n_win = _sc_worker(picks * n)

        @pl.loop(0, n_win)
        def _(j):
            base = (wid * n_win + j) * SC_WINDOW
            pltpu.sync_copy(pos_hbm.at[pl.ds(base, SC_WINDOW)], idx_v)
            pltpu.sync_copy(src_hbm.at[idx_v], rows_v)
            pltpu.sync_copy(rows_v, out_hbm.at[pl.ds(base, SC_WINDOW)])

    return gather(src, pos.reshape(-1)).reshape(picks, n, width)


COMBINE_RING = 3


def _combine_kernel(h_ref, g_hbm, gate_ref, pe_ref, ws13_ref, ws2_ref, ln_g_ref, ln_b_ref, wpg_ref, wpi_ref, o_ref,
                    gbuf, sem, *, alpha):
    s = pl.program_id(0)
    n_steps = pl.num_programs(0)
    tm = h_ref.shape[0]

    def tile_copy(t):
        slot = t % COMBINE_RING
        return pltpu.make_async_copy(g_hbm.at[:, pl.ds(pl.multiple_of(t * tm, tm), tm), :], gbuf.at[slot],
                                     sem.at[slot])

    @pl.when(s == 0)
    def _():
        for t in range(COMBINE_RING - 1):
            @pl.when(t < n_steps)
            def _(t=t):
                tile_copy(t).start()

    @pl.when(s + COMBINE_RING - 1 < n_steps)
    def _():
        tile_copy(s + COMBINE_RING - 1).start()

    h = h_ref[...]
    y = _swiglu(h.astype(BF16), ws13_ref[...], ws2_ref[...], SHARED_DIM)
    gate = gate_ref[...]
    tile_copy(s).wait()
    g_ref = gbuf.at[s % COMBINE_RING]
    for k in range(TOP_K_EXPERTS):
        y = y + gate[:, k:k + 1] * _unpack_rows_f32(g_ref[k])
    z = _layer_norm(alpha * h + y, ln_g_ref[...], ln_b_ref[...])
    ple_gate = jax.nn.sigmoid(jnp.dot(z.astype(BF16), wpg_ref[...], preferred_element_type=F32))
    emb = jnp.dot(pe_ref[...].astype(BF16), wpi_ref[...], preferred_element_type=F32)
    o_ref[...] = z + ple_gate * emb


def _combine(h, gathered, gate, pe, ws13, ws2, g, b, wpg, wpi, tm, alpha):
    n = h.shape[0]
    row = lambda w: pl.BlockSpec((tm, w), lambda i: (i, 0))
    full = lambda r, c: pl.BlockSpec((r, c), lambda i: (0, 0))
    return pl.pallas_call(
        functools.partial(_combine_kernel, alpha=alpha),
        grid=(n // tm,),
        in_specs=[row(D_MODEL), pl.BlockSpec(memory_space=pl.ANY), row(LANES),
                  row(PLE_DIM), full(D_MODEL, 2 * SHARED_DIM), full(SHARED_DIM, D_MODEL),
                  full(1, D_MODEL), full(1, D_MODEL), full(D_MODEL, D_MODEL), full(PLE_DIM, D_MODEL)],
        out_specs=row(D_MODEL),
        out_shape=jax.ShapeDtypeStruct((n, D_MODEL), F32),
        scratch_shapes=[pltpu.VMEM((COMBINE_RING, TOP_K_EXPERTS, tm, PACKED), I32),
                        pltpu.SemaphoreType.DMA((COMBINE_RING,))],
        compiler_params=_params(("arbitrary",)),
        name="combine",
    )(h, gathered, gate, pe, ws13, ws2, g, b, wpg, wpi)


def _rope_table(pos):
    inv = ROPE_THETA ** (-jnp.arange(0, HEAD_DIM, 2, dtype=F32) / HEAD_DIM)
    ang = pos.astype(F32)[:, None] * inv[None, :]
    return jnp.concatenate([jnp.tile(jnp.cos(ang), (1, 4)), jnp.tile(jnp.sin(ang), (1, 4))], axis=1)


def _fused_in_weight(w_in):
    offs = np.cumsum(IN_SIZES)[:-1].tolist()
    wq, wk, wv, wqi, wki, wwi, wu, wga, wgb = jnp.split(w_in, offs, axis=1)
    pad = jnp.zeros((D_MODEL, LANES - HEAD_DIM - IDX_HEADS), w_in.dtype)
    w_big = jnp.concatenate([wq, wqi, wk, wki, wv, wwi, pad, wu], axis=1).astype(BF16)
    return w_big, w_big[:, 0:C_U].T, wga.astype(BF16), wgb.astype(BF16)


def _pages_transposed(cache):
    return jnp.transpose(cache[0], (0, 2, 1))


def _heads_major(a, n_heads):
    b, t, w = a.shape
    d = w // n_heads
    return a.reshape(b, t, n_heads, d).transpose(0, 2, 1, 3).reshape(b, n_heads * t, d)


def kernel(x_prompt, x_sample, cache_k, cache_v, cache_kidx, state_pool, page_table, p_prompt, p_sample, w_in, w_att_out, w_pool_grp, pool_scale, w_pool_out, w_out, ln1_g, ln1_b, w_router, router_bias, w_exp13, w_exp2, w_sh13, w_sh2, ln2_g, ln2_b, w_ple_in, w_ple_gate):
    B, S, D = x_prompt.shape
    DB, T, _ = x_sample.shape
    depth = w_in.shape[0]
    assert depth == 1, "single layer step"
    page = cache_k.shape[2]
    past = page_table.shape[1] * page
    alpha = (2 * depth) ** 0.25
    n_p, n_s = B * S, DB * T

    w_big, w_t, wga, wgb = _fused_in_weight(w_in[0])
    wao, wpo, wo = w_att_out[0].astype(BF16), w_pool_out[0].astype(BF16), w_out[0].astype(BF16)
    wgrp = w_pool_grp[0].astype(BF16)
    pscale = pool_scale[0].reshape(1, POOL_WIDTH)
    g1, b1 = ln1_g[0].reshape(1, D), ln1_b[0].reshape(1, D)
    g2, b2 = ln2_g[0].reshape(1, D), ln2_b[0].reshape(1, D)
    wr_t = w_router[0].T.astype(BF16)
    rbias = router_bias[0].reshape(N_EXPERTS, 1)
    w13, w2 = w_exp13[0], w_exp2[0]
    ws13, ws2 = w_sh13[0].astype(BF16), w_sh2[0].astype(BF16)
    wpg, wpi = w_ple_gate[0].astype(BF16), w_ple_in[0].astype(BF16)

    cs_p = _rope_table(jnp.arange(S, dtype=I32))
    cs_s = jnp.tile(_rope_table(past + jnp.arange(T, dtype=I32)), (DB, 1))

    xp = x_prompt.reshape(n_p, D)
    qt, qit, wit, kb, kib, vbt, kt, vt, kit, u = _proj_prompt(xp, w_big, w_t, cs_p, S, PROJ_TILE)
    attn_p = _attn_prompt(qt, qit, wit, kb, kib, vbt)
    u3 = u.reshape(B, S, POOL_WIDTH)
    pool_p = _pool(jnp.zeros((B, PREV_ROWS, POOL_WIDTH), F32), u3, wgrp, pscale, 0, 1).reshape(n_p, POOL_WIDTH)
    h_p, hp_p = _merge(xp, attn_p, pool_p, wga, wgb, wao, wpo, wo, g1, b1, MERGE_TILE, alpha)

    xs = x_sample.reshape(n_s, D)
    qs, qis, ks, vs, kis, wis, us = _proj_sample(xs, w_big, cs_s)
    q_hq = _heads_major(qs.reshape(DB, T, ATT_WIDTH), N_HEADS)
    qi_hq = _heads_major(qis.reshape(DB, T, IDX_HEADS * IDX_DIM), IDX_HEADS)
    wi_hq = wis.reshape(DB, T, IDX_HEADS).transpose(0, 2, 1).reshape(DB, IDX_HEADS * T, 1)
    caches = (_pages_transposed(cache_k), _pages_transposed(cache_v), _pages_transposed(cache_kidx))
    new_rows = (ks.reshape(DB, T, HEAD_DIM), vs.reshape(DB, T, HEAD_DIM), kis.reshape(DB, T, IDX_DIM))
    half = DB * 5 // 8 // SAMPLE_ROWS_PER_STEP * SAMPLE_ROWS_PER_STEP
    o_halves = [_attn_sample(page_table[sl], q_hq[sl], qi_hq[sl], wi_hq[sl], *(a[sl] for a in new_rows), *caches)
                for sl in (slice(0, half), slice(half, DB))]
    o_hq = jnp.concatenate(o_halves, axis=0)
    attn_s = o_hq.reshape(DB, N_HEADS, T, HEAD_DIM).transpose(0, 2, 1, 3).reshape(n_s, ATT_WIDTH).astype(BF16)
    us3 = us.reshape(DB, T, POOL_WIDTH)
    prev_s = jnp.concatenate([jnp.zeros((DB, PREV_ROWS - POOL_STATE, POOL_WIDTH), F32), state_pool[0]], axis=1)
    pool_s = _pool(prev_s, us3, wgrp, pscale, past, DB).reshape(n_s, POOL_WIDTH)
    h_s, _ = _merge(xs, attn_s, pool_s, wga, wgb, wao, wpo, wo, g1, b1, n_s, alpha)

    pos, gate, blk, used = _dispatch(h_p, wr_t, rbias, DISPATCH_TILE)
    sorted_in = _scatter_rows(hp_p, pos, _sorted_rows(n_p))
    anchor = lax.bitcast_convert_type(o_halves[0][0, 0, 0:1], I32)
    sorted_out, w13_b, w2_b = _grouped(blk.reshape(-1), used.reshape(-1), anchor, sorted_in, w13, w2)
    gathered = _gather_rows(sorted_out, pos)

    comb_s = _router(h_s, wr_t, rbias, n_s)
    y_s = _final(h_s, _moe(h_s, comb_s, ws13, ws2, w13_b, w2_b, n_s), p_sample[0].reshape(n_s, PLE_DIM), g2, b2, wpg,
                 wpi, n_s, alpha)
    y_p = _combine(h_p, gathered, gate, p_prompt[0].reshape(n_p, PLE_DIM), ws13, ws2, g2, b2, wpg, wpi, COMBINE_TILE,
                   alpha)

    ext_s = jnp.concatenate([state_pool[0], us3], axis=1)
    return (y_p.reshape(B, S, D), y_s.reshape(DB, T, D),
            jnp.transpose(kt, (0, 2, 1))[None], jnp.transpose(vt, (0, 2, 1))[None],
            jnp.transpose(kit, (0, 2, 1))[None],
            u3[:, S - POOL_STATE:][None],
            ks.reshape(1, DB, T, HEAD_DIM), vs.reshape(1, DB, T, HEAD_DIM), kis.reshape(1, DB, T, IDX_DIM),
            ext_s[:, T:][None])
```

```python
import functools

import numpy as np
import jax
import jax.numpy as jnp
from jax import lax
from jax.experimental import pallas as pl
from jax.experimental.pallas import tpu as pltpu
from jax.experimental.pallas import tpu_sc as plsc

F32 = jnp.float32
BF16 = jnp.bfloat16
I32 = jnp.int32

D_MODEL = 1024
N_HEADS = 8
HEAD_DIM = 64
ATT_WIDTH = N_HEADS * HEAD_DIM
IDX_HEADS = 4
IDX_DIM = 64
TOP_K_MAX = 256
Q_BLOCK = 256
ROPE_THETA = 10000.0
POOL_WINDOWS = (2, 4, 8, 16)
POOL_GROUPS = 4
POOL_WIDTH = 512
POOL_GW = POOL_WIDTH // POOL_GROUPS
POOL_STATE = 15
N_EXPERTS = 64
TOP_K_EXPERTS = 8
N_GROUPS = 8
GROUP_SIZE = N_EXPERTS // N_GROUPS
TOPK_GROUPS = 4
EXPERT_DIM = 256
SHARED_DIM = 256
ROUTED_SCALE = 2.5
PLE_DIM = 256
LN_EPS = 1e-5
IN_SIZES = (ATT_WIDTH, HEAD_DIM, HEAD_DIM, IDX_HEADS * IDX_DIM, IDX_DIM, IDX_HEADS, POOL_WIDTH, D_MODEL, D_MODEL)

LANES = 128
SUBLANES = 8
INT_MIN = -2147483648
NEG_BIG = -1e30
VMEM_LIMIT = 56 * 1024 * 1024
PROJ_TILE = 512
MERGE_TILE = 1024
DISPATCH_TILE = 1024
COMBINE_TILE = 512

C_Q = 0
C_QI = 512
C_KK = 768
C_VW = 896
C_U = 1024
C_END = 1536
HALF = HEAD_DIM // 2

NT_DIMS = (((1,), (1,)), ((), ()))

Q_SCALE = HEAD_DIM ** -0.5 * float(np.log2(np.e))
QI_SCALE = IDX_DIM ** -0.5


def _params(sem):
    return pltpu.CompilerParams(dimension_semantics=sem, vmem_limit_bytes=VMEM_LIMIT)


def _layer_norm(x, g, b):
    mu = jnp.mean(x, axis=-1, keepdims=True)
    xc = x - mu
    var = jnp.mean(xc * xc, axis=-1, keepdims=True)
    return xc * lax.rsqrt(var + LN_EPS) * g + b


def _rope_rows(a, cos, sin):
    first_half = lax.broadcasted_iota(I32, (a.shape[0], LANES), 1) % HEAD_DIM < HALF
    out = []
    for s in range(a.shape[1] // LANES):
        x = a[:, s * LANES:(s + 1) * LANES]
        rot = jnp.where(first_half, -pltpu.roll(x, LANES - HALF, axis=1), pltpu.roll(x, HALF, axis=1))
        out.append(x * cos + rot * sin)
    return out[0] if len(out) == 1 else jnp.concatenate(out, axis=1)


def _proj_sample_kernel(x_ref, w_ref, cs_ref, q_ref, qi_ref, k_ref, v_ref, ki_ref, wi_ref, u_ref):
    xb = x_ref[...].astype(BF16)
    cos = cs_ref[:, 0:LANES]
    sin = cs_ref[:, LANES:2 * LANES]

    def mm(c0, n):
        return jnp.dot(xb, w_ref[:, c0:c0 + n], preferred_element_type=F32)

    def rope(c0, n):
        return _rope_rows(mm(c0, n), cos, sin)

    q_ref[...] = (rope(C_Q, ATT_WIDTH) * Q_SCALE).astype(BF16)
    qi_ref[...] = (rope(C_QI, IDX_HEADS * IDX_DIM) * QI_SCALE).astype(BF16)
    kk = rope(C_KK, LANES)
    k_ref[...] = kk[:, 0:HEAD_DIM]
    ki_ref[...] = kk[:, HEAD_DIM:2 * HEAD_DIM]
    vw = mm(C_VW, LANES)
    v_ref[...] = vw[:, 0:HEAD_DIM]
    wi_ref[...] = vw[:, HEAD_DIM:HEAD_DIM + IDX_HEADS] * (IDX_HEADS ** -0.5)
    u_ref[...] = mm(C_U, POOL_WIDTH)


def _proj_sample(x, w_big, cs):
    n = x.shape[0]
    full = lambda r, c: pl.BlockSpec((r, c), lambda i: (0, 0))
    widths = (ATT_WIDTH, IDX_HEADS * IDX_DIM, HEAD_DIM, HEAD_DIM, IDX_DIM, IDX_HEADS, POOL_WIDTH)
    dtypes = (BF16, BF16, F32, F32, F32, F32, F32)
    return pl.pallas_call(
        _proj_sample_kernel,
        grid=(1,),
        in_specs=[full(n, D_MODEL), full(D_MODEL, C_END), full(n, 2 * LANES)],
        out_specs=tuple(full(n, w) for w in widths),
        out_shape=tuple(jax.ShapeDtypeStruct((n, w), dt) for w, dt in zip(widths, dtypes)),
        compiler_params=_params(("arbitrary",)),
        name="proj_sample",
    )(x, w_big, cs)


def _proj_prompt_kernel(x_ref, w_ref, wt_ref, cs_ref, cst_ref, qt_ref, qit_ref, wit_ref, kb_ref, kib_ref, vbt_ref,
                        kt_ref, vt_ref, kit_ref, u_ref):
    xb = x_ref[...].astype(BF16)
    tm = xb.shape[0]
    cos = cs_ref[:, 0:LANES]
    sin = cs_ref[:, LANES:2 * LANES]
    cos_t = cst_ref[0:HEAD_DIM, :]
    sin_t = cst_ref[LANES:LANES + HEAD_DIM, :]

    def mm(c0, n):
        return jnp.dot(xb, w_ref[:, c0:c0 + n], preferred_element_type=F32)

    def mm_t(c0, n):
        return lax.dot_general(wt_ref[c0:c0 + n, :], xb, NT_DIMS, preferred_element_type=F32)

    def rope_t(c0, heads):
        a = mm_t(c0, heads * HEAD_DIM)
        parts = []
        for h in range(heads):
            x1 = a[h * HEAD_DIM:h * HEAD_DIM + HALF, :]
            x2 = a[h * HEAD_DIM + HALF:(h + 1) * HEAD_DIM, :]
            rot = jnp.concatenate([-x2, x1], axis=0)
            parts.append(a[h * HEAD_DIM:(h + 1) * HEAD_DIM, :] * cos_t + rot * sin_t)
        return parts[0] if heads == 1 else jnp.concatenate(parts, axis=0)

    kk = _rope_rows(mm(C_KK, LANES), cos, sin)
    kb_ref[...] = kk[:, 0:HEAD_DIM].astype(BF16)
    kib_ref[...] = kk[:, HEAD_DIM:2 * HEAD_DIM].astype(BF16)
    u_ref[...] = mm(C_U, POOL_WIDTH)

    qt = (rope_t(C_Q, N_HEADS) * Q_SCALE).astype(BF16)
    qit = (rope_t(C_QI, IDX_HEADS) * QI_SCALE).astype(BF16)
    for blk in range(tm // Q_BLOCK):
        cols = slice(blk * Q_BLOCK, (blk + 1) * Q_BLOCK)
        for h in range(N_HEADS):
            qt_ref[blk, :, h * Q_BLOCK:(h + 1) * Q_BLOCK] = qt[h * HEAD_DIM:(h + 1) * HEAD_DIM, cols]
        for h in range(IDX_HEADS):
            qit_ref[blk, :, h * Q_BLOCK:(h + 1) * Q_BLOCK] = qit[h * IDX_DIM:(h + 1) * IDX_DIM, cols]

    kkt = rope_t(C_KK, 2)
    kt_ref[...] = kkt[0:HEAD_DIM, :]
    kit_ref[...] = kkt[HEAD_DIM:2 * HEAD_DIM, :]
    vwt = mm_t(C_VW, LANES)
    vt_ref[...] = vwt[0:HEAD_DIM, :]
    vbt_ref[...] = vwt[0:HEAD_DIM, :].astype(BF16)
    wit_ref[...] = vwt[HEAD_DIM:HEAD_DIM + SUBLANES, :] * (IDX_HEADS ** -0.5)


def _proj_prompt(x, w_big, w_t, cs, seq, tm):
    n = x.shape[0]
    nb = seq // tm
    qb = tm // Q_BLOCK
    row = lambda w: pl.BlockSpec((tm, w), lambda i: (i, 0))
    col = lambda r: pl.BlockSpec((None, r, tm), lambda i: (i // nb, 0, i % nb))
    slab = lambda heads: pl.BlockSpec((qb, HEAD_DIM, heads * Q_BLOCK), lambda i: (i, 0, 0))
    pm = lambda r, dt: jax.ShapeDtypeStruct((n // seq, r, seq), dt)
    out_shape = (
        jax.ShapeDtypeStruct((n // Q_BLOCK, HEAD_DIM, N_HEADS * Q_BLOCK), BF16),
        jax.ShapeDtypeStruct((n // Q_BLOCK, IDX_DIM, IDX_HEADS * Q_BLOCK), BF16),
        pm(SUBLANES, F32),
        jax.ShapeDtypeStruct((n, HEAD_DIM), BF16), jax.ShapeDtypeStruct((n, IDX_DIM), BF16),
        pm(HEAD_DIM, BF16),
        pm(HEAD_DIM, F32), pm(HEAD_DIM, F32), pm(IDX_DIM, F32),
        jax.ShapeDtypeStruct((n, POOL_WIDTH), F32),
    )
    return pl.pallas_call(
        _proj_prompt_kernel,
        grid=(n // tm,),
        in_specs=[
            row(D_MODEL),
            pl.BlockSpec((D_MODEL, C_END), lambda i: (0, 0)),
            pl.BlockSpec((C_U, D_MODEL), lambda i: (0, 0)),
            pl.BlockSpec((tm, 2 * LANES), lambda i: (i % nb, 0)),
            pl.BlockSpec((2 * LANES, tm), lambda i: (0, i % nb)),
        ],
        out_specs=(slab(N_HEADS), slab(IDX_HEADS), col(SUBLANES), row(HEAD_DIM), row(IDX_DIM), col(HEAD_DIM),
                   col(HEAD_DIM), col(HEAD_DIM), col(IDX_DIM), row(POOL_WIDTH)),
        out_shape=out_shape,
        compiler_params=_params(("parallel",)),
        name="proj_prompt",
    )(x, w_big, w_t, cs, cs.T)


def _float_of_rank(u):
    key = u ^ INT_MIN
    bits = jnp.where(key < 0, INT_MIN - key, key)
    return pltpu.bitcast(bits, F32)


def _count(mask):
    return jnp.sum(mask.astype(F32), axis=1, keepdims=True)


def _topk_bias(sc_ref, j_ref, adm, n_adm, lc, k):
    rows = sc_ref.shape[0]
    kf = float(k)

    def value_step(i, t_u):
        hi = jnp.left_shift(jnp.int32(1), 31 - 2 * i)
        lo = jnp.left_shift(jnp.int32(1), 30 - 2 * i)
        for cand_u in (t_u | lo, t_u | hi, t_u | hi | lo):
            cnt = _count(sc_ref[:, 0:lc] >= _float_of_rank(cand_u))
            t_u = jnp.where(cnt >= kf, cand_u, t_u)
        return t_u

    t_u = lax.fori_loop(0, 16, value_step, jnp.zeros((rows, 1), I32))
    few = n_adm < k
    thr = jnp.where(few, -jnp.inf, _float_of_rank(t_u))
    sc = sc_ref[:, 0:lc]
    cnt_gt = _count(sc > thr)
    cnt_eq = _count(sc == thr)
    need = kf - cnt_gt
    cut_needed = jnp.logical_and(cnt_gt + cnt_eq > kf, jnp.logical_not(few))
    any_cut = jnp.max(cut_needed.astype(F32)) > 0.0
    idx = lax.broadcasted_iota(I32, (rows, lc), 1)
    nbits = int(np.ceil(np.log2(lc)))

    j_ref[...] = jnp.full((rows, 1), lc, I32)

    @pl.when(any_cut)
    def _():
        def index_step(i, j):
            cand = j | jnp.left_shift(jnp.int32(1), nbits - 1 - i)
            c = _count(jnp.logical_and(sc_ref[:, 0:lc] == thr, idx < cand))
            return jnp.where(c < need, cand, j)

        j_ref[...] = lax.fori_loop(0, nbits, index_step, jnp.zeros((rows, 1), I32))

    sel = jnp.logical_or(sc > thr, jnp.logical_and(sc == thr, idx <= j_ref[...]))
    return jnp.where(jnp.logical_and(sel, adm), 0.0, NEG_BIG)


ATTN_CHUNK = 256


def _attn_prompt_block(n_chunks, q0, top_k, qt_ref, qit_ref, wit_ref, kb_ref, kib_ref, vbt_ref, o_ref,
                       key_ref, bias_ref, lg_ref, j_ref):
    tq, ch = Q_BLOCK, ATTN_CHUNK
    kf = float(top_k)
    kpos = lax.broadcasted_iota(I32, (ch, tq), 0)
    qpos = q0 + lax.broadcasted_iota(I32, (ch, tq), 1)

    def rows(c):
        return slice(c * ch, (c + 1) * ch)

    def fold(x, op):
        return op(x.reshape(ch // SUBLANES, SUBLANES, tq), axis=0)

    def head(x, h):
        return x[:, h * tq:(h + 1) * tq]

    qit = qit_ref[...]
    wit = wit_ref[...]
    for c in range(n_chunks if n_chunks * ch > top_k else 0):
        d = jnp.dot(kib_ref[rows(c), :], qit, preferred_element_type=F32)
        s = wit[0:1, :] * jnp.maximum(head(d, 0), 0.0)
        for h in range(1, IDX_HEADS):
            s = s + wit[h:h + 1, :] * jnp.maximum(head(d, h), 0.0)
        key_ref[rows(c), :] = jnp.where(c * ch + kpos <= qpos, s, -jnp.inf)

    def count(pred):
        acc = jnp.zeros((SUBLANES, tq), F32)
        for c in range(n_chunks):
            acc = acc + fold(pred(key_ref[rows(c), :], c).astype(F32), jnp.sum)
        return jnp.sum(acc, axis=0, keepdims=True)

    if n_chunks * ch <= top_k:
        for c in range(n_chunks):
            bias_ref[rows(c), :] = jnp.where(c * ch + kpos <= qpos, 0.0, NEG_BIG)
    else:
        def value_step(i, carry):
            t_u, n_ge = carry
            cand_u = t_u | jnp.left_shift(jnp.int32(1), 31 - i)
            cand = _float_of_rank(cand_u)
            cnt = count(lambda k, c: k >= cand)
            ok = cnt >= kf
            return jnp.where(ok, cand_u, t_u), jnp.where(ok, cnt, n_ge)

        t_u, n_ge = lax.fori_loop(0, 32, value_step,
                                  (jnp.zeros((1, tq), I32), jnp.full((1, tq), float(n_chunks * ch), F32)))
        few = qpos[0:1, :] + 1 <= top_k
        thr = jnp.where(few, -jnp.inf, _float_of_rank(t_u))
        cut_needed = jnp.logical_and(n_ge > kf, jnp.logical_not(few))
        any_cut = jnp.max(cut_needed.astype(F32)) > 0.0

        nbits = int(np.ceil(np.log2(n_chunks * ch)))
        j_ref[...] = jnp.full(j_ref.shape, n_chunks * ch, I32)

        @pl.when(any_cut)
        def _():
            need = kf - count(lambda k, c: k > thr)

            def index_step(i, j):
                cand = j | jnp.left_shift(jnp.int32(1), nbits - 1 - i)
                n_before = count(lambda k, c: jnp.logical_and(k == thr, c * ch + kpos < cand))
                return jnp.where(n_before < need, cand, j)

            j = lax.fori_loop(0, nbits, index_step, jnp.zeros((1, tq), I32))
            j_ref[...] = jnp.broadcast_to(j, j_ref.shape)

        j_cut = j_ref[0:1, :]
        for c in range(n_chunks):
            k = key_ref[rows(c), :]
            pos = c * ch + kpos
            sel = jnp.logical_or(k > thr, jnp.logical_and(k == thr, pos <= j_cut))
            bias_ref[rows(c), :] = jnp.where(jnp.logical_and(sel, pos <= qpos), 0.0, NEG_BIG)

    qt = qt_ref[...]
    mx = [jnp.full((SUBLANES, tq), -jnp.inf, F32) for _ in range(N_HEADS)]
    for c in range(n_chunks):
        lg = jnp.dot(kb_ref[rows(c), :], qt, preferred_element_type=F32)
        bias = bias_ref[rows(c), :]
        for h in range(N_HEADS):
            lgh = head(lg, h) + bias
            lg_ref[h, rows(c), :] = lgh
            mx[h] = jnp.maximum(mx[h], fold(lgh, jnp.max))

    outs = []
    for h in range(N_HEADS):
        m = jnp.max(mx[h], axis=0, keepdims=True)
        lsum = jnp.zeros((SUBLANES, tq), F32)
        ot = jnp.zeros((HEAD_DIM, tq), F32)
        for c in range(n_chunks):
            p = jnp.exp2(lg_ref[h, rows(c), :] - m)
            lsum = lsum + fold(p, jnp.sum)
            ot = ot + jnp.dot(vbt_ref[:, rows(c)], p.astype(BF16), preferred_element_type=F32)
        outs.append(ot / jnp.sum(lsum, axis=0, keepdims=True))
    o_ref[...] = jnp.concatenate(outs, axis=0).T.astype(BF16)


def _attn_prompt_kernel(qt_ref, qit_ref, wit_ref, kb_ref, kib_ref, vbt_ref, o_ref, key_ref, bias_ref, lg_ref, j_ref,
                        *, top_k):
    jq = pl.program_id(1)
    blocks_per_chunk = ATTN_CHUNK // Q_BLOCK
    n_classes = key_ref.shape[0] // ATTN_CHUNK
    for cls in range(n_classes):
        @pl.when(jq // blocks_per_chunk == cls)
        def _(cls=cls):
            _attn_prompt_block(cls + 1, jq * Q_BLOCK, top_k, qt_ref, qit_ref, wit_ref, kb_ref, kib_ref, vbt_ref,
                               o_ref, key_ref, bias_ref, lg_ref, j_ref)


def _attn_prompt(qt, qit, wit, kb, kib, vbt):
    batch, _, seq = vbt.shape
    nb = seq // Q_BLOCK
    top_k = min(TOP_K_MAX, seq // 4)
    slab = lambda heads: pl.BlockSpec((None, HEAD_DIM, heads * Q_BLOCK), lambda b, j: (b * nb + j, 0, 0))
    keys = pl.BlockSpec((seq, HEAD_DIM), lambda b, j: (b, 0))
    return pl.pallas_call(
        functools.partial(_attn_prompt_kernel, top_k=top_k),
        grid=(batch, nb),
        in_specs=[slab(N_HEADS), slab(IDX_HEADS), pl.BlockSpec((None, SUBLANES, Q_BLOCK), lambda b, j: (b, 0, j)),
                  keys, keys, pl.BlockSpec((None, HEAD_DIM, seq), lambda b, j: (b, 0, 0))],
        out_specs=pl.BlockSpec((Q_BLOCK, ATT_WIDTH), lambda b, j: (b * nb + j, 0)),
        out_shape=jax.ShapeDtypeStruct((batch * seq, ATT_WIDTH), BF16),
        scratch_shapes=[pltpu.VMEM((seq, Q_BLOCK), F32), pltpu.VMEM((seq, Q_BLOCK), F32),
                        pltpu.VMEM((N_HEADS, seq, Q_BLOCK), F32), pltpu.VMEM((SUBLANES, Q_BLOCK), I32)],
        compiler_params=_params(("parallel", "arbitrary")),
        name="attn_prompt",
    )(qt, qit, wit, kb, kib, vbt)


SAMPLE_CHUNK = 1024
SAMPLE_ROWS_PER_STEP = 2


def _attn_sample_kernel(pt_ref, q_ref, qi_ref, wi_ref, kn_ref, vn_ref, kin_ref, ck_hbm, cv_hbm, cki_hbm, o_ref,
                        kbuf, vbuf, kibuf, sem, key_scr, bias_scr, lg_scr, j_scr, *, n_pages, page, t_new, top_k):
    b = pl.program_id(0)
    n_b = pl.num_programs(0)
    slot = b % 2
    per_step = q_ref.shape[0]
    past = n_pages * page
    lc = past + page
    n_chunks = past // SAMPLE_CHUNK

    def page_copies(step, sl, p):
        dst = pl.ds(pl.multiple_of(p * page, page), page)
        copies = []
        for r in range(per_step):
            phys = pt_ref[(step * per_step + r) * n_pages + p]
            copies += [pltpu.make_async_copy(src.at[phys], buf.at[sl, r, :, dst], sem.at[i, sl])
                       for i, (src, buf) in enumerate(((ck_hbm, kbuf), (cv_hbm, vbuf), (cki_hbm, kibuf)))]
        return copies

    def start_batch(bb, sl):
        def body(p, carry):
            for n, cp in enumerate(page_copies(bb, sl, p)):
                cp.start(priority=n % 2)
            return carry
        lax.fori_loop(0, n_pages, body, 0)

    def wait_batch(bb, sl):
        def body(p, carry):
            for cp in page_copies(bb, sl, p):
                cp.wait()
            return carry
        lax.fori_loop(0, n_pages, body, 0)

    @pl.when(b == 0)
    def _():
        start_batch(0, 0)

    @pl.when(b + 1 < n_b)
    def _():
        start_batch(b + 1, 1 - slot)

    wait_batch(b, slot)

    def head_sum(r, d):
        x = wi_ref[r] * jnp.maximum(d, 0.0)
        s = x[0:t_new]
        for h in range(1, IDX_HEADS):
            s = s + x[h * t_new:(h + 1) * t_new]
        return s

    def new_rows(ref, r):
        pad = jnp.zeros((page - t_new, ref.shape[2]), F32)
        return jnp.concatenate([ref[r], pad], axis=0).astype(BF16)

    adm_new = lax.broadcasted_iota(I32, (t_new, page), 1) <= lax.broadcasted_iota(I32, (t_new, page), 0)
    for r in range(per_step):
        qrows = slice(r * t_new, (r + 1) * t_new)
        qi = qi_ref[r]
        for c in range(n_chunks):
            sl = slice(c * SAMPLE_CHUNK, (c + 1) * SAMPLE_CHUNK)
            d = jnp.dot(qi, kibuf[slot, r, :, sl].astype(BF16), preferred_element_type=F32)
            key_scr[qrows, sl] = head_sum(r, d)
        d_new = lax.dot_general(qi, new_rows(kin_ref, r), NT_DIMS, preferred_element_type=F32)
        key_scr[qrows, past:lc] = jnp.where(adm_new, head_sum(r, d_new), -jnp.inf)

    n_q = per_step * t_new
    idx = lax.broadcasted_iota(I32, (n_q, lc), 1)
    trow = lax.broadcasted_iota(I32, (n_q, lc), 0) % t_new
    n_adm = past + 1 + lax.broadcasted_iota(I32, (n_q, 1), 0) % t_new
    bias_scr[...] = _topk_bias(key_scr, j_scr, idx - past <= trow, n_adm, lc, top_k)

    for r in range(per_step):
        q = q_ref[r]

        def bias_rows(sl, r=r):
            return jnp.concatenate([bias_scr[r * t_new:(r + 1) * t_new, sl]] * N_HEADS, axis=0)

        m = jnp.full((N_HEADS * t_new, 1), -jnp.inf, F32)
        for c in range(n_chunks):
            sl = slice(c * SAMPLE_CHUNK, (c + 1) * SAMPLE_CHUNK)
            lg = jnp.dot(q, kbuf[slot, r, :, sl].astype(BF16), preferred_element_type=F32) + bias_rows(sl)
            lg_scr[:, sl] = lg
            m = jnp.maximum(m, jnp.max(lg, axis=1, keepdims=True))
        lg_new = (lax.dot_general(q, new_rows(kn_ref, r), NT_DIMS, preferred_element_type=F32)
                  + bias_rows(slice(past, lc)))
        m = jnp.maximum(m, jnp.max(lg_new, axis=1, keepdims=True))

        p_new = jnp.exp2(lg_new - m)
        l = jnp.sum(p_new, axis=1, keepdims=True)
        o = jnp.dot(p_new.astype(BF16), new_rows(vn_ref, r), preferred_element_type=F32)
        for c in range(n_chunks):
            sl = slice(c * SAMPLE_CHUNK, (c + 1) * SAMPLE_CHUNK)
            pr = jnp.exp2(lg_scr[:, sl] - m)
            l = l + jnp.sum(pr, axis=1, keepdims=True)
            o = o + lax.dot_general(pr.astype(BF16), vbuf[slot, r, :, sl].astype(BF16), NT_DIMS,
                                    preferred_element_type=F32)
        o_ref[r] = o / l


def _attn_sample(page_table, q_hq, qi_hq, wi_hq, k_new, v_new, ki_new, cache_kt, cache_vt, cache_kit):
    db, n_pages = page_table.shape
    page = cache_kt.shape[2]
    t_new = k_new.shape[1]
    past = n_pages * page
    lc = past + page
    top_k = min(TOP_K_MAX, (past + t_new) // 4)
    rows = SAMPLE_ROWS_PER_STEP
    per_b = lambda r, w: pl.BlockSpec((rows, r, w), lambda b, pt: (b, 0, 0))
    hbm = pl.BlockSpec(memory_space=pl.ANY)
    kern = functools.partial(_attn_sample_kernel, n_pages=n_pages, page=page, t_new=t_new, top_k=top_k)
    slab = pltpu.VMEM((2, rows, HEAD_DIM, past), F32)
    grid_spec = pltpu.PrefetchScalarGridSpec(
        num_scalar_prefetch=1,
        grid=(db // rows,),
        in_specs=[per_b(N_HEADS * t_new, HEAD_DIM), per_b(IDX_HEADS * t_new, IDX_DIM), per_b(IDX_HEADS * t_new, 1),
                  per_b(t_new, HEAD_DIM), per_b(t_new, HEAD_DIM), per_b(t_new, IDX_DIM),
                  hbm, hbm, hbm],
        out_specs=per_b(N_HEADS * t_new, HEAD_DIM),
        scratch_shapes=[slab, slab, slab, pltpu.SemaphoreType.DMA((3, 2)),
                        pltpu.VMEM((rows * t_new, lc), F32), pltpu.VMEM((rows * t_new, lc), F32),
                        pltpu.VMEM((N_HEADS * t_new, past), F32), pltpu.VMEM((rows * t_new, 1), I32)],
    )
    return pl.pallas_call(
        kern,
        grid_spec=grid_spec,
        out_shape=jax.ShapeDtypeStruct((db, N_HEADS * t_new, HEAD_DIM), F32),
        compiler_params=_params(("arbitrary",)),
        name="attn_sample",
    )(page_table.reshape(-1), q_hq, qi_hq, wi_hq, k_new, v_new, ki_new, cache_kt, cache_vt, cache_kit)


PREV_ROWS = 16


def _pool_kernel(prev_ref, u_ref, wg_ref, sc_ref, o_ref, ext_ref, *, pos0):
    per_step, t_len, _ = u_ref.shape
    pos = pos0 + lax.broadcasted_iota(I32, (t_len, 1), 0)
    for b in range(per_step):
        ext_ref[0:PREV_ROWS, :] = prev_ref[b]
        ext_ref[PREV_ROWS:PREV_ROWS + t_len, :] = u_ref[b]
        for g, w in enumerate(POOL_WINDOWS):
            sl = slice(g * POOL_GW, (g + 1) * POOL_GW)
            u_new = ext_ref[PREV_ROWS:PREV_ROWS + t_len, sl]
            win = u_new
            for back in range(1, w):
                win = win + ext_ref[PREV_ROWS - back:PREV_ROWS - back + t_len, sl]
            count = jnp.minimum(pos + 1, w).astype(F32)
            r = win / count - u_new
            mixed = jnp.dot(r.astype(BF16), wg_ref[g], preferred_element_type=F32) * sc_ref[:, sl]
            o_ref[b, :, sl] = mixed.astype(BF16)


def _pool(prev, u, w_grp, scale, pos0, per_step):
    nb, t_len, _ = u.shape
    seqs = lambda rows: pl.BlockSpec((per_step, rows, POOL_WIDTH), lambda b: (b, 0, 0))
    return pl.pallas_call(
        functools.partial(_pool_kernel, pos0=pos0),
        grid=(nb // per_step,),
        in_specs=[seqs(PREV_ROWS), seqs(t_len),
                  pl.BlockSpec((POOL_GROUPS, POOL_GW, POOL_GW), lambda b: (0, 0, 0)),
                  pl.BlockSpec((1, POOL_WIDTH), lambda b: (0, 0))],
        out_specs=seqs(t_len),
        out_shape=jax.ShapeDtypeStruct((nb, t_len, POOL_WIDTH), BF16),
        scratch_shapes=[pltpu.VMEM((PREV_ROWS + t_len, POOL_WIDTH), F32)],
        compiler_params=_params(("parallel",)),
        name="pool",
    )(prev, u, w_grp, scale)


def _merge_kernel(x_ref, a_ref, p_ref, wga_ref, wgb_ref, wao_ref, wpo_ref, wo_ref, g_ref, b_ref, h_ref, hp_ref, *,
                  alpha):
    x = x_ref[...]
    xb = x.astype(BF16)
    ga = jnp.dot(xb, wga_ref[...], preferred_element_type=F32)
    gb = jnp.dot(xb, wgb_ref[...], preferred_element_type=F32)
    ya = jnp.dot(a_ref[...], wao_ref[...], preferred_element_type=F32)
    yp = jnp.dot(p_ref[...], wpo_ref[...], preferred_element_type=F32)
    mix = jax.nn.sigmoid(ga) * ya + jax.nn.sigmoid(gb) * yp
    out = jnp.dot(mix.astype(BF16), wo_ref[...], preferred_element_type=F32)
    h = _layer_norm(alpha * x + out, g_ref[...], b_ref[...])
    h_ref[...] = h
    hp_ref[...] = _pack_rows(h)


def _merge(x, attn, pool, wga, wgb, wao, wpo, wo, g, b, tm, alpha):
    n = x.shape[0]
    row = lambda w: pl.BlockSpec((tm, w), lambda i: (i, 0))
    full = lambda r, c: pl.BlockSpec((r, c), lambda i: (0, 0), pipeline_mode=pl.Buffered(1))
    return pl.pallas_call(
        functools.partial(_merge_kernel, alpha=alpha),
        grid=(n // tm,),
        in_specs=[row(D_MODEL), row(ATT_WIDTH), row(POOL_WIDTH), full(D_MODEL, D_MODEL), full(D_MODEL, D_MODEL),
                  full(ATT_WIDTH, D_MODEL), full(POOL_WIDTH, D_MODEL), full(D_MODEL, D_MODEL),
                  full(1, D_MODEL), full(1, D_MODEL)],
        out_specs=(row(D_MODEL), row(PACKED)),
        out_shape=(jax.ShapeDtypeStruct((n, D_MODEL), F32), jax.ShapeDtypeStruct((n, PACKED), I32)),
        compiler_params=_params(("parallel",)),
        name="merge",
    )(x, attn, pool, wga, wgb, wao, wpo, wo, g, b)


def _route(h, wr_t, bias_col):
    tm = h.shape[0]
    logits = lax.dot_general(wr_t, h.astype(BF16), NT_DIMS, preferred_element_type=F32)
    s = jax.nn.sigmoid(logits)
    sb = s + bias_col
    neg_inf = -jnp.inf

    rows = []
    for g in range(N_GROUPS):
        blk = sb[g * GROUP_SIZE:(g + 1) * GROUP_SIZE, :]
        m1 = jnp.max(blk, axis=0, keepdims=True)
        is_m1 = blk == m1
        n_m1 = jnp.sum(is_m1.astype(F32), axis=0, keepdims=True)
        m2 = jnp.max(jnp.where(is_m1, neg_inf, blk), axis=0, keepdims=True)
        rows.append(m1 + jnp.where(n_m1 >= 2.0, m1, m2))
    gs = jnp.concatenate(rows, axis=0)

    gi = lax.broadcasted_iota(I32, (N_GROUPS, tm), 0)
    rank = jnp.zeros((N_GROUPS, tm), F32)
    for g in range(N_GROUPS):
        row = gs[g:g + 1, :]
        beats = jnp.logical_or(row > gs, jnp.logical_and(row == gs, g < gi))
        rank = rank + beats.astype(F32)
    gkeep = rank < float(TOPK_GROUPS)
    emask = jnp.concatenate(
        [jnp.broadcast_to(gkeep[g:g + 1, :], (GROUP_SIZE, tm)) for g in range(N_GROUPS)], axis=0)

    ei = lax.broadcasted_iota(I32, (N_EXPERTS, tm), 0)
    x = jnp.where(emask, sb, neg_inf)
    sel = jnp.zeros((N_EXPERTS, tm), jnp.bool_)
    picks = []
    for _ in range(TOP_K_EXPERTS):
        m = jnp.max(x, axis=0, keepdims=True)
        first = jnp.min(jnp.where(x == m, ei, N_EXPERTS), axis=0, keepdims=True)
        pick = ei == first
        sel = jnp.logical_or(sel, pick)
        x = jnp.where(pick, neg_inf, x)
        picks.append(first)

    gate = jnp.where(sel, s, 0.0)
    comb = gate / jnp.sum(gate, axis=0, keepdims=True) * ROUTED_SCALE
    return comb, sel, picks


def _router_kernel(h_ref, wr_ref, bias_ref, c_ref):
    comb, _, _ = _route(h_ref[...], wr_ref[...], bias_ref[...])
    comb = jnp.concatenate([comb, jnp.zeros((LANES - N_EXPERTS, comb.shape[1]), F32)], axis=0)
    c_ref[...] = comb.T


def _router(h, wr_t, bias_col, tm):
    n = h.shape[0]
    return pl.pallas_call(
        _router_kernel,
        grid=(n // tm,),
        in_specs=[pl.BlockSpec((tm, D_MODEL), lambda i: (i, 0)),
                  pl.BlockSpec((N_EXPERTS, D_MODEL), lambda i: (0, 0)),
                  pl.BlockSpec((N_EXPERTS, 1), lambda i: (0, 0))],
        out_specs=pl.BlockSpec((tm, LANES), lambda i: (i, 0)),
        out_shape=jax.ShapeDtypeStruct((n, LANES), F32),
        compiler_params=_params(("parallel",)),
        name="router",
    )(h, wr_t, bias_col)


def _swiglu(xb, w13, w2, hidden):
    ab = jnp.dot(xb, w13, preferred_element_type=F32)
    act = jax.nn.silu(ab[:, 0:hidden]) * ab[:, hidden:2 * hidden]
    return jnp.dot(act.astype(BF16), w2, preferred_element_type=F32)


def _moe_kernel(h_ref, c_ref, ws13_ref, ws2_ref, w13_ref, w2_ref, y_ref, hb_ref):
    e = pl.program_id(1)

    @pl.when(e == 0)
    def _():
        hb_ref[...] = h_ref[...].astype(BF16)
        y_ref[...] = _swiglu(hb_ref[...], ws13_ref[...], ws2_ref[...], SHARED_DIM)

    ye = _swiglu(hb_ref[...], w13_ref[...].astype(BF16), w2_ref[...].astype(BF16), EXPERT_DIM)
    lane = lax.broadcasted_iota(I32, c_ref.shape, 1)
    ce = jnp.sum(jnp.where(lane == e, c_ref[...], 0.0), axis=1, keepdims=True)
    y_ref[...] += ce * ye


def _moe(h, comb, ws13, ws2, w13, w2, tm):
    n = h.shape[0]
    return pl.pallas_call(
        _moe_kernel,
        grid=(n // tm, N_EXPERTS),
        in_specs=[pl.BlockSpec((tm, D_MODEL), lambda i, e: (i, 0)),
                  pl.BlockSpec((tm, LANES), lambda i, e: (i, 0)),
                  pl.BlockSpec((D_MODEL, 2 * SHARED_DIM), lambda i, e: (0, 0)),
                  pl.BlockSpec((SHARED_DIM, D_MODEL), lambda i, e: (0, 0)),
                  pl.BlockSpec((None, D_MODEL, 2 * EXPERT_DIM), lambda i, e: (e, 0, 0)),
                  pl.BlockSpec((None, EXPERT_DIM, D_MODEL), lambda i, e: (e, 0, 0))],
        out_specs=pl.BlockSpec((tm, D_MODEL), lambda i, e: (i, 0)),
        out_shape=jax.ShapeDtypeStruct((n, D_MODEL), F32),
        scratch_shapes=[pltpu.VMEM((tm, D_MODEL), BF16)],
        compiler_params=_params(("parallel", "arbitrary")),
        name="moe",
    )(h, comb, ws13, ws2, w13, w2)


def _final_kernel(h_ref, y_ref, pe_ref, g_ref, b_ref, wpg_ref, wpi_ref, o_ref, *, alpha):
    z = _layer_norm(alpha * h_ref[...] + y_ref[...], g_ref[...], b_ref[...])
    gate = jax.nn.sigmoid(jnp.dot(z.astype(BF16), wpg_ref[...], preferred_element_type=F32))
    emb = jnp.dot(pe_ref[...].astype(BF16), wpi_ref[...], preferred_element_type=F32)
    o_ref[...] = z + gate * emb


def _final(h, y, pe, g, b, wpg, wpi, tm, alpha):
    n = h.shape[0]
    row = lambda w: pl.BlockSpec((tm, w), lambda i: (i, 0))
    full = lambda r, c: pl.BlockSpec((r, c), lambda i: (0, 0))
    return pl.pallas_call(
        functools.partial(_final_kernel, alpha=alpha),
        grid=(n // tm,),
        in_specs=[row(D_MODEL), row(D_MODEL), row(PLE_DIM), full(1, D_MODEL), full(1, D_MODEL),
                  full(D_MODEL, D_MODEL), full(PLE_DIM, D_MODEL)],
        out_specs=row(D_MODEL),
        out_shape=jax.ShapeDtypeStruct((n, D_MODEL), F32),
        compiler_params=_params(("parallel",)),
        name="final",
    )(h, y, pe, g, b, wpg, wpi)


MOE_BLOCK = 2176


def _sorted_rows(n_tokens):
    worst = n_tokens * TOP_K_EXPERTS + N_EXPERTS * MOE_BLOCK
    return -(-worst // MOE_BLOCK) * MOE_BLOCK


def _dispatch_kernel(h_ref, wr_ref, bias_ref, tri_ref, pos_ref, gate_ref, blk_ref, used_ref,
                     eidx_s, rank_s, gate_s, cnt_s):
    p = pl.program_id(0)
    i = pl.program_id(1)
    tm = h_ref.shape[0]
    ei = lax.broadcasted_iota(I32, (N_EXPERTS, tm), 0)

    @pl.when(p == 0)
    def _():
        comb, sel, picks = _route(h_ref[...], wr_ref[...], bias_ref[...])
        before = jnp.dot(sel.astype(BF16), tri_ref[...], preferred_element_type=F32)
        ranks, gates = [], []
        for first in picks:
            pick = ei == first
            ranks.append(jnp.sum(jnp.where(pick, before, 0.0), axis=0, keepdims=True))
            gates.append(jnp.sum(jnp.where(pick, comb, 0.0), axis=0, keepdims=True))
        eidx_s[i] = jnp.concatenate(picks, axis=0)
        rank_s[i] = jnp.concatenate(ranks, axis=0)
        gate_s[i] = jnp.concatenate(gates, axis=0)
        cnt_s[i] = jnp.broadcast_to(jnp.sum(sel.astype(F32), axis=1, keepdims=True), (N_EXPERTS, LANES))

    @pl.when(p == 1)
    def _():
        cnt = cnt_s[...]
        tile_id = lax.broadcasted_iota(I32, cnt.shape, 0)
        total = jnp.sum(cnt, axis=0)
        prior = jnp.sum(jnp.where(tile_id < i, cnt, 0.0), axis=0)
        seg_blk = jnp.maximum(jnp.ceil(total * (1.0 / MOE_BLOCK) - 0.25 / MOE_BLOCK), 1.0)
        lower = (lax.broadcasted_iota(I32, (N_EXPERTS, N_EXPERTS), 1)
                 < lax.broadcasted_iota(I32, (N_EXPERTS, N_EXPERTS), 0)).astype(F32)
        off_blk = jnp.dot(lower, seg_blk, precision=lax.Precision.HIGHEST, preferred_element_type=F32)
        seg_off = off_blk * MOE_BLOCK
        base = (seg_off + prior)[:, 0:1]
        eidx = eidx_s[i]
        rank = rank_s[i]
        rows = []
        for k in range(TOP_K_EXPERTS):
            pick = ei == eidx[k:k + 1, :]
            rows.append(rank[k:k + 1, :] + jnp.sum(jnp.where(pick, base, 0.0), axis=0, keepdims=True))
        pos_ref[...] = jnp.concatenate(rows, axis=0).astype(I32)
        gate_ref[...] = jnp.concatenate([gate_s[i], jnp.zeros((LANES - TOP_K_EXPERTS, tm), F32)], axis=0).T

        end_blk = (off_blk + seg_blk)[:, 0:1]
        n_blk = blk_ref.shape[1]
        blk_id = lax.broadcasted_iota(I32, (N_EXPERTS, n_blk), 1).astype(F32)
        owner = jnp.sum((end_blk <= blk_id).astype(F32), axis=0, keepdims=True)
        blk_ref[...] = jnp.minimum(owner, N_EXPERTS - 1.0).astype(I32)
        used_ref[...] = jnp.broadcast_to(end_blk[N_EXPERTS - 1:N_EXPERTS, :], used_ref.shape).astype(I32)


def _dispatch(h, wr_t, bias_col, tm):
    n = h.shape[0]
    n_tiles = n // tm
    n_blk = _sorted_rows(n) // MOE_BLOCK
    n_blk_pad = -(-n_blk // LANES) * LANES
    tri = jnp.triu(jnp.ones((tm, tm), BF16), k=1)
    const = lambda r, c: pl.BlockSpec((r, c), lambda p, i: (0, 0))
    per_tile = lambda dt: pltpu.VMEM((n_tiles, TOP_K_EXPERTS, tm), dt)
    return pl.pallas_call(
        _dispatch_kernel,
        grid=(2, n_tiles),
        in_specs=[pl.BlockSpec((tm, D_MODEL), lambda p, i: (i * (1 - p), 0)),
                  const(N_EXPERTS, D_MODEL), const(N_EXPERTS, 1), const(tm, tm)],
        out_specs=(pl.BlockSpec((TOP_K_EXPERTS, tm), lambda p, i: (0, i * p)),
                   pl.BlockSpec((tm, LANES), lambda p, i: (i * p, 0)),
                   const(1, n_blk_pad), const(1, LANES)),
        out_shape=(jax.ShapeDtypeStruct((TOP_K_EXPERTS, n), I32), jax.ShapeDtypeStruct((n, LANES), F32),
                   jax.ShapeDtypeStruct((1, n_blk_pad), I32), jax.ShapeDtypeStruct((1, LANES), I32)),
        scratch_shapes=[per_tile(I32), per_tile(F32), per_tile(F32), pltpu.VMEM((n_tiles, N_EXPERTS, LANES), F32)],
        compiler_params=_params(("arbitrary", "arbitrary")),
        name="dispatch",
    )(h, wr_t, bias_col, tri)


PACKED = D_MODEL // 2


def _pack_rows(x):
    lo = pltpu.bitcast(x[:, 0:PACKED].astype(BF16).astype(F32), I32)
    hi = pltpu.bitcast(x[:, PACKED:D_MODEL].astype(BF16).astype(F32), I32)
    return jnp.bitwise_or(hi, lax.shift_right_logical(lo, 16))


def _unpack_rows_f32(w):
    lo = pltpu.bitcast(lax.shift_left(w, 16), F32)
    hi = pltpu.bitcast(jnp.bitwise_and(w, -65536), F32)
    return jnp.concatenate([lo, hi], axis=1)


def _unpack_rows(w):
    return _unpack_rows_f32(w).astype(BF16)


GROUPED_RING = 3


def _grouped_kernel(blk_ref, used_ref, anchor_ref, xs_hbm, w13_ref, w2_ref, ys_ref, w13b_ref, w2b_ref, xbuf, sem):
    s = pl.program_id(0)
    used = used_ref[0]

    def row_copy(b):
        slot = b % GROUPED_RING
        return pltpu.make_async_copy(xs_hbm.at[pl.ds(pl.multiple_of(b * MOE_BLOCK, MOE_BLOCK), MOE_BLOCK)],
                                     xbuf.at[slot], sem.at[slot])

    @pl.when(s == 0)
    def _():
        for b in range(GROUPED_RING - 1):
            @pl.when(b < used)
            def _(b=b):
                row_copy(b).start()

    @pl.when(s + GROUPED_RING - 1 < used)
    def _():
        row_copy(s + GROUPED_RING - 1).start()

    @pl.when(s < used)
    def _():
        row_copy(s).wait()
        w13b_ref[...] = w13_ref[...].astype(BF16)
        w2b_ref[...] = w2_ref[...].astype(BF16)
        ys = _swiglu(_unpack_rows(xbuf[s % GROUPED_RING]), w13b_ref[...], w2b_ref[...], EXPERT_DIM)
        ys_ref[...] = _pack_rows(ys)


def _grouped(blk, used, anchor, xs, w13, w2):
    ns = xs.shape[0]
    row_blk = lambda b, blk, used, anchor: (jnp.minimum(b, used[0] - 1), 0)
    expert = lambda b, blk, used, anchor: (blk[b], 0, 0)
    grid_spec = pltpu.PrefetchScalarGridSpec(
        num_scalar_prefetch=3,
        grid=(ns // MOE_BLOCK,),
        in_specs=[pl.BlockSpec(memory_space=pl.ANY),
                  pl.BlockSpec((None, D_MODEL, 2 * EXPERT_DIM), expert),
                  pl.BlockSpec((None, EXPERT_DIM, D_MODEL), expert)],
        out_specs=(pl.BlockSpec((MOE_BLOCK, PACKED), row_blk),
                   pl.BlockSpec((None, D_MODEL, 2 * EXPERT_DIM), expert),
                   pl.BlockSpec((None, EXPERT_DIM, D_MODEL), expert)),
        scratch_shapes=[pltpu.VMEM((GROUPED_RING, MOE_BLOCK, PACKED), I32), pltpu.SemaphoreType.DMA((GROUPED_RING,))],
    )
    return pl.pallas_call(
        _grouped_kernel,
        grid_spec=grid_spec,
        out_shape=(jax.ShapeDtypeStruct((ns, PACKED), I32), jax.ShapeDtypeStruct(w13.shape, BF16),
                   jax.ShapeDtypeStruct(w2.shape, BF16)),
        compiler_params=_params(("arbitrary",)),
        name="grouped",
    )(blk, used, anchor, xs, w13, w2)


SC_WINDOW = 128


def _sc_mesh():
    return plsc.VectorSubcoreMesh(core_axis_name="core", subcore_axis_name="subcore")


def _sc_worker(n_items):
    info = plsc.get_sparse_core_info()
    n_workers = info.num_cores * info.num_subcores
    assert n_items % (SC_WINDOW * n_workers) == 0, "rows must split evenly into windows over the vector subcores"
    wid = lax.axis_index("subcore") * info.num_cores + lax.axis_index("core")
    return wid, n_items // (SC_WINDOW * n_workers)


def _scatter_rows(x, pos, n_out):
    n, width = x.shape
    picks = pos.shape[0]

    @functools.partial(
        pl.kernel, mesh=_sc_mesh(), out_type=jax.ShapeDtypeStruct((n_out, width), I32),
        scratch_types=[pltpu.VMEM((picks, SC_WINDOW), I32), pltpu.VMEM((SC_WINDOW, width), I32)],
        name="scatter_rows")
    def scatter(x_hbm, pos_hbm, out_hbm, idx_v, rows_v):
        wid, n_win = _sc_worker(n)

        @pl.loop(0, n_win)
        def _(j):
            base = (wid * n_win + j) * SC_WINDOW
            pltpu.sync_copy(pos_hbm.at[:, pl.ds(base, SC_WINDOW)], idx_v)
            pltpu.sync_copy(x_hbm.at[pl.ds(base, SC_WINDOW)], rows_v)
            for k in range(picks):
                pltpu.sync_copy(rows_v, out_hbm.at[idx_v.at[k]])

    return scatter(x, pos)


def _gather_rows(src, pos):
    width = src.shape[1]
    picks, n = pos.shape

    @functools.partial(
        pl.kernel, mesh=_sc_mesh(), out_type=jax.ShapeDtypeStruct((picks * n, width), I32),
        scratch_types=[pltpu.VMEM((SC_WINDOW,), I32), pltpu.VMEM((SC_WINDOW, width), I32)],
        name="gather_rows")
    def gather(src_hbm, pos_hbm, out_hbm, idx_v, rows_v):
        wid, n_win = _sc_worker(picks * n)

        @pl.loop(0, n_win)
        def _(j):
            base = (wid * n_win + j) * SC_WINDOW
            pltpu.sync_copy(pos_hbm.at[pl.ds(base, SC_WINDOW)], idx_v)
            pltpu.sync_copy(src_hbm.at[idx_v], rows_v)
            pltpu.sync_copy(rows_v, out_hbm.at[pl.ds(base, SC_WINDOW)])

    return gather(src, pos.reshape(-1)).reshape(picks, n, width)


def _combine_kernel(h_ref, g_ref, gate_ref, pe_ref, ws13_ref, ws2_ref, ln_g_ref, ln_b_ref, wpg_ref, wpi_ref, o_ref, *,
                    alpha):
    h = h_ref[...]
    y = _swiglu(h.astype(BF16), ws13_ref[...], ws2_ref[...], SHARED_DIM)
    gate = gate_ref[...]
    for k in range(TOP_K_EXPERTS):
        y = y + gate[:, k:k + 1] * _unpack_rows_f32(g_ref[k])
    z = _layer_norm(alpha * h + y, ln_g_ref[...], ln_b_ref[...])
    ple_gate = jax.nn.sigmoid(jnp.dot(z.astype(BF16), wpg_ref[...], preferred_element_type=F32))
    emb = jnp.dot(pe_ref[...].astype(BF16), wpi_ref[...], preferred_element_type=F32)
    o_ref[...] = z + ple_gate * emb


def _combine(h, gathered, gate, pe, ws13, ws2, g, b, wpg, wpi, tm, alpha):
    n = h.shape[0]
    row = lambda w: pl.BlockSpec((tm, w), lambda i: (i, 0))
    full = lambda r, c: pl.BlockSpec((r, c), lambda i: (0, 0))
    return pl.pallas_call(
        functools.partial(_combine_kernel, alpha=alpha),
        grid=(n // tm,),
        in_specs=[row(D_MODEL), pl.BlockSpec((TOP_K_EXPERTS, tm, PACKED), lambda i: (0, i, 0)), row(LANES),
                  row(PLE_DIM), full(D_MODEL, 2 * SHARED_DIM), full(SHARED_DIM, D_MODEL),
                  full(1, D_MODEL), full(1, D_MODEL), full(D_MODEL, D_MODEL), full(PLE_DIM, D_MODEL)],
        out_specs=row(D_MODEL),
        out_shape=jax.ShapeDtypeStruct((n, D_MODEL), F32),
        compiler_params=_params(("parallel",)),
        name="combine",
    )(h, gathered, gate, pe, ws13, ws2, g, b, wpg, wpi)


def _rope_table(pos):
    inv = ROPE_THETA ** (-jnp.arange(0, HEAD_DIM, 2, dtype=F32) / HEAD_DIM)
    ang = pos.astype(F32)[:, None] * inv[None, :]
    return jnp.concatenate([jnp.tile(jnp.cos(ang), (1, 4)), jnp.tile(jnp.sin(ang), (1, 4))], axis=1)


def _fused_in_weight(w_in):
    offs = np.cumsum(IN_SIZES)[:-1].tolist()
    wq, wk, wv, wqi, wki, wwi, wu, wga, wgb = jnp.split(w_in, offs, axis=1)
    pad = jnp.zeros((D_MODEL, LANES - HEAD_DIM - IDX_HEADS), w_in.dtype)
    w_big = jnp.concatenate([wq, wqi, wk, wki, wv, wwi, pad, wu], axis=1).astype(BF16)
    return w_big, w_big[:, 0:C_U].T, wga.astype(BF16), wgb.astype(BF16)


def _pages_transposed(cache):
    return jnp.transpose(cache[0], (0, 2, 1))


def _heads_major(a, n_heads):
    b, t, w = a.shape
    d = w // n_heads
    return a.reshape(b, t, n_heads, d).transpose(0, 2, 1, 3).reshape(b, n_heads * t, d)


def kernel(x_prompt, x_sample, cache_k, cache_v, cache_kidx, state_pool, page_table, p_prompt, p_sample, w_in, w_att_out, w_pool_grp, pool_scale, w_pool_out, w_out, ln1_g, ln1_b, w_router, router_bias, w_exp13, w_exp2, w_sh13, w_sh2, ln2_g, ln2_b, w_ple_in, w_ple_gate):
    B, S, D = x_prompt.shape
    DB, T, _ = x_sample.shape
    depth = w_in.shape[0]
    assert depth == 1, "single layer step"
    page = cache_k.shape[2]
    past = page_table.shape[1] * page
    alpha = (2 * depth) ** 0.25
    n_p, n_s = B * S, DB * T

    w_big, w_t, wga, wgb = _fused_in_weight(w_in[0])
    wao, wpo, wo = w_att_out[0].astype(BF16), w_pool_out[0].astype(BF16), w_out[0].astype(BF16)
    wgrp = w_pool_grp[0].astype(BF16)
    pscale = pool_scale[0].reshape(1, POOL_WIDTH)
    g1, b1 = ln1_g[0].reshape(1, D), ln1_b[0].reshape(1, D)
    g2, b2 = ln2_g[0].reshape(1, D), ln2_b[0].reshape(1, D)
    wr_t = w_router[0].T.astype(BF16)
    rbias = router_bias[0].reshape(N_EXPERTS, 1)
    w13, w2 = w_exp13[0], w_exp2[0]
    ws13, ws2 = w_sh13[0].astype(BF16), w_sh2[0].astype(BF16)
    wpg, wpi = w_ple_gate[0].astype(BF16), w_ple_in[0].astype(BF16)

    cs_p = _rope_table(jnp.arange(S, dtype=I32))
    cs_s = jnp.tile(_rope_table(past + jnp.arange(T, dtype=I32)), (DB, 1))

    xp = x_prompt.reshape(n_p, D)
    qt, qit, wit, kb, kib, vbt, kt, vt, kit, u = _proj_prompt(xp, w_big, w_t, cs_p, S, PROJ_TILE)
    attn_p = _attn_prompt(qt, qit, wit, kb, kib, vbt)
    u3 = u.reshape(B, S, POOL_WIDTH)
    pool_p = _pool(jnp.zeros((B, PREV_ROWS, POOL_WIDTH), F32), u3, wgrp, pscale, 0, 1).reshape(n_p, POOL_WIDTH)
    h_p, hp_p = _merge(xp, attn_p, pool_p, wga, wgb, wao, wpo, wo, g1, b1, MERGE_TILE, alpha)

    xs = x_sample.reshape(n_s, D)
    qs, qis, ks, vs, kis, wis, us = _proj_sample(xs, w_big, cs_s)
    q_hq = _heads_major(qs.reshape(DB, T, ATT_WIDTH), N_HEADS)
    qi_hq = _heads_major(qis.reshape(DB, T, IDX_HEADS * IDX_DIM), IDX_HEADS)
    wi_hq = wis.reshape(DB, T, IDX_HEADS).transpose(0, 2, 1).reshape(DB, IDX_HEADS * T, 1)
    caches = (_pages_transposed(cache_k), _pages_transposed(cache_v), _pages_transposed(cache_kidx))
    new_rows = (ks.reshape(DB, T, HEAD_DIM), vs.reshape(DB, T, HEAD_DIM), kis.reshape(DB, T, IDX_DIM))
    half = DB * 5 // 8 // SAMPLE_ROWS_PER_STEP * SAMPLE_ROWS_PER_STEP
    o_halves = [_attn_sample(page_table[sl], q_hq[sl], qi_hq[sl], wi_hq[sl], *(a[sl] for a in new_rows), *caches)
                for sl in (slice(0, half), slice(half, DB))]
    o_hq = jnp.concatenate(o_halves, axis=0)
    attn_s = o_hq.reshape(DB, N_HEADS, T, HEAD_DIM).transpose(0, 2, 1, 3).reshape(n_s, ATT_WIDTH).astype(BF16)
    us3 = us.reshape(DB, T, POOL_WIDTH)
    prev_s = jnp.concatenate([jnp.zeros((DB, PREV_ROWS - POOL_STATE, POOL_WIDTH), F32), state_pool[0]], axis=1)
    pool_s = _pool(prev_s, us3, wgrp, pscale, past, DB).reshape(n_s, POOL_WIDTH)
    h_s, _ = _merge(xs, attn_s, pool_s, wga, wgb, wao, wpo, wo, g1, b1, n_s, alpha)

    pos, gate, blk, used = _dispatch(h_p, wr_t, rbias, DISPATCH_TILE)
    sorted_in = _scatter_rows(hp_p, pos, _sorted_rows(n_p))
    anchor = lax.bitcast_convert_type(o_halves[0][0, 0, 0:1], I32)
    sorted_out, w13_b, w2_b = _grouped(blk.reshape(-1), used.reshape(-1), anchor, sorted_in, w13, w2)
    gathered = _gather_rows(sorted_out, pos)

    comb_s = _router(h_s, wr_t, rbias, n_s)
    y_s = _final(h_s, _moe(h_s, comb_s, ws13, ws2, w13_b, w2_b, n_s), p_sample[0].reshape(n_s, PLE_DIM), g2, b2, wpg,
                 wpi, n_s, alpha)
    y_p = _combine(h_p, gathered, gate, p_prompt[0].reshape(n_p, PLE_DIM), ws13, ws2, g2, b2, wpg, wpi, COMBINE_TILE,
                   alpha)

    ext_s = jnp.concatenate([state_pool[0], us3], axis=1)
    return (y_p.reshape(B, S, D), y_s.reshape(DB, T, D),
            jnp.transpose(kt, (0, 2, 1))[None], jnp.transpose(vt, (0, 2, 1))[None],
            jnp.transpose(kit, (0, 2, 1))[None],
            u3[:, S - POOL_STATE:][None],
            ks.reshape(1, DB, T, HEAD_DIM), vs.reshape(1, DB, T, HEAD_DIM), kis.reshape(1, DB, T, IDX_DIM),
            ext_s[:, T:][None])
```

```python
import functools

import numpy as np
import jax
import jax.numpy as jnp
from jax import lax
from jax.experimental import pallas as pl
from jax.experimental.pallas import tpu as pltpu
from jax.experimental.pallas import tpu_sc as plsc

F32 = jnp.float32
BF16 = jnp.bfloat16
I32 = jnp.int32

D_MODEL = 1024
N_HEADS = 8
HEAD_DIM = 64
ATT_WIDTH = N_HEADS * HEAD_DIM
IDX_HEADS = 4
IDX_DIM = 64
TOP_K_MAX = 256
Q_BLOCK = 256
ROPE_THETA = 10000.0
POOL_WINDOWS = (2, 4, 8, 16)
POOL_GROUPS = 4
POOL_WIDTH = 512
POOL_GW = POOL_WIDTH // POOL_GROUPS
POOL_STATE = 15
N_EXPERTS = 64
TOP_K_EXPERTS = 8
N_GROUPS = 8
GROUP_SIZE = N_EXPERTS // N_GROUPS
TOPK_GROUPS = 4
EXPERT_DIM = 256
SHARED_DIM = 256
ROUTED_SCALE = 2.5
PLE_DIM = 256
LN_EPS = 1e-5
IN_SIZES = (ATT_WIDTH, HEAD_DIM, HEAD_DIM, IDX_HEADS * IDX_DIM, IDX_DIM, IDX_HEADS, POOL_WIDTH, D_MODEL, D_MODEL)

LANES = 128
SUBLANES = 8
INT_MIN = -2147483648
NEG_BIG = -1e30
VMEM_LIMIT = 56 * 1024 * 1024
PROJ_TILE = 512
MERGE_TILE = 1024
DISPATCH_TILE = 1024
COMBINE_TILE = 512

C_Q = 0
C_QI = 512
C_KK = 768
C_VW = 896
C_U = 1024
C_END = 1536
HALF = HEAD_DIM // 2

NT_DIMS = (((1,), (1,)), ((), ()))

Q_SCALE = HEAD_DIM ** -0.5 * float(np.log2(np.e))
QI_SCALE = IDX_DIM ** -0.5


def _params(sem):
    return pltpu.CompilerParams(dimension_semantics=sem, vmem_limit_bytes=VMEM_LIMIT)


def _layer_norm(x, g, b):
    mu = jnp.mean(x, axis=-1, keepdims=True)
    xc = x - mu
    var = jnp.mean(xc * xc, axis=-1, keepdims=True)
    return xc * lax.rsqrt(var + LN_EPS) * g + b


def _rope_rows(a, cos, sin):
    first_half = lax.broadcasted_iota(I32, (a.shape[0], LANES), 1) % HEAD_DIM < HALF
    out = []
    for s in range(a.shape[1] // LANES):
        x = a[:, s * LANES:(s + 1) * LANES]
        rot = jnp.where(first_half, -pltpu.roll(x, LANES - HALF, axis=1), pltpu.roll(x, HALF, axis=1))
        out.append(x * cos + rot * sin)
    return out[0] if len(out) == 1 else jnp.concatenate(out, axis=1)


def _proj_sample_kernel(x_ref, w_ref, cs_ref, q_ref, qi_ref, k_ref, v_ref, ki_ref, wi_ref, u_ref):
    xb = x_ref[...].astype(BF16)
    cos = cs_ref[:, 0:LANES]
    sin = cs_ref[:, LANES:2 * LANES]

    def mm(c0, n):
        return jnp.dot(xb, w_ref[:, c0:c0 + n], preferred_element_type=F32)

    def rope(c0, n):
        return _rope_rows(mm(c0, n), cos, sin)

    q_ref[...] = (rope(C_Q, ATT_WIDTH) * Q_SCALE).astype(BF16)
    qi_ref[...] = (rope(C_QI, IDX_HEADS * IDX_DIM) * QI_SCALE).astype(BF16)
    kk = rope(C_KK, LANES)
    k_ref[...] = kk[:, 0:HEAD_DIM]
    ki_ref[...] = kk[:, HEAD_DIM:2 * HEAD_DIM]
    vw = mm(C_VW, LANES)
    v_ref[...] = vw[:, 0:HEAD_DIM]
    wi_ref[...] = vw[:, HEAD_DIM:HEAD_DIM + IDX_HEADS] * (IDX_HEADS ** -0.5)
    u_ref[...] = mm(C_U, POOL_WIDTH)


def _proj_sample(x, w_big, cs):
    n = x.shape[0]
    full = lambda r, c: pl.BlockSpec((r, c), lambda i: (0, 0))
    widths = (ATT_WIDTH, IDX_HEADS * IDX_DIM, HEAD_DIM, HEAD_DIM, IDX_DIM, IDX_HEADS, POOL_WIDTH)
    dtypes = (BF16, BF16, F32, F32, F32, F32, F32)
    return pl.pallas_call(
        _proj_sample_kernel,
        grid=(1,),
        in_specs=[full(n, D_MODEL), full(D_MODEL, C_END), full(n, 2 * LANES)],
        out_specs=tuple(full(n, w) for w in widths),
        out_shape=tuple(jax.ShapeDtypeStruct((n, w), dt) for w, dt in zip(widths, dtypes)),
        compiler_params=_params(("arbitrary",)),
        name="proj_sample",
    )(x, w_big, cs)


def _proj_prompt_kernel(x_ref, w_ref, wt_ref, cs_ref, cst_ref, qt_ref, qit_ref, wit_ref, kb_ref, kib_ref, vbt_ref,
                        kt_ref, vt_ref, kit_ref, u_ref):
    xb = x_ref[...].astype(BF16)
    tm = xb.shape[0]
    cos = cs_ref[:, 0:LANES]
    sin = cs_ref[:, LANES:2 * LANES]
    cos_t = cst_ref[0:HEAD_DIM, :]
    sin_t = cst_ref[LANES:LANES + HEAD_DIM, :]

    def mm(c0, n):
        return jnp.dot(xb, w_ref[:, c0:c0 + n], preferred_element_type=F32)

    def mm_t(c0, n):
        return lax.dot_general(wt_ref[c0:c0 + n, :], xb, NT_DIMS, preferred_element_type=F32)

    def rope_t(c0, heads):
        a = mm_t(c0, heads * HEAD_DIM)
        parts = []
        for h in range(heads):
            x1 = a[h * HEAD_DIM:h * HEAD_DIM + HALF, :]
            x2 = a[h * HEAD_DIM + HALF:(h + 1) * HEAD_DIM, :]
            rot = jnp.concatenate([-x2, x1], axis=0)
            parts.append(a[h * HEAD_DIM:(h + 1) * HEAD_DIM, :] * cos_t + rot * sin_t)
        return parts[0] if heads == 1 else jnp.concatenate(parts, axis=0)

    kk = _rope_rows(mm(C_KK, LANES), cos, sin)
    kb_ref[...] = kk[:, 0:HEAD_DIM].astype(BF16)
    kib_ref[...] = kk[:, HEAD_DIM:2 * HEAD_DIM].astype(BF16)
    u_ref[...] = mm(C_U, POOL_WIDTH)

    qt = (rope_t(C_Q, N_HEADS) * Q_SCALE).astype(BF16)
    qit = (rope_t(C_QI, IDX_HEADS) * QI_SCALE).astype(BF16)
    for blk in range(tm // Q_BLOCK):
        cols = slice(blk * Q_BLOCK, (blk + 1) * Q_BLOCK)
        for h in range(N_HEADS):
            qt_ref[blk, :, h * Q_BLOCK:(h + 1) * Q_BLOCK] = qt[h * HEAD_DIM:(h + 1) * HEAD_DIM, cols]
        for h in range(IDX_HEADS):
            qit_ref[blk, :, h * Q_BLOCK:(h + 1) * Q_BLOCK] = qit[h * IDX_DIM:(h + 1) * IDX_DIM, cols]

    kkt = rope_t(C_KK, 2)
    kt_ref[...] = kkt[0:HEAD_DIM, :]
    kit_ref[...] = kkt[HEAD_DIM:2 * HEAD_DIM, :]
    vwt = mm_t(C_VW, LANES)
    vt_ref[...] = vwt[0:HEAD_DIM, :]
    vbt_ref[...] = vwt[0:HEAD_DIM, :].astype(BF16)
    wit_ref[...] = vwt[HEAD_DIM:HEAD_DIM + SUBLANES, :] * (IDX_HEADS ** -0.5)


def _proj_prompt(x, w_big, w_t, cs, seq, tm):
    n = x.shape[0]
    nb = seq // tm
    qb = tm // Q_BLOCK
    row = lambda w: pl.BlockSpec((tm, w), lambda i: (i, 0))
    col = lambda r: pl.BlockSpec((None, r, tm), lambda i: (i // nb, 0, i % nb))
    slab = lambda heads: pl.BlockSpec((qb, HEAD_DIM, heads * Q_BLOCK), lambda i: (i, 0, 0))
    pm = lambda r, dt: jax.ShapeDtypeStruct((n // seq, r, seq), dt)
    out_shape = (
        jax.ShapeDtypeStruct((n // Q_BLOCK, HEAD_DIM, N_HEADS * Q_BLOCK), BF16),
        jax.ShapeDtypeStruct((n // Q_BLOCK, IDX_DIM, IDX_HEADS * Q_BLOCK), BF16),
        pm(SUBLANES, F32),
        jax.ShapeDtypeStruct((n, HEAD_DIM), BF16), jax.ShapeDtypeStruct((n, IDX_DIM), BF16),
        pm(HEAD_DIM, BF16),
        pm(HEAD_DIM, F32), pm(HEAD_DIM, F32), pm(IDX_DIM, F32),
        jax.ShapeDtypeStruct((n, POOL_WIDTH), F32),
    )
    return pl.pallas_call(
        _proj_prompt_kernel,
        grid=(n // tm,),
        in_specs=[
            row(D_MODEL),
            pl.BlockSpec((D_MODEL, C_END), lambda i: (0, 0)),
            pl.BlockSpec((C_U, D_MODEL), lambda i: (0, 0)),
            pl.BlockSpec((tm, 2 * LANES), lambda i: (i % nb, 0)),
            pl.BlockSpec((2 * LANES, tm), lambda i: (0, i % nb)),
        ],
        out_specs=(slab(N_HEADS), slab(IDX_HEADS), col(SUBLANES), row(HEAD_DIM), row(IDX_DIM), col(HEAD_DIM),
                   col(HEAD_DIM), col(HEAD_DIM), col(IDX_DIM), row(POOL_WIDTH)),
        out_shape=out_shape,
        compiler_params=_params(("parallel",)),
        name="proj_prompt",
    )(x, w_big, w_t, cs, cs.T)


def _float_of_rank(u):
    key = u ^ INT_MIN
    bits = jnp.where(key < 0, INT_MIN - key, key)
    return pltpu.bitcast(bits, F32)


def _count(mask):
    return jnp.sum(mask.astype(F32), axis=1, keepdims=True)


def _topk_bias(sc_ref, j_ref, adm, n_adm, lc, k):
    rows = sc_ref.shape[0]
    kf = float(k)

    def value_step(i, t_u):
        hi = jnp.left_shift(jnp.int32(1), 31 - 2 * i)
        lo = jnp.left_shift(jnp.int32(1), 30 - 2 * i)
        for cand_u in (t_u | lo, t_u | hi, t_u | hi | lo):
            cnt = _count(sc_ref[:, 0:lc] >= _float_of_rank(cand_u))
            t_u = jnp.where(cnt >= kf, cand_u, t_u)
        return t_u

    t_u = lax.fori_loop(0, 16, value_step, jnp.zeros((rows, 1), I32))
    few = n_adm < k
    thr = jnp.where(few, -jnp.inf, _float_of_rank(t_u))
    sc = sc_ref[:, 0:lc]
    cnt_gt = _count(sc > thr)
    cnt_eq = _count(sc == thr)
    need = kf - cnt_gt
    cut_needed = jnp.logical_and(cnt_gt + cnt_eq > kf, jnp.logical_not(few))
    any_cut = jnp.max(cut_needed.astype(F32)) > 0.0
    idx = lax.broadcasted_iota(I32, (rows, lc), 1)
    nbits = int(np.ceil(np.log2(lc)))

    j_ref[...] = jnp.full((rows, 1), lc, I32)

    @pl.when(any_cut)
    def _():
        def index_step(i, j):
            cand = j | jnp.left_shift(jnp.int32(1), nbits - 1 - i)
            c = _count(jnp.logical_and(sc_ref[:, 0:lc] == thr, idx < cand))
            return jnp.where(c < need, cand, j)

        j_ref[...] = lax.fori_loop(0, nbits, index_step, jnp.zeros((rows, 1), I32))

    sel = jnp.logical_or(sc > thr, jnp.logical_and(sc == thr, idx <= j_ref[...]))
    return jnp.where(jnp.logical_and(sel, adm), 0.0, NEG_BIG)


ATTN_CHUNK = 256


def _attn_prompt_block(n_chunks, q0, top_k, qt_ref, qit_ref, wit_ref, kb_ref, kib_ref, vbt_ref, o_ref,
                       key_ref, bias_ref, lg_ref, j_ref):
    tq, ch = Q_BLOCK, ATTN_CHUNK
    kf = float(top_k)
    kpos = lax.broadcasted_iota(I32, (ch, tq), 0)
    qpos = q0 + lax.broadcasted_iota(I32, (ch, tq), 1)

    def rows(c):
        return slice(c * ch, (c + 1) * ch)

    def fold(x, op):
        return op(x.reshape(ch // SUBLANES, SUBLANES, tq), axis=0)

    def head(x, h):
        return x[:, h * tq:(h + 1) * tq]

    qit = qit_ref[...]
    wit = wit_ref[...]
    for c in range(n_chunks if n_chunks * ch > top_k else 0):
        d = jnp.dot(kib_ref[rows(c), :], qit, preferred_element_type=F32)
        s = wit[0:1, :] * jnp.maximum(head(d, 0), 0.0)
        for h in range(1, IDX_HEADS):
            s = s + wit[h:h + 1, :] * jnp.maximum(head(d, h), 0.0)
        key_ref[rows(c), :] = jnp.where(c * ch + kpos <= qpos, s, -jnp.inf)

    def count(pred):
        acc = jnp.zeros((SUBLANES, tq), F32)
        for c in range(n_chunks):
            acc = acc + fold(pred(key_ref[rows(c), :], c).astype(F32), jnp.sum)
        return jnp.sum(acc, axis=0, keepdims=True)

    if n_chunks * ch <= top_k:
        for c in range(n_chunks):
            bias_ref[rows(c), :] = jnp.where(c * ch + kpos <= qpos, 0.0, NEG_BIG)
    else:
        def value_step(i, carry):
            t_u, n_ge = carry
            cand_u = t_u | jnp.left_shift(jnp.int32(1), 31 - i)
            cand = _float_of_rank(cand_u)
            cnt = count(lambda k, c: k >= cand)
            ok = cnt >= kf
            return jnp.where(ok, cand_u, t_u), jnp.where(ok, cnt, n_ge)

        t_u, n_ge = lax.fori_loop(0, 32, value_step,
                                  (jnp.zeros((1, tq), I32), jnp.full((1, tq), float(n_chunks * ch), F32)))
        few = qpos[0:1, :] + 1 <= top_k
        thr = jnp.where(few, -jnp.inf, _float_of_rank(t_u))
        cut_needed = jnp.logical_and(n_ge > kf, jnp.logical_not(few))
        any_cut = jnp.max(cut_needed.astype(F32)) > 0.0

        nbits = int(np.ceil(np.log2(n_chunks * ch)))
        j_ref[...] = jnp.full(j_ref.shape, n_chunks * ch, I32)

        @pl.when(any_cut)
        def _():
            need = kf - count(lambda k, c: k > thr)

            def index_step(i, j):
                cand = j | jnp.left_shift(jnp.int32(1), nbits - 1 - i)
                n_before = count(lambda k, c: jnp.logical_and(k == thr, c * ch + kpos < cand))
                return jnp.where(n_before < need, cand, j)

            j = lax.fori_loop(0, nbits, index_step, jnp.zeros((1, tq), I32))
            j_ref[...] = jnp.broadcast_to(j, j_ref.shape)

        j_cut = j_ref[0:1, :]
        for c in range(n_chunks):
            k = key_ref[rows(c), :]
            pos = c * ch + kpos
            sel = jnp.logical_or(k > thr, jnp.logical_and(k == thr, pos <= j_cut))
            bias_ref[rows(c), :] = jnp.where(jnp.logical_and(sel, pos <= qpos), 0.0, NEG_BIG)

    qt = qt_ref[...]
    mx = [jnp.full((SUBLANES, tq), -jnp.inf, F32) for _ in range(N_HEADS)]
    for c in range(n_chunks):
        lg = jnp.dot(kb_ref[rows(c), :], qt, preferred_element_type=F32)
        bias = bias_ref[rows(c), :]
        for h in range(N_HEADS):
            lgh = head(lg, h) + bias
            lg_ref[h, rows(c), :] = lgh
            mx[h] = jnp.maximum(mx[h], fold(lgh, jnp.max))

    outs = []
    for h in range(N_HEADS):
        m = jnp.max(mx[h], axis=0, keepdims=True)
        lsum = jnp.zeros((SUBLANES, tq), F32)
        ot = jnp.zeros((HEAD_DIM, tq), F32)
        for c in range(n_chunks):
            p = jnp.exp2(lg_ref[h, rows(c), :] - m)
            lsum = lsum + fold(p, jnp.sum)
            ot = ot + jnp.dot(vbt_ref[:, rows(c)], p.astype(BF16), preferred_element_type=F32)
        outs.append(ot / jnp.sum(lsum, axis=0, keepdims=True))
    o_ref[...] = jnp.concatenate(outs, axis=0).T.astype(BF16)


def _attn_prompt_kernel(qt_ref, qit_ref, wit_ref, kb_ref, kib_ref, vbt_ref, o_ref, key_ref, bias_ref, lg_ref, j_ref,
                        *, top_k):
    jq = pl.program_id(1)
    blocks_per_chunk = ATTN_CHUNK // Q_BLOCK
    n_classes = key_ref.shape[0] // ATTN_CHUNK
    for cls in range(n_classes):
        @pl.when(jq // blocks_per_chunk == cls)
        def _(cls=cls):
            _attn_prompt_block(cls + 1, jq * Q_BLOCK, top_k, qt_ref, qit_ref, wit_ref, kb_ref, kib_ref, vbt_ref,
                               o_ref, key_ref, bias_ref, lg_ref, j_ref)


def _attn_prompt(qt, qit, wit, kb, kib, vbt):
    batch, _, seq = vbt.shape
    nb = seq // Q_BLOCK
    top_k = min(TOP_K_MAX, seq // 4)
    slab = lambda heads: pl.BlockSpec((None, HEAD_DIM, heads * Q_BLOCK), lambda b, j: (b * nb + j, 0, 0))
    keys = pl.BlockSpec((seq, HEAD_DIM), lambda b, j: (b, 0))
    return pl.pallas_call(
        functools.partial(_attn_prompt_kernel, top_k=top_k),
        grid=(batch, nb),
        in_specs=[slab(N_HEADS), slab(IDX_HEADS), pl.BlockSpec((None, SUBLANES, Q_BLOCK), lambda b, j: (b, 0, j)),
                  keys, keys, pl.BlockSpec((None, HEAD_DIM, seq), lambda b, j: (b, 0, 0))],
        out_specs=pl.BlockSpec((Q_BLOCK, ATT_WIDTH), lambda b, j: (b * nb + j, 0)),
        out_shape=jax.ShapeDtypeStruct((batch * seq, ATT_WIDTH), BF16),
        scratch_shapes=[pltpu.VMEM((seq, Q_BLOCK), F32), pltpu.VMEM((seq, Q_BLOCK), F32),
                        pltpu.VMEM((N_HEADS, seq, Q_BLOCK), F32), pltpu.VMEM((SUBLANES, Q_BLOCK), I32)],
        compiler_params=_params(("parallel", "arbitrary")),
        name="attn_prompt",
    )(qt, qit, wit, kb, kib, vbt)


SAMPLE_CHUNK = 1024
SAMPLE_ROWS_PER_STEP = 2


def _attn_sample_kernel(pt_ref, q_ref, qi_ref, wi_ref, kn_ref, vn_ref, kin_ref, ck_hbm, cv_hbm, cki_hbm, o_ref,
                        kbuf, vbuf, kibuf, sem, key_scr, bias_scr, lg_scr, j_scr, *, n_pages, page, t_new, top_k):
    b = pl.program_id(0)
    n_b = pl.num_programs(0)
    slot = b % 2
    per_step = q_ref.shape[0]
    past = n_pages * page
    lc = past + page
    n_chunks = past // SAMPLE_CHUNK

    def page_copies(step, sl, p):
        dst = pl.ds(pl.multiple_of(p * page, page), page)
        copies = []
        for r in range(per_step):
            phys = pt_ref[(step * per_step + r) * n_pages + p]
            copies += [pltpu.make_async_copy(src.at[phys], buf.at[sl, r, :, dst], sem.at[i, sl])
                       for i, (src, buf) in enumerate(((ck_hbm, kbuf), (cv_hbm, vbuf), (cki_hbm, kibuf)))]
        return copies

    def start_batch(bb, sl):
        def body(p, carry):
            for cp in page_copies(bb, sl, p):
                cp.start()
            return carry
        lax.fori_loop(0, n_pages, body, 0)

    def wait_batch(bb, sl):
        def body(p, carry):
            for cp in page_copies(bb, sl, p):
                cp.wait()
            return carry
        lax.fori_loop(0, n_pages, body, 0)

    @pl.when(b == 0)
    def _():
        start_batch(0, 0)

    @pl.when(b + 1 < n_b)
    def _():
        start_batch(b + 1, 1 - slot)

    wait_batch(b, slot)

    def head_sum(r, d):
        x = wi_ref[r] * jnp.maximum(d, 0.0)
        s = x[0:t_new]
        for h in range(1, IDX_HEADS):
            s = s + x[h * t_new:(h + 1) * t_new]
        return s

    def new_rows(ref, r):
        pad = jnp.zeros((page - t_new, ref.shape[2]), F32)
        return jnp.concatenate([ref[r], pad], axis=0).astype(BF16)

    adm_new = lax.broadcasted_iota(I32, (t_new, page), 1) <= lax.broadcasted_iota(I32, (t_new, page), 0)
    for r in range(per_step):
        qrows = slice(r * t_new, (r + 1) * t_new)
        qi = qi_ref[r]
        for c in range(n_chunks):
            sl = slice(c * SAMPLE_CHUNK, (c + 1) * SAMPLE_CHUNK)
            d = jnp.dot(qi, kibuf[slot, r, :, sl].astype(BF16), preferred_element_type=F32)
            key_scr[qrows, sl] = head_sum(r, d)
        d_new = lax.dot_general(qi, new_rows(kin_ref, r), NT_DIMS, preferred_element_type=F32)
        key_scr[qrows, past:lc] = jnp.where(adm_new, head_sum(r, d_new), -jnp.inf)

    n_q = per_step * t_new
    idx = lax.broadcasted_iota(I32, (n_q, lc), 1)
    trow = lax.broadcasted_iota(I32, (n_q, lc), 0) % t_new
    n_adm = past + 1 + lax.broadcasted_iota(I32, (n_q, 1), 0) % t_new
    bias_scr[...] = _topk_bias(key_scr, j_scr, idx - past <= trow, n_adm, lc, top_k)

    for r in range(per_step):
        q = q_ref[r]

        def bias_rows(sl, r=r):
            return jnp.concatenate([bias_scr[r * t_new:(r + 1) * t_new, sl]] * N_HEADS, axis=0)

        m = jnp.full((N_HEADS * t_new, 1), -jnp.inf, F32)
        for c in range(n_chunks):
            sl = slice(c * SAMPLE_CHUNK, (c + 1) * SAMPLE_CHUNK)
            lg = jnp.dot(q, kbuf[slot, r, :, sl].astype(BF16), preferred_element_type=F32) + bias_rows(sl)
            lg_scr[:, sl] = lg
            m = jnp.maximum(m, jnp.max(lg, axis=1, keepdims=True))
        lg_new = (lax.dot_general(q, new_rows(kn_ref, r), NT_DIMS, preferred_element_type=F32)
                  + bias_rows(slice(past, lc)))
        m = jnp.maximum(m, jnp.max(lg_new, axis=1, keepdims=True))

        p_new = jnp.exp2(lg_new - m)
        l = jnp.sum(p_new, axis=1, keepdims=True)
        o = jnp.dot(p_new.astype(BF16), new_rows(vn_ref, r), preferred_element_type=F32)
        for c in range(n_chunks):
            sl = slice(c * SAMPLE_CHUNK, (c + 1) * SAMPLE_CHUNK)
            pr = jnp.exp2(lg_scr[:, sl] - m)
            l = l + jnp.sum(pr, axis=1, keepdims=True)
            o = o + lax.dot_general(pr.astype(BF16), vbuf[slot, r, :, sl].astype(BF16), NT_DIMS,
                                    preferred_element_type=F32)
        o_ref[r] = o / l


def _attn_sample(page_table, q_hq, qi_hq, wi_hq, k_new, v_new, ki_new, cache_kt, cache_vt, cache_kit):
    db, n_pages = page_table.shape
    page = cache_kt.shape[2]
    t_new = k_new.shape[1]
    past = n_pages * page
    lc = past + page
    top_k = min(TOP_K_MAX, (past + t_new) // 4)
    rows = SAMPLE_ROWS_PER_STEP
    per_b = lambda r, w: pl.BlockSpec((rows, r, w), lambda b, pt: (b, 0, 0))
    hbm = pl.BlockSpec(memory_space=pl.ANY)
    kern = functools.partial(_attn_sample_kernel, n_pages=n_pages, page=page, t_new=t_new, top_k=top_k)
    slab = pltpu.VMEM((2, rows, HEAD_DIM, past), F32)
    grid_spec = pltpu.PrefetchScalarGridSpec(
        num_scalar_prefetch=1,
        grid=(db // rows,),
        in_specs=[per_b(N_HEADS * t_new, HEAD_DIM), per_b(IDX_HEADS * t_new, IDX_DIM), per_b(IDX_HEADS * t_new, 1),
                  per_b(t_new, HEAD_DIM), per_b(t_new, HEAD_DIM), per_b(t_new, IDX_DIM),
                  hbm, hbm, hbm],
        out_specs=per_b(N_HEADS * t_new, HEAD_DIM),
        scratch_shapes=[slab, slab, slab, pltpu.SemaphoreType.DMA((3, 2)),
                        pltpu.VMEM((rows * t_new, lc), F32), pltpu.VMEM((rows * t_new, lc), F32),
                        pltpu.VMEM((N_HEADS * t_new, past), F32), pltpu.VMEM((rows * t_new, 1), I32)],
    )
    return pl.pallas_call(
        kern,
        grid_spec=grid_spec,
        out_shape=jax.ShapeDtypeStruct((db, N_HEADS * t_new, HEAD_DIM), F32),
        compiler_params=_params(("arbitrary",)),
        name="attn_sample",
    )(page_table.reshape(-1), q_hq, qi_hq, wi_hq, k_new, v_new, ki_new, cache_kt, cache_vt, cache_kit)


PREV_ROWS = 16


def _pool_kernel(prev_ref, u_ref, wg_ref, sc_ref, o_ref, ext_ref, *, pos0):
    per_step, t_len, _ = u_ref.shape
    pos = pos0 + lax.broadcasted_iota(I32, (t_len, 1), 0)
    for b in range(per_step):
        ext_ref[0:PREV_ROWS, :] = prev_ref[b]
        ext_ref[PREV_ROWS:PREV_ROWS + t_len, :] = u_ref[b]
        for g, w in enumerate(POOL_WINDOWS):
            sl = slice(g * POOL_GW, (g + 1) * POOL_GW)
            u_new = ext_ref[PREV_ROWS:PREV_ROWS + t_len, sl]
            win = u_new
            for back in range(1, w):
                win = win + ext_ref[PREV_ROWS - back:PREV_ROWS - back + t_len, sl]
            count = jnp.minimum(pos + 1, w).astype(F32)
            r = win / count - u_new
            mixed = jnp.dot(r.astype(BF16), wg_ref[g], preferred_element_type=F32) * sc_ref[:, sl]
            o_ref[b, :, sl] = mixed.astype(BF16)


def _pool(prev, u, w_grp, scale, pos0, per_step):
    nb, t_len, _ = u.shape
    seqs = lambda rows: pl.BlockSpec((per_step, rows, POOL_WIDTH), lambda b: (b, 0, 0))
    return pl.pallas_call(
        functools.partial(_pool_kernel, pos0=pos0),
        grid=(nb // per_step,),
        in_specs=[seqs(PREV_ROWS), seqs(t_len),
                  pl.BlockSpec((POOL_GROUPS, POOL_GW, POOL_GW), lambda b: (0, 0, 0)),
                  pl.BlockSpec((1, POOL_WIDTH), lambda b: (0, 0))],
        out_specs=seqs(t_len),
        out_shape=jax.ShapeDtypeStruct((nb, t_len, POOL_WIDTH), BF16),
        scratch_shapes=[pltpu.VMEM((PREV_ROWS + t_len, POOL_WIDTH), F32)],
        compiler_params=_params(("parallel",)),
        name="pool",
    )(prev, u, w_grp, scale)


def _merge_kernel(x_ref, a_ref, p_ref, wga_ref, wgb_ref, wao_ref, wpo_ref, wo_ref, g_ref, b_ref, h_ref, hp_ref, *,
                  alpha):
    x = x_ref[...]
    xb = x.astype(BF16)
    ga = jnp.dot(xb, wga_ref[...], preferred_element_type=F32)
    gb = jnp.dot(xb, wgb_ref[...], preferred_element_type=F32)
    ya = jnp.dot(a_ref[...], wao_ref[...], preferred_element_type=F32)
    yp = jnp.dot(p_ref[...], wpo_ref[...], preferred_element_type=F32)
    mix = jax.nn.sigmoid(ga) * ya + jax.nn.sigmoid(gb) * yp
    out = jnp.dot(mix.astype(BF16), wo_ref[...], preferred_element_type=F32)
    h = _layer_norm(alpha * x + out, g_ref[...], b_ref[...])
    h_ref[...] = h
    hp_ref[...] = _pack_rows(h)


def _merge(x, attn, pool, wga, wgb, wao, wpo, wo, g, b, tm, alpha):
    n = x.shape[0]
    row = lambda w: pl.BlockSpec((tm, w), lambda i: (i, 0))
    full = lambda r, c: pl.BlockSpec((r, c), lambda i: (0, 0), pipeline_mode=pl.Buffered(1))
    return pl.pallas_call(
        functools.partial(_merge_kernel, alpha=alpha),
        grid=(n // tm,),
        in_specs=[row(D_MODEL), row(ATT_WIDTH), row(POOL_WIDTH), full(D_MODEL, D_MODEL), full(D_MODEL, D_MODEL),
                  full(ATT_WIDTH, D_MODEL), full(POOL_WIDTH, D_MODEL), full(D_MODEL, D_MODEL),
                  full(1, D_MODEL), full(1, D_MODEL)],
        out_specs=(row(D_MODEL), row(PACKED)),
        out_shape=(jax.ShapeDtypeStruct((n, D_MODEL), F32), jax.ShapeDtypeStruct((n, PACKED), I32)),
        compiler_params=_params(("parallel",)),
        name="merge",
    )(x, attn, pool, wga, wgb, wao, wpo, wo, g, b)


def _route(h, wr_t, bias_col):
    tm = h.shape[0]
    logits = lax.dot_general(wr_t, h.astype(BF16), NT_DIMS, preferred_element_type=F32)
    s = jax.nn.sigmoid(logits)
    sb = s + bias_col
    neg_inf = -jnp.inf

    rows = []
    for g in range(N_GROUPS):
        blk = sb[g * GROUP_SIZE:(g + 1) * GROUP_SIZE, :]
        m1 = jnp.max(blk, axis=0, keepdims=True)
        is_m1 = blk == m1
        n_m1 = jnp.sum(is_m1.astype(F32), axis=0, keepdims=True)
        m2 = jnp.max(jnp.where(is_m1, neg_inf, blk), axis=0, keepdims=True)
        rows.append(m1 + jnp.where(n_m1 >= 2.0, m1, m2))
    gs = jnp.concatenate(rows, axis=0)

    gi = lax.broadcasted_iota(I32, (N_GROUPS, tm), 0)
    rank = jnp.zeros((N_GROUPS, tm), F32)
    for g in range(N_GROUPS):
        row = gs[g:g + 1, :]
        beats = jnp.logical_or(row > gs, jnp.logical_and(row == gs, g < gi))
        rank = rank + beats.astype(F32)
    gkeep = rank < float(TOPK_GROUPS)
    emask = jnp.concatenate(
        [jnp.broadcast_to(gkeep[g:g + 1, :], (GROUP_SIZE, tm)) for g in range(N_GROUPS)], axis=0)

    ei = lax.broadcasted_iota(I32, (N_EXPERTS, tm), 0)
    x = jnp.where(emask, sb, neg_inf)
    sel = jnp.zeros((N_EXPERTS, tm), jnp.bool_)
    picks = []
    for _ in range(TOP_K_EXPERTS):
        m = jnp.max(x, axis=0, keepdims=True)
        first = jnp.min(jnp.where(x == m, ei, N_EXPERTS), axis=0, keepdims=True)
        pick = ei == first
        sel = jnp.logical_or(sel, pick)
        x = jnp.where(pick, neg_inf, x)
        picks.append(first)

    gate = jnp.where(sel, s, 0.0)
    comb = gate / jnp.sum(gate, axis=0, keepdims=True) * ROUTED_SCALE
    return comb, sel, picks


def _router_kernel(h_ref, wr_ref, bias_ref, c_ref):
    comb, _, _ = _route(h_ref[...], wr_ref[...], bias_ref[...])
    comb = jnp.concatenate([comb, jnp.zeros((LANES - N_EXPERTS, comb.shape[1]), F32)], axis=0)
    c_ref[...] = comb.T


def _router(h, wr_t, bias_col, tm):
    n = h.shape[0]
    return pl.pallas_call(
        _router_kernel,
        grid=(n // tm,),
        in_specs=[pl.BlockSpec((tm, D_MODEL), lambda i: (i, 0)),
                  pl.BlockSpec((N_EXPERTS, D_MODEL), lambda i: (0, 0)),
                  pl.BlockSpec((N_EXPERTS, 1), lambda i: (0, 0))],
        out_specs=pl.BlockSpec((tm, LANES), lambda i: (i, 0)),
        out_shape=jax.ShapeDtypeStruct((n, LANES), F32),
        compiler_params=_params(("parallel",)),
        name="router",
    )(h, wr_t, bias_col)


def _swiglu(xb, w13, w2, hidden):
    ab = jnp.dot(xb, w13, preferred_element_type=F32)
    act = jax.nn.silu(ab[:, 0:hidden]) * ab[:, hidden:2 * hidden]
    return jnp.dot(act.astype(BF16), w2, preferred_element_type=F32)


def _moe_kernel(h_ref, c_ref, ws13_ref, ws2_ref, w13_ref, w2_ref, y_ref, hb_ref):
    e = pl.program_id(1)

    @pl.when(e == 0)
    def _():
        hb_ref[...] = h_ref[...].astype(BF16)
        y_ref[...] = _swiglu(hb_ref[...], ws13_ref[...], ws2_ref[...], SHARED_DIM)

    ye = _swiglu(hb_ref[...], w13_ref[...].astype(BF16), w2_ref[...].astype(BF16), EXPERT_DIM)
    lane = lax.broadcasted_iota(I32, c_ref.shape, 1)
    ce = jnp.sum(jnp.where(lane == e, c_ref[...], 0.0), axis=1, keepdims=True)
    y_ref[...] += ce * ye


def _moe(h, comb, ws13, ws2, w13, w2, tm):
    n = h.shape[0]
    return pl.pallas_call(
        _moe_kernel,
        grid=(n // tm, N_EXPERTS),
        in_specs=[pl.BlockSpec((tm, D_MODEL), lambda i, e: (i, 0)),
                  pl.BlockSpec((tm, LANES), lambda i, e: (i, 0)),
                  pl.BlockSpec((D_MODEL, 2 * SHARED_DIM), lambda i, e: (0, 0)),
                  pl.BlockSpec((SHARED_DIM, D_MODEL), lambda i, e: (0, 0)),
                  pl.BlockSpec((None, D_MODEL, 2 * EXPERT_DIM), lambda i, e: (e, 0, 0)),
                  pl.BlockSpec((None, EXPERT_DIM, D_MODEL), lambda i, e: (e, 0, 0))],
        out_specs=pl.BlockSpec((tm, D_MODEL), lambda i, e: (i, 0)),
        out_shape=jax.ShapeDtypeStruct((n, D_MODEL), F32),
        scratch_shapes=[pltpu.VMEM((tm, D_MODEL), BF16)],
        compiler_params=_params(("parallel", "arbitrary")),
        name="moe",
    )(h, comb, ws13, ws2, w13, w2)


def _final_kernel(h_ref, y_ref, pe_ref, g_ref, b_ref, wpg_ref, wpi_ref, o_ref, *, alpha):
    z = _layer_norm(alpha * h_ref[...] + y_ref[...], g_ref[...], b_ref[...])
    gate = jax.nn.sigmoid(jnp.dot(z.astype(BF16), wpg_ref[...], preferred_element_type=F32))
    emb = jnp.dot(pe_ref[...].astype(BF16), wpi_ref[...], preferred_element_type=F32)
    o_ref[...] = z + gate * emb


def _final(h, y, pe, g, b, wpg, wpi, tm, alpha):
    n = h.shape[0]
    row = lambda w: pl.BlockSpec((tm, w), lambda i: (i, 0))
    full = lambda r, c: pl.BlockSpec((r, c), lambda i: (0, 0))
    return pl.pallas_call(
        functools.partial(_final_kernel, alpha=alpha),
        grid=(n // tm,),
        in_specs=[row(D_MODEL), row(D_MODEL), row(PLE_DIM), full(1, D_MODEL), full(1, D_MODEL),
                  full(D_MODEL, D_MODEL), full(PLE_DIM, D_MODEL)],
        out_specs=row(D_MODEL),
        out_shape=jax.ShapeDtypeStruct((n, D_MODEL), F32),
        compiler_params=_params(("parallel",)),
        name="final",
    )(h, y, pe, g, b, wpg, wpi)


MOE_BLOCK = 2176


def _sorted_rows(n_tokens):
    worst = n_tokens * TOP_K_EXPERTS + N_EXPERTS * MOE_BLOCK
    return -(-worst // MOE_BLOCK) * MOE_BLOCK


def _dispatch_kernel(h_ref, wr_ref, bias_ref, tri_ref, pos_ref, gate_ref, blk_ref, used_ref,
                     eidx_s, rank_s, gate_s, cnt_s):
    p = pl.program_id(0)
    i = pl.program_id(1)
    tm = h_ref.shape[0]
    ei = lax.broadcasted_iota(I32, (N_EXPERTS, tm), 0)

    @pl.when(p == 0)
    def _():
        comb, sel, picks = _route(h_ref[...], wr_ref[...], bias_ref[...])
        before = jnp.dot(sel.astype(BF16), tri_ref[...], preferred_element_type=F32)
        ranks, gates = [], []
        for first in picks:
            pick = ei == first
            ranks.append(jnp.sum(jnp.where(pick, before, 0.0), axis=0, keepdims=True))
            gates.append(jnp.sum(jnp.where(pick, comb, 0.0), axis=0, keepdims=True))
        eidx_s[i] = jnp.concatenate(picks, axis=0)
        rank_s[i] = jnp.concatenate(ranks, axis=0)
        gate_s[i] = jnp.concatenate(gates, axis=0)
        cnt_s[i] = jnp.broadcast_to(jnp.sum(sel.astype(F32), axis=1, keepdims=True), (N_EXPERTS, LANES))

    @pl.when(p == 1)
    def _():
        cnt = cnt_s[...]
        tile_id = lax.broadcasted_iota(I32, cnt.shape, 0)
        total = jnp.sum(cnt, axis=0)
        prior = jnp.sum(jnp.where(tile_id < i, cnt, 0.0), axis=0)
        seg_blk = jnp.maximum(jnp.ceil(total * (1.0 / MOE_BLOCK) - 0.25 / MOE_BLOCK), 1.0)
        lower = (lax.broadcasted_iota(I32, (N_EXPERTS, N_EXPERTS), 1)
                 < lax.broadcasted_iota(I32, (N_EXPERTS, N_EXPERTS), 0)).astype(F32)
        off_blk = jnp.dot(lower, seg_blk, precision=lax.Precision.HIGHEST, preferred_element_type=F32)
        seg_off = off_blk * MOE_BLOCK
        base = (seg_off + prior)[:, 0:1]
        eidx = eidx_s[i]
        rank = rank_s[i]
        rows = []
        for k in range(TOP_K_EXPERTS):
            pick = ei == eidx[k:k + 1, :]
            rows.append(rank[k:k + 1, :] + jnp.sum(jnp.where(pick, base, 0.0), axis=0, keepdims=True))
        pos_ref[...] = jnp.concatenate(rows, axis=0).astype(I32)
        gate_ref[...] = jnp.concatenate([gate_s[i], jnp.zeros((LANES - TOP_K_EXPERTS, tm), F32)], axis=0).T

        end_blk = (off_blk + seg_blk)[:, 0:1]
        n_blk = blk_ref.shape[1]
        blk_id = lax.broadcasted_iota(I32, (N_EXPERTS, n_blk), 1).astype(F32)
        owner = jnp.sum((end_blk <= blk_id).astype(F32), axis=0, keepdims=True)
        blk_ref[...] = jnp.minimum(owner, N_EXPERTS - 1.0).astype(I32)
        used_ref[...] = jnp.broadcast_to(end_blk[N_EXPERTS - 1:N_EXPERTS, :], used_ref.shape).astype(I32)


def _dispatch(h, wr_t, bias_col, tm):
    n = h.shape[0]
    n_tiles = n // tm
    n_blk = _sorted_rows(n) // MOE_BLOCK
    n_blk_pad = -(-n_blk // LANES) * LANES
    tri = jnp.triu(jnp.ones((tm, tm), BF16), k=1)
    const = lambda r, c: pl.BlockSpec((r, c), lambda p, i: (0, 0))
    per_tile = lambda dt: pltpu.VMEM((n_tiles, TOP_K_EXPERTS, tm), dt)
    return pl.pallas_call(
        _dispatch_kernel,
        grid=(2, n_tiles),
        in_specs=[pl.BlockSpec((tm, D_MODEL), lambda p, i: (i * (1 - p), 0)),
                  const(N_EXPERTS, D_MODEL), const(N_EXPERTS, 1), const(tm, tm)],
        out_specs=(pl.BlockSpec((TOP_K_EXPERTS, tm), lambda p, i: (0, i * p)),
                   pl.BlockSpec((tm, LANES), lambda p, i: (i * p, 0)),
                   const(1, n_blk_pad), const(1, LANES)),
        out_shape=(jax.ShapeDtypeStruct((TOP_K_EXPERTS, n), I32), jax.ShapeDtypeStruct((n, LANES), F32),
                   jax.ShapeDtypeStruct((1, n_blk_pad), I32), jax.ShapeDtypeStruct((1, LANES), I32)),
        scratch_shapes=[per_tile(I32), per_tile(F32), per_tile(F32), pltpu.VMEM((n_tiles, N_EXPERTS, LANES), F32)],
        compiler_params=_params(("arbitrary", "arbitrary")),
        name="dispatch",
    )(h, wr_t, bias_col, tri)


PACKED = D_MODEL // 2


def _pack_rows(x):
    lo = pltpu.bitcast(x[:, 0:PACKED].astype(BF16).astype(F32), I32)
    hi = pltpu.bitcast(x[:, PACKED:D_MODEL].astype(BF16).astype(F32), I32)
    return jnp.bitwise_or(hi, lax.shift_right_logical(lo, 16))


def _unpack_rows_f32(w):
    lo = pltpu.bitcast(lax.shift_left(w, 16), F32)
    hi = pltpu.bitcast(jnp.bitwise_and(w, -65536), F32)
    return jnp.concatenate([lo, hi], axis=1)


def _unpack_rows(w):
    return _unpack_rows_f32(w).astype(BF16)


GROUPED_RING = 3


def _grouped_kernel(blk_ref, used_ref, anchor_ref, xs_hbm, w13_ref, w2_ref, ys_ref, w13b_ref, w2b_ref, xbuf, sem):
    s = pl.program_id(0)
    used = used_ref[0]

    def row_copy(b):
        slot = b % GROUPED_RING
        return pltpu.make_async_copy(xs_hbm.at[pl.ds(pl.multiple_of(b * MOE_BLOCK, MOE_BLOCK), MOE_BLOCK)],
                                     xbuf.at[slot], sem.at[slot])

    @pl.when(s == 0)
    def _():
        for b in range(GROUPED_RING - 1):
            @pl.when(b < used)
            def _(b=b):
                row_copy(b).start()

    @pl.when(s + GROUPED_RING - 1 < used)
    def _():
        row_copy(s + GROUPED_RING - 1).start(priority=1)

    @pl.when(s < used)
    def _():
        row_copy(s).wait()
        w13b_ref[...] = w13_ref[...].astype(BF16)
        w2b_ref[...] = w2_ref[...].astype(BF16)
        ys = _swiglu(_unpack_rows(xbuf[s % GROUPED_RING]), w13b_ref[...], w2b_ref[...], EXPERT_DIM)
        ys_ref[...] = _pack_rows(ys)


def _grouped(blk, used, anchor, xs, w13, w2):
    ns = xs.shape[0]
    row_blk = lambda b, blk, used, anchor: (jnp.minimum(b, used[0] - 1), 0)
    expert = lambda b, blk, used, anchor: (blk[b], 0, 0)
    grid_spec = pltpu.PrefetchScalarGridSpec(
        num_scalar_prefetch=3,
        grid=(ns // MOE_BLOCK,),
        in_specs=[pl.BlockSpec(memory_space=pl.ANY),
                  pl.BlockSpec((None, D_MODEL, 2 * EXPERT_DIM), expert),
                  pl.BlockSpec((None, EXPERT_DIM, D_MODEL), expert)],
        out_specs=(pl.BlockSpec((MOE_BLOCK, PACKED), row_blk),
                   pl.BlockSpec((None, D_MODEL, 2 * EXPERT_DIM), expert),
                   pl.BlockSpec((None, EXPERT_DIM, D_MODEL), expert)),
        scratch_shapes=[pltpu.VMEM((GROUPED_RING, MOE_BLOCK, PACKED), I32), pltpu.SemaphoreType.DMA((GROUPED_RING,))],
    )
    return pl.pallas_call(
        _grouped_kernel,
        grid_spec=grid_spec,
        out_shape=(jax.ShapeDtypeStruct((ns, PACKED), I32), jax.ShapeDtypeStruct(w13.shape, BF16),
                   jax.ShapeDtypeStruct(w2.shape, BF16)),
        compiler_params=_params(("arbitrary",)),
        name="grouped",
    )(blk, used, anchor, xs, w13, w2)


SC_WINDOW = 128


def _sc_mesh():
    return plsc.VectorSubcoreMesh(core_axis_name="core", subcore_axis_name="subcore")


def _sc_worker(n_items):
    info = plsc.get_sparse_core_info()
    n_workers = info.num_cores * info.num_subcores
    assert n_items % (SC_WINDOW * n_workers) == 0, "rows must split evenly into windows over the vector subcores"
    wid = lax.axis_index("subcore") * info.num_cores + lax.axis_index("core")
    return wid, n_items // (SC_WINDOW * n_workers)


def _scatter_rows(x, pos, n_out):
    n, width = x.shape
    picks = pos.shape[0]

    @functools.partial(
        pl.kernel, mesh=_sc_mesh(), out_type=jax.ShapeDtypeStruct((n_out, width), I32),
        scratch_types=[pltpu.VMEM((picks, SC_WINDOW), I32), pltpu.VMEM((SC_WINDOW, width), I32)],
        name="scatter_rows")
    def scatter(x_hbm, pos_hbm, out_hbm, idx_v, rows_v):
        wid, n_win = _sc_worker(n)

        @pl.loop(0, n_win)
        def _(j):
            base = (wid * n_win + j) * SC_WINDOW
            pltpu.sync_copy(pos_hbm.at[:, pl.ds(base, SC_WINDOW)], idx_v)
            pltpu.sync_copy(x_hbm.at[pl.ds(base, SC_WINDOW)], rows_v)
            for k in range(picks):
                pltpu.sync_copy(rows_v, out_hbm.at[idx_v.at[k]])

    return scatter(x, pos)


def _gather_rows(src, pos):
    width = src.shape[1]
    picks, n = pos.shape

    @functools.partial(
        pl.kernel, mesh=_sc_mesh(), out_type=jax.ShapeDtypeStruct((picks * n, width), I32),
        scratch_types=[pltpu.VMEM((SC_WINDOW,), I32), pltpu.VMEM((SC_WINDOW, width), I32)],
        name="gather_rows")
    def gather(src_hbm, pos_hbm, out_hbm, idx_v, rows_v):
        wid, n_win = _sc_worker(picks * n)

        @pl.loop(0, n_win)
        def _(j):
            base = (wid * n_win + j) * SC_WINDOW
            pltpu.sync_copy(pos_hbm.at[pl.ds(base, SC_WINDOW)], idx_v)
            pltpu.sync_copy(src_hbm.at[idx_v], rows_v)
            pltpu.sync_copy(rows_v, out_hbm.at[pl.ds(base, SC_WINDOW)])

    return gather(src, pos.reshape(-1)).reshape(picks, n, width)


def _combine_kernel(h_ref, g_ref, gate_ref, pe_ref, ws13_ref, ws2_ref, ln_g_ref, ln_b_ref, wpg_ref, wpi_ref, o_ref, *,
                    alpha):
    h = h_ref[...]
    y = _swiglu(h.astype(BF16), ws13_ref[...], ws2_ref[...], SHARED_DIM)
    gate = gate_ref[...]
    for k in range(TOP_K_EXPERTS):
        y = y + gate[:, k:k + 1] * _unpack_rows_f32(g_ref[k])
    z = _layer_norm(alpha * h + y, ln_g_ref[...], ln_b_ref[...])
    ple_gate = jax.nn.sigmoid(jnp.dot(z.astype(BF16), wpg_ref[...], preferred_element_type=F32))
    emb = jnp.dot(pe_ref[...].astype(BF16), wpi_ref[...], preferred_element_type=F32)
    o_ref[...] = z + ple_gate * emb


def _combine(h, gathered, gate, pe, ws13, ws2, g, b, wpg, wpi, tm, alpha):
    n = h.shape[0]
    row = lambda w: pl.BlockSpec((tm, w), lambda i: (i, 0))
    full = lambda r, c: pl.BlockSpec((r, c), lambda i: (0, 0))
    return pl.pallas_call(
        functools.partial(_combine_kernel, alpha=alpha),
        grid=(n // tm,),
        in_specs=[row(D_MODEL), pl.BlockSpec((TOP_K_EXPERTS, tm, PACKED), lambda i: (0, i, 0)), row(LANES),
                  row(PLE_DIM), full(D_MODEL, 2 * SHARED_DIM), full(SHARED_DIM, D_MODEL),
                  full(1, D_MODEL), full(1, D_MODEL), full(D_MODEL, D_MODEL), full(PLE_DIM, D_MODEL)],
        out_specs=row(D_MODEL),
        out_shape=jax.ShapeDtypeStruct((n, D_MODEL), F32),
        compiler_params=_params(("parallel",)),
        name="combine",
    )(h, gathered, gate, pe, ws13, ws2, g, b, wpg, wpi)


def _rope_table(pos):
    inv = ROPE_THETA ** (-jnp.arange(0, HEAD_DIM, 2, dtype=F32) / HEAD_DIM)
    ang = pos.astype(F32)[:, None] * inv[None, :]
    return jnp.concatenate([jnp.tile(jnp.cos(ang), (1, 4)), jnp.tile(jnp.sin(ang), (1, 4))], axis=1)


def _fused_in_weight(w_in):
    offs = np.cumsum(IN_SIZES)[:-1].tolist()
    wq, wk, wv, wqi, wki, wwi, wu, wga, wgb = jnp.split(w_in, offs, axis=1)
    pad = jnp.zeros((D_MODEL, LANES - HEAD_DIM - IDX_HEADS), w_in.dtype)
    w_big = jnp.concatenate([wq, wqi, wk, wki, wv, wwi, pad, wu], axis=1).astype(BF16)
    return w_big, w_big[:, 0:C_U].T, wga.astype(BF16), wgb.astype(BF16)


def _pages_transposed(cache):
    return jnp.transpose(cache[0], (0, 2, 1))


def _heads_major(a, n_heads):
    b, t, w = a.shape
    d = w // n_heads
    return a.reshape(b, t, n_heads, d).transpose(0, 2, 1, 3).reshape(b, n_heads * t, d)


def kernel(x_prompt, x_sample, cache_k, cache_v, cache_kidx, state_pool, page_table, p_prompt, p_sample, w_in, w_att_out, w_pool_grp, pool_scale, w_pool_out, w_out, ln1_g, ln1_b, w_router, router_bias, w_exp13, w_exp2, w_sh13, w_sh2, ln2_g, ln2_b, w_ple_in, w_ple_gate):
    B, S, D = x_prompt.shape
    DB, T, _ = x_sample.shape
    depth = w_in.shape[0]
    assert depth == 1, "single layer step"
    page = cache_k.shape[2]
    past = page_table.shape[1] * page
    alpha = (2 * depth) ** 0.25
    n_p, n_s = B * S, DB * T

    w_big, w_t, wga, wgb = _fused_in_weight(w_in[0])
    wao, wpo, wo = w_att_out[0].astype(BF16), w_pool_out[0].astype(BF16), w_out[0].astype(BF16)
    wgrp = w_pool_grp[0].astype(BF16)
    pscale = pool_scale[0].reshape(1, POOL_WIDTH)
    g1, b1 = ln1_g[0].reshape(1, D), ln1_b[0].reshape(1, D)
    g2, b2 = ln2_g[0].reshape(1, D), ln2_b[0].reshape(1, D)
    wr_t = w_router[0].T.astype(BF16)
    rbias = router_bias[0].reshape(N_EXPERTS, 1)
    w13, w2 = w_exp13[0], w_exp2[0]
    ws13, ws2 = w_sh13[0].astype(BF16), w_sh2[0].astype(BF16)
    wpg, wpi = w_ple_gate[0].astype(BF16), w_ple_in[0].astype(BF16)

    cs_p = _rope_table(jnp.arange(S, dtype=I32))
    cs_s = jnp.tile(_rope_table(past + jnp.arange(T, dtype=I32)), (DB, 1))

    xp = x_prompt.reshape(n_p, D)
    qt, qit, wit, kb, kib, vbt, kt, vt, kit, u = _proj_prompt(xp, w_big, w_t, cs_p, S, PROJ_TILE)
    attn_p = _attn_prompt(qt, qit, wit, kb, kib, vbt)
    u3 = u.reshape(B, S, POOL_WIDTH)
    pool_p = _pool(jnp.zeros((B, PREV_ROWS, POOL_WIDTH), F32), u3, wgrp, pscale, 0, 1).reshape(n_p, POOL_WIDTH)
    h_p, hp_p = _merge(xp, attn_p, pool_p, wga, wgb, wao, wpo, wo, g1, b1, MERGE_TILE, alpha)

    xs = x_sample.reshape(n_s, D)
    qs, qis, ks, vs, kis, wis, us = _proj_sample(xs, w_big, cs_s)
    q_hq = _heads_major(qs.reshape(DB, T, ATT_WIDTH), N_HEADS)
    qi_hq = _heads_major(qis.reshape(DB, T, IDX_HEADS * IDX_DIM), IDX_HEADS)
    wi_hq = wis.reshape(DB, T, IDX_HEADS).transpose(0, 2, 1).reshape(DB, IDX_HEADS * T, 1)
    caches = (_pages_transposed(cache_k), _pages_transposed(cache_v), _pages_transposed(cache_kidx))
    new_rows = (ks.reshape(DB, T, HEAD_DIM), vs.reshape(DB, T, HEAD_DIM), kis.reshape(DB, T, IDX_DIM))
    half = DB * 5 // 8 // SAMPLE_ROWS_PER_STEP * SAMPLE_ROWS_PER_STEP
    o_halves = [_attn_sample(page_table[sl], q_hq[sl], qi_hq[sl], wi_hq[sl], *(a[sl] for a in new_rows), *caches)
                for sl in (slice(0, half), slice(half, DB))]
    o_hq = jnp.concatenate(o_halves, axis=0)
    attn_s = o_hq.reshape(DB, N_HEADS, T, HEAD_DIM).transpose(0, 2, 1, 3).reshape(n_s, ATT_WIDTH).astype(BF16)
    us3 = us.reshape(DB, T, POOL_WIDTH)
    prev_s = jnp.concatenate([jnp.zeros((DB, PREV_ROWS - POOL_STATE, POOL_WIDTH), F32), state_pool[0]], axis=1)
    pool_s = _pool(prev_s, us3, wgrp, pscale, past, DB).reshape(n_s, POOL_WIDTH)
    h_s, _ = _merge(xs, attn_s, pool_s, wga, wgb, wao, wpo, wo, g1, b1, n_s, alpha)

    pos, gate, blk, used = _dispatch(h_p, wr_t, rbias, DISPATCH_TILE)
    sorted_in = _scatter_rows(hp_p, pos, _sorted_rows(n_p))
    anchor = lax.bitcast_convert_type(o_halves[0][0, 0, 0:1], I32)
    sorted_out, w13_b, w2_b = _grouped(blk.reshape(-1), used.reshape(-1), anchor, sorted_in, w13, w2)
    gathered = _gather_rows(sorted_out, pos)

    comb_s = _router(h_s, wr_t, rbias, n_s)
    y_s = _final(h_s, _moe(h_s, comb_s, ws13, ws2, w13_b, w2_b, n_s), p_sample[0].reshape(n_s, PLE_DIM), g2, b2, wpg,
                 wpi, n_s, alpha)
    y_p = _combine(h_p, gathered, gate, p_prompt[0].reshape(n_p, PLE_DIM), ws13, ws2, g2, b2, wpg, wpi, COMBINE_TILE,
                   alpha)

    ext_s = jnp.concatenate([state_pool[0], us3], axis=1)
    return (y_p.reshape(B, S, D), y_s.reshape(DB, T, D),
            jnp.transpose(kt, (0, 2, 1))[None], jnp.transpose(vt, (0, 2, 1))[None],
            jnp.transpose(kit, (0, 2, 1))[None],
            u3[:, S - POOL_STATE:][None],
            ks.reshape(1, DB, T, HEAD_DIM), vs.reshape(1, DB, T, HEAD_DIM), kis.reshape(1, DB, T, IDX_DIM),
            ext_s[:, T:][None])
```
